```python
import jax, jax.numpy as jnp
from jax import lax
import numpy as np

D_MODEL = 1024
BATCH = 16
SEQ = 256
DEPTH = 1
DEC_BATCH = 4
DEC_SEQ = 1024
PAST_LEN = 256

GRID_W = 64
RW_HD = 64
RW_W = D_MODEL // 2
RW_HEADS = RW_W // RW_HD
RW_DECAY_LORA = 64
RW_AAA_LORA = 64
RW_GATE_LORA = 128
RW_DECAY_SCALE = 0.606531
RW_LN_EPS = 64e-5
GLA_HEADS = 4
GLA_V_W = D_MODEL - RW_W
GLA_QK_W = GLA_V_W // 2
GLA_DV = GLA_V_W // GLA_HEADS
GLA_DK = GLA_QK_W // GLA_HEADS
GLA_GK_LORA = 16
GLA_GATE_NORM = 16.0
GLA_CHUNK = 32
N_GROUPS = 4
EXPERTS_PER_GROUP = 4
N_EXPERTS = N_GROUPS * EXPERTS_PER_GROUP
D_EXPERT = 256
TOP_K_IN_GROUP = 2
N_MOD = 6
EPS = 1e-6
IN_SPLITS = (RW_W, RW_W, RW_W, RW_DECAY_LORA, RW_DECAY_LORA, RW_AAA_LORA, RW_AAA_LORA, RW_GATE_LORA,
             GLA_QK_W, GLA_QK_W, GLA_V_W, GLA_V_W, GLA_GK_LORA, GLA_GK_LORA)
D_IN = RW_W * 3 + RW_DECAY_LORA * 2 + RW_AAA_LORA * 2 + RW_GATE_LORA + GLA_QK_W * 2 + GLA_V_W * 2 + GLA_GK_LORA * 2

kernel_name = "bidir_rwkv7_gla_hier_moe_prefix_ctx"


def split_cols(x, sizes):
    offsets = [int(o) for o in np.cumsum(sizes)[:-1]]
    return jnp.split(x, offsets, axis=-1)


def rmsnorm(x, g):
    xf = x.astype(jnp.float32)
    return xf * lax.rsqrt(jnp.mean(xf * xf, axis=-1, keepdims=True) + EPS) * g.astype(jnp.float32)


def short_conv(x, w, grid):
    B, T, C = x.shape
    w = w.astype(jnp.float32)
    if grid:
        rows = T // GRID_W
        xg = jnp.pad(x.reshape(B, rows, GRID_W, C), ((0, 0), (1, 1), (1, 1), (0, 0)))
        out = sum(xg[:, i:i + rows, j:j + GRID_W, :] * w[i, j] for i in range(3) for j in range(3))
        return out.reshape(B, T, C)
    xp = jnp.pad(x, ((0, 0), (1, 1), (0, 0)))
    return sum(xp[:, j:j + T, :] * w[1, j] for j in range(3))


def rwkv_scan(r, w, k, a_vec, b_vec, v, s0, reverse):
    def step(S, inp):
        r_t, w_t, k_t, a_t, b_t, v_t = inp
        sa = jnp.einsum('bhvk,bhk->bhv', S, a_t)
        S = S * w_t[:, :, None, :] + sa[..., None] * b_t[:, :, None, :] + v_t[..., None] * k_t[:, :, None, :]
        return S, jnp.einsum('bhvk,bhk->bhv', S, r_t)
    xs = tuple(jnp.moveaxis(t, 1, 0) for t in (r, w, k, a_vec, b_vec, v))
    s_fin, ys = lax.scan(step, s0, xs, reverse=reverse)
    return jnp.moveaxis(ys, 0, 1), s_fin


def rwkv_mixer(r, k, v, lw, la, lg, s0, grid, p):
    B, T, _ = r.shape
    heads = lambda t: t.reshape(B, T, RW_HEADS, RW_HD)
    rkv = short_conv(jnp.concatenate([r, k, v], axis=-1), p['rw_conv'], grid)
    r, k, v = jnp.split(rkv, 3, axis=-1)
    g = jax.nn.sigmoid(lg) @ p['rw_g2']
    kk = heads(k * p['rw_k_k'])
    kk = kk * lax.rsqrt(jnp.sum(kk * kk, axis=-1, keepdims=True) + EPS)
    rh, kh, vh = heads(r), heads(k), heads(v)
    s0 = s0.astype(jnp.float32)
    ys, finals = [], []
    for d in range(2):
        w = jnp.exp(-RW_DECAY_SCALE * jax.nn.sigmoid(p['rw_w0'][d] + jnp.tanh(lw[d]) @ p['rw_w2'][d]))
        a = jax.nn.sigmoid(p['rw_a0'][d] + la[d] @ p['rw_a2'][d])
        kd = heads(k * (1.0 + (a - 1.0) * p['rw_k_a']))
        y, s_fin = rwkv_scan(rh, heads(w), kd, -kk, kk * heads(a), vh, s0[:, d], reverse=(d == 1))
        ys.append(y)
        finals.append(s_fin)
    y = ys[0] + ys[1]
    mu = jnp.mean(y, axis=-1, keepdims=True)
    var = jnp.mean(jnp.square(y - mu), axis=-1, keepdims=True)
    yn = ((y - mu) * lax.rsqrt(var + RW_LN_EPS)).reshape(B, T, RW_W) * p['rw_ln_w'] + p['rw_ln_b']
    bonus = (jnp.sum(rh * kh * p['rw_r_k'], axis=-1, keepdims=True) * vh).reshape(B, T, RW_W)
    return (yn + bonus) * g, jnp.stack(finals, axis=1)


def gla_chunk_scan(q, k, v, gk, s0):
    B, T = q.shape[:2]
    nc = T // GLA_CHUNK
    chunk = lambda t: jnp.moveaxis(t.reshape(B, nc, GLA_CHUNK, *t.shape[2:]), 1, 0)
    mask = np.tril(np.ones((GLA_CHUNK, GLA_CHUNK), dtype=bool))[None, :, :, None, None]

    def step(S, inp):
        qc, kc, vc, gc = inp
        bcum = jnp.cumsum(gc, axis=1)
        diff = bcum[:, :, None] - bcum[:, None, :]
        decay = jnp.where(mask, jnp.exp(jnp.where(mask, diff, 0.0)), 0.0)
        A = jnp.einsum('bihd,bjhd,bijhd->bhij', qc, kc, decay)
        o = jnp.einsum('bhij,bjhv->bihv', A, vc) + jnp.einsum('bihd,bhdv->bihv', qc * jnp.exp(bcum), S)
        blast = bcum[:, -1]
        S = S * jnp.exp(blast)[..., None] + jnp.einsum('bjhd,bjhv->bhdv', kc * jnp.exp(blast[:, None] - bcum), vc)
        return S, o

    s_fin, o = lax.scan(step, s0, tuple(chunk(t) for t in (q, k, v, gk)))
    return jnp.moveaxis(o, 0, 1).reshape(B, T, GLA_HEADS, GLA_DV), s_fin


def gla_mixer(q, k, v, og, lgk, s0, p):
    B, T, _ = q.shape
    qh = q.reshape(B, T, GLA_HEADS, GLA_DK) * (GLA_DK ** -0.5)
    kh = k.reshape(B, T, GLA_HEADS, GLA_DK)
    vh = v.reshape(B, T, GLA_HEADS, GLA_DV)
    s0 = s0.astype(jnp.float32)
    outs, finals = [], []
    for d in range(2):
        gk = (jax.nn.log_sigmoid(lgk[d] @ p['gla_gk2'][d] + p['gla_gk_b'][d]) / GLA_GATE_NORM).reshape(B, T, GLA_HEADS, GLA_DK)
        if d == 0:
            o, s_fin = gla_chunk_scan(qh, kh, vh, gk, s0[:, d])
        else:
            o, s_fin = gla_chunk_scan(qh[:, ::-1], kh[:, ::-1], vh[:, ::-1], gk[:, ::-1], s0[:, d])
            o = o[:, ::-1]
        outs.append(o)
        finals.append(s_fin)
    o = rmsnorm(outs[0] + outs[1], p['gla_norm_g']).reshape(B, T, GLA_V_W)
    return o * jax.nn.silu(og), jnp.stack(finals, axis=1)


def hier_moe(h, p):
    B, T, D = h.shape
    hf = h.reshape(B * T, D)
    n_tok = hf.shape[0]
    g_logits = (hf @ p['moe_w_group'] + p['moe_b_group']).astype(jnp.float32)
    g_prob = jax.nn.softmax(g_logits, axis=-1)
    g_idx = jnp.argmax(g_logits, axis=-1)
    g_w = jnp.take_along_axis(g_prob, g_idx[:, None], axis=1)
    e_logits = (hf @ p['moe_w_expert'] + p['moe_b_expert']).astype(jnp.float32).reshape(n_tok, N_GROUPS, EXPERTS_PER_GROUP)
    e_sel = jnp.take_along_axis(e_logits, g_idx[:, None, None], axis=1)[:, 0]
    top_v, top_i = lax.top_k(jax.nn.softmax(e_sel, axis=-1), TOP_K_IN_GROUP)
    wts = g_w * top_v / jnp.sum(top_v, axis=-1, keepdims=True)
    e_idx = g_idx[:, None] * EXPERTS_PER_GROUP + top_i
    combine = jnp.sum(jax.nn.one_hot(e_idx, N_EXPERTS, dtype=jnp.float32) * wts[..., None], axis=1)
    up = jnp.einsum('td,edf->tef', hf, p['moe_w1'])
    gate = jnp.einsum('td,edf->tef', hf, p['moe_w3'])
    act = jax.nn.silu(gate) * up * combine[:, :, None]
    return jnp.einsum('tef,efd->td', act, p['moe_w2']).reshape(B, T, D)


def trunk_layer(x, cond, s_rw0, s_gla0, grid, p):
    B = x.shape[0]
    mod = (jax.nn.silu(cond.astype(jnp.float32)) @ p['w_ada'] + p['b_ada']).reshape(B, N_MOD, D_MODEL)
    sh1, sc1, ga1, sh2, sc2, ga2 = [mod[:, i][:, None, :] for i in range(N_MOD)]
    h = rmsnorm(x, p['norm1_g']) * (1.0 + sc1) + sh1
    (r, k, v, lw_f, lw_b, la_f, la_b, lg, q_g, k_g, v_g, og, lgk_f, lgk_b) = split_cols(h @ p['w_in'], IN_SPLITS)
    y_rw, s_rw = rwkv_mixer(r, k, v, (lw_f, lw_b), (la_f, la_b), lg, s_rw0, grid, p)
    y_gla, s_gla = gla_mixer(q_g, k_g, v_g, og, (lgk_f, lgk_b), s_gla0, p)
    mix = jnp.concatenate([y_rw, y_gla], axis=-1) @ p['w_out']
    x = x + (ga1 * mix).astype(x.dtype)
    h2 = rmsnorm(x, p['norm2_g']) * (1.0 + sc2) + sh2
    x = x + (ga2 * hier_moe(h2, p)).astype(x.dtype)
    return x, s_rw, s_gla


def setup_inputs(seed: int = 0) -> dict:
    key = jax.random.key(seed)
    ks = jax.random.split(key, 40)
    nrm = lambda i, shape, s: jax.random.normal(ks[i], shape, jnp.float32) * s
    L = DEPTH
    return {
        'x_prompt': nrm(0, (BATCH, SEQ, D_MODEL), 1.0),
        'x_sample': nrm(1, (DEC_BATCH, DEC_SEQ, D_MODEL), 1.0),
        'state_rwkv': nrm(2, (DEC_BATCH, L, 2, RW_HEADS, RW_HD, RW_HD), 0.3),
        'state_gla': nrm(3, (DEC_BATCH, L, 2, GLA_HEADS, GLA_DK, GLA_DV), 0.3),
        'c': nrm(4, (DEC_BATCH, D_MODEL), 1.0),
        'c_ctx': nrm(5, (D_MODEL,), 1.0),
        'norm1_g': 1.0 + nrm(6, (L, D_MODEL), 0.02),
        'norm2_g': 1.0 + nrm(7, (L, D_MODEL), 0.02),
        'w_ada': nrm(8, (L, D_MODEL, N_MOD * D_MODEL), 0.5 * D_MODEL ** -0.5),
        'b_ada': nrm(9, (L, N_MOD * D_MODEL), 0.02),
        'w_in': nrm(10, (L, D_MODEL, D_IN), D_MODEL ** -0.5),
        'w_out': nrm(11, (L, RW_W + GLA_V_W, D_MODEL), (RW_W + GLA_V_W) ** -0.5),
        'rw_conv': nrm(12, (L, 3, 3, 3 * RW_W), 1.0 / 3.0),
        'rw_w0': nrm(13, (L, 2, RW_W), 0.5),
        'rw_w2': nrm(14, (L, 2, RW_DECAY_LORA, RW_W), 0.5 * RW_DECAY_LORA ** -0.5),
        'rw_a0': nrm(15, (L, 2, RW_W), 0.5),
        'rw_a2': nrm(16, (L, 2, RW_AAA_LORA, RW_W), 0.5 * RW_AAA_LORA ** -0.5),
        'rw_g2': nrm(17, (L, RW_GATE_LORA, RW_W), RW_GATE_LORA ** -0.5),
        'rw_k_k': 0.85 + nrm(18, (L, RW_W), 0.05),
        'rw_k_a': 1.0 + nrm(19, (L, RW_W), 0.05),
        'rw_r_k': nrm(20, (L, RW_HEADS, RW_HD), 0.1),
        'rw_ln_w': 1.0 + nrm(21, (L, RW_W), 0.02),
        'rw_ln_b': nrm(22, (L, RW_W), 0.02),
        'gla_gk2': nrm(23, (L, 2, GLA_GK_LORA, GLA_QK_W), 0.5 * GLA_GK_LORA ** -0.5),
        'gla_gk_b': 1.0 + nrm(24, (L, 2, GLA_QK_W), 0.5),
        'gla_norm_g': 1.0 + nrm(25, (L, GLA_DV), 0.02),
        'moe_w_group': nrm(26, (L, D_MODEL, N_GROUPS), D_MODEL ** -0.5),
        'moe_b_group': nrm(27, (L, N_GROUPS), 0.01),
        'moe_w_expert': nrm(28, (L, D_MODEL, N_EXPERTS), D_MODEL ** -0.5),
        'moe_b_expert': nrm(29, (L, N_EXPERTS), 0.01),
        'moe_w1': nrm(30, (L, N_EXPERTS, D_MODEL, D_EXPERT), D_MODEL ** -0.5),
        'moe_w3': nrm(31, (L, N_EXPERTS, D_MODEL, D_EXPERT), D_MODEL ** -0.5),
        'moe_w2': nrm(32, (L, N_EXPERTS, D_EXPERT, D_MODEL), D_EXPERT ** -0.5),
        'final_g': 1.0 + nrm(33, (D_MODEL,), 0.02),
    }


def reference(x_prompt, x_sample, state_rwkv, state_gla, c, c_ctx, norm1_g, norm2_g, w_ada, b_ada, w_in, w_out,
              rw_conv, rw_w0, rw_w2, rw_a0, rw_a2, rw_g2, rw_k_k, rw_k_a, rw_r_k, rw_ln_w, rw_ln_b,
              gla_gk2, gla_gk_b, gla_norm_g, moe_w_group, moe_b_group, moe_w_expert, moe_b_expert,
              moe_w1, moe_w3, moe_w2, final_g):
    def layer_params(l):
        return {'norm1_g': norm1_g[l], 'norm2_g': norm2_g[l], 'w_ada': w_ada[l], 'b_ada': b_ada[l],
                'w_in': w_in[l], 'w_out': w_out[l], 'rw_conv': rw_conv[l], 'rw_w0': rw_w0[l], 'rw_w2': rw_w2[l],
                'rw_a0': rw_a0[l], 'rw_a2': rw_a2[l], 'rw_g2': rw_g2[l], 'rw_k_k': rw_k_k[l], 'rw_k_a': rw_k_a[l],
                'rw_r_k': rw_r_k[l], 'rw_ln_w': rw_ln_w[l], 'rw_ln_b': rw_ln_b[l], 'gla_gk2': gla_gk2[l],
                'gla_gk_b': gla_gk_b[l], 'gla_norm_g': gla_norm_g[l], 'moe_w_group': moe_w_group[l],
                'moe_b_group': moe_b_group[l], 'moe_w_expert': moe_w_expert[l], 'moe_b_expert': moe_b_expert[l],
                'moe_w1': moe_w1[l], 'moe_w3': moe_w3[l], 'moe_w2': moe_w2[l]}

    b_ctx = x_prompt.shape[0]
    ctx_cond = jnp.broadcast_to(c_ctx[None, :], (b_ctx, D_MODEL))
    x = x_prompt
    rw_states, gla_states = [], []
    for l in range(DEPTH):
        zeros_rw = jnp.zeros((b_ctx, 2, RW_HEADS, RW_HD, RW_HD), jnp.float32)
        zeros_gla = jnp.zeros((b_ctx, 2, GLA_HEADS, GLA_DK, GLA_DV), jnp.float32)
        x, s_rw, s_gla = trunk_layer(x, ctx_cond, zeros_rw, zeros_gla, False, layer_params(l))
        rw_states.append(s_rw)
        gla_states.append(s_gla)
    y_prompt = rmsnorm(x, final_g).astype(x_prompt.dtype)
    new_state_rwkv = jnp.stack(rw_states, axis=1)
    new_state_gla = jnp.stack(gla_states, axis=1)

    xs = x_sample
    for l in range(DEPTH):
        xs, _, _ = trunk_layer(xs, c, state_rwkv[:, l], state_gla[:, l], True, layer_params(l))
    y_sample = rmsnorm(xs, final_g).astype(x_sample.dtype)
    return (y_prompt, y_sample, new_state_rwkv, new_state_gla)
```

```python
import functools

import jax
import jax.numpy as jnp
from jax import lax
from jax.experimental import pallas as pl
from jax.experimental.pallas import tpu as pltpu

F32 = jnp.float32
BF16 = jnp.bfloat16

D_MODEL = 1024
RW_W = 512
GLA_V_W = 512
GLA_QK_W = 256
N_EXPERTS = 16
D_EXPERT = 256
N_MOD = 6
EPS = 1e-6
RW_LN_EPS = 64e-5
RW_DECAY_SCALE = 0.606531
GLA_GATE_NORM = 16.0
GLA_Q_SCALE = 64 ** -0.5
GRID_W = 64

LANE = 128
CHUNK = 64
CONV_PAD = 128
D_PROJ = 28 * LANE
VMEM_LIMIT = 56 * 1024 * 1024

CB_R, CB_K, CB_V, CB_LORA, CB_LGK, CB_GQ, CB_GK, CB_GV, CB_OG = 0, 4, 8, 12, 15, 16, 18, 20, 24

_NN = (((1,), (0,)), ((), ()))
_NT = (((1,), (1,)), ((), ()))
_TN = (((0,), (0,)), ((), ()))


def _dot(a, b, dims=_NN):
    return lax.dot_general(a, b, dims, preferred_element_type=F32)


def _mm(a, b, dims=_NN):
    return _dot(a.astype(BF16), b.astype(BF16), dims)


def _split2(x):
    hi = x.astype(BF16)
    lo = (x - hi.astype(F32)).astype(BF16)
    return hi, lo


def _split3(x):
    hi = x.astype(BF16)
    r1 = x - hi.astype(F32)
    mid = r1.astype(BF16)
    lo = (r1 - mid.astype(F32)).astype(BF16)
    return hi, mid, lo


def _mm3(a, b, dims=_NN):
    ah, al = _split2(a)
    bh, bl = _split2(b)
    return _dot(ah, bh, dims) + _dot(ah, bl, dims) + _dot(al, bh, dims)


def _mm_exact_rhs(a, b01, dims=_NN):
    a1, a2, a3 = _split3(a)
    return _dot(a1, b01, dims) + _dot(a2, b01, dims) + _dot(a3, b01, dims)


def _mm_exact_lhs(a01, b, dims=_NN):
    b1, b2, b3 = _split3(b)
    return _dot(a01, b1, dims) + _dot(a01, b2, dims) + _dot(a01, b3, dims)


def _sigmoid(x):
    return 1.0 / (1.0 + jnp.exp(-x))


def _silu(x):
    return x * _sigmoid(x)


def _log_sigmoid(x):
    return jnp.minimum(x, 0.0) - jnp.log(1.0 + jnp.exp(-jnp.abs(x)))


def _iota(shape, dim):
    return lax.broadcasted_iota(jnp.int32, shape, dim)


def _cparams(n_axes):
    return pltpu.CompilerParams(dimension_semantics=("arbitrary",) * n_axes, vmem_limit_bytes=VMEM_LIMIT)


MOD_TN = 768


def _mod_kernel(c_ref, w_ref, b_ref, o_ref):
    o_ref[...] = _mm3(_silu(c_ref[...]), w_ref[...]) + b_ref[...]


def _modulation(cond8, w_ada, b_ada):
    n = w_ada.shape[1]
    return pl.pallas_call(
        _mod_kernel,
        grid=(n // MOD_TN,),
        in_specs=[pl.BlockSpec((8, D_MODEL), lambda j: (0, 0)),
                  pl.BlockSpec((D_MODEL, MOD_TN), lambda j: (0, j)),
                  pl.BlockSpec((1, MOD_TN), lambda j: (0, j))],
        out_specs=pl.BlockSpec((8, MOD_TN), lambda j: (0, j)),
        out_shape=jax.ShapeDtypeStruct((8, n), F32),
        compiler_params=_cparams(1),
        name="adaln_mod",
    )(cond8, w_ada, b_ada)


PROJ_TM = 256


def _rmsnorm_rows(x):
    return x * lax.rsqrt(jnp.mean(x * x, axis=-1, keepdims=True) + EPS)


def _inproj_kernel(x_ref, mod_ref, g_ref, w_ref, o_ref):
    m = mod_ref[0]
    h = _rmsnorm_rows(x_ref[...]) * g_ref[...] * (1.0 + m[1:2]) + m[0:1]
    o_ref[...] = _mm(h, w_ref[...])


def _mod_index(seq_len, tm, ctx_row):
    if ctx_row is not None:
        return lambda i: (ctx_row, 0, 0)
    return lambda i: (i // (seq_len // tm), 0, 0)


def _in_projection(x2d, mod, norm_g, w_in_p, seq_len, ctx_row):
    n = x2d.shape[0]
    return pl.pallas_call(
        _inproj_kernel,
        grid=(n // PROJ_TM,),
        in_specs=[pl.BlockSpec((PROJ_TM, D_MODEL), lambda i: (i, 0)),
                  pl.BlockSpec((1, N_MOD, D_MODEL), _mod_index(seq_len, PROJ_TM, ctx_row)),
                  pl.BlockSpec((1, D_MODEL), lambda i: (0, 0)),
                  pl.BlockSpec((D_MODEL, D_PROJ), lambda i: (0, 0))],
        out_specs=pl.BlockSpec((PROJ_TM, D_PROJ), lambda i: (i, 0)),
        out_shape=jax.ShapeDtypeStruct((n, D_PROJ), F32),
        compiler_params=_cparams(1),
        name="in_proj",
    )(x2d, mod, norm_g, w_in_p)


def _time_masks(reverse):
    r = _iota((2 * CHUNK, 2 * CHUNK), 0) % CHUNK
    c = _iota((2 * CHUNK, 2 * CHUNK), 1) % CHUNK
    if reverse:
        return r < c, r <= c
    return r > c, r >= c


def _cumsum_matrix(reverse):
    r = _iota((CHUNK, CHUNK), 0)
    c = _iota((CHUNK, CHUNK), 1)
    tri = (r <= c) if reverse else (r >= c)
    return tri.astype(BF16)


def _stack_heads(x, half):
    m0 = _iota(x.shape, 1) < half
    return jnp.concatenate([jnp.where(m0, x, 0.0), jnp.where(m0, 0.0, x)], axis=0)


def _block_ones():
    r = _iota((LANE, LANE), 0) // 64
    c = _iota((LANE, LANE), 1) // 64
    return (r == c).astype(BF16)


def _rwkv_chunk(r, lw, kd, a, b, v, s, reverse):
    cum = _mm_exact_lhs(_cumsum_matrix(reverse), lw)
    end = cum[0:1] if reverse else cum[CHUNK - 1:CHUNK]
    inv_w = jnp.exp(-cum)
    rem_w = jnp.exp(end - cum)
    a_s = _stack_heads(a * jnp.exp(cum - lw), 64)
    r_s = _stack_heads(r * jnp.exp(cum), 64)
    b_s = _stack_heads(b * inv_w, 64)
    k_s = _stack_heads(kd * inv_w, 64)
    v_s = _stack_heads(v, 64)
    bh_s = _stack_heads(b * rem_w, 64)
    kh_s = _stack_heads(kd * rem_w, 64)
    ar = jnp.concatenate([a_s, r_s], axis=0).astype(BF16)
    m = _dot(ar, jnp.concatenate([b_s, k_s], axis=0).astype(BF16), _NT)
    strict, incl = _time_masks(reverse)
    n2 = 2 * CHUNK
    l_ab = jnp.where(strict, m[0:n2, 0:n2], 0.0)
    l_ak = jnp.where(strict, m[0:n2, n2:2 * n2], 0.0)
    m_rb = jnp.where(incl, m[n2:2 * n2, 0:n2], 0.0)
    m_rk = jnp.where(incl, m[n2:2 * n2, n2:2 * n2], 0.0)
    eye = (_iota((n2, n2), 0) == _iota((n2, n2), 1)).astype(F32)
    p = eye + l_ab
    lp = l_ab
    for _ in range(5):
        lp = _mm(lp, lp)
        p = p + _mm(p, lp)
    xs = _dot(ar, s.astype(BF16), _NT)
    lv = _mm(jnp.concatenate([l_ak, m_rk], axis=0), v_s)
    u = _mm(p, xs[0:n2] + lv[0:n2])
    y2 = xs[n2:2 * n2] + lv[n2:2 * n2] + _mm(m_rb, u)
    y = y2[0:CHUNK] + y2[CHUNK:n2]
    s_new = s * jnp.exp(end) + _mm(jnp.concatenate([u, v_s], axis=0),
                                   jnp.concatenate([bh_s, kh_s], axis=0), _TN)
    return y, s_new


def _gla_chunk(q, k, v, g, s, reverse):
    cum = _mm_exact_lhs(_cumsum_matrix(reverse), g)
    end = cum[0:1] if reverse else cum[CHUNK - 1:CHUNK]
    q_s = _stack_heads(q * jnp.exp(cum), 64).astype(BF16)
    k_s = _stack_heads(k * jnp.exp(-cum), 64)
    kh_s = _stack_heads(k * jnp.exp(end - cum), 64)
    v_s = _stack_heads(v, 128)
    _, incl = _time_masks(reverse)
    att = jnp.where(incl, _dot(q_s, k_s.astype(BF16), _NT), 0.0)
    o2 = _mm(att, v_s) + _dot(q_s, s.astype(BF16), _NT)
    o = o2[0:CHUNK] + o2[CHUNK:2 * CHUNK]
    s_new = s * jnp.exp(end) + _mm(v_s, kh_s, _TN)
    return o, s_new


def _rwkv_kernel(seq_len, is_grid, r_ref, k_ref, v_ref, lora_ref, cwr_ref, cwk_ref, cwv_ref, w2_ref, a2_ref,
                 g2_ref, vec_ref, s0_ref, y_ref, sout_ref, pad_ref, rs_ref, ks_ref, vs_ref, kk_ref, yf_ref,
                 yb_ref, st_ref):
    n_chunks = seq_len // CHUNK
    vec = vec_ref[...]
    w0 = (vec[0:1], vec[1:2])
    a0 = (vec[2:3], vec[3:4])
    k_k, k_a, r_k, ln_w, ln_b = vec[4:5], vec[5:6], vec[6:7], vec[7:8], vec[8:9]
    ones_blk = _block_ones()

    def block_sum(x):
        return _mm_exact_rhs(x, ones_blk)

    zeros = jnp.zeros((CONV_PAD, 3 * LANE), F32)
    pad_ref[0:CONV_PAD, :] = zeros
    pad_ref[CONV_PAD + seq_len:2 * CONV_PAD + seq_len, :] = zeros
    pad_ref[CONV_PAD:CONV_PAD + seq_len, 0:LANE] = r_ref[...]
    pad_ref[CONV_PAD:CONV_PAD + seq_len, LANE:2 * LANE] = k_ref[...]
    pad_ref[CONV_PAD:CONV_PAD + seq_len, 2 * LANE:3 * LANE] = v_ref[...]
    cw = jnp.concatenate([cwr_ref[...], cwk_ref[...], cwv_ref[...]], axis=1)
    col = _iota((CHUNK, 3 * LANE), 0)

    def conv_body(c, carry):
        base = pl.multiple_of(CONV_PAD + c * CHUNK, CHUNK)
        acc = jnp.zeros((CHUNK, 3 * LANE), F32)
        for di in ((-1, 0, 1) if is_grid else (0,)):
            win = pad_ref[pl.ds(base + di * GRID_W - 8, CHUNK + 16), :]
            for dj in (-1, 0, 1):
                tap = (di + 1) * 3 + dj + 1
                term = win[8 + dj:8 + dj + CHUNK] * cw[tap:tap + 1]
                if is_grid and dj == -1:
                    term = jnp.where(col >= 1, term, 0.0)
                if is_grid and dj == 1:
                    term = jnp.where(col <= GRID_W - 2, term, 0.0)
                acc = acc + term
        off = pl.multiple_of(c * CHUNK, CHUNK)
        kc = acc[:, LANE:2 * LANE]
        kk = kc * k_k
        kk = kk * lax.rsqrt(block_sum(kk * kk) + EPS)
        rs_ref[pl.ds(off, CHUNK), :] = acc[:, 0:LANE]
        ks_ref[pl.ds(off, CHUNK), :] = kc
        vs_ref[pl.ds(off, CHUNK), :] = acc[:, 2 * LANE:3 * LANE]
        kk_ref[pl.ds(off, CHUNK), :] = kk
        return carry

    lax.fori_loop(0, n_chunks, conv_body, 0)

    st_ref[...] = s0_ref[0, :, 0]

    def scan_body(i, carry):
        for d in range(2):
            c = i if d == 0 else n_chunks - 1 - i
            off = pl.multiple_of(c * CHUNK, CHUNK)
            lora = lora_ref[pl.ds(off, CHUNK), :]
            lw = -RW_DECAY_SCALE * _sigmoid(w0[d] + _mm3(jnp.tanh(lora[:, 0:LANE]), w2_ref[d]))
            ag = _sigmoid(a0[d] + _mm3(lora[:, LANE:2 * LANE], a2_ref[d]))
            kc = ks_ref[pl.ds(off, CHUNK), :]
            kk = kk_ref[pl.ds(off, CHUNK), :]
            kd = kc * (1.0 + (ag - 1.0) * k_a)
            y, s_new = _rwkv_chunk(rs_ref[pl.ds(off, CHUNK), :], lw, kd, -kk, kk * ag,
                                   vs_ref[pl.ds(off, CHUNK), :], st_ref[d], reverse=(d == 1))
            st_ref[d] = s_new
            if d == 0:
                yf_ref[pl.ds(off, CHUNK), :] = y
            else:
                yb_ref[pl.ds(off, CHUNK), :] = y
        return carry

    lax.fori_loop(0, n_chunks, scan_body, 0)
    sout_ref[0, :, 0] = st_ref[...]

    def post_body(c, carry):
        off = pl.multiple_of(c * CHUNK, CHUNK)
        y = yf_ref[pl.ds(off, CHUNK), :] + yb_ref[pl.ds(off, CHUNK), :]
        mu = block_sum(y) * (1.0 / 64)
        dlt = y - mu
        var = block_sum(dlt * dlt) * (1.0 / 64)
        yn = dlt * lax.rsqrt(var + RW_LN_EPS) * ln_w + ln_b
        rc = rs_ref[pl.ds(off, CHUNK), :]
        kc = ks_ref[pl.ds(off, CHUNK), :]
        bonus = block_sum(rc * kc * r_k) * vs_ref[pl.ds(off, CHUNK), :]
        gate = _mm3(_sigmoid(lora_ref[pl.ds(off, CHUNK), 2 * LANE:3 * LANE]), g2_ref[...])
        y_ref[pl.ds(off, CHUNK), :] = (yn + bonus) * gate
        return carry

    lax.fori_loop(0, n_chunks, post_body, 0)


def _rwkv_mixer(proj, n_seq, seq_len, is_grid, prm, s0):
    n_pairs = RW_W // LANE
    col = lambda cb: (lambda b, p: (b, cb + p))
    fixed2 = lambda b, p: (0, p)
    kernel = functools.partial(_rwkv_kernel, seq_len, is_grid)
    y, s_out = pl.pallas_call(
        kernel,
        grid=(n_seq, n_pairs),
        in_specs=[pl.BlockSpec((seq_len, LANE), col(CB_R)),
                  pl.BlockSpec((seq_len, LANE), col(CB_K)),
                  pl.BlockSpec((seq_len, LANE), col(CB_V)),
                  pl.BlockSpec((seq_len, 3 * LANE), lambda b, p: (b, CB_LORA // 3)),
                  pl.BlockSpec((9, LANE), lambda b, p: (0, CB_R + p)),
                  pl.BlockSpec((9, LANE), lambda b, p: (0, CB_K + p)),
                  pl.BlockSpec((9, LANE), lambda b, p: (0, CB_V + p)),
                  pl.BlockSpec((2, LANE, LANE), lambda b, p: (0, 0, p)),
                  pl.BlockSpec((2, LANE, LANE), lambda b, p: (0, 0, p)),
                  pl.BlockSpec((LANE, LANE), fixed2),
                  pl.BlockSpec((16, LANE), fixed2),
                  pl.BlockSpec((1, 2, 1, LANE, LANE), lambda b, p: (b, 0, p, 0, 0))],
        out_specs=[pl.BlockSpec((seq_len, LANE), lambda b, p: (b, p)),
                   pl.BlockSpec((1, 2, 1, LANE, LANE), lambda b, p: (b, 0, p, 0, 0))],
        out_shape=[jax.ShapeDtypeStruct((n_seq * seq_len, RW_W), F32),
                   jax.ShapeDtypeStruct((n_seq, 2, n_pairs, LANE, LANE), F32)],
        scratch_shapes=[pltpu.VMEM((seq_len + 2 * CONV_PAD, 3 * LANE), F32)]
                       + [pltpu.VMEM((seq_len, LANE), F32)] * 6
                       + [pltpu.VMEM((2, LANE, LANE), F32)],
        compiler_params=_cparams(2),
        name="rwkv_mixer",
    )(proj, proj, proj, proj, prm['conv'], prm['conv'], prm['conv'], prm['w2p'], prm['a2p'], prm['g2'],
      prm['vec'], s0)
    return y, s_out


def _gla_kernel(seq_len, q_ref, k_ref, v_ref, og_ref, lgk_ref, gk2_ref, gvec_ref, s0_ref, y_ref, sout_ref,
                of_ref, ob_ref, st_ref):
    n_chunks = seq_len // CHUNK
    gvec = gvec_ref[...]
    st_ref[...] = s0_ref[0, :, 0]

    def scan_body(i, carry):
        for d in range(2):
            c = i if d == 0 else n_chunks - 1 - i
            off = pl.multiple_of(c * CHUNK, CHUNK)
            x = _mm3(lgk_ref[pl.ds(off, CHUNK), :], gk2_ref[d]) + gvec[d:d + 1]
            g = _log_sigmoid(x) * (1.0 / GLA_GATE_NORM)
            o, s_new = _gla_chunk(q_ref[pl.ds(off, CHUNK), :] * GLA_Q_SCALE, k_ref[pl.ds(off, CHUNK), :],
                                  v_ref[pl.ds(off, CHUNK), :], g, st_ref[d], reverse=(d == 1))
            st_ref[d] = s_new
            if d == 0:
                of_ref[pl.ds(off, CHUNK), :] = o
            else:
                ob_ref[pl.ds(off, CHUNK), :] = o
        return carry

    lax.fori_loop(0, n_chunks, scan_body, 0)
    sout_ref[0, :, 0] = st_ref[...]
    norm_g = gvec[2:3, 0:LANE]

    def post_body(c, carry):
        off = pl.multiple_of(c * CHUNK, CHUNK)
        for h in range(2):
            o = of_ref[pl.ds(off, CHUNK), h * LANE:(h + 1) * LANE] + ob_ref[pl.ds(off, CHUNK), h * LANE:(h + 1) * LANE]
            gate = _silu(og_ref[pl.ds(off, CHUNK), h * LANE:(h + 1) * LANE])
            y_ref[pl.ds(off, CHUNK), h * LANE:(h + 1) * LANE] = _rmsnorm_rows(o) * norm_g * gate
        return carry

    lax.fori_loop(0, n_chunks, post_body, 0)


def _gla_mixer(proj, n_seq, seq_len, prm, s0):
    n_pairs = GLA_QK_W // LANE
    kernel = functools.partial(_gla_kernel, seq_len)
    y, s_out = pl.pallas_call(
        kernel,
        grid=(n_seq, n_pairs),
        in_specs=[pl.BlockSpec((seq_len, LANE), lambda b, p: (b, CB_GQ + p)),
                  pl.BlockSpec((seq_len, LANE), lambda b, p: (b, CB_GK + p)),
                  pl.BlockSpec((seq_len, 2 * LANE), lambda b, p: (b, CB_GV // 2 + p)),
                  pl.BlockSpec((seq_len, 2 * LANE), lambda b, p: (b, CB_OG // 2 + p)),
                  pl.BlockSpec((seq_len, LANE), lambda b, p: (b, CB_LGK)),
                  pl.BlockSpec((2, LANE, LANE), lambda b, p: (0, 0, p)),
                  pl.BlockSpec((8, LANE), lambda b, p: (0, p)),
                  pl.BlockSpec((1, 2, 1, 2 * LANE, LANE), lambda b, p: (b, 0, p, 0, 0))],
        out_specs=[pl.BlockSpec((seq_len, 2 * LANE), lambda b, p: (b, p)),
                   pl.BlockSpec((1, 2, 1, 2 * LANE, LANE), lambda b, p: (b, 0, p, 0, 0))],
        out_shape=[jax.ShapeDtypeStruct((n_seq * seq_len, GLA_V_W), F32),
                   jax.ShapeDtypeStruct((n_seq, 2, n_pairs, 2 * LANE, LANE), F32)],
        scratch_shapes=[pltpu.VMEM((seq_len, 2 * LANE), F32)] * 2 + [pltpu.VMEM((2, 2 * LANE, LANE), F32)],
        compiler_params=_cparams(2),
        name="gla_mixer",
    )(proj, proj, proj, proj, proj, prm['gk2p'], prm['gvec'], s0)
    return y, s_out


OUT_TM = 256
ROUTE_NEG = -1e30
LANE_GROUP0 = N_EXPERTS


def _route(logits):
    lane = _iota(logits.shape, 1)
    lane_f = lane.astype(F32)
    big = float(LANE)
    is_g = (lane >= LANE_GROUP0) & (lane < LANE_GROUP0 + 4)
    gmax = jnp.max(jnp.where(is_g, logits, ROUTE_NEG), axis=-1, keepdims=True)
    gidx = jnp.min(jnp.where(is_g & (logits == gmax), lane_f, big), axis=-1, keepdims=True) - LANE_GROUP0
    gsum = jnp.sum(jnp.where(is_g, jnp.exp(jnp.minimum(logits - gmax, 0.0)), 0.0), axis=-1, keepdims=True)
    g_w = 1.0 / gsum
    in_grp = (lane < N_EXPERTS) & ((lane // 4).astype(F32) == gidx)
    m1 = jnp.max(jnp.where(in_grp, logits, ROUTE_NEG), axis=-1, keepdims=True)
    i1 = jnp.min(jnp.where(in_grp & (logits == m1), lane_f, big), axis=-1, keepdims=True)
    rest = in_grp & (lane_f != i1)
    m2 = jnp.max(jnp.where(rest, logits, ROUTE_NEG), axis=-1, keepdims=True)
    i2 = jnp.min(jnp.where(rest & (logits == m2), lane_f, big), axis=-1, keepdims=True)
    t = jnp.exp(m2 - m1)
    w1 = g_w / (1.0 + t)
    return jnp.where(lane_f == i1, w1, 0.0) + jnp.where(lane_f == i2, w1 * t, 0.0)


def _outproj_kernel(yr_ref, yg_ref, x_ref, mod_ref, wo_ref, g_ref, wr_ref, br_ref, x1_ref, h2_ref, cmb_ref):
    m = mod_ref[0]
    mix = _mm(yr_ref[...], wo_ref[0:RW_W, :]) + _mm(yg_ref[...], wo_ref[RW_W:RW_W + GLA_V_W, :])
    x1 = x_ref[...] + m[2:3] * mix
    h2 = _rmsnorm_rows(x1) * g_ref[...] * (1.0 + m[4:5]) + m[3:4]
    x1_ref[...] = x1
    h2_ref[...] = h2.astype(BF16)
    cmb_ref[...] = _route(_mm3(h2, wr_ref[...]) + br_ref[...])


def _out_projection(y_rw, y_gla, x2d, mod, w_out, norm_g, w_route, b_route, seq_len, ctx_row):
    n = x2d.shape[0]
    tile = lambda w: pl.BlockSpec((OUT_TM, w), lambda i: (i, 0))
    full = lambda a: pl.BlockSpec(a.shape, lambda i: (0,) * a.ndim)
    return pl.pallas_call(
        _outproj_kernel,
        grid=(n // OUT_TM,),
        in_specs=[tile(RW_W), tile(GLA_V_W), tile(D_MODEL),
                  pl.BlockSpec((1, N_MOD, D_MODEL), _mod_index(seq_len, OUT_TM, ctx_row)),
                  full(w_out), full(norm_g), full(w_route), full(b_route)],
        out_specs=[tile(D_MODEL), tile(D_MODEL), tile(LANE)],
        out_shape=[jax.ShapeDtypeStruct((n, D_MODEL), F32), jax.ShapeDtypeStruct((n, D_MODEL), BF16),
                   jax.ShapeDtypeStruct((n, LANE), F32)],
        compiler_params=_cparams(1),
        name="out_proj_router",
    )(y_rw, y_gla, x2d, mod, w_out, norm_g, w_route, b_route)


MOE_TM = 512


def _moe_kernel(h2_ref, cmb_ref, x1_ref, mod_ref, w1_ref, w3_ref, w2_ref, fg_ref, y_ref, acc_ref):
    e = pl.program_id(1)

    @pl.when(e == 0)
    def _():
        acc_ref[...] = jnp.zeros_like(acc_ref)

    h2 = h2_ref[...]
    cmb = cmb_ref[...]
    wgt = jnp.sum(jnp.where(_iota(cmb.shape, 1) == e, cmb, 0.0), axis=-1, keepdims=True)
    up = _dot(h2, w1_ref[0])
    gate = _dot(h2, w3_ref[0])
    acc_ref[...] += _mm(_silu(gate) * up * wgt, w2_ref[0])

    @pl.when(e == N_EXPERTS - 1)
    def _():
        x2 = x1_ref[...] + mod_ref[0][5:6] * acc_ref[...]
        y_ref[...] = _rmsnorm_rows(x2) * fg_ref[...]


def _moe(h2, cmb, x1, mod, w1, w3, w2, final_g, seq_len, ctx_row):
    n = h2.shape[0]
    mod_idx = _mod_index(seq_len, MOE_TM, ctx_row)
    tile = lambda w: pl.BlockSpec((MOE_TM, w), lambda i, e: (i, 0))
    return pl.pallas_call(
        _moe_kernel,
        grid=(n // MOE_TM, N_EXPERTS),
        in_specs=[tile(D_MODEL), tile(LANE), tile(D_MODEL),
                  pl.BlockSpec((1, N_MOD, D_MODEL), lambda i, e: mod_idx(i)),
                  pl.BlockSpec((1, D_MODEL, D_EXPERT), lambda i, e: (e, 0, 0)),
                  pl.BlockSpec((1, D_MODEL, D_EXPERT), lambda i, e: (e, 0, 0)),
                  pl.BlockSpec((1, D_EXPERT, D_MODEL), lambda i, e: (e, 0, 0)),
                  pl.BlockSpec((1, D_MODEL), lambda i, e: (0, 0))],
        out_specs=tile(D_MODEL),
        out_shape=jax.ShapeDtypeStruct((n, D_MODEL), F32),
        scratch_shapes=[pltpu.VMEM((MOE_TM, D_MODEL), F32)],
        compiler_params=_cparams(2),
        name="moe_experts",
    )(h2, cmb, x1, mod, w1, w3, w2, final_g)


def _pad_rows(x, rows):
    return jnp.pad(x, ((0, rows - x.shape[0]),) + ((0, 0),) * (x.ndim - 1))


def _pack_params(l, w_in, rw_conv, rw_w0, rw_w2, rw_a0, rw_a2, rw_g2, rw_k_k, rw_k_a, rw_r_k, rw_ln_w, rw_ln_b,
                 gla_gk2, gla_gk_b, gla_norm_g, moe_w_group, moe_b_group, moe_w_expert, moe_b_expert):
    wi = w_in[l]
    z = lambda n: jnp.zeros((D_MODEL, n), F32)
    w_in_p = jnp.concatenate([wi[:, 0:1920], wi[:, 3456:3488], z(LANE - 32), wi[:, 1920:3456]], axis=1).astype(BF16)
    z64 = jnp.zeros((64, RW_W), F32)
    w2p = jnp.stack([jnp.concatenate([rw_w2[l, 0], z64], 0), jnp.concatenate([z64, rw_w2[l, 1]], 0)])
    a2p = jnp.stack([jnp.concatenate([rw_a2[l, 0], z64], 0), jnp.concatenate([z64, rw_a2[l, 1]], 0)])
    vec = _pad_rows(jnp.stack([rw_w0[l, 0], rw_w0[l, 1], rw_a0[l, 0], rw_a0[l, 1], rw_k_k[l], rw_k_a[l],
                               rw_r_k[l].reshape(RW_W), rw_ln_w[l], rw_ln_b[l]]), 16)
    rw = {'conv': rw_conv[l].reshape(9, 3 * RW_W), 'w2p': w2p, 'a2p': a2p, 'g2': rw_g2[l], 'vec': vec}
    gk2p = jnp.stack([_pad_rows(gla_gk2[l, 0], LANE),
                      _pad_rows(jnp.concatenate([jnp.zeros((16, GLA_QK_W), F32), gla_gk2[l, 1]], 0), LANE)])
    gvec = _pad_rows(jnp.stack([gla_gk_b[l, 0], gla_gk_b[l, 1], jnp.tile(gla_norm_g[l], 2)]), 8)
    gla = {'gk2p': gk2p, 'gvec': gvec}
    w_route = jnp.concatenate([moe_w_expert[l], moe_w_group[l], z(LANE - N_EXPERTS - 4)], axis=1)
    b_route = jnp.concatenate([moe_b_expert[l], moe_b_group[l], jnp.zeros((LANE - N_EXPERTS - 4,), F32)])[None]
    return w_in_p, rw, gla, w_route, b_route


def _pair_states_rwkv(s):
    b = s.shape[0]
    s = s.reshape(b, 2, 4, 2, 64, 64)
    z = jnp.zeros_like(s[:, :, :, 0])
    top = jnp.concatenate([s[:, :, :, 0], z], axis=-1)
    bot = jnp.concatenate([z, s[:, :, :, 1]], axis=-1)
    return jnp.concatenate([top, bot], axis=-2)


def _unpair_states_rwkv(s):
    b = s.shape[0]
    return jnp.stack([s[..., 0:64, 0:64], s[..., 64:128, 64:128]], axis=3).reshape(b, 2, 8, 64, 64)


def _pair_states_gla(s):
    b = s.shape[0]
    st = jnp.swapaxes(s, -1, -2).reshape(b, 2, 2, 2, 128, 64)
    z = jnp.zeros_like(st[:, :, :, 0])
    top = jnp.concatenate([st[:, :, :, 0], z], axis=-1)
    bot = jnp.concatenate([z, st[:, :, :, 1]], axis=-1)
    return jnp.concatenate([top, bot], axis=-2)


def _unpair_states_gla(s):
    b = s.shape[0]
    st = jnp.stack([s[..., 0:128, 0:64], s[..., 128:256, 64:128]], axis=3).reshape(b, 2, 4, 128, 64)
    return jnp.swapaxes(st, -1, -2)


def _trunk_pass(x, mod, ctx_row, s_rw0, s_gla0, is_grid, pk, norm1_g, norm2_g, w_out, w1, w3, w2, final_g):
    n_seq, seq_len, _ = x.shape
    x2d = x.reshape(n_seq * seq_len, D_MODEL)
    w_in_p, rw, gla, w_route, b_route = pk
    proj = _in_projection(x2d, mod, norm1_g, w_in_p, seq_len, ctx_row)
    y_rw, s_rw = _rwkv_mixer(proj, n_seq, seq_len, is_grid, rw, s_rw0)
    y_gla, s_gla = _gla_mixer(proj, n_seq, seq_len, gla, s_gla0)
    x1, h2, cmb = _out_projection(y_rw, y_gla, x2d, mod, w_out, norm2_g, w_route, b_route, seq_len, ctx_row)
    y = _moe(h2, cmb, x1, mod, w1, w3, w2, final_g, seq_len, ctx_row)
    return y.reshape(n_seq, seq_len, D_MODEL), s_rw, s_gla


def kernel(x_prompt, x_sample, state_rwkv, state_gla, c, c_ctx, norm1_g, norm2_g, w_ada, b_ada, w_in, w_out,
           rw_conv, rw_w0, rw_w2, rw_a0, rw_a2, rw_g2, rw_k_k, rw_k_a, rw_r_k, rw_ln_w, rw_ln_b,
           gla_gk2, gla_gk_b, gla_norm_g, moe_w_group, moe_b_group, moe_w_expert, moe_b_expert,
           moe_w1, moe_w3, moe_w2, final_g):
    depth = w_in.shape[0]
    assert depth == 1, "the packed layout below handles the single-layer trunk of this problem"
    l = 0
    n_dec = x_sample.shape[0]
    ctx_row = n_dec
    cond8 = _pad_rows(jnp.concatenate([c, c_ctx[None]], axis=0), 8)
    mod = _modulation(cond8, w_ada[l], b_ada[l][None]).reshape(8, N_MOD, D_MODEL)
    pk = _pack_params(l, w_in, rw_conv, rw_w0, rw_w2, rw_a0, rw_a2, rw_g2, rw_k_k, rw_k_a, rw_r_k, rw_ln_w,
                      rw_ln_b, gla_gk2, gla_gk_b, gla_norm_g, moe_w_group, moe_b_group, moe_w_expert,
                      moe_b_expert)
    shared = (pk, norm1_g[l][None], norm2_g[l][None], w_out[l].astype(BF16), moe_w1[l].astype(BF16),
              moe_w3[l].astype(BF16), moe_w2[l].astype(BF16), final_g[None])

    n_ctx = x_prompt.shape[0]
    zeros_rw = jnp.zeros((n_ctx, 2, RW_W // LANE, LANE, LANE), F32)
    zeros_gla = jnp.zeros((n_ctx, 2, GLA_QK_W // LANE, 2 * LANE, LANE), F32)
    y_prompt, s_rw, s_gla = _trunk_pass(x_prompt, mod, ctx_row, zeros_rw, zeros_gla, False, *shared)
    new_state_rwkv = _unpair_states_rwkv(s_rw)[:, None]
    new_state_gla = _unpair_states_gla(s_gla)[:, None]

    y_sample, _, _ = _trunk_pass(x_sample, mod, None, _pair_states_rwkv(state_rwkv[:, l]),
                                 _pair_states_gla(state_gla[:, l]), True, *shared)
    return (y_prompt, y_sample, new_state_rwkv, new_state_gla)
```

```python
import functools

import jax
import jax.numpy as jnp
from jax import lax
from jax.experimental import pallas as pl
from jax.experimental.pallas import tpu as pltpu

F32 = jnp.float32
BF16 = jnp.bfloat16

D_MODEL = 1024
RW_W = 512
GLA_V_W = 512
GLA_QK_W = 256
N_EXPERTS = 16
D_EXPERT = 256
N_MOD = 6
EPS = 1e-6
RW_LN_EPS = 64e-5
RW_DECAY_SCALE = 0.606531
GLA_GATE_NORM = 16.0
GLA_Q_SCALE = 64 ** -0.5
GRID_W = 64

LANE = 128
CHUNK = 64
CONV_PAD = 128
TERM_UNROLL = 4
D_PROJ = 28 * LANE
VMEM_LIMIT = 56 * 1024 * 1024

CB_R, CB_K, CB_V, CB_LORA, CB_LGK, CB_GQ, CB_GK, CB_GV, CB_OG = 0, 4, 8, 12, 15, 16, 18, 20, 24

_NN = (((1,), (0,)), ((), ()))
_NT = (((1,), (1,)), ((), ()))
_TN = (((0,), (0,)), ((), ()))


def _dot(a, b, dims=_NN):
    return lax.dot_general(a, b, dims, preferred_element_type=F32)


def _mm(a, b, dims=_NN):
    return _dot(a.astype(BF16), b.astype(BF16), dims)


def _split2(x):
    hi = x.astype(BF16)
    lo = (x - hi.astype(F32)).astype(BF16)
    return hi, lo


def _split3(x):
    hi = x.astype(BF16)
    r1 = x - hi.astype(F32)
    mid = r1.astype(BF16)
    lo = (r1 - mid.astype(F32)).astype(BF16)
    return hi, mid, lo


def _mm3(a, b, dims=_NN):
    ah, al = _split2(a)
    bh, bl = _split2(b)
    return _dot(ah, bh, dims) + _dot(ah, bl, dims) + _dot(al, bh, dims)


def _mm_exact_rhs(a, b01, dims=_NN):
    a1, a2, a3 = _split3(a)
    return _dot(a1, b01, dims) + _dot(a2, b01, dims) + _dot(a3, b01, dims)


def _mm_exact_lhs(a01, b, dims=_NN):
    b1, b2, b3 = _split3(b)
    return _dot(a01, b1, dims) + _dot(a01, b2, dims) + _dot(a01, b3, dims)


def _sigmoid(x):
    return 1.0 / (1.0 + jnp.exp(-x))


def _silu(x):
    return x * _sigmoid(x)


def _log_sigmoid(x):
    return jnp.minimum(x, 0.0) - jnp.log(1.0 + jnp.exp(-jnp.abs(x)))


def _iota(shape, dim):
    return lax.broadcasted_iota(jnp.int32, shape, dim)


def _cparams(n_axes):
    return pltpu.CompilerParams(dimension_semantics=("arbitrary",) * n_axes, vmem_limit_bytes=VMEM_LIMIT)


MOD_TN = 768


def _mod_kernel(c_ref, w_ref, b_ref, o_ref):
    o_ref[...] = _mm3(_silu(c_ref[...]), w_ref[...]) + b_ref[...]


def _modulation(cond8, w_ada, b_ada):
    n = w_ada.shape[1]
    return pl.pallas_call(
        _mod_kernel,
        grid=(n // MOD_TN,),
        in_specs=[pl.BlockSpec((8, D_MODEL), lambda j: (0, 0)),
                  pl.BlockSpec((D_MODEL, MOD_TN), lambda j: (0, j)),
                  pl.BlockSpec((1, MOD_TN), lambda j: (0, j))],
        out_specs=pl.BlockSpec((8, MOD_TN), lambda j: (0, j)),
        out_shape=jax.ShapeDtypeStruct((8, n), F32),
        compiler_params=_cparams(1),
        name="adaln_mod",
    )(cond8, w_ada, b_ada)


PROJ_TM = 256


def _rmsnorm_rows(x):
    return x * lax.rsqrt(jnp.mean(x * x, axis=-1, keepdims=True) + EPS)


def _inproj_kernel(x_ref, mod_ref, g_ref, w_ref, o_ref):
    m = mod_ref[0]
    h = _rmsnorm_rows(x_ref[...]) * g_ref[...] * (1.0 + m[1:2]) + m[0:1]
    o_ref[...] = _mm(h, w_ref[...])


def _mod_index(seq_len, tm, ctx_row):
    if ctx_row is not None:
        return lambda i: (ctx_row, 0, 0)
    return lambda i: (i // (seq_len // tm), 0, 0)


def _in_projection(x2d, mod, norm_g, w_in_p, seq_len, ctx_row):
    n = x2d.shape[0]
    return pl.pallas_call(
        _inproj_kernel,
        grid=(n // PROJ_TM,),
        in_specs=[pl.BlockSpec((PROJ_TM, D_MODEL), lambda i: (i, 0)),
                  pl.BlockSpec((1, N_MOD, D_MODEL), _mod_index(seq_len, PROJ_TM, ctx_row)),
                  pl.BlockSpec((1, D_MODEL), lambda i: (0, 0)),
                  pl.BlockSpec((D_MODEL, D_PROJ), lambda i: (0, 0))],
        out_specs=pl.BlockSpec((PROJ_TM, D_PROJ), lambda i: (i, 0)),
        out_shape=jax.ShapeDtypeStruct((n, D_PROJ), F32),
        compiler_params=_cparams(1),
        name="in_proj",
    )(x2d, mod, norm_g, w_in_p)


def _time_masks(reverse):
    r = _iota((2 * CHUNK, 2 * CHUNK), 0) % CHUNK
    c = _iota((2 * CHUNK, 2 * CHUNK), 1) % CHUNK
    if reverse:
        return r < c, r <= c
    return r > c, r >= c


def _cumsum_matrix(reverse):
    r = _iota((CHUNK, CHUNK), 0)
    c = _iota((CHUNK, CHUNK), 1)
    tri = (r <= c) if reverse else (r >= c)
    return tri.astype(BF16)


def _stack_heads(x, half):
    m0 = _iota(x.shape, 1) < half
    return jnp.concatenate([jnp.where(m0, x, 0.0), jnp.where(m0, 0.0, x)], axis=0)


def _block_ones():
    r = _iota((LANE, LANE), 0) // 64
    c = _iota((LANE, LANE), 1) // 64
    return (r == c).astype(BF16)


def _rwkv_chunk_terms(insts):
    n2 = 2 * CHUNK
    eye = (_iota((n2, n2), 0) == _iota((n2, n2), 1)).astype(F32)
    cums = [_mm_exact_lhs(_cumsum_matrix(rev), lw) for (_, lw, _, _, _, _, rev) in insts]
    pre = []
    for (r, lw, kd, a, b, v, rev), cum in zip(insts, cums):
        end = cum[0:1] if rev else cum[CHUNK - 1:CHUNK]
        inv_w = jnp.exp(-cum)
        rem_w = jnp.exp(end - cum)
        a_s = _stack_heads(a * jnp.exp(cum - lw), 64)
        r_s = _stack_heads(r * jnp.exp(cum), 64)
        bk = jnp.concatenate([_stack_heads(b * inv_w, 64), _stack_heads(kd * inv_w, 64)], axis=0).astype(BF16)
        bkh = jnp.concatenate([_stack_heads(b * rem_w, 64), _stack_heads(kd * rem_w, 64)], axis=0).astype(BF16)
        pre.append((a_s, r_s, bk, bkh, _stack_heads(v, 64), jnp.exp(end)))
    ms = [_dot(jnp.concatenate([a_s, r_s], axis=0).astype(BF16), bk, _NT) for (a_s, r_s, bk, _, _, _) in pre]
    mats = []
    for m, (_, _, _, _, _, _, rev) in zip(ms, insts):
        strict, incl = _time_masks(rev)
        l_ab = jnp.where(strict, m[0:n2, 0:n2], 0.0)
        l_akrk = jnp.concatenate([jnp.where(strict, m[0:n2, n2:2 * n2], 0.0),
                                  jnp.where(incl, m[n2:2 * n2, n2:2 * n2], 0.0)], axis=0).astype(BF16)
        m_rb = jnp.where(incl, m[n2:2 * n2, 0:n2], 0.0).astype(BF16)
        mats.append((l_ab, l_akrk, m_rb))
    lvs = [_dot(l_akrk, pr[4].astype(BF16)) for (_, l_akrk, _), pr in zip(mats, pre)]
    ps = [eye + l_ab for (l_ab, _, _) in mats]
    lps = [l_ab.astype(BF16) for (l_ab, _, _) in mats]
    for _ in range(5):
        lps = [_dot(lp, lp).astype(BF16) for lp in lps]
        ps = [p + _dot(p.astype(BF16), lp) for p, lp in zip(ps, lps)]
    pxs = [_mm(p, jnp.concatenate([pr[0], lv[0:n2]], axis=1)) for p, pr, lv in zip(ps, pre, lvs)]
    mzs = [_dot(mt[2], px.astype(BF16)) for mt, px in zip(mats, pxs)]
    out = []
    for pr, lv, px, mz in zip(pre, lvs, pxs, mzs):
        a_s, r_s, _, bkh, v_s, w_end = pr
        q2 = r_s + mz[:, 0:n2]
        y02 = mz[:, n2:2 * n2] + lv[n2:2 * n2]
        t = _dot(px[:, 0:n2].astype(BF16), bkh[0:n2], _TN)
        g = _dot(jnp.concatenate([px[:, n2:2 * n2], v_s], axis=0).astype(BF16), bkh, _TN)
        out.append((t.astype(BF16), g, w_end, (q2[0:CHUNK] + q2[CHUNK:n2]).astype(BF16),
                    y02[0:CHUNK] + y02[CHUNK:n2]))
    return out


def _gla_chunk_terms(insts):
    cums = [_mm_exact_lhs(_cumsum_matrix(rev), g) for (_, _, _, g, rev) in insts]
    pre = []
    for (q, k, v, g, rev), cum in zip(insts, cums):
        end = cum[0:1] if rev else cum[CHUNK - 1:CHUNK]
        qt = q * jnp.exp(cum)
        k_s = _stack_heads(k * jnp.exp(-cum), 64).astype(BF16)
        kh_s = _stack_heads(k * jnp.exp(end - cum), 64).astype(BF16)
        pre.append((qt.astype(BF16), _stack_heads(qt, 64).astype(BF16), k_s, kh_s,
                    _stack_heads(v, 128).astype(BF16), jnp.exp(end)))
    atts = [_dot(pr[1], pr[2], _NT) for pr in pre]
    atts = [jnp.where(_time_masks(inst[4])[1], att, 0.0).astype(BF16) for att, inst in zip(atts, insts)]
    o2s = [_dot(att, pr[4]) for att, pr in zip(atts, pre)]
    kvs = [_dot(pr[4], pr[3], _TN) for pr in pre]
    return [(pr[0], o2[0:CHUNK] + o2[CHUNK:2 * CHUNK], pr[5], kv) for pr, o2, kv in zip(pre, o2s, kvs)]


def _rwkv_kernel(seq_len, is_grid, r_ref, k_ref, v_ref, lora_ref, cwr_ref, cwk_ref, cwv_ref, w2_ref, a2_ref,
                 g2_ref, vec_ref, s0_ref, y_ref, sout_ref, pad_ref, bonus_ref, gate_ref, yf_ref,
                 yb_ref, st_ref, tt_ref, tg_ref, tw_ref, tq_ref, ty_ref):
    n_chunks = seq_len // CHUNK
    vec = vec_ref[...]
    w0 = (vec[0:1], vec[1:2])
    a0 = (vec[2:3], vec[3:4])
    k_k, k_a, r_k, ln_w, ln_b = vec[4:5], vec[5:6], vec[6:7], vec[7:8], vec[8:9]
    ones_blk = _block_ones()

    def block_sum(x):
        return _mm_exact_rhs(x, ones_blk)

    zeros = jnp.zeros((CONV_PAD, 3 * LANE), F32)
    pad_ref[0:CONV_PAD, :] = zeros
    pad_ref[CONV_PAD + seq_len:2 * CONV_PAD + seq_len, :] = zeros
    pad_ref[CONV_PAD:CONV_PAD + seq_len, 0:LANE] = r_ref[...]
    pad_ref[CONV_PAD:CONV_PAD + seq_len, LANE:2 * LANE] = k_ref[...]
    pad_ref[CONV_PAD:CONV_PAD + seq_len, 2 * LANE:3 * LANE] = v_ref[...]
    cw = jnp.concatenate([cwr_ref[...], cwk_ref[...], cwv_ref[...]], axis=1)
    col = _iota((CHUNK, 3 * LANE), 0)

    def conv_chunk(c):
        base = pl.multiple_of(CONV_PAD + c * CHUNK, CHUNK)
        acc = jnp.zeros((CHUNK, 3 * LANE), F32)
        for di in ((-1, 0, 1) if is_grid else (0,)):
            win = pad_ref[pl.ds(base + di * GRID_W - 8, CHUNK + 16), :]
            for dj in (-1, 0, 1):
                tap = (di + 1) * 3 + dj + 1
                term = win[8 + dj:8 + dj + CHUNK] * cw[tap:tap + 1]
                if is_grid and dj == -1:
                    term = jnp.where(col >= 1, term, 0.0)
                if is_grid and dj == 1:
                    term = jnp.where(col <= GRID_W - 2, term, 0.0)
                acc = acc + term
        return acc[:, 0:LANE], acc[:, LANE:2 * LANE], acc[:, 2 * LANE:3 * LANE]

    def terms_body(j, carry):
        cs = [j * TERM_UNROLL + u for u in range(TERM_UNROLL)]
        offs = [pl.multiple_of(c * CHUNK, CHUNK) for c in cs]
        rkv = [conv_chunk(c) for c in cs]
        loras = [lora_ref[pl.ds(off, CHUNK), :] for off in offs]
        kks = [kc * k_k for (_, kc, _) in rkv]
        kk_ss = [block_sum(kk * kk) for kk in kks]
        bon_ss = [block_sum(rc * kc * r_k) for (rc, kc, _) in rkv]
        gates = [_mm3(_sigmoid(lo[:, 2 * LANE:3 * LANE]), g2_ref[...]) for lo in loras]
        lws = [[_mm3(jnp.tanh(lo[:, 0:LANE]), w2_ref[d]) for d in range(2)] for lo in loras]
        ags = [[_mm3(lo[:, LANE:2 * LANE], a2_ref[d]) for d in range(2)] for lo in loras]
        insts, where = [], []
        for u in range(TERM_UNROLL):
            rc, kc, vc = rkv[u]
            kk = kks[u] * lax.rsqrt(kk_ss[u] + EPS)
            bonus_ref[pl.ds(offs[u], CHUNK), :] = bon_ss[u] * vc
            gate_ref[pl.ds(offs[u], CHUNK), :] = gates[u]
            for d in range(2):
                lw = -RW_DECAY_SCALE * _sigmoid(w0[d] + lws[u][d])
                ag = _sigmoid(a0[d] + ags[u][d])
                kd = kc * (1.0 + (ag - 1.0) * k_a)
                insts.append((rc, lw, kd, -kk, kk * ag, vc, d == 1))
                where.append((d, cs[u]))
        for (d, c), (t, g, w_end, q, y0) in zip(where, _rwkv_chunk_terms(insts)):
            tt_ref[d, c] = t
            tg_ref[d, c] = g
            tw_ref[d, c] = jnp.broadcast_to(w_end, (8, LANE))
            tq_ref[d, c] = q
            ty_ref[d, c] = y0
        return carry

    lax.fori_loop(0, n_chunks // TERM_UNROLL, terms_body, 0)

    st_ref[...] = s0_ref[0, :, 0]

    def scan_body(i, carry):
        cs = (i, n_chunks - 1 - i)
        ss = [st_ref[d] for d in range(2)]
        sb = [s.astype(BF16) for s in ss]
        ys = [_dot(tq_ref[d, cs[d]], sb[d], _NT) + ty_ref[d, cs[d]] for d in range(2)]
        sn = [ss[d] * tw_ref[d, cs[d]][0:1] + _dot(sb[d], tt_ref[d, cs[d]]) + tg_ref[d, cs[d]] for d in range(2)]
        for d, out_ref in enumerate((yf_ref, yb_ref)):
            st_ref[d] = sn[d]
            out_ref[pl.ds(pl.multiple_of(cs[d] * CHUNK, CHUNK), CHUNK), :] = ys[d]
        return carry

    lax.fori_loop(0, n_chunks, scan_body, 0)
    sout_ref[0, :, 0] = st_ref[...]

    def post_body(j, carry):
        offs = [pl.multiple_of((j * TERM_UNROLL + u) * CHUNK, CHUNK) for u in range(TERM_UNROLL)]
        ys = [yf_ref[pl.ds(off, CHUNK), :] + yb_ref[pl.ds(off, CHUNK), :] for off in offs]
        mus = [block_sum(y) * (1.0 / 64) for y in ys]
        dlts = [y - mu for y, mu in zip(ys, mus)]
        vrs = [block_sum(dlt * dlt) * (1.0 / 64) for dlt in dlts]
        for off, dlt, var in zip(offs, dlts, vrs):
            yn = dlt * lax.rsqrt(var + RW_LN_EPS) * ln_w + ln_b
            y_ref[pl.ds(off, CHUNK), :] = (yn + bonus_ref[pl.ds(off, CHUNK), :]) * gate_ref[pl.ds(off, CHUNK), :]
        return carry

    lax.fori_loop(0, n_chunks // TERM_UNROLL, post_body, 0)


def _rwkv_mixer(proj, n_seq, seq_len, is_grid, prm, s0):
    n_pairs = RW_W // LANE
    n_chunks = seq_len // CHUNK
    assert n_chunks % TERM_UNROLL == 0
    col = lambda cb: (lambda b, p: (b, cb + p))
    fixed2 = lambda b, p: (0, p)
    kernel = functools.partial(_rwkv_kernel, seq_len, is_grid)
    y, s_out = pl.pallas_call(
        kernel,
        grid=(n_seq, n_pairs),
        in_specs=[pl.BlockSpec((seq_len, LANE), col(CB_R)),
                  pl.BlockSpec((seq_len, LANE), col(CB_K)),
                  pl.BlockSpec((seq_len, LANE), col(CB_V)),
                  pl.BlockSpec((seq_len, 3 * LANE), lambda b, p: (b, CB_LORA // 3)),
                  pl.BlockSpec((9, LANE), lambda b, p: (0, CB_R + p)),
                  pl.BlockSpec((9, LANE), lambda b, p: (0, CB_K + p)),
                  pl.BlockSpec((9, LANE), lambda b, p: (0, CB_V + p)),
                  pl.BlockSpec((2, LANE, LANE), lambda b, p: (0, 0, p)),
                  pl.BlockSpec((2, LANE, LANE), lambda b, p: (0, 0, p)),
                  pl.BlockSpec((LANE, LANE), fixed2),
                  pl.BlockSpec((16, LANE), fixed2),
                  pl.BlockSpec((1, 2, 1, LANE, LANE), lambda b, p: (b, 0, p, 0, 0))],
        out_specs=[pl.BlockSpec((seq_len, LANE), lambda b, p: (b, p)),
                   pl.BlockSpec((1, 2, 1, LANE, LANE), lambda b, p: (b, 0, p, 0, 0))],
        out_shape=[jax.ShapeDtypeStruct((n_seq * seq_len, RW_W), F32),
                   jax.ShapeDtypeStruct((n_seq, 2, n_pairs, LANE, LANE), F32)],
        scratch_shapes=[pltpu.VMEM((seq_len + 2 * CONV_PAD, 3 * LANE), F32)]
                       + [pltpu.VMEM((seq_len, LANE), F32)] * 4
                       + [pltpu.VMEM((2, LANE, LANE), F32),
                          pltpu.VMEM((2, n_chunks, LANE, LANE), BF16),
                          pltpu.VMEM((2, n_chunks, LANE, LANE), F32),
                          pltpu.VMEM((2, n_chunks, 8, LANE), F32),
                          pltpu.VMEM((2, n_chunks, CHUNK, LANE), BF16),
                          pltpu.VMEM((2, n_chunks, CHUNK, LANE), F32)],
        compiler_params=_cparams(2),
        name="rwkv_mixer",
    )(proj, proj, proj, proj, prm['conv'], prm['conv'], prm['conv'], prm['w2p'], prm['a2p'], prm['g2'],
      prm['vec'], s0)
    return y, s_out


def _gla_kernel(seq_len, q_ref, k_ref, v_ref, og_ref, lgk_ref, gk2_ref, gvec_ref, s0_ref, y_ref, sout_ref,
                of_ref, ob_ref, st_ref, tq_ref, to_ref, ta_ref, tkv_ref):
    n_chunks = seq_len // CHUNK
    gvec = gvec_ref[...]

    def terms_body(j, carry):
        insts, where = [], []
        for u in range(TERM_UNROLL):
            c = j * TERM_UNROLL + u
            off = pl.multiple_of(c * CHUNK, CHUNK)
            lgk = lgk_ref[pl.ds(off, CHUNK), :]
            qc = q_ref[pl.ds(off, CHUNK), :] * GLA_Q_SCALE
            kc = k_ref[pl.ds(off, CHUNK), :]
            vc = v_ref[pl.ds(off, CHUNK), :]
            for d in range(2):
                g = _log_sigmoid(_mm3(lgk, gk2_ref[d]) + gvec[d:d + 1]) * (1.0 / GLA_GATE_NORM)
                insts.append((qc, kc, vc, g, d == 1))
                where.append((d, c))
        for (d, c), (qt, o0, a_end, kv) in zip(where, _gla_chunk_terms(insts)):
            tq_ref[d, c] = qt
            to_ref[d, c] = o0
            ta_ref[d, c] = jnp.broadcast_to(a_end, (8, LANE))
            tkv_ref[d, c] = kv
        return carry

    lax.fori_loop(0, n_chunks // TERM_UNROLL, terms_body, 0)

    st_ref[...] = s0_ref[0, :, 0]

    def scan_body(i, carry):
        cs = (i, n_chunks - 1 - i)
        ss = [st_ref[d] for d in range(2)]
        os_ = [_dot(tq_ref[d, cs[d]], ss[d].astype(BF16), _NT) + to_ref[d, cs[d]] for d in range(2)]
        for d, out_ref in enumerate((of_ref, ob_ref)):
            st_ref[d] = ss[d] * ta_ref[d, cs[d]][0:1] + tkv_ref[d, cs[d]]
            out_ref[pl.ds(pl.multiple_of(cs[d] * CHUNK, CHUNK), CHUNK), :] = os_[d]
        return carry

    lax.fori_loop(0, n_chunks, scan_body, 0)
    sout_ref[0, :, 0] = st_ref[...]
    norm_g = gvec[2:3, 0:LANE]

    def post_body(c, carry):
        off = pl.multiple_of(c * CHUNK, CHUNK)
        for h in range(2):
            o = of_ref[pl.ds(off, CHUNK), h * LANE:(h + 1) * LANE] + ob_ref[pl.ds(off, CHUNK), h * LANE:(h + 1) * LANE]
            gate = _silu(og_ref[pl.ds(off, CHUNK), h * LANE:(h + 1) * LANE])
            y_ref[pl.ds(off, CHUNK), h * LANE:(h + 1) * LANE] = _rmsnorm_rows(o) * norm_g * gate
        return carry

    lax.fori_loop(0, n_chunks, post_body, 0)


def _gla_mixer(proj, n_seq, seq_len, prm, s0):
    n_pairs = GLA_QK_W // LANE
    n_chunks = seq_len // CHUNK
    assert n_chunks % TERM_UNROLL == 0
    kernel = functools.partial(_gla_kernel, seq_len)
    y, s_out = pl.pallas_call(
        kernel,
        grid=(n_seq, n_pairs),
        in_specs=[pl.BlockSpec((seq_len, LANE), lambda b, p: (b, CB_GQ + p)),
                  pl.BlockSpec((seq_len, LANE), lambda b, p: (b, CB_GK + p)),
                  pl.BlockSpec((seq_len, 2 * LANE), lambda b, p: (b, CB_GV // 2 + p)),
                  pl.BlockSpec((seq_len, 2 * LANE), lambda b, p: (b, CB_OG // 2 + p)),
                  pl.BlockSpec((seq_len, LANE), lambda b, p: (b, CB_LGK)),
                  pl.BlockSpec((2, LANE, LANE), lambda b, p: (0, 0, p)),
                  pl.BlockSpec((8, LANE), lambda b, p: (0, p)),
                  pl.BlockSpec((1, 2, 1, 2 * LANE, LANE), lambda b, p: (b, 0, p, 0, 0))],
        out_specs=[pl.BlockSpec((seq_len, 2 * LANE), lambda b, p: (b, p)),
                   pl.BlockSpec((1, 2, 1, 2 * LANE, LANE), lambda b, p: (b, 0, p, 0, 0))],
        out_shape=[jax.ShapeDtypeStruct((n_seq * seq_len, GLA_V_W), F32),
                   jax.ShapeDtypeStruct((n_seq, 2, n_pairs, 2 * LANE, LANE), F32)],
        scratch_shapes=[pltpu.VMEM((seq_len, 2 * LANE), F32)] * 2
                       + [pltpu.VMEM((2, 2 * LANE, LANE), F32),
                          pltpu.VMEM((2, n_chunks, CHUNK, LANE), BF16),
                          pltpu.VMEM((2, n_chunks, CHUNK, 2 * LANE), F32),
                          pltpu.VMEM((2, n_chunks, 8, LANE), F32),
                          pltpu.VMEM((2, n_chunks, 2 * LANE, LANE), F32)],
        compiler_params=_cparams(2),
        name="gla_mixer",
    )(proj, proj, proj, proj, proj, prm['gk2p'], prm['gvec'], s0)
    return y, s_out


OUT_TM = 256
ROUTE_NEG = -1e30
LANE_GROUP0 = N_EXPERTS


def _route(logits):
    lane = _iota(logits.shape, 1)
    lane_f = lane.astype(F32)
    big = float(LANE)
    is_g = (lane >= LANE_GROUP0) & (lane < LANE_GROUP0 + 4)
    gmax = jnp.max(jnp.where(is_g, logits, ROUTE_NEG), axis=-1, keepdims=True)
    gidx = jnp.min(jnp.where(is_g & (logits == gmax), lane_f, big), axis=-1, keepdims=True) - LANE_GROUP0
    gsum = jnp.sum(jnp.where(is_g, jnp.exp(jnp.minimum(logits - gmax, 0.0)), 0.0), axis=-1, keepdims=True)
    g_w = 1.0 / gsum
    in_grp = (lane < N_EXPERTS) & ((lane // 4).astype(F32) == gidx)
    m1 = jnp.max(jnp.where(in_grp, logits, ROUTE_NEG), axis=-1, keepdims=True)
    i1 = jnp.min(jnp.where(in_grp & (logits == m1), lane_f, big), axis=-1, keepdims=True)
    rest = in_grp & (lane_f != i1)
    m2 = jnp.max(jnp.where(rest, logits, ROUTE_NEG), axis=-1, keepdims=True)
    i2 = jnp.min(jnp.where(rest & (logits == m2), lane_f, big), axis=-1, keepdims=True)
    t = jnp.exp(m2 - m1)
    w1 = g_w / (1.0 + t)
    return jnp.where(lane_f == i1, w1, 0.0) + jnp.where(lane_f == i2, w1 * t, 0.0)


def _outproj_kernel(yr_ref, yg_ref, x_ref, mod_ref, wo_ref, g_ref, wr_ref, br_ref, x1_ref, h2_ref, cmb_ref):
    m = mod_ref[0]
    mix = _mm(yr_ref[...], wo_ref[0:RW_W, :]) + _mm(yg_ref[...], wo_ref[RW_W:RW_W + GLA_V_W, :])
    x1 = x_ref[...] + m[2:3] * mix
    h2 = _rmsnorm_rows(x1) * g_ref[...] * (1.0 + m[4:5]) + m[3:4]
    x1_ref[...] = x1
    h2_ref[...] = h2.astype(BF16)
    cmb_ref[...] = _route(_mm3(h2, wr_ref[...]) + br_ref[...])


def _out_projection(y_rw, y_gla, x2d, mod, w_out, norm_g, w_route, b_route, seq_len, ctx_row):
    n = x2d.shape[0]
    tile = lambda w: pl.BlockSpec((OUT_TM, w), lambda i: (i, 0))
    full = lambda a: pl.BlockSpec(a.shape, lambda i: (0,) * a.ndim)
    return pl.pallas_call(
        _outproj_kernel,
        grid=(n // OUT_TM,),
        in_specs=[tile(RW_W), tile(GLA_V_W), tile(D_MODEL),
                  pl.BlockSpec((1, N_MOD, D_MODEL), _mod_index(seq_len, OUT_TM, ctx_row)),
                  full(w_out), full(norm_g), full(w_route), full(b_route)],
        out_specs=[tile(D_MODEL), tile(D_MODEL), tile(LANE)],
        out_shape=[jax.ShapeDtypeStruct((n, D_MODEL), F32), jax.ShapeDtypeStruct((n, D_MODEL), BF16),
                   jax.ShapeDtypeStruct((n, LANE), F32)],
        compiler_params=_cparams(1),
        name="out_proj_router",
    )(y_rw, y_gla, x2d, mod, w_out, norm_g, w_route, b_route)


MOE_TM = 512


def _moe_kernel(h2_ref, cmb_ref, x1_ref, mod_ref, w1_ref, w3_ref, w2_ref, fg_ref, y_ref, acc_ref):
    e = pl.program_id(1)

    @pl.when(e == 0)
    def _():
        acc_ref[...] = jnp.zeros_like(acc_ref)

    h2 = h2_ref[...]
    cmb = cmb_ref[...]
    wgt = jnp.sum(jnp.where(_iota(cmb.shape, 1) == e, cmb, 0.0), axis=-1, keepdims=True)
    up = _dot(h2, w1_ref[0])
    gate = _dot(h2, w3_ref[0])
    acc_ref[...] += _mm(_silu(gate) * up * wgt, w2_ref[0])

    @pl.when(e == N_EXPERTS - 1)
    def _():
        x2 = x1_ref[...] + mod_ref[0][5:6] * acc_ref[...]
        y_ref[...] = _rmsnorm_rows(x2) * fg_ref[...]


def _moe(h2, cmb, x1, mod, w1, w3, w2, final_g, seq_len, ctx_row):
    n = h2.shape[0]
    mod_idx = _mod_index(seq_len, MOE_TM, ctx_row)
    tile = lambda w: pl.BlockSpec((MOE_TM, w), lambda i, e: (i, 0))
    return pl.pallas_call(
        _moe_kernel,
        grid=(n // MOE_TM, N_EXPERTS),
        in_specs=[tile(D_MODEL), tile(LANE), tile(D_MODEL),
                  pl.BlockSpec((1, N_MOD, D_MODEL), lambda i, e: mod_idx(i)),
                  pl.BlockSpec((1, D_MODEL, D_EXPERT), lambda i, e: (e, 0, 0)),
                  pl.BlockSpec((1, D_MODEL, D_EXPERT), lambda i, e: (e, 0, 0)),
                  pl.BlockSpec((1, D_EXPERT, D_MODEL), lambda i, e: (e, 0, 0)),
                  pl.BlockSpec((1, D_MODEL), lambda i, e: (0, 0))],
        out_specs=tile(D_MODEL),
        out_shape=jax.ShapeDtypeStruct((n, D_MODEL), F32),
        scratch_shapes=[pltpu.VMEM((MOE_TM, D_MODEL), F32)],
        compiler_params=_cparams(2),
        name="moe_experts",
    )(h2, cmb, x1, mod, w1, w3, w2, final_g)


def _pad_rows(x, rows):
    return jnp.pad(x, ((0, rows - x.shape[0]),) + ((0, 0),) * (x.ndim - 1))


def _pack_params(l, w_in, rw_conv, rw_w0, rw_w2, rw_a0, rw_a2, rw_g2, rw_k_k, rw_k_a, rw_r_k, rw_ln_w, rw_ln_b,
                 gla_gk2, gla_gk_b, gla_norm_g, moe_w_group, moe_b_group, moe_w_expert, moe_b_expert):
    wi = w_in[l]
    z = lambda n: jnp.zeros((D_MODEL, n), F32)
    w_in_p = jnp.concatenate([wi[:, 0:1920], wi[:, 3456:3488], z(LANE - 32), wi[:, 1920:3456]], axis=1).astype(BF16)
    z64 = jnp.zeros((64, RW_W), F32)
    w2p = jnp.stack([jnp.concatenate([rw_w2[l, 0], z64], 0), jnp.concatenate([z64, rw_w2[l, 1]], 0)])
    a2p = jnp.stack([jnp.concatenate([rw_a2[l, 0], z64], 0), jnp.concatenate([z64, rw_a2[l, 1]], 0)])
    vec = _pad_rows(jnp.stack([rw_w0[l, 0], rw_w0[l, 1], rw_a0[l, 0], rw_a0[l, 1], rw_k_k[l], rw_k_a[l],
                               rw_r_k[l].reshape(RW_W), rw_ln_w[l], rw_ln_b[l]]), 16)
    rw = {'conv': rw_conv[l].reshape(9, 3 * RW_W), 'w2p': w2p, 'a2p': a2p, 'g2': rw_g2[l], 'vec': vec}
    gk2p = jnp.stack([_pad_rows(gla_gk2[l, 0], LANE),
                      _pad_rows(jnp.concatenate([jnp.zeros((16, GLA_QK_W), F32), gla_gk2[l, 1]], 0), LANE)])
    gvec = _pad_rows(jnp.stack([gla_gk_b[l, 0], gla_gk_b[l, 1], jnp.tile(gla_norm_g[l], 2)]), 8)
    gla = {'gk2p': gk2p, 'gvec': gvec}
    w_route = jnp.concatenate([moe_w_expert[l], moe_w_group[l], z(LANE - N_EXPERTS - 4)], axis=1)
    b_route = jnp.concatenate([moe_b_expert[l], moe_b_group[l], jnp.zeros((LANE - N_EXPERTS - 4,), F32)])[None]
    return w_in_p, rw, gla, w_route, b_route


def _pair_states_rwkv(s):
    b = s.shape[0]
    s = s.reshape(b, 2, 4, 2, 64, 64)
    z = jnp.zeros_like(s[:, :, :, 0])
    top = jnp.concatenate([s[:, :, :, 0], z], axis=-1)
    bot = jnp.concatenate([z, s[:, :, :, 1]], axis=-1)
    return jnp.concatenate([top, bot], axis=-2)


def _unpair_states_rwkv(s):
    b = s.shape[0]
    return jnp.stack([s[..., 0:64, 0:64], s[..., 64:128, 64:128]], axis=3).reshape(b, 2, 8, 64, 64)


def _pair_states_gla(s):
    b = s.shape[0]
    st = jnp.swapaxes(s, -1, -2).reshape(b, 2, 2, 2, 128, 64)
    z = jnp.zeros_like(st[:, :, :, 0])
    top = jnp.concatenate([st[:, :, :, 0], z], axis=-1)
    bot = jnp.concatenate([z, st[:, :, :, 1]], axis=-1)
    return jnp.concatenate([top, bot], axis=-2)


def _unpair_states_gla(s):
    b = s.shape[0]
    st = jnp.stack([s[..., 0:128, 0:64], s[..., 128:256, 64:128]], axis=3).reshape(b, 2, 4, 128, 64)
    return jnp.swapaxes(st, -1, -2)


def _trunk_pass(x, mod, ctx_row, s_rw0, s_gla0, is_grid, pk, norm1_g, norm2_g, w_out, w1, w3, w2, final_g):
    n_seq, seq_len, _ = x.shape
    x2d = x.reshape(n_seq * seq_len, D_MODEL)
    w_in_p, rw, gla, w_route, b_route = pk
    proj = _in_projection(x2d, mod, norm1_g, w_in_p, seq_len, ctx_row)
    y_rw, s_rw = _rwkv_mixer(proj, n_seq, seq_len, is_grid, rw, s_rw0)
    y_gla, s_gla = _gla_mixer(proj, n_seq, seq_len, gla, s_gla0)
    x1, h2, cmb = _out_projection(y_rw, y_gla, x2d, mod, w_out, norm2_g, w_route, b_route, seq_len, ctx_row)
    y = _moe(h2, cmb, x1, mod, w1, w3, w2, final_g, seq_len, ctx_row)
    return y.reshape(n_seq, seq_len, D_MODEL), s_rw, s_gla


def kernel(x_prompt, x_sample, state_rwkv, state_gla, c, c_ctx, norm1_g, norm2_g, w_ada, b_ada, w_in, w_out,
           rw_conv, rw_w0, rw_w2, rw_a0, rw_a2, rw_g2, rw_k_k, rw_k_a, rw_r_k, rw_ln_w, rw_ln_b,
           gla_gk2, gla_gk_b, gla_norm_g, moe_w_group, moe_b_group, moe_w_expert, moe_b_expert,
           moe_w1, moe_w3, moe_w2, final_g):
    depth = w_in.shape[0]
    assert depth == 1, "the packed layout below handles the single-layer trunk of this problem"
    l = 0
    n_dec = x_sample.shape[0]
    ctx_row = n_dec
    cond8 = _pad_rows(jnp.concatenate([c, c_ctx[None]], axis=0), 8)
    mod = _modulation(cond8, w_ada[l], b_ada[l][None]).reshape(8, N_MOD, D_MODEL)
    pk = _pack_params(l, w_in, rw_conv, rw_w0, rw_w2, rw_a0, rw_a2, rw_g2, rw_k_k, rw_k_a, rw_r_k, rw_ln_w,
                      rw_ln_b, gla_gk2, gla_gk_b, gla_norm_g, moe_w_group, moe_b_group, moe_w_expert,
                      moe_b_expert)
    shared = (pk, norm1_g[l][None], norm2_g[l][None], w_out[l].astype(BF16), moe_w1[l].astype(BF16),
              moe_w3[l].astype(BF16), moe_w2[l].astype(BF16), final_g[None])

    n_ctx = x_prompt.shape[0]
    zeros_rw = jnp.zeros((n_ctx, 2, RW_W // LANE, LANE, LANE), F32)
    zeros_gla = jnp.zeros((n_ctx, 2, GLA_QK_W // LANE, 2 * LANE, LANE), F32)
    y_prompt, s_rw, s_gla = _trunk_pass(x_prompt, mod, ctx_row, zeros_rw, zeros_gla, False, *shared)
    new_state_rwkv = _unpair_states_rwkv(s_rw)[:, None]
    new_state_gla = _unpair_states_gla(s_gla)[:, None]

    y_sample, _, _ = _trunk_pass(x_sample, mod, None, _pair_states_rwkv(state_rwkv[:, l]),
                                 _pair_states_gla(state_gla[:, l]), True, *shared)
    return (y_prompt, y_sample, new_state_rwkv, new_state_gla)
```

```python
import functools

import jax
import jax.numpy as jnp
from jax import lax
from jax.experimental import pallas as pl
from jax.experimental.pallas import tpu as pltpu

F32 = jnp.float32
BF16 = jnp.bfloat16

D_MODEL = 1024
RW_W = 512
GLA_V_W = 512
GLA_QK_W = 256
N_EXPERTS = 16
D_EXPERT = 256
N_MOD = 6
EPS = 1e-6
RW_LN_EPS = 64e-5
RW_DECAY_SCALE = 0.606531
GLA_GATE_NORM = 16.0
GLA_Q_SCALE = 64 ** -0.5
GRID_W = 64

LANE = 128
CHUNK = 64
CONV_PAD = 128
TERM_UNROLL = 4
D_PROJ = 28 * LANE
VMEM_LIMIT = 56 * 1024 * 1024

CB_R, CB_K, CB_V, CB_LORA, CB_LGK, CB_GQ, CB_GK, CB_GV, CB_OG = 0, 4, 8, 12, 15, 16, 18, 20, 24

_NN = (((1,), (0,)), ((), ()))
_NT = (((1,), (1,)), ((), ()))
_TN = (((0,), (0,)), ((), ()))


def _dot(a, b, dims=_NN):
    return lax.dot_general(a, b, dims, preferred_element_type=F32)


def _mm(a, b, dims=_NN):
    return _dot(a.astype(BF16), b.astype(BF16), dims)


def _split2(x):
    hi = x.astype(BF16)
    lo = (x - hi.astype(F32)).astype(BF16)
    return hi, lo


def _split3(x):
    hi = x.astype(BF16)
    r1 = x - hi.astype(F32)
    mid = r1.astype(BF16)
    lo = (r1 - mid.astype(F32)).astype(BF16)
    return hi, mid, lo


def _mm3(a, b, dims=_NN):
    ah, al = _split2(a)
    bh, bl = _split2(b)
    return _dot(ah, bh, dims) + _dot(ah, bl, dims) + _dot(al, bh, dims)


def _mm_exact_rhs(a, b01, dims=_NN):
    a1, a2, a3 = _split3(a)
    return _dot(a1, b01, dims) + _dot(a2, b01, dims) + _dot(a3, b01, dims)


def _mm_exact_lhs(a01, b, dims=_NN):
    b1, b2, b3 = _split3(b)
    return _dot(a01, b1, dims) + _dot(a01, b2, dims) + _dot(a01, b3, dims)


def _sigmoid(x):
    return 1.0 / (1.0 + jnp.exp(-x))


def _silu(x):
    return x * _sigmoid(x)


def _log_sigmoid(x):
    return jnp.minimum(x, 0.0) - jnp.log(1.0 + jnp.exp(-jnp.abs(x)))


def _iota(shape, dim):
    return lax.broadcasted_iota(jnp.int32, shape, dim)


def _cparams(n_axes):
    return pltpu.CompilerParams(dimension_semantics=("arbitrary",) * n_axes, vmem_limit_bytes=VMEM_LIMIT)


MOD_TN = 768


def _mod_kernel(c_ref, w_ref, b_ref, o_ref):
    o_ref[...] = _mm3(_silu(c_ref[...]), w_ref[...]) + b_ref[...]


def _modulation(cond8, w_ada, b_ada):
    n = w_ada.shape[1]
    return pl.pallas_call(
        _mod_kernel,
        grid=(n // MOD_TN,),
        in_specs=[pl.BlockSpec((8, D_MODEL), lambda j: (0, 0)),
                  pl.BlockSpec((D_MODEL, MOD_TN), lambda j: (0, j)),
                  pl.BlockSpec((1, MOD_TN), lambda j: (0, j))],
        out_specs=pl.BlockSpec((8, MOD_TN), lambda j: (0, j)),
        out_shape=jax.ShapeDtypeStruct((8, n), F32),
        compiler_params=_cparams(1),
        name="adaln_mod",
    )(cond8, w_ada, b_ada)


PROJ_TM = 256


def _rmsnorm_rows(x):
    return x * lax.rsqrt(jnp.mean(x * x, axis=-1, keepdims=True) + EPS)


def _inproj_kernel(x_ref, mod_ref, g_ref, w_ref, o_ref):
    m = mod_ref[0]
    h = _rmsnorm_rows(x_ref[...]) * g_ref[...] * (1.0 + m[1:2]) + m[0:1]
    o_ref[...] = _mm(h, w_ref[...])


def _mod_index(seq_len, tm, ctx_row):
    if ctx_row is not None:
        return lambda i: (ctx_row, 0, 0)
    return lambda i: (i // (seq_len // tm), 0, 0)


def _in_projection(x2d, mod, norm_g, w_in_p, seq_len, ctx_row):
    n = x2d.shape[0]
    return pl.pallas_call(
        _inproj_kernel,
        grid=(n // PROJ_TM,),
        in_specs=[pl.BlockSpec((PROJ_TM, D_MODEL), lambda i: (i, 0)),
                  pl.BlockSpec((1, N_MOD, D_MODEL), _mod_index(seq_len, PROJ_TM, ctx_row)),
                  pl.BlockSpec((1, D_MODEL), lambda i: (0, 0)),
                  pl.BlockSpec((D_MODEL, D_PROJ), lambda i: (0, 0))],
        out_specs=pl.BlockSpec((PROJ_TM, D_PROJ), lambda i: (i, 0)),
        out_shape=jax.ShapeDtypeStruct((n, D_PROJ), F32),
        compiler_params=_cparams(1),
        name="in_proj",
    )(x2d, mod, norm_g, w_in_p)


def _time_masks(reverse):
    r = _iota((2 * CHUNK, 2 * CHUNK), 0) % CHUNK
    c = _iota((2 * CHUNK, 2 * CHUNK), 1) % CHUNK
    if reverse:
        return r < c, r <= c
    return r > c, r >= c


def _cumsum_matrix(reverse):
    r = _iota((CHUNK, CHUNK), 0)
    c = _iota((CHUNK, CHUNK), 1)
    tri = (r <= c) if reverse else (r >= c)
    return tri.astype(BF16)


def _stack_heads(x, half):
    m0 = _iota(x.shape, 1) < half
    return jnp.concatenate([jnp.where(m0, x, 0.0), jnp.where(m0, 0.0, x)], axis=0)


def _block_ones():
    r = _iota((LANE, LANE), 0) // 64
    c = _iota((LANE, LANE), 1) // 64
    return (r == c).astype(BF16)


def _rwkv_chunk_terms(insts):
    n2 = 2 * CHUNK
    eye = (_iota((n2, n2), 0) == _iota((n2, n2), 1)).astype(F32)
    cums = [_mm_exact_lhs(_cumsum_matrix(rev), lw) for (_, lw, _, _, _, _, rev) in insts]
    pre = []
    for (r, lw, kd, a, b, v, rev), cum in zip(insts, cums):
        end = cum[0:1] if rev else cum[CHUNK - 1:CHUNK]
        inv_w = jnp.exp(-cum)
        rem_w = jnp.exp(end - cum)
        a_s = _stack_heads(a * jnp.exp(cum - lw), 64)
        r_s = _stack_heads(r * jnp.exp(cum), 64)
        bk = jnp.concatenate([_stack_heads(b * inv_w, 64), _stack_heads(kd * inv_w, 64)], axis=0).astype(BF16)
        bkh = jnp.concatenate([_stack_heads(b * rem_w, 64), _stack_heads(kd * rem_w, 64)], axis=0).astype(BF16)
        pre.append((a_s, r_s, bk, bkh, _stack_heads(v, 64), jnp.exp(end)))
    ms = [_dot(jnp.concatenate([a_s, r_s], axis=0).astype(BF16), bk, _NT) for (a_s, r_s, bk, _, _, _) in pre]
    mats = []
    for m, (_, _, _, _, _, _, rev) in zip(ms, insts):
        strict, incl = _time_masks(rev)
        l_ab = jnp.where(strict, m[0:n2, 0:n2], 0.0)
        l_akrk = jnp.concatenate([jnp.where(strict, m[0:n2, n2:2 * n2], 0.0),
                                  jnp.where(incl, m[n2:2 * n2, n2:2 * n2], 0.0)], axis=0).astype(BF16)
        m_rb = jnp.where(incl, m[n2:2 * n2, 0:n2], 0.0).astype(BF16)
        mats.append((l_ab, l_akrk, m_rb))
    lvs = [_dot(l_akrk, pr[4].astype(BF16)) for (_, l_akrk, _), pr in zip(mats, pre)]
    ps = [eye + l_ab for (l_ab, _, _) in mats]
    lps = [l_ab.astype(BF16) for (l_ab, _, _) in mats]
    for _ in range(5):
        lps = [_dot(lp, lp).astype(BF16) for lp in lps]
        ps = [p + _dot(p.astype(BF16), lp) for p, lp in zip(ps, lps)]
    pxs = [_mm(p, jnp.concatenate([pr[0], lv[0:n2]], axis=1)) for p, pr, lv in zip(ps, pre, lvs)]
    mzs = [_dot(mt[2], px.astype(BF16)) for mt, px in zip(mats, pxs)]
    out = []
    for pr, lv, px, mz in zip(pre, lvs, pxs, mzs):
        a_s, r_s, _, bkh, v_s, w_end = pr
        q2 = r_s + mz[:, 0:n2]
        y02 = mz[:, n2:2 * n2] + lv[n2:2 * n2]
        t = _dot(px[:, 0:n2].astype(BF16), bkh[0:n2], _TN)
        g = _dot(jnp.concatenate([px[:, n2:2 * n2], v_s], axis=0).astype(BF16), bkh, _TN)
        out.append((t.astype(BF16), g, w_end, (q2[0:CHUNK] + q2[CHUNK:n2]).astype(BF16),
                    y02[0:CHUNK] + y02[CHUNK:n2]))
    return out


def _gla_chunk_terms(insts):
    cums = [_mm_exact_lhs(_cumsum_matrix(rev), g) for (_, _, _, g, rev) in insts]
    pre = []
    for (q, k, v, g, rev), cum in zip(insts, cums):
        end = cum[0:1] if rev else cum[CHUNK - 1:CHUNK]
        qt = q * jnp.exp(cum)
        k_s = _stack_heads(k * jnp.exp(-cum), 64).astype(BF16)
        kh_s = _stack_heads(k * jnp.exp(end - cum), 64).astype(BF16)
        pre.append((qt.astype(BF16), _stack_heads(qt, 64).astype(BF16), k_s, kh_s,
                    _stack_heads(v, 128).astype(BF16), jnp.exp(end)))
    atts = [_dot(pr[1], pr[2], _NT) for pr in pre]
    atts = [jnp.where(_time_masks(inst[4])[1], att, 0.0).astype(BF16) for att, inst in zip(atts, insts)]
    o2s = [_dot(att, pr[4]) for att, pr in zip(atts, pre)]
    kvs = [_dot(pr[4], pr[3], _TN) for pr in pre]
    return [(pr[0], o2[0:CHUNK] + o2[CHUNK:2 * CHUNK], pr[5], kv) for pr, o2, kv in zip(pre, o2s, kvs)]


def _rwkv_kernel(seq_len, is_grid, r_ref, k_ref, v_ref, lora_ref, cwr_ref, cwk_ref, cwv_ref, w2_ref, a2_ref,
                 g2_ref, vec_ref, s0_ref, y_ref, sout_ref, pad_ref, bonus_ref, gate_ref, yf_ref,
                 yb_ref, st_ref, tt_ref, tg_ref, tw_ref, tq_ref, ty_ref):
    n_chunks = seq_len // CHUNK
    vec = vec_ref[...]
    w0 = (vec[0:1], vec[1:2])
    a0 = (vec[2:3], vec[3:4])
    k_k, k_a, r_k, ln_w, ln_b = vec[4:5], vec[5:6], vec[6:7], vec[7:8], vec[8:9]
    ones_blk = _block_ones()

    def block_sum(x):
        return _mm_exact_rhs(x, ones_blk)

    zeros = jnp.zeros((CONV_PAD, 3 * LANE), F32)
    pad_ref[0:CONV_PAD, :] = zeros
    pad_ref[CONV_PAD + seq_len:2 * CONV_PAD + seq_len, :] = zeros
    pad_ref[CONV_PAD:CONV_PAD + seq_len, 0:LANE] = r_ref[...]
    pad_ref[CONV_PAD:CONV_PAD + seq_len, LANE:2 * LANE] = k_ref[...]
    pad_ref[CONV_PAD:CONV_PAD + seq_len, 2 * LANE:3 * LANE] = v_ref[...]
    cw = jnp.concatenate([cwr_ref[...], cwk_ref[...], cwv_ref[...]], axis=1)
    col = _iota((CHUNK, 3 * LANE), 0)

    def conv_chunk(c):
        base = pl.multiple_of(CONV_PAD + c * CHUNK, CHUNK)
        acc = jnp.zeros((CHUNK, 3 * LANE), F32)
        for di in ((-1, 0, 1) if is_grid else (0,)):
            win = pad_ref[pl.ds(base + di * GRID_W - 8, CHUNK + 16), :]
            for dj in (-1, 0, 1):
                tap = (di + 1) * 3 + dj + 1
                term = win[8 + dj:8 + dj + CHUNK] * cw[tap:tap + 1]
                if is_grid and dj == -1:
                    term = jnp.where(col >= 1, term, 0.0)
                if is_grid and dj == 1:
                    term = jnp.where(col <= GRID_W - 2, term, 0.0)
                acc = acc + term
        return acc[:, 0:LANE], acc[:, LANE:2 * LANE], acc[:, 2 * LANE:3 * LANE]

    def terms_body(j, carry):
        cs = [j * TERM_UNROLL + u for u in range(TERM_UNROLL)]
        offs = [pl.multiple_of(c * CHUNK, CHUNK) for c in cs]
        rkv = [conv_chunk(c) for c in cs]
        loras = [lora_ref[pl.ds(off, CHUNK), :] for off in offs]
        kks = [kc * k_k for (_, kc, _) in rkv]
        kk_ss = [block_sum(kk * kk) for kk in kks]
        bon_ss = [block_sum(rc * kc * r_k) for (rc, kc, _) in rkv]
        gates = [_mm3(_sigmoid(lo[:, 2 * LANE:3 * LANE]), g2_ref[...]) for lo in loras]
        lws = [[_mm3(jnp.tanh(lo[:, 0:LANE]), w2_ref[d]) for d in range(2)] for lo in loras]
        ags = [[_mm3(lo[:, LANE:2 * LANE], a2_ref[d]) for d in range(2)] for lo in loras]
        insts, where = [], []
        for u in range(TERM_UNROLL):
            rc, kc, vc = rkv[u]
            kk = kks[u] * lax.rsqrt(kk_ss[u] + EPS)
            bonus_ref[pl.ds(offs[u], CHUNK), :] = bon_ss[u] * vc
            gate_ref[pl.ds(offs[u], CHUNK), :] = gates[u]
            for d in range(2):
                lw = -RW_DECAY_SCALE * _sigmoid(w0[d] + lws[u][d])
                ag = _sigmoid(a0[d] + ags[u][d])
                kd = kc * (1.0 + (ag - 1.0) * k_a)
                insts.append((rc, lw, kd, -kk, kk * ag, vc, d == 1))
                where.append((d, cs[u]))
        for (d, c), (t, g, w_end, q, y0) in zip(where, _rwkv_chunk_terms(insts)):
            tt_ref[d, c] = t
            tg_ref[d, c] = g
            tw_ref[d, c] = jnp.broadcast_to(w_end, (8, LANE))
            tq_ref[d, c] = q
            ty_ref[d, c] = y0
        return carry

    lax.fori_loop(0, n_chunks // TERM_UNROLL, terms_body, 0)

    st_ref[...] = s0_ref[0, :, 0]

    def scan_body(i, carry):
        cs = (i, n_chunks - 1 - i)
        ss = [st_ref[d] for d in range(2)]
        sb = [s.astype(BF16) for s in ss]
        ys = [_dot(tq_ref[d, cs[d]], sb[d], _NT) + ty_ref[d, cs[d]] for d in range(2)]
        sn = [ss[d] * tw_ref[d, cs[d]][0:1] + _dot(sb[d], tt_ref[d, cs[d]]) + tg_ref[d, cs[d]] for d in range(2)]
        for d, out_ref in enumerate((yf_ref, yb_ref)):
            st_ref[d] = sn[d]
            out_ref[pl.ds(pl.multiple_of(cs[d] * CHUNK, CHUNK), CHUNK), :] = ys[d]
        return carry

    lax.fori_loop(0, n_chunks, scan_body, 0)
    sout_ref[0, :, 0] = st_ref[...]

    def post_body(j, carry):
        offs = [pl.multiple_of((j * TERM_UNROLL + u) * CHUNK, CHUNK) for u in range(TERM_UNROLL)]
        ys = [yf_ref[pl.ds(off, CHUNK), :] + yb_ref[pl.ds(off, CHUNK), :] for off in offs]
        mus = [block_sum(y) * (1.0 / 64) for y in ys]
        dlts = [y - mu for y, mu in zip(ys, mus)]
        vrs = [block_sum(dlt * dlt) * (1.0 / 64) for dlt in dlts]
        for off, dlt, var in zip(offs, dlts, vrs):
            yn = dlt * lax.rsqrt(var + RW_LN_EPS) * ln_w + ln_b
            y_ref[pl.ds(off, CHUNK), :] = (yn + bonus_ref[pl.ds(off, CHUNK), :]) * gate_ref[pl.ds(off, CHUNK), :]
        return carry

    lax.fori_loop(0, n_chunks // TERM_UNROLL, post_body, 0)


def _rwkv_mixer(proj, n_seq, seq_len, is_grid, prm, s0):
    n_pairs = RW_W // LANE
    n_chunks = seq_len // CHUNK
    assert n_chunks % TERM_UNROLL == 0
    col = lambda cb: (lambda b, p: (b, cb + p))
    fixed2 = lambda b, p: (0, p)
    kernel = functools.partial(_rwkv_kernel, seq_len, is_grid)
    y, s_out = pl.pallas_call(
        kernel,
        grid=(n_seq, n_pairs),
        in_specs=[pl.BlockSpec((seq_len, LANE), col(CB_R)),
                  pl.BlockSpec((seq_len, LANE), col(CB_K)),
                  pl.BlockSpec((seq_len, LANE), col(CB_V)),
                  pl.BlockSpec((seq_len, 3 * LANE), lambda b, p: (b, CB_LORA // 3)),
                  pl.BlockSpec((9, LANE), lambda b, p: (0, CB_R + p)),
                  pl.BlockSpec((9, LANE), lambda b, p: (0, CB_K + p)),
                  pl.BlockSpec((9, LANE), lambda b, p: (0, CB_V + p)),
                  pl.BlockSpec((2, LANE, LANE), lambda b, p: (0, 0, p)),
                  pl.BlockSpec((2, LANE, LANE), lambda b, p: (0, 0, p)),
                  pl.BlockSpec((LANE, LANE), fixed2),
                  pl.BlockSpec((16, LANE), fixed2),
                  pl.BlockSpec((1, 2, 1, LANE, LANE), lambda b, p: (b, 0, p, 0, 0))],
        out_specs=[pl.BlockSpec((seq_len, LANE), lambda b, p: (b, p)),
                   pl.BlockSpec((1, 2, 1, LANE, LANE), lambda b, p: (b, 0, p, 0, 0))],
        out_shape=[jax.ShapeDtypeStruct((n_seq * seq_len, RW_W), F32),
                   jax.ShapeDtypeStruct((n_seq, 2, n_pairs, LANE, LANE), F32)],
        scratch_shapes=[pltpu.VMEM((seq_len + 2 * CONV_PAD, 3 * LANE), F32)]
                       + [pltpu.VMEM((seq_len, LANE), F32)] * 4
                       + [pltpu.VMEM((2, LANE, LANE), F32),
                          pltpu.VMEM((2, n_chunks, LANE, LANE), BF16),
                          pltpu.VMEM((2, n_chunks, LANE, LANE), F32),
                          pltpu.VMEM((2, n_chunks, 8, LANE), F32),
                          pltpu.VMEM((2, n_chunks, CHUNK, LANE), BF16),
                          pltpu.VMEM((2, n_chunks, CHUNK, LANE), F32)],
        compiler_params=_cparams(2),
        name="rwkv_mixer",
    )(proj, proj, proj, proj, prm['conv'], prm['conv'], prm['conv'], prm['w2p'], prm['a2p'], prm['g2'],
      prm['vec'], s0)
    return y, s_out


def _gla_kernel(seq_len, q_ref, k_ref, v_ref, og_ref, lgk_ref, gk2_ref, gvec_ref, s0_ref, y_ref, sout_ref,
                of_ref, ob_ref, st_ref, tq_ref, to_ref, ta_ref, tkv_ref):
    n_chunks = seq_len // CHUNK
    gvec = gvec_ref[...]

    def terms_body(j, carry):
        insts, where = [], []
        for u in range(TERM_UNROLL):
            c = j * TERM_UNROLL + u
            off = pl.multiple_of(c * CHUNK, CHUNK)
            lgk = lgk_ref[pl.ds(off, CHUNK), :]
            qc = q_ref[pl.ds(off, CHUNK), :] * GLA_Q_SCALE
            kc = k_ref[pl.ds(off, CHUNK), :]
            vc = v_ref[pl.ds(off, CHUNK), :]
            for d in range(2):
                g = _log_sigmoid(_mm3(lgk, gk2_ref[d]) + gvec[d:d + 1]) * (1.0 / GLA_GATE_NORM)
                insts.append((qc, kc, vc, g, d == 1))
                where.append((d, c))
        for (d, c), (qt, o0, a_end, kv) in zip(where, _gla_chunk_terms(insts)):
            tq_ref[d, c] = qt
            to_ref[d, c] = o0
            ta_ref[d, c] = jnp.broadcast_to(a_end, (8, LANE))
            tkv_ref[d, c] = kv
        return carry

    lax.fori_loop(0, n_chunks // TERM_UNROLL, terms_body, 0)

    st_ref[...] = s0_ref[0, :, 0]

    def scan_body(i, carry):
        cs = (i, n_chunks - 1 - i)
        ss = [st_ref[d] for d in range(2)]
        os_ = [_dot(tq_ref[d, cs[d]], ss[d].astype(BF16), _NT) + to_ref[d, cs[d]] for d in range(2)]
        for d, out_ref in enumerate((of_ref, ob_ref)):
            st_ref[d] = ss[d] * ta_ref[d, cs[d]][0:1] + tkv_ref[d, cs[d]]
            out_ref[pl.ds(pl.multiple_of(cs[d] * CHUNK, CHUNK), CHUNK), :] = os_[d]
        return carry

    lax.fori_loop(0, n_chunks, scan_body, 0)
    sout_ref[0, :, 0] = st_ref[...]
    norm_g = gvec[2:3, 0:LANE]

    def post_body(c, carry):
        off = pl.multiple_of(c * CHUNK, CHUNK)
        for h in range(2):
            o = of_ref[pl.ds(off, CHUNK), h * LANE:(h + 1) * LANE] + ob_ref[pl.ds(off, CHUNK), h * LANE:(h + 1) * LANE]
            gate = _silu(og_ref[pl.ds(off, CHUNK), h * LANE:(h + 1) * LANE])
            y_ref[pl.ds(off, CHUNK), h * LANE:(h + 1) * LANE] = _rmsnorm_rows(o) * norm_g * gate
        return carry

    lax.fori_loop(0, n_chunks, post_body, 0)


def _gla_mixer(proj, n_seq, seq_len, prm, s0):
    n_pairs = GLA_QK_W // LANE
    n_chunks = seq_len // CHUNK
    assert n_chunks % TERM_UNROLL == 0
    kernel = functools.partial(_gla_kernel, seq_len)
    y, s_out = pl.pallas_call(
        kernel,
        grid=(n_seq, n_pairs),
        in_specs=[pl.BlockSpec((seq_len, LANE), lambda b, p: (b, CB_GQ + p)),
                  pl.BlockSpec((seq_len, LANE), lambda b, p: (b, CB_GK + p)),
                  pl.BlockSpec((seq_len, 2 * LANE), lambda b, p: (b, CB_GV // 2 + p)),
                  pl.BlockSpec((seq_len, 2 * LANE), lambda b, p: (b, CB_OG // 2 + p)),
                  pl.BlockSpec((seq_len, LANE), lambda b, p: (b, CB_LGK)),
                  pl.BlockSpec((2, LANE, LANE), lambda b, p: (0, 0, p)),
                  pl.BlockSpec((8, LANE), lambda b, p: (0, p)),
                  pl.BlockSpec((1, 2, 1, 2 * LANE, LANE), lambda b, p: (b, 0, p, 0, 0))],
        out_specs=[pl.BlockSpec((seq_len, 2 * LANE), lambda b, p: (b, p)),
                   pl.BlockSpec((1, 2, 1, 2 * LANE, LANE), lambda b, p: (b, 0, p, 0, 0))],
        out_shape=[jax.ShapeDtypeStruct((n_seq * seq_len, GLA_V_W), F32),
                   jax.ShapeDtypeStruct((n_seq, 2, n_pairs, 2 * LANE, LANE), F32)],
        scratch_shapes=[pltpu.VMEM((seq_len, 2 * LANE), F32)] * 2
                       + [pltpu.VMEM((2, 2 * LANE, LANE), F32),
                          pltpu.VMEM((2, n_chunks, CHUNK, LANE), BF16),
                          pltpu.VMEM((2, n_chunks, CHUNK, 2 * LANE), F32),
                          pltpu.VMEM((2, n_chunks, 8, LANE), F32),
                          pltpu.VMEM((2, n_chunks, 2 * LANE, LANE), F32)],
        compiler_params=_cparams(2),
        name="gla_mixer",
    )(proj, proj, proj, proj, proj, prm['gk2p'], prm['gvec'], s0)
    return y, s_out


OUT_TM = 256
ROUTE_NEG = -1e30
LANE_GROUP0 = N_EXPERTS


def _route(logits):
    lane = _iota(logits.shape, 1)
    lane_f = lane.astype(F32)
    big = float(LANE)
    is_g = (lane >= LANE_GROUP0) & (lane < LANE_GROUP0 + 4)
    gmax = jnp.max(jnp.where(is_g, logits, ROUTE_NEG), axis=-1, keepdims=True)
    gidx = jnp.min(jnp.where(is_g & (logits == gmax), lane_f, big), axis=-1, keepdims=True) - LANE_GROUP0
    gsum = jnp.sum(jnp.where(is_g, jnp.exp(jnp.minimum(logits - gmax, 0.0)), 0.0), axis=-1, keepdims=True)
    g_w = 1.0 / gsum
    in_grp = (lane < N_EXPERTS) & ((lane // 4).astype(F32) == gidx)
    m1 = jnp.max(jnp.where(in_grp, logits, ROUTE_NEG), axis=-1, keepdims=True)
    i1 = jnp.min(jnp.where(in_grp & (logits == m1), lane_f, big), axis=-1, keepdims=True)
    rest = in_grp & (lane_f != i1)
    m2 = jnp.max(jnp.where(rest, logits, ROUTE_NEG), axis=-1, keepdims=True)
    i2 = jnp.min(jnp.where(rest & (logits == m2), lane_f, big), axis=-1, keepdims=True)
    t = jnp.exp(m2 - m1)
    w1 = g_w / (1.0 + t)
    return jnp.where(lane_f == i1, w1, 0.0) + jnp.where(lane_f == i2, w1 * t, 0.0)


def _outproj_kernel(yr_ref, yg_ref, x_ref, mod_ref, wo_ref, g_ref, wr_ref, br_ref, x1_ref, h2_ref, cmb_ref):
    m = mod_ref[0]
    mix = _mm(yr_ref[...], wo_ref[0:RW_W, :]) + _mm(yg_ref[...], wo_ref[RW_W:RW_W + GLA_V_W, :])
    x1 = x_ref[...] + m[2:3] * mix
    h2 = _rmsnorm_rows(x1) * g_ref[...] * (1.0 + m[4:5]) + m[3:4]
    x1_ref[...] = x1
    h2_ref[...] = h2.astype(BF16)
    cmb_ref[...] = _route(_mm3(h2, wr_ref[...]) + br_ref[...])


def _out_projection(y_rw, y_gla, x2d, mod, w_out, norm_g, w_route, b_route, seq_len, ctx_row):
    n = x2d.shape[0]
    tile = lambda w: pl.BlockSpec((OUT_TM, w), lambda i: (i, 0))
    full = lambda a: pl.BlockSpec(a.shape, lambda i: (0,) * a.ndim)
    return pl.pallas_call(
        _outproj_kernel,
        grid=(n // OUT_TM,),
        in_specs=[tile(RW_W), tile(GLA_V_W), tile(D_MODEL),
                  pl.BlockSpec((1, N_MOD, D_MODEL), _mod_index(seq_len, OUT_TM, ctx_row)),
                  full(w_out), full(norm_g), full(w_route), full(b_route)],
        out_specs=[tile(D_MODEL), tile(D_MODEL), tile(LANE)],
        out_shape=[jax.ShapeDtypeStruct((n, D_MODEL), F32), jax.ShapeDtypeStruct((n, D_MODEL), BF16),
                   jax.ShapeDtypeStruct((n, LANE), F32)],
        compiler_params=_cparams(1),
        name="out_proj_router",
    )(y_rw, y_gla, x2d, mod, w_out, norm_g, w_route, b_route)


MOE_TM = 512
MOE_RB = 128
MOE_INTERLEAVE = 4
SLOT_ALIGN = 16
MOE_SLOTS = 2 * MOE_TM + N_EXPERTS * SLOT_ALIGN + MOE_RB


def _moe_kernel(h2_ref, cmb_ref, x1_ref, mod_ref, w1_ref, w3_ref, w2_ref, fg_ref, y_ref, xs_ref, ys_ref):
    cmb = cmb_ref[...]
    lane = _iota(cmb.shape, 1).astype(F32)
    sel = cmb > 0.0
    sel01 = jnp.where(sel, 1.0, 0.0).astype(BF16)
    before = (_iota((MOE_TM, MOE_TM), 0) > _iota((MOE_TM, MOE_TM), 1)).astype(BF16)
    pos = _dot(before, sel01)
    cnt = pos[MOE_TM - 1:MOE_TM] + sel01[MOE_TM - 1:MOE_TM].astype(F32)
    seg = jnp.floor((cnt + (SLOT_ALIGN - 1)) * (1.0 / SLOT_ALIGN))
    lower_experts = (_iota((LANE, LANE), 0) < _iota((LANE, LANE), 1)).astype(BF16)
    start = _dot(jnp.broadcast_to(seg, (8, LANE)).astype(BF16), lower_experts)[0:1] * SLOT_ALIGN
    n_blk = jnp.floor((cnt + (MOE_RB - 1)) * (1.0 / MOE_RB)).astype(jnp.int32)
    start_i = start.astype(jnp.int32)
    cnt_i = cnt.astype(jnp.int32)
    slot = start + pos
    e_a = jnp.min(jnp.where(sel, lane, float(LANE)), axis=-1, keepdims=True)
    e_b = jnp.max(jnp.where(sel, lane, -1.0), axis=-1, keepdims=True)
    pick = lambda e, x: jnp.sum(jnp.where(lane == e, x, 0.0), axis=-1, keepdims=True)
    slot_a, w_a = pick(e_a, slot), pick(e_a, cmb)
    slot_b = jnp.where(e_b != e_a, pick(e_b, slot), -1.0)
    w_b = pick(e_b, cmb)

    slots_t = jnp.where(lane == 0.0, slot_a, jnp.where(lane == 1.0, slot_b, -1.0)).T
    row_slot = _iota((MOE_SLOTS, MOE_TM), 0).astype(F32)
    gather = jnp.where((row_slot == slots_t[0:1]) | (row_slot == slots_t[1:2]), 1.0, 0.0).astype(BF16)
    xs_ref[...] = _dot(gather, h2_ref[...]).astype(BF16)
    ys_ref[...] = jnp.zeros_like(ys_ref)

    row_in_blk = _iota((MOE_RB, D_MODEL), 0)

    def expert_blocks(experts, r0s, ends):
        xbs = [xs_ref[pl.ds(r0, MOE_RB), :] for r0 in r0s]
        gates = [_dot(xb, w3_ref[e]) for xb, e in zip(xbs, experts)]
        ups = [_dot(xb, w1_ref[e]) for xb, e in zip(xbs, experts)]
        acts = [(_silu(g) * u).astype(BF16) for g, u in zip(gates, ups)]
        outs = [_dot(a, w2_ref[e]) for a, e in zip(acts, experts)]
        for r0, end, out in zip(r0s, ends, outs):
            keep = row_in_blk + r0 >= end
            ys_ref[pl.ds(r0, MOE_RB), :] = jnp.where(keep, ys_ref[pl.ds(r0, MOE_RB), :], out.astype(BF16))

    seg_start = [pl.multiple_of(start_i[0, e], SLOT_ALIGN) for e in range(N_EXPERTS)]
    seg_end = [seg_start[e] + cnt_i[0, e] for e in range(N_EXPERTS)]
    for e0 in range(0, N_EXPERTS, MOE_INTERLEAVE):
        es = list(range(e0, e0 + MOE_INTERLEAVE))
        expert_blocks(es, [seg_start[e] for e in es], [seg_end[e] for e in es])
    for e in range(N_EXPERTS):
        def extra_block(b, carry, e=e):
            expert_blocks([e], [pl.multiple_of(seg_start[e] + b * MOE_RB, SLOT_ALIGN)], [seg_end[e]])
            return carry

        lax.fori_loop(1, n_blk[0, e], extra_block, 0)

    col_slot = _iota((MOE_TM, MOE_SLOTS), 1).astype(F32)
    scatter = (jnp.where(col_slot == slot_a, w_a, 0.0) + jnp.where(col_slot == slot_b, w_b, 0.0)).astype(BF16)
    x2 = x1_ref[...] + mod_ref[0][5:6] * _dot(scatter, ys_ref[...])
    y_ref[...] = _rmsnorm_rows(x2) * fg_ref[...]


def _moe(h2, cmb, x1, mod, w1, w3, w2, final_g, seq_len, ctx_row):
    n = h2.shape[0]
    tile = lambda w: pl.BlockSpec((MOE_TM, w), lambda i: (i, 0))
    resident = lambda a: pl.BlockSpec(a.shape, lambda i: (0,) * a.ndim, pipeline_mode=pl.Buffered(1))
    return pl.pallas_call(
        _moe_kernel,
        grid=(n // MOE_TM,),
        in_specs=[tile(D_MODEL), tile(LANE), tile(D_MODEL),
                  pl.BlockSpec((1, N_MOD, D_MODEL), _mod_index(seq_len, MOE_TM, ctx_row)),
                  resident(w1), resident(w3), resident(w2),
                  pl.BlockSpec((1, D_MODEL), lambda i: (0, 0))],
        out_specs=tile(D_MODEL),
        out_shape=jax.ShapeDtypeStruct((n, D_MODEL), F32),
        scratch_shapes=[pltpu.VMEM((MOE_SLOTS, D_MODEL), BF16), pltpu.VMEM((MOE_SLOTS, D_MODEL), BF16)],
        compiler_params=_cparams(1),
        name="moe_experts",
    )(h2, cmb, x1, mod, w1, w3, w2, final_g)


def _pad_rows(x, rows):
    return jnp.pad(x, ((0, rows - x.shape[0]),) + ((0, 0),) * (x.ndim - 1))


def _pack_params(l, w_in, rw_conv, rw_w0, rw_w2, rw_a0, rw_a2, rw_g2, rw_k_k, rw_k_a, rw_r_k, rw_ln_w, rw_ln_b,
                 gla_gk2, gla_gk_b, gla_norm_g, moe_w_group, moe_b_group, moe_w_expert, moe_b_expert):
    wi = w_in[l]
    z = lambda n: jnp.zeros((D_MODEL, n), F32)
    w_in_p = jnp.concatenate([wi[:, 0:1920], wi[:, 3456:3488], z(LANE - 32), wi[:, 1920:3456]], axis=1).astype(BF16)
    z64 = jnp.zeros((64, RW_W), F32)
    w2p = jnp.stack([jnp.concatenate([rw_w2[l, 0], z64], 0), jnp.concatenate([z64, rw_w2[l, 1]], 0)])
    a2p = jnp.stack([jnp.concatenate([rw_a2[l, 0], z64], 0), jnp.concatenate([z64, rw_a2[l, 1]], 0)])
    vec = _pad_rows(jnp.stack([rw_w0[l, 0], rw_w0[l, 1], rw_a0[l, 0], rw_a0[l, 1], rw_k_k[l], rw_k_a[l],
                               rw_r_k[l].reshape(RW_W), rw_ln_w[l], rw_ln_b[l]]), 16)
    rw = {'conv': rw_conv[l].reshape(9, 3 * RW_W), 'w2p': w2p, 'a2p': a2p, 'g2': rw_g2[l], 'vec': vec}
    gk2p = jnp.stack([_pad_rows(gla_gk2[l, 0], LANE),
                      _pad_rows(jnp.concatenate([jnp.zeros((16, GLA_QK_W), F32), gla_gk2[l, 1]], 0), LANE)])
    gvec = _pad_rows(jnp.stack([gla_gk_b[l, 0], gla_gk_b[l, 1], jnp.tile(gla_norm_g[l], 2)]), 8)
    gla = {'gk2p': gk2p, 'gvec': gvec}
    w_route = jnp.concatenate([moe_w_expert[l], moe_w_group[l], z(LANE - N_EXPERTS - 4)], axis=1)
    b_route = jnp.concatenate([moe_b_expert[l], moe_b_group[l], jnp.zeros((LANE - N_EXPERTS - 4,), F32)])[None]
    return w_in_p, rw, gla, w_route, b_route


def _pair_states_rwkv(s):
    b = s.shape[0]
    s = s.reshape(b, 2, 4, 2, 64, 64)
    z = jnp.zeros_like(s[:, :, :, 0])
    top = jnp.concatenate([s[:, :, :, 0], z], axis=-1)
    bot = jnp.concatenate([z, s[:, :, :, 1]], axis=-1)
    return jnp.concatenate([top, bot], axis=-2)


def _unpair_states_rwkv(s):
    b = s.shape[0]
    return jnp.stack([s[..., 0:64, 0:64], s[..., 64:128, 64:128]], axis=3).reshape(b, 2, 8, 64, 64)


def _pair_states_gla(s):
    b = s.shape[0]
    st = jnp.swapaxes(s, -1, -2).reshape(b, 2, 2, 2, 128, 64)
    z = jnp.zeros_like(st[:, :, :, 0])
    top = jnp.concatenate([st[:, :, :, 0], z], axis=-1)
    bot = jnp.concatenate([z, st[:, :, :, 1]], axis=-1)
    return jnp.concatenate([top, bot], axis=-2)


def _unpair_states_gla(s):
    b = s.shape[0]
    st = jnp.stack([s[..., 0:128, 0:64], s[..., 128:256, 64:128]], axis=3).reshape(b, 2, 4, 128, 64)
    return jnp.swapaxes(st, -1, -2)


def _trunk_pass(x, mod, ctx_row, s_rw0, s_gla0, is_grid, pk, norm1_g, norm2_g, w_out, w1, w3, w2, final_g):
    n_seq, seq_len, _ = x.shape
    x2d = x.reshape(n_seq * seq_len, D_MODEL)
    w_in_p, rw, gla, w_route, b_route = pk
    proj = _in_projection(x2d, mod, norm1_g, w_in_p, seq_len, ctx_row)
    y_rw, s_rw = _rwkv_mixer(proj, n_seq, seq_len, is_grid, rw, s_rw0)
    y_gla, s_gla = _gla_mixer(proj, n_seq, seq_len, gla, s_gla0)
    x1, h2, cmb = _out_projection(y_rw, y_gla, x2d, mod, w_out, norm2_g, w_route, b_route, seq_len, ctx_row)
    y = _moe(h2, cmb, x1, mod, w1, w3, w2, final_g, seq_len, ctx_row)
    return y.reshape(n_seq, seq_len, D_MODEL), s_rw, s_gla


def kernel(x_prompt, x_sample, state_rwkv, state_gla, c, c_ctx, norm1_g, norm2_g, w_ada, b_ada, w_in, w_out,
           rw_conv, rw_w0, rw_w2, rw_a0, rw_a2, rw_g2, rw_k_k, rw_k_a, rw_r_k, rw_ln_w, rw_ln_b,
           gla_gk2, gla_gk_b, gla_norm_g, moe_w_group, moe_b_group, moe_w_expert, moe_b_expert,
           moe_w1, moe_w3, moe_w2, final_g):
    depth = w_in.shape[0]
    assert depth == 1, "the packed layout below handles the single-layer trunk of this problem"
    l = 0
    n_dec = x_sample.shape[0]
    ctx_row = n_dec
    cond8 = _pad_rows(jnp.concatenate([c, c_ctx[None]], axis=0), 8)
    mod = _modulation(cond8, w_ada[l], b_ada[l][None]).reshape(8, N_MOD, D_MODEL)
    pk = _pack_params(l, w_in, rw_conv, rw_w0, rw_w2, rw_a0, rw_a2, rw_g2, rw_k_k, rw_k_a, rw_r_k, rw_ln_w,
                      rw_ln_b, gla_gk2, gla_gk_b, gla_norm_g, moe_w_group, moe_b_group, moe_w_expert,
                      moe_b_expert)
    shared = (pk, norm1_g[l][None], norm2_g[l][None], w_out[l].astype(BF16), moe_w1[l].astype(BF16),
              moe_w3[l].astype(BF16), moe_w2[l].astype(BF16), final_g[None])

    n_ctx = x_prompt.shape[0]
    zeros_rw = jnp.zeros((n_ctx, 2, RW_W // LANE, LANE, LANE), F32)
    zeros_gla = jnp.zeros((n_ctx, 2, GLA_QK_W // LANE, 2 * LANE, LANE), F32)
    y_prompt, s_rw, s_gla = _trunk_pass(x_prompt, mod, ctx_row, zeros_rw, zeros_gla, False, *shared)
    new_state_rwkv = _unpair_states_rwkv(s_rw)[:, None]
    new_state_gla = _unpair_states_gla(s_gla)[:, None]

    y_sample, _, _ = _trunk_pass(x_sample, mod, None, _pair_states_rwkv(state_rwkv[:, l]),
                                 _pair_states_gla(state_gla[:, l]), True, *shared)
    return (y_prompt, y_sample, new_state_rwkv, new_state_gla)
```

```python
import functools

import jax
import jax.numpy as jnp
from jax import lax
from jax.experimental import pallas as pl
from jax.experimental.pallas import tpu as pltpu

F32 = jnp.float32
BF16 = jnp.bfloat16

D_MODEL = 1024
RW_W = 512
GLA_V_W = 512
GLA_QK_W = 256
N_EXPERTS = 16
D_EXPERT = 256
N_MOD = 6
EPS = 1e-6
RW_LN_EPS = 64e-5
RW_DECAY_SCALE = 0.606531
GLA_GATE_NORM = 16.0
GLA_Q_SCALE = 64 ** -0.5
GRID_W = 64

LANE = 128
CHUNK = 64
CONV_PAD = 128
TERM_UNROLL = 4
RW_PAIRS_PER_STEP = 2
D_PROJ = 28 * LANE
VMEM_LIMIT = 56 * 1024 * 1024

CB_R, CB_K, CB_V, CB_LORA, CB_LGK, CB_GQ, CB_GK, CB_GV, CB_OG = 0, 4, 8, 12, 15, 16, 18, 20, 24

_NN = (((1,), (0,)), ((), ()))
_NT = (((1,), (1,)), ((), ()))
_TN = (((0,), (0,)), ((), ()))


def _dot(a, b, dims=_NN):
    return lax.dot_general(a, b, dims, preferred_element_type=F32)


def _mm(a, b, dims=_NN):
    return _dot(a.astype(BF16), b.astype(BF16), dims)


def _split2(x):
    hi = x.astype(BF16)
    lo = (x - hi.astype(F32)).astype(BF16)
    return hi, lo


def _mm3(a, b, dims=_NN):
    ah, al = _split2(a)
    bh, bl = _split2(b)
    return _dot(ah, bh, dims) + _dot(ah, bl, dims) + _dot(al, bh, dims)


def _mm3_split(a_split, b_split, dims=_NN):
    (ah, al), (bh, bl) = a_split, b_split
    return _dot(ah, bh, dims) + _dot(ah, bl, dims) + _dot(al, bh, dims)


def _split_param(w):
    return jnp.stack(_split2(w))


def _mm_01_lhs(a01, b, dims=_NN):
    b1, b2 = _split2(b)
    return _dot(a01, b1, dims) + _dot(a01, b2, dims)


def _sigmoid(x):
    return 1.0 / (1.0 + jnp.exp(-x))


def _silu(x):
    return x * _sigmoid(x)


def _log_sigmoid(x):
    return jnp.minimum(x, 0.0) - jnp.log(1.0 + jnp.exp(-jnp.abs(x)))


def _iota(shape, dim):
    return lax.broadcasted_iota(jnp.int32, shape, dim)


def _cparams(n_axes):
    return pltpu.CompilerParams(dimension_semantics=("arbitrary",) * n_axes, vmem_limit_bytes=VMEM_LIMIT)


MOD_TN = 768


def _mod_kernel(c_ref, w_ref, b_ref, o_ref):
    o_ref[...] = _mm3(_silu(c_ref[...]), w_ref[...]) + b_ref[...]


def _modulation(cond8, w_ada, b_ada):
    n = w_ada.shape[1]
    return pl.pallas_call(
        _mod_kernel,
        grid=(n // MOD_TN,),
        in_specs=[pl.BlockSpec((8, D_MODEL), lambda j: (0, 0)),
                  pl.BlockSpec((D_MODEL, MOD_TN), lambda j: (0, j)),
                  pl.BlockSpec((1, MOD_TN), lambda j: (0, j))],
        out_specs=pl.BlockSpec((8, MOD_TN), lambda j: (0, j)),
        out_shape=jax.ShapeDtypeStruct((8, n), F32),
        compiler_params=_cparams(1),
        name="adaln_mod",
    )(cond8, w_ada, b_ada)


PROJ_TM = 256


def _rmsnorm_rows(x):
    return x * lax.rsqrt(jnp.mean(x * x, axis=-1, keepdims=True) + EPS)


def _inproj_kernel(x_ref, mod_ref, g_ref, w_ref, o_ref):
    m = mod_ref[0]
    h = _rmsnorm_rows(x_ref[...]) * g_ref[...] * (1.0 + m[1:2]) + m[0:1]
    o_ref[...] = _mm(h, w_ref[...])


def _mod_index(seq_len, tm, ctx_row):
    if ctx_row is not None:
        return lambda i: (ctx_row, 0, 0)
    return lambda i: (i // (seq_len // tm), 0, 0)


def _in_projection(x2d, mod, norm_g, w_in_p, seq_len, ctx_row):
    n = x2d.shape[0]
    return pl.pallas_call(
        _inproj_kernel,
        grid=(n // PROJ_TM,),
        in_specs=[pl.BlockSpec((PROJ_TM, D_MODEL), lambda i: (i, 0)),
                  pl.BlockSpec((1, N_MOD, D_MODEL), _mod_index(seq_len, PROJ_TM, ctx_row)),
                  pl.BlockSpec((1, D_MODEL), lambda i: (0, 0)),
                  pl.BlockSpec((D_MODEL, D_PROJ), lambda i: (0, 0))],
        out_specs=pl.BlockSpec((PROJ_TM, D_PROJ), lambda i: (i, 0)),
        out_shape=jax.ShapeDtypeStruct((n, D_PROJ), F32),
        compiler_params=_cparams(1),
        name="in_proj",
    )(x2d, mod, norm_g, w_in_p)


def _time_masks(reverse):
    r = _iota((2 * CHUNK, 2 * CHUNK), 0) % CHUNK
    c = _iota((2 * CHUNK, 2 * CHUNK), 1) % CHUNK
    if reverse:
        return r < c, r <= c
    return r > c, r >= c


def _cumsum_matrix(reverse):
    r = _iota((CHUNK, CHUNK), 0)
    c = _iota((CHUNK, CHUNK), 1)
    tri = (r <= c) if reverse else (r >= c)
    return tri.astype(BF16)


def _stack_heads(x, half):
    m0 = _iota(x.shape, 1) < half
    return jnp.concatenate([jnp.where(m0, x, 0.0), jnp.where(m0, 0.0, x)], axis=0)


def _head_sums(x):
    parts = []
    for p in range(x.shape[1] // LANE):
        xp = x[:, p * LANE:(p + 1) * LANE]
        m0 = _iota(xp.shape, 1) < 64
        s0 = jnp.sum(jnp.where(m0, xp, 0.0), axis=-1, keepdims=True)
        s1 = jnp.sum(jnp.where(m0, 0.0, xp), axis=-1, keepdims=True)
        parts.append(jnp.where(m0, s0, s1))
    return parts[0] if len(parts) == 1 else jnp.concatenate(parts, axis=1)


def _rwkv_chunk_terms(insts):
    c = CHUNK
    step_row = _iota((c, LANE), 0)
    step_col = _iota((c, LANE), 1) % c
    eye_w = (step_row == step_col).astype(F32)
    same_head = (_iota((LANE, LANE), 0) // 64) == (_iota((LANE, LANE), 1) // 64)
    stack_bf = lambda x: _stack_heads(x, 64).astype(BF16)
    cums = [_mm_01_lhs(_cumsum_matrix(rev), lw) for (_, lw, _, _, _, _, rev) in insts]
    pre = []
    for (r, lw, kd, a, b, v, rev), cum in zip(insts, cums):
        end = cum[0:1] if rev else cum[c - 1:c]
        inv_w = jnp.exp(-cum)
        rem_w = jnp.exp(end - cum)
        a_t = a * jnp.exp(cum - lw)
        r_t = r * jnp.exp(cum)
        bk_s = jnp.concatenate([stack_bf(b * inv_w), stack_bf(kd * inv_w)], axis=0)
        bkh = jnp.concatenate([b * rem_w, kd * rem_w], axis=0).astype(BF16)
        pre.append((a_t, r_t, bk_s, bkh, v, jnp.exp(end)))
    ms = [_dot(jnp.concatenate([a_t, r_t], axis=0).astype(BF16), bk_s, _NT) for (a_t, r_t, bk_s, _, _, _) in pre]
    mats = []
    for m, (_, _, _, _, _, _, rev) in zip(ms, insts):
        strict = (step_row < step_col) if rev else (step_row > step_col)
        incl = (step_row <= step_col) if rev else (step_row >= step_col)
        l_ab = jnp.where(strict, m[0:c, 0:LANE], 0.0)
        l_akrk = jnp.concatenate([jnp.where(strict, m[0:c, LANE:2 * LANE], 0.0),
                                  jnp.where(incl, m[c:2 * c, LANE:2 * LANE], 0.0)], axis=0).astype(BF16)
        m_rb = jnp.where(incl, m[c:2 * c, 0:LANE], 0.0).astype(BF16)
        mats.append((l_ab, l_akrk, m_rb))
    lvs = [_dot(l_akrk, stack_bf(pr[4])) for (_, l_akrk, _), pr in zip(mats, pre)]
    ps = [eye_w + l_ab for (l_ab, _, _) in mats]
    lps = [_dot(l_ab.astype(BF16), stack_bf(l_ab)) for (l_ab, _, _) in mats]
    for level in range(1, 6):
        if level < 5:
            xs = [_dot(lp.astype(BF16), jnp.concatenate([stack_bf(p), stack_bf(lp)], axis=1))
                  for lp, p in zip(lps, ps)]
            ps = [p + x[:, 0:LANE] for p, x in zip(ps, xs)]
            lps = [x[:, LANE:2 * LANE] for x in xs]
        else:
            ps = [p + _dot(lp.astype(BF16), stack_bf(p)) for lp, p in zip(lps, ps)]
    pxs = [_dot(p.astype(BF16), jnp.concatenate([stack_bf(pr[0]), stack_bf(lv[0:c])], axis=1))
           for p, pr, lv in zip(ps, pre, lvs)]
    mzs = [_dot(mt[2], jnp.concatenate([stack_bf(px[:, 0:LANE]), stack_bf(px[:, LANE:2 * LANE])], axis=1))
           for mt, px in zip(mats, pxs)]
    ts = [_dot(px[:, 0:LANE].astype(BF16), pr[3][0:c], _TN) for px, pr in zip(pxs, pre)]
    gs = [_dot(jnp.concatenate([px[:, LANE:2 * LANE], pr[4]], axis=0).astype(BF16), pr[3], _TN)
          for px, pr in zip(pxs, pre)]
    out = []
    for pr, lv, mz, t, g in zip(pre, lvs, mzs, ts, gs):
        q = pr[1] + mz[:, 0:LANE]
        y0 = mz[:, LANE:2 * LANE] + lv[c:2 * c]
        out.append((jnp.where(same_head, t, 0.0).astype(BF16), jnp.where(same_head, g, 0.0), pr[5],
                    q.astype(BF16), y0))
    return out


def _gla_chunk_terms(insts):
    cums = [_mm_01_lhs(_cumsum_matrix(rev), g) for (_, _, _, g, rev) in insts]
    pre = []
    for (q, k, v, g, rev), cum in zip(insts, cums):
        end = cum[0:1] if rev else cum[CHUNK - 1:CHUNK]
        qt = q * jnp.exp(cum)
        k_s = _stack_heads(k * jnp.exp(-cum), 64).astype(BF16)
        kh_s = _stack_heads(k * jnp.exp(end - cum), 64).astype(BF16)
        pre.append((qt.astype(BF16), _stack_heads(qt, 64).astype(BF16), k_s, kh_s,
                    _stack_heads(v, 128).astype(BF16), jnp.exp(end)))
    atts = [_dot(pr[1], pr[2], _NT) for pr in pre]
    atts = [jnp.where(_time_masks(inst[4])[1], att, 0.0).astype(BF16) for att, inst in zip(atts, insts)]
    o2s = [_dot(att, pr[4]) for att, pr in zip(atts, pre)]
    kvs = [_dot(pr[4], pr[3], _TN) for pr in pre]
    return [(pr[0], o2[0:CHUNK] + o2[CHUNK:2 * CHUNK], pr[5], kv) for pr, o2, kv in zip(pre, o2s, kvs)]


def _rwkv_kernel(seq_len, is_grid, r_ref, k_ref, v_ref, lora_ref, cwr_ref, cwk_ref, cwv_ref, w2_ref, a2_ref,
                 g2_ref, vec_ref, s0_ref, y_ref, sout_ref, pad_ref, bonus_ref, gate_ref, yf_ref,
                 yb_ref, st_ref, tt_ref, tg_ref, tw_ref, tq_ref, ty_ref):
    n_chunks = seq_len // CHUNK
    npp = RW_PAIRS_PER_STEP
    w = npp * LANE
    pair = lambda x, p: x[:, p * LANE:(p + 1) * LANE]
    vec = vec_ref[...]
    w0 = (vec[0:1], vec[1:2])
    a0 = (vec[2:3], vec[3:4])
    k_k, k_a, r_k, ln_w, ln_b = vec[4:5], vec[5:6], vec[6:7], vec[7:8], vec[8:9]
    block_sum = _head_sums

    zeros = jnp.zeros((CONV_PAD, 3 * w), F32)
    pad_ref[0:CONV_PAD, :] = zeros
    pad_ref[CONV_PAD + seq_len:2 * CONV_PAD + seq_len, :] = zeros
    pad_ref[CONV_PAD:CONV_PAD + seq_len, 0:w] = r_ref[...]
    pad_ref[CONV_PAD:CONV_PAD + seq_len, w:2 * w] = k_ref[...]
    pad_ref[CONV_PAD:CONV_PAD + seq_len, 2 * w:3 * w] = v_ref[...]
    cw = jnp.concatenate([cwr_ref[...], cwk_ref[...], cwv_ref[...]], axis=1)
    col = _iota((CHUNK, 3 * w), 0)

    def conv_chunk(c):
        base = pl.multiple_of(CONV_PAD + c * CHUNK, CHUNK)
        acc = jnp.zeros((CHUNK, 3 * w), F32)
        for di in ((-1, 0, 1) if is_grid else (0,)):
            win = pad_ref[pl.ds(base + di * GRID_W - 8, CHUNK + 16), :]
            for dj in (-1, 0, 1):
                tap = (di + 1) * 3 + dj + 1
                term = win[8 + dj:8 + dj + CHUNK] * cw[tap:tap + 1]
                if is_grid and dj == -1:
                    term = jnp.where(col >= 1, term, 0.0)
                if is_grid and dj == 1:
                    term = jnp.where(col <= GRID_W - 2, term, 0.0)
                acc = acc + term
        return acc[:, 0:w], acc[:, w:2 * w], acc[:, 2 * w:3 * w]

    def terms_body(j, carry):
        cs = [j * TERM_UNROLL + u for u in range(TERM_UNROLL)]
        offs = [pl.multiple_of(c * CHUNK, CHUNK) for c in cs]
        rkv = [conv_chunk(c) for c in cs]
        loras = [lora_ref[pl.ds(off, CHUNK), :] for off in offs]
        kks = [kc * k_k for (_, kc, _) in rkv]
        kk_ss = [block_sum(kk * kk) for kk in kks]
        bon_ss = [block_sum(rc * kc * r_k) for (rc, kc, _) in rkv]
        gates = [_mm3_split(_split2(_sigmoid(lo[:, 2 * LANE:3 * LANE])), (g2_ref[0], g2_ref[1])) for lo in loras]
        lw_ins = [_split2(jnp.tanh(lo[:, 0:LANE])) for lo in loras]
        la_ins = [_split2(lo[:, LANE:2 * LANE]) for lo in loras]
        lws = [[_mm3_split(x, (w2_ref[0, d], w2_ref[1, d])) for d in range(2)] for x in lw_ins]
        ags = [[_mm3_split(x, (a2_ref[0, d], a2_ref[1, d])) for d in range(2)] for x in la_ins]
        insts, where = [], []
        for u in range(TERM_UNROLL):
            rc, kc, vc = rkv[u]
            kk = kks[u] * lax.rsqrt(kk_ss[u] + EPS)
            bonus_ref[pl.ds(offs[u], CHUNK), :] = bon_ss[u] * vc
            gate_ref[pl.ds(offs[u], CHUNK), :] = gates[u]
            for d in range(2):
                lw = -RW_DECAY_SCALE * _sigmoid(w0[d] + lws[u][d])
                ag = _sigmoid(a0[d] + ags[u][d])
                kd = kc * (1.0 + (ag - 1.0) * k_a)
                kb = kk * ag
                for p in range(npp):
                    insts.append((pair(rc, p), pair(lw, p), pair(kd, p), -pair(kk, p), pair(kb, p), pair(vc, p),
                                  d == 1))
                    where.append((d, p, cs[u]))
        for (d, p, c), (t, g, w_end, q, y0) in zip(where, _rwkv_chunk_terms(insts)):
            tt_ref[d, p, c] = t
            tg_ref[d, p, c] = g
            tw_ref[d, p, c] = jnp.broadcast_to(w_end, (8, LANE))
            tq_ref[d, p, c] = q
            ty_ref[d, p, c] = y0
        return carry

    lax.fori_loop(0, n_chunks // TERM_UNROLL, terms_body, 0)

    st_ref[...] = s0_ref[0]
    chains = [(d, p) for d in range(2) for p in range(npp)]

    def scan_body(i, carry):
        cs = (i, n_chunks - 1 - i)
        ss = [st_ref[d, p] for d, p in chains]
        sb = [s.astype(BF16) for s in ss]
        ys = [_dot(tq_ref[d, p, cs[d]], b, _NT) + ty_ref[d, p, cs[d]] for (d, p), b in zip(chains, sb)]
        sn = [s * tw_ref[d, p, cs[d]][0:1] + _dot(b, tt_ref[d, p, cs[d]]) + tg_ref[d, p, cs[d]]
              for (d, p), s, b in zip(chains, ss, sb)]
        for (d, p), y, s in zip(chains, ys, sn):
            st_ref[d, p] = s
            out_ref = yf_ref if d == 0 else yb_ref
            out_ref[pl.ds(pl.multiple_of(cs[d] * CHUNK, CHUNK), CHUNK), p * LANE:(p + 1) * LANE] = y
        return carry

    lax.fori_loop(0, n_chunks, scan_body, 0)
    sout_ref[0] = st_ref[...]

    def post_body(j, carry):
        offs = [pl.multiple_of((j * TERM_UNROLL + u) * CHUNK, CHUNK) for u in range(TERM_UNROLL)]
        ys = [yf_ref[pl.ds(off, CHUNK), :] + yb_ref[pl.ds(off, CHUNK), :] for off in offs]
        mus = [block_sum(y) * (1.0 / 64) for y in ys]
        dlts = [y - mu for y, mu in zip(ys, mus)]
        vrs = [block_sum(dlt * dlt) * (1.0 / 64) for dlt in dlts]
        for off, dlt, var in zip(offs, dlts, vrs):
            yn = dlt * lax.rsqrt(var + RW_LN_EPS) * ln_w + ln_b
            y_ref[pl.ds(off, CHUNK), :] = (yn + bonus_ref[pl.ds(off, CHUNK), :]) * gate_ref[pl.ds(off, CHUNK), :]
        return carry

    lax.fori_loop(0, n_chunks // TERM_UNROLL, post_body, 0)


def _rwkv_mixer(proj, n_seq, seq_len, is_grid, prm, s0):
    n_pairs = RW_W // LANE
    n_chunks = seq_len // CHUNK
    npp = RW_PAIRS_PER_STEP
    w = npp * LANE
    assert n_chunks % TERM_UNROLL == 0 and n_pairs % npp == 0
    col = lambda cb: (lambda b, p: (b, cb // npp + p))
    par = lambda cb: (lambda b, p: (0, cb // npp + p))
    kernel = functools.partial(_rwkv_kernel, seq_len, is_grid)
    y, s_out = pl.pallas_call(
        kernel,
        grid=(n_seq, n_pairs // npp),
        in_specs=[pl.BlockSpec((seq_len, w), col(CB_R)),
                  pl.BlockSpec((seq_len, w), col(CB_K)),
                  pl.BlockSpec((seq_len, w), col(CB_V)),
                  pl.BlockSpec((seq_len, 3 * LANE), lambda b, p: (b, CB_LORA // 3)),
                  pl.BlockSpec((9, w), par(CB_R)),
                  pl.BlockSpec((9, w), par(CB_K)),
                  pl.BlockSpec((9, w), par(CB_V)),
                  pl.BlockSpec((2, 2, LANE, w), lambda b, p: (0, 0, 0, p)),
                  pl.BlockSpec((2, 2, LANE, w), lambda b, p: (0, 0, 0, p)),
                  pl.BlockSpec((2, LANE, w), lambda b, p: (0, 0, p)),
                  pl.BlockSpec((16, w), lambda b, p: (0, p)),
                  pl.BlockSpec((1, 2, npp, LANE, LANE), lambda b, p: (b, 0, p, 0, 0))],
        out_specs=[pl.BlockSpec((seq_len, w), lambda b, p: (b, p)),
                   pl.BlockSpec((1, 2, npp, LANE, LANE), lambda b, p: (b, 0, p, 0, 0))],
        out_shape=[jax.ShapeDtypeStruct((n_seq * seq_len, RW_W), F32),
                   jax.ShapeDtypeStruct((n_seq, 2, n_pairs, LANE, LANE), F32)],
        scratch_shapes=[pltpu.VMEM((seq_len + 2 * CONV_PAD, 3 * w), F32)]
                       + [pltpu.VMEM((seq_len, w), F32)] * 4
                       + [pltpu.VMEM((2, npp, LANE, LANE), F32),
                          pltpu.VMEM((2, npp, n_chunks, LANE, LANE), BF16),
                          pltpu.VMEM((2, npp, n_chunks, LANE, LANE), F32),
                          pltpu.VMEM((2, npp, n_chunks, 8, LANE), F32),
                          pltpu.VMEM((2, npp, n_chunks, CHUNK, LANE), BF16),
                          pltpu.VMEM((2, npp, n_chunks, CHUNK, LANE), F32)],
        compiler_params=_cparams(2),
        name="rwkv_mixer",
    )(proj, proj, proj, proj, prm['conv'], prm['conv'], prm['conv'], prm['w2p'], prm['a2p'], prm['g2'],
      prm['vec'], s0)
    return y, s_out


def _gla_kernel(seq_len, q_ref, k_ref, v_ref, og_ref, lgk_ref, gk2_ref, gvec_ref, s0_ref, y_ref, sout_ref,
                of_ref, ob_ref, st_ref, tq_ref, to_ref, ta_ref, tkv_ref):
    n_chunks = seq_len // CHUNK
    gvec = gvec_ref[...]

    def terms_body(j, carry):
        insts, where = [], []
        for u in range(TERM_UNROLL):
            c = j * TERM_UNROLL + u
            off = pl.multiple_of(c * CHUNK, CHUNK)
            lgk = _split2(lgk_ref[pl.ds(off, CHUNK), :])
            qc = q_ref[pl.ds(off, CHUNK), :] * GLA_Q_SCALE
            kc = k_ref[pl.ds(off, CHUNK), :]
            vc = v_ref[pl.ds(off, CHUNK), :]
            for d in range(2):
                x = _mm3_split(lgk, (gk2_ref[0, d], gk2_ref[1, d])) + gvec[d:d + 1]
                g = _log_sigmoid(x) * (1.0 / GLA_GATE_NORM)
                insts.append((qc, kc, vc, g, d == 1))
                where.append((d, c))
        for (d, c), (qt, o0, a_end, kv) in zip(where, _gla_chunk_terms(insts)):
            tq_ref[d, c] = qt
            to_ref[d, c] = o0
            ta_ref[d, c] = jnp.broadcast_to(a_end, (8, LANE))
            tkv_ref[d, c] = kv
        return carry

    lax.fori_loop(0, n_chunks // TERM_UNROLL, terms_body, 0)

    st_ref[...] = s0_ref[0, :, 0]

    def scan_body(i, carry):
        cs = (i, n_chunks - 1 - i)
        ss = [st_ref[d] for d in range(2)]
        os_ = [_dot(tq_ref[d, cs[d]], ss[d].astype(BF16), _NT) + to_ref[d, cs[d]] for d in range(2)]
        for d, out_ref in enumerate((of_ref, ob_ref)):
            st_ref[d] = ss[d] * ta_ref[d, cs[d]][0:1] + tkv_ref[d, cs[d]]
            out_ref[pl.ds(pl.multiple_of(cs[d] * CHUNK, CHUNK), CHUNK), :] = os_[d]
        return carry

    lax.fori_loop(0, n_chunks, scan_body, 0)
    sout_ref[0, :, 0] = st_ref[...]
    norm_g = gvec[2:3, 0:LANE]

    def post_body(c, carry):
        off = pl.multiple_of(c * CHUNK, CHUNK)
        for h in range(2):
            o = of_ref[pl.ds(off, CHUNK), h * LANE:(h + 1) * LANE] + ob_ref[pl.ds(off, CHUNK), h * LANE:(h + 1) * LANE]
            gate = _silu(og_ref[pl.ds(off, CHUNK), h * LANE:(h + 1) * LANE])
            y_ref[pl.ds(off, CHUNK), h * LANE:(h + 1) * LANE] = _rmsnorm_rows(o) * norm_g * gate
        return carry

    lax.fori_loop(0, n_chunks, post_body, 0)


def _gla_mixer(proj, n_seq, seq_len, prm, s0):
    n_pairs = GLA_QK_W // LANE
    n_chunks = seq_len // CHUNK
    assert n_chunks % TERM_UNROLL == 0
    kernel = functools.partial(_gla_kernel, seq_len)
    y, s_out = pl.pallas_call(
        kernel,
        grid=(n_seq, n_pairs),
        in_specs=[pl.BlockSpec((seq_len, LANE), lambda b, p: (b, CB_GQ + p)),
                  pl.BlockSpec((seq_len, LANE), lambda b, p: (b, CB_GK + p)),
                  pl.BlockSpec((seq_len, 2 * LANE), lambda b, p: (b, CB_GV // 2 + p)),
                  pl.BlockSpec((seq_len, 2 * LANE), lambda b, p: (b, CB_OG // 2 + p)),
                  pl.BlockSpec((seq_len, LANE), lambda b, p: (b, CB_LGK)),
                  pl.BlockSpec((2, 2, LANE, LANE), lambda b, p: (0, 0, 0, p)),
                  pl.BlockSpec((8, LANE), lambda b, p: (0, p)),
                  pl.BlockSpec((1, 2, 1, 2 * LANE, LANE), lambda b, p: (b, 0, p, 0, 0))],
        out_specs=[pl.BlockSpec((seq_len, 2 * LANE), lambda b, p: (b, p)),
                   pl.BlockSpec((1, 2, 1, 2 * LANE, LANE), lambda b, p: (b, 0, p, 0, 0))],
        out_shape=[jax.ShapeDtypeStruct((n_seq * seq_len, GLA_V_W), F32),
                   jax.ShapeDtypeStruct((n_seq, 2, n_pairs, 2 * LANE, LANE), F32)],
        scratch_shapes=[pltpu.VMEM((seq_len, 2 * LANE), F32)] * 2
                       + [pltpu.VMEM((2, 2 * LANE, LANE), F32),
                          pltpu.VMEM((2, n_chunks, CHUNK, LANE), BF16),
                          pltpu.VMEM((2, n_chunks, CHUNK, 2 * LANE), F32),
                          pltpu.VMEM((2, n_chunks, 8, LANE), F32),
                          pltpu.VMEM((2, n_chunks, 2 * LANE, LANE), F32)],
        compiler_params=_cparams(2),
        name="gla_mixer",
    )(proj, proj, proj, proj, proj, prm['gk2p'], prm['gvec'], s0)
    return y, s_out


OUT_TM = 256
ROUTE_NEG = -1e30
LANE_GROUP0 = N_EXPERTS


def _route(logits):
    lane = _iota(logits.shape, 1)
    lane_f = lane.astype(F32)
    big = float(LANE)
    is_g = (lane >= LANE_GROUP0) & (lane < LANE_GROUP0 + 4)
    gmax = jnp.max(jnp.where(is_g, logits, ROUTE_NEG), axis=-1, keepdims=True)
    gidx = jnp.min(jnp.where(is_g & (logits == gmax), lane_f, big), axis=-1, keepdims=True) - LANE_GROUP0
    gsum = jnp.sum(jnp.where(is_g, jnp.exp(jnp.minimum(logits - gmax, 0.0)), 0.0), axis=-1, keepdims=True)
    g_w = 1.0 / gsum
    in_grp = (lane < N_EXPERTS) & ((lane // 4).astype(F32) == gidx)
    m1 = jnp.max(jnp.where(in_grp, logits, ROUTE_NEG), axis=-1, keepdims=True)
    i1 = jnp.min(jnp.where(in_grp & (logits == m1), lane_f, big), axis=-1, keepdims=True)
    rest = in_grp & (lane_f != i1)
    m2 = jnp.max(jnp.where(rest, logits, ROUTE_NEG), axis=-1, keepdims=True)
    i2 = jnp.min(jnp.where(rest & (logits == m2), lane_f, big), axis=-1, keepdims=True)
    t = jnp.exp(m2 - m1)
    w1 = g_w / (1.0 + t)
    return jnp.where(lane_f == i1, w1, 0.0) + jnp.where(lane_f == i2, w1 * t, 0.0)


def _outproj_kernel(yr_ref, yg_ref, x_ref, mod_ref, wo_ref, g_ref, wr_ref, br_ref, x1_ref, h2_ref, cmb_ref):
    m = mod_ref[0]
    mix = _mm(yr_ref[...], wo_ref[0:RW_W, :]) + _mm(yg_ref[...], wo_ref[RW_W:RW_W + GLA_V_W, :])
    x1 = x_ref[...] + m[2:3] * mix
    h2 = _rmsnorm_rows(x1) * g_ref[...] * (1.0 + m[4:5]) + m[3:4]
    x1_ref[...] = x1
    h2_ref[...] = h2.astype(BF16)
    cmb_ref[...] = _route(_mm3_split(_split2(h2), (wr_ref[0], wr_ref[1])) + br_ref[...])


def _out_projection(y_rw, y_gla, x2d, mod, w_out, norm_g, w_route, b_route, seq_len, ctx_row):
    n = x2d.shape[0]
    tile = lambda w: pl.BlockSpec((OUT_TM, w), lambda i: (i, 0))
    full = lambda a: pl.BlockSpec(a.shape, lambda i: (0,) * a.ndim)
    return pl.pallas_call(
        _outproj_kernel,
        grid=(n // OUT_TM,),
        in_specs=[tile(RW_W), tile(GLA_V_W), tile(D_MODEL),
                  pl.BlockSpec((1, N_MOD, D_MODEL), _mod_index(seq_len, OUT_TM, ctx_row)),
                  full(w_out), full(norm_g), full(w_route), full(b_route)],
        out_specs=[tile(D_MODEL), tile(D_MODEL), tile(LANE)],
        out_shape=[jax.ShapeDtypeStruct((n, D_MODEL), F32), jax.ShapeDtypeStruct((n, D_MODEL), BF16),
                   jax.ShapeDtypeStruct((n, LANE), F32)],
        compiler_params=_cparams(1),
        name="out_proj_router",
    )(y_rw, y_gla, x2d, mod, w_out, norm_g, w_route, b_route)


MOE_TM = 512
MOE_RB = 128
MOE_INTERLEAVE = 4
SLOT_ALIGN = 16
MOE_SLOTS = 2 * MOE_TM + N_EXPERTS * SLOT_ALIGN + MOE_RB


def _moe_kernel(h2_ref, cmb_ref, x1_ref, mod_ref, w1_ref, w3_ref, w2_ref, fg_ref, y_ref, xs_ref, ys_ref):
    cmb = cmb_ref[...]
    lane = _iota(cmb.shape, 1).astype(F32)
    sel = cmb > 0.0
    sel01 = jnp.where(sel, 1.0, 0.0).astype(BF16)
    before = (_iota((MOE_TM, MOE_TM), 0) > _iota((MOE_TM, MOE_TM), 1)).astype(BF16)
    pos = _dot(before, sel01)
    cnt = pos[MOE_TM - 1:MOE_TM] + sel01[MOE_TM - 1:MOE_TM].astype(F32)
    seg = jnp.floor((cnt + (SLOT_ALIGN - 1)) * (1.0 / SLOT_ALIGN))
    lower_experts = (_iota((LANE, LANE), 0) < _iota((LANE, LANE), 1)).astype(BF16)
    start = _dot(jnp.broadcast_to(seg, (8, LANE)).astype(BF16), lower_experts)[0:1] * SLOT_ALIGN
    n_blk = jnp.floor((cnt + (MOE_RB - 1)) * (1.0 / MOE_RB)).astype(jnp.int32)
    start_i = start.astype(jnp.int32)
    cnt_i = cnt.astype(jnp.int32)
    slot = start + pos
    e_a = jnp.min(jnp.where(sel, lane, float(LANE)), axis=-1, keepdims=True)
    e_b = jnp.max(jnp.where(sel, lane, -1.0), axis=-1, keepdims=True)
    pick = lambda e, x: jnp.sum(jnp.where(lane == e, x, 0.0), axis=-1, keepdims=True)
    slot_a, w_a = pick(e_a, slot), pick(e_a, cmb)
    slot_b = jnp.where(e_b != e_a, pick(e_b, slot), -1.0)
    w_b = pick(e_b, cmb)

    slots_t = jnp.where(lane == 0.0, slot_a, jnp.where(lane == 1.0, slot_b, -1.0)).T
    row_slot = _iota((MOE_SLOTS, MOE_TM), 0).astype(F32)
    gather = jnp.where((row_slot == slots_t[0:1]) | (row_slot == slots_t[1:2]), 1.0, 0.0).astype(BF16)
    xs_ref[...] = _dot(gather, h2_ref[...]).astype(BF16)
    ys_ref[...] = jnp.zeros_like(ys_ref)

    row_in_blk = _iota((MOE_RB, D_MODEL), 0)

    def expert_blocks(experts, r0s, ends):
        xbs = [xs_ref[pl.ds(r0, MOE_RB), :] for r0 in r0s]
        gates = [_dot(xb, w3_ref[e]) for xb, e in zip(xbs, experts)]
        ups = [_dot(xb, w1_ref[e]) for xb, e in zip(xbs, experts)]
        acts = [(_silu(g) * u).astype(BF16) for g, u in zip(gates, ups)]
        outs = [_dot(a, w2_ref[e]) for a, e in zip(acts, experts)]
        for r0, end, out in zip(r0s, ends, outs):
            keep = row_in_blk + r0 >= end
            ys_ref[pl.ds(r0, MOE_RB), :] = jnp.where(keep, ys_ref[pl.ds(r0, MOE_RB), :], out.astype(BF16))

    seg_start = [pl.multiple_of(start_i[0, e], SLOT_ALIGN) for e in range(N_EXPERTS)]
    seg_end = [seg_start[e] + cnt_i[0, e] for e in range(N_EXPERTS)]
    for e0 in range(0, N_EXPERTS, MOE_INTERLEAVE):
        es = list(range(e0, e0 + MOE_INTERLEAVE))
        expert_blocks(es, [seg_start[e] for e in es], [seg_end[e] for e in es])
    for e in range(N_EXPERTS):
        def extra_block(b, carry, e=e):
            expert_blocks([e], [pl.multiple_of(seg_start[e] + b * MOE_RB, SLOT_ALIGN)], [seg_end[e]])
            return carry

        lax.fori_loop(1, n_blk[0, e], extra_block, 0)

    col_slot = _iota((MOE_TM, MOE_SLOTS), 1).astype(F32)
    scatter = (jnp.where(col_slot == slot_a, w_a, 0.0) + jnp.where(col_slot == slot_b, w_b, 0.0)).astype(BF16)
    x2 = x1_ref[...] + mod_ref[0][5:6] * _dot(scatter, ys_ref[...])
    y_ref[...] = _rmsnorm_rows(x2) * fg_ref[...]


def _moe(h2, cmb, x1, mod, w1, w3, w2, final_g, seq_len, ctx_row):
    n = h2.shape[0]
    tile = lambda w: pl.BlockSpec((MOE_TM, w), lambda i: (i, 0))
    resident = lambda a: pl.BlockSpec(a.shape, lambda i: (0,) * a.ndim, pipeline_mode=pl.Buffered(1))
    return pl.pallas_call(
        _moe_kernel,
        grid=(n // MOE_TM,),
        in_specs=[tile(D_MODEL), tile(LANE), tile(D_MODEL),
                  pl.BlockSpec((1, N_MOD, D_MODEL), _mod_index(seq_len, MOE_TM, ctx_row)),
                  resident(w1), resident(w3), resident(w2),
                  pl.BlockSpec((1, D_MODEL), lambda i: (0, 0))],
        out_specs=tile(D_MODEL),
        out_shape=jax.ShapeDtypeStruct((n, D_MODEL), F32),
        scratch_shapes=[pltpu.VMEM((MOE_SLOTS, D_MODEL), BF16), pltpu.VMEM((MOE_SLOTS, D_MODEL), BF16)],
        compiler_params=_cparams(1),
        name="moe_experts",
    )(h2, cmb, x1, mod, w1, w3, w2, final_g)


def _pad_rows(x, rows):
    return jnp.pad(x, ((0, rows - x.shape[0]),) + ((0, 0),) * (x.ndim - 1))


def _pack_params(l, w_in, rw_conv, rw_w0, rw_w2, rw_a0, rw_a2, rw_g2, rw_k_k, rw_k_a, rw_r_k, rw_ln_w, rw_ln_b,
                 gla_gk2, gla_gk_b, gla_norm_g, moe_w_group, moe_b_group, moe_w_expert, moe_b_expert):
    wi = w_in[l]
    z = lambda n: jnp.zeros((D_MODEL, n), F32)
    w_in_p = jnp.concatenate([wi[:, 0:1920], wi[:, 3456:3488], z(LANE - 32), wi[:, 1920:3456]], axis=1).astype(BF16)
    z64 = jnp.zeros((64, RW_W), F32)
    w2p = jnp.stack([jnp.concatenate([rw_w2[l, 0], z64], 0), jnp.concatenate([z64, rw_w2[l, 1]], 0)])
    a2p = jnp.stack([jnp.concatenate([rw_a2[l, 0], z64], 0), jnp.concatenate([z64, rw_a2[l, 1]], 0)])
    vec = _pad_rows(jnp.stack([rw_w0[l, 0], rw_w0[l, 1], rw_a0[l, 0], rw_a0[l, 1], rw_k_k[l], rw_k_a[l],
                               rw_r_k[l].reshape(RW_W), rw_ln_w[l], rw_ln_b[l]]), 16)
    rw = {'conv': rw_conv[l].reshape(9, 3 * RW_W), 'w2p': _split_param(w2p), 'a2p': _split_param(a2p),
          'g2': _split_param(rw_g2[l]), 'vec': vec}
    gk2p = jnp.stack([_pad_rows(gla_gk2[l, 0], LANE),
                      _pad_rows(jnp.concatenate([jnp.zeros((16, GLA_QK_W), F32), gla_gk2[l, 1]], 0), LANE)])
    gvec = _pad_rows(jnp.stack([gla_gk_b[l, 0], gla_gk_b[l, 1], jnp.tile(gla_norm_g[l], 2)]), 8)
    gla = {'gk2p': _split_param(gk2p), 'gvec': gvec}
    w_route = _split_param(jnp.concatenate([moe_w_expert[l], moe_w_group[l], z(LANE - N_EXPERTS - 4)], axis=1))
    b_route = jnp.concatenate([moe_b_expert[l], moe_b_group[l], jnp.zeros((LANE - N_EXPERTS - 4,), F32)])[None]
    return w_in_p, rw, gla, w_route, b_route


def _pair_states_rwkv(s):
    b = s.shape[0]
    s = s.reshape(b, 2, 4, 2, 64, 64)
    z = jnp.zeros_like(s[:, :, :, 0])
    top = jnp.concatenate([s[:, :, :, 0], z], axis=-1)
    bot = jnp.concatenate([z, s[:, :, :, 1]], axis=-1)
    return jnp.concatenate([top, bot], axis=-2)


def _unpair_states_rwkv(s):
    b = s.shape[0]
    return jnp.stack([s[..., 0:64, 0:64], s[..., 64:128, 64:128]], axis=3).reshape(b, 2, 8, 64, 64)


def _pair_states_gla(s):
    b = s.shape[0]
    st = jnp.swapaxes(s, -1, -2).reshape(b, 2, 2, 2, 128, 64)
    z = jnp.zeros_like(st[:, :, :, 0])
    top = jnp.concatenate([st[:, :, :, 0], z], axis=-1)
    bot = jnp.concatenate([z, st[:, :, :, 1]], axis=-1)
    return jnp.concatenate([top, bot], axis=-2)


def _unpair_states_gla(s):
    b = s.shape[0]
    st = jnp.stack([s[..., 0:128, 0:64], s[..., 128:256, 64:128]], axis=3).reshape(b, 2, 4, 128, 64)
    return jnp.swapaxes(st, -1, -2)


def _trunk_pass(x, mod, ctx_row, s_rw0, s_gla0, is_grid, pk, norm1_g, norm2_g, w_out, w1, w3, w2, final_g):
    n_seq, seq_len, _ = x.shape
    x2d = x.reshape(n_seq * seq_len, D_MODEL)
    w_in_p, rw, gla, w_route, b_route = pk
    proj = _in_projection(x2d, mod, norm1_g, w_in_p, seq_len, ctx_row)
    y_rw, s_rw = _rwkv_mixer(proj, n_seq, seq_len, is_grid, rw, s_rw0)
    y_gla, s_gla = _gla_mixer(proj, n_seq, seq_len, gla, s_gla0)
    x1, h2, cmb = _out_projection(y_rw, y_gla, x2d, mod, w_out, norm2_g, w_route, b_route, seq_len, ctx_row)
    y = _moe(h2, cmb, x1, mod, w1, w3, w2, final_g, seq_len, ctx_row)
    return y.reshape(n_seq, seq_len, D_MODEL), s_rw, s_gla


def kernel(x_prompt, x_sample, state_rwkv, state_gla, c, c_ctx, norm1_g, norm2_g, w_ada, b_ada, w_in, w_out,
           rw_conv, rw_w0, rw_w2, rw_a0, rw_a2, rw_g2, rw_k_k, rw_k_a, rw_r_k, rw_ln_w, rw_ln_b,
           gla_gk2, gla_gk_b, gla_norm_g, moe_w_group, moe_b_group, moe_w_expert, moe_b_expert,
           moe_w1, moe_w3, moe_w2, final_g):
    depth = w_in.shape[0]
    assert depth == 1, "the packed layout below handles the single-layer trunk of this problem"
    l = 0
    n_dec = x_sample.shape[0]
    ctx_row = n_dec
    cond8 = _pad_rows(jnp.concatenate([c, c_ctx[None]], axis=0), 8)
    mod = _modulation(cond8, w_ada[l], b_ada[l][None]).reshape(8, N_MOD, D_MODEL)
    pk = _pack_params(l, w_in, rw_conv, rw_w0, rw_w2, rw_a0, rw_a2, rw_g2, rw_k_k, rw_k_a, rw_r_k, rw_ln_w,
                      rw_ln_b, gla_gk2, gla_gk_b, gla_norm_g, moe_w_group, moe_b_group, moe_w_expert,
                      moe_b_expert)
    shared = (pk, norm1_g[l][None], norm2_g[l][None], w_out[l].astype(BF16), moe_w1[l].astype(BF16),
              moe_w3[l].astype(BF16), moe_w2[l].astype(BF16), final_g[None])

    n_ctx = x_prompt.shape[0]
    zeros_rw = jnp.zeros((n_ctx, 2, RW_W // LANE, LANE, LANE), F32)
    zeros_gla = jnp.zeros((n_ctx, 2, GLA_QK_W // LANE, 2 * LANE, LANE), F32)
    y_prompt, s_rw, s_gla = _trunk_pass(x_prompt, mod, ctx_row, zeros_rw, zeros_gla, False, *shared)
    new_state_rwkv = _unpair_states_rwkv(s_rw)[:, None]
    new_state_gla = _unpair_states_gla(s_gla)[:, None]

    y_sample, _, _ = _trunk_pass(x_sample, mod, None, _pair_states_rwkv(state_rwkv[:, l]),
                                 _pair_states_gla(state_gla[:, l]), True, *shared)
    return (y_prompt, y_sample, new_state_rwkv, new_state_gla)
```

```python
import functools

import jax
import jax.numpy as jnp
from jax import lax
from jax.experimental import pallas as pl
from jax.experimental.pallas import tpu as pltpu

F32 = jnp.float32
BF16 = jnp.bfloat16

D_MODEL = 1024
RW_W = 512
GLA_V_W = 512
GLA_QK_W = 256
N_EXPERTS = 16
D_EXPERT = 256
N_MOD = 6
EPS = 1e-6
RW_LN_EPS = 64e-5
RW_DECAY_SCALE = 0.606531
GLA_GATE_NORM = 16.0
GLA_Q_SCALE = 64 ** -0.5
GRID_W = 64

LANE = 128
CHUNK = 64
CONV_PAD = 128
TERM_UNROLL = 4
RW_PAIRS_PER_STEP = 2
GLA_PAIRS = 2
D_PROJ = 28 * LANE
VMEM_LIMIT = 56 * 1024 * 1024

CB_R, CB_K, CB_V, CB_LORA, CB_LGK, CB_GQ, CB_GK, CB_GV, CB_OG = 0, 4, 8, 12, 15, 16, 18, 20, 24

_NN = (((1,), (0,)), ((), ()))
_NT = (((1,), (1,)), ((), ()))
_TN = (((0,), (0,)), ((), ()))


def _dot(a, b, dims=_NN):
    return lax.dot_general(a, b, dims, preferred_element_type=F32)


def _mm(a, b, dims=_NN):
    return _dot(a.astype(BF16), b.astype(BF16), dims)


def _split2(x):
    hi = x.astype(BF16)
    lo = (x - hi.astype(F32)).astype(BF16)
    return hi, lo


def _mm3(a, b, dims=_NN):
    ah, al = _split2(a)
    bh, bl = _split2(b)
    return _dot(ah, bh, dims) + _dot(ah, bl, dims) + _dot(al, bh, dims)


def _mm3_split(a_split, b_split, dims=_NN):
    (ah, al), (bh, bl) = a_split, b_split
    return _dot(ah, bh, dims) + _dot(ah, bl, dims) + _dot(al, bh, dims)


def _split_param(w):
    return jnp.stack(_split2(w))


def _mm_01_lhs(a01, b, dims=_NN):
    b1, b2 = _split2(b)
    return _dot(a01, b1, dims) + _dot(a01, b2, dims)


def _sigmoid(x):
    return 1.0 / (1.0 + jnp.exp(-x))


def _silu(x):
    return x * _sigmoid(x)


def _log_sigmoid(x):
    return jnp.minimum(x, 0.0) - jnp.log(1.0 + jnp.exp(-jnp.abs(x)))


def _iota(shape, dim):
    return lax.broadcasted_iota(jnp.int32, shape, dim)


def _cparams(n_axes):
    return pltpu.CompilerParams(dimension_semantics=("arbitrary",) * n_axes, vmem_limit_bytes=VMEM_LIMIT)


MOD_TN = 768


def _mod_kernel(c_ref, w_ref, b_ref, o_ref):
    o_ref[...] = _mm3(_silu(c_ref[...]), w_ref[...]) + b_ref[...]


def _modulation(cond8, w_ada, b_ada):
    n = w_ada.shape[1]
    return pl.pallas_call(
        _mod_kernel,
        grid=(n // MOD_TN,),
        in_specs=[pl.BlockSpec((8, D_MODEL), lambda j: (0, 0)),
                  pl.BlockSpec((D_MODEL, MOD_TN), lambda j: (0, j)),
                  pl.BlockSpec((1, MOD_TN), lambda j: (0, j))],
        out_specs=pl.BlockSpec((8, MOD_TN), lambda j: (0, j)),
        out_shape=jax.ShapeDtypeStruct((8, n), F32),
        compiler_params=_cparams(1),
        name="adaln_mod",
    )(cond8, w_ada, b_ada)


PROJ_TM = 512


def _rmsnorm_rows(x):
    return x * lax.rsqrt(jnp.mean(x * x, axis=-1, keepdims=True) + EPS)


def _inproj_kernel(x_ref, mod_ref, g_ref, w_ref, o_ref):
    m = mod_ref[0]
    h = _rmsnorm_rows(x_ref[...]) * g_ref[...] * (1.0 + m[1:2]) + m[0:1]
    o_ref[...] = _mm(h, w_ref[...])


def _mod_index(seq_len, tm, ctx_row):
    if ctx_row is not None:
        return lambda i: (ctx_row, 0, 0)
    return lambda i: (i // (seq_len // tm), 0, 0)


def _in_projection(x2d, mod, norm_g, w_in_p, seq_len, ctx_row):
    n = x2d.shape[0]
    return pl.pallas_call(
        _inproj_kernel,
        grid=(n // PROJ_TM,),
        in_specs=[pl.BlockSpec((PROJ_TM, D_MODEL), lambda i: (i, 0)),
                  pl.BlockSpec((1, N_MOD, D_MODEL), _mod_index(seq_len, PROJ_TM, ctx_row)),
                  pl.BlockSpec((1, D_MODEL), lambda i: (0, 0)),
                  pl.BlockSpec((D_MODEL, D_PROJ), lambda i: (0, 0))],
        out_specs=pl.BlockSpec((PROJ_TM, D_PROJ), lambda i: (i, 0)),
        out_shape=jax.ShapeDtypeStruct((n, D_PROJ), F32),
        compiler_params=_cparams(1),
        name="in_proj",
    )(x2d, mod, norm_g, w_in_p)


def _time_masks(reverse):
    r = _iota((2 * CHUNK, 2 * CHUNK), 0) % CHUNK
    c = _iota((2 * CHUNK, 2 * CHUNK), 1) % CHUNK
    if reverse:
        return r < c, r <= c
    return r > c, r >= c


def _cumsum_matrix(reverse):
    r = _iota((CHUNK, CHUNK), 0)
    c = _iota((CHUNK, CHUNK), 1)
    tri = (r <= c) if reverse else (r >= c)
    return tri.astype(BF16)


def _stack_heads(x, half):
    m0 = _iota(x.shape, 1) < half
    return jnp.concatenate([jnp.where(m0, x, 0.0), jnp.where(m0, 0.0, x)], axis=0)


def _head_sums(x):
    parts = []
    for p in range(x.shape[1] // LANE):
        xp = x[:, p * LANE:(p + 1) * LANE]
        m0 = _iota(xp.shape, 1) < 64
        s0 = jnp.sum(jnp.where(m0, xp, 0.0), axis=-1, keepdims=True)
        s1 = jnp.sum(jnp.where(m0, 0.0, xp), axis=-1, keepdims=True)
        parts.append(jnp.where(m0, s0, s1))
    return parts[0] if len(parts) == 1 else jnp.concatenate(parts, axis=1)


def _rwkv_chunk_terms(insts):
    c = CHUNK
    step_row = _iota((c, LANE), 0)
    step_col = _iota((c, LANE), 1) % c
    eye_w = (step_row == step_col).astype(F32)
    same_head = (_iota((LANE, LANE), 0) // 64) == (_iota((LANE, LANE), 1) // 64)
    stack_bf = lambda x: _stack_heads(x, 64).astype(BF16)
    cums = [_mm_01_lhs(_cumsum_matrix(rev), lw) for (_, lw, _, _, _, _, rev) in insts]
    pre = []
    for (r, lw, kd, a, b, v, rev), cum in zip(insts, cums):
        end = cum[0:1] if rev else cum[c - 1:c]
        inv_w = jnp.exp(-cum)
        rem_w = jnp.exp(end - cum)
        a_t = a * jnp.exp(cum - lw)
        r_t = r * jnp.exp(cum)
        bk_s = jnp.concatenate([stack_bf(b * inv_w), stack_bf(kd * inv_w)], axis=0)
        bkh = jnp.concatenate([b * rem_w, kd * rem_w], axis=0).astype(BF16)
        pre.append((a_t, r_t, bk_s, bkh, v, jnp.exp(end)))
    ms = [_dot(jnp.concatenate([a_t, r_t], axis=0).astype(BF16), bk_s, _NT) for (a_t, r_t, bk_s, _, _, _) in pre]
    mats = []
    for m, (_, _, _, _, _, _, rev) in zip(ms, insts):
        strict = (step_row < step_col) if rev else (step_row > step_col)
        incl = (step_row <= step_col) if rev else (step_row >= step_col)
        l_ab = jnp.where(strict, m[0:c, 0:LANE], 0.0)
        l_akrk = jnp.concatenate([jnp.where(strict, m[0:c, LANE:2 * LANE], 0.0),
                                  jnp.where(incl, m[c:2 * c, LANE:2 * LANE], 0.0)], axis=0).astype(BF16)
        m_rb = jnp.where(incl, m[c:2 * c, 0:LANE], 0.0).astype(BF16)
        mats.append((l_ab, l_akrk, m_rb))
    lvs = [_dot(l_akrk, stack_bf(pr[4])) for (_, l_akrk, _), pr in zip(mats, pre)]
    ps = [eye_w + l_ab for (l_ab, _, _) in mats]
    lps = [_dot(l_ab.astype(BF16), stack_bf(l_ab)) for (l_ab, _, _) in mats]
    for level in range(1, 6):
        if level < 5:
            xs = [_dot(lp.astype(BF16), jnp.concatenate([stack_bf(p), stack_bf(lp)], axis=1))
                  for lp, p in zip(lps, ps)]
            ps = [p + x[:, 0:LANE] for p, x in zip(ps, xs)]
            lps = [x[:, LANE:2 * LANE] for x in xs]
        else:
            ps = [p + _dot(lp.astype(BF16), stack_bf(p)) for lp, p in zip(lps, ps)]
    pxs = [_dot(p.astype(BF16), jnp.concatenate([stack_bf(pr[0]), stack_bf(lv[0:c])], axis=1))
           for p, pr, lv in zip(ps, pre, lvs)]
    mzs = [_dot(mt[2], jnp.concatenate([stack_bf(px[:, 0:LANE]), stack_bf(px[:, LANE:2 * LANE])], axis=1))
           for mt, px in zip(mats, pxs)]
    ts = [_dot(px[:, 0:LANE].astype(BF16), pr[3][0:c], _TN) for px, pr in zip(pxs, pre)]
    gs = [_dot(jnp.concatenate([px[:, LANE:2 * LANE], pr[4]], axis=0).astype(BF16), pr[3], _TN)
          for px, pr in zip(pxs, pre)]
    out = []
    for pr, lv, mz, t, g in zip(pre, lvs, mzs, ts, gs):
        q = pr[1] + mz[:, 0:LANE]
        y0 = mz[:, LANE:2 * LANE] + lv[c:2 * c]
        out.append((jnp.where(same_head, t, 0.0).astype(BF16), jnp.where(same_head, g, 0.0), pr[5],
                    q.astype(BF16), y0))
    return out


def _gla_chunk_terms(insts):
    c = CHUNK
    step_row = _iota((c, LANE), 0)
    step_col = _iota((c, LANE), 1) % c
    same_head = (_iota((LANE, 2 * LANE), 0) // 64) == (_iota((LANE, 2 * LANE), 1) // LANE)
    cums = [_mm_01_lhs(_cumsum_matrix(rev), g) for (_, _, _, g, rev) in insts]
    pre = []
    for (q, k, v, g, rev), cum in zip(insts, cums):
        end = cum[0:1] if rev else cum[c - 1:c]
        qt = (q * jnp.exp(cum)).astype(BF16)
        k_s = _stack_heads(k * jnp.exp(-cum), 64).astype(BF16)
        kh = (k * jnp.exp(end - cum)).astype(BF16)
        a_col = jnp.broadcast_to(jnp.exp(end), (LANE, LANE)).T
        pre.append((qt, k_s, kh, v.astype(BF16), _stack_heads(v, LANE).astype(BF16), a_col))
    atts = [_dot(pr[0], pr[1], _NT) for pr in pre]
    atts = [jnp.where((step_row <= step_col) if inst[4] else (step_row >= step_col), att, 0.0).astype(BF16)
            for att, inst in zip(atts, insts)]
    o0s = [_dot(att, pr[4]) for att, pr in zip(atts, pre)]
    kvs = [jnp.where(same_head, _dot(pr[2], pr[3], _TN), 0.0) for pr in pre]
    return [(pr[0], o0, pr[5], kv) for pr, o0, kv in zip(pre, o0s, kvs)]


def _rwkv_kernel(seq_len, is_grid, zero_init, r_ref, k_ref, v_ref, lora_ref, cwr_ref, cwk_ref, cwv_ref, w2_ref,
                 a2_ref, g2_ref, vec_ref, *rest):
    s0_ref = None if zero_init else rest[0]
    (y_ref, sout_ref, pad_ref, bonus_ref, gate_ref, yf_ref, yb_ref, st_ref, tt_ref, tg_ref, tw_ref, tq_ref,
     ty_ref) = rest[0 if zero_init else 1:]
    n_chunks = seq_len // CHUNK
    npp = RW_PAIRS_PER_STEP
    w = npp * LANE
    pair = lambda x, p: x[:, p * LANE:(p + 1) * LANE]
    vec = vec_ref[...]
    w0 = (vec[0:1], vec[1:2])
    a0 = (vec[2:3], vec[3:4])
    k_k, k_a, r_k, ln_w, ln_b = vec[4:5], vec[5:6], vec[6:7], vec[7:8], vec[8:9]
    block_sum = _head_sums

    zeros = jnp.zeros((CONV_PAD, 3 * w), F32)
    pad_ref[0:CONV_PAD, :] = zeros
    pad_ref[CONV_PAD + seq_len:2 * CONV_PAD + seq_len, :] = zeros
    pad_ref[CONV_PAD:CONV_PAD + seq_len, 0:w] = r_ref[...]
    pad_ref[CONV_PAD:CONV_PAD + seq_len, w:2 * w] = k_ref[...]
    pad_ref[CONV_PAD:CONV_PAD + seq_len, 2 * w:3 * w] = v_ref[...]
    cw = jnp.concatenate([cwr_ref[...], cwk_ref[...], cwv_ref[...]], axis=1)
    col = _iota((CHUNK, 3 * w), 0)

    def conv_chunk(c):
        base = pl.multiple_of(CONV_PAD + c * CHUNK, CHUNK)
        acc = jnp.zeros((CHUNK, 3 * w), F32)
        for di in ((-1, 0, 1) if is_grid else (0,)):
            win = pad_ref[pl.ds(base + di * GRID_W - 8, CHUNK + 16), :]
            for dj in (-1, 0, 1):
                tap = (di + 1) * 3 + dj + 1
                term = win[8 + dj:8 + dj + CHUNK] * cw[tap:tap + 1]
                if is_grid and dj == -1:
                    term = jnp.where(col >= 1, term, 0.0)
                if is_grid and dj == 1:
                    term = jnp.where(col <= GRID_W - 2, term, 0.0)
                acc = acc + term
        return acc[:, 0:w], acc[:, w:2 * w], acc[:, 2 * w:3 * w]

    def terms_body(j, carry):
        cs = [j * TERM_UNROLL + u for u in range(TERM_UNROLL)]
        offs = [pl.multiple_of(c * CHUNK, CHUNK) for c in cs]
        rkv = [conv_chunk(c) for c in cs]
        loras = [lora_ref[pl.ds(off, CHUNK), :] for off in offs]
        kks = [kc * k_k for (_, kc, _) in rkv]
        kk_ss = [block_sum(kk * kk) for kk in kks]
        bon_ss = [block_sum(rc * kc * r_k) for (rc, kc, _) in rkv]
        gates = [_mm3_split(_split2(_sigmoid(lo[:, 2 * LANE:3 * LANE])), (g2_ref[0], g2_ref[1])) for lo in loras]
        lw_ins = [_split2(jnp.tanh(lo[:, 0:LANE])) for lo in loras]
        la_ins = [_split2(lo[:, LANE:2 * LANE]) for lo in loras]
        lws = [[_mm3_split(x, (w2_ref[0, d], w2_ref[1, d])) for d in range(2)] for x in lw_ins]
        ags = [[_mm3_split(x, (a2_ref[0, d], a2_ref[1, d])) for d in range(2)] for x in la_ins]
        insts, where = [], []
        for u in range(TERM_UNROLL):
            rc, kc, vc = rkv[u]
            kk = kks[u] * lax.rsqrt(kk_ss[u] + EPS)
            bonus_ref[pl.ds(offs[u], CHUNK), :] = bon_ss[u] * vc
            gate_ref[pl.ds(offs[u], CHUNK), :] = gates[u]
            for d in range(2):
                lw = -RW_DECAY_SCALE * _sigmoid(w0[d] + lws[u][d])
                ag = _sigmoid(a0[d] + ags[u][d])
                kd = kc * (1.0 + (ag - 1.0) * k_a)
                kb = kk * ag
                for p in range(npp):
                    insts.append((pair(rc, p), pair(lw, p), pair(kd, p), -pair(kk, p), pair(kb, p), pair(vc, p),
                                  d == 1))
                    where.append((d, p, cs[u]))
        for (d, p, c), (t, g, w_end, q, y0) in zip(where, _rwkv_chunk_terms(insts)):
            tt_ref[d, p, c] = t
            tg_ref[d, p, c] = g
            tw_ref[d, p, c] = jnp.broadcast_to(w_end, (8, LANE))
            tq_ref[d, p, c] = q
            ty_ref[d, p, c] = y0
        return carry

    lax.fori_loop(0, n_chunks // TERM_UNROLL, terms_body, 0)

    chains = [(d, p) for d in range(2) for p in range(npp)]
    for d, p in chains:
        if zero_init:
            st_ref[d, p] = jnp.zeros((LANE, LANE), F32)
        else:
            z = jnp.zeros((64, 64), F32)
            st_ref[d, p] = jnp.concatenate([jnp.concatenate([s0_ref[0, d, 2 * p], z], axis=1),
                                            jnp.concatenate([z, s0_ref[0, d, 2 * p + 1]], axis=1)], axis=0)

    def scan_body(i, carry):
        cs = (i, n_chunks - 1 - i)
        ss = [st_ref[d, p] for d, p in chains]
        sb = [s.astype(BF16) for s in ss]
        ys = [_dot(tq_ref[d, p, cs[d]], b, _NT) + ty_ref[d, p, cs[d]] for (d, p), b in zip(chains, sb)]
        sn = [s * tw_ref[d, p, cs[d]][0:1] + _dot(b, tt_ref[d, p, cs[d]]) + tg_ref[d, p, cs[d]]
              for (d, p), s, b in zip(chains, ss, sb)]
        for (d, p), y, s in zip(chains, ys, sn):
            st_ref[d, p] = s
            out_ref = yf_ref if d == 0 else yb_ref
            out_ref[pl.ds(pl.multiple_of(cs[d] * CHUNK, CHUNK), CHUNK), p * LANE:(p + 1) * LANE] = y
        return carry

    lax.fori_loop(0, n_chunks, scan_body, 0)
    for d, p in chains:
        s = st_ref[d, p]
        sout_ref[0, d, 2 * p] = s[0:64, 0:64]
        sout_ref[0, d, 2 * p + 1] = s[64:LANE, 64:LANE]

    def post_body(j, carry):
        offs = [pl.multiple_of((j * TERM_UNROLL + u) * CHUNK, CHUNK) for u in range(TERM_UNROLL)]
        ys = [yf_ref[pl.ds(off, CHUNK), :] + yb_ref[pl.ds(off, CHUNK), :] for off in offs]
        mus = [block_sum(y) * (1.0 / 64) for y in ys]
        dlts = [y - mu for y, mu in zip(ys, mus)]
        vrs = [block_sum(dlt * dlt) * (1.0 / 64) for dlt in dlts]
        for off, dlt, var in zip(offs, dlts, vrs):
            yn = dlt * lax.rsqrt(var + RW_LN_EPS) * ln_w + ln_b
            y_ref[pl.ds(off, CHUNK), :] = (yn + bonus_ref[pl.ds(off, CHUNK), :]) * gate_ref[pl.ds(off, CHUNK), :]
        return carry

    lax.fori_loop(0, n_chunks // TERM_UNROLL, post_body, 0)


def _rwkv_mixer(proj, n_seq, seq_len, is_grid, prm, s0):
    n_pairs = RW_W // LANE
    n_chunks = seq_len // CHUNK
    npp = RW_PAIRS_PER_STEP
    w = npp * LANE
    assert n_chunks % TERM_UNROLL == 0 and n_pairs % npp == 0
    col = lambda cb: (lambda b, p: (b, cb // npp + p))
    par = lambda cb: (lambda b, p: (0, cb // npp + p))
    state_spec = pl.BlockSpec((1, 2, 2 * npp, 64, 64), lambda b, p: (b, 0, p, 0, 0))
    kernel = functools.partial(_rwkv_kernel, seq_len, is_grid, s0 is None)
    y, s_out = pl.pallas_call(
        kernel,
        grid=(n_seq, n_pairs // npp),
        in_specs=[pl.BlockSpec((seq_len, w), col(CB_R)),
                  pl.BlockSpec((seq_len, w), col(CB_K)),
                  pl.BlockSpec((seq_len, w), col(CB_V)),
                  pl.BlockSpec((seq_len, 3 * LANE), lambda b, p: (b, CB_LORA // 3)),
                  pl.BlockSpec((9, w), par(CB_R)),
                  pl.BlockSpec((9, w), par(CB_K)),
                  pl.BlockSpec((9, w), par(CB_V)),
                  pl.BlockSpec((2, 2, LANE, w), lambda b, p: (0, 0, 0, p)),
                  pl.BlockSpec((2, 2, LANE, w), lambda b, p: (0, 0, 0, p)),
                  pl.BlockSpec((2, LANE, w), lambda b, p: (0, 0, p)),
                  pl.BlockSpec((16, w), lambda b, p: (0, p))] + ([] if s0 is None else [state_spec]),
        out_specs=[pl.BlockSpec((seq_len, w), lambda b, p: (b, p)), state_spec],
        out_shape=[jax.ShapeDtypeStruct((n_seq * seq_len, RW_W), F32),
                   jax.ShapeDtypeStruct((n_seq, 2, 2 * n_pairs, 64, 64), F32)],
        scratch_shapes=[pltpu.VMEM((seq_len + 2 * CONV_PAD, 3 * w), F32)]
                       + [pltpu.VMEM((seq_len, w), F32)] * 4
                       + [pltpu.VMEM((2, npp, LANE, LANE), F32),
                          pltpu.VMEM((2, npp, n_chunks, LANE, LANE), BF16),
                          pltpu.VMEM((2, npp, n_chunks, LANE, LANE), F32),
                          pltpu.VMEM((2, npp, n_chunks, 8, LANE), F32),
                          pltpu.VMEM((2, npp, n_chunks, CHUNK, LANE), BF16),
                          pltpu.VMEM((2, npp, n_chunks, CHUNK, LANE), F32)],
        compiler_params=_cparams(2),
        name="rwkv_mixer",
    )(proj, proj, proj, proj, prm['conv'], prm['conv'], prm['conv'], prm['w2p'], prm['a2p'], prm['g2'],
      prm['vec'], *([] if s0 is None else [s0]))
    return y, s_out


def _gla_kernel(seq_len, zero_init, q_ref, k_ref, v_ref, og_ref, lgk_ref, gk2_ref, gvec_ref, *rest):
    s0_ref = None if zero_init else rest[0]
    y_ref, sout_ref, of_ref, ob_ref, st_ref, tq_ref, to_ref, ta_ref, tkv_ref = rest[0 if zero_init else 1:]
    n_chunks = seq_len // CHUNK
    npp = GLA_PAIRS
    gvec = gvec_ref[...]
    chains = [(d, p) for d in range(2) for p in range(npp)]

    def terms_body(j, carry):
        insts, where = [], []
        for u in range(TERM_UNROLL):
            c = j * TERM_UNROLL + u
            off = pl.multiple_of(c * CHUNK, CHUNK)
            lgk = _split2(lgk_ref[pl.ds(off, CHUNK), :])
            qc = q_ref[pl.ds(off, CHUNK), :] * GLA_Q_SCALE
            kc = k_ref[pl.ds(off, CHUNK), :]
            vc = v_ref[pl.ds(off, CHUNK), :]
            for d in range(2):
                x = _mm3_split(lgk, (gk2_ref[0, d], gk2_ref[1, d])) + gvec[d:d + 1, 0:GLA_QK_W]
                g = _log_sigmoid(x) * (1.0 / GLA_GATE_NORM)
                for p in range(npp):
                    qk = slice(p * LANE, (p + 1) * LANE)
                    insts.append((qc[:, qk], kc[:, qk], vc[:, 2 * p * LANE:2 * (p + 1) * LANE], g[:, qk], d == 1))
                    where.append((d, p, c))
        for (d, p, c), (qt, o0, a_col, kv) in zip(where, _gla_chunk_terms(insts)):
            tq_ref[d, p, c] = qt
            to_ref[d, p, c] = o0
            ta_ref[d, p, c] = a_col
            tkv_ref[d, p, c] = kv
        return carry

    lax.fori_loop(0, n_chunks // TERM_UNROLL, terms_body, 0)

    for d, p in chains:
        if zero_init:
            st_ref[d, p] = jnp.zeros((LANE, 2 * LANE), F32)
        else:
            z = jnp.zeros((64, LANE), F32)
            st_ref[d, p] = jnp.concatenate([jnp.concatenate([s0_ref[0, d, 2 * p], z], axis=1),
                                            jnp.concatenate([z, s0_ref[0, d, 2 * p + 1]], axis=1)], axis=0)

    def scan_body(i, carry):
        cs = (i, n_chunks - 1 - i)
        ss = [st_ref[d, p] for d, p in chains]
        os_ = [_dot(tq_ref[d, p, cs[d]], s.astype(BF16)) + to_ref[d, p, cs[d]] for (d, p), s in zip(chains, ss)]
        for (d, p), s, o in zip(chains, ss, os_):
            a_col = ta_ref[d, p, cs[d]]
            st_ref[d, p] = s * jnp.concatenate([a_col, a_col], axis=1) + tkv_ref[d, p, cs[d]]
            out_ref = of_ref if d == 0 else ob_ref
            out_ref[pl.ds(pl.multiple_of(cs[d] * CHUNK, CHUNK), CHUNK), 2 * p * LANE:2 * (p + 1) * LANE] = o
        return carry

    lax.fori_loop(0, n_chunks, scan_body, 0)
    for d, p in chains:
        s = st_ref[d, p]
        sout_ref[0, d, 2 * p] = s[0:64, 0:LANE]
        sout_ref[0, d, 2 * p + 1] = s[64:LANE, LANE:2 * LANE]

    def post_body(c, carry):
        off = pl.multiple_of(c * CHUNK, CHUNK)
        for h in range(2 * npp):
            hs = slice(h * LANE, (h + 1) * LANE)
            o = of_ref[pl.ds(off, CHUNK), hs] + ob_ref[pl.ds(off, CHUNK), hs]
            gate = _silu(og_ref[pl.ds(off, CHUNK), hs])
            y_ref[pl.ds(off, CHUNK), hs] = _rmsnorm_rows(o) * gvec[2:3, hs] * gate
        return carry

    lax.fori_loop(0, n_chunks, post_body, 0)


def _gla_mixer(proj, n_seq, seq_len, prm, s0):
    npp = GLA_PAIRS
    n_heads = 2 * npp
    n_chunks = seq_len // CHUNK
    assert n_chunks % TERM_UNROLL == 0
    state_spec = pl.BlockSpec((1, 2, n_heads, 64, LANE), lambda b: (b, 0, 0, 0, 0))
    kernel = functools.partial(_gla_kernel, seq_len, s0 is None)
    y, s_out = pl.pallas_call(
        kernel,
        grid=(n_seq,),
        in_specs=[pl.BlockSpec((seq_len, GLA_QK_W), lambda b: (b, CB_GQ * LANE // GLA_QK_W)),
                  pl.BlockSpec((seq_len, GLA_QK_W), lambda b: (b, CB_GK * LANE // GLA_QK_W)),
                  pl.BlockSpec((seq_len, GLA_V_W), lambda b: (b, CB_GV * LANE // GLA_V_W)),
                  pl.BlockSpec((seq_len, GLA_V_W), lambda b: (b, CB_OG * LANE // GLA_V_W)),
                  pl.BlockSpec((seq_len, LANE), lambda b: (b, CB_LGK)),
                  pl.BlockSpec((2, 2, LANE, GLA_QK_W), lambda b: (0, 0, 0, 0)),
                  pl.BlockSpec((8, GLA_V_W), lambda b: (0, 0))] + ([] if s0 is None else [state_spec]),
        out_specs=[pl.BlockSpec((seq_len, GLA_V_W), lambda b: (b, 0)), state_spec],
        out_shape=[jax.ShapeDtypeStruct((n_seq * seq_len, GLA_V_W), F32),
                   jax.ShapeDtypeStruct((n_seq, 2, n_heads, 64, LANE), F32)],
        scratch_shapes=[pltpu.VMEM((seq_len, GLA_V_W), F32)] * 2
                       + [pltpu.VMEM((2, npp, LANE, 2 * LANE), F32),
                          pltpu.VMEM((2, npp, n_chunks, CHUNK, LANE), BF16),
                          pltpu.VMEM((2, npp, n_chunks, CHUNK, 2 * LANE), F32),
                          pltpu.VMEM((2, npp, n_chunks, LANE, LANE), F32),
                          pltpu.VMEM((2, npp, n_chunks, LANE, 2 * LANE), F32)],
        compiler_params=_cparams(1),
        name="gla_mixer",
    )(proj, proj, proj, proj, proj, prm['gk2p'], prm['gvec'], *([] if s0 is None else [s0]))
    return y, s_out


OUT_TM = 512
ROUTE_NEG = -1e30
LANE_GROUP0 = N_EXPERTS


def _route(logits):
    lane = _iota(logits.shape, 1)
    lane_f = lane.astype(F32)
    big = float(LANE)
    is_g = (lane >= LANE_GROUP0) & (lane < LANE_GROUP0 + 4)
    gmax = jnp.max(jnp.where(is_g, logits, ROUTE_NEG), axis=-1, keepdims=True)
    gidx = jnp.min(jnp.where(is_g & (logits == gmax), lane_f, big), axis=-1, keepdims=True) - LANE_GROUP0
    gsum = jnp.sum(jnp.where(is_g, jnp.exp(jnp.minimum(logits - gmax, 0.0)), 0.0), axis=-1, keepdims=True)
    g_w = 1.0 / gsum
    in_grp = (lane < N_EXPERTS) & ((lane // 4).astype(F32) == gidx)
    m1 = jnp.max(jnp.where(in_grp, logits, ROUTE_NEG), axis=-1, keepdims=True)
    i1 = jnp.min(jnp.where(in_grp & (logits == m1), lane_f, big), axis=-1, keepdims=True)
    rest = in_grp & (lane_f != i1)
    m2 = jnp.max(jnp.where(rest, logits, ROUTE_NEG), axis=-1, keepdims=True)
    i2 = jnp.min(jnp.where(rest & (logits == m2), lane_f, big), axis=-1, keepdims=True)
    t = jnp.exp(m2 - m1)
    w1 = g_w / (1.0 + t)
    return jnp.where(lane_f == i1, w1, 0.0) + jnp.where(lane_f == i2, w1 * t, 0.0)


def _outproj_kernel(yr_ref, yg_ref, x_ref, mod_ref, wo_ref, g_ref, wr_ref, br_ref, x1_ref, h2_ref, cmb_ref):
    m = mod_ref[0]
    mix = _mm(yr_ref[...], wo_ref[0:RW_W, :]) + _mm(yg_ref[...], wo_ref[RW_W:RW_W + GLA_V_W, :])
    x1 = x_ref[...] + m[2:3] * mix
    h2 = _rmsnorm_rows(x1) * g_ref[...] * (1.0 + m[4:5]) + m[3:4]
    x1_ref[...] = x1
    h2_ref[...] = h2.astype(BF16)
    cmb_ref[...] = _route(_mm3_split(_split2(h2), (wr_ref[0], wr_ref[1])) + br_ref[...])


def _out_projection(y_rw, y_gla, x2d, mod, w_out, norm_g, w_route, b_route, seq_len, ctx_row):
    n = x2d.shape[0]
    tile = lambda w: pl.BlockSpec((OUT_TM, w), lambda i: (i, 0))
    full = lambda a: pl.BlockSpec(a.shape, lambda i: (0,) * a.ndim)
    return pl.pallas_call(
        _outproj_kernel,
        grid=(n // OUT_TM,),
        in_specs=[tile(RW_W), tile(GLA_V_W), tile(D_MODEL),
                  pl.BlockSpec((1, N_MOD, D_MODEL), _mod_index(seq_len, OUT_TM, ctx_row)),
                  full(w_out), full(norm_g), full(w_route), full(b_route)],
        out_specs=[tile(D_MODEL), tile(D_MODEL), tile(LANE)],
        out_shape=[jax.ShapeDtypeStruct((n, D_MODEL), F32), jax.ShapeDtypeStruct((n, D_MODEL), BF16),
                   jax.ShapeDtypeStruct((n, LANE), F32)],
        compiler_params=_cparams(1),
        name="out_proj_router",
    )(y_rw, y_gla, x2d, mod, w_out, norm_g, w_route, b_route)


MOE_TM = 512
MOE_RB = 128
MOE_INTERLEAVE = 4
SLOT_ALIGN = 16
MOE_SLOTS = 2 * MOE_TM + N_EXPERTS * SLOT_ALIGN + MOE_RB


def _moe_kernel(h2_ref, cmb_ref, x1_ref, mod_ref, w1_ref, w3_ref, w2_ref, fg_ref, y_ref, xs_ref, ys_ref):
    cmb = cmb_ref[...]
    lane = _iota(cmb.shape, 1).astype(F32)
    sel = cmb > 0.0
    sel01 = jnp.where(sel, 1.0, 0.0).astype(BF16)
    before = (_iota((MOE_TM, MOE_TM), 0) > _iota((MOE_TM, MOE_TM), 1)).astype(BF16)
    pos = _dot(before, sel01)
    cnt = pos[MOE_TM - 1:MOE_TM] + sel01[MOE_TM - 1:MOE_TM].astype(F32)
    seg = jnp.floor((cnt + (SLOT_ALIGN - 1)) * (1.0 / SLOT_ALIGN))
    lower_experts = (_iota((LANE, LANE), 0) < _iota((LANE, LANE), 1)).astype(BF16)
    start = _dot(jnp.broadcast_to(seg, (8, LANE)).astype(BF16), lower_experts)[0:1] * SLOT_ALIGN
    n_blk = jnp.floor((cnt + (MOE_RB - 1)) * (1.0 / MOE_RB)).astype(jnp.int32)
    start_i = start.astype(jnp.int32)
    cnt_i = cnt.astype(jnp.int32)
    slot = start + pos
    e_a = jnp.min(jnp.where(sel, lane, float(LANE)), axis=-1, keepdims=True)
    e_b = jnp.max(jnp.where(sel, lane, -1.0), axis=-1, keepdims=True)
    pick = lambda e, x: jnp.sum(jnp.where(lane == e, x, 0.0), axis=-1, keepdims=True)
    slot_a, w_a = pick(e_a, slot), pick(e_a, cmb)
    slot_b = jnp.where(e_b != e_a, pick(e_b, slot), -1.0)
    w_b = pick(e_b, cmb)

    slots_t = jnp.where(lane == 0.0, slot_a, jnp.where(lane == 1.0, slot_b, -1.0)).T
    row_slot = _iota((MOE_SLOTS, MOE_TM), 0).astype(F32)
    gather = jnp.where((row_slot == slots_t[0:1]) | (row_slot == slots_t[1:2]), 1.0, 0.0).astype(BF16)
    xs_ref[...] = _dot(gather, h2_ref[...]).astype(BF16)
    ys_ref[...] = jnp.zeros_like(ys_ref)

    row_in_blk = _iota((MOE_RB, D_MODEL), 0)

    def expert_blocks(experts, r0s, ends):
        xbs = [xs_ref[pl.ds(r0, MOE_RB), :] for r0 in r0s]
        gates = [_dot(xb, w3_ref[e]) for xb, e in zip(xbs, experts)]
        ups = [_dot(xb, w1_ref[e]) for xb, e in zip(xbs, experts)]
        acts = [(_silu(g) * u).astype(BF16) for g, u in zip(gates, ups)]
        outs = [_dot(a, w2_ref[e]) for a, e in zip(acts, experts)]
        for r0, end, out in zip(r0s, ends, outs):
            keep = row_in_blk + r0 >= end
            ys_ref[pl.ds(r0, MOE_RB), :] = jnp.where(keep, ys_ref[pl.ds(r0, MOE_RB), :], out.astype(BF16))

    seg_start = [pl.multiple_of(start_i[0, e], SLOT_ALIGN) for e in range(N_EXPERTS)]
    seg_end = [seg_start[e] + cnt_i[0, e] for e in range(N_EXPERTS)]
    for e0 in range(0, N_EXPERTS, MOE_INTERLEAVE):
        es = list(range(e0, e0 + MOE_INTERLEAVE))
        expert_blocks(es, [seg_start[e] for e in es], [seg_end[e] for e in es])
    for e in range(N_EXPERTS):
        def extra_block(b, carry, e=e):
            expert_blocks([e], [pl.multiple_of(seg_start[e] + b * MOE_RB, SLOT_ALIGN)], [seg_end[e]])
            return carry

        lax.fori_loop(1, n_blk[0, e], extra_block, 0)

    col_slot = _iota((MOE_TM, MOE_SLOTS), 1).astype(F32)
    scatter = (jnp.where(col_slot == slot_a, w_a, 0.0) + jnp.where(col_slot == slot_b, w_b, 0.0)).astype(BF16)
    x2 = x1_ref[...] + mod_ref[0][5:6] * _dot(scatter, ys_ref[...])
    y_ref[...] = _rmsnorm_rows(x2) * fg_ref[...]


def _moe(h2, cmb, x1, mod, w1, w3, w2, final_g, seq_len, ctx_row):
    n = h2.shape[0]
    tile = lambda w: pl.BlockSpec((MOE_TM, w), lambda i: (i, 0))
    resident = lambda a: pl.BlockSpec(a.shape, lambda i: (0,) * a.ndim, pipeline_mode=pl.Buffered(1))
    return pl.pallas_call(
        _moe_kernel,
        grid=(n // MOE_TM,),
        in_specs=[tile(D_MODEL), tile(LANE), tile(D_MODEL),
                  pl.BlockSpec((1, N_MOD, D_MODEL), _mod_index(seq_len, MOE_TM, ctx_row)),
                  resident(w1), resident(w3), resident(w2),
                  pl.BlockSpec((1, D_MODEL), lambda i: (0, 0))],
        out_specs=tile(D_MODEL),
        out_shape=jax.ShapeDtypeStruct((n, D_MODEL), F32),
        scratch_shapes=[pltpu.VMEM((MOE_SLOTS, D_MODEL), BF16), pltpu.VMEM((MOE_SLOTS, D_MODEL), BF16)],
        compiler_params=_cparams(1),
        name="moe_experts",
    )(h2, cmb, x1, mod, w1, w3, w2, final_g)


def _pad_rows(x, rows):
    return jnp.pad(x, ((0, rows - x.shape[0]),) + ((0, 0),) * (x.ndim - 1))


def _pack_params(l, w_in, rw_conv, rw_w0, rw_w2, rw_a0, rw_a2, rw_g2, rw_k_k, rw_k_a, rw_r_k, rw_ln_w, rw_ln_b,
                 gla_gk2, gla_gk_b, gla_norm_g, moe_w_group, moe_b_group, moe_w_expert, moe_b_expert):
    wi = w_in[l]
    z = lambda n: jnp.zeros((D_MODEL, n), F32)
    w_in_p = jnp.concatenate([wi[:, 0:1920], wi[:, 3456:3488], z(LANE - 32), wi[:, 1920:3456]], axis=1).astype(BF16)
    z64 = jnp.zeros((64, RW_W), F32)
    w2p = jnp.stack([jnp.concatenate([rw_w2[l, 0], z64], 0), jnp.concatenate([z64, rw_w2[l, 1]], 0)])
    a2p = jnp.stack([jnp.concatenate([rw_a2[l, 0], z64], 0), jnp.concatenate([z64, rw_a2[l, 1]], 0)])
    vec = _pad_rows(jnp.stack([rw_w0[l, 0], rw_w0[l, 1], rw_a0[l, 0], rw_a0[l, 1], rw_k_k[l], rw_k_a[l],
                               rw_r_k[l].reshape(RW_W), rw_ln_w[l], rw_ln_b[l]]), 16)
    rw = {'conv': rw_conv[l].reshape(9, 3 * RW_W), 'w2p': _split_param(w2p), 'a2p': _split_param(a2p),
          'g2': _split_param(rw_g2[l]), 'vec': vec}
    gk2p = jnp.stack([_pad_rows(gla_gk2[l, 0], LANE),
                      _pad_rows(jnp.concatenate([jnp.zeros((16, GLA_QK_W), F32), gla_gk2[l, 1]], 0), LANE)])
    gk_b = jnp.pad(gla_gk_b[l], ((0, 0), (0, GLA_V_W - GLA_QK_W)))
    gvec = _pad_rows(jnp.concatenate([gk_b, jnp.tile(gla_norm_g[l], GLA_V_W // LANE)[None]], axis=0), 8)
    gla = {'gk2p': _split_param(gk2p), 'gvec': gvec}
    w_route = _split_param(jnp.concatenate([moe_w_expert[l], moe_w_group[l], z(LANE - N_EXPERTS - 4)], axis=1))
    b_route = jnp.concatenate([moe_b_expert[l], moe_b_group[l], jnp.zeros((LANE - N_EXPERTS - 4,), F32)])[None]
    return w_in_p, rw, gla, w_route, b_route


def _trunk_pass(x, mod, ctx_row, s_rw0, s_gla0, is_grid, pk, norm1_g, norm2_g, w_out, w1, w3, w2, final_g):
    n_seq, seq_len, _ = x.shape
    x2d = x.reshape(n_seq * seq_len, D_MODEL)
    w_in_p, rw, gla, w_route, b_route = pk
    proj = _in_projection(x2d, mod, norm1_g, w_in_p, seq_len, ctx_row)
    y_rw, s_rw = _rwkv_mixer(proj, n_seq, seq_len, is_grid, rw, s_rw0)
    y_gla, s_gla = _gla_mixer(proj, n_seq, seq_len, gla, s_gla0)
    x1, h2, cmb = _out_projection(y_rw, y_gla, x2d, mod, w_out, norm2_g, w_route, b_route, seq_len, ctx_row)
    y = _moe(h2, cmb, x1, mod, w1, w3, w2, final_g, seq_len, ctx_row)
    return y.reshape(n_seq, seq_len, D_MODEL), s_rw, s_gla


def kernel(x_prompt, x_sample, state_rwkv, state_gla, c, c_ctx, norm1_g, norm2_g, w_ada, b_ada, w_in, w_out,
           rw_conv, rw_w0, rw_w2, rw_a0, rw_a2, rw_g2, rw_k_k, rw_k_a, rw_r_k, rw_ln_w, rw_ln_b,
           gla_gk2, gla_gk_b, gla_norm_g, moe_w_group, moe_b_group, moe_w_expert, moe_b_expert,
           moe_w1, moe_w3, moe_w2, final_g):
    depth = w_in.shape[0]
    assert depth == 1, "the packed layout below handles the single-layer trunk of this problem"
    l = 0
    n_dec = x_sample.shape[0]
    ctx_row = n_dec
    cond8 = _pad_rows(jnp.concatenate([c, c_ctx[None]], axis=0), 8)
    mod = _modulation(cond8, w_ada[l], b_ada[l][None]).reshape(8, N_MOD, D_MODEL)
    pk = _pack_params(l, w_in, rw_conv, rw_w0, rw_w2, rw_a0, rw_a2, rw_g2, rw_k_k, rw_k_a, rw_r_k, rw_ln_w,
                      rw_ln_b, gla_gk2, gla_gk_b, gla_norm_g, moe_w_group, moe_b_group, moe_w_expert,
                      moe_b_expert)
    shared = (pk, norm1_g[l][None], norm2_g[l][None], w_out[l].astype(BF16), moe_w1[l].astype(BF16),
              moe_w3[l].astype(BF16), moe_w2[l].astype(BF16), final_g[None])

    y_prompt, s_rw, s_gla = _trunk_pass(x_prompt, mod, ctx_row, None, None, False, *shared)
    y_sample, _, _ = _trunk_pass(x_sample, mod, None, state_rwkv[:, l], state_gla[:, l], True, *shared)
    return (y_prompt, y_sample, s_rw[:, None], s_gla[:, None])
```

```python
import functools

import jax
import jax.numpy as jnp
from jax import lax
from jax.experimental import pallas as pl
from jax.experimental.pallas import tpu as pltpu

F32 = jnp.float32
BF16 = jnp.bfloat16

D_MODEL = 1024
RW_W = 512
GLA_V_W = 512
GLA_QK_W = 256
N_EXPERTS = 16
D_EXPERT = 256
N_MOD = 6
EPS = 1e-6
RW_LN_EPS = 64e-5
RW_DECAY_SCALE = 0.606531
GLA_GATE_NORM = 16.0
GLA_Q_SCALE = 64 ** -0.5
GRID_W = 64

LANE = 128
CHUNK = 64
CONV_PAD = 128
TERM_UNROLL = 4
RW_PAIRS_PER_STEP = 2
GLA_PAIRS = 2
D_PROJ = 28 * LANE
VMEM_LIMIT = 56 * 1024 * 1024

CB_R, CB_K, CB_V, CB_LORA, CB_LGK, CB_GQ, CB_GK, CB_GV, CB_OG = 0, 4, 8, 12, 15, 16, 18, 20, 24

_NN = (((1,), (0,)), ((), ()))
_NT = (((1,), (1,)), ((), ()))
_TN = (((0,), (0,)), ((), ()))


def _dot(a, b, dims=_NN):
    return lax.dot_general(a, b, dims, preferred_element_type=F32)


def _mm(a, b, dims=_NN):
    return _dot(a.astype(BF16), b.astype(BF16), dims)


def _split2(x):
    hi = x.astype(BF16)
    lo = (x - hi.astype(F32)).astype(BF16)
    return hi, lo


def _mm3(a, b, dims=_NN):
    ah, al = _split2(a)
    bh, bl = _split2(b)
    return _dot(ah, bh, dims) + _dot(ah, bl, dims) + _dot(al, bh, dims)


def _mm3_split(a_split, b_split, dims=_NN):
    (ah, al), (bh, bl) = a_split, b_split
    return _dot(ah, bh, dims) + _dot(ah, bl, dims) + _dot(al, bh, dims)


def _split_param(w):
    return jnp.stack(_split2(w))


def _mm_01_lhs(a01, b, dims=_NN):
    b1, b2 = _split2(b)
    return _dot(a01, b1, dims) + _dot(a01, b2, dims)


def _sigmoid(x):
    return 0.5 * jnp.tanh(0.5 * x) + 0.5


def _silu(x):
    return x * _sigmoid(x)


def _log_sigmoid(x):
    return jnp.minimum(x, 0.0) - jnp.log(1.0 + jnp.exp(-jnp.abs(x)))


def _iota(shape, dim):
    return lax.broadcasted_iota(jnp.int32, shape, dim)


def _cparams(n_axes):
    return pltpu.CompilerParams(dimension_semantics=("arbitrary",) * n_axes, vmem_limit_bytes=VMEM_LIMIT)


MOD_TN = 768


def _mod_kernel(c_ref, w_ref, b_ref, o_ref):
    o_ref[...] = _mm3(_silu(c_ref[...]), w_ref[...]) + b_ref[...]


def _modulation(cond8, w_ada, b_ada):
    n = w_ada.shape[1]
    return pl.pallas_call(
        _mod_kernel,
        grid=(n // MOD_TN,),
        in_specs=[pl.BlockSpec((8, D_MODEL), lambda j: (0, 0)),
                  pl.BlockSpec((D_MODEL, MOD_TN), lambda j: (0, j)),
                  pl.BlockSpec((1, MOD_TN), lambda j: (0, j))],
        out_specs=pl.BlockSpec((8, MOD_TN), lambda j: (0, j)),
        out_shape=jax.ShapeDtypeStruct((8, n), F32),
        compiler_params=_cparams(1),
        name="adaln_mod",
    )(cond8, w_ada, b_ada)


PROJ_TM = 512


def _rmsnorm_rows(x):
    return x * lax.rsqrt(jnp.mean(x * x, axis=-1, keepdims=True) + EPS)


def _inproj_kernel(x_ref, mod_ref, g_ref, w_ref, o_ref):
    m = mod_ref[0]
    h = _rmsnorm_rows(x_ref[...]) * g_ref[...] * (1.0 + m[1:2]) + m[0:1]
    o_ref[...] = _mm(h, w_ref[...])


def _mod_index(seq_len, tm, ctx_row):
    if ctx_row is not None:
        return lambda i: (ctx_row, 0, 0)
    return lambda i: (i // (seq_len // tm), 0, 0)


def _in_projection(x2d, mod, norm_g, w_in_p, seq_len, ctx_row):
    n = x2d.shape[0]
    return pl.pallas_call(
        _inproj_kernel,
        grid=(n // PROJ_TM,),
        in_specs=[pl.BlockSpec((PROJ_TM, D_MODEL), lambda i: (i, 0)),
                  pl.BlockSpec((1, N_MOD, D_MODEL), _mod_index(seq_len, PROJ_TM, ctx_row)),
                  pl.BlockSpec((1, D_MODEL), lambda i: (0, 0)),
                  pl.BlockSpec((D_MODEL, D_PROJ), lambda i: (0, 0))],
        out_specs=pl.BlockSpec((PROJ_TM, D_PROJ), lambda i: (i, 0)),
        out_shape=jax.ShapeDtypeStruct((n, D_PROJ), F32),
        compiler_params=_cparams(1),
        name="in_proj",
    )(x2d, mod, norm_g, w_in_p)


def _time_masks(reverse):
    r = _iota((2 * CHUNK, 2 * CHUNK), 0) % CHUNK
    c = _iota((2 * CHUNK, 2 * CHUNK), 1) % CHUNK
    if reverse:
        return r < c, r <= c
    return r > c, r >= c


def _cumsum_matrix(reverse):
    r = _iota((CHUNK, CHUNK), 0)
    c = _iota((CHUNK, CHUNK), 1)
    tri = (r <= c) if reverse else (r >= c)
    return tri.astype(BF16)


def _stack_heads(x, half):
    m0 = _iota(x.shape, 1) < half
    return jnp.concatenate([jnp.where(m0, x, 0.0), jnp.where(m0, 0.0, x)], axis=0)


def _head_sums(x):
    parts = []
    for p in range(x.shape[1] // LANE):
        xp = x[:, p * LANE:(p + 1) * LANE]
        m0 = _iota(xp.shape, 1) < 64
        s0 = jnp.sum(jnp.where(m0, xp, 0.0), axis=-1, keepdims=True)
        s1 = jnp.sum(jnp.where(m0, 0.0, xp), axis=-1, keepdims=True)
        parts.append(jnp.where(m0, s0, s1))
    return parts[0] if len(parts) == 1 else jnp.concatenate(parts, axis=1)


def _rwkv_chunk_terms(insts):
    c = CHUNK
    step_row = _iota((c, LANE), 0)
    step_col = _iota((c, LANE), 1) % c
    eye_w = (step_row == step_col).astype(F32)
    same_head = (_iota((LANE, LANE), 0) // 64) == (_iota((LANE, LANE), 1) // 64)
    stack_bf = lambda x: _stack_heads(x, 64).astype(BF16)
    cums = [_mm_01_lhs(_cumsum_matrix(rev), lw) for (_, lw, _, _, _, _, rev) in insts]
    pre = []
    for (r, lw, kd, a, b, v, rev), cum in zip(insts, cums):
        end = cum[0:1] if rev else cum[c - 1:c]
        inv_w = jnp.exp(-cum)
        rem_w = jnp.exp(end - cum)
        a_t = a * jnp.exp(cum - lw)
        r_t = r * jnp.exp(cum)
        bk_s = jnp.concatenate([stack_bf(b * inv_w), stack_bf(kd * inv_w)], axis=0)
        bkh = jnp.concatenate([b * rem_w, kd * rem_w], axis=0).astype(BF16)
        pre.append((a_t, r_t, bk_s, bkh, v, jnp.exp(end)))
    ms = [_dot(jnp.concatenate([a_t, r_t], axis=0).astype(BF16), bk_s, _NT) for (a_t, r_t, bk_s, _, _, _) in pre]
    mats = []
    for m, (_, _, _, _, _, _, rev) in zip(ms, insts):
        strict = (step_row < step_col) if rev else (step_row > step_col)
        incl = (step_row <= step_col) if rev else (step_row >= step_col)
        l_ab = jnp.where(strict, m[0:c, 0:LANE], 0.0)
        l_akrk = jnp.concatenate([jnp.where(strict, m[0:c, LANE:2 * LANE], 0.0),
                                  jnp.where(incl, m[c:2 * c, LANE:2 * LANE], 0.0)], axis=0).astype(BF16)
        m_rb = jnp.where(incl, m[c:2 * c, 0:LANE], 0.0).astype(BF16)
        mats.append((l_ab, l_akrk, m_rb))
    lvs = [_dot(l_akrk, stack_bf(pr[4])) for (_, l_akrk, _), pr in zip(mats, pre)]
    ps = [eye_w + l_ab for (l_ab, _, _) in mats]
    lps = [_dot(l_ab.astype(BF16), stack_bf(l_ab)) for (l_ab, _, _) in mats]
    for level in range(1, 6):
        if level < 5:
            xs = [_dot(lp.astype(BF16), jnp.concatenate([stack_bf(p), stack_bf(lp)], axis=1))
                  for lp, p in zip(lps, ps)]
            ps = [p + x[:, 0:LANE] for p, x in zip(ps, xs)]
            lps = [x[:, LANE:2 * LANE] for x in xs]
        else:
            ps = [p + _dot(lp.astype(BF16), stack_bf(p)) for lp, p in zip(lps, ps)]
    pxs = [_dot(p.astype(BF16), jnp.concatenate([stack_bf(pr[0]), stack_bf(lv[0:c])], axis=1))
           for p, pr, lv in zip(ps, pre, lvs)]
    mzs = [_dot(mt[2], jnp.concatenate([stack_bf(px[:, 0:LANE]), stack_bf(px[:, LANE:2 * LANE])], axis=1))
           for mt, px in zip(mats, pxs)]
    ts = [_dot(px[:, 0:LANE].astype(BF16), pr[3][0:c], _TN) for px, pr in zip(pxs, pre)]
    gs = [_dot(jnp.concatenate([px[:, LANE:2 * LANE], pr[4]], axis=0).astype(BF16), pr[3], _TN)
          for px, pr in zip(pxs, pre)]
    out = []
    for pr, lv, mz, t, g in zip(pre, lvs, mzs, ts, gs):
        q = pr[1] + mz[:, 0:LANE]
        y0 = mz[:, LANE:2 * LANE] + lv[c:2 * c]
        out.append((jnp.where(same_head, t, 0.0).astype(BF16), jnp.where(same_head, g, 0.0), pr[5],
                    q.astype(BF16), y0))
    return out


def _gla_chunk_terms(insts):
    c = CHUNK
    step_row = _iota((c, LANE), 0)
    step_col = _iota((c, LANE), 1) % c
    same_head = (_iota((LANE, 2 * LANE), 0) // 64) == (_iota((LANE, 2 * LANE), 1) // LANE)
    cums = [_mm_01_lhs(_cumsum_matrix(rev), g) for (_, _, _, g, rev) in insts]
    pre = []
    for (q, k, v, g, rev), cum in zip(insts, cums):
        end = cum[0:1] if rev else cum[c - 1:c]
        qt = (q * jnp.exp(cum)).astype(BF16)
        k_s = _stack_heads(k * jnp.exp(-cum), 64).astype(BF16)
        kh = (k * jnp.exp(end - cum)).astype(BF16)
        a_col = jnp.broadcast_to(jnp.exp(end), (LANE, LANE)).T
        pre.append((qt, k_s, kh, v.astype(BF16), _stack_heads(v, LANE).astype(BF16), a_col))
    atts = [_dot(pr[0], pr[1], _NT) for pr in pre]
    atts = [jnp.where((step_row <= step_col) if inst[4] else (step_row >= step_col), att, 0.0).astype(BF16)
            for att, inst in zip(atts, insts)]
    o0s = [_dot(att, pr[4]) for att, pr in zip(atts, pre)]
    kvs = [jnp.where(same_head, _dot(pr[2], pr[3], _TN), 0.0) for pr in pre]
    return [(pr[0], o0, pr[5], kv) for pr, o0, kv in zip(pre, o0s, kvs)]


def _rwkv_kernel(seq_len, is_grid, zero_init, r_ref, k_ref, v_ref, lora_ref, cwr_ref, cwk_ref, cwv_ref, w2_ref,
                 a2_ref, g2_ref, vec_ref, *rest):
    s0_ref = None if zero_init else rest[0]
    (y_ref, sout_ref, pad_ref, bonus_ref, gate_ref, yf_ref, yb_ref, st_ref, tt_ref, tg_ref, tw_ref, tq_ref,
     ty_ref) = rest[0 if zero_init else 1:][:13]
    left_ref, right_ref = rest[-2:] if is_grid else (None, None)
    n_chunks = seq_len // CHUNK
    npp = RW_PAIRS_PER_STEP
    w = npp * LANE
    pair = lambda x, p: x[:, p * LANE:(p + 1) * LANE]
    vec = vec_ref[...]
    w0 = (vec[0:1], vec[1:2])
    a0 = (vec[2:3], vec[3:4])
    k_k, k_a, r_k, ln_w, ln_b = vec[4:5], vec[5:6], vec[6:7], vec[7:8], vec[8:9]
    block_sum = _head_sums

    zeros = jnp.zeros((CONV_PAD, 3 * w), F32)
    pad_ref[0:CONV_PAD, :] = zeros
    pad_ref[CONV_PAD + seq_len:2 * CONV_PAD + seq_len, :] = zeros
    pad_ref[CONV_PAD:CONV_PAD + seq_len, 0:w] = r_ref[...]
    pad_ref[CONV_PAD:CONV_PAD + seq_len, w:2 * w] = k_ref[...]
    pad_ref[CONV_PAD:CONV_PAD + seq_len, 2 * w:3 * w] = v_ref[...]
    cw = jnp.concatenate([cwr_ref[...], cwk_ref[...], cwv_ref[...]], axis=1)

    if is_grid:
        col = _iota((CHUNK, 3 * w), 0)
        left_ref[0:CONV_PAD, :] = zeros
        left_ref[CONV_PAD + seq_len:2 * CONV_PAD + seq_len, :] = zeros
        right_ref[0:CONV_PAD, :] = zeros
        right_ref[CONV_PAD + seq_len:2 * CONV_PAD + seq_len, :] = zeros

        def shift_body(c, carry):
            base = pl.multiple_of(CONV_PAD + c * CHUNK, CHUNK)
            win = pad_ref[pl.ds(base - 8, CHUNK + 16), :]
            left_ref[pl.ds(base, CHUNK), :] = jnp.where(col >= 1, win[7:7 + CHUNK], 0.0)
            right_ref[pl.ds(base, CHUNK), :] = jnp.where(col <= GRID_W - 2, win[9:9 + CHUNK], 0.0)
            return carry

        lax.fori_loop(0, n_chunks, shift_body, 0)

    def conv_chunk(c):
        base = pl.multiple_of(CONV_PAD + c * CHUNK, CHUNK)
        acc = jnp.zeros((CHUNK, 3 * w), F32)
        if is_grid:
            for di in (-1, 0, 1):
                row = pl.ds(base + di * GRID_W, CHUNK)
                for dj, src in ((-1, left_ref), (0, pad_ref), (1, right_ref)):
                    tap = (di + 1) * 3 + dj + 1
                    acc = acc + src[row, :] * cw[tap:tap + 1]
        else:
            win = pad_ref[pl.ds(base - 8, CHUNK + 16), :]
            for dj in (-1, 0, 1):
                acc = acc + win[8 + dj:8 + dj + CHUNK] * cw[4 + dj:5 + dj]
        return acc[:, 0:w], acc[:, w:2 * w], acc[:, 2 * w:3 * w]

    def terms_body(j, carry):
        cs = [j * TERM_UNROLL + u for u in range(TERM_UNROLL)]
        offs = [pl.multiple_of(c * CHUNK, CHUNK) for c in cs]
        rkv = [conv_chunk(c) for c in cs]
        loras = [lora_ref[pl.ds(off, CHUNK), :] for off in offs]
        kks = [kc * k_k for (_, kc, _) in rkv]
        kk_ss = [block_sum(kk * kk) for kk in kks]
        bon_ss = [block_sum(rc * kc * r_k) for (rc, kc, _) in rkv]
        gates = [_mm3_split(_split2(_sigmoid(lo[:, 2 * LANE:3 * LANE])), (g2_ref[0], g2_ref[1])) for lo in loras]
        lw_ins = [_split2(jnp.tanh(lo[:, 0:LANE])) for lo in loras]
        la_ins = [_split2(lo[:, LANE:2 * LANE]) for lo in loras]
        lws = [[_mm3_split(x, (w2_ref[0, d], w2_ref[1, d])) for d in range(2)] for x in lw_ins]
        ags = [[_mm3_split(x, (a2_ref[0, d], a2_ref[1, d])) for d in range(2)] for x in la_ins]
        insts, where = [], []
        for u in range(TERM_UNROLL):
            rc, kc, vc = rkv[u]
            kk = kks[u] * lax.rsqrt(kk_ss[u] + EPS)
            bonus_ref[pl.ds(offs[u], CHUNK), :] = bon_ss[u] * vc
            gate_ref[pl.ds(offs[u], CHUNK), :] = gates[u]
            for d in range(2):
                lw = -RW_DECAY_SCALE * _sigmoid(w0[d] + lws[u][d])
                ag = _sigmoid(a0[d] + ags[u][d])
                kd = kc * (1.0 + (ag - 1.0) * k_a)
                kb = kk * ag
                for p in range(npp):
                    insts.append((pair(rc, p), pair(lw, p), pair(kd, p), -pair(kk, p), pair(kb, p), pair(vc, p),
                                  d == 1))
                    where.append((d, p, cs[u]))
        for (d, p, c), (t, g, w_end, q, y0) in zip(where, _rwkv_chunk_terms(insts)):
            tt_ref[d, p, c] = t
            tg_ref[d, p, c] = g
            tw_ref[d, p, c] = jnp.broadcast_to(w_end, (8, LANE))
            tq_ref[d, p, c] = q
            ty_ref[d, p, c] = y0
        return carry

    lax.fori_loop(0, n_chunks // TERM_UNROLL, terms_body, 0)

    chains = [(d, p) for d in range(2) for p in range(npp)]
    for d, p in chains:
        if zero_init:
            st_ref[d, p] = jnp.zeros((LANE, LANE), F32)
        else:
            z = jnp.zeros((64, 64), F32)
            st_ref[d, p] = jnp.concatenate([jnp.concatenate([s0_ref[0, d, 2 * p], z], axis=1),
                                            jnp.concatenate([z, s0_ref[0, d, 2 * p + 1]], axis=1)], axis=0)

    def scan_body(i, carry):
        cs = (i, n_chunks - 1 - i)
        ss = [st_ref[d, p] for d, p in chains]
        sb = [s.astype(BF16) for s in ss]
        ys = [_dot(tq_ref[d, p, cs[d]], b, _NT) + ty_ref[d, p, cs[d]] for (d, p), b in zip(chains, sb)]
        sn = [s * tw_ref[d, p, cs[d]][0:1] + _dot(b, tt_ref[d, p, cs[d]]) + tg_ref[d, p, cs[d]]
              for (d, p), s, b in zip(chains, ss, sb)]
        for (d, p), y, s in zip(chains, ys, sn):
            st_ref[d, p] = s
            out_ref = yf_ref if d == 0 else yb_ref
            out_ref[pl.ds(pl.multiple_of(cs[d] * CHUNK, CHUNK), CHUNK), p * LANE:(p + 1) * LANE] = y
        return carry

    lax.fori_loop(0, n_chunks, scan_body, 0)
    for d, p in chains:
        s = st_ref[d, p]
        sout_ref[0, d, 2 * p] = s[0:64, 0:64]
        sout_ref[0, d, 2 * p + 1] = s[64:LANE, 64:LANE]

    def post_body(j, carry):
        offs = [pl.multiple_of((j * TERM_UNROLL + u) * CHUNK, CHUNK) for u in range(TERM_UNROLL)]
        ys = [yf_ref[pl.ds(off, CHUNK), :] + yb_ref[pl.ds(off, CHUNK), :] for off in offs]
        mus = [block_sum(y) * (1.0 / 64) for y in ys]
        dlts = [y - mu for y, mu in zip(ys, mus)]
        vrs = [block_sum(dlt * dlt) * (1.0 / 64) for dlt in dlts]
        for off, dlt, var in zip(offs, dlts, vrs):
            yn = dlt * lax.rsqrt(var + RW_LN_EPS) * ln_w + ln_b
            y_ref[pl.ds(off, CHUNK), :] = (yn + bonus_ref[pl.ds(off, CHUNK), :]) * gate_ref[pl.ds(off, CHUNK), :]
        return carry

    lax.fori_loop(0, n_chunks // TERM_UNROLL, post_body, 0)


def _rwkv_mixer(proj, n_seq, seq_len, is_grid, prm, s0):
    n_pairs = RW_W // LANE
    n_chunks = seq_len // CHUNK
    npp = RW_PAIRS_PER_STEP
    w = npp * LANE
    assert n_chunks % TERM_UNROLL == 0 and n_pairs % npp == 0
    col = lambda cb: (lambda b, p: (b, cb // npp + p))
    par = lambda cb: (lambda b, p: (0, cb // npp + p))
    state_spec = pl.BlockSpec((1, 2, 2 * npp, 64, 64), lambda b, p: (b, 0, p, 0, 0))
    kernel = functools.partial(_rwkv_kernel, seq_len, is_grid, s0 is None)
    y, s_out = pl.pallas_call(
        kernel,
        grid=(n_seq, n_pairs // npp),
        in_specs=[pl.BlockSpec((seq_len, w), col(CB_R)),
                  pl.BlockSpec((seq_len, w), col(CB_K)),
                  pl.BlockSpec((seq_len, w), col(CB_V)),
                  pl.BlockSpec((seq_len, 3 * LANE), lambda b, p: (b, CB_LORA // 3)),
                  pl.BlockSpec((9, w), par(CB_R)),
                  pl.BlockSpec((9, w), par(CB_K)),
                  pl.BlockSpec((9, w), par(CB_V)),
                  pl.BlockSpec((2, 2, LANE, w), lambda b, p: (0, 0, 0, p)),
                  pl.BlockSpec((2, 2, LANE, w), lambda b, p: (0, 0, 0, p)),
                  pl.BlockSpec((2, LANE, w), lambda b, p: (0, 0, p)),
                  pl.BlockSpec((16, w), lambda b, p: (0, p))] + ([] if s0 is None else [state_spec]),
        out_specs=[pl.BlockSpec((seq_len, w), lambda b, p: (b, p)), state_spec],
        out_shape=[jax.ShapeDtypeStruct((n_seq * seq_len, RW_W), F32),
                   jax.ShapeDtypeStruct((n_seq, 2, 2 * n_pairs, 64, 64), F32)],
        scratch_shapes=[pltpu.VMEM((seq_len + 2 * CONV_PAD, 3 * w), F32)]
                       + [pltpu.VMEM((seq_len, w), F32)] * 4
                       + [pltpu.VMEM((2, npp, LANE, LANE), F32),
                          pltpu.VMEM((2, npp, n_chunks, LANE, LANE), BF16),
                          pltpu.VMEM((2, npp, n_chunks, LANE, LANE), F32),
                          pltpu.VMEM((2, npp, n_chunks, 8, LANE), F32),
                          pltpu.VMEM((2, npp, n_chunks, CHUNK, LANE), BF16),
                          pltpu.VMEM((2, npp, n_chunks, CHUNK, LANE), F32)]
                       + ([pltpu.VMEM((seq_len + 2 * CONV_PAD, 3 * w), F32)] * 2 if is_grid else []),
        compiler_params=_cparams(2),
        name="rwkv_mixer",
    )(proj, proj, proj, proj, prm['conv'], prm['conv'], prm['conv'], prm['w2p'], prm['a2p'], prm['g2'],
      prm['vec'], *([] if s0 is None else [s0]))
    return y, s_out


def _gla_kernel(seq_len, zero_init, q_ref, k_ref, v_ref, og_ref, lgk_ref, gk2_ref, gvec_ref, *rest):
    s0_ref = None if zero_init else rest[0]
    y_ref, sout_ref, of_ref, ob_ref, st_ref, tq_ref, to_ref, ta_ref, tkv_ref = rest[0 if zero_init else 1:]
    n_chunks = seq_len // CHUNK
    npp = GLA_PAIRS
    gvec = gvec_ref[...]
    chains = [(d, p) for d in range(2) for p in range(npp)]

    def terms_body(j, carry):
        insts, where = [], []
        for u in range(TERM_UNROLL):
            c = j * TERM_UNROLL + u
            off = pl.multiple_of(c * CHUNK, CHUNK)
            lgk = _split2(lgk_ref[pl.ds(off, CHUNK), :])
            qc = q_ref[pl.ds(off, CHUNK), :] * GLA_Q_SCALE
            kc = k_ref[pl.ds(off, CHUNK), :]
            vc = v_ref[pl.ds(off, CHUNK), :]
            for d in range(2):
                x = _mm3_split(lgk, (gk2_ref[0, d], gk2_ref[1, d])) + gvec[d:d + 1, 0:GLA_QK_W]
                g = _log_sigmoid(x) * (1.0 / GLA_GATE_NORM)
                for p in range(npp):
                    qk = slice(p * LANE, (p + 1) * LANE)
                    insts.append((qc[:, qk], kc[:, qk], vc[:, 2 * p * LANE:2 * (p + 1) * LANE], g[:, qk], d == 1))
                    where.append((d, p, c))
        for (d, p, c), (qt, o0, a_col, kv) in zip(where, _gla_chunk_terms(insts)):
            tq_ref[d, p, c] = qt
            to_ref[d, p, c] = o0
            ta_ref[d, p, c] = a_col
            tkv_ref[d, p, c] = kv
        return carry

    lax.fori_loop(0, n_chunks // TERM_UNROLL, terms_body, 0)

    for d, p in chains:
        if zero_init:
            st_ref[d, p] = jnp.zeros((LANE, 2 * LANE), F32)
        else:
            z = jnp.zeros((64, LANE), F32)
            st_ref[d, p] = jnp.concatenate([jnp.concatenate([s0_ref[0, d, 2 * p], z], axis=1),
                                            jnp.concatenate([z, s0_ref[0, d, 2 * p + 1]], axis=1)], axis=0)

    def scan_body(i, carry):
        cs = (i, n_chunks - 1 - i)
        ss = [st_ref[d, p] for d, p in chains]
        os_ = [_dot(tq_ref[d, p, cs[d]], s.astype(BF16)) + to_ref[d, p, cs[d]] for (d, p), s in zip(chains, ss)]
        for (d, p), s, o in zip(chains, ss, os_):
            a_col = ta_ref[d, p, cs[d]]
            st_ref[d, p] = s * jnp.concatenate([a_col, a_col], axis=1) + tkv_ref[d, p, cs[d]]
            out_ref = of_ref if d == 0 else ob_ref
            out_ref[pl.ds(pl.multiple_of(cs[d] * CHUNK, CHUNK), CHUNK), 2 * p * LANE:2 * (p + 1) * LANE] = o
        return carry

    lax.fori_loop(0, n_chunks, scan_body, 0)
    for d, p in chains:
        s = st_ref[d, p]
        sout_ref[0, d, 2 * p] = s[0:64, 0:LANE]
        sout_ref[0, d, 2 * p + 1] = s[64:LANE, LANE:2 * LANE]

    def post_body(c, carry):
        off = pl.multiple_of(c * CHUNK, CHUNK)
        for h in range(2 * npp):
            hs = slice(h * LANE, (h + 1) * LANE)
            o = of_ref[pl.ds(off, CHUNK), hs] + ob_ref[pl.ds(off, CHUNK), hs]
            gate = _silu(og_ref[pl.ds(off, CHUNK), hs])
            y_ref[pl.ds(off, CHUNK), hs] = _rmsnorm_rows(o) * gvec[2:3, hs] * gate
        return carry

    lax.fori_loop(0, n_chunks, post_body, 0)


def _gla_mixer(proj, n_seq, seq_len, prm, s0):
    npp = GLA_PAIRS
    n_heads = 2 * npp
    n_chunks = seq_len // CHUNK
    assert n_chunks % TERM_UNROLL == 0
    state_spec = pl.BlockSpec((1, 2, n_heads, 64, LANE), lambda b: (b, 0, 0, 0, 0))
    kernel = functools.partial(_gla_kernel, seq_len, s0 is None)
    y, s_out = pl.pallas_call(
        kernel,
        grid=(n_seq,),
        in_specs=[pl.BlockSpec((seq_len, GLA_QK_W), lambda b: (b, CB_GQ * LANE // GLA_QK_W)),
                  pl.BlockSpec((seq_len, GLA_QK_W), lambda b: (b, CB_GK * LANE // GLA_QK_W)),
                  pl.BlockSpec((seq_len, GLA_V_W), lambda b: (b, CB_GV * LANE // GLA_V_W)),
                  pl.BlockSpec((seq_len, GLA_V_W), lambda b: (b, CB_OG * LANE // GLA_V_W)),
                  pl.BlockSpec((seq_len, LANE), lambda b: (b, CB_LGK)),
                  pl.BlockSpec((2, 2, LANE, GLA_QK_W), lambda b: (0, 0, 0, 0)),
                  pl.BlockSpec((8, GLA_V_W), lambda b: (0, 0))] + ([] if s0 is None else [state_spec]),
        out_specs=[pl.BlockSpec((seq_len, GLA_V_W), lambda b: (b, 0)), state_spec],
        out_shape=[jax.ShapeDtypeStruct((n_seq * seq_len, GLA_V_W), F32),
                   jax.ShapeDtypeStruct((n_seq, 2, n_heads, 64, LANE), F32)],
        scratch_shapes=[pltpu.VMEM((seq_len, GLA_V_W), F32)] * 2
                       + [pltpu.VMEM((2, npp, LANE, 2 * LANE), F32),
                          pltpu.VMEM((2, npp, n_chunks, CHUNK, LANE), BF16),
                          pltpu.VMEM((2, npp, n_chunks, CHUNK, 2 * LANE), F32),
                          pltpu.VMEM((2, npp, n_chunks, LANE, LANE), F32),
                          pltpu.VMEM((2, npp, n_chunks, LANE, 2 * LANE), F32)],
        compiler_params=_cparams(1),
        name="gla_mixer",
    )(proj, proj, proj, proj, proj, prm['gk2p'], prm['gvec'], *([] if s0 is None else [s0]))
    return y, s_out


OUT_TM = 512
ROUTE_NEG = -1e30
LANE_GROUP0 = N_EXPERTS


def _route(logits):
    lane = _iota(logits.shape, 1)
    lane_f = lane.astype(F32)
    big = float(LANE)
    is_g = (lane >= LANE_GROUP0) & (lane < LANE_GROUP0 + 4)
    gmax = jnp.max(jnp.where(is_g, logits, ROUTE_NEG), axis=-1, keepdims=True)
    gidx = jnp.min(jnp.where(is_g & (logits == gmax), lane_f, big), axis=-1, keepdims=True) - LANE_GROUP0
    gsum = jnp.sum(jnp.where(is_g, jnp.exp(jnp.minimum(logits - gmax, 0.0)), 0.0), axis=-1, keepdims=True)
    g_w = 1.0 / gsum
    in_grp = (lane < N_EXPERTS) & ((lane // 4).astype(F32) == gidx)
    m1 = jnp.max(jnp.where(in_grp, logits, ROUTE_NEG), axis=-1, keepdims=True)
    i1 = jnp.min(jnp.where(in_grp & (logits == m1), lane_f, big), axis=-1, keepdims=True)
    rest = in_grp & (lane_f != i1)
    m2 = jnp.max(jnp.where(rest, logits, ROUTE_NEG), axis=-1, keepdims=True)
    i2 = jnp.min(jnp.where(rest & (logits == m2), lane_f, big), axis=-1, keepdims=True)
    t = jnp.exp(m2 - m1)
    w1 = g_w / (1.0 + t)
    return jnp.where(lane_f == i1, w1, 0.0) + jnp.where(lane_f == i2, w1 * t, 0.0)


def _outproj_kernel(yr_ref, yg_ref, x_ref, mod_ref, wo_ref, g_ref, wr_ref, br_ref, x1_ref, h2_ref, cmb_ref):
    m = mod_ref[0]
    mix = _mm(yr_ref[...], wo_ref[0:RW_W, :]) + _mm(yg_ref[...], wo_ref[RW_W:RW_W + GLA_V_W, :])
    x1 = x_ref[...] + m[2:3] * mix
    h2 = _rmsnorm_rows(x1) * g_ref[...] * (1.0 + m[4:5]) + m[3:4]
    x1_ref[...] = x1
    h2_ref[...] = h2.astype(BF16)
    cmb_ref[...] = _route(_mm3_split(_split2(h2), (wr_ref[0], wr_ref[1])) + br_ref[...])


def _out_projection(y_rw, y_gla, x2d, mod, w_out, norm_g, w_route, b_route, seq_len, ctx_row):
    n = x2d.shape[0]
    tile = lambda w: pl.BlockSpec((OUT_TM, w), lambda i: (i, 0))
    full = lambda a: pl.BlockSpec(a.shape, lambda i: (0,) * a.ndim)
    return pl.pallas_call(
        _outproj_kernel,
        grid=(n // OUT_TM,),
        in_specs=[tile(RW_W), tile(GLA_V_W), tile(D_MODEL),
                  pl.BlockSpec((1, N_MOD, D_MODEL), _mod_index(seq_len, OUT_TM, ctx_row)),
                  full(w_out), full(norm_g), full(w_route), full(b_route)],
        out_specs=[tile(D_MODEL), tile(D_MODEL), tile(LANE)],
        out_shape=[jax.ShapeDtypeStruct((n, D_MODEL), F32), jax.ShapeDtypeStruct((n, D_MODEL), BF16),
                   jax.ShapeDtypeStruct((n, LANE), F32)],
        compiler_params=_cparams(1),
        name="out_proj_router",
    )(y_rw, y_gla, x2d, mod, w_out, norm_g, w_route, b_route)


MOE_TM = 512
MOE_RB = 128
MOE_INTERLEAVE = 4
SLOT_ALIGN = 16
MOE_SLOTS = 2 * MOE_TM + N_EXPERTS * SLOT_ALIGN + MOE_RB


def _moe_kernel(h2_ref, cmb_ref, x1_ref, mod_ref, w1_ref, w3_ref, w2_ref, fg_ref, y_ref, xs_ref, ys_ref):
    cmb = cmb_ref[...]
    lane = _iota(cmb.shape, 1).astype(F32)
    sel = cmb > 0.0
    sel01 = jnp.where(sel, 1.0, 0.0).astype(BF16)
    before = (_iota((MOE_TM, MOE_TM), 0) > _iota((MOE_TM, MOE_TM), 1)).astype(BF16)
    pos = _dot(before, sel01)
    cnt = pos[MOE_TM - 1:MOE_TM] + sel01[MOE_TM - 1:MOE_TM].astype(F32)
    seg = jnp.floor((cnt + (SLOT_ALIGN - 1)) * (1.0 / SLOT_ALIGN))
    lower_experts = (_iota((LANE, LANE), 0) < _iota((LANE, LANE), 1)).astype(BF16)
    start = _dot(jnp.broadcast_to(seg, (8, LANE)).astype(BF16), lower_experts)[0:1] * SLOT_ALIGN
    n_blk = jnp.floor((cnt + (MOE_RB - 1)) * (1.0 / MOE_RB)).astype(jnp.int32)
    start_i = start.astype(jnp.int32)
    cnt_i = cnt.astype(jnp.int32)
    slot = start + pos
    e_a = jnp.min(jnp.where(sel, lane, float(LANE)), axis=-1, keepdims=True)
    e_b = jnp.max(jnp.where(sel, lane, -1.0), axis=-1, keepdims=True)
    pick = lambda e, x: jnp.sum(jnp.where(lane == e, x, 0.0), axis=-1, keepdims=True)
    slot_a, w_a = pick(e_a, slot), pick(e_a, cmb)
    slot_b = jnp.where(e_b != e_a, pick(e_b, slot), -1.0)
    w_b = pick(e_b, cmb)

    slots_t = jnp.where(lane == 0.0, slot_a, jnp.where(lane == 1.0, slot_b, -1.0)).T
    row_slot = _iota((MOE_SLOTS, MOE_TM), 0).astype(F32)
    gather = jnp.where((row_slot == slots_t[0:1]) | (row_slot == slots_t[1:2]), 1.0, 0.0).astype(BF16)
    xs_ref[...] = _dot(gather, h2_ref[...]).astype(BF16)
    ys_ref[...] = jnp.zeros_like(ys_ref)

    row_in_blk = _iota((MOE_RB, D_MODEL), 0)

    def expert_blocks(experts, r0s, ends):
        xbs = [xs_ref[pl.ds(r0, MOE_RB), :] for r0 in r0s]
        gates = [_dot(xb, w3_ref[e]) for xb, e in zip(xbs, experts)]
        ups = [_dot(xb, w1_ref[e]) for xb, e in zip(xbs, experts)]
        acts = [(_silu(g) * u).astype(BF16) for g, u in zip(gates, ups)]
        outs = [_dot(a, w2_ref[e]) for a, e in zip(acts, experts)]
        for r0, end, out in zip(r0s, ends, outs):
            keep = row_in_blk + r0 >= end
            ys_ref[pl.ds(r0, MOE_RB), :] = jnp.where(keep, ys_ref[pl.ds(r0, MOE_RB), :], out.astype(BF16))

    seg_start = [pl.multiple_of(start_i[0, e], SLOT_ALIGN) for e in range(N_EXPERTS)]
    seg_end = [seg_start[e] + cnt_i[0, e] for e in range(N_EXPERTS)]
    for e0 in range(0, N_EXPERTS, MOE_INTERLEAVE):
        es = list(range(e0, e0 + MOE_INTERLEAVE))
        expert_blocks(es, [seg_start[e] for e in es], [seg_end[e] for e in es])
    for e in range(N_EXPERTS):
        def extra_block(b, carry, e=e):
            expert_blocks([e], [pl.multiple_of(seg_start[e] + b * MOE_RB, SLOT_ALIGN)], [seg_end[e]])
            return carry

        lax.fori_loop(1, n_blk[0, e], extra_block, 0)

    col_slot = _iota((MOE_TM, MOE_SLOTS), 1).astype(F32)
    scatter = (jnp.where(col_slot == slot_a, w_a, 0.0) + jnp.where(col_slot == slot_b, w_b, 0.0)).astype(BF16)
    x2 = x1_ref[...] + mod_ref[0][5:6] * _dot(scatter, ys_ref[...])
    y_ref[...] = _rmsnorm_rows(x2) * fg_ref[...]


def _moe(h2, cmb, x1, mod, w1, w3, w2, final_g, seq_len, ctx_row):
    n = h2.shape[0]
    tile = lambda w: pl.BlockSpec((MOE_TM, w), lambda i: (i, 0))
    resident = lambda a: pl.BlockSpec(a.shape, lambda i: (0,) * a.ndim, pipeline_mode=pl.Buffered(1))
    return pl.pallas_call(
        _moe_kernel,
        grid=(n // MOE_TM,),
        in_specs=[tile(D_MODEL), tile(LANE), tile(D_MODEL),
                  pl.BlockSpec((1, N_MOD, D_MODEL), _mod_index(seq_len, MOE_TM, ctx_row)),
                  resident(w1), resident(w3), resident(w2),
                  pl.BlockSpec((1, D_MODEL), lambda i: (0, 0))],
        out_specs=tile(D_MODEL),
        out_shape=jax.ShapeDtypeStruct((n, D_MODEL), F32),
        scratch_shapes=[pltpu.VMEM((MOE_SLOTS, D_MODEL), BF16), pltpu.VMEM((MOE_SLOTS, D_MODEL), BF16)],
        compiler_params=_cparams(1),
        name="moe_experts",
    )(h2, cmb, x1, mod, w1, w3, w2, final_g)


def _pad_rows(x, rows):
    return jnp.pad(x, ((0, rows - x.shape[0]),) + ((0, 0),) * (x.ndim - 1))


def _pack_params(l, w_in, rw_conv, rw_w0, rw_w2, rw_a0, rw_a2, rw_g2, rw_k_k, rw_k_a, rw_r_k, rw_ln_w, rw_ln_b,
                 gla_gk2, gla_gk_b, gla_norm_g, moe_w_group, moe_b_group, moe_w_expert, moe_b_expert):
    wi = w_in[l]
    z = lambda n: jnp.zeros((D_MODEL, n), F32)
    w_in_p = jnp.concatenate([wi[:, 0:1920], wi[:, 3456:3488], z(LANE - 32), wi[:, 1920:3456]], axis=1).astype(BF16)
    z64 = jnp.zeros((64, RW_W), F32)
    w2p = jnp.stack([jnp.concatenate([rw_w2[l, 0], z64], 0), jnp.concatenate([z64, rw_w2[l, 1]], 0)])
    a2p = jnp.stack([jnp.concatenate([rw_a2[l, 0], z64], 0), jnp.concatenate([z64, rw_a2[l, 1]], 0)])
    vec = _pad_rows(jnp.stack([rw_w0[l, 0], rw_w0[l, 1], rw_a0[l, 0], rw_a0[l, 1], rw_k_k[l], rw_k_a[l],
                               rw_r_k[l].reshape(RW_W), rw_ln_w[l], rw_ln_b[l]]), 16)
    rw = {'conv': rw_conv[l].reshape(9, 3 * RW_W), 'w2p': _split_param(w2p), 'a2p': _split_param(a2p),
          'g2': _split_param(rw_g2[l]), 'vec': vec}
    gk2p = jnp.stack([_pad_rows(gla_gk2[l, 0], LANE),
                      _pad_rows(jnp.concatenate([jnp.zeros((16, GLA_QK_W), F32), gla_gk2[l, 1]], 0), LANE)])
    gk_b = jnp.pad(gla_gk_b[l], ((0, 0), (0, GLA_V_W - GLA_QK_W)))
    gvec = _pad_rows(jnp.concatenate([gk_b, jnp.tile(gla_norm_g[l], GLA_V_W // LANE)[None]], axis=0), 8)
    gla = {'gk2p': _split_param(gk2p), 'gvec': gvec}
    w_route = _split_param(jnp.concatenate([moe_w_expert[l], moe_w_group[l], z(LANE - N_EXPERTS - 4)], axis=1))
    b_route = jnp.concatenate([moe_b_expert[l], moe_b_group[l], jnp.zeros((LANE - N_EXPERTS - 4,), F32)])[None]
    return w_in_p, rw, gla, w_route, b_route


def _trunk_pass(x, mod, ctx_row, s_rw0, s_gla0, is_grid, pk, norm1_g, norm2_g, w_out, w1, w3, w2, final_g):
    n_seq, seq_len, _ = x.shape
    x2d = x.reshape(n_seq * seq_len, D_MODEL)
    w_in_p, rw, gla, w_route, b_route = pk
    proj = _in_projection(x2d, mod, norm1_g, w_in_p, seq_len, ctx_row)
    y_rw, s_rw = _rwkv_mixer(proj, n_seq, seq_len, is_grid, rw, s_rw0)
    y_gla, s_gla = _gla_mixer(proj, n_seq, seq_len, gla, s_gla0)
    x1, h2, cmb = _out_projection(y_rw, y_gla, x2d, mod, w_out, norm2_g, w_route, b_route, seq_len, ctx_row)
    y = _moe(h2, cmb, x1, mod, w1, w3, w2, final_g, seq_len, ctx_row)
    return y.reshape(n_seq, seq_len, D_MODEL), s_rw, s_gla


def kernel(x_prompt, x_sample, state_rwkv, state_gla, c, c_ctx, norm1_g, norm2_g, w_ada, b_ada, w_in, w_out,
           rw_conv, rw_w0, rw_w2, rw_a0, rw_a2, rw_g2, rw_k_k, rw_k_a, rw_r_k, rw_ln_w, rw_ln_b,
           gla_gk2, gla_gk_b, gla_norm_g, moe_w_group, moe_b_group, moe_w_expert, moe_b_expert,
           moe_w1, moe_w3, moe_w2, final_g):
    depth = w_in.shape[0]
    assert depth == 1, "the packed layout below handles the single-layer trunk of this problem"
    l = 0
    n_dec = x_sample.shape[0]
    ctx_row = n_dec
    cond8 = _pad_rows(jnp.concatenate([c, c_ctx[None]], axis=0), 8)
    mod = _modulation(cond8, w_ada[l], b_ada[l][None]).reshape(8, N_MOD, D_MODEL)
    pk = _pack_params(l, w_in, rw_conv, rw_w0, rw_w2, rw_a0, rw_a2, rw_g2, rw_k_k, rw_k_a, rw_r_k, rw_ln_w,
                      rw_ln_b, gla_gk2, gla_gk_b, gla_norm_g, moe_w_group, moe_b_group, moe_w_expert,
                      moe_b_expert)
    shared = (pk, norm1_g[l][None], norm2_g[l][None], w_out[l].astype(BF16), moe_w1[l].astype(BF16),
              moe_w3[l].astype(BF16), moe_w2[l].astype(BF16), final_g[None])

    y_prompt, s_rw, s_gla = _trunk_pass(x_prompt, mod, ctx_row, None, None, False, *shared)
    y_sample, _, _ = _trunk_pass(x_sample, mod, None, state_rwkv[:, l], state_gla[:, l], True, *shared)
    return (y_prompt, y_sample, s_rw[:, None], s_gla[:, None])
```

```python
import functools

import jax
import jax.numpy as jnp
from jax import lax
from jax.experimental import pallas as pl
from jax.experimental.pallas import tpu as pltpu

F32 = jnp.float32
BF16 = jnp.bfloat16

D_MODEL = 1024
RW_W = 512
GLA_V_W = 512
GLA_QK_W = 256
N_EXPERTS = 16
D_EXPERT = 256
N_MOD = 6
EPS = 1e-6
RW_LN_EPS = 64e-5
RW_DECAY_SCALE = 0.606531
GLA_GATE_NORM = 16.0
GLA_Q_SCALE = 64 ** -0.5
GRID_W = 64

LANE = 128
CHUNK = 64
CONV_PAD = 128
TERM_UNROLL = 4
RW_PAIRS_PER_STEP = 2
GLA_PAIRS = 2
D_PROJ = 28 * LANE
VMEM_LIMIT = 56 * 1024 * 1024

CB_R, CB_K, CB_V, CB_LORA, CB_LGK, CB_GQ, CB_GK, CB_GV, CB_OG = 0, 4, 8, 12, 15, 16, 18, 20, 24

_NN = (((1,), (0,)), ((), ()))
_NT = (((1,), (1,)), ((), ()))
_TN = (((0,), (0,)), ((), ()))


def _dot(a, b, dims=_NN):
    return lax.dot_general(a, b, dims, preferred_element_type=F32)


def _mm(a, b, dims=_NN):
    return _dot(a.astype(BF16), b.astype(BF16), dims)


def _split2(x):
    hi = x.astype(BF16)
    lo = (x - hi.astype(F32)).astype(BF16)
    return hi, lo


def _mm3(a, b, dims=_NN):
    ah, al = _split2(a)
    bh, bl = _split2(b)
    return _dot(ah, bh, dims) + _dot(ah, bl, dims) + _dot(al, bh, dims)


def _mm3_split(a_split, b_split, dims=_NN):
    (ah, al), (bh, bl) = a_split, b_split
    return _dot(ah, bh, dims) + _dot(ah, bl, dims) + _dot(al, bh, dims)


def _split_param(w):
    return jnp.stack(_split2(w))


def _mm_01_lhs(a01, b, dims=_NN):
    b1, b2 = _split2(b)
    return _dot(a01, b1, dims) + _dot(a01, b2, dims)


def _sigmoid(x):
    return 0.5 * jnp.tanh(0.5 * x) + 0.5


def _silu(x):
    return x * _sigmoid(x)


def _log_sigmoid(x):
    return jnp.minimum(x, 0.0) - jnp.log(1.0 + jnp.exp(-jnp.abs(x)))


def _iota(shape, dim):
    return lax.broadcasted_iota(jnp.int32, shape, dim)


def _cparams(n_axes):
    return pltpu.CompilerParams(dimension_semantics=("arbitrary",) * n_axes, vmem_limit_bytes=VMEM_LIMIT)


MOD_TN = 768


def _mod_kernel(c_ref, w_ref, b_ref, o_ref):
    o_ref[...] = _mm3(_silu(c_ref[...]), w_ref[...]) + b_ref[...]


def _modulation(cond8, w_ada, b_ada):
    n = w_ada.shape[1]
    return pl.pallas_call(
        _mod_kernel,
        grid=(n // MOD_TN,),
        in_specs=[pl.BlockSpec((8, D_MODEL), lambda j: (0, 0)),
                  pl.BlockSpec((D_MODEL, MOD_TN), lambda j: (0, j)),
                  pl.BlockSpec((1, MOD_TN), lambda j: (0, j))],
        out_specs=pl.BlockSpec((8, MOD_TN), lambda j: (0, j)),
        out_shape=jax.ShapeDtypeStruct((8, n), F32),
        compiler_params=_cparams(1),
        name="adaln_mod",
    )(cond8, w_ada, b_ada)


D_IN = 3488
LGK_SRC_BLOCK = (D_IN - 32) // LANE


def _repack_kernel(w_ref, o_ref):
    w = w_ref[...]
    past_end = (pl.program_id(0) == CB_LGK) & (_iota(w.shape, 1) >= D_IN - LGK_SRC_BLOCK * LANE)
    o_ref[...] = jnp.where(past_end, 0.0, w).astype(BF16)


def _repack_w_in(w_in):
    src_block = lambda j: (0, jnp.where(j < CB_LGK, j, jnp.where(j == CB_LGK, LGK_SRC_BLOCK, j - 1)))
    return pl.pallas_call(
        _repack_kernel,
        grid=(D_PROJ // LANE,),
        in_specs=[pl.BlockSpec((D_MODEL, LANE), src_block)],
        out_specs=pl.BlockSpec((D_MODEL, LANE), lambda j: (0, j)),
        out_shape=jax.ShapeDtypeStruct((D_MODEL, D_PROJ), BF16),
        compiler_params=_cparams(1),
        name="repack_w_in",
    )(w_in)


PROJ_TM = 512


def _rmsnorm_rows(x):
    return x * lax.rsqrt(jnp.mean(x * x, axis=-1, keepdims=True) + EPS)


class _Tiles:
    def __init__(self, n_ctx_tokens, n_dec_tokens, dec_seq_len, ctx_row, tm):
        self.tm = tm
        self.n_ctx = n_ctx_tokens // tm
        self.n_dec = n_dec_tokens // tm
        self.per_seq = dec_seq_len // tm
        self.ctx_row = ctx_row

    def specs(self, width):
        last_ctx = self.n_ctx - 1
        n_ctx = self.n_ctx
        return (pl.BlockSpec((self.tm, width), lambda i: (jnp.minimum(i, last_ctx), 0)),
                pl.BlockSpec((self.tm, width), lambda i: (jnp.maximum(i - n_ctx, 0), 0)))

    def merged(self, width):
        return pl.BlockSpec((self.tm, width), lambda i: (i, 0))

    def mod_spec(self):
        n_ctx, per_seq, ctx_row = self.n_ctx, self.per_seq, self.ctx_row
        return pl.BlockSpec((1, N_MOD, D_MODEL),
                            lambda i: (jnp.where(i < n_ctx, ctx_row, (i - n_ctx) // per_seq), 0, 0))

    def by_pass(self, run_ctx, run_dec):
        i = pl.program_id(0)
        pl.when(i < self.n_ctx)(run_ctx)
        pl.when(i >= self.n_ctx)(run_dec)


def _inproj_kernel(tiles, xc_ref, xd_ref, mod_ref, g_ref, w_ref, o_ref):
    def run(x_ref):
        m = mod_ref[0]
        h = _rmsnorm_rows(x_ref[...]) * g_ref[...] * (1.0 + m[1:2]) + m[0:1]
        o_ref[...] = _mm(h, w_ref[...])

    tiles.by_pass(functools.partial(run, xc_ref), functools.partial(run, xd_ref))


def _in_projection(tiles, x_ctx, x_dec, mod, norm_g, w_in_p):
    full = lambda a: pl.BlockSpec(a.shape, lambda i: (0,) * a.ndim)
    return pl.pallas_call(
        functools.partial(_inproj_kernel, tiles),
        grid=(tiles.n_ctx + tiles.n_dec,),
        in_specs=[*tiles.specs(D_MODEL), tiles.mod_spec(), full(norm_g), full(w_in_p)],
        out_specs=tiles.merged(D_PROJ),
        out_shape=jax.ShapeDtypeStruct((x_ctx.shape[0] + x_dec.shape[0], D_PROJ), F32),
        compiler_params=_cparams(1),
        name="in_proj",
    )(x_ctx, x_dec, mod, norm_g, w_in_p)


def _time_masks(reverse):
    r = _iota((2 * CHUNK, 2 * CHUNK), 0) % CHUNK
    c = _iota((2 * CHUNK, 2 * CHUNK), 1) % CHUNK
    if reverse:
        return r < c, r <= c
    return r > c, r >= c


def _cumsum_matrix(reverse):
    r = _iota((CHUNK, CHUNK), 0)
    c = _iota((CHUNK, CHUNK), 1)
    tri = (r <= c) if reverse else (r >= c)
    return tri.astype(BF16)


def _stack_heads(x, half):
    m0 = _iota(x.shape, 1) < half
    return jnp.concatenate([jnp.where(m0, x, 0.0), jnp.where(m0, 0.0, x)], axis=0)


def _head_sums(x):
    parts = []
    for p in range(x.shape[1] // LANE):
        xp = x[:, p * LANE:(p + 1) * LANE]
        m0 = _iota(xp.shape, 1) < 64
        s0 = jnp.sum(jnp.where(m0, xp, 0.0), axis=-1, keepdims=True)
        s1 = jnp.sum(jnp.where(m0, 0.0, xp), axis=-1, keepdims=True)
        parts.append(jnp.where(m0, s0, s1))
    return parts[0] if len(parts) == 1 else jnp.concatenate(parts, axis=1)


def _rwkv_chunk_terms(insts):
    c = CHUNK
    step_row = _iota((c, LANE), 0)
    step_col = _iota((c, LANE), 1) % c
    eye_w = (step_row == step_col).astype(F32)
    same_head = (_iota((LANE, LANE), 0) // 64) == (_iota((LANE, LANE), 1) // 64)
    stack_bf = lambda x: _stack_heads(x, 64).astype(BF16)
    cums = [_mm_01_lhs(_cumsum_matrix(rev), lw) for (_, lw, _, _, _, _, rev) in insts]
    pre = []
    for (r, lw, kd, a, b, v, rev), cum in zip(insts, cums):
        end = cum[0:1] if rev else cum[c - 1:c]
        inv_w = jnp.exp(-cum)
        rem_w = jnp.exp(end - cum)
        a_t = a * jnp.exp(cum - lw)
        r_t = r * jnp.exp(cum)
        bk_s = jnp.concatenate([stack_bf(b * inv_w), stack_bf(kd * inv_w)], axis=0)
        bkh = jnp.concatenate([b * rem_w, kd * rem_w], axis=0).astype(BF16)
        pre.append((a_t, r_t, bk_s, bkh, v, jnp.exp(end)))
    ms = [_dot(jnp.concatenate([a_t, r_t], axis=0).astype(BF16), bk_s, _NT) for (a_t, r_t, bk_s, _, _, _) in pre]
    mats = []
    for m, (_, _, _, _, _, _, rev) in zip(ms, insts):
        strict = (step_row < step_col) if rev else (step_row > step_col)
        incl = (step_row <= step_col) if rev else (step_row >= step_col)
        l_ab = jnp.where(strict, m[0:c, 0:LANE], 0.0)
        l_akrk = jnp.concatenate([jnp.where(strict, m[0:c, LANE:2 * LANE], 0.0),
                                  jnp.where(incl, m[c:2 * c, LANE:2 * LANE], 0.0)], axis=0).astype(BF16)
        m_rb = jnp.where(incl, m[c:2 * c, 0:LANE], 0.0).astype(BF16)
        mats.append((l_ab, l_akrk, m_rb))
    lvs = [_dot(l_akrk, stack_bf(pr[4])) for (_, l_akrk, _), pr in zip(mats, pre)]
    ps = [eye_w + l_ab for (l_ab, _, _) in mats]
    lps = [_dot(l_ab.astype(BF16), stack_bf(l_ab)) for (l_ab, _, _) in mats]
    for level in range(1, 6):
        if level < 5:
            xs = [_dot(lp.astype(BF16), jnp.concatenate([stack_bf(p), stack_bf(lp)], axis=1))
                  for lp, p in zip(lps, ps)]
            ps = [p + x[:, 0:LANE] for p, x in zip(ps, xs)]
            lps = [x[:, LANE:2 * LANE] for x in xs]
        else:
            ps = [p + _dot(lp.astype(BF16), stack_bf(p)) for lp, p in zip(lps, ps)]
    pxs = [_dot(p.astype(BF16), jnp.concatenate([stack_bf(pr[0]), stack_bf(lv[0:c])], axis=1))
           for p, pr, lv in zip(ps, pre, lvs)]
    mzs = [_dot(mt[2], jnp.concatenate([stack_bf(px[:, 0:LANE]), stack_bf(px[:, LANE:2 * LANE])], axis=1))
           for mt, px in zip(mats, pxs)]
    ts = [_dot(px[:, 0:LANE].astype(BF16), pr[3][0:c], _TN) for px, pr in zip(pxs, pre)]
    gs = [_dot(jnp.concatenate([px[:, LANE:2 * LANE], pr[4]], axis=0).astype(BF16), pr[3], _TN)
          for px, pr in zip(pxs, pre)]
    out = []
    for pr, lv, mz, t, g in zip(pre, lvs, mzs, ts, gs):
        q = pr[1] + mz[:, 0:LANE]
        y0 = mz[:, LANE:2 * LANE] + lv[c:2 * c]
        out.append((jnp.where(same_head, t, 0.0).astype(BF16), jnp.where(same_head, g, 0.0), pr[5],
                    q.astype(BF16), y0))
    return out


def _gla_chunk_terms(insts):
    c = CHUNK
    step_row = _iota((c, LANE), 0)
    step_col = _iota((c, LANE), 1) % c
    same_head = (_iota((LANE, 2 * LANE), 0) // 64) == (_iota((LANE, 2 * LANE), 1) // LANE)
    cums = [_mm_01_lhs(_cumsum_matrix(rev), g) for (_, _, _, g, rev) in insts]
    pre = []
    for (q, k, v, g, rev), cum in zip(insts, cums):
        end = cum[0:1] if rev else cum[c - 1:c]
        qt = (q * jnp.exp(cum)).astype(BF16)
        k_s = _stack_heads(k * jnp.exp(-cum), 64).astype(BF16)
        kh = (k * jnp.exp(end - cum)).astype(BF16)
        a_col = jnp.broadcast_to(jnp.exp(end), (LANE, LANE)).T
        pre.append((qt, k_s, kh, v.astype(BF16), _stack_heads(v, LANE).astype(BF16), a_col))
    atts = [_dot(pr[0], pr[1], _NT) for pr in pre]
    atts = [jnp.where((step_row <= step_col) if inst[4] else (step_row >= step_col), att, 0.0).astype(BF16)
            for att, inst in zip(atts, insts)]
    o0s = [_dot(att, pr[4]) for att, pr in zip(atts, pre)]
    kvs = [jnp.where(same_head, _dot(pr[2], pr[3], _TN), 0.0) for pr in pre]
    return [(pr[0], o0, pr[5], kv) for pr, o0, kv in zip(pre, o0s, kvs)]


def _rwkv_kernel(seq_len, is_grid, zero_init, r_ref, k_ref, v_ref, lora_ref, cwr_ref, cwk_ref, cwv_ref, w2_ref,
                 a2_ref, g2_ref, vec_ref, *rest):
    s0_ref = None if zero_init else rest[0]
    (y_ref, sout_ref, pad_ref, bonus_ref, gate_ref, yf_ref, yb_ref, st_ref, tt_ref, tg_ref, tw_ref, tq_ref,
     ty_ref) = rest[0 if zero_init else 1:][:13]
    left_ref, right_ref = rest[-2:] if is_grid else (None, None)
    n_chunks = seq_len // CHUNK
    npp = RW_PAIRS_PER_STEP
    w = npp * LANE
    pair = lambda x, p: x[:, p * LANE:(p + 1) * LANE]
    vec = vec_ref[...]
    w0 = (vec[0:1], vec[1:2])
    a0 = (vec[2:3], vec[3:4])
    k_k, k_a, r_k, ln_w, ln_b = vec[4:5], vec[5:6], vec[6:7], vec[7:8], vec[8:9]
    block_sum = _head_sums

    zeros = jnp.zeros((CONV_PAD, 3 * w), F32)
    pad_ref[0:CONV_PAD, :] = zeros
    pad_ref[CONV_PAD + seq_len:2 * CONV_PAD + seq_len, :] = zeros
    pad_ref[CONV_PAD:CONV_PAD + seq_len, 0:w] = r_ref[...]
    pad_ref[CONV_PAD:CONV_PAD + seq_len, w:2 * w] = k_ref[...]
    pad_ref[CONV_PAD:CONV_PAD + seq_len, 2 * w:3 * w] = v_ref[...]
    cw = jnp.concatenate([cwr_ref[...], cwk_ref[...], cwv_ref[...]], axis=1)

    if is_grid:
        col = _iota((CHUNK, 3 * w), 0)
        left_ref[0:CONV_PAD, :] = zeros
        left_ref[CONV_PAD + seq_len:2 * CONV_PAD + seq_len, :] = zeros
        right_ref[0:CONV_PAD, :] = zeros
        right_ref[CONV_PAD + seq_len:2 * CONV_PAD + seq_len, :] = zeros

        def shift_body(c, carry):
            base = pl.multiple_of(CONV_PAD + c * CHUNK, CHUNK)
            win = pad_ref[pl.ds(base - 8, CHUNK + 16), :]
            left_ref[pl.ds(base, CHUNK), :] = jnp.where(col >= 1, win[7:7 + CHUNK], 0.0)
            right_ref[pl.ds(base, CHUNK), :] = jnp.where(col <= GRID_W - 2, win[9:9 + CHUNK], 0.0)
            return carry

        lax.fori_loop(0, n_chunks, shift_body, 0)

    def conv_chunk(c):
        base = pl.multiple_of(CONV_PAD + c * CHUNK, CHUNK)
        acc = jnp.zeros((CHUNK, 3 * w), F32)
        if is_grid:
            for di in (-1, 0, 1):
                row = pl.ds(base + di * GRID_W, CHUNK)
                for dj, src in ((-1, left_ref), (0, pad_ref), (1, right_ref)):
                    tap = (di + 1) * 3 + dj + 1
                    acc = acc + src[row, :] * cw[tap:tap + 1]
        else:
            win = pad_ref[pl.ds(base - 8, CHUNK + 16), :]
            for dj in (-1, 0, 1):
                acc = acc + win[8 + dj:8 + dj + CHUNK] * cw[4 + dj:5 + dj]
        return acc[:, 0:w], acc[:, w:2 * w], acc[:, 2 * w:3 * w]

    def terms_body(j, carry):
        cs = [j * TERM_UNROLL + u for u in range(TERM_UNROLL)]
        offs = [pl.multiple_of(c * CHUNK, CHUNK) for c in cs]
        rkv = [conv_chunk(c) for c in cs]
        loras = [lora_ref[pl.ds(off, CHUNK), :] for off in offs]
        kks = [kc * k_k for (_, kc, _) in rkv]
        kk_ss = [block_sum(kk * kk) for kk in kks]
        bon_ss = [block_sum(rc * kc * r_k) for (rc, kc, _) in rkv]
        gates = [_mm3_split(_split2(_sigmoid(lo[:, 2 * LANE:3 * LANE])), (g2_ref[0], g2_ref[1])) for lo in loras]
        lw_ins = [_split2(jnp.tanh(lo[:, 0:LANE])) for lo in loras]
        la_ins = [_split2(lo[:, LANE:2 * LANE]) for lo in loras]
        lws = [[_mm3_split(x, (w2_ref[0, d], w2_ref[1, d])) for d in range(2)] for x in lw_ins]
        ags = [[_mm3_split(x, (a2_ref[0, d], a2_ref[1, d])) for d in range(2)] for x in la_ins]
        insts, where = [], []
        for u in range(TERM_UNROLL):
            rc, kc, vc = rkv[u]
            kk = kks[u] * lax.rsqrt(kk_ss[u] + EPS)
            bonus_ref[pl.ds(offs[u], CHUNK), :] = bon_ss[u] * vc
            gate_ref[pl.ds(offs[u], CHUNK), :] = gates[u]
            for d in range(2):
                lw = -RW_DECAY_SCALE * _sigmoid(w0[d] + lws[u][d])
                ag = _sigmoid(a0[d] + ags[u][d])
                kd = kc * (1.0 + (ag - 1.0) * k_a)
                kb = kk * ag
                for p in range(npp):
                    insts.append((pair(rc, p), pair(lw, p), pair(kd, p), -pair(kk, p), pair(kb, p), pair(vc, p),
                                  d == 1))
                    where.append((d, p, cs[u]))
        for (d, p, c), (t, g, w_end, q, y0) in zip(where, _rwkv_chunk_terms(insts)):
            tt_ref[d, p, c] = t
            tg_ref[d, p, c] = g
            tw_ref[d, p, c] = jnp.broadcast_to(w_end, (8, LANE))
            tq_ref[d, p, c] = q
            ty_ref[d, p, c] = y0
        return carry

    lax.fori_loop(0, n_chunks // TERM_UNROLL, terms_body, 0)

    chains = [(d, p) for d in range(2) for p in range(npp)]
    for d, p in chains:
        if zero_init:
            st_ref[d, p] = jnp.zeros((LANE, LANE), F32)
        else:
            z = jnp.zeros((64, 64), F32)
            st_ref[d, p] = jnp.concatenate([jnp.concatenate([s0_ref[0, d, 2 * p], z], axis=1),
                                            jnp.concatenate([z, s0_ref[0, d, 2 * p + 1]], axis=1)], axis=0)

    def scan_body(i, carry):
        cs = (i, n_chunks - 1 - i)
        ss = [st_ref[d, p] for d, p in chains]
        sb = [s.astype(BF16) for s in ss]
        ys = [_dot(tq_ref[d, p, cs[d]], b, _NT) + ty_ref[d, p, cs[d]] for (d, p), b in zip(chains, sb)]
        sn = [s * tw_ref[d, p, cs[d]][0:1] + _dot(b, tt_ref[d, p, cs[d]]) + tg_ref[d, p, cs[d]]
              for (d, p), s, b in zip(chains, ss, sb)]
        for (d, p), y, s in zip(chains, ys, sn):
            st_ref[d, p] = s
            out_ref = yf_ref if d == 0 else yb_ref
            out_ref[pl.ds(pl.multiple_of(cs[d] * CHUNK, CHUNK), CHUNK), p * LANE:(p + 1) * LANE] = y
        return carry

    lax.fori_loop(0, n_chunks, scan_body, 0)
    for d, p in chains:
        s = st_ref[d, p]
        sout_ref[0, d, 2 * p] = s[0:64, 0:64]
        sout_ref[0, d, 2 * p + 1] = s[64:LANE, 64:LANE]

    def post_body(j, carry):
        offs = [pl.multiple_of((j * TERM_UNROLL + u) * CHUNK, CHUNK) for u in range(TERM_UNROLL)]
        ys = [yf_ref[pl.ds(off, CHUNK), :] + yb_ref[pl.ds(off, CHUNK), :] for off in offs]
        mus = [block_sum(y) * (1.0 / 64) for y in ys]
        dlts = [y - mu for y, mu in zip(ys, mus)]
        vrs = [block_sum(dlt * dlt) * (1.0 / 64) for dlt in dlts]
        for off, dlt, var in zip(offs, dlts, vrs):
            yn = dlt * lax.rsqrt(var + RW_LN_EPS) * ln_w + ln_b
            y_ref[pl.ds(off, CHUNK), :] = (yn + bonus_ref[pl.ds(off, CHUNK), :]) * gate_ref[pl.ds(off, CHUNK), :]
        return carry

    lax.fori_loop(0, n_chunks // TERM_UNROLL, post_body, 0)


def _rwkv_mixer(proj, first_seq, n_seq, seq_len, is_grid, prm, s0):
    n_pairs = RW_W // LANE
    n_chunks = seq_len // CHUNK
    npp = RW_PAIRS_PER_STEP
    w = npp * LANE
    assert n_chunks % TERM_UNROLL == 0 and n_pairs % npp == 0
    col = lambda cb: (lambda b, p: (b + first_seq, cb // npp + p))
    par = lambda cb: (lambda b, p: (0, cb // npp + p))
    state_spec = pl.BlockSpec((1, 2, 2 * npp, 64, 64), lambda b, p: (b, 0, p, 0, 0))
    kernel = functools.partial(_rwkv_kernel, seq_len, is_grid, s0 is None)
    y, s_out = pl.pallas_call(
        kernel,
        grid=(n_seq, n_pairs // npp),
        in_specs=[pl.BlockSpec((seq_len, w), col(CB_R)),
                  pl.BlockSpec((seq_len, w), col(CB_K)),
                  pl.BlockSpec((seq_len, w), col(CB_V)),
                  pl.BlockSpec((seq_len, 3 * LANE), lambda b, p: (b + first_seq, CB_LORA // 3)),
                  pl.BlockSpec((9, w), par(CB_R)),
                  pl.BlockSpec((9, w), par(CB_K)),
                  pl.BlockSpec((9, w), par(CB_V)),
                  pl.BlockSpec((2, 2, LANE, w), lambda b, p: (0, 0, 0, p)),
                  pl.BlockSpec((2, 2, LANE, w), lambda b, p: (0, 0, 0, p)),
                  pl.BlockSpec((2, LANE, w), lambda b, p: (0, 0, p)),
                  pl.BlockSpec((16, w), lambda b, p: (0, p))] + ([] if s0 is None else [state_spec]),
        out_specs=[pl.BlockSpec((seq_len, w), lambda b, p: (b, p)), state_spec],
        out_shape=[jax.ShapeDtypeStruct((n_seq * seq_len, RW_W), F32),
                   jax.ShapeDtypeStruct((n_seq, 2, 2 * n_pairs, 64, 64), F32)],
        scratch_shapes=[pltpu.VMEM((seq_len + 2 * CONV_PAD, 3 * w), F32)]
                       + [pltpu.VMEM((seq_len, w), F32)] * 4
                       + [pltpu.VMEM((2, npp, LANE, LANE), F32),
                          pltpu.VMEM((2, npp, n_chunks, LANE, LANE), BF16),
                          pltpu.VMEM((2, npp, n_chunks, LANE, LANE), F32),
                          pltpu.VMEM((2, npp, n_chunks, 8, LANE), F32),
                          pltpu.VMEM((2, npp, n_chunks, CHUNK, LANE), BF16),
                          pltpu.VMEM((2, npp, n_chunks, CHUNK, LANE), F32)]
                       + ([pltpu.VMEM((seq_len + 2 * CONV_PAD, 3 * w), F32)] * 2 if is_grid else []),
        compiler_params=_cparams(2),
        name="rwkv_mixer",
    )(proj, proj, proj, proj, prm['conv'], prm['conv'], prm['conv'], prm['w2p'], prm['a2p'], prm['g2'],
      prm['vec'], *([] if s0 is None else [s0]))
    return y, s_out


def _gla_kernel(seq_len, zero_init, q_ref, k_ref, v_ref, og_ref, lgk_ref, gk2_ref, gvec_ref, *rest):
    s0_ref = None if zero_init else rest[0]
    y_ref, sout_ref, of_ref, ob_ref, st_ref, tq_ref, to_ref, ta_ref, tkv_ref = rest[0 if zero_init else 1:]
    n_chunks = seq_len // CHUNK
    npp = GLA_PAIRS
    gvec = gvec_ref[...]
    chains = [(d, p) for d in range(2) for p in range(npp)]

    def terms_body(j, carry):
        insts, where = [], []
        for u in range(TERM_UNROLL):
            c = j * TERM_UNROLL + u
            off = pl.multiple_of(c * CHUNK, CHUNK)
            lgk = _split2(lgk_ref[pl.ds(off, CHUNK), :])
            qc = q_ref[pl.ds(off, CHUNK), :] * GLA_Q_SCALE
            kc = k_ref[pl.ds(off, CHUNK), :]
            vc = v_ref[pl.ds(off, CHUNK), :]
            for d in range(2):
                x = _mm3_split(lgk, (gk2_ref[0, d], gk2_ref[1, d])) + gvec[d:d + 1, 0:GLA_QK_W]
                g = _log_sigmoid(x) * (1.0 / GLA_GATE_NORM)
                for p in range(npp):
                    qk = slice(p * LANE, (p + 1) * LANE)
                    insts.append((qc[:, qk], kc[:, qk], vc[:, 2 * p * LANE:2 * (p + 1) * LANE], g[:, qk], d == 1))
                    where.append((d, p, c))
        for (d, p, c), (qt, o0, a_col, kv) in zip(where, _gla_chunk_terms(insts)):
            tq_ref[d, p, c] = qt
            to_ref[d, p, c] = o0
            ta_ref[d, p, c] = a_col
            tkv_ref[d, p, c] = kv
        return carry

    lax.fori_loop(0, n_chunks // TERM_UNROLL, terms_body, 0)

    for d, p in chains:
        if zero_init:
            st_ref[d, p] = jnp.zeros((LANE, 2 * LANE), F32)
        else:
            z = jnp.zeros((64, LANE), F32)
            st_ref[d, p] = jnp.concatenate([jnp.concatenate([s0_ref[0, d, 2 * p], z], axis=1),
                                            jnp.concatenate([z, s0_ref[0, d, 2 * p + 1]], axis=1)], axis=0)

    def scan_body(i, carry):
        cs = (i, n_chunks - 1 - i)
        ss = [st_ref[d, p] for d, p in chains]
        os_ = [_dot(tq_ref[d, p, cs[d]], s.astype(BF16)) + to_ref[d, p, cs[d]] for (d, p), s in zip(chains, ss)]
        for (d, p), s, o in zip(chains, ss, os_):
            a_col = ta_ref[d, p, cs[d]]
            st_ref[d, p] = s * jnp.concatenate([a_col, a_col], axis=1) + tkv_ref[d, p, cs[d]]
            out_ref = of_ref if d == 0 else ob_ref
            out_ref[pl.ds(pl.multiple_of(cs[d] * CHUNK, CHUNK), CHUNK), 2 * p * LANE:2 * (p + 1) * LANE] = o
        return carry

    lax.fori_loop(0, n_chunks, scan_body, 0)
    for d, p in chains:
        s = st_ref[d, p]
        sout_ref[0, d, 2 * p] = s[0:64, 0:LANE]
        sout_ref[0, d, 2 * p + 1] = s[64:LANE, LANE:2 * LANE]

    def post_body(c, carry):
        off = pl.multiple_of(c * CHUNK, CHUNK)
        for h in range(2 * npp):
            hs = slice(h * LANE, (h + 1) * LANE)
            o = of_ref[pl.ds(off, CHUNK), hs] + ob_ref[pl.ds(off, CHUNK), hs]
            gate = _silu(og_ref[pl.ds(off, CHUNK), hs])
            y_ref[pl.ds(off, CHUNK), hs] = _rmsnorm_rows(o) * gvec[2:3, hs] * gate
        return carry

    lax.fori_loop(0, n_chunks, post_body, 0)


def _gla_mixer(proj, first_seq, n_seq, seq_len, prm, s0):
    npp = GLA_PAIRS
    n_heads = 2 * npp
    n_chunks = seq_len // CHUNK
    assert n_chunks % TERM_UNROLL == 0
    state_spec = pl.BlockSpec((1, 2, n_heads, 64, LANE), lambda b: (b, 0, 0, 0, 0))
    kernel = functools.partial(_gla_kernel, seq_len, s0 is None)
    y, s_out = pl.pallas_call(
        kernel,
        grid=(n_seq,),
        in_specs=[pl.BlockSpec((seq_len, GLA_QK_W), lambda b: (b + first_seq, CB_GQ * LANE // GLA_QK_W)),
                  pl.BlockSpec((seq_len, GLA_QK_W), lambda b: (b + first_seq, CB_GK * LANE // GLA_QK_W)),
                  pl.BlockSpec((seq_len, GLA_V_W), lambda b: (b + first_seq, CB_GV * LANE // GLA_V_W)),
                  pl.BlockSpec((seq_len, GLA_V_W), lambda b: (b + first_seq, CB_OG * LANE // GLA_V_W)),
                  pl.BlockSpec((seq_len, LANE), lambda b: (b + first_seq, CB_LGK)),
                  pl.BlockSpec((2, 2, LANE, GLA_QK_W), lambda b: (0, 0, 0, 0)),
                  pl.BlockSpec((8, GLA_V_W), lambda b: (0, 0))] + ([] if s0 is None else [state_spec]),
        out_specs=[pl.BlockSpec((seq_len, GLA_V_W), lambda b: (b, 0)), state_spec],
        out_shape=[jax.ShapeDtypeStruct((n_seq * seq_len, GLA_V_W), F32),
                   jax.ShapeDtypeStruct((n_seq, 2, n_heads, 64, LANE), F32)],
        scratch_shapes=[pltpu.VMEM((seq_len, GLA_V_W), F32)] * 2
                       + [pltpu.VMEM((2, npp, LANE, 2 * LANE), F32),
                          pltpu.VMEM((2, npp, n_chunks, CHUNK, LANE), BF16),
                          pltpu.VMEM((2, npp, n_chunks, CHUNK, 2 * LANE), F32),
                          pltpu.VMEM((2, npp, n_chunks, LANE, LANE), F32),
                          pltpu.VMEM((2, npp, n_chunks, LANE, 2 * LANE), F32)],
        compiler_params=_cparams(1),
        name="gla_mixer",
    )(proj, proj, proj, proj, proj, prm['gk2p'], prm['gvec'], *([] if s0 is None else [s0]))
    return y, s_out


OUT_TM = 512
ROUTE_NEG = -1e30
LANE_GROUP0 = N_EXPERTS


def _route(logits):
    lane = _iota(logits.shape, 1)
    lane_f = lane.astype(F32)
    big = float(LANE)
    is_g = (lane >= LANE_GROUP0) & (lane < LANE_GROUP0 + 4)
    gmax = jnp.max(jnp.where(is_g, logits, ROUTE_NEG), axis=-1, keepdims=True)
    gidx = jnp.min(jnp.where(is_g & (logits == gmax), lane_f, big), axis=-1, keepdims=True) - LANE_GROUP0
    gsum = jnp.sum(jnp.where(is_g, jnp.exp(jnp.minimum(logits - gmax, 0.0)), 0.0), axis=-1, keepdims=True)
    g_w = 1.0 / gsum
    in_grp = (lane < N_EXPERTS) & ((lane // 4).astype(F32) == gidx)
    m1 = jnp.max(jnp.where(in_grp, logits, ROUTE_NEG), axis=-1, keepdims=True)
    i1 = jnp.min(jnp.where(in_grp & (logits == m1), lane_f, big), axis=-1, keepdims=True)
    rest = in_grp & (lane_f != i1)
    m2 = jnp.max(jnp.where(rest, logits, ROUTE_NEG), axis=-1, keepdims=True)
    i2 = jnp.min(jnp.where(rest & (logits == m2), lane_f, big), axis=-1, keepdims=True)
    t = jnp.exp(m2 - m1)
    w1 = g_w / (1.0 + t)
    return jnp.where(lane_f == i1, w1, 0.0) + jnp.where(lane_f == i2, w1 * t, 0.0)


def _outproj_kernel(tiles, yrc_ref, yrd_ref, ygc_ref, ygd_ref, xc_ref, xd_ref, mod_ref, wo_ref, g_ref, wr_ref,
                    br_ref, x1_ref, h2_ref, cmb_ref):
    def run(yr_ref, yg_ref, x_ref):
        m = mod_ref[0]
        mix = _mm(yr_ref[...], wo_ref[0:RW_W, :]) + _mm(yg_ref[...], wo_ref[RW_W:RW_W + GLA_V_W, :])
        x1 = x_ref[...] + m[2:3] * mix
        h2 = _rmsnorm_rows(x1) * g_ref[...] * (1.0 + m[4:5]) + m[3:4]
        x1_ref[...] = x1
        h2_ref[...] = h2.astype(BF16)
        cmb_ref[...] = _route(_mm3_split(_split2(h2), (wr_ref[0], wr_ref[1])) + br_ref[...])

    tiles.by_pass(functools.partial(run, yrc_ref, ygc_ref, xc_ref), functools.partial(run, yrd_ref, ygd_ref, xd_ref))


def _out_projection(tiles, y_rw, y_gla, x, mod, w_out, norm_g, w_route, b_route):
    n = x[0].shape[0] + x[1].shape[0]
    full = lambda a: pl.BlockSpec(a.shape, lambda i: (0,) * a.ndim)
    return pl.pallas_call(
        functools.partial(_outproj_kernel, tiles),
        grid=(tiles.n_ctx + tiles.n_dec,),
        in_specs=[*tiles.specs(RW_W), *tiles.specs(GLA_V_W), *tiles.specs(D_MODEL), tiles.mod_spec(),
                  full(w_out), full(norm_g), full(w_route), full(b_route)],
        out_specs=[tiles.merged(D_MODEL), tiles.merged(D_MODEL), tiles.merged(LANE)],
        out_shape=[jax.ShapeDtypeStruct((n, D_MODEL), F32), jax.ShapeDtypeStruct((n, D_MODEL), BF16),
                   jax.ShapeDtypeStruct((n, LANE), F32)],
        compiler_params=_cparams(1),
        name="out_proj_router",
    )(*y_rw, *y_gla, *x, mod, w_out, norm_g, w_route, b_route)


MOE_TM = 512
MOE_RB = 128
MOE_INTERLEAVE = 4
SLOT_ALIGN = 16
MOE_SLOTS = 2 * MOE_TM + N_EXPERTS * SLOT_ALIGN + MOE_RB


def _stage_expert_weights(src_hbm, dst_ref, stage_ref, sem):
    copy = lambda e: pltpu.make_async_copy(src_hbm.at[e], stage_ref.at[e % 2], sem.at[e % 2])
    copy(0).start()
    for e in range(N_EXPERTS):
        if e + 1 < N_EXPERTS:
            copy(e + 1).start()
        copy(e).wait()
        dst_ref[e] = stage_ref[e % 2].astype(BF16)


def _moe_kernel(tiles, h2_ref, cmb_ref, x1_ref, mod_ref, w1_hbm, w3_hbm, w2_hbm, fg_ref, yc_ref, yd_ref,
                xs_ref, ys_ref, w1_ref, w3_ref, w2_ref, stage_up_ref, stage_down_ref, sem_up, sem_down):
    @pl.when(pl.program_id(0) == 0)
    def _():
        _stage_expert_weights(w1_hbm, w1_ref, stage_up_ref, sem_up)
        _stage_expert_weights(w3_hbm, w3_ref, stage_up_ref, sem_up)
        _stage_expert_weights(w2_hbm, w2_ref, stage_down_ref, sem_down)

    cmb = cmb_ref[...]
    lane = _iota(cmb.shape, 1).astype(F32)
    sel = cmb > 0.0
    sel01 = jnp.where(sel, 1.0, 0.0).astype(BF16)
    before = (_iota((MOE_TM, MOE_TM), 0) > _iota((MOE_TM, MOE_TM), 1)).astype(BF16)
    pos = _dot(before, sel01)
    cnt = pos[MOE_TM - 1:MOE_TM] + sel01[MOE_TM - 1:MOE_TM].astype(F32)
    seg = jnp.floor((cnt + (SLOT_ALIGN - 1)) * (1.0 / SLOT_ALIGN))
    lower_experts = (_iota((LANE, LANE), 0) < _iota((LANE, LANE), 1)).astype(BF16)
    start = _dot(jnp.broadcast_to(seg, (8, LANE)).astype(BF16), lower_experts)[0:1] * SLOT_ALIGN
    n_blk = jnp.floor((cnt + (MOE_RB - 1)) * (1.0 / MOE_RB)).astype(jnp.int32)
    start_i = start.astype(jnp.int32)
    cnt_i = cnt.astype(jnp.int32)
    slot = start + pos
    e_a = jnp.min(jnp.where(sel, lane, float(LANE)), axis=-1, keepdims=True)
    e_b = jnp.max(jnp.where(sel, lane, -1.0), axis=-1, keepdims=True)
    pick = lambda e, x: jnp.sum(jnp.where(lane == e, x, 0.0), axis=-1, keepdims=True)
    slot_a, w_a = pick(e_a, slot), pick(e_a, cmb)
    slot_b = jnp.where(e_b != e_a, pick(e_b, slot), -1.0)
    w_b = pick(e_b, cmb)

    slots_t = jnp.where(lane == 0.0, slot_a, jnp.where(lane == 1.0, slot_b, -1.0)).T
    row_slot = _iota((MOE_SLOTS, MOE_TM), 0).astype(F32)
    gather = jnp.where((row_slot == slots_t[0:1]) | (row_slot == slots_t[1:2]), 1.0, 0.0).astype(BF16)
    xs_ref[...] = _dot(gather, h2_ref[...]).astype(BF16)
    ys_ref[...] = jnp.zeros_like(ys_ref)

    row_in_blk = _iota((MOE_RB, D_MODEL), 0)

    def expert_blocks(experts, r0s, ends):
        xbs = [xs_ref[pl.ds(r0, MOE_RB), :] for r0 in r0s]
        gates = [_dot(xb, w3_ref[e]) for xb, e in zip(xbs, experts)]
        ups = [_dot(xb, w1_ref[e]) for xb, e in zip(xbs, experts)]
        acts = [(_silu(g) * u).astype(BF16) for g, u in zip(gates, ups)]
        outs = [_dot(a, w2_ref[e]) for a, e in zip(acts, experts)]
        for r0, end, out in zip(r0s, ends, outs):
            keep = row_in_blk + r0 >= end
            ys_ref[pl.ds(r0, MOE_RB), :] = jnp.where(keep, ys_ref[pl.ds(r0, MOE_RB), :], out.astype(BF16))

    seg_start = [pl.multiple_of(start_i[0, e], SLOT_ALIGN) for e in range(N_EXPERTS)]
    seg_end = [seg_start[e] + cnt_i[0, e] for e in range(N_EXPERTS)]
    for e0 in range(0, N_EXPERTS, MOE_INTERLEAVE):
        es = list(range(e0, e0 + MOE_INTERLEAVE))
        expert_blocks(es, [seg_start[e] for e in es], [seg_end[e] for e in es])
    for e in range(N_EXPERTS):
        def extra_block(b, carry, e=e):
            expert_blocks([e], [pl.multiple_of(seg_start[e] + b * MOE_RB, SLOT_ALIGN)], [seg_end[e]])
            return carry

        lax.fori_loop(1, n_blk[0, e], extra_block, 0)

    col_slot = _iota((MOE_TM, MOE_SLOTS), 1).astype(F32)
    scatter = (jnp.where(col_slot == slot_a, w_a, 0.0) + jnp.where(col_slot == slot_b, w_b, 0.0)).astype(BF16)
    x2 = x1_ref[...] + mod_ref[0][5:6] * _dot(scatter, ys_ref[...])
    y = _rmsnorm_rows(x2) * fg_ref[...]

    def write(y_ref):
        y_ref[...] = y

    tiles.by_pass(functools.partial(write, yc_ref), functools.partial(write, yd_ref))


def _moe(tiles, h2, cmb, x1, mod, w1, w3, w2, final_g):
    assert tiles.tm == MOE_TM
    hbm = pl.BlockSpec(memory_space=pl.ANY)
    out_ctx, out_dec = tiles.specs(D_MODEL)
    return pl.pallas_call(
        functools.partial(_moe_kernel, tiles),
        grid=(tiles.n_ctx + tiles.n_dec,),
        in_specs=[tiles.merged(D_MODEL), tiles.merged(LANE), tiles.merged(D_MODEL), tiles.mod_spec(),
                  hbm, hbm, hbm, pl.BlockSpec((1, D_MODEL), lambda i: (0, 0))],
        out_specs=[out_ctx, out_dec],
        out_shape=[jax.ShapeDtypeStruct((tiles.n_ctx * MOE_TM, D_MODEL), F32),
                   jax.ShapeDtypeStruct((tiles.n_dec * MOE_TM, D_MODEL), F32)],
        scratch_shapes=[pltpu.VMEM((MOE_SLOTS, D_MODEL), BF16), pltpu.VMEM((MOE_SLOTS, D_MODEL), BF16),
                        pltpu.VMEM((N_EXPERTS, D_MODEL, D_EXPERT), BF16),
                        pltpu.VMEM((N_EXPERTS, D_MODEL, D_EXPERT), BF16),
                        pltpu.VMEM((N_EXPERTS, D_EXPERT, D_MODEL), BF16),
                        pltpu.VMEM((2, D_MODEL, D_EXPERT), F32), pltpu.VMEM((2, D_EXPERT, D_MODEL), F32),
                        pltpu.SemaphoreType.DMA((2,)), pltpu.SemaphoreType.DMA((2,))],
        compiler_params=_cparams(1),
        name="moe_experts",
    )(h2, cmb, x1, mod, w1, w3, w2, final_g)


def _pad_rows(x, rows):
    return jnp.pad(x, ((0, rows - x.shape[0]),) + ((0, 0),) * (x.ndim - 1))


def _pack_params(l, w_in, rw_conv, rw_w0, rw_w2, rw_a0, rw_a2, rw_g2, rw_k_k, rw_k_a, rw_r_k, rw_ln_w, rw_ln_b,
                 gla_gk2, gla_gk_b, gla_norm_g, moe_w_group, moe_b_group, moe_w_expert, moe_b_expert):
    wi = w_in[l]
    z = lambda n: jnp.zeros((D_MODEL, n), F32)
    w_in_p = _repack_w_in(wi)
    z64 = jnp.zeros((64, RW_W), F32)
    w2p = jnp.stack([jnp.concatenate([rw_w2[l, 0], z64], 0), jnp.concatenate([z64, rw_w2[l, 1]], 0)])
    a2p = jnp.stack([jnp.concatenate([rw_a2[l, 0], z64], 0), jnp.concatenate([z64, rw_a2[l, 1]], 0)])
    vec = _pad_rows(jnp.stack([rw_w0[l, 0], rw_w0[l, 1], rw_a0[l, 0], rw_a0[l, 1], rw_k_k[l], rw_k_a[l],
                               rw_r_k[l].reshape(RW_W), rw_ln_w[l], rw_ln_b[l]]), 16)
    rw = {'conv': rw_conv[l].reshape(9, 3 * RW_W), 'w2p': _split_param(w2p), 'a2p': _split_param(a2p),
          'g2': _split_param(rw_g2[l]), 'vec': vec}
    gk2p = jnp.stack([_pad_rows(gla_gk2[l, 0], LANE),
                      _pad_rows(jnp.concatenate([jnp.zeros((16, GLA_QK_W), F32), gla_gk2[l, 1]], 0), LANE)])
    gk_b = jnp.pad(gla_gk_b[l], ((0, 0), (0, GLA_V_W - GLA_QK_W)))
    gvec = _pad_rows(jnp.concatenate([gk_b, jnp.tile(gla_norm_g[l], GLA_V_W // LANE)[None]], axis=0), 8)
    gla = {'gk2p': _split_param(gk2p), 'gvec': gvec}
    w_route = _split_param(jnp.concatenate([moe_w_expert[l], moe_w_group[l], z(LANE - N_EXPERTS - 4)], axis=1))
    b_route = jnp.concatenate([moe_b_expert[l], moe_b_group[l], jnp.zeros((LANE - N_EXPERTS - 4,), F32)])[None]
    return w_in_p, rw, gla, w_route, b_route


def kernel(x_prompt, x_sample, state_rwkv, state_gla, c, c_ctx, norm1_g, norm2_g, w_ada, b_ada, w_in, w_out,
           rw_conv, rw_w0, rw_w2, rw_a0, rw_a2, rw_g2, rw_k_k, rw_k_a, rw_r_k, rw_ln_w, rw_ln_b,
           gla_gk2, gla_gk_b, gla_norm_g, moe_w_group, moe_b_group, moe_w_expert, moe_b_expert,
           moe_w1, moe_w3, moe_w2, final_g):
    depth = w_in.shape[0]
    assert depth == 1, "the packed layout below handles the single-layer trunk of this problem"
    l = 0
    n_dec = x_sample.shape[0]
    ctx_row = n_dec
    cond8 = _pad_rows(jnp.concatenate([c, c_ctx[None]], axis=0), 8)
    mod = _modulation(cond8, w_ada[l], b_ada[l][None]).reshape(8, N_MOD, D_MODEL)
    pk = _pack_params(l, w_in, rw_conv, rw_w0, rw_w2, rw_a0, rw_a2, rw_g2, rw_k_k, rw_k_a, rw_r_k, rw_ln_w,
                      rw_ln_b, gla_gk2, gla_gk_b, gla_norm_g, moe_w_group, moe_b_group, moe_w_expert,
                      moe_b_expert)
    w_in_p, rw, gla, w_route, b_route = pk

    n_ctx, ctx_len, _ = x_prompt.shape
    dec_len = x_sample.shape[1]
    x_ctx = x_prompt.reshape(n_ctx * ctx_len, D_MODEL)
    x_dec = x_sample.reshape(n_dec * dec_len, D_MODEL)
    assert (n_ctx * ctx_len) % dec_len == 0, "denoising sequences must start on a dec_len row block of proj"
    first_dec = n_ctx * ctx_len // dec_len
    tiles = _Tiles(n_ctx * ctx_len, n_dec * dec_len, dec_len, ctx_row, PROJ_TM)
    assert PROJ_TM == OUT_TM == MOE_TM

    proj = _in_projection(tiles, x_ctx, x_dec, mod, norm1_g[l][None], w_in_p)
    y_rw_c, s_rw = _rwkv_mixer(proj, 0, n_ctx, ctx_len, False, rw, None)
    y_gla_c, s_gla = _gla_mixer(proj, 0, n_ctx, ctx_len, gla, None)
    y_rw_d, _ = _rwkv_mixer(proj, first_dec, n_dec, dec_len, True, rw, state_rwkv[:, l])
    y_gla_d, _ = _gla_mixer(proj, first_dec, n_dec, dec_len, gla, state_gla[:, l])
    x1, h2, cmb = _out_projection(tiles, (y_rw_c, y_rw_d), (y_gla_c, y_gla_d), (x_ctx, x_dec), mod,
                                  w_out[l].astype(BF16), norm2_g[l][None], w_route, b_route)
    y_ctx, y_dec = _moe(tiles, h2, cmb, x1, mod, moe_w1[l], moe_w3[l], moe_w2[l], final_g[None])
    return (y_ctx.reshape(x_prompt.shape), y_dec.reshape(x_sample.shape), s_rw[:, None], s_gla[:, None])
```

```python
import functools

import jax
import jax.numpy as jnp
from jax import lax
from jax.experimental import pallas as pl
from jax.experimental.pallas import tpu as pltpu

F32 = jnp.float32
BF16 = jnp.bfloat16

D_MODEL = 1024
RW_W = 512
GLA_V_W = 512
GLA_QK_W = 256
N_EXPERTS = 16
D_EXPERT = 256
N_MOD = 6
EPS = 1e-6
RW_LN_EPS = 64e-5
RW_DECAY_SCALE = 0.606531
GLA_GATE_NORM = 16.0
GLA_Q_SCALE = 64 ** -0.5
GRID_W = 64

LANE = 128
CHUNK = 64
CONV_PAD = 128
TERM_UNROLL = 4
RW_PAIRS_PER_STEP = 2
GLA_PAIRS = 2
D_PROJ = 28 * LANE
VMEM_LIMIT = 56 * 1024 * 1024

CB_R, CB_K, CB_V, CB_LORA, CB_LGK, CB_GQ, CB_GK, CB_GV, CB_OG = 0, 4, 8, 12, 15, 16, 18, 20, 24

_NN = (((1,), (0,)), ((), ()))
_NT = (((1,), (1,)), ((), ()))
_TN = (((0,), (0,)), ((), ()))


def _dot(a, b, dims=_NN):
    return lax.dot_general(a, b, dims, preferred_element_type=F32)


def _mm(a, b, dims=_NN):
    return _dot(a.astype(BF16), b.astype(BF16), dims)


def _split2(x):
    hi = x.astype(BF16)
    lo = (x - hi.astype(F32)).astype(BF16)
    return hi, lo


def _mm3(a, b, dims=_NN):
    ah, al = _split2(a)
    bh, bl = _split2(b)
    return _dot(ah, bh, dims) + _dot(ah, bl, dims) + _dot(al, bh, dims)


def _mm3_split(a_split, b_split, dims=_NN):
    (ah, al), (bh, bl) = a_split, b_split
    return _dot(ah, bh, dims) + _dot(ah, bl, dims) + _dot(al, bh, dims)


def _split_param(w):
    return jnp.stack(_split2(w))


def _mm_01_lhs(a01, b, dims=_NN):
    b1, b2 = _split2(b)
    return _dot(a01, b1, dims) + _dot(a01, b2, dims)


def _sigmoid(x):
    return 0.5 * jnp.tanh(0.5 * x) + 0.5


def _silu(x):
    return x * _sigmoid(x)


def _log_sigmoid(x):
    return jnp.minimum(x, 0.0) - jnp.log(1.0 + jnp.exp(-jnp.abs(x)))


def _iota(shape, dim):
    return lax.broadcasted_iota(jnp.int32, shape, dim)


def _cparams(n_axes):
    return pltpu.CompilerParams(dimension_semantics=("arbitrary",) * n_axes, vmem_limit_bytes=VMEM_LIMIT)


MOD_TN = 768


def _mod_kernel(c_ref, w_ref, b_ref, o_ref):
    o_ref[...] = _mm3(_silu(c_ref[...]), w_ref[...]) + b_ref[...]


def _modulation(cond8, w_ada, b_ada):
    n = w_ada.shape[1]
    return pl.pallas_call(
        _mod_kernel,
        grid=(n // MOD_TN,),
        in_specs=[pl.BlockSpec((8, D_MODEL), lambda j: (0, 0)),
                  pl.BlockSpec((D_MODEL, MOD_TN), lambda j: (0, j)),
                  pl.BlockSpec((1, MOD_TN), lambda j: (0, j))],
        out_specs=pl.BlockSpec((8, MOD_TN), lambda j: (0, j)),
        out_shape=jax.ShapeDtypeStruct((8, n), F32),
        compiler_params=_cparams(1),
        name="adaln_mod",
    )(cond8, w_ada, b_ada)


PROJ_TM = 512
D_IN = 3488
N_LGK = 32


def _rmsnorm_rows(x):
    return x * lax.rsqrt(jnp.mean(x * x, axis=-1, keepdims=True) + EPS)


class _Tiles:
    def __init__(self, n_ctx_tokens, n_dec_tokens, dec_seq_len, ctx_row, tm):
        self.tm = tm
        self.n_ctx = n_ctx_tokens // tm
        self.n_dec = n_dec_tokens // tm
        self.per_seq = dec_seq_len // tm
        self.ctx_row = ctx_row

    def specs(self, width):
        last_ctx = self.n_ctx - 1
        n_ctx = self.n_ctx
        return (pl.BlockSpec((self.tm, width), lambda i: (jnp.minimum(i, last_ctx), 0)),
                pl.BlockSpec((self.tm, width), lambda i: (jnp.maximum(i - n_ctx, 0), 0)))

    def merged(self, width):
        return pl.BlockSpec((self.tm, width), lambda i: (i, 0))

    def mod_spec(self):
        n_ctx, per_seq, ctx_row = self.n_ctx, self.per_seq, self.ctx_row
        return pl.BlockSpec((1, N_MOD, D_MODEL),
                            lambda i: (jnp.where(i < n_ctx, ctx_row, (i - n_ctx) // per_seq), 0, 0))

    def by_pass(self, run_ctx, run_dec):
        i = pl.program_id(0)
        pl.when(i < self.n_ctx)(run_ctx)
        pl.when(i >= self.n_ctx)(run_dec)


def _inproj_kernel(tiles, xc_ref, xd_ref, mod_ref, g_ref, wt_ref, o_ref, w_ref):
    @pl.when(pl.program_id(0) == 0)
    def _():
        for j in range(D_PROJ // LANE):
            if j == CB_LGK:
                blk = jnp.concatenate([wt_ref[D_IN - N_LGK:D_IN, :], jnp.zeros((LANE - N_LGK, D_MODEL), F32)], axis=0)
            else:
                src = j if j < CB_LGK else j - 1
                blk = wt_ref[src * LANE:(src + 1) * LANE, :]
            w_ref[:, j * LANE:(j + 1) * LANE] = blk.T.astype(BF16)

    def run(x_ref):
        m = mod_ref[0]
        h = _rmsnorm_rows(x_ref[...]) * g_ref[...] * (1.0 + m[1:2]) + m[0:1]
        o_ref[...] = _mm(h, w_ref[...])

    tiles.by_pass(functools.partial(run, xc_ref), functools.partial(run, xd_ref))


def _in_projection(tiles, x_ctx, x_dec, mod, norm_g, w_in_t):
    full = lambda a: pl.BlockSpec(a.shape, lambda i: (0,) * a.ndim)
    return pl.pallas_call(
        functools.partial(_inproj_kernel, tiles),
        grid=(tiles.n_ctx + tiles.n_dec,),
        in_specs=[*tiles.specs(D_MODEL), tiles.mod_spec(), full(norm_g),
                  pl.BlockSpec(w_in_t.shape, lambda i: (0, 0), pipeline_mode=pl.Buffered(1))],
        out_specs=tiles.merged(D_PROJ),
        out_shape=jax.ShapeDtypeStruct((x_ctx.shape[0] + x_dec.shape[0], D_PROJ), F32),
        scratch_shapes=[pltpu.VMEM((D_MODEL, D_PROJ), BF16)],
        compiler_params=_cparams(1),
        name="in_proj",
    )(x_ctx, x_dec, mod, norm_g, w_in_t)


def _time_masks(reverse):
    r = _iota((2 * CHUNK, 2 * CHUNK), 0) % CHUNK
    c = _iota((2 * CHUNK, 2 * CHUNK), 1) % CHUNK
    if reverse:
        return r < c, r <= c
    return r > c, r >= c


def _cumsum_matrix(reverse):
    r = _iota((CHUNK, CHUNK), 0)
    c = _iota((CHUNK, CHUNK), 1)
    tri = (r <= c) if reverse else (r >= c)
    return tri.astype(BF16)


def _stack_heads(x, half):
    m0 = _iota(x.shape, 1) < half
    return jnp.concatenate([jnp.where(m0, x, 0.0), jnp.where(m0, 0.0, x)], axis=0)


def _head_sums(x):
    parts = []
    for p in range(x.shape[1] // LANE):
        xp = x[:, p * LANE:(p + 1) * LANE]
        m0 = _iota(xp.shape, 1) < 64
        s0 = jnp.sum(jnp.where(m0, xp, 0.0), axis=-1, keepdims=True)
        s1 = jnp.sum(jnp.where(m0, 0.0, xp), axis=-1, keepdims=True)
        parts.append(jnp.where(m0, s0, s1))
    return parts[0] if len(parts) == 1 else jnp.concatenate(parts, axis=1)


def _rwkv_chunk_terms(insts):
    c = CHUNK
    step_row = _iota((c, LANE), 0)
    step_col = _iota((c, LANE), 1) % c
    eye_w = (step_row == step_col).astype(F32)
    same_head = (_iota((LANE, LANE), 0) // 64) == (_iota((LANE, LANE), 1) // 64)
    stack_bf = lambda x: _stack_heads(x, 64).astype(BF16)
    cums = [_mm_01_lhs(_cumsum_matrix(rev), lw) for (_, lw, _, _, _, _, rev) in insts]
    pre = []
    for (r, lw, kd, a, b, v, rev), cum in zip(insts, cums):
        end = cum[0:1] if rev else cum[c - 1:c]
        inv_w = jnp.exp(-cum)
        rem_w = jnp.exp(end - cum)
        a_t = a * jnp.exp(cum - lw)
        r_t = r * jnp.exp(cum)
        bk_s = jnp.concatenate([stack_bf(b * inv_w), stack_bf(kd * inv_w)], axis=0)
        bkh = jnp.concatenate([b * rem_w, kd * rem_w], axis=0).astype(BF16)
        pre.append((a_t, r_t, bk_s, bkh, v, jnp.exp(end)))
    ms = [_dot(jnp.concatenate([a_t, r_t], axis=0).astype(BF16), bk_s, _NT) for (a_t, r_t, bk_s, _, _, _) in pre]
    mats = []
    for m, (_, _, _, _, _, _, rev) in zip(ms, insts):
        strict = (step_row < step_col) if rev else (step_row > step_col)
        incl = (step_row <= step_col) if rev else (step_row >= step_col)
        l_ab = jnp.where(strict, m[0:c, 0:LANE], 0.0)
        l_akrk = jnp.concatenate([jnp.where(strict, m[0:c, LANE:2 * LANE], 0.0),
                                  jnp.where(incl, m[c:2 * c, LANE:2 * LANE], 0.0)], axis=0).astype(BF16)
        m_rb = jnp.where(incl, m[c:2 * c, 0:LANE], 0.0).astype(BF16)
        mats.append((l_ab, l_akrk, m_rb))
    lvs = [_dot(l_akrk, stack_bf(pr[4])) for (_, l_akrk, _), pr in zip(mats, pre)]
    ps = [eye_w + l_ab for (l_ab, _, _) in mats]
    lps = [_dot(l_ab.astype(BF16), stack_bf(l_ab)) for (l_ab, _, _) in mats]
    for level in range(1, 6):
        if level < 5:
            xs = [_dot(lp.astype(BF16), jnp.concatenate([stack_bf(p), stack_bf(lp)], axis=1))
                  for lp, p in zip(lps, ps)]
            ps = [p + x[:, 0:LANE] for p, x in zip(ps, xs)]
            lps = [x[:, LANE:2 * LANE] for x in xs]
        else:
            ps = [p + _dot(lp.astype(BF16), stack_bf(p)) for lp, p in zip(lps, ps)]
    pxs = [_dot(p.astype(BF16), jnp.concatenate([stack_bf(pr[0]), stack_bf(lv[0:c])], axis=1))
           for p, pr, lv in zip(ps, pre, lvs)]
    mzs = [_dot(mt[2], jnp.concatenate([stack_bf(px[:, 0:LANE]), stack_bf(px[:, LANE:2 * LANE])], axis=1))
           for mt, px in zip(mats, pxs)]
    ts = [_dot(px[:, 0:LANE].astype(BF16), pr[3][0:c], _TN) for px, pr in zip(pxs, pre)]
    gs = [_dot(jnp.concatenate([px[:, LANE:2 * LANE], pr[4]], axis=0).astype(BF16), pr[3], _TN)
          for px, pr in zip(pxs, pre)]
    out = []
    for pr, lv, mz, t, g in zip(pre, lvs, mzs, ts, gs):
        q = pr[1] + mz[:, 0:LANE]
        y0 = mz[:, LANE:2 * LANE] + lv[c:2 * c]
        out.append((jnp.where(same_head, t, 0.0).astype(BF16), jnp.where(same_head, g, 0.0), pr[5],
                    q.astype(BF16), y0))
    return out


def _gla_chunk_terms(insts):
    c = CHUNK
    step_row = _iota((c, LANE), 0)
    step_col = _iota((c, LANE), 1) % c
    same_head = (_iota((LANE, 2 * LANE), 0) // 64) == (_iota((LANE, 2 * LANE), 1) // LANE)
    cums = [_mm_01_lhs(_cumsum_matrix(rev), g) for (_, _, _, g, rev) in insts]
    pre = []
    for (q, k, v, g, rev), cum in zip(insts, cums):
        end = cum[0:1] if rev else cum[c - 1:c]
        qt = (q * jnp.exp(cum)).astype(BF16)
        k_s = _stack_heads(k * jnp.exp(-cum), 64).astype(BF16)
        kh = (k * jnp.exp(end - cum)).astype(BF16)
        a_col = jnp.broadcast_to(jnp.exp(end), (LANE, LANE)).T
        pre.append((qt, k_s, kh, v.astype(BF16), _stack_heads(v, LANE).astype(BF16), a_col))
    atts = [_dot(pr[0], pr[1], _NT) for pr in pre]
    atts = [jnp.where((step_row <= step_col) if inst[4] else (step_row >= step_col), att, 0.0).astype(BF16)
            for att, inst in zip(atts, insts)]
    o0s = [_dot(att, pr[4]) for att, pr in zip(atts, pre)]
    kvs = [jnp.where(same_head, _dot(pr[2], pr[3], _TN), 0.0) for pr in pre]
    return [(pr[0], o0, pr[5], kv) for pr, o0, kv in zip(pre, o0s, kvs)]


def _rwkv_kernel(seq_len, is_grid, zero_init, r_ref, k_ref, v_ref, lora_ref, cwr_ref, cwk_ref, cwv_ref, w2_ref,
                 a2_ref, g2_ref, vec_ref, *rest):
    s0_ref = None if zero_init else rest[0]
    (y_ref, sout_ref, pad_ref, bonus_ref, gate_ref, yf_ref, yb_ref, st_ref, tt_ref, tg_ref, tw_ref, tq_ref,
     ty_ref) = rest[0 if zero_init else 1:][:13]
    left_ref, right_ref = rest[-2:] if is_grid else (None, None)
    n_chunks = seq_len // CHUNK
    npp = RW_PAIRS_PER_STEP
    w = npp * LANE
    pair = lambda x, p: x[:, p * LANE:(p + 1) * LANE]
    vec = vec_ref[...]
    w0 = (vec[0:1], vec[1:2])
    a0 = (vec[2:3], vec[3:4])
    k_k, k_a, r_k, ln_w, ln_b = vec[4:5], vec[5:6], vec[6:7], vec[7:8], vec[8:9]
    block_sum = _head_sums

    zeros = jnp.zeros((CONV_PAD, 3 * w), F32)
    pad_ref[0:CONV_PAD, :] = zeros
    pad_ref[CONV_PAD + seq_len:2 * CONV_PAD + seq_len, :] = zeros
    pad_ref[CONV_PAD:CONV_PAD + seq_len, 0:w] = r_ref[...]
    pad_ref[CONV_PAD:CONV_PAD + seq_len, w:2 * w] = k_ref[...]
    pad_ref[CONV_PAD:CONV_PAD + seq_len, 2 * w:3 * w] = v_ref[...]
    cw = jnp.concatenate([cwr_ref[...], cwk_ref[...], cwv_ref[...]], axis=1)

    if is_grid:
        col = _iota((CHUNK, 3 * w), 0)
        left_ref[0:CONV_PAD, :] = zeros
        left_ref[CONV_PAD + seq_len:2 * CONV_PAD + seq_len, :] = zeros
        right_ref[0:CONV_PAD, :] = zeros
        right_ref[CONV_PAD + seq_len:2 * CONV_PAD + seq_len, :] = zeros

        def shift_body(c, carry):
            base = pl.multiple_of(CONV_PAD + c * CHUNK, CHUNK)
            win = pad_ref[pl.ds(base - 8, CHUNK + 16), :]
            left_ref[pl.ds(base, CHUNK), :] = jnp.where(col >= 1, win[7:7 + CHUNK], 0.0)
            right_ref[pl.ds(base, CHUNK), :] = jnp.where(col <= GRID_W - 2, win[9:9 + CHUNK], 0.0)
            return carry

        lax.fori_loop(0, n_chunks, shift_body, 0)

    def conv_chunk(c):
        base = pl.multiple_of(CONV_PAD + c * CHUNK, CHUNK)
        acc = jnp.zeros((CHUNK, 3 * w), F32)
        if is_grid:
            for di in (-1, 0, 1):
                row = pl.ds(base + di * GRID_W, CHUNK)
                for dj, src in ((-1, left_ref), (0, pad_ref), (1, right_ref)):
                    tap = (di + 1) * 3 + dj + 1
                    acc = acc + src[row, :] * cw[tap:tap + 1]
        else:
            win = pad_ref[pl.ds(base - 8, CHUNK + 16), :]
            for dj in (-1, 0, 1):
                acc = acc + win[8 + dj:8 + dj + CHUNK] * cw[4 + dj:5 + dj]
        return acc[:, 0:w], acc[:, w:2 * w], acc[:, 2 * w:3 * w]

    def terms_body(j, carry):
        cs = [j * TERM_UNROLL + u for u in range(TERM_UNROLL)]
        offs = [pl.multiple_of(c * CHUNK, CHUNK) for c in cs]
        rkv = [conv_chunk(c) for c in cs]
        loras = [lora_ref[pl.ds(off, CHUNK), :] for off in offs]
        kks = [kc * k_k for (_, kc, _) in rkv]
        kk_ss = [block_sum(kk * kk) for kk in kks]
        bon_ss = [block_sum(rc * kc * r_k) for (rc, kc, _) in rkv]
        gates = [_mm3_split(_split2(_sigmoid(lo[:, 2 * LANE:3 * LANE])), (g2_ref[0], g2_ref[1])) for lo in loras]
        lw_ins = [_split2(jnp.tanh(lo[:, 0:LANE])) for lo in loras]
        la_ins = [_split2(lo[:, LANE:2 * LANE]) for lo in loras]
        lws = [[_mm3_split(x, (w2_ref[0, d], w2_ref[1, d])) for d in range(2)] for x in lw_ins]
        ags = [[_mm3_split(x, (a2_ref[0, d], a2_ref[1, d])) for d in range(2)] for x in la_ins]
        insts, where = [], []
        for u in range(TERM_UNROLL):
            rc, kc, vc = rkv[u]
            kk = kks[u] * lax.rsqrt(kk_ss[u] + EPS)
            bonus_ref[pl.ds(offs[u], CHUNK), :] = bon_ss[u] * vc
            gate_ref[pl.ds(offs[u], CHUNK), :] = gates[u]
            for d in range(2):
                lw = -RW_DECAY_SCALE * _sigmoid(w0[d] + lws[u][d])
                ag = _sigmoid(a0[d] + ags[u][d])
                kd = kc * (1.0 + (ag - 1.0) * k_a)
                kb = kk * ag
                for p in range(npp):
                    insts.append((pair(rc, p), pair(lw, p), pair(kd, p), -pair(kk, p), pair(kb, p), pair(vc, p),
                                  d == 1))
                    where.append((d, p, cs[u]))
        for (d, p, c), (t, g, w_end, q, y0) in zip(where, _rwkv_chunk_terms(insts)):
            tt_ref[d, p, c] = t
            tg_ref[d, p, c] = g
            tw_ref[d, p, c] = jnp.broadcast_to(w_end, (8, LANE))
            tq_ref[d, p, c] = q
            ty_ref[d, p, c] = y0
        return carry

    lax.fori_loop(0, n_chunks // TERM_UNROLL, terms_body, 0)

    chains = [(d, p) for d in range(2) for p in range(npp)]
    for d, p in chains:
        if zero_init:
            st_ref[d, p] = jnp.zeros((LANE, LANE), F32)
        else:
            z = jnp.zeros((64, 64), F32)
            st_ref[d, p] = jnp.concatenate([jnp.concatenate([s0_ref[0, d, 2 * p], z], axis=1),
                                            jnp.concatenate([z, s0_ref[0, d, 2 * p + 1]], axis=1)], axis=0)

    def scan_body(i, carry):
        cs = (i, n_chunks - 1 - i)
        ss = [st_ref[d, p] for d, p in chains]
        sb = [s.astype(BF16) for s in ss]
        ys = [_dot(tq_ref[d, p, cs[d]], b, _NT) + ty_ref[d, p, cs[d]] for (d, p), b in zip(chains, sb)]
        sn = [s * tw_ref[d, p, cs[d]][0:1] + _dot(b, tt_ref[d, p, cs[d]]) + tg_ref[d, p, cs[d]]
              for (d, p), s, b in zip(chains, ss, sb)]
        for (d, p), y, s in zip(chains, ys, sn):
            st_ref[d, p] = s
            out_ref = yf_ref if d == 0 else yb_ref
            out_ref[pl.ds(pl.multiple_of(cs[d] * CHUNK, CHUNK), CHUNK), p * LANE:(p + 1) * LANE] = y
        return carry

    lax.fori_loop(0, n_chunks, scan_body, 0)
    for d, p in chains:
        s = st_ref[d, p]
        sout_ref[0, d, 2 * p] = s[0:64, 0:64]
        sout_ref[0, d, 2 * p + 1] = s[64:LANE, 64:LANE]

    def post_body(j, carry):
        offs = [pl.multiple_of((j * TERM_UNROLL + u) * CHUNK, CHUNK) for u in range(TERM_UNROLL)]
        ys = [yf_ref[pl.ds(off, CHUNK), :] + yb_ref[pl.ds(off, CHUNK), :] for off in offs]
        mus = [block_sum(y) * (1.0 / 64) for y in ys]
        dlts = [y - mu for y, mu in zip(ys, mus)]
        vrs = [block_sum(dlt * dlt) * (1.0 / 64) for dlt in dlts]
        for off, dlt, var in zip(offs, dlts, vrs):
            yn = dlt * lax.rsqrt(var + RW_LN_EPS) * ln_w + ln_b
            y_ref[pl.ds(off, CHUNK), :] = (yn + bonus_ref[pl.ds(off, CHUNK), :]) * gate_ref[pl.ds(off, CHUNK), :]
        return carry

    lax.fori_loop(0, n_chunks // TERM_UNROLL, post_body, 0)


def _rwkv_mixer(proj, first_seq, n_seq, seq_len, is_grid, prm, s0):
    n_pairs = RW_W // LANE
    n_chunks = seq_len // CHUNK
    npp = RW_PAIRS_PER_STEP
    w = npp * LANE
    assert n_chunks % TERM_UNROLL == 0 and n_pairs % npp == 0
    col = lambda cb: (lambda b, p: (b + first_seq, cb // npp + p))
    par = lambda cb: (lambda b, p: (0, cb // npp + p))
    state_spec = pl.BlockSpec((1, 2, 2 * npp, 64, 64), lambda b, p: (b, 0, p, 0, 0))
    kernel = functools.partial(_rwkv_kernel, seq_len, is_grid, s0 is None)
    y, s_out = pl.pallas_call(
        kernel,
        grid=(n_seq, n_pairs // npp),
        in_specs=[pl.BlockSpec((seq_len, w), col(CB_R)),
                  pl.BlockSpec((seq_len, w), col(CB_K)),
                  pl.BlockSpec((seq_len, w), col(CB_V)),
                  pl.BlockSpec((seq_len, 3 * LANE), lambda b, p: (b + first_seq, CB_LORA // 3)),
                  pl.BlockSpec((9, w), par(CB_R)),
                  pl.BlockSpec((9, w), par(CB_K)),
                  pl.BlockSpec((9, w), par(CB_V)),
                  pl.BlockSpec((2, 2, LANE, w), lambda b, p: (0, 0, 0, p)),
                  pl.BlockSpec((2, 2, LANE, w), lambda b, p: (0, 0, 0, p)),
                  pl.BlockSpec((2, LANE, w), lambda b, p: (0, 0, p)),
                  pl.BlockSpec((16, w), lambda b, p: (0, p))] + ([] if s0 is None else [state_spec]),
        out_specs=[pl.BlockSpec((seq_len, w), lambda b, p: (b, p)), state_spec],
        out_shape=[jax.ShapeDtypeStruct((n_seq * seq_len, RW_W), F32),
                   jax.ShapeDtypeStruct((n_seq, 2, 2 * n_pairs, 64, 64), F32)],
        scratch_shapes=[pltpu.VMEM((seq_len + 2 * CONV_PAD, 3 * w), F32)]
                       + [pltpu.VMEM((seq_len, w), F32)] * 4
                       + [pltpu.VMEM((2, npp, LANE, LANE), F32),
                          pltpu.VMEM((2, npp, n_chunks, LANE, LANE), BF16),
                          pltpu.VMEM((2, npp, n_chunks, LANE, LANE), F32),
                          pltpu.VMEM((2, npp, n_chunks, 8, LANE), F32),
                          pltpu.VMEM((2, npp, n_chunks, CHUNK, LANE), BF16),
                          pltpu.VMEM((2, npp, n_chunks, CHUNK, LANE), F32)]
                       + ([pltpu.VMEM((seq_len + 2 * CONV_PAD, 3 * w), F32)] * 2 if is_grid else []),
        compiler_params=_cparams(2),
        name="rwkv_mixer",
    )(proj, proj, proj, proj, prm['conv'], prm['conv'], prm['conv'], prm['w2p'], prm['a2p'], prm['g2'],
      prm['vec'], *([] if s0 is None else [s0]))
    return y, s_out


def _gla_kernel(seq_len, zero_init, q_ref, k_ref, v_ref, og_ref, lgk_ref, gk2_ref, gvec_ref, *rest):
    s0_ref = None if zero_init else rest[0]
    y_ref, sout_ref, of_ref, ob_ref, st_ref, tq_ref, to_ref, ta_ref, tkv_ref = rest[0 if zero_init else 1:]
    n_chunks = seq_len // CHUNK
    npp = GLA_PAIRS
    gvec = gvec_ref[...]
    chains = [(d, p) for d in range(2) for p in range(npp)]

    def terms_body(j, carry):
        insts, where = [], []
        for u in range(TERM_UNROLL):
            c = j * TERM_UNROLL + u
            off = pl.multiple_of(c * CHUNK, CHUNK)
            lgk = _split2(lgk_ref[pl.ds(off, CHUNK), :])
            qc = q_ref[pl.ds(off, CHUNK), :] * GLA_Q_SCALE
            kc = k_ref[pl.ds(off, CHUNK), :]
            vc = v_ref[pl.ds(off, CHUNK), :]
            for d in range(2):
                x = _mm3_split(lgk, (gk2_ref[0, d], gk2_ref[1, d])) + gvec[d:d + 1, 0:GLA_QK_W]
                g = _log_sigmoid(x) * (1.0 / GLA_GATE_NORM)
                for p in range(npp):
                    qk = slice(p * LANE, (p + 1) * LANE)
                    insts.append((qc[:, qk], kc[:, qk], vc[:, 2 * p * LANE:2 * (p + 1) * LANE], g[:, qk], d == 1))
                    where.append((d, p, c))
        for (d, p, c), (qt, o0, a_col, kv) in zip(where, _gla_chunk_terms(insts)):
            tq_ref[d, p, c] = qt
            to_ref[d, p, c] = o0
            ta_ref[d, p, c] = a_col
            tkv_ref[d, p, c] = kv
        return carry

    lax.fori_loop(0, n_chunks // TERM_UNROLL, terms_body, 0)

    for d, p in chains:
        if zero_init:
            st_ref[d, p] = jnp.zeros((LANE, 2 * LANE), F32)
        else:
            z = jnp.zeros((64, LANE), F32)
            st_ref[d, p] = jnp.concatenate([jnp.concatenate([s0_ref[0, d, 2 * p], z], axis=1),
                                            jnp.concatenate([z, s0_ref[0, d, 2 * p + 1]], axis=1)], axis=0)

    def scan_body(i, carry):
        cs = (i, n_chunks - 1 - i)
        ss = [st_ref[d, p] for d, p in chains]
        os_ = [_dot(tq_ref[d, p, cs[d]], s.astype(BF16)) + to_ref[d, p, cs[d]] for (d, p), s in zip(chains, ss)]
        for (d, p), s, o in zip(chains, ss, os_):
            a_col = ta_ref[d, p, cs[d]]
            st_ref[d, p] = s * jnp.concatenate([a_col, a_col], axis=1) + tkv_ref[d, p, cs[d]]
            out_ref = of_ref if d == 0 else ob_ref
            out_ref[pl.ds(pl.multiple_of(cs[d] * CHUNK, CHUNK), CHUNK), 2 * p * LANE:2 * (p + 1) * LANE] = o
        return carry

    lax.fori_loop(0, n_chunks, scan_body, 0)
    for d, p in chains:
        s = st_ref[d, p]
        sout_ref[0, d, 2 * p] = s[0:64, 0:LANE]
        sout_ref[0, d, 2 * p + 1] = s[64:LANE, LANE:2 * LANE]

    def post_body(c, carry):
        off = pl.multiple_of(c * CHUNK, CHUNK)
        for h in range(2 * npp):
            hs = slice(h * LANE, (h + 1) * LANE)
            o = of_ref[pl.ds(off, CHUNK), hs] + ob_ref[pl.ds(off, CHUNK), hs]
            gate = _silu(og_ref[pl.ds(off, CHUNK), hs])
            y_ref[pl.ds(off, CHUNK), hs] = _rmsnorm_rows(o) * gvec[2:3, hs] * gate
        return carry

    lax.fori_loop(0, n_chunks, post_body, 0)


def _gla_mixer(proj, first_seq, n_seq, seq_len, prm, s0):
    npp = GLA_PAIRS
    n_heads = 2 * npp
    n_chunks = seq_len // CHUNK
    assert n_chunks % TERM_UNROLL == 0
    state_spec = pl.BlockSpec((1, 2, n_heads, 64, LANE), lambda b: (b, 0, 0, 0, 0))
    kernel = functools.partial(_gla_kernel, seq_len, s0 is None)
    y, s_out = pl.pallas_call(
        kernel,
        grid=(n_seq,),
        in_specs=[pl.BlockSpec((seq_len, GLA_QK_W), lambda b: (b + first_seq, CB_GQ * LANE // GLA_QK_W)),
                  pl.BlockSpec((seq_len, GLA_QK_W), lambda b: (b + first_seq, CB_GK * LANE // GLA_QK_W)),
                  pl.BlockSpec((seq_len, GLA_V_W), lambda b: (b + first_seq, CB_GV * LANE // GLA_V_W)),
                  pl.BlockSpec((seq_len, GLA_V_W), lambda b: (b + first_seq, CB_OG * LANE // GLA_V_W)),
                  pl.BlockSpec((seq_len, LANE), lambda b: (b + first_seq, CB_LGK)),
                  pl.BlockSpec((2, 2, LANE, GLA_QK_W), lambda b: (0, 0, 0, 0)),
                  pl.BlockSpec((8, GLA_V_W), lambda b: (0, 0))] + ([] if s0 is None else [state_spec]),
        out_specs=[pl.BlockSpec((seq_len, GLA_V_W), lambda b: (b, 0)), state_spec],
        out_shape=[jax.ShapeDtypeStruct((n_seq * seq_len, GLA_V_W), F32),
                   jax.ShapeDtypeStruct((n_seq, 2, n_heads, 64, LANE), F32)],
        scratch_shapes=[pltpu.VMEM((seq_len, GLA_V_W), F32)] * 2
                       + [pltpu.VMEM((2, npp, LANE, 2 * LANE), F32),
                          pltpu.VMEM((2, npp, n_chunks, CHUNK, LANE), BF16),
                          pltpu.VMEM((2, npp, n_chunks, CHUNK, 2 * LANE), F32),
                          pltpu.VMEM((2, npp, n_chunks, LANE, LANE), F32),
                          pltpu.VMEM((2, npp, n_chunks, LANE, 2 * LANE), F32)],
        compiler_params=_cparams(1),
        name="gla_mixer",
    )(proj, proj, proj, proj, proj, prm['gk2p'], prm['gvec'], *([] if s0 is None else [s0]))
    return y, s_out


OUT_TM = 512
ROUTE_NEG = -1e30
LANE_GROUP0 = N_EXPERTS


def _route(logits):
    lane = _iota(logits.shape, 1)
    lane_f = lane.astype(F32)
    big = float(LANE)
    is_g = (lane >= LANE_GROUP0) & (lane < LANE_GROUP0 + 4)
    gmax = jnp.max(jnp.where(is_g, logits, ROUTE_NEG), axis=-1, keepdims=True)
    gidx = jnp.min(jnp.where(is_g & (logits == gmax), lane_f, big), axis=-1, keepdims=True) - LANE_GROUP0
    gsum = jnp.sum(jnp.where(is_g, jnp.exp(jnp.minimum(logits - gmax, 0.0)), 0.0), axis=-1, keepdims=True)
    g_w = 1.0 / gsum
    in_grp = (lane < N_EXPERTS) & ((lane // 4).astype(F32) == gidx)
    m1 = jnp.max(jnp.where(in_grp, logits, ROUTE_NEG), axis=-1, keepdims=True)
    i1 = jnp.min(jnp.where(in_grp & (logits == m1), lane_f, big), axis=-1, keepdims=True)
    rest = in_grp & (lane_f != i1)
    m2 = jnp.max(jnp.where(rest, logits, ROUTE_NEG), axis=-1, keepdims=True)
    i2 = jnp.min(jnp.where(rest & (logits == m2), lane_f, big), axis=-1, keepdims=True)
    t = jnp.exp(m2 - m1)
    w1 = g_w / (1.0 + t)
    return jnp.where(lane_f == i1, w1, 0.0) + jnp.where(lane_f == i2, w1 * t, 0.0)


def _outproj_kernel(tiles, yrc_ref, yrd_ref, ygc_ref, ygd_ref, xc_ref, xd_ref, mod_ref, wo_ref, g_ref, wr_ref,
                    br_ref, x1_ref, h2_ref, cmb_ref):
    def run(yr_ref, yg_ref, x_ref):
        m = mod_ref[0]
        mix = _mm(yr_ref[...], wo_ref[0:RW_W, :]) + _mm(yg_ref[...], wo_ref[RW_W:RW_W + GLA_V_W, :])
        x1 = x_ref[...] + m[2:3] * mix
        h2 = _rmsnorm_rows(x1) * g_ref[...] * (1.0 + m[4:5]) + m[3:4]
        x1_ref[...] = x1
        h2_ref[...] = h2.astype(BF16)
        cmb_ref[...] = _route(_mm3_split(_split2(h2), (wr_ref[0], wr_ref[1])) + br_ref[...])

    tiles.by_pass(functools.partial(run, yrc_ref, ygc_ref, xc_ref), functools.partial(run, yrd_ref, ygd_ref, xd_ref))


def _out_projection(tiles, y_rw, y_gla, x, mod, w_out, norm_g, w_route, b_route):
    n = x[0].shape[0] + x[1].shape[0]
    full = lambda a: pl.BlockSpec(a.shape, lambda i: (0,) * a.ndim)
    return pl.pallas_call(
        functools.partial(_outproj_kernel, tiles),
        grid=(tiles.n_ctx + tiles.n_dec,),
        in_specs=[*tiles.specs(RW_W), *tiles.specs(GLA_V_W), *tiles.specs(D_MODEL), tiles.mod_spec(),
                  full(w_out), full(norm_g), full(w_route), full(b_route)],
        out_specs=[tiles.merged(D_MODEL), tiles.merged(D_MODEL), tiles.merged(LANE)],
        out_shape=[jax.ShapeDtypeStruct((n, D_MODEL), F32), jax.ShapeDtypeStruct((n, D_MODEL), BF16),
                   jax.ShapeDtypeStruct((n, LANE), F32)],
        compiler_params=_cparams(1),
        name="out_proj_router",
    )(*y_rw, *y_gla, *x, mod, w_out, norm_g, w_route, b_route)


MOE_TM = 512
MOE_RB = 128
MOE_INTERLEAVE = 4
SLOT_ALIGN = 16
MOE_SLOTS = 2 * MOE_TM + N_EXPERTS * SLOT_ALIGN + MOE_RB


def _stage_expert_weights(srcs_hbm, dst_refs, stage_refs, sems):
    def copies(e):
        return [pltpu.make_async_copy(src.at[e], stage.at[e % 2], sem.at[e % 2])
                for src, stage, sem in zip(srcs_hbm, stage_refs, sems)]

    for c in copies(0):
        c.start()
    for e in range(N_EXPERTS):
        if e + 1 < N_EXPERTS:
            for c in copies(e + 1):
                c.start()
        for c, dst, stage in zip(copies(e), dst_refs, stage_refs):
            c.wait()
            dst[e] = stage[e % 2].astype(BF16)


def _moe_kernel(tiles, h2_ref, cmb_ref, x1_ref, mod_ref, w1_hbm, w3_hbm, w2_hbm, fg_ref, yc_ref, yd_ref,
                xs_ref, ys_ref, w1_ref, w3_ref, w2_ref, stage1_ref, stage3_ref, stage2_ref, sem1, sem3, sem2):
    @pl.when(pl.program_id(0) == 0)
    def _():
        _stage_expert_weights((w1_hbm, w3_hbm, w2_hbm), (w1_ref, w3_ref, w2_ref),
                              (stage1_ref, stage3_ref, stage2_ref), (sem1, sem3, sem2))

    cmb = cmb_ref[...]
    lane = _iota(cmb.shape, 1).astype(F32)
    sel = cmb > 0.0
    sel01 = jnp.where(sel, 1.0, 0.0).astype(BF16)
    before = (_iota((MOE_TM, MOE_TM), 0) > _iota((MOE_TM, MOE_TM), 1)).astype(BF16)
    pos = _dot(before, sel01)
    cnt = pos[MOE_TM - 1:MOE_TM] + sel01[MOE_TM - 1:MOE_TM].astype(F32)
    seg = jnp.floor((cnt + (SLOT_ALIGN - 1)) * (1.0 / SLOT_ALIGN))
    lower_experts = (_iota((LANE, LANE), 0) < _iota((LANE, LANE), 1)).astype(BF16)
    start = _dot(jnp.broadcast_to(seg, (8, LANE)).astype(BF16), lower_experts)[0:1] * SLOT_ALIGN
    n_blk = jnp.floor((cnt + (MOE_RB - 1)) * (1.0 / MOE_RB)).astype(jnp.int32)
    start_i = start.astype(jnp.int32)
    cnt_i = cnt.astype(jnp.int32)
    slot = start + pos
    e_a = jnp.min(jnp.where(sel, lane, float(LANE)), axis=-1, keepdims=True)
    e_b = jnp.max(jnp.where(sel, lane, -1.0), axis=-1, keepdims=True)
    pick = lambda e, x: jnp.sum(jnp.where(lane == e, x, 0.0), axis=-1, keepdims=True)
    slot_a, w_a = pick(e_a, slot), pick(e_a, cmb)
    slot_b = jnp.where(e_b != e_a, pick(e_b, slot), -1.0)
    w_b = pick(e_b, cmb)

    slots_t = jnp.where(lane == 0.0, slot_a, jnp.where(lane == 1.0, slot_b, -1.0)).T
    row_slot = _iota((MOE_SLOTS, MOE_TM), 0).astype(F32)
    gather = jnp.where((row_slot == slots_t[0:1]) | (row_slot == slots_t[1:2]), 1.0, 0.0).astype(BF16)
    xs_ref[...] = _dot(gather, h2_ref[...]).astype(BF16)
    ys_ref[...] = jnp.zeros_like(ys_ref)

    row_in_blk = _iota((MOE_RB, D_MODEL), 0)

    def expert_blocks(experts, r0s, ends):
        xbs = [xs_ref[pl.ds(r0, MOE_RB), :] for r0 in r0s]
        gates = [_dot(xb, w3_ref[e]) for xb, e in zip(xbs, experts)]
        ups = [_dot(xb, w1_ref[e]) for xb, e in zip(xbs, experts)]
        acts = [(_silu(g) * u).astype(BF16) for g, u in zip(gates, ups)]
        outs = [_dot(a, w2_ref[e]) for a, e in zip(acts, experts)]
        for r0, end, out in zip(r0s, ends, outs):
            keep = row_in_blk + r0 >= end
            ys_ref[pl.ds(r0, MOE_RB), :] = jnp.where(keep, ys_ref[pl.ds(r0, MOE_RB), :], out.astype(BF16))

    seg_start = [pl.multiple_of(start_i[0, e], SLOT_ALIGN) for e in range(N_EXPERTS)]
    seg_end = [seg_start[e] + cnt_i[0, e] for e in range(N_EXPERTS)]
    for e0 in range(0, N_EXPERTS, MOE_INTERLEAVE):
        es = list(range(e0, e0 + MOE_INTERLEAVE))
        expert_blocks(es, [seg_start[e] for e in es], [seg_end[e] for e in es])
    for e in range(N_EXPERTS):
        def extra_block(b, carry, e=e):
            expert_blocks([e], [pl.multiple_of(seg_start[e] + b * MOE_RB, SLOT_ALIGN)], [seg_end[e]])
            return carry

        lax.fori_loop(1, n_blk[0, e], extra_block, 0)

    col_slot = _iota((MOE_TM, MOE_SLOTS), 1).astype(F32)
    scatter = (jnp.where(col_slot == slot_a, w_a, 0.0) + jnp.where(col_slot == slot_b, w_b, 0.0)).astype(BF16)
    x2 = x1_ref[...] + mod_ref[0][5:6] * _dot(scatter, ys_ref[...])
    y = _rmsnorm_rows(x2) * fg_ref[...]

    def write(y_ref):
        y_ref[...] = y

    tiles.by_pass(functools.partial(write, yc_ref), functools.partial(write, yd_ref))


def _moe(tiles, h2, cmb, x1, mod, w1, w3, w2, final_g):
    assert tiles.tm == MOE_TM
    hbm = pl.BlockSpec(memory_space=pl.ANY)
    out_ctx, out_dec = tiles.specs(D_MODEL)
    return pl.pallas_call(
        functools.partial(_moe_kernel, tiles),
        grid=(tiles.n_ctx + tiles.n_dec,),
        in_specs=[tiles.merged(D_MODEL), tiles.merged(LANE), tiles.merged(D_MODEL), tiles.mod_spec(),
                  hbm, hbm, hbm, pl.BlockSpec((1, D_MODEL), lambda i: (0, 0))],
        out_specs=[out_ctx, out_dec],
        out_shape=[jax.ShapeDtypeStruct((tiles.n_ctx * MOE_TM, D_MODEL), F32),
                   jax.ShapeDtypeStruct((tiles.n_dec * MOE_TM, D_MODEL), F32)],
        scratch_shapes=[pltpu.VMEM((MOE_SLOTS, D_MODEL), BF16), pltpu.VMEM((MOE_SLOTS, D_MODEL), BF16),
                        pltpu.VMEM((N_EXPERTS, D_MODEL, D_EXPERT), BF16),
                        pltpu.VMEM((N_EXPERTS, D_MODEL, D_EXPERT), BF16),
                        pltpu.VMEM((N_EXPERTS, D_EXPERT, D_MODEL), BF16),
                        pltpu.VMEM((2, D_MODEL, D_EXPERT), F32), pltpu.VMEM((2, D_MODEL, D_EXPERT), F32),
                        pltpu.VMEM((2, D_EXPERT, D_MODEL), F32)] + [pltpu.SemaphoreType.DMA((2,))] * 3,
        compiler_params=_cparams(1),
        name="moe_experts",
    )(h2, cmb, x1, mod, w1, w3, w2, final_g)


def _pad_rows(x, rows):
    return jnp.pad(x, ((0, rows - x.shape[0]),) + ((0, 0),) * (x.ndim - 1))


def _pack_params(l, w_in, rw_conv, rw_w0, rw_w2, rw_a0, rw_a2, rw_g2, rw_k_k, rw_k_a, rw_r_k, rw_ln_w, rw_ln_b,
                 gla_gk2, gla_gk_b, gla_norm_g, moe_w_group, moe_b_group, moe_w_expert, moe_b_expert):
    wi = w_in[l]
    z = lambda n: jnp.zeros((D_MODEL, n), F32)
    w_in_t = jnp.swapaxes(wi, 0, 1)
    z64 = jnp.zeros((64, RW_W), F32)
    w2p = jnp.stack([jnp.concatenate([rw_w2[l, 0], z64], 0), jnp.concatenate([z64, rw_w2[l, 1]], 0)])
    a2p = jnp.stack([jnp.concatenate([rw_a2[l, 0], z64], 0), jnp.concatenate([z64, rw_a2[l, 1]], 0)])
    vec = _pad_rows(jnp.stack([rw_w0[l, 0], rw_w0[l, 1], rw_a0[l, 0], rw_a0[l, 1], rw_k_k[l], rw_k_a[l],
                               rw_r_k[l].reshape(RW_W), rw_ln_w[l], rw_ln_b[l]]), 16)
    rw = {'conv': rw_conv[l].reshape(9, 3 * RW_W), 'w2p': _split_param(w2p), 'a2p': _split_param(a2p),
          'g2': _split_param(rw_g2[l]), 'vec': vec}
    gk2p = jnp.stack([_pad_rows(gla_gk2[l, 0], LANE),
                      _pad_rows(jnp.concatenate([jnp.zeros((16, GLA_QK_W), F32), gla_gk2[l, 1]], 0), LANE)])
    gk_b = jnp.pad(gla_gk_b[l], ((0, 0), (0, GLA_V_W - GLA_QK_W)))
    gvec = _pad_rows(jnp.concatenate([gk_b, jnp.tile(gla_norm_g[l], GLA_V_W // LANE)[None]], axis=0), 8)
    gla = {'gk2p': _split_param(gk2p), 'gvec': gvec}
    w_route = _split_param(jnp.concatenate([moe_w_expert[l], moe_w_group[l], z(LANE - N_EXPERTS - 4)], axis=1))
    b_route = jnp.concatenate([moe_b_expert[l], moe_b_group[l], jnp.zeros((LANE - N_EXPERTS - 4,), F32)])[None]
    return w_in_t, rw, gla, w_route, b_route


def kernel(x_prompt, x_sample, state_rwkv, state_gla, c, c_ctx, norm1_g, norm2_g, w_ada, b_ada, w_in, w_out,
           rw_conv, rw_w0, rw_w2, rw_a0, rw_a2, rw_g2, rw_k_k, rw_k_a, rw_r_k, rw_ln_w, rw_ln_b,
           gla_gk2, gla_gk_b, gla_norm_g, moe_w_group, moe_b_group, moe_w_expert, moe_b_expert,
           moe_w1, moe_w3, moe_w2, final_g):
    depth = w_in.shape[0]
    assert depth == 1, "the packed layout below handles the single-layer trunk of this problem"
    l = 0
    n_dec = x_sample.shape[0]
    ctx_row = n_dec
    cond8 = _pad_rows(jnp.concatenate([c, c_ctx[None]], axis=0), 8)
    mod = _modulation(cond8, w_ada[l], b_ada[l][None]).reshape(8, N_MOD, D_MODEL)
    pk = _pack_params(l, w_in, rw_conv, rw_w0, rw_w2, rw_a0, rw_a2, rw_g2, rw_k_k, rw_k_a, rw_r_k, rw_ln_w,
                      rw_ln_b, gla_gk2, gla_gk_b, gla_norm_g, moe_w_group, moe_b_group, moe_w_expert,
                      moe_b_expert)
    w_in_t, rw, gla, w_route, b_route = pk

    n_ctx, ctx_len, _ = x_prompt.shape
    dec_len = x_sample.shape[1]
    x_ctx = x_prompt.reshape(n_ctx * ctx_len, D_MODEL)
    x_dec = x_sample.reshape(n_dec * dec_len, D_MODEL)
    assert (n_ctx * ctx_len) % dec_len == 0, "denoising sequences must start on a dec_len row block of proj"
    first_dec = n_ctx * ctx_len // dec_len
    tiles = _Tiles(n_ctx * ctx_len, n_dec * dec_len, dec_len, ctx_row, PROJ_TM)
    assert PROJ_TM == OUT_TM == MOE_TM

    proj = _in_projection(tiles, x_ctx, x_dec, mod, norm1_g[l][None], w_in_t)
    y_rw_c, s_rw = _rwkv_mixer(proj, 0, n_ctx, ctx_len, False, rw, None)
    y_gla_c, s_gla = _gla_mixer(proj, 0, n_ctx, ctx_len, gla, None)
    y_rw_d, _ = _rwkv_mixer(proj, first_dec, n_dec, dec_len, True, rw, state_rwkv[:, l])
    y_gla_d, _ = _gla_mixer(proj, first_dec, n_dec, dec_len, gla, state_gla[:, l])
    x1, h2, cmb = _out_projection(tiles, (y_rw_c, y_rw_d), (y_gla_c, y_gla_d), (x_ctx, x_dec), mod,
                                  w_out[l].astype(BF16), norm2_g[l][None], w_route, b_route)
    y_ctx, y_dec = _moe(tiles, h2, cmb, x1, mod, moe_w1[l], moe_w3[l], moe_w2[l], final_g[None])
    return (y_ctx.reshape(x_prompt.shape), y_dec.reshape(x_sample.shape), s_rw[:, None], s_gla[:, None])
```

```python
import functools

import jax
import jax.numpy as jnp
from jax import lax
from jax.experimental import pallas as pl
from jax.experimental.pallas import tpu as pltpu

F32 = jnp.float32
BF16 = jnp.bfloat16

D_MODEL = 1024
RW_W = 512
GLA_V_W = 512
GLA_QK_W = 256
N_EXPERTS = 16
D_EXPERT = 256
N_MOD = 6
EPS = 1e-6
RW_LN_EPS = 64e-5
RW_DECAY_SCALE = 0.606531
GLA_GATE_NORM = 16.0
GLA_Q_SCALE = 64 ** -0.5
GRID_W = 64

LANE = 128
CHUNK = 64
CONV_PAD = 128
TERM_UNROLL = 4
RW_PAIRS_PER_STEP = 2
GLA_PAIRS = 2
RW_GROUPS = 4
D_PROJ = 28 * LANE
VMEM_LIMIT = 56 * 1024 * 1024

CB_R, CB_K, CB_V, CB_LORA, CB_LGK, CB_GQ, CB_GK, CB_GV, CB_OG = 0, 4, 8, 12, 15, 16, 18, 20, 24

_NN = (((1,), (0,)), ((), ()))
_NT = (((1,), (1,)), ((), ()))
_TN = (((0,), (0,)), ((), ()))


def _dot(a, b, dims=_NN):
    return lax.dot_general(a, b, dims, preferred_element_type=F32)


def _mm(a, b, dims=_NN):
    return _dot(a.astype(BF16), b.astype(BF16), dims)


def _split2(x):
    hi = x.astype(BF16)
    lo = (x - hi.astype(F32)).astype(BF16)
    return hi, lo


def _mm3(a, b, dims=_NN):
    ah, al = _split2(a)
    bh, bl = _split2(b)
    return _dot(ah, bh, dims) + _dot(ah, bl, dims) + _dot(al, bh, dims)


def _mm3_split(a_split, b_split, dims=_NN):
    (ah, al), (bh, bl) = a_split, b_split
    return _dot(ah, bh, dims) + _dot(ah, bl, dims) + _dot(al, bh, dims)


def _split_param(w):
    return jnp.stack(_split2(w))


def _mm_01_lhs(a01, b, dims=_NN):
    b1, b2 = _split2(b)
    return _dot(a01, b1, dims) + _dot(a01, b2, dims)


def _sigmoid(x):
    return 0.5 * jnp.tanh(0.5 * x) + 0.5


def _silu(x):
    return x * _sigmoid(x)


def _log_sigmoid(x):
    return jnp.minimum(x, 0.0) - jnp.log(1.0 + jnp.exp(-jnp.abs(x)))


def _iota(shape, dim):
    return lax.broadcasted_iota(jnp.int32, shape, dim)


def _cparams(n_axes):
    return pltpu.CompilerParams(dimension_semantics=("arbitrary",) * n_axes, vmem_limit_bytes=VMEM_LIMIT)


MOD_TN = 768


def _mod_kernel(c_ref, w_ref, b_ref, o_ref):
    o_ref[...] = _mm3(_silu(c_ref[...]), w_ref[...]) + b_ref[...]


def _modulation(cond8, w_ada, b_ada):
    n = w_ada.shape[1]
    return pl.pallas_call(
        _mod_kernel,
        grid=(n // MOD_TN,),
        in_specs=[pl.BlockSpec((8, D_MODEL), lambda j: (0, 0)),
                  pl.BlockSpec((D_MODEL, MOD_TN), lambda j: (0, j)),
                  pl.BlockSpec((1, MOD_TN), lambda j: (0, j))],
        out_specs=pl.BlockSpec((8, MOD_TN), lambda j: (0, j)),
        out_shape=jax.ShapeDtypeStruct((8, n), F32),
        compiler_params=_cparams(1),
        name="adaln_mod",
    )(cond8, w_ada, b_ada)


PROJ_TM = 512
D_IN = 3488
N_LGK = 32


def _rmsnorm_rows(x):
    return x * lax.rsqrt(jnp.mean(x * x, axis=-1, keepdims=True) + EPS)


class _Tiles:
    def __init__(self, n_ctx_tokens, n_dec_tokens, dec_seq_len, ctx_row, tm):
        self.tm = tm
        self.n_ctx = n_ctx_tokens // tm
        self.n_dec = n_dec_tokens // tm
        self.per_seq = dec_seq_len // tm
        self.ctx_row = ctx_row

    def specs(self, width):
        last_ctx = self.n_ctx - 1
        n_ctx = self.n_ctx
        return (pl.BlockSpec((self.tm, width), lambda i: (jnp.minimum(i, last_ctx), 0)),
                pl.BlockSpec((self.tm, width), lambda i: (jnp.maximum(i - n_ctx, 0), 0)))

    def merged(self, width):
        return pl.BlockSpec((self.tm, width), lambda i: (i, 0))

    def mod_spec(self):
        n_ctx, per_seq, ctx_row = self.n_ctx, self.per_seq, self.ctx_row
        return pl.BlockSpec((1, N_MOD, D_MODEL),
                            lambda i: (jnp.where(i < n_ctx, ctx_row, (i - n_ctx) // per_seq), 0, 0))

    def by_pass(self, run_ctx, run_dec):
        i = pl.program_id(0)
        pl.when(i < self.n_ctx)(run_ctx)
        pl.when(i >= self.n_ctx)(run_dec)


def _inproj_kernel(tiles, xc_ref, xd_ref, mod_ref, g_ref, wt_ref, o_ref, w_ref):
    @pl.when(pl.program_id(0) == 0)
    def _():
        for j in range(D_PROJ // LANE):
            if j == CB_LGK:
                blk = jnp.concatenate([wt_ref[D_IN - N_LGK:D_IN, :], jnp.zeros((LANE - N_LGK, D_MODEL), F32)], axis=0)
            else:
                src = j if j < CB_LGK else j - 1
                blk = wt_ref[src * LANE:(src + 1) * LANE, :]
            w_ref[:, j * LANE:(j + 1) * LANE] = blk.T.astype(BF16)

    def run(x_ref):
        m = mod_ref[0]
        h = _rmsnorm_rows(x_ref[...]) * g_ref[...] * (1.0 + m[1:2]) + m[0:1]
        o_ref[...] = _mm(h, w_ref[...])

    tiles.by_pass(functools.partial(run, xc_ref), functools.partial(run, xd_ref))


def _in_projection(tiles, x_ctx, x_dec, mod, norm_g, w_in_t):
    full = lambda a: pl.BlockSpec(a.shape, lambda i: (0,) * a.ndim)
    return pl.pallas_call(
        functools.partial(_inproj_kernel, tiles),
        grid=(tiles.n_ctx + tiles.n_dec,),
        in_specs=[*tiles.specs(D_MODEL), tiles.mod_spec(), full(norm_g),
                  pl.BlockSpec(w_in_t.shape, lambda i: (0, 0), pipeline_mode=pl.Buffered(1))],
        out_specs=tiles.merged(D_PROJ),
        out_shape=jax.ShapeDtypeStruct((x_ctx.shape[0] + x_dec.shape[0], D_PROJ), F32),
        scratch_shapes=[pltpu.VMEM((D_MODEL, D_PROJ), BF16)],
        compiler_params=_cparams(1),
        name="in_proj",
    )(x_ctx, x_dec, mod, norm_g, w_in_t)


def _time_masks(reverse):
    r = _iota((2 * CHUNK, 2 * CHUNK), 0) % CHUNK
    c = _iota((2 * CHUNK, 2 * CHUNK), 1) % CHUNK
    if reverse:
        return r < c, r <= c
    return r > c, r >= c


def _cumsum_matrix(reverse):
    r = _iota((CHUNK, CHUNK), 0)
    c = _iota((CHUNK, CHUNK), 1)
    tri = (r <= c) if reverse else (r >= c)
    return tri.astype(BF16)


def _stack_heads(x, half):
    m0 = _iota(x.shape, 1) < half
    return jnp.concatenate([jnp.where(m0, x, 0.0), jnp.where(m0, 0.0, x)], axis=0)


def _head_sums(x):
    parts = []
    for p in range(x.shape[1] // LANE):
        xp = x[:, p * LANE:(p + 1) * LANE]
        m0 = _iota(xp.shape, 1) < 64
        s0 = jnp.sum(jnp.where(m0, xp, 0.0), axis=-1, keepdims=True)
        s1 = jnp.sum(jnp.where(m0, 0.0, xp), axis=-1, keepdims=True)
        parts.append(jnp.where(m0, s0, s1))
    return parts[0] if len(parts) == 1 else jnp.concatenate(parts, axis=1)


def _rwkv_chunk_terms(insts, interleaved=()):
    c = CHUNK
    step_row = _iota((c, LANE), 0)
    step_col = _iota((c, LANE), 1) % c
    eye_w = (step_row == step_col).astype(F32)
    same_head = (_iota((LANE, LANE), 0) // 64) == (_iota((LANE, LANE), 1) // 64)
    stack_bf = lambda x: _stack_heads(x, 64).astype(BF16)
    cums = [_mm_01_lhs(_cumsum_matrix(rev), lw) for (_, lw, _, _, _, _, rev) in insts]
    pre = []
    for (r, lw, kd, a, b, v, rev), cum in zip(insts, cums):
        end = cum[0:1] if rev else cum[c - 1:c]
        inv_w = jnp.exp(-cum)
        rem_w = jnp.exp(end - cum)
        a_t = a * jnp.exp(cum - lw)
        r_t = r * jnp.exp(cum)
        bk_s = jnp.concatenate([stack_bf(b * inv_w), stack_bf(kd * inv_w)], axis=0)
        bkh = jnp.concatenate([b * rem_w, kd * rem_w], axis=0).astype(BF16)
        pre.append((a_t, r_t, bk_s, bkh, v, jnp.exp(end)))
    ms = [_dot(jnp.concatenate([a_t, r_t], axis=0).astype(BF16), bk_s, _NT) for (a_t, r_t, bk_s, _, _, _) in pre]
    mats = []
    for m, (_, _, _, _, _, _, rev) in zip(ms, insts):
        strict = (step_row < step_col) if rev else (step_row > step_col)
        incl = (step_row <= step_col) if rev else (step_row >= step_col)
        l_ab = jnp.where(strict, m[0:c, 0:LANE], 0.0)
        l_akrk = jnp.concatenate([jnp.where(strict, m[0:c, LANE:2 * LANE], 0.0),
                                  jnp.where(incl, m[c:2 * c, LANE:2 * LANE], 0.0)], axis=0).astype(BF16)
        m_rb = jnp.where(incl, m[c:2 * c, 0:LANE], 0.0).astype(BF16)
        mats.append((l_ab, l_akrk, m_rb))
    pending = list(interleaved)

    def run_interleaved():
        if pending:
            pending.pop(0)()

    lvs = [_dot(l_akrk, stack_bf(pr[4])) for (_, l_akrk, _), pr in zip(mats, pre)]
    run_interleaved()
    ps = [eye_w + l_ab for (l_ab, _, _) in mats]
    lps = [_dot(l_ab.astype(BF16), stack_bf(l_ab)) for (l_ab, _, _) in mats]
    for level in range(1, 6):
        if level < 5:
            xs = [_dot(lp.astype(BF16), jnp.concatenate([stack_bf(p), stack_bf(lp)], axis=1))
                  for lp, p in zip(lps, ps)]
            ps = [p + x[:, 0:LANE] for p, x in zip(ps, xs)]
            lps = [x[:, LANE:2 * LANE] for x in xs]
        else:
            ps = [p + _dot(lp.astype(BF16), stack_bf(p)) for lp, p in zip(lps, ps)]
        if level in (2, 4):
            run_interleaved()
    pxs = [_dot(p.astype(BF16), jnp.concatenate([stack_bf(pr[0]), stack_bf(lv[0:c])], axis=1))
           for p, pr, lv in zip(ps, pre, lvs)]
    run_interleaved()
    mzs = [_dot(mt[2], jnp.concatenate([stack_bf(px[:, 0:LANE]), stack_bf(px[:, LANE:2 * LANE])], axis=1))
           for mt, px in zip(mats, pxs)]
    ts = [_dot(px[:, 0:LANE].astype(BF16), pr[3][0:c], _TN) for px, pr in zip(pxs, pre)]
    gs = [_dot(jnp.concatenate([px[:, LANE:2 * LANE], pr[4]], axis=0).astype(BF16), pr[3], _TN)
          for px, pr in zip(pxs, pre)]
    while pending:
        run_interleaved()
    out = []
    for pr, lv, mz, t, g in zip(pre, lvs, mzs, ts, gs):
        q = pr[1] + mz[:, 0:LANE]
        y0 = mz[:, LANE:2 * LANE] + lv[c:2 * c]
        out.append((jnp.where(same_head, t, 0.0).astype(BF16), jnp.where(same_head, g, 0.0), pr[5],
                    q.astype(BF16), y0))
    return out


def _gla_chunk_terms(insts):
    c = CHUNK
    step_row = _iota((c, LANE), 0)
    step_col = _iota((c, LANE), 1) % c
    same_head = (_iota((LANE, 2 * LANE), 0) // 64) == (_iota((LANE, 2 * LANE), 1) // LANE)
    cums = [_mm_01_lhs(_cumsum_matrix(rev), g) for (_, _, _, g, rev) in insts]
    pre = []
    for (q, k, v, g, rev), cum in zip(insts, cums):
        end = cum[0:1] if rev else cum[c - 1:c]
        qt = (q * jnp.exp(cum)).astype(BF16)
        k_s = _stack_heads(k * jnp.exp(-cum), 64).astype(BF16)
        kh = (k * jnp.exp(end - cum)).astype(BF16)
        a_col = jnp.broadcast_to(jnp.exp(end), (LANE, LANE)).T
        pre.append((qt, k_s, kh, v.astype(BF16), _stack_heads(v, LANE).astype(BF16), a_col))
    atts = [_dot(pr[0], pr[1], _NT) for pr in pre]
    atts = [jnp.where((step_row <= step_col) if inst[4] else (step_row >= step_col), att, 0.0).astype(BF16)
            for att, inst in zip(atts, insts)]
    o0s = [_dot(att, pr[4]) for att, pr in zip(atts, pre)]
    kvs = [jnp.where(same_head, _dot(pr[2], pr[3], _TN), 0.0) for pr in pre]
    return [(pr[0], o0, pr[5], kv) for pr, o0, kv in zip(pre, o0s, kvs)]


def _rwkv_kernel(seq_len, nseq, is_grid, zero_init, r_ref, k_ref, v_ref, lora_ref, cwr_ref, cwk_ref, cwv_ref,
                 w2_ref, a2_ref, g2_ref, vec_ref, *rest):
    s0_ref = None if zero_init else rest[0]
    (y_ref, sout_ref, pad_ref, rs_ref, ks_ref, vs_ref, kk_ref, bonus_ref, gate_ref, yf_ref, yb_ref, st_ref, tt_ref,
     tg_ref, tw_ref, tq_ref, ty_ref) = rest[0 if zero_init else 1:][:17]
    left_ref, right_ref = rest[-2:] if is_grid else (None, None)
    n_chunks = seq_len // CHUNK
    npp = RW_PAIRS_PER_STEP
    w = npp * LANE
    pair = lambda x, p: x[:, p * LANE:(p + 1) * LANE]
    vec = vec_ref[...]
    w0 = (vec[0:1], vec[1:2])
    a0 = (vec[2:3], vec[3:4])
    k_k, k_a, r_k, ln_w, ln_b = vec[4:5], vec[5:6], vec[6:7], vec[7:8], vec[8:9]
    block_sum = _head_sums
    chains = [(d, p) for d in range(2) for p in range(npp)]
    rows_of = lambda s, c: slice((s * n_chunks + c) * CHUNK, (s * n_chunks + c + 1) * CHUNK)

    zeros = jnp.zeros((CONV_PAD, 3 * w), F32)
    for ref in (pad_ref, left_ref, right_ref) if is_grid else (pad_ref,):
        ref[0:CONV_PAD, :] = zeros
        ref[CONV_PAD + seq_len:2 * CONV_PAD + seq_len, :] = zeros
    cw = jnp.concatenate([cwr_ref[...], cwk_ref[...], cwv_ref[...]], axis=1)
    col = _iota((CHUNK, 3 * w), 0)

    def shift_body(c, carry):
        base = pl.multiple_of(CONV_PAD + c * CHUNK, CHUNK)
        win = pad_ref[pl.ds(base - 8, CHUNK + 16), :]
        left_ref[pl.ds(base, CHUNK), :] = jnp.where(col >= 1, win[7:7 + CHUNK], 0.0)
        right_ref[pl.ds(base, CHUNK), :] = jnp.where(col <= GRID_W - 2, win[9:9 + CHUNK], 0.0)
        return carry

    def conv_chunk(c):
        base = pl.multiple_of(CONV_PAD + c * CHUNK, CHUNK)
        acc = jnp.zeros((CHUNK, 3 * w), F32)
        if is_grid:
            for di in (-1, 0, 1):
                row = pl.ds(base + di * GRID_W, CHUNK)
                for dj, src in ((-1, left_ref), (0, pad_ref), (1, right_ref)):
                    tap = (di + 1) * 3 + dj + 1
                    acc = acc + src[row, :] * cw[tap:tap + 1]
        else:
            win = pad_ref[pl.ds(base - 8, CHUNK + 16), :]
            for dj in (-1, 0, 1):
                acc = acc + win[8 + dj:8 + dj + CHUNK] * cw[4 + dj:5 + dj]
        return acc[:, 0:w], acc[:, w:2 * w], acc[:, 2 * w:3 * w]

    for s in range(nseq):
        seq_rows = slice(s * seq_len, (s + 1) * seq_len)
        pad_ref[CONV_PAD:CONV_PAD + seq_len, 0:w] = r_ref[seq_rows, :]
        pad_ref[CONV_PAD:CONV_PAD + seq_len, w:2 * w] = k_ref[seq_rows, :]
        pad_ref[CONV_PAD:CONV_PAD + seq_len, 2 * w:3 * w] = v_ref[seq_rows, :]
        if is_grid:
            lax.fori_loop(0, n_chunks, shift_body, 0)

        def conv_body(j, carry, s=s):
            for u in range(TERM_UNROLL):
                c = j * TERM_UNROLL + u
                rows = pl.ds(pl.multiple_of((s * n_chunks + c) * CHUNK, CHUNK), CHUNK)
                rc, kc, vc = conv_chunk(c)
                kk = kc * k_k
                rs_ref[rows, :] = rc
                ks_ref[rows, :] = kc
                vs_ref[rows, :] = vc
                kk_ref[rows, :] = kk * lax.rsqrt(block_sum(kk * kk) + EPS)
                bonus_ref[rows, :] = block_sum(rc * kc * r_k) * vc
                gate_ref[rows, :] = _mm3_split(_split2(_sigmoid(lora_ref[rows, 2 * LANE:3 * LANE])),
                                               (g2_ref[0], g2_ref[1]))
            return carry

        lax.fori_loop(0, n_chunks // TERM_UNROLL, conv_body, 0)

    for s in range(nseq):
        for d, p in chains:
            if zero_init:
                st_ref[s, d, p] = jnp.zeros((LANE, LANE), F32)
            else:
                z = jnp.zeros((64, 64), F32)
                st_ref[s, d, p] = jnp.concatenate([jnp.concatenate([s0_ref[s, d, 2 * p], z], axis=1),
                                                   jnp.concatenate([z, s0_ref[s, d, 2 * p + 1]], axis=1)], axis=0)

    if nseq == 1:
        groups = [(0, list(range(g * TERM_UNROLL, (g + 1) * TERM_UNROLL)),
                   list(range(n_chunks - 1 - g * TERM_UNROLL, n_chunks - 1 - (g + 1) * TERM_UNROLL, -1)))
                  for g in range(n_chunks // TERM_UNROLL)]
    else:
        groups = [(s, list(range(n_chunks)), list(range(n_chunks - 1, -1, -1))) for s in range(nseq)]

    def group_terms(group, interleaved):
        s, fwd, bwd = group
        insts, where, lora_in = [], [], {}
        for d, chunks in ((0, fwd), (1, bwd)):
            for c in chunks:
                rows = rows_of(s, c)
                if c not in lora_in:
                    lora_in[c] = (_split2(jnp.tanh(lora_ref[rows, 0:LANE])), _split2(lora_ref[rows, LANE:2 * LANE]))
                lw = -RW_DECAY_SCALE * _sigmoid(w0[d] + _mm3_split(lora_in[c][0], (w2_ref[0, d], w2_ref[1, d])))
                ag = _sigmoid(a0[d] + _mm3_split(lora_in[c][1], (a2_ref[0, d], a2_ref[1, d])))
                rc, kc, vc, kk = rs_ref[rows, :], ks_ref[rows, :], vs_ref[rows, :], kk_ref[rows, :]
                kd = kc * (1.0 + (ag - 1.0) * k_a)
                kb = kk * ag
                for p in range(npp):
                    insts.append((pair(rc, p), pair(lw, p), pair(kd, p), -pair(kk, p), pair(kb, p), pair(vc, p),
                                  d == 1))
                    where.append((d, p, s * n_chunks + c))
        for (d, p, gc), (t, g, w_end, q, y0) in zip(where, _rwkv_chunk_terms(insts, interleaved)):
            tt_ref[d, p, gc] = t
            tg_ref[d, p, gc] = g
            tw_ref[d, p, gc] = jnp.broadcast_to(w_end, (8, LANE))
            tq_ref[d, p, gc] = q
            ty_ref[d, p, gc] = y0

    def scan_step(s, chunk_of_dir):
        gcs = [s * n_chunks + chunk_of_dir[d] for d, _ in chains]
        ss = [st_ref[s, d, p] for d, p in chains]
        sb = [x.astype(BF16) for x in ss]
        ys = [_dot(tq_ref[d, p, gc], b, _NT) + ty_ref[d, p, gc] for (d, p), gc, b in zip(chains, gcs, sb)]
        sn = [x * tw_ref[d, p, gc][0:1] + _dot(b, tt_ref[d, p, gc]) + tg_ref[d, p, gc]
              for (d, p), gc, x, b in zip(chains, gcs, ss, sb)]
        for (d, p), gc, y, x in zip(chains, gcs, ys, sn):
            st_ref[s, d, p] = x
            out_ref = yf_ref if d == 0 else yb_ref
            out_ref[gc * CHUNK:(gc + 1) * CHUNK, p * LANE:(p + 1) * LANE] = y

    def group_scan(group):
        s, fwd, bwd = group
        return [functools.partial(scan_step, s, (cf, cb)) for cf, cb in zip(fwd, bwd)]

    for g, group in enumerate(groups):
        group_terms(group, group_scan(groups[g - 1]) if g else [])
    for step in group_scan(groups[-1]):
        step()
    for s in range(nseq):
        for d, p in chains:
            x = st_ref[s, d, p]
            sout_ref[s, d, 2 * p] = x[0:64, 0:64]
            sout_ref[s, d, 2 * p + 1] = x[64:LANE, 64:LANE]

    def post_body(j, carry):
        offs = [pl.multiple_of((j * TERM_UNROLL + u) * CHUNK, CHUNK) for u in range(TERM_UNROLL)]
        ys = [yf_ref[pl.ds(off, CHUNK), :] + yb_ref[pl.ds(off, CHUNK), :] for off in offs]
        mus = [block_sum(y) * (1.0 / 64) for y in ys]
        dlts = [y - mu for y, mu in zip(ys, mus)]
        vrs = [block_sum(dlt * dlt) * (1.0 / 64) for dlt in dlts]
        for off, dlt, var in zip(offs, dlts, vrs):
            yn = dlt * lax.rsqrt(var + RW_LN_EPS) * ln_w + ln_b
            y_ref[pl.ds(off, CHUNK), :] = (yn + bonus_ref[pl.ds(off, CHUNK), :]) * gate_ref[pl.ds(off, CHUNK), :]
        return carry

    lax.fori_loop(0, nseq * n_chunks // TERM_UNROLL, post_body, 0)


def _rwkv_mixer(proj, first_seq, n_seq, seq_len, is_grid, prm, s0):
    n_pairs = RW_W // LANE
    n_chunks = seq_len // CHUNK
    npp = RW_PAIRS_PER_STEP
    w = npp * LANE
    nseq = max(1, RW_GROUPS * TERM_UNROLL // n_chunks)
    assert n_chunks % TERM_UNROLL == 0 and n_pairs % npp == 0
    assert (n_chunks == TERM_UNROLL or nseq == 1) and n_seq % nseq == 0 and first_seq % nseq == 0
    rows = nseq * seq_len
    total_chunks = nseq * n_chunks
    first = first_seq // nseq
    col = lambda cb: (lambda b, p: (b + first, cb // npp + p))
    par = lambda cb: (lambda b, p: (0, cb // npp + p))
    state_spec = pl.BlockSpec((nseq, 2, 2 * npp, 64, 64), lambda b, p: (b, 0, p, 0, 0))
    kernel = functools.partial(_rwkv_kernel, seq_len, nseq, is_grid, s0 is None)
    y, s_out = pl.pallas_call(
        kernel,
        grid=(n_seq // nseq, n_pairs // npp),
        in_specs=[pl.BlockSpec((rows, w), col(CB_R)),
                  pl.BlockSpec((rows, w), col(CB_K)),
                  pl.BlockSpec((rows, w), col(CB_V)),
                  pl.BlockSpec((rows, 3 * LANE), lambda b, p: (b + first, CB_LORA // 3)),
                  pl.BlockSpec((9, w), par(CB_R)),
                  pl.BlockSpec((9, w), par(CB_K)),
                  pl.BlockSpec((9, w), par(CB_V)),
                  pl.BlockSpec((2, 2, LANE, w), lambda b, p: (0, 0, 0, p)),
                  pl.BlockSpec((2, 2, LANE, w), lambda b, p: (0, 0, 0, p)),
                  pl.BlockSpec((2, LANE, w), lambda b, p: (0, 0, p)),
                  pl.BlockSpec((16, w), lambda b, p: (0, p))] + ([] if s0 is None else [state_spec]),
        out_specs=[pl.BlockSpec((rows, w), lambda b, p: (b, p)), state_spec],
        out_shape=[jax.ShapeDtypeStruct((n_seq * seq_len, RW_W), F32),
                   jax.ShapeDtypeStruct((n_seq, 2, 2 * n_pairs, 64, 64), F32)],
        scratch_shapes=[pltpu.VMEM((seq_len + 2 * CONV_PAD, 3 * w), F32)]
                       + [pltpu.VMEM((rows, w), F32)] * 8
                       + [pltpu.VMEM((nseq, 2, npp, LANE, LANE), F32),
                          pltpu.VMEM((2, npp, total_chunks, LANE, LANE), BF16),
                          pltpu.VMEM((2, npp, total_chunks, LANE, LANE), F32),
                          pltpu.VMEM((2, npp, total_chunks, 8, LANE), F32),
                          pltpu.VMEM((2, npp, total_chunks, CHUNK, LANE), BF16),
                          pltpu.VMEM((2, npp, total_chunks, CHUNK, LANE), F32)]
                       + ([pltpu.VMEM((seq_len + 2 * CONV_PAD, 3 * w), F32)] * 2 if is_grid else []),
        compiler_params=_cparams(2),
        name="rwkv_mixer",
    )(proj, proj, proj, proj, prm['conv'], prm['conv'], prm['conv'], prm['w2p'], prm['a2p'], prm['g2'],
      prm['vec'], *([] if s0 is None else [s0]))
    return y, s_out


def _gla_kernel(seq_len, zero_init, q_ref, k_ref, v_ref, og_ref, lgk_ref, gk2_ref, gvec_ref, *rest):
    s0_ref = None if zero_init else rest[0]
    y_ref, sout_ref, of_ref, ob_ref, st_ref, tq_ref, to_ref, ta_ref, tkv_ref = rest[0 if zero_init else 1:]
    n_chunks = seq_len // CHUNK
    npp = GLA_PAIRS
    gvec = gvec_ref[...]
    chains = [(d, p) for d in range(2) for p in range(npp)]

    def terms_body(j, carry):
        insts, where = [], []
        for u in range(TERM_UNROLL):
            c = j * TERM_UNROLL + u
            off = pl.multiple_of(c * CHUNK, CHUNK)
            lgk = _split2(lgk_ref[pl.ds(off, CHUNK), :])
            qc = q_ref[pl.ds(off, CHUNK), :] * GLA_Q_SCALE
            kc = k_ref[pl.ds(off, CHUNK), :]
            vc = v_ref[pl.ds(off, CHUNK), :]
            for d in range(2):
                x = _mm3_split(lgk, (gk2_ref[0, d], gk2_ref[1, d])) + gvec[d:d + 1, 0:GLA_QK_W]
                g = _log_sigmoid(x) * (1.0 / GLA_GATE_NORM)
                for p in range(npp):
                    qk = slice(p * LANE, (p + 1) * LANE)
                    insts.append((qc[:, qk], kc[:, qk], vc[:, 2 * p * LANE:2 * (p + 1) * LANE], g[:, qk], d == 1))
                    where.append((d, p, c))
        for (d, p, c), (qt, o0, a_col, kv) in zip(where, _gla_chunk_terms(insts)):
            tq_ref[d, p, c] = qt
            to_ref[d, p, c] = o0
            ta_ref[d, p, c] = a_col
            tkv_ref[d, p, c] = kv
        return carry

    lax.fori_loop(0, n_chunks // TERM_UNROLL, terms_body, 0)

    for d, p in chains:
        if zero_init:
            st_ref[d, p] = jnp.zeros((LANE, 2 * LANE), F32)
        else:
            z = jnp.zeros((64, LANE), F32)
            st_ref[d, p] = jnp.concatenate([jnp.concatenate([s0_ref[0, d, 2 * p], z], axis=1),
                                            jnp.concatenate([z, s0_ref[0, d, 2 * p + 1]], axis=1)], axis=0)

    def scan_body(i, carry):
        cs = (i, n_chunks - 1 - i)
        ss = [st_ref[d, p] for d, p in chains]
        os_ = [_dot(tq_ref[d, p, cs[d]], s.astype(BF16)) + to_ref[d, p, cs[d]] for (d, p), s in zip(chains, ss)]
        for (d, p), s, o in zip(chains, ss, os_):
            a_col = ta_ref[d, p, cs[d]]
            st_ref[d, p] = s * jnp.concatenate([a_col, a_col], axis=1) + tkv_ref[d, p, cs[d]]
            out_ref = of_ref if d == 0 else ob_ref
            out_ref[pl.ds(pl.multiple_of(cs[d] * CHUNK, CHUNK), CHUNK), 2 * p * LANE:2 * (p + 1) * LANE] = o
        return carry

    lax.fori_loop(0, n_chunks, scan_body, 0)
    for d, p in chains:
        s = st_ref[d, p]
        sout_ref[0, d, 2 * p] = s[0:64, 0:LANE]
        sout_ref[0, d, 2 * p + 1] = s[64:LANE, LANE:2 * LANE]

    def post_body(c, carry):
        off = pl.multiple_of(c * CHUNK, CHUNK)
        for h in range(2 * npp):
            hs = slice(h * LANE, (h + 1) * LANE)
            o = of_ref[pl.ds(off, CHUNK), hs] + ob_ref[pl.ds(off, CHUNK), hs]
            gate = _silu(og_ref[pl.ds(off, CHUNK), hs])
            y_ref[pl.ds(off, CHUNK), hs] = _rmsnorm_rows(o) * gvec[2:3, hs] * gate
        return carry

    lax.fori_loop(0, n_chunks, post_body, 0)


def _gla_mixer(proj, first_seq, n_seq, seq_len, prm, s0):
    npp = GLA_PAIRS
    n_heads = 2 * npp
    n_chunks = seq_len // CHUNK
    assert n_chunks % TERM_UNROLL == 0
    state_spec = pl.BlockSpec((1, 2, n_heads, 64, LANE), lambda b: (b, 0, 0, 0, 0))
    kernel = functools.partial(_gla_kernel, seq_len, s0 is None)
    y, s_out = pl.pallas_call(
        kernel,
        grid=(n_seq,),
        in_specs=[pl.BlockSpec((seq_len, GLA_QK_W), lambda b: (b + first_seq, CB_GQ * LANE // GLA_QK_W)),
                  pl.BlockSpec((seq_len, GLA_QK_W), lambda b: (b + first_seq, CB_GK * LANE // GLA_QK_W)),
                  pl.BlockSpec((seq_len, GLA_V_W), lambda b: (b + first_seq, CB_GV * LANE // GLA_V_W)),
                  pl.BlockSpec((seq_len, GLA_V_W), lambda b: (b + first_seq, CB_OG * LANE // GLA_V_W)),
                  pl.BlockSpec((seq_len, LANE), lambda b: (b + first_seq, CB_LGK)),
                  pl.BlockSpec((2, 2, LANE, GLA_QK_W), lambda b: (0, 0, 0, 0)),
                  pl.BlockSpec((8, GLA_V_W), lambda b: (0, 0))] + ([] if s0 is None else [state_spec]),
        out_specs=[pl.BlockSpec((seq_len, GLA_V_W), lambda b: (b, 0)), state_spec],
        out_shape=[jax.ShapeDtypeStruct((n_seq * seq_len, GLA_V_W), F32),
                   jax.ShapeDtypeStruct((n_seq, 2, n_heads, 64, LANE), F32)],
        scratch_shapes=[pltpu.VMEM((seq_len, GLA_V_W), F32)] * 2
                       + [pltpu.VMEM((2, npp, LANE, 2 * LANE), F32),
                          pltpu.VMEM((2, npp, n_chunks, CHUNK, LANE), BF16),
                          pltpu.VMEM((2, npp, n_chunks, CHUNK, 2 * LANE), F32),
                          pltpu.VMEM((2, npp, n_chunks, LANE, LANE), F32),
                          pltpu.VMEM((2, npp, n_chunks, LANE, 2 * LANE), F32)],
        compiler_params=_cparams(1),
        name="gla_mixer",
    )(proj, proj, proj, proj, proj, prm['gk2p'], prm['gvec'], *([] if s0 is None else [s0]))
    return y, s_out


OUT_TM = 512
ROUTE_NEG = -1e30
LANE_GROUP0 = N_EXPERTS


def _route(logits):
    lane = _iota(logits.shape, 1)
    lane_f = lane.astype(F32)
    big = float(LANE)
    is_g = (lane >= LANE_GROUP0) & (lane < LANE_GROUP0 + 4)
    gmax = jnp.max(jnp.where(is_g, logits, ROUTE_NEG), axis=-1, keepdims=True)
    gidx = jnp.min(jnp.where(is_g & (logits == gmax), lane_f, big), axis=-1, keepdims=True) - LANE_GROUP0
    gsum = jnp.sum(jnp.where(is_g, jnp.exp(jnp.minimum(logits - gmax, 0.0)), 0.0), axis=-1, keepdims=True)
    g_w = 1.0 / gsum
    in_grp = (lane < N_EXPERTS) & ((lane // 4).astype(F32) == gidx)
    m1 = jnp.max(jnp.where(in_grp, logits, ROUTE_NEG), axis=-1, keepdims=True)
    i1 = jnp.min(jnp.where(in_grp & (logits == m1), lane_f, big), axis=-1, keepdims=True)
    rest = in_grp & (lane_f != i1)
    m2 = jnp.max(jnp.where(rest, logits, ROUTE_NEG), axis=-1, keepdims=True)
    i2 = jnp.min(jnp.where(rest & (logits == m2), lane_f, big), axis=-1, keepdims=True)
    t = jnp.exp(m2 - m1)
    w1 = g_w / (1.0 + t)
    return jnp.where(lane_f == i1, w1, 0.0) + jnp.where(lane_f == i2, w1 * t, 0.0)


def _outproj_kernel(tiles, yrc_ref, yrd_ref, ygc_ref, ygd_ref, xc_ref, xd_ref, mod_ref, wo_ref, g_ref, wr_ref,
                    br_ref, x1_ref, h2_ref, cmb_ref):
    def run(yr_ref, yg_ref, x_ref):
        m = mod_ref[0]
        mix = _mm(yr_ref[...], wo_ref[0:RW_W, :]) + _mm(yg_ref[...], wo_ref[RW_W:RW_W + GLA_V_W, :])
        x1 = x_ref[...] + m[2:3] * mix
        h2 = _rmsnorm_rows(x1) * g_ref[...] * (1.0 + m[4:5]) + m[3:4]
        x1_ref[...] = x1
        h2_ref[...] = h2.astype(BF16)
        cmb_ref[...] = _route(_mm3_split(_split2(h2), (wr_ref[0], wr_ref[1])) + br_ref[...])

    tiles.by_pass(functools.partial(run, yrc_ref, ygc_ref, xc_ref), functools.partial(run, yrd_ref, ygd_ref, xd_ref))


def _out_projection(tiles, y_rw, y_gla, x, mod, w_out, norm_g, w_route, b_route):
    n = x[0].shape[0] + x[1].shape[0]
    full = lambda a: pl.BlockSpec(a.shape, lambda i: (0,) * a.ndim)
    return pl.pallas_call(
        functools.partial(_outproj_kernel, tiles),
        grid=(tiles.n_ctx + tiles.n_dec,),
        in_specs=[*tiles.specs(RW_W), *tiles.specs(GLA_V_W), *tiles.specs(D_MODEL), tiles.mod_spec(),
                  full(w_out), full(norm_g), full(w_route), full(b_route)],
        out_specs=[tiles.merged(D_MODEL), tiles.merged(D_MODEL), tiles.merged(LANE)],
        out_shape=[jax.ShapeDtypeStruct((n, D_MODEL), F32), jax.ShapeDtypeStruct((n, D_MODEL), BF16),
                   jax.ShapeDtypeStruct((n, LANE), F32)],
        compiler_params=_cparams(1),
        name="out_proj_router",
    )(*y_rw, *y_gla, *x, mod, w_out, norm_g, w_route, b_route)


MOE_TM = 512
MOE_RB = 128
MOE_INTERLEAVE = 4
SLOT_ALIGN = 16
MOE_SLOTS = 2 * MOE_TM + N_EXPERTS * SLOT_ALIGN + MOE_RB


def _stage_expert_weights(srcs_hbm, dst_refs, stage_refs, sems):
    def copies(e):
        return [pltpu.make_async_copy(src.at[e], stage.at[e % 2], sem.at[e % 2])
                for src, stage, sem in zip(srcs_hbm, stage_refs, sems)]

    for c in copies(0):
        c.start()
    for e in range(N_EXPERTS):
        if e + 1 < N_EXPERTS:
            for c in copies(e + 1):
                c.start()
        for c, dst, stage in zip(copies(e), dst_refs, stage_refs):
            c.wait()
            dst[e] = stage[e % 2].astype(BF16)


def _moe_kernel(tiles, h2_ref, cmb_ref, x1_ref, mod_ref, w1_hbm, w3_hbm, w2_hbm, fg_ref, yc_ref, yd_ref,
                xs_ref, ys_ref, w1_ref, w3_ref, w2_ref, stage1_ref, stage3_ref, stage2_ref, sem1, sem3, sem2):
    @pl.when(pl.program_id(0) == 0)
    def _():
        _stage_expert_weights((w1_hbm, w3_hbm, w2_hbm), (w1_ref, w3_ref, w2_ref),
                              (stage1_ref, stage3_ref, stage2_ref), (sem1, sem3, sem2))

    cmb = cmb_ref[...]
    lane = _iota(cmb.shape, 1).astype(F32)
    sel = cmb > 0.0
    sel01 = jnp.where(sel, 1.0, 0.0).astype(BF16)
    before = (_iota((MOE_TM, MOE_TM), 0) > _iota((MOE_TM, MOE_TM), 1)).astype(BF16)
    pos = _dot(before, sel01)
    cnt = pos[MOE_TM - 1:MOE_TM] + sel01[MOE_TM - 1:MOE_TM].astype(F32)
    seg = jnp.floor((cnt + (SLOT_ALIGN - 1)) * (1.0 / SLOT_ALIGN))
    lower_experts = (_iota((LANE, LANE), 0) < _iota((LANE, LANE), 1)).astype(BF16)
    start = _dot(jnp.broadcast_to(seg, (8, LANE)).astype(BF16), lower_experts)[0:1] * SLOT_ALIGN
    n_blk = jnp.floor((cnt + (MOE_RB - 1)) * (1.0 / MOE_RB)).astype(jnp.int32)
    start_i = start.astype(jnp.int32)
    cnt_i = cnt.astype(jnp.int32)
    slot = start + pos
    e_a = jnp.min(jnp.where(sel, lane, float(LANE)), axis=-1, keepdims=True)
    e_b = jnp.max(jnp.where(sel, lane, -1.0), axis=-1, keepdims=True)
    pick = lambda e, x: jnp.sum(jnp.where(lane == e, x, 0.0), axis=-1, keepdims=True)
    slot_a, w_a = pick(e_a, slot), pick(e_a, cmb)
    slot_b = jnp.where(e_b != e_a, pick(e_b, slot), -1.0)
    w_b = pick(e_b, cmb)

    slots_t = jnp.where(lane == 0.0, slot_a, jnp.where(lane == 1.0, slot_b, -1.0)).T
    row_slot = _iota((MOE_SLOTS, MOE_TM), 0).astype(F32)
    gather = jnp.where((row_slot == slots_t[0:1]) | (row_slot == slots_t[1:2]), 1.0, 0.0).astype(BF16)
    xs_ref[...] = _dot(gather, h2_ref[...]).astype(BF16)
    ys_ref[...] = jnp.zeros_like(ys_ref)

    row_in_blk = _iota((MOE_RB, D_MODEL), 0)

    def expert_blocks(experts, r0s, ends):
        xbs = [xs_ref[pl.ds(r0, MOE_RB), :] for r0 in r0s]
        gates = [_dot(xb, w3_ref[e]) for xb, e in zip(xbs, experts)]
        ups = [_dot(xb, w1_ref[e]) for xb, e in zip(xbs, experts)]
        acts = [(_silu(g) * u).astype(BF16) for g, u in zip(gates, ups)]
        outs = [_dot(a, w2_ref[e]) for a, e in zip(acts, experts)]
        for r0, end, out in zip(r0s, ends, outs):
            keep = row_in_blk + r0 >= end
            ys_ref[pl.ds(r0, MOE_RB), :] = jnp.where(keep, ys_ref[pl.ds(r0, MOE_RB), :], out.astype(BF16))

    seg_start = [pl.multiple_of(start_i[0, e], SLOT_ALIGN) for e in range(N_EXPERTS)]
    seg_end = [seg_start[e] + cnt_i[0, e] for e in range(N_EXPERTS)]
    for e0 in range(0, N_EXPERTS, MOE_INTERLEAVE):
        es = list(range(e0, e0 + MOE_INTERLEAVE))
        expert_blocks(es, [seg_start[e] for e in es], [seg_end[e] for e in es])
    for e in range(N_EXPERTS):
        def extra_block(b, carry, e=e):
            expert_blocks([e], [pl.multiple_of(seg_start[e] + b * MOE_RB, SLOT_ALIGN)], [seg_end[e]])
            return carry

        lax.fori_loop(1, n_blk[0, e], extra_block, 0)

    col_slot = _iota((MOE_TM, MOE_SLOTS), 1).astype(F32)
    scatter = (jnp.where(col_slot == slot_a, w_a, 0.0) + jnp.where(col_slot == slot_b, w_b, 0.0)).astype(BF16)
    x2 = x1_ref[...] + mod_ref[0][5:6] * _dot(scatter, ys_ref[...])
    y = _rmsnorm_rows(x2) * fg_ref[...]

    def write(y_ref):
        y_ref[...] = y

    tiles.by_pass(functools.partial(write, yc_ref), functools.partial(write, yd_ref))


def _moe(tiles, h2, cmb, x1, mod, w1, w3, w2, final_g):
    assert tiles.tm == MOE_TM
    hbm = pl.BlockSpec(memory_space=pl.ANY)
    out_ctx, out_dec = tiles.specs(D_MODEL)
    return pl.pallas_call(
        functools.partial(_moe_kernel, tiles),
        grid=(tiles.n_ctx + tiles.n_dec,),
        in_specs=[tiles.merged(D_MODEL), tiles.merged(LANE), tiles.merged(D_MODEL), tiles.mod_spec(),
                  hbm, hbm, hbm, pl.BlockSpec((1, D_MODEL), lambda i: (0, 0))],
        out_specs=[out_ctx, out_dec],
        out_shape=[jax.ShapeDtypeStruct((tiles.n_ctx * MOE_TM, D_MODEL), F32),
                   jax.ShapeDtypeStruct((tiles.n_dec * MOE_TM, D_MODEL), F32)],
        scratch_shapes=[pltpu.VMEM((MOE_SLOTS, D_MODEL), BF16), pltpu.VMEM((MOE_SLOTS, D_MODEL), BF16),
                        pltpu.VMEM((N_EXPERTS, D_MODEL, D_EXPERT), BF16),
                        pltpu.VMEM((N_EXPERTS, D_MODEL, D_EXPERT), BF16),
                        pltpu.VMEM((N_EXPERTS, D_EXPERT, D_MODEL), BF16),
                        pltpu.VMEM((2, D_MODEL, D_EXPERT), F32), pltpu.VMEM((2, D_MODEL, D_EXPERT), F32),
                        pltpu.VMEM((2, D_EXPERT, D_MODEL), F32)] + [pltpu.SemaphoreType.DMA((2,))] * 3,
        compiler_params=_cparams(1),
        name="moe_experts",
    )(h2, cmb, x1, mod, w1, w3, w2, final_g)


def _pad_rows(x, rows):
    return jnp.pad(x, ((0, rows - x.shape[0]),) + ((0, 0),) * (x.ndim - 1))


def _pack_params(l, w_in, rw_conv, rw_w0, rw_w2, rw_a0, rw_a2, rw_g2, rw_k_k, rw_k_a, rw_r_k, rw_ln_w, rw_ln_b,
                 gla_gk2, gla_gk_b, gla_norm_g, moe_w_group, moe_b_group, moe_w_expert, moe_b_expert):
    wi = w_in[l]
    z = lambda n: jnp.zeros((D_MODEL, n), F32)
    w_in_t = jnp.swapaxes(wi, 0, 1)
    z64 = jnp.zeros((64, RW_W), F32)
    w2p = jnp.stack([jnp.concatenate([rw_w2[l, 0], z64], 0), jnp.concatenate([z64, rw_w2[l, 1]], 0)])
    a2p = jnp.stack([jnp.concatenate([rw_a2[l, 0], z64], 0), jnp.concatenate([z64, rw_a2[l, 1]], 0)])
    vec = _pad_rows(jnp.stack([rw_w0[l, 0], rw_w0[l, 1], rw_a0[l, 0], rw_a0[l, 1], rw_k_k[l], rw_k_a[l],
                               rw_r_k[l].reshape(RW_W), rw_ln_w[l], rw_ln_b[l]]), 16)
    rw = {'conv': rw_conv[l].reshape(9, 3 * RW_W), 'w2p': _split_param(w2p), 'a2p': _split_param(a2p),
          'g2': _split_param(rw_g2[l]), 'vec': vec}
    gk2p = jnp.stack([_pad_rows(gla_gk2[l, 0], LANE),
                      _pad_rows(jnp.concatenate([jnp.zeros((16, GLA_QK_W), F32), gla_gk2[l, 1]], 0), LANE)])
    gk_b = jnp.pad(gla_gk_b[l], ((0, 0), (0, GLA_V_W - GLA_QK_W)))
    gvec = _pad_rows(jnp.concatenate([gk_b, jnp.tile(gla_norm_g[l], GLA_V_W // LANE)[None]], axis=0), 8)
    gla = {'gk2p': _split_param(gk2p), 'gvec': gvec}
    w_route = _split_param(jnp.concatenate([moe_w_expert[l], moe_w_group[l], z(LANE - N_EXPERTS - 4)], axis=1))
    b_route = jnp.concatenate([moe_b_expert[l], moe_b_group[l], jnp.zeros((LANE - N_EXPERTS - 4,), F32)])[None]
    return w_in_t, rw, gla, w_route, b_route


def kernel(x_prompt, x_sample, state_rwkv, state_gla, c, c_ctx, norm1_g, norm2_g, w_ada, b_ada, w_in, w_out,
           rw_conv, rw_w0, rw_w2, rw_a0, rw_a2, rw_g2, rw_k_k, rw_k_a, rw_r_k, rw_ln_w, rw_ln_b,
           gla_gk2, gla_gk_b, gla_norm_g, moe_w_group, moe_b_group, moe_w_expert, moe_b_expert,
           moe_w1, moe_w3, moe_w2, final_g):
    depth = w_in.shape[0]
    assert depth == 1, "the packed layout below handles the single-layer trunk of this problem"
    l = 0
    n_dec = x_sample.shape[0]
    ctx_row = n_dec
    cond8 = _pad_rows(jnp.concatenate([c, c_ctx[None]], axis=0), 8)
    mod = _modulation(cond8, w_ada[l], b_ada[l][None]).reshape(8, N_MOD, D_MODEL)
    pk = _pack_params(l, w_in, rw_conv, rw_w0, rw_w2, rw_a0, rw_a2, rw_g2, rw_k_k, rw_k_a, rw_r_k, rw_ln_w,
                      rw_ln_b, gla_gk2, gla_gk_b, gla_norm_g, moe_w_group, moe_b_group, moe_w_expert,
                      moe_b_expert)
    w_in_t, rw, gla, w_route, b_route = pk

    n_ctx, ctx_len, _ = x_prompt.shape
    dec_len = x_sample.shape[1]
    x_ctx = x_prompt.reshape(n_ctx * ctx_len, D_MODEL)
    x_dec = x_sample.reshape(n_dec * dec_len, D_MODEL)
    assert (n_ctx * ctx_len) % dec_len == 0, "denoising sequences must start on a dec_len row block of proj"
    first_dec = n_ctx * ctx_len // dec_len
    tiles = _Tiles(n_ctx * ctx_len, n_dec * dec_len, dec_len, ctx_row, PROJ_TM)
    assert PROJ_TM == OUT_TM == MOE_TM

    proj = _in_projection(tiles, x_ctx, x_dec, mod, norm1_g[l][None], w_in_t)
    y_rw_c, s_rw = _rwkv_mixer(proj, 0, n_ctx, ctx_len, False, rw, None)
    y_gla_c, s_gla = _gla_mixer(proj, 0, n_ctx, ctx_len, gla, None)
    y_rw_d, _ = _rwkv_mixer(proj, first_dec, n_dec, dec_len, True, rw, state_rwkv[:, l])
    y_gla_d, _ = _gla_mixer(proj, first_dec, n_dec, dec_len, gla, state_gla[:, l])
    x1, h2, cmb = _out_projection(tiles, (y_rw_c, y_rw_d), (y_gla_c, y_gla_d), (x_ctx, x_dec), mod,
                                  w_out[l].astype(BF16), norm2_g[l][None], w_route, b_route)
    y_ctx, y_dec = _moe(tiles, h2, cmb, x1, mod, moe_w1[l], moe_w3[l], moe_w2[l], final_g[None])
    return (y_ctx.reshape(x_prompt.shape), y_dec.reshape(x_sample.shape), s_rw[:, None], s_gla[:, None])
```

```python
import functools

import jax
import jax.numpy as jnp
from jax import lax
from jax.experimental import pallas as pl
from jax.experimental.pallas import tpu as pltpu

F32 = jnp.float32
BF16 = jnp.bfloat16

D_MODEL = 1024
RW_W = 512
GLA_V_W = 512
GLA_QK_W = 256
N_EXPERTS = 16
D_EXPERT = 256
N_MOD = 6
EPS = 1e-6
RW_LN_EPS = 64e-5
RW_DECAY_SCALE = 0.606531
GLA_GATE_NORM = 16.0
GLA_Q_SCALE = 64 ** -0.5
GRID_W = 64

LANE = 128
CHUNK = 64
CONV_PAD = 128
TERM_UNROLL = 4
RW_PAIRS_PER_STEP = 2
GLA_PAIRS = 2
RW_GROUPS = 4
D_PROJ = 28 * LANE
VMEM_LIMIT = 56 * 1024 * 1024

CB_R, CB_K, CB_V, CB_LORA, CB_LGK, CB_GQ, CB_GK, CB_GV, CB_OG = 0, 4, 8, 12, 15, 16, 18, 20, 24

_NN = (((1,), (0,)), ((), ()))
_NT = (((1,), (1,)), ((), ()))
_TN = (((0,), (0,)), ((), ()))


def _dot(a, b, dims=_NN):
    return lax.dot_general(a, b, dims, preferred_element_type=F32)


def _mm(a, b, dims=_NN):
    return _dot(a.astype(BF16), b.astype(BF16), dims)


def _split2(x):
    hi = x.astype(BF16)
    lo = (x - hi.astype(F32)).astype(BF16)
    return hi, lo


def _mm3(a, b, dims=_NN):
    ah, al = _split2(a)
    bh, bl = _split2(b)
    return _dot(ah, bh, dims) + _dot(ah, bl, dims) + _dot(al, bh, dims)


def _mm3_split(a_split, b_split, dims=_NN):
    (ah, al), (bh, bl) = a_split, b_split
    return _dot(ah, bh, dims) + _dot(ah, bl, dims) + _dot(al, bh, dims)


def _split_param(w):
    return jnp.stack(_split2(w))


def _mm_01_lhs(a01, b, dims=_NN):
    b1, b2 = _split2(b)
    return _dot(a01, b1, dims) + _dot(a01, b2, dims)


def _sigmoid(x):
    return 0.5 * jnp.tanh(0.5 * x) + 0.5


def _silu(x):
    return x * _sigmoid(x)


def _log_sigmoid(x):
    return jnp.minimum(x, 0.0) - jnp.log(1.0 + jnp.exp(-jnp.abs(x)))


def _iota(shape, dim):
    return lax.broadcasted_iota(jnp.int32, shape, dim)


def _cparams(n_axes):
    return pltpu.CompilerParams(dimension_semantics=("arbitrary",) * n_axes, vmem_limit_bytes=VMEM_LIMIT)


MOD_TN = 768


def _mod_kernel(c_ref, w_ref, b_ref, o_ref):
    o_ref[...] = _mm3(_silu(c_ref[...]), w_ref[...]) + b_ref[...]


def _modulation(cond8, w_ada, b_ada):
    n = w_ada.shape[1]
    return pl.pallas_call(
        _mod_kernel,
        grid=(n // MOD_TN,),
        in_specs=[pl.BlockSpec((8, D_MODEL), lambda j: (0, 0)),
                  pl.BlockSpec((D_MODEL, MOD_TN), lambda j: (0, j)),
                  pl.BlockSpec((1, MOD_TN), lambda j: (0, j))],
        out_specs=pl.BlockSpec((8, MOD_TN), lambda j: (0, j)),
        out_shape=jax.ShapeDtypeStruct((8, n), F32),
        compiler_params=_cparams(1),
        name="adaln_mod",
    )(cond8, w_ada, b_ada)


PROJ_TM = 512
D_IN = 3488
N_LGK = 32


def _rmsnorm_rows(x):
    return x * lax.rsqrt(jnp.mean(x * x, axis=-1, keepdims=True) + EPS)


class _Tiles:
    def __init__(self, n_ctx_tokens, n_dec_tokens, dec_seq_len, ctx_row, tm):
        self.tm = tm
        self.n_ctx = n_ctx_tokens // tm
        self.n_dec = n_dec_tokens // tm
        self.per_seq = dec_seq_len // tm
        self.ctx_row = ctx_row

    def specs(self, width):
        last_ctx = self.n_ctx - 1
        n_ctx = self.n_ctx
        return (pl.BlockSpec((self.tm, width), lambda i: (jnp.minimum(i, last_ctx), 0)),
                pl.BlockSpec((self.tm, width), lambda i: (jnp.maximum(i - n_ctx, 0), 0)))

    def merged(self, width):
        return pl.BlockSpec((self.tm, width), lambda i: (i, 0))

    def mod_spec(self):
        n_ctx, per_seq, ctx_row = self.n_ctx, self.per_seq, self.ctx_row
        return pl.BlockSpec((1, N_MOD, D_MODEL),
                            lambda i: (jnp.where(i < n_ctx, ctx_row, (i - n_ctx) // per_seq), 0, 0))

    def by_pass(self, run_ctx, run_dec):
        i = pl.program_id(0)
        pl.when(i < self.n_ctx)(run_ctx)
        pl.when(i >= self.n_ctx)(run_dec)


def _inproj_kernel(tiles, xc_ref, xd_ref, mod_ref, g_ref, wt_ref, o_ref, w_ref):
    @pl.when(pl.program_id(0) == 0)
    def _():
        for j in range(D_PROJ // LANE):
            if j == CB_LGK:
                blk = jnp.concatenate([wt_ref[D_IN - N_LGK:D_IN, :], jnp.zeros((LANE - N_LGK, D_MODEL), F32)], axis=0)
            else:
                src = j if j < CB_LGK else j - 1
                blk = wt_ref[src * LANE:(src + 1) * LANE, :]
            w_ref[:, j * LANE:(j + 1) * LANE] = blk.T.astype(BF16)

    def run(x_ref):
        m = mod_ref[0]
        h = _rmsnorm_rows(x_ref[...]) * g_ref[...] * (1.0 + m[1:2]) + m[0:1]
        o_ref[...] = _mm(h, w_ref[...])

    tiles.by_pass(functools.partial(run, xc_ref), functools.partial(run, xd_ref))


def _in_projection(tiles, x_ctx, x_dec, mod, norm_g, w_in_t):
    full = lambda a: pl.BlockSpec(a.shape, lambda i: (0,) * a.ndim)
    return pl.pallas_call(
        functools.partial(_inproj_kernel, tiles),
        grid=(tiles.n_ctx + tiles.n_dec,),
        in_specs=[*tiles.specs(D_MODEL), tiles.mod_spec(), full(norm_g),
                  pl.BlockSpec(w_in_t.shape, lambda i: (0, 0), pipeline_mode=pl.Buffered(1))],
        out_specs=tiles.merged(D_PROJ),
        out_shape=jax.ShapeDtypeStruct((x_ctx.shape[0] + x_dec.shape[0], D_PROJ), F32),
        scratch_shapes=[pltpu.VMEM((D_MODEL, D_PROJ), BF16)],
        compiler_params=_cparams(1),
        name="in_proj",
    )(x_ctx, x_dec, mod, norm_g, w_in_t)


def _time_masks(reverse):
    r = _iota((2 * CHUNK, 2 * CHUNK), 0) % CHUNK
    c = _iota((2 * CHUNK, 2 * CHUNK), 1) % CHUNK
    if reverse:
        return r < c, r <= c
    return r > c, r >= c


def _cumsum_matrix(reverse):
    r = _iota((CHUNK, CHUNK), 0)
    c = _iota((CHUNK, CHUNK), 1)
    tri = (r <= c) if reverse else (r >= c)
    return tri.astype(BF16)


def _stack_heads(x, half):
    m0 = _iota(x.shape, 1) < half
    return jnp.concatenate([jnp.where(m0, x, 0.0), jnp.where(m0, 0.0, x)], axis=0)


def _head_sums(x):
    parts = []
    for p in range(x.shape[1] // LANE):
        xp = x[:, p * LANE:(p + 1) * LANE]
        m0 = _iota(xp.shape, 1) < 64
        s0 = jnp.sum(jnp.where(m0, xp, 0.0), axis=-1, keepdims=True)
        s1 = jnp.sum(jnp.where(m0, 0.0, xp), axis=-1, keepdims=True)
        parts.append(jnp.where(m0, s0, s1))
    return parts[0] if len(parts) == 1 else jnp.concatenate(parts, axis=1)


def _rwkv_chunk_terms(insts, interleaved=()):
    c = CHUNK
    step_row = _iota((c, LANE), 0)
    step_col = _iota((c, LANE), 1) % c
    eye_w = (step_row == step_col).astype(F32)
    same_head = (_iota((LANE, LANE), 0) // 64) == (_iota((LANE, LANE), 1) // 64)
    stack_bf = lambda x: _stack_heads(x, 64).astype(BF16)
    cums = [_mm_01_lhs(_cumsum_matrix(rev), lw) for (_, lw, _, _, _, _, rev) in insts]
    pre = []
    for (r, lw, kd, a, b, v, rev), cum in zip(insts, cums):
        end = cum[0:1] if rev else cum[c - 1:c]
        inv_w = jnp.exp(-cum)
        rem_w = jnp.exp(end - cum)
        a_t = a * jnp.exp(cum - lw)
        r_t = r * jnp.exp(cum)
        bk_s = jnp.concatenate([stack_bf(b * inv_w), stack_bf(kd * inv_w)], axis=0)
        bkh = jnp.concatenate([b * rem_w, kd * rem_w], axis=0).astype(BF16)
        pre.append((a_t, r_t, bk_s, bkh, v, jnp.exp(end)))
    ms = [_dot(jnp.concatenate([a_t, r_t], axis=0).astype(BF16), bk_s, _NT) for (a_t, r_t, bk_s, _, _, _) in pre]
    mats = []
    for m, (_, _, _, _, _, _, rev) in zip(ms, insts):
        strict = (step_row < step_col) if rev else (step_row > step_col)
        incl = (step_row <= step_col) if rev else (step_row >= step_col)
        l_ab = jnp.where(strict, m[0:c, 0:LANE], 0.0)
        l_akrk = jnp.concatenate([jnp.where(strict, m[0:c, LANE:2 * LANE], 0.0),
                                  jnp.where(incl, m[c:2 * c, LANE:2 * LANE], 0.0)], axis=0).astype(BF16)
        m_rb = jnp.where(incl, m[c:2 * c, 0:LANE], 0.0).astype(BF16)
        mats.append((l_ab, l_akrk, m_rb))
    pending = list(interleaved)

    def run_interleaved():
        if pending:
            pending.pop(0)()

    lvs = [_dot(l_akrk, stack_bf(pr[4])) for (_, l_akrk, _), pr in zip(mats, pre)]
    run_interleaved()
    ps = [eye_w + l_ab for (l_ab, _, _) in mats]
    lps = [_dot(l_ab.astype(BF16), stack_bf(l_ab)) for (l_ab, _, _) in mats]
    for level in range(1, 6):
        if level < 5:
            xs = [_dot(lp.astype(BF16), jnp.concatenate([stack_bf(p), stack_bf(lp)], axis=1))
                  for lp, p in zip(lps, ps)]
            ps = [p + x[:, 0:LANE] for p, x in zip(ps, xs)]
            lps = [x[:, LANE:2 * LANE] for x in xs]
        else:
            ps = [p + _dot(lp.astype(BF16), stack_bf(p)) for lp, p in zip(lps, ps)]
        if level in (2, 4):
            run_interleaved()
    pxs = [_dot(p.astype(BF16), jnp.concatenate([stack_bf(pr[0]), stack_bf(lv[0:c])], axis=1))
           for p, pr, lv in zip(ps, pre, lvs)]
    run_interleaved()
    mzs = [_dot(mt[2], jnp.concatenate([stack_bf(px[:, 0:LANE]), stack_bf(px[:, LANE:2 * LANE])], axis=1))
           for mt, px in zip(mats, pxs)]
    ts = [_dot(px[:, 0:LANE].astype(BF16), pr[3][0:c], _TN) for px, pr in zip(pxs, pre)]
    gs = [_dot(jnp.concatenate([px[:, LANE:2 * LANE], pr[4]], axis=0).astype(BF16), pr[3], _TN)
          for px, pr in zip(pxs, pre)]
    while pending:
        run_interleaved()
    out = []
    for pr, lv, mz, t, g in zip(pre, lvs, mzs, ts, gs):
        q = pr[1] + mz[:, 0:LANE]
        y0 = mz[:, LANE:2 * LANE] + lv[c:2 * c]
        out.append((jnp.where(same_head, t, 0.0).astype(BF16), jnp.where(same_head, g, 0.0), pr[5],
                    q.astype(BF16), y0))
    return out


def _gla_chunk_terms(insts, interleaved=()):
    c = CHUNK
    step_row = _iota((c, LANE), 0)
    step_col = _iota((c, LANE), 1) % c
    same_head = (_iota((LANE, 2 * LANE), 0) // 64) == (_iota((LANE, 2 * LANE), 1) // LANE)
    cums = [_mm_01_lhs(_cumsum_matrix(rev), g) for (_, _, _, g, rev) in insts]
    pre = []
    for (q, k, v, g, rev), cum in zip(insts, cums):
        end = cum[0:1] if rev else cum[c - 1:c]
        qt = (q * jnp.exp(cum)).astype(BF16)
        k_s = _stack_heads(k * jnp.exp(-cum), 64).astype(BF16)
        kh = (k * jnp.exp(end - cum)).astype(BF16)
        a_col = jnp.broadcast_to(jnp.exp(end), (LANE, LANE)).T
        pre.append((qt, k_s, kh, v.astype(BF16), _stack_heads(v, LANE).astype(BF16), a_col))
    pending = list(interleaved)

    def run_interleaved():
        if pending:
            pending.pop(0)()

    run_interleaved()
    atts = [_dot(pr[0], pr[1], _NT) for pr in pre]
    run_interleaved()
    atts = [jnp.where((step_row <= step_col) if inst[4] else (step_row >= step_col), att, 0.0).astype(BF16)
            for att, inst in zip(atts, insts)]
    o0s = [_dot(att, pr[4]) for att, pr in zip(atts, pre)]
    run_interleaved()
    kvs = [jnp.where(same_head, _dot(pr[2], pr[3], _TN), 0.0) for pr in pre]
    while pending:
        run_interleaved()
    return [(pr[0], o0, pr[5], kv) for pr, o0, kv in zip(pre, o0s, kvs)]


def _rwkv_kernel(seq_len, nseq, is_grid, zero_init, r_ref, k_ref, v_ref, lora_ref, cwr_ref, cwk_ref, cwv_ref,
                 w2_ref, a2_ref, g2_ref, vec_ref, *rest):
    s0_ref = None if zero_init else rest[0]
    (y_ref, sout_ref, pad_ref, rs_ref, ks_ref, vs_ref, kk_ref, bonus_ref, gate_ref, yf_ref, yb_ref, st_ref, tt_ref,
     tg_ref, tw_ref, tq_ref, ty_ref) = rest[0 if zero_init else 1:][:17]
    left_ref, right_ref = rest[-2:] if is_grid else (None, None)
    n_chunks = seq_len // CHUNK
    npp = RW_PAIRS_PER_STEP
    w = npp * LANE
    pair = lambda x, p: x[:, p * LANE:(p + 1) * LANE]
    vec = vec_ref[...]
    w0 = (vec[0:1], vec[1:2])
    a0 = (vec[2:3], vec[3:4])
    k_k, k_a, r_k, ln_w, ln_b = vec[4:5], vec[5:6], vec[6:7], vec[7:8], vec[8:9]
    block_sum = _head_sums
    chains = [(d, p) for d in range(2) for p in range(npp)]
    rows_of = lambda s, c: slice((s * n_chunks + c) * CHUNK, (s * n_chunks + c + 1) * CHUNK)

    zeros = jnp.zeros((CONV_PAD, 3 * w), F32)
    for ref in (pad_ref, left_ref, right_ref) if is_grid else (pad_ref,):
        ref[0:CONV_PAD, :] = zeros
        ref[CONV_PAD + seq_len:2 * CONV_PAD + seq_len, :] = zeros
    cw = jnp.concatenate([cwr_ref[...], cwk_ref[...], cwv_ref[...]], axis=1)
    col = _iota((CHUNK, 3 * w), 0)

    def shift_body(c, carry):
        base = pl.multiple_of(CONV_PAD + c * CHUNK, CHUNK)
        win = pad_ref[pl.ds(base - 8, CHUNK + 16), :]
        left_ref[pl.ds(base, CHUNK), :] = jnp.where(col >= 1, win[7:7 + CHUNK], 0.0)
        right_ref[pl.ds(base, CHUNK), :] = jnp.where(col <= GRID_W - 2, win[9:9 + CHUNK], 0.0)
        return carry

    def conv_chunk(c):
        base = CONV_PAD + c * CHUNK
        acc = jnp.zeros((CHUNK, 3 * w), F32)
        if is_grid:
            for di in (-1, 0, 1):
                row = pl.ds(base + di * GRID_W, CHUNK)
                for dj, src in ((-1, left_ref), (0, pad_ref), (1, right_ref)):
                    tap = (di + 1) * 3 + dj + 1
                    acc = acc + src[row, :] * cw[tap:tap + 1]
        else:
            win = pad_ref[pl.ds(base - 8, CHUNK + 16), :]
            for dj in (-1, 0, 1):
                acc = acc + win[8 + dj:8 + dj + CHUNK] * cw[4 + dj:5 + dj]
        return acc[:, 0:w], acc[:, w:2 * w], acc[:, 2 * w:3 * w]

    def load_sequence(s):
        seq_rows = slice(s * seq_len, (s + 1) * seq_len)
        pad_ref[CONV_PAD:CONV_PAD + seq_len, 0:w] = r_ref[seq_rows, :]
        pad_ref[CONV_PAD:CONV_PAD + seq_len, w:2 * w] = k_ref[seq_rows, :]
        pad_ref[CONV_PAD:CONV_PAD + seq_len, 2 * w:3 * w] = v_ref[seq_rows, :]

    def conv_store(s, c):
        rows = rows_of(s, c)
        rc, kc, vc = conv_chunk(c)
        kk = kc * k_k
        rs_ref[rows, :] = rc
        ks_ref[rows, :] = kc
        vs_ref[rows, :] = vc
        kk_ref[rows, :] = kk * lax.rsqrt(block_sum(kk * kk) + EPS)
        bonus_ref[rows, :] = block_sum(rc * kc * r_k) * vc
        gate_ref[rows, :] = _mm3_split(_split2(_sigmoid(lora_ref[rows, 2 * LANE:3 * LANE])), (g2_ref[0], g2_ref[1]))

    for s in range(nseq):
        for d, p in chains:
            if zero_init:
                st_ref[s, d, p] = jnp.zeros((LANE, LANE), F32)
            else:
                z = jnp.zeros((64, 64), F32)
                st_ref[s, d, p] = jnp.concatenate([jnp.concatenate([s0_ref[s, d, 2 * p], z], axis=1),
                                                   jnp.concatenate([z, s0_ref[s, d, 2 * p + 1]], axis=1)], axis=0)

    if nseq == 1:
        groups = [(0, list(range(g * TERM_UNROLL, (g + 1) * TERM_UNROLL)),
                   list(range(n_chunks - 1 - g * TERM_UNROLL, n_chunks - 1 - (g + 1) * TERM_UNROLL, -1)))
                  for g in range(n_chunks // TERM_UNROLL)]
    else:
        groups = [(s, list(range(n_chunks)), list(range(n_chunks - 1, -1, -1))) for s in range(nseq)]

    def group_terms(group, interleaved):
        s, fwd, bwd = group
        insts, where, lora_in = [], [], {}
        for d, chunks in ((0, fwd), (1, bwd)):
            for c in chunks:
                rows = rows_of(s, c)
                if c not in lora_in:
                    lora_in[c] = (_split2(jnp.tanh(lora_ref[rows, 0:LANE])), _split2(lora_ref[rows, LANE:2 * LANE]))
                lw = -RW_DECAY_SCALE * _sigmoid(w0[d] + _mm3_split(lora_in[c][0], (w2_ref[0, d], w2_ref[1, d])))
                ag = _sigmoid(a0[d] + _mm3_split(lora_in[c][1], (a2_ref[0, d], a2_ref[1, d])))
                rc, kc, vc, kk = rs_ref[rows, :], ks_ref[rows, :], vs_ref[rows, :], kk_ref[rows, :]
                kd = kc * (1.0 + (ag - 1.0) * k_a)
                kb = kk * ag
                for p in range(npp):
                    insts.append((pair(rc, p), pair(lw, p), pair(kd, p), -pair(kk, p), pair(kb, p), pair(vc, p),
                                  d == 1))
                    where.append((d, p, s * n_chunks + c))
        for (d, p, gc), (t, g, w_end, q, y0) in zip(where, _rwkv_chunk_terms(insts, interleaved)):
            tt_ref[d, p, gc] = t
            tg_ref[d, p, gc] = g
            tw_ref[d, p, gc] = jnp.broadcast_to(w_end, (8, LANE))
            tq_ref[d, p, gc] = q
            ty_ref[d, p, gc] = y0

    def scan_step(s, chunk_of_dir):
        gcs = [s * n_chunks + chunk_of_dir[d] for d, _ in chains]
        ss = [st_ref[s, d, p] for d, p in chains]
        sb = [x.astype(BF16) for x in ss]
        ys = [_dot(tq_ref[d, p, gc], b, _NT) + ty_ref[d, p, gc] for (d, p), gc, b in zip(chains, gcs, sb)]
        sn = [x * tw_ref[d, p, gc][0:1] + _dot(b, tt_ref[d, p, gc]) + tg_ref[d, p, gc]
              for (d, p), gc, x, b in zip(chains, gcs, ss, sb)]
        for (d, p), gc, y, x in zip(chains, gcs, ys, sn):
            st_ref[s, d, p] = x
            out_ref = yf_ref if d == 0 else yb_ref
            out_ref[gc * CHUNK:(gc + 1) * CHUNK, p * LANE:(p + 1) * LANE] = y

    def group_scan(group):
        s, fwd, bwd = group
        return [functools.partial(scan_step, s, (cf, cb)) for cf, cb in zip(fwd, bwd)]

    conv_done, seq_loaded = set(), set()

    def group_conv(group):
        s, fwd, bwd = group
        todo = [c for c in sorted(set(fwd) | set(bwd)) if (s, c) not in conv_done]
        conv_done.update((s, c) for c in todo)
        thunks = []
        for i in range(0, len(todo), max(1, -(-len(todo) // TERM_UNROLL))):
            part = todo[i:i + max(1, -(-len(todo) // TERM_UNROLL))]
            need_load = s not in seq_loaded
            seq_loaded.add(s)

            def run(part=part, need_load=need_load):
                if need_load:
                    assert not is_grid or nseq == 1
                    load_sequence(s)
                for c in part:
                    conv_store(s, c)

            thunks.append(run)
        return thunks

    def merge(a, b):
        n = max(len(a), len(b))
        pick = lambda lst, i: lst[i] if i < len(lst) else (lambda: None)
        return [lambda i=i: (pick(a, i)(), pick(b, i)()) for i in range(n)]

    if is_grid:
        load_sequence(0)
        seq_loaded.add(0)
        lax.fori_loop(0, n_chunks, shift_body, 0)
    for thunk in group_conv(groups[0]):
        thunk()
    for g, group in enumerate(groups):
        scans = group_scan(groups[g - 1]) if g else []
        convs = group_conv(groups[g + 1]) if g + 1 < len(groups) else []
        group_terms(group, merge(scans, convs))
    for step in group_scan(groups[-1]):
        step()
    for s in range(nseq):
        for d, p in chains:
            x = st_ref[s, d, p]
            sout_ref[s, d, 2 * p] = x[0:64, 0:64]
            sout_ref[s, d, 2 * p + 1] = x[64:LANE, 64:LANE]

    def post_body(j, carry):
        offs = [pl.multiple_of((j * TERM_UNROLL + u) * CHUNK, CHUNK) for u in range(TERM_UNROLL)]
        ys = [yf_ref[pl.ds(off, CHUNK), :] + yb_ref[pl.ds(off, CHUNK), :] for off in offs]
        mus = [block_sum(y) * (1.0 / 64) for y in ys]
        dlts = [y - mu for y, mu in zip(ys, mus)]
        vrs = [block_sum(dlt * dlt) * (1.0 / 64) for dlt in dlts]
        for off, dlt, var in zip(offs, dlts, vrs):
            yn = dlt * lax.rsqrt(var + RW_LN_EPS) * ln_w + ln_b
            y_ref[pl.ds(off, CHUNK), :] = (yn + bonus_ref[pl.ds(off, CHUNK), :]) * gate_ref[pl.ds(off, CHUNK), :]
        return carry

    lax.fori_loop(0, nseq * n_chunks // TERM_UNROLL, post_body, 0)


def _rwkv_mixer(proj, first_seq, n_seq, seq_len, is_grid, prm, s0):
    n_pairs = RW_W // LANE
    n_chunks = seq_len // CHUNK
    npp = RW_PAIRS_PER_STEP
    w = npp * LANE
    nseq = max(1, RW_GROUPS * TERM_UNROLL // n_chunks)
    assert n_chunks % TERM_UNROLL == 0 and n_pairs % npp == 0
    assert (n_chunks == TERM_UNROLL or nseq == 1) and n_seq % nseq == 0 and first_seq % nseq == 0
    rows = nseq * seq_len
    total_chunks = nseq * n_chunks
    first = first_seq // nseq
    col = lambda cb: (lambda b, p: (b + first, cb // npp + p))
    par = lambda cb: (lambda b, p: (0, cb // npp + p))
    state_spec = pl.BlockSpec((nseq, 2, 2 * npp, 64, 64), lambda b, p: (b, 0, p, 0, 0))
    kernel = functools.partial(_rwkv_kernel, seq_len, nseq, is_grid, s0 is None)
    y, s_out = pl.pallas_call(
        kernel,
        grid=(n_seq // nseq, n_pairs // npp),
        in_specs=[pl.BlockSpec((rows, w), col(CB_R)),
                  pl.BlockSpec((rows, w), col(CB_K)),
                  pl.BlockSpec((rows, w), col(CB_V)),
                  pl.BlockSpec((rows, 3 * LANE), lambda b, p: (b + first, CB_LORA // 3)),
                  pl.BlockSpec((9, w), par(CB_R)),
                  pl.BlockSpec((9, w), par(CB_K)),
                  pl.BlockSpec((9, w), par(CB_V)),
                  pl.BlockSpec((2, 2, LANE, w), lambda b, p: (0, 0, 0, p)),
                  pl.BlockSpec((2, 2, LANE, w), lambda b, p: (0, 0, 0, p)),
                  pl.BlockSpec((2, LANE, w), lambda b, p: (0, 0, p)),
                  pl.BlockSpec((16, w), lambda b, p: (0, p))] + ([] if s0 is None else [state_spec]),
        out_specs=[pl.BlockSpec((rows, w), lambda b, p: (b, p)), state_spec],
        out_shape=[jax.ShapeDtypeStruct((n_seq * seq_len, RW_W), F32),
                   jax.ShapeDtypeStruct((n_seq, 2, 2 * n_pairs, 64, 64), F32)],
        scratch_shapes=[pltpu.VMEM((seq_len + 2 * CONV_PAD, 3 * w), F32)]
                       + [pltpu.VMEM((rows, w), F32)] * 8
                       + [pltpu.VMEM((nseq, 2, npp, LANE, LANE), F32),
                          pltpu.VMEM((2, npp, total_chunks, LANE, LANE), BF16),
                          pltpu.VMEM((2, npp, total_chunks, LANE, LANE), F32),
                          pltpu.VMEM((2, npp, total_chunks, 8, LANE), F32),
                          pltpu.VMEM((2, npp, total_chunks, CHUNK, LANE), BF16),
                          pltpu.VMEM((2, npp, total_chunks, CHUNK, LANE), F32)]
                       + ([pltpu.VMEM((seq_len + 2 * CONV_PAD, 3 * w), F32)] * 2 if is_grid else []),
        compiler_params=_cparams(2),
        name="rwkv_mixer",
    )(proj, proj, proj, proj, prm['conv'], prm['conv'], prm['conv'], prm['w2p'], prm['a2p'], prm['g2'],
      prm['vec'], *([] if s0 is None else [s0]))
    return y, s_out


def _gla_kernel(seq_len, nseq, zero_init, q_ref, k_ref, v_ref, og_ref, lgk_ref, gk2_ref, gvec_ref, *rest):
    s0_ref = None if zero_init else rest[0]
    y_ref, sout_ref, of_ref, ob_ref, st_ref, tq_ref, to_ref, ta_ref, tkv_ref = rest[0 if zero_init else 1:]
    n_chunks = seq_len // CHUNK
    npp = GLA_PAIRS
    gvec = gvec_ref[...]
    chains = [(d, p) for d in range(2) for p in range(npp)]

    for s in range(nseq):
        for d, p in chains:
            if zero_init:
                st_ref[s, d, p] = jnp.zeros((LANE, 2 * LANE), F32)
            else:
                z = jnp.zeros((64, LANE), F32)
                st_ref[s, d, p] = jnp.concatenate([jnp.concatenate([s0_ref[s, d, 2 * p], z], axis=1),
                                                   jnp.concatenate([z, s0_ref[s, d, 2 * p + 1]], axis=1)], axis=0)

    if nseq == 1:
        groups = [(0, list(range(g * TERM_UNROLL, (g + 1) * TERM_UNROLL)),
                   list(range(n_chunks - 1 - g * TERM_UNROLL, n_chunks - 1 - (g + 1) * TERM_UNROLL, -1)))
                  for g in range(n_chunks // TERM_UNROLL)]
    else:
        groups = [(s, list(range(n_chunks)), list(range(n_chunks - 1, -1, -1))) for s in range(nseq)]

    def group_terms(group, interleaved):
        s, fwd, bwd = group
        insts, where, lgk_in = [], [], {}
        for d, chunks in ((0, fwd), (1, bwd)):
            for c in chunks:
                gc = s * n_chunks + c
                rows = slice(gc * CHUNK, (gc + 1) * CHUNK)
                if c not in lgk_in:
                    lgk_in[c] = _split2(lgk_ref[rows, :])
                x = _mm3_split(lgk_in[c], (gk2_ref[0, d], gk2_ref[1, d])) + gvec[d:d + 1, 0:GLA_QK_W]
                g = _log_sigmoid(x) * (1.0 / GLA_GATE_NORM)
                qc = q_ref[rows, :] * GLA_Q_SCALE
                kc = k_ref[rows, :]
                vc = v_ref[rows, :]
                for p in range(npp):
                    qk = slice(p * LANE, (p + 1) * LANE)
                    insts.append((qc[:, qk], kc[:, qk], vc[:, 2 * p * LANE:2 * (p + 1) * LANE], g[:, qk], d == 1))
                    where.append((d, p, gc))
        for (d, p, gc), (qt, o0, a_col, kv) in zip(where, _gla_chunk_terms(insts, interleaved)):
            tq_ref[d, p, gc] = qt
            to_ref[d, p, gc] = o0
            ta_ref[d, p, gc] = a_col
            tkv_ref[d, p, gc] = kv

    def scan_step(s, chunk_of_dir):
        gcs = [s * n_chunks + chunk_of_dir[d] for d, _ in chains]
        ss = [st_ref[s, d, p] for d, p in chains]
        os_ = [_dot(tq_ref[d, p, gc], x.astype(BF16)) + to_ref[d, p, gc] for (d, p), gc, x in zip(chains, gcs, ss)]
        for (d, p), gc, x, o in zip(chains, gcs, ss, os_):
            a_col = ta_ref[d, p, gc]
            st_ref[s, d, p] = x * jnp.concatenate([a_col, a_col], axis=1) + tkv_ref[d, p, gc]
            out_ref = of_ref if d == 0 else ob_ref
            out_ref[gc * CHUNK:(gc + 1) * CHUNK, 2 * p * LANE:2 * (p + 1) * LANE] = o

    def group_scan(group):
        s, fwd, bwd = group
        return [functools.partial(scan_step, s, (cf, cb)) for cf, cb in zip(fwd, bwd)]

    for g, group in enumerate(groups):
        group_terms(group, group_scan(groups[g - 1]) if g else [])
    for step in group_scan(groups[-1]):
        step()
    for s in range(nseq):
        for d, p in chains:
            x = st_ref[s, d, p]
            sout_ref[s, d, 2 * p] = x[0:64, 0:LANE]
            sout_ref[s, d, 2 * p + 1] = x[64:LANE, LANE:2 * LANE]

    def post_body(c, carry):
        off = pl.multiple_of(c * CHUNK, CHUNK)
        for h in range(2 * npp):
            hs = slice(h * LANE, (h + 1) * LANE)
            o = of_ref[pl.ds(off, CHUNK), hs] + ob_ref[pl.ds(off, CHUNK), hs]
            gate = _silu(og_ref[pl.ds(off, CHUNK), hs])
            y_ref[pl.ds(off, CHUNK), hs] = _rmsnorm_rows(o) * gvec[2:3, hs] * gate
        return carry

    lax.fori_loop(0, nseq * n_chunks, post_body, 0)


def _gla_mixer(proj, first_seq, n_seq, seq_len, prm, s0):
    npp = GLA_PAIRS
    n_heads = 2 * npp
    n_chunks = seq_len // CHUNK
    nseq = max(1, RW_GROUPS * TERM_UNROLL // n_chunks)
    assert n_chunks % TERM_UNROLL == 0
    assert (n_chunks == TERM_UNROLL or nseq == 1) and n_seq % nseq == 0 and first_seq % nseq == 0
    rows = nseq * seq_len
    total_chunks = nseq * n_chunks
    first = first_seq // nseq
    state_spec = pl.BlockSpec((nseq, 2, n_heads, 64, LANE), lambda b: (b, 0, 0, 0, 0))
    kernel = functools.partial(_gla_kernel, seq_len, nseq, s0 is None)
    y, s_out = pl.pallas_call(
        kernel,
        grid=(n_seq // nseq,),
        in_specs=[pl.BlockSpec((rows, GLA_QK_W), lambda b: (b + first, CB_GQ * LANE // GLA_QK_W)),
                  pl.BlockSpec((rows, GLA_QK_W), lambda b: (b + first, CB_GK * LANE // GLA_QK_W)),
                  pl.BlockSpec((rows, GLA_V_W), lambda b: (b + first, CB_GV * LANE // GLA_V_W)),
                  pl.BlockSpec((rows, GLA_V_W), lambda b: (b + first, CB_OG * LANE // GLA_V_W)),
                  pl.BlockSpec((rows, LANE), lambda b: (b + first, CB_LGK)),
                  pl.BlockSpec((2, 2, LANE, GLA_QK_W), lambda b: (0, 0, 0, 0)),
                  pl.BlockSpec((8, GLA_V_W), lambda b: (0, 0))] + ([] if s0 is None else [state_spec]),
        out_specs=[pl.BlockSpec((rows, GLA_V_W), lambda b: (b, 0)), state_spec],
        out_shape=[jax.ShapeDtypeStruct((n_seq * seq_len, GLA_V_W), F32),
                   jax.ShapeDtypeStruct((n_seq, 2, n_heads, 64, LANE), F32)],
        scratch_shapes=[pltpu.VMEM((rows, GLA_V_W), F32)] * 2
                       + [pltpu.VMEM((nseq, 2, npp, LANE, 2 * LANE), F32),
                          pltpu.VMEM((2, npp, total_chunks, CHUNK, LANE), BF16),
                          pltpu.VMEM((2, npp, total_chunks, CHUNK, 2 * LANE), F32),
                          pltpu.VMEM((2, npp, total_chunks, LANE, LANE), F32),
                          pltpu.VMEM((2, npp, total_chunks, LANE, 2 * LANE), F32)],
        compiler_params=_cparams(1),
        name="gla_mixer",
    )(proj, proj, proj, proj, proj, prm['gk2p'], prm['gvec'], *([] if s0 is None else [s0]))
    return y, s_out


OUT_TM = 512
ROUTE_NEG = -1e30
LANE_GROUP0 = N_EXPERTS


def _route(logits):
    lane = _iota(logits.shape, 1)
    lane_f = lane.astype(F32)
    big = float(LANE)
    is_g = (lane >= LANE_GROUP0) & (lane < LANE_GROUP0 + 4)
    gmax = jnp.max(jnp.where(is_g, logits, ROUTE_NEG), axis=-1, keepdims=True)
    gidx = jnp.min(jnp.where(is_g & (logits == gmax), lane_f, big), axis=-1, keepdims=True) - LANE_GROUP0
    gsum = jnp.sum(jnp.where(is_g, jnp.exp(jnp.minimum(logits - gmax, 0.0)), 0.0), axis=-1, keepdims=True)
    g_w = 1.0 / gsum
    in_grp = (lane < N_EXPERTS) & ((lane // 4).astype(F32) == gidx)
    m1 = jnp.max(jnp.where(in_grp, logits, ROUTE_NEG), axis=-1, keepdims=True)
    i1 = jnp.min(jnp.where(in_grp & (logits == m1), lane_f, big), axis=-1, keepdims=True)
    rest = in_grp & (lane_f != i1)
    m2 = jnp.max(jnp.where(rest, logits, ROUTE_NEG), axis=-1, keepdims=True)
    i2 = jnp.min(jnp.where(rest & (logits == m2), lane_f, big), axis=-1, keepdims=True)
    t = jnp.exp(m2 - m1)
    w1 = g_w / (1.0 + t)
    return jnp.where(lane_f == i1, w1, 0.0) + jnp.where(lane_f == i2, w1 * t, 0.0)


def _outproj_kernel(tiles, yrc_ref, yrd_ref, ygc_ref, ygd_ref, xc_ref, xd_ref, mod_ref, wo_ref, g_ref, wr_ref,
                    br_ref, x1_ref, h2_ref, cmb_ref):
    def run(yr_ref, yg_ref, x_ref):
        m = mod_ref[0]
        mix = _mm(yr_ref[...], wo_ref[0:RW_W, :]) + _mm(yg_ref[...], wo_ref[RW_W:RW_W + GLA_V_W, :])
        x1 = x_ref[...] + m[2:3] * mix
        h2 = _rmsnorm_rows(x1) * g_ref[...] * (1.0 + m[4:5]) + m[3:4]
        x1_ref[...] = x1
        h2_ref[...] = h2.astype(BF16)
        cmb_ref[...] = _route(_mm3_split(_split2(h2), (wr_ref[0], wr_ref[1])) + br_ref[...])

    tiles.by_pass(functools.partial(run, yrc_ref, ygc_ref, xc_ref), functools.partial(run, yrd_ref, ygd_ref, xd_ref))


def _out_projection(tiles, y_rw, y_gla, x, mod, w_out, norm_g, w_route, b_route):
    n = x[0].shape[0] + x[1].shape[0]
    full = lambda a: pl.BlockSpec(a.shape, lambda i: (0,) * a.ndim)
    return pl.pallas_call(
        functools.partial(_outproj_kernel, tiles),
        grid=(tiles.n_ctx + tiles.n_dec,),
        in_specs=[*tiles.specs(RW_W), *tiles.specs(GLA_V_W), *tiles.specs(D_MODEL), tiles.mod_spec(),
                  full(w_out), full(norm_g), full(w_route), full(b_route)],
        out_specs=[tiles.merged(D_MODEL), tiles.merged(D_MODEL), tiles.merged(LANE)],
        out_shape=[jax.ShapeDtypeStruct((n, D_MODEL), F32), jax.ShapeDtypeStruct((n, D_MODEL), BF16),
                   jax.ShapeDtypeStruct((n, LANE), F32)],
        compiler_params=_cparams(1),
        name="out_proj_router",
    )(*y_rw, *y_gla, *x, mod, w_out, norm_g, w_route, b_route)


MOE_TM = 512
MOE_RB = 128
MOE_INTERLEAVE = 4
SLOT_ALIGN = 16
MOE_SLOTS = 2 * MOE_TM + N_EXPERTS * SLOT_ALIGN + MOE_RB


def _stage_expert_weights(srcs_hbm, dst_refs, stage_refs, sems):
    def copies(e):
        return [pltpu.make_async_copy(src.at[e], stage.at[e % 2], sem.at[e % 2])
                for src, stage, sem in zip(srcs_hbm, stage_refs, sems)]

    for c in copies(0):
        c.start()
    for e in range(N_EXPERTS):
        if e + 1 < N_EXPERTS:
            for c in copies(e + 1):
                c.start()
        for c, dst, stage in zip(copies(e), dst_refs, stage_refs):
            c.wait()
            dst[e] = stage[e % 2].astype(BF16)


def _moe_kernel(tiles, h2_ref, cmb_ref, x1_ref, mod_ref, w1_hbm, w3_hbm, w2_hbm, fg_ref, yc_ref, yd_ref,
                xs_ref, ys_ref, w1_ref, w3_ref, w2_ref, stage1_ref, stage3_ref, stage2_ref, sem1, sem3, sem2):
    @pl.when(pl.program_id(0) == 0)
    def _():
        _stage_expert_weights((w1_hbm, w3_hbm, w2_hbm), (w1_ref, w3_ref, w2_ref),
                              (stage1_ref, stage3_ref, stage2_ref), (sem1, sem3, sem2))

    cmb = cmb_ref[...]
    lane = _iota(cmb.shape, 1).astype(F32)
    sel = cmb > 0.0
    sel01 = jnp.where(sel, 1.0, 0.0).astype(BF16)
    before = (_iota((MOE_TM, MOE_TM), 0) > _iota((MOE_TM, MOE_TM), 1)).astype(BF16)
    pos = _dot(before, sel01)
    cnt = pos[MOE_TM - 1:MOE_TM] + sel01[MOE_TM - 1:MOE_TM].astype(F32)
    seg = jnp.floor((cnt + (SLOT_ALIGN - 1)) * (1.0 / SLOT_ALIGN))
    lower_experts = (_iota((LANE, LANE), 0) < _iota((LANE, LANE), 1)).astype(BF16)
    start = _dot(jnp.broadcast_to(seg, (8, LANE)).astype(BF16), lower_experts)[0:1] * SLOT_ALIGN
    n_blk = jnp.floor((cnt + (MOE_RB - 1)) * (1.0 / MOE_RB)).astype(jnp.int32)
    start_i = start.astype(jnp.int32)
    cnt_i = cnt.astype(jnp.int32)
    slot = start + pos
    e_a = jnp.min(jnp.where(sel, lane, float(LANE)), axis=-1, keepdims=True)
    e_b = jnp.max(jnp.where(sel, lane, -1.0), axis=-1, keepdims=True)
    pick = lambda e, x: jnp.sum(jnp.where(lane == e, x, 0.0), axis=-1, keepdims=True)
    slot_a, w_a = pick(e_a, slot), pick(e_a, cmb)
    slot_b = jnp.where(e_b != e_a, pick(e_b, slot), -1.0)
    w_b = pick(e_b, cmb)

    slots_t = jnp.where(lane == 0.0, slot_a, jnp.where(lane == 1.0, slot_b, -1.0)).T
    row_slot = _iota((MOE_SLOTS, MOE_TM), 0).astype(F32)
    gather = jnp.where((row_slot == slots_t[0:1]) | (row_slot == slots_t[1:2]), 1.0, 0.0).astype(BF16)
    xs_ref[...] = _dot(gather, h2_ref[...]).astype(BF16)
    ys_ref[...] = jnp.zeros_like(ys_ref)

    row_in_blk = _iota((MOE_RB, D_MODEL), 0)

    def expert_blocks(experts, r0s, ends):
        xbs = [xs_ref[pl.ds(r0, MOE_RB), :] for r0 in r0s]
        gates = [_dot(xb, w3_ref[e]) for xb, e in zip(xbs, experts)]
        ups = [_dot(xb, w1_ref[e]) for xb, e in zip(xbs, experts)]
        acts = [(_silu(g) * u).astype(BF16) for g, u in zip(gates, ups)]
        outs = [_dot(a, w2_ref[e]) for a, e in zip(acts, experts)]
        for r0, end, out in zip(r0s, ends, outs):
            keep = row_in_blk + r0 >= end
            ys_ref[pl.ds(r0, MOE_RB), :] = jnp.where(keep, ys_ref[pl.ds(r0, MOE_RB), :], out.astype(BF16))

    seg_start = [pl.multiple_of(start_i[0, e], SLOT_ALIGN) for e in range(N_EXPERTS)]
    seg_end = [seg_start[e] + cnt_i[0, e] for e in range(N_EXPERTS)]
    for e0 in range(0, N_EXPERTS, MOE_INTERLEAVE):
        es = list(range(e0, e0 + MOE_INTERLEAVE))
        expert_blocks(es, [seg_start[e] for e in es], [seg_end[e] for e in es])
    for e in range(N_EXPERTS):
        def extra_block(b, carry, e=e):
            expert_blocks([e], [pl.multiple_of(seg_start[e] + b * MOE_RB, SLOT_ALIGN)], [seg_end[e]])
            return carry

        lax.fori_loop(1, n_blk[0, e], extra_block, 0)

    col_slot = _iota((MOE_TM, MOE_SLOTS), 1).astype(F32)
    scatter = (jnp.where(col_slot == slot_a, w_a, 0.0) + jnp.where(col_slot == slot_b, w_b, 0.0)).astype(BF16)
    x2 = x1_ref[...] + mod_ref[0][5:6] * _dot(scatter, ys_ref[...])
    y = _rmsnorm_rows(x2) * fg_ref[...]

    def write(y_ref):
        y_ref[...] = y

    tiles.by_pass(functools.partial(write, yc_ref), functools.partial(write, yd_ref))


def _moe(tiles, h2, cmb, x1, mod, w1, w3, w2, final_g):
    assert tiles.tm == MOE_TM
    hbm = pl.BlockSpec(memory_space=pl.ANY)
    out_ctx, out_dec = tiles.specs(D_MODEL)
    return pl.pallas_call(
        functools.partial(_moe_kernel, tiles),
        grid=(tiles.n_ctx + tiles.n_dec,),
        in_specs=[tiles.merged(D_MODEL), tiles.merged(LANE), tiles.merged(D_MODEL), tiles.mod_spec(),
                  hbm, hbm, hbm, pl.BlockSpec((1, D_MODEL), lambda i: (0, 0))],
        out_specs=[out_ctx, out_dec],
        out_shape=[jax.ShapeDtypeStruct((tiles.n_ctx * MOE_TM, D_MODEL), F32),
                   jax.ShapeDtypeStruct((tiles.n_dec * MOE_TM, D_MODEL), F32)],
        scratch_shapes=[pltpu.VMEM((MOE_SLOTS, D_MODEL), BF16), pltpu.VMEM((MOE_SLOTS, D_MODEL), BF16),
                        pltpu.VMEM((N_EXPERTS, D_MODEL, D_EXPERT), BF16),
                        pltpu.VMEM((N_EXPERTS, D_MODEL, D_EXPERT), BF16),
                        pltpu.VMEM((N_EXPERTS, D_EXPERT, D_MODEL), BF16),
                        pltpu.VMEM((2, D_MODEL, D_EXPERT), F32), pltpu.VMEM((2, D_MODEL, D_EXPERT), F32),
                        pltpu.VMEM((2, D_EXPERT, D_MODEL), F32)] + [pltpu.SemaphoreType.DMA((2,))] * 3,
        compiler_params=_cparams(1),
        name="moe_experts",
    )(h2, cmb, x1, mod, w1, w3, w2, final_g)


def _pad_rows(x, rows):
    return jnp.pad(x, ((0, rows - x.shape[0]),) + ((0, 0),) * (x.ndim - 1))


def _pack_params(l, w_in, rw_conv, rw_w0, rw_w2, rw_a0, rw_a2, rw_g2, rw_k_k, rw_k_a, rw_r_k, rw_ln_w, rw_ln_b,
                 gla_gk2, gla_gk_b, gla_norm_g, moe_w_group, moe_b_group, moe_w_expert, moe_b_expert):
    wi = w_in[l]
    z = lambda n: jnp.zeros((D_MODEL, n), F32)
    w_in_t = jnp.swapaxes(wi, 0, 1)
    z64 = jnp.zeros((64, RW_W), F32)
    w2p = jnp.stack([jnp.concatenate([rw_w2[l, 0], z64], 0), jnp.concatenate([z64, rw_w2[l, 1]], 0)])
    a2p = jnp.stack([jnp.concatenate([rw_a2[l, 0], z64], 0), jnp.concatenate([z64, rw_a2[l, 1]], 0)])
    vec = _pad_rows(jnp.stack([rw_w0[l, 0], rw_w0[l, 1], rw_a0[l, 0], rw_a0[l, 1], rw_k_k[l], rw_k_a[l],
                               rw_r_k[l].reshape(RW_W), rw_ln_w[l], rw_ln_b[l]]), 16)
    rw = {'conv': rw_conv[l].reshape(9, 3 * RW_W), 'w2p': _split_param(w2p), 'a2p': _split_param(a2p),
          'g2': _split_param(rw_g2[l]), 'vec': vec}
    gk2p = jnp.stack([_pad_rows(gla_gk2[l, 0], LANE),
                      _pad_rows(jnp.concatenate([jnp.zeros((16, GLA_QK_W), F32), gla_gk2[l, 1]], 0), LANE)])
    gk_b = jnp.pad(gla_gk_b[l], ((0, 0), (0, GLA_V_W - GLA_QK_W)))
    gvec = _pad_rows(jnp.concatenate([gk_b, jnp.tile(gla_norm_g[l], GLA_V_W // LANE)[None]], axis=0), 8)
    gla = {'gk2p': _split_param(gk2p), 'gvec': gvec}
    w_route = _split_param(jnp.concatenate([moe_w_expert[l], moe_w_group[l], z(LANE - N_EXPERTS - 4)], axis=1))
    b_route = jnp.concatenate([moe_b_expert[l], moe_b_group[l], jnp.zeros((LANE - N_EXPERTS - 4,), F32)])[None]
    return w_in_t, rw, gla, w_route, b_route


def kernel(x_prompt, x_sample, state_rwkv, state_gla, c, c_ctx, norm1_g, norm2_g, w_ada, b_ada, w_in, w_out,
           rw_conv, rw_w0, rw_w2, rw_a0, rw_a2, rw_g2, rw_k_k, rw_k_a, rw_r_k, rw_ln_w, rw_ln_b,
           gla_gk2, gla_gk_b, gla_norm_g, moe_w_group, moe_b_group, moe_w_expert, moe_b_expert,
           moe_w1, moe_w3, moe_w2, final_g):
    depth = w_in.shape[0]
    assert depth == 1, "the packed layout below handles the single-layer trunk of this problem"
    l = 0
    n_dec = x_sample.shape[0]
    ctx_row = n_dec
    cond8 = _pad_rows(jnp.concatenate([c, c_ctx[None]], axis=0), 8)
    mod = _modulation(cond8, w_ada[l], b_ada[l][None]).reshape(8, N_MOD, D_MODEL)
    pk = _pack_params(l, w_in, rw_conv, rw_w0, rw_w2, rw_a0, rw_a2, rw_g2, rw_k_k, rw_k_a, rw_r_k, rw_ln_w,
                      rw_ln_b, gla_gk2, gla_gk_b, gla_norm_g, moe_w_group, moe_b_group, moe_w_expert,
                      moe_b_expert)
    w_in_t, rw, gla, w_route, b_route = pk

    n_ctx, ctx_len, _ = x_prompt.shape
    dec_len = x_sample.shape[1]
    x_ctx = x_prompt.reshape(n_ctx * ctx_len, D_MODEL)
    x_dec = x_sample.reshape(n_dec * dec_len, D_MODEL)
    assert (n_ctx * ctx_len) % dec_len == 0, "denoising sequences must start on a dec_len row block of proj"
    first_dec = n_ctx * ctx_len // dec_len
    tiles = _Tiles(n_ctx * ctx_len, n_dec * dec_len, dec_len, ctx_row, PROJ_TM)
    assert PROJ_TM == OUT_TM == MOE_TM

    proj = _in_projection(tiles, x_ctx, x_dec, mod, norm1_g[l][None], w_in_t)
    y_rw_c, s_rw = _rwkv_mixer(proj, 0, n_ctx, ctx_len, False, rw, None)
    y_gla_c, s_gla = _gla_mixer(proj, 0, n_ctx, ctx_len, gla, None)
    y_rw_d, _ = _rwkv_mixer(proj, first_dec, n_dec, dec_len, True, rw, state_rwkv[:, l])
    y_gla_d, _ = _gla_mixer(proj, first_dec, n_dec, dec_len, gla, state_gla[:, l])
    x1, h2, cmb = _out_projection(tiles, (y_rw_c, y_rw_d), (y_gla_c, y_gla_d), (x_ctx, x_dec), mod,
                                  w_out[l].astype(BF16), norm2_g[l][None], w_route, b_route)
    y_ctx, y_dec = _moe(tiles, h2, cmb, x1, mod, moe_w1[l], moe_w3[l], moe_w2[l], final_g[None])
    return (y_ctx.reshape(x_prompt.shape), y_dec.reshape(x_sample.shape), s_rw[:, None], s_gla[:, None])
```

```python
import functools

import jax
import jax.numpy as jnp
from jax import lax
from jax.experimental import pallas as pl
from jax.experimental.pallas import tpu as pltpu

F32 = jnp.float32
BF16 = jnp.bfloat16

D_MODEL = 1024
RW_W = 512
GLA_V_W = 512
GLA_QK_W = 256
N_EXPERTS = 16
D_EXPERT = 256
N_MOD = 6
EPS = 1e-6
RW_LN_EPS = 64e-5
RW_DECAY_SCALE = 0.606531
GLA_GATE_NORM = 16.0
GLA_Q_SCALE = 64 ** -0.5
GRID_W = 64

LANE = 128
CHUNK = 64
CONV_PAD = 128
TERM_UNROLL = 4
RW_PAIRS_PER_STEP = 2
GLA_PAIRS = 2
RW_GROUPS = 4
D_PROJ = 28 * LANE
VMEM_LIMIT = 56 * 1024 * 1024

CB_R, CB_K, CB_V, CB_LORA, CB_LGK, CB_GQ, CB_GK, CB_GV, CB_OG = 0, 4, 8, 12, 15, 16, 18, 20, 24

_NN = (((1,), (0,)), ((), ()))
_NT = (((1,), (1,)), ((), ()))
_TN = (((0,), (0,)), ((), ()))


def _dot(a, b, dims=_NN):
    return lax.dot_general(a, b, dims, preferred_element_type=F32)


def _mm(a, b, dims=_NN):
    return _dot(a.astype(BF16), b.astype(BF16), dims)


def _split2(x):
    hi = x.astype(BF16)
    lo = (x - hi.astype(F32)).astype(BF16)
    return hi, lo


def _mm3(a, b, dims=_NN):
    ah, al = _split2(a)
    bh, bl = _split2(b)
    return _dot(ah, bh, dims) + _dot(ah, bl, dims) + _dot(al, bh, dims)


def _mm_01_lhs(a01, b, dims=_NN):
    b1, b2 = _split2(b)
    return _dot(a01, b1, dims) + _dot(a01, b2, dims)


def _sigmoid(x):
    return 0.5 * jnp.tanh(0.5 * x) + 0.5


def _silu(x):
    return x * _sigmoid(x)


def _log_sigmoid(x):
    return jnp.minimum(x, 0.0) - jnp.log(1.0 + jnp.exp(-jnp.abs(x)))


def _iota(shape, dim):
    return lax.broadcasted_iota(jnp.int32, shape, dim)


def _cparams(n_axes):
    return pltpu.CompilerParams(dimension_semantics=("arbitrary",) * n_axes, vmem_limit_bytes=VMEM_LIMIT)


MOD_TN = 768


def _mod_kernel(c_ref, w_ref, b_ref, o_ref):
    o_ref[...] = _mm3(_silu(c_ref[...]), w_ref[...]) + b_ref[...]


def _modulation(cond8, w_ada, b_ada):
    n = w_ada.shape[1]
    return pl.pallas_call(
        _mod_kernel,
        grid=(n // MOD_TN,),
        in_specs=[pl.BlockSpec((8, D_MODEL), lambda j: (0, 0)),
                  pl.BlockSpec((D_MODEL, MOD_TN), lambda j: (0, j)),
                  pl.BlockSpec((1, MOD_TN), lambda j: (0, j))],
        out_specs=pl.BlockSpec((8, MOD_TN), lambda j: (0, j)),
        out_shape=jax.ShapeDtypeStruct((8, n), F32),
        compiler_params=_cparams(1),
        name="adaln_mod",
    )(cond8, w_ada, b_ada)


PROJ_TM = 512
D_IN = 3488
N_LGK = 32


def _rmsnorm_rows(x):
    return x * lax.rsqrt(jnp.mean(x * x, axis=-1, keepdims=True) + EPS)


class _Tiles:
    def __init__(self, n_ctx_tokens, n_dec_tokens, dec_seq_len, ctx_row, tm):
        self.tm = tm
        self.n_ctx = n_ctx_tokens // tm
        self.n_dec = n_dec_tokens // tm
        self.per_seq = dec_seq_len // tm
        self.ctx_row = ctx_row

    def specs(self, width):
        last_ctx = self.n_ctx - 1
        n_ctx = self.n_ctx
        return (pl.BlockSpec((self.tm, width), lambda i: (jnp.minimum(i, last_ctx), 0)),
                pl.BlockSpec((self.tm, width), lambda i: (jnp.maximum(i - n_ctx, 0), 0)))

    def merged(self, width):
        return pl.BlockSpec((self.tm, width), lambda i: (i, 0))

    def mod_spec(self):
        n_ctx, per_seq, ctx_row = self.n_ctx, self.per_seq, self.ctx_row
        return pl.BlockSpec((1, N_MOD, D_MODEL),
                            lambda i: (jnp.where(i < n_ctx, ctx_row, (i - n_ctx) // per_seq), 0, 0))

    def by_pass(self, run_ctx, run_dec):
        i = pl.program_id(0)
        pl.when(i < self.n_ctx)(run_ctx)
        pl.when(i >= self.n_ctx)(run_dec)


def _inproj_kernel(tiles, xc_ref, xd_ref, mod_ref, g_ref, wt_ref, o_ref, w_ref):
    @pl.when(pl.program_id(0) == 0)
    def _():
        for j in range(D_PROJ // LANE):
            if j == CB_LGK:
                blk = jnp.concatenate([wt_ref[D_IN - N_LGK:D_IN, :], jnp.zeros((LANE - N_LGK, D_MODEL), F32)], axis=0)
            else:
                src = j if j < CB_LGK else j - 1
                blk = wt_ref[src * LANE:(src + 1) * LANE, :]
            w_ref[:, j * LANE:(j + 1) * LANE] = blk.T.astype(BF16)

    def run(x_ref):
        m = mod_ref[0]
        h = _rmsnorm_rows(x_ref[...]) * g_ref[...] * (1.0 + m[1:2]) + m[0:1]
        o_ref[...] = _mm(h, w_ref[...])

    tiles.by_pass(functools.partial(run, xc_ref), functools.partial(run, xd_ref))


def _in_projection(tiles, x_ctx, x_dec, mod, norm_g, w_in_t):
    full = lambda a: pl.BlockSpec(a.shape, lambda i: (0,) * a.ndim)
    return pl.pallas_call(
        functools.partial(_inproj_kernel, tiles),
        grid=(tiles.n_ctx + tiles.n_dec,),
        in_specs=[*tiles.specs(D_MODEL), tiles.mod_spec(), full(norm_g),
                  pl.BlockSpec(w_in_t.shape, lambda i: (0, 0), pipeline_mode=pl.Buffered(1))],
        out_specs=tiles.merged(D_PROJ),
        out_shape=jax.ShapeDtypeStruct((x_ctx.shape[0] + x_dec.shape[0], D_PROJ), F32),
        scratch_shapes=[pltpu.VMEM((D_MODEL, D_PROJ), BF16)],
        compiler_params=_cparams(1),
        name="in_proj",
    )(x_ctx, x_dec, mod, norm_g, w_in_t)


def _time_masks(reverse):
    r = _iota((2 * CHUNK, 2 * CHUNK), 0) % CHUNK
    c = _iota((2 * CHUNK, 2 * CHUNK), 1) % CHUNK
    if reverse:
        return r < c, r <= c
    return r > c, r >= c


def _cumsum_matrix(reverse):
    r = _iota((CHUNK, CHUNK), 0)
    c = _iota((CHUNK, CHUNK), 1)
    tri = (r <= c) if reverse else (r >= c)
    return tri.astype(BF16)


def _stack_heads(x, half):
    m0 = _iota(x.shape, 1) < half
    return jnp.concatenate([jnp.where(m0, x, 0.0), jnp.where(m0, 0.0, x)], axis=0)


def _head_sums(x):
    parts = []
    for p in range(x.shape[1] // LANE):
        xp = x[:, p * LANE:(p + 1) * LANE]
        m0 = _iota(xp.shape, 1) < 64
        s0 = jnp.sum(jnp.where(m0, xp, 0.0), axis=-1, keepdims=True)
        s1 = jnp.sum(jnp.where(m0, 0.0, xp), axis=-1, keepdims=True)
        parts.append(jnp.where(m0, s0, s1))
    return parts[0] if len(parts) == 1 else jnp.concatenate(parts, axis=1)


def _rwkv_chunk_terms(insts, interleaved=()):
    c = CHUNK
    step_row = _iota((c, LANE), 0)
    step_col = _iota((c, LANE), 1) % c
    eye_w = (step_row == step_col).astype(F32)
    same_head = (_iota((LANE, LANE), 0) // 64) == (_iota((LANE, LANE), 1) // 64)
    stack_bf = lambda x: _stack_heads(x, 64).astype(BF16)
    cums = [_mm_01_lhs(_cumsum_matrix(rev), lw) for (_, lw, _, _, _, _, rev) in insts]
    pre = []
    for (r, lw, kd, a, b, v, rev), cum in zip(insts, cums):
        end = cum[0:1] if rev else cum[c - 1:c]
        inv_w = jnp.exp(-cum)
        rem_w = jnp.exp(end - cum)
        a_t = a * jnp.exp(cum - lw)
        r_t = r * jnp.exp(cum)
        bk_s = jnp.concatenate([stack_bf(b * inv_w), stack_bf(kd * inv_w)], axis=0)
        bkh = jnp.concatenate([b * rem_w, kd * rem_w], axis=0).astype(BF16)
        pre.append((a_t, r_t, bk_s, bkh, v, jnp.exp(end)))
    ms = [_dot(jnp.concatenate([a_t, r_t], axis=0).astype(BF16), bk_s, _NT) for (a_t, r_t, bk_s, _, _, _) in pre]
    mats = []
    for m, (_, _, _, _, _, _, rev) in zip(ms, insts):
        strict = (step_row < step_col) if rev else (step_row > step_col)
        incl = (step_row <= step_col) if rev else (step_row >= step_col)
        l_ab = jnp.where(strict, m[0:c, 0:LANE], 0.0)
        l_akrk = jnp.concatenate([jnp.where(strict, m[0:c, LANE:2 * LANE], 0.0),
                                  jnp.where(incl, m[c:2 * c, LANE:2 * LANE], 0.0)], axis=0).astype(BF16)
        m_rb = jnp.where(incl, m[c:2 * c, 0:LANE], 0.0).astype(BF16)
        mats.append((l_ab, l_akrk, m_rb))
    pending = list(interleaved)

    def run_interleaved():
        if pending:
            pending.pop(0)()

    lvs = [_dot(l_akrk, stack_bf(pr[4])) for (_, l_akrk, _), pr in zip(mats, pre)]
    run_interleaved()
    ps = [eye_w + l_ab for (l_ab, _, _) in mats]
    lps = [_dot(l_ab.astype(BF16), stack_bf(l_ab)) for (l_ab, _, _) in mats]
    for level in range(1, 6):
        if level < 5:
            xs = [_dot(lp.astype(BF16), jnp.concatenate([stack_bf(p), stack_bf(lp)], axis=1))
                  for lp, p in zip(lps, ps)]
            ps = [p + x[:, 0:LANE] for p, x in zip(ps, xs)]
            lps = [x[:, LANE:2 * LANE] for x in xs]
        else:
            ps = [p + _dot(lp.astype(BF16), stack_bf(p)) for lp, p in zip(lps, ps)]
        if level in (2, 4):
            run_interleaved()
    pxs = [_dot(p.astype(BF16), jnp.concatenate([stack_bf(pr[0]), stack_bf(lv[0:c])], axis=1))
           for p, pr, lv in zip(ps, pre, lvs)]
    run_interleaved()
    mzs = [_dot(mt[2], jnp.concatenate([stack_bf(px[:, 0:LANE]), stack_bf(px[:, LANE:2 * LANE])], axis=1))
           for mt, px in zip(mats, pxs)]
    ts = [_dot(px[:, 0:LANE].astype(BF16), pr[3][0:c], _TN) for px, pr in zip(pxs, pre)]
    gs = [_dot(jnp.concatenate([px[:, LANE:2 * LANE], pr[4]], axis=0).astype(BF16), pr[3], _TN)
          for px, pr in zip(pxs, pre)]
    while pending:
        run_interleaved()
    out = []
    for pr, lv, mz, t, g in zip(pre, lvs, mzs, ts, gs):
        q = pr[1] + mz[:, 0:LANE]
        y0 = mz[:, LANE:2 * LANE] + lv[c:2 * c]
        out.append((jnp.where(same_head, t, 0.0).astype(BF16), jnp.where(same_head, g, 0.0), pr[5],
                    q.astype(BF16), y0))
    return out


def _gla_chunk_terms(insts, interleaved=()):
    c = CHUNK
    step_row = _iota((c, LANE), 0)
    step_col = _iota((c, LANE), 1) % c
    same_head = (_iota((LANE, 2 * LANE), 0) // 64) == (_iota((LANE, 2 * LANE), 1) // LANE)
    cums = [_mm_01_lhs(_cumsum_matrix(rev), g) for (_, _, _, g, rev) in insts]
    pre = []
    for (q, k, v, g, rev), cum in zip(insts, cums):
        end = cum[0:1] if rev else cum[c - 1:c]
        qt = (q * jnp.exp(cum)).astype(BF16)
        k_s = _stack_heads(k * jnp.exp(-cum), 64).astype(BF16)
        kh = (k * jnp.exp(end - cum)).astype(BF16)
        a_col = jnp.broadcast_to(jnp.exp(end), (LANE, LANE)).T
        pre.append((qt, k_s, kh, v.astype(BF16), _stack_heads(v, LANE).astype(BF16), a_col))
    pending = list(interleaved)

    def run_interleaved():
        if pending:
            pending.pop(0)()

    run_interleaved()
    atts = [_dot(pr[0], pr[1], _NT) for pr in pre]
    run_interleaved()
    atts = [jnp.where((step_row <= step_col) if inst[4] else (step_row >= step_col), att, 0.0).astype(BF16)
            for att, inst in zip(atts, insts)]
    o0s = [_dot(att, pr[4]) for att, pr in zip(atts, pre)]
    run_interleaved()
    kvs = [jnp.where(same_head, _dot(pr[2], pr[3], _TN), 0.0) for pr in pre]
    while pending:
        run_interleaved()
    return [(pr[0], o0, pr[5], kv) for pr, o0, kv in zip(pre, o0s, kvs)]


def _rwkv_kernel(seq_len, nseq, is_grid, zero_init, r_ref, k_ref, v_ref, lora_ref, cwr_ref, cwk_ref, cwv_ref,
                 w2_ref, a2_ref, g2_ref, vec_ref, *rest):
    s0_ref = None if zero_init else rest[0]
    (y_ref, sout_ref, pad_ref, rs_ref, ks_ref, vs_ref, kk_ref, bonus_ref, gate_ref, yf_ref, yb_ref, st_ref, tt_ref,
     tg_ref, tw_ref, tq_ref, ty_ref) = rest[0 if zero_init else 1:][:17]
    left_ref, right_ref = rest[-2:] if is_grid else (None, None)
    n_chunks = seq_len // CHUNK
    npp = RW_PAIRS_PER_STEP
    w = npp * LANE
    pair = lambda x, p: x[:, p * LANE:(p + 1) * LANE]
    vec = vec_ref[...]
    w0 = (vec[0:1], vec[1:2])
    a0 = (vec[2:3], vec[3:4])
    k_k, k_a, r_k, ln_w, ln_b = vec[4:5], vec[5:6], vec[6:7], vec[7:8], vec[8:9]
    block_sum = _head_sums
    chains = [(d, p) for d in range(2) for p in range(npp)]
    rows_of = lambda s, c: slice((s * n_chunks + c) * CHUNK, (s * n_chunks + c + 1) * CHUNK)

    zeros = jnp.zeros((CONV_PAD, 3 * w), F32)
    for ref in (pad_ref, left_ref, right_ref) if is_grid else (pad_ref,):
        ref[0:CONV_PAD, :] = zeros
        ref[CONV_PAD + seq_len:2 * CONV_PAD + seq_len, :] = zeros
    cw = jnp.concatenate([cwr_ref[...], cwk_ref[...], cwv_ref[...]], axis=1)
    col = _iota((CHUNK, 3 * w), 0)

    def shift_body(c, carry):
        base = pl.multiple_of(CONV_PAD + c * CHUNK, CHUNK)
        win = pad_ref[pl.ds(base - 8, CHUNK + 16), :]
        left_ref[pl.ds(base, CHUNK), :] = jnp.where(col >= 1, win[7:7 + CHUNK], 0.0)
        right_ref[pl.ds(base, CHUNK), :] = jnp.where(col <= GRID_W - 2, win[9:9 + CHUNK], 0.0)
        return carry

    def conv_chunk(c):
        base = CONV_PAD + c * CHUNK
        acc = jnp.zeros((CHUNK, 3 * w), F32)
        if is_grid:
            for di in (-1, 0, 1):
                row = pl.ds(base + di * GRID_W, CHUNK)
                for dj, src in ((-1, left_ref), (0, pad_ref), (1, right_ref)):
                    tap = (di + 1) * 3 + dj + 1
                    acc = acc + src[row, :] * cw[tap:tap + 1]
        else:
            win = pad_ref[pl.ds(base - 8, CHUNK + 16), :]
            for dj in (-1, 0, 1):
                acc = acc + win[8 + dj:8 + dj + CHUNK] * cw[4 + dj:5 + dj]
        return acc[:, 0:w], acc[:, w:2 * w], acc[:, 2 * w:3 * w]

    def load_sequence(s):
        seq_rows = slice(s * seq_len, (s + 1) * seq_len)
        pad_ref[CONV_PAD:CONV_PAD + seq_len, 0:w] = r_ref[seq_rows, :]
        pad_ref[CONV_PAD:CONV_PAD + seq_len, w:2 * w] = k_ref[seq_rows, :]
        pad_ref[CONV_PAD:CONV_PAD + seq_len, 2 * w:3 * w] = v_ref[seq_rows, :]

    def conv_store(s, c):
        rows = rows_of(s, c)
        rc, kc, vc = conv_chunk(c)
        kk = kc * k_k
        rs_ref[rows, :] = rc
        ks_ref[rows, :] = kc
        vs_ref[rows, :] = vc
        kk_ref[rows, :] = kk * lax.rsqrt(block_sum(kk * kk) + EPS)
        bonus_ref[rows, :] = block_sum(rc * kc * r_k) * vc
        gate_ref[rows, :] = _mm(_sigmoid(lora_ref[rows, 2 * LANE:3 * LANE]), g2_ref[...])

    for s in range(nseq):
        for d, p in chains:
            if zero_init:
                st_ref[s, d, p] = jnp.zeros((LANE, LANE), F32)
            else:
                z = jnp.zeros((64, 64), F32)
                st_ref[s, d, p] = jnp.concatenate([jnp.concatenate([s0_ref[s, d, 2 * p], z], axis=1),
                                                   jnp.concatenate([z, s0_ref[s, d, 2 * p + 1]], axis=1)], axis=0)

    if nseq == 1:
        groups = [(0, list(range(g * TERM_UNROLL, (g + 1) * TERM_UNROLL)),
                   list(range(n_chunks - 1 - g * TERM_UNROLL, n_chunks - 1 - (g + 1) * TERM_UNROLL, -1)))
                  for g in range(n_chunks // TERM_UNROLL)]
    else:
        groups = [(s, list(range(n_chunks)), list(range(n_chunks - 1, -1, -1))) for s in range(nseq)]

    def group_terms(group, interleaved):
        s, fwd, bwd = group
        insts, where, lora_in = [], [], {}
        for d, chunks in ((0, fwd), (1, bwd)):
            for c in chunks:
                rows = rows_of(s, c)
                if c not in lora_in:
                    lora_in[c] = (jnp.tanh(lora_ref[rows, 0:LANE]).astype(BF16),
                                  lora_ref[rows, LANE:2 * LANE].astype(BF16))
                lw = -RW_DECAY_SCALE * _sigmoid(w0[d] + _dot(lora_in[c][0], w2_ref[d]))
                ag = _sigmoid(a0[d] + _dot(lora_in[c][1], a2_ref[d]))
                rc, kc, vc, kk = rs_ref[rows, :], ks_ref[rows, :], vs_ref[rows, :], kk_ref[rows, :]
                kd = kc * (1.0 + (ag - 1.0) * k_a)
                kb = kk * ag
                for p in range(npp):
                    insts.append((pair(rc, p), pair(lw, p), pair(kd, p), -pair(kk, p), pair(kb, p), pair(vc, p),
                                  d == 1))
                    where.append((d, p, s * n_chunks + c))
        for (d, p, gc), (t, g, w_end, q, y0) in zip(where, _rwkv_chunk_terms(insts, interleaved)):
            tt_ref[d, p, gc] = t
            tg_ref[d, p, gc] = g
            tw_ref[d, p, gc] = jnp.broadcast_to(w_end, (8, LANE))
            tq_ref[d, p, gc] = q
            ty_ref[d, p, gc] = y0

    def scan_step(s, chunk_of_dir):
        gcs = [s * n_chunks + chunk_of_dir[d] for d, _ in chains]
        ss = [st_ref[s, d, p] for d, p in chains]
        sb = [x.astype(BF16) for x in ss]
        ys = [_dot(tq_ref[d, p, gc], b, _NT) + ty_ref[d, p, gc] for (d, p), gc, b in zip(chains, gcs, sb)]
        sn = [x * tw_ref[d, p, gc][0:1] + _dot(b, tt_ref[d, p, gc]) + tg_ref[d, p, gc]
              for (d, p), gc, x, b in zip(chains, gcs, ss, sb)]
        for (d, p), gc, y, x in zip(chains, gcs, ys, sn):
            st_ref[s, d, p] = x
            out_ref = yf_ref if d == 0 else yb_ref
            out_ref[gc * CHUNK:(gc + 1) * CHUNK, p * LANE:(p + 1) * LANE] = y

    def group_scan(group):
        s, fwd, bwd = group
        return [functools.partial(scan_step, s, (cf, cb)) for cf, cb in zip(fwd, bwd)]

    conv_done, seq_loaded = set(), set()

    def group_conv(group):
        s, fwd, bwd = group
        todo = [c for c in sorted(set(fwd) | set(bwd)) if (s, c) not in conv_done]
        conv_done.update((s, c) for c in todo)
        thunks = []
        for i in range(0, len(todo), max(1, -(-len(todo) // TERM_UNROLL))):
            part = todo[i:i + max(1, -(-len(todo) // TERM_UNROLL))]
            need_load = s not in seq_loaded
            seq_loaded.add(s)

            def run(part=part, need_load=need_load):
                if need_load:
                    assert not is_grid or nseq == 1
                    load_sequence(s)
                for c in part:
                    conv_store(s, c)

            thunks.append(run)
        return thunks

    def merge(a, b):
        n = max(len(a), len(b))
        pick = lambda lst, i: lst[i] if i < len(lst) else (lambda: None)
        return [lambda i=i: (pick(a, i)(), pick(b, i)()) for i in range(n)]

    if is_grid:
        load_sequence(0)
        seq_loaded.add(0)
        lax.fori_loop(0, n_chunks, shift_body, 0)
    for thunk in group_conv(groups[0]):
        thunk()
    for g, group in enumerate(groups):
        scans = group_scan(groups[g - 1]) if g else []
        convs = group_conv(groups[g + 1]) if g + 1 < len(groups) else []
        group_terms(group, merge(scans, convs))
    for step in group_scan(groups[-1]):
        step()
    for s in range(nseq):
        for d, p in chains:
            x = st_ref[s, d, p]
            sout_ref[s, d, 2 * p] = x[0:64, 0:64]
            sout_ref[s, d, 2 * p + 1] = x[64:LANE, 64:LANE]

    def post_body(j, carry):
        offs = [pl.multiple_of((j * TERM_UNROLL + u) * CHUNK, CHUNK) for u in range(TERM_UNROLL)]
        ys = [yf_ref[pl.ds(off, CHUNK), :] + yb_ref[pl.ds(off, CHUNK), :] for off in offs]
        mus = [block_sum(y) * (1.0 / 64) for y in ys]
        dlts = [y - mu for y, mu in zip(ys, mus)]
        vrs = [block_sum(dlt * dlt) * (1.0 / 64) for dlt in dlts]
        for off, dlt, var in zip(offs, dlts, vrs):
            yn = dlt * lax.rsqrt(var + RW_LN_EPS) * ln_w + ln_b
            y_ref[pl.ds(off, CHUNK), :] = (yn + bonus_ref[pl.ds(off, CHUNK), :]) * gate_ref[pl.ds(off, CHUNK), :]
        return carry

    lax.fori_loop(0, nseq * n_chunks // TERM_UNROLL, post_body, 0)


def _rwkv_mixer(proj, first_seq, n_seq, seq_len, is_grid, prm, s0):
    n_pairs = RW_W // LANE
    n_chunks = seq_len // CHUNK
    npp = RW_PAIRS_PER_STEP
    w = npp * LANE
    nseq = max(1, RW_GROUPS * TERM_UNROLL // n_chunks)
    assert n_chunks % TERM_UNROLL == 0 and n_pairs % npp == 0
    assert (n_chunks == TERM_UNROLL or nseq == 1) and n_seq % nseq == 0 and first_seq % nseq == 0
    rows = nseq * seq_len
    total_chunks = nseq * n_chunks
    first = first_seq // nseq
    col = lambda cb: (lambda b, p: (b + first, cb // npp + p))
    par = lambda cb: (lambda b, p: (0, cb // npp + p))
    state_spec = pl.BlockSpec((nseq, 2, 2 * npp, 64, 64), lambda b, p: (b, 0, p, 0, 0))
    kernel = functools.partial(_rwkv_kernel, seq_len, nseq, is_grid, s0 is None)
    y, s_out = pl.pallas_call(
        kernel,
        grid=(n_seq // nseq, n_pairs // npp),
        in_specs=[pl.BlockSpec((rows, w), col(CB_R)),
                  pl.BlockSpec((rows, w), col(CB_K)),
                  pl.BlockSpec((rows, w), col(CB_V)),
                  pl.BlockSpec((rows, 3 * LANE), lambda b, p: (b + first, CB_LORA // 3)),
                  pl.BlockSpec((9, w), par(CB_R)),
                  pl.BlockSpec((9, w), par(CB_K)),
                  pl.BlockSpec((9, w), par(CB_V)),
                  pl.BlockSpec((2, LANE, w), lambda b, p: (0, 0, p)),
                  pl.BlockSpec((2, LANE, w), lambda b, p: (0, 0, p)),
                  pl.BlockSpec((LANE, w), lambda b, p: (0, p)),
                  pl.BlockSpec((16, w), lambda b, p: (0, p))] + ([] if s0 is None else [state_spec]),
        out_specs=[pl.BlockSpec((rows, w), lambda b, p: (b, p)), state_spec],
        out_shape=[jax.ShapeDtypeStruct((n_seq * seq_len, RW_W), F32),
                   jax.ShapeDtypeStruct((n_seq, 2, 2 * n_pairs, 64, 64), F32)],
        scratch_shapes=[pltpu.VMEM((seq_len + 2 * CONV_PAD, 3 * w), F32)]
                       + [pltpu.VMEM((rows, w), F32)] * 8
                       + [pltpu.VMEM((nseq, 2, npp, LANE, LANE), F32),
                          pltpu.VMEM((2, npp, total_chunks, LANE, LANE), BF16),
                          pltpu.VMEM((2, npp, total_chunks, LANE, LANE), F32),
                          pltpu.VMEM((2, npp, total_chunks, 8, LANE), F32),
                          pltpu.VMEM((2, npp, total_chunks, CHUNK, LANE), BF16),
                          pltpu.VMEM((2, npp, total_chunks, CHUNK, LANE), F32)]
                       + ([pltpu.VMEM((seq_len + 2 * CONV_PAD, 3 * w), F32)] * 2 if is_grid else []),
        compiler_params=_cparams(2),
        name="rwkv_mixer",
    )(proj, proj, proj, proj, prm['conv'], prm['conv'], prm['conv'], prm['w2p'], prm['a2p'], prm['g2'],
      prm['vec'], *([] if s0 is None else [s0]))
    return y, s_out


def _gla_kernel(seq_len, nseq, zero_init, q_ref, k_ref, v_ref, og_ref, lgk_ref, gk2_ref, gvec_ref, *rest):
    s0_ref = None if zero_init else rest[0]
    y_ref, sout_ref, of_ref, ob_ref, st_ref, tq_ref, to_ref, ta_ref, tkv_ref = rest[0 if zero_init else 1:]
    n_chunks = seq_len // CHUNK
    npp = GLA_PAIRS
    gvec = gvec_ref[...]
    chains = [(d, p) for d in range(2) for p in range(npp)]

    for s in range(nseq):
        for d, p in chains:
            if zero_init:
                st_ref[s, d, p] = jnp.zeros((LANE, 2 * LANE), F32)
            else:
                z = jnp.zeros((64, LANE), F32)
                st_ref[s, d, p] = jnp.concatenate([jnp.concatenate([s0_ref[s, d, 2 * p], z], axis=1),
                                                   jnp.concatenate([z, s0_ref[s, d, 2 * p + 1]], axis=1)], axis=0)

    if nseq == 1:
        groups = [(0, list(range(g * TERM_UNROLL, (g + 1) * TERM_UNROLL)),
                   list(range(n_chunks - 1 - g * TERM_UNROLL, n_chunks - 1 - (g + 1) * TERM_UNROLL, -1)))
                  for g in range(n_chunks // TERM_UNROLL)]
    else:
        groups = [(s, list(range(n_chunks)), list(range(n_chunks - 1, -1, -1))) for s in range(nseq)]

    def group_terms(group, interleaved):
        s, fwd, bwd = group
        insts, where, lgk_in = [], [], {}
        for d, chunks in ((0, fwd), (1, bwd)):
            for c in chunks:
                gc = s * n_chunks + c
                rows = slice(gc * CHUNK, (gc + 1) * CHUNK)
                if c not in lgk_in:
                    lgk_in[c] = lgk_ref[rows, :].astype(BF16)
                x = _dot(lgk_in[c], gk2_ref[d]) + gvec[d:d + 1, 0:GLA_QK_W]
                g = _log_sigmoid(x) * (1.0 / GLA_GATE_NORM)
                qc = q_ref[rows, :] * GLA_Q_SCALE
                kc = k_ref[rows, :]
                vc = v_ref[rows, :]
                for p in range(npp):
                    qk = slice(p * LANE, (p + 1) * LANE)
                    insts.append((qc[:, qk], kc[:, qk], vc[:, 2 * p * LANE:2 * (p + 1) * LANE], g[:, qk], d == 1))
                    where.append((d, p, gc))
        for (d, p, gc), (qt, o0, a_col, kv) in zip(where, _gla_chunk_terms(insts, interleaved)):
            tq_ref[d, p, gc] = qt
            to_ref[d, p, gc] = o0
            ta_ref[d, p, gc] = a_col
            tkv_ref[d, p, gc] = kv

    def scan_step(s, chunk_of_dir):
        gcs = [s * n_chunks + chunk_of_dir[d] for d, _ in chains]
        ss = [st_ref[s, d, p] for d, p in chains]
        os_ = [_dot(tq_ref[d, p, gc], x.astype(BF16)) + to_ref[d, p, gc] for (d, p), gc, x in zip(chains, gcs, ss)]
        for (d, p), gc, x, o in zip(chains, gcs, ss, os_):
            a_col = ta_ref[d, p, gc]
            st_ref[s, d, p] = x * jnp.concatenate([a_col, a_col], axis=1) + tkv_ref[d, p, gc]
            out_ref = of_ref if d == 0 else ob_ref
            out_ref[gc * CHUNK:(gc + 1) * CHUNK, 2 * p * LANE:2 * (p + 1) * LANE] = o

    def group_scan(group):
        s, fwd, bwd = group
        return [functools.partial(scan_step, s, (cf, cb)) for cf, cb in zip(fwd, bwd)]

    for g, group in enumerate(groups):
        group_terms(group, group_scan(groups[g - 1]) if g else [])
    for step in group_scan(groups[-1]):
        step()
    for s in range(nseq):
        for d, p in chains:
            x = st_ref[s, d, p]
            sout_ref[s, d, 2 * p] = x[0:64, 0:LANE]
            sout_ref[s, d, 2 * p + 1] = x[64:LANE, LANE:2 * LANE]

    def post_body(c, carry):
        off = pl.multiple_of(c * CHUNK, CHUNK)
        for h in range(2 * npp):
            hs = slice(h * LANE, (h + 1) * LANE)
            o = of_ref[pl.ds(off, CHUNK), hs] + ob_ref[pl.ds(off, CHUNK), hs]
            gate = _silu(og_ref[pl.ds(off, CHUNK), hs])
            y_ref[pl.ds(off, CHUNK), hs] = _rmsnorm_rows(o) * gvec[2:3, hs] * gate
        return carry

    lax.fori_loop(0, nseq * n_chunks, post_body, 0)


def _gla_mixer(proj, first_seq, n_seq, seq_len, prm, s0):
    npp = GLA_PAIRS
    n_heads = 2 * npp
    n_chunks = seq_len // CHUNK
    nseq = max(1, RW_GROUPS * TERM_UNROLL // n_chunks)
    assert n_chunks % TERM_UNROLL == 0
    assert (n_chunks == TERM_UNROLL or nseq == 1) and n_seq % nseq == 0 and first_seq % nseq == 0
    rows = nseq * seq_len
    total_chunks = nseq * n_chunks
    first = first_seq // nseq
    state_spec = pl.BlockSpec((nseq, 2, n_heads, 64, LANE), lambda b: (b, 0, 0, 0, 0))
    kernel = functools.partial(_gla_kernel, seq_len, nseq, s0 is None)
    y, s_out = pl.pallas_call(
        kernel,
        grid=(n_seq // nseq,),
        in_specs=[pl.BlockSpec((rows, GLA_QK_W), lambda b: (b + first, CB_GQ * LANE // GLA_QK_W)),
                  pl.BlockSpec((rows, GLA_QK_W), lambda b: (b + first, CB_GK * LANE // GLA_QK_W)),
                  pl.BlockSpec((rows, GLA_V_W), lambda b: (b + first, CB_GV * LANE // GLA_V_W)),
                  pl.BlockSpec((rows, GLA_V_W), lambda b: (b + first, CB_OG * LANE // GLA_V_W)),
                  pl.BlockSpec((rows, LANE), lambda b: (b + first, CB_LGK)),
                  pl.BlockSpec((2, LANE, GLA_QK_W), lambda b: (0, 0, 0)),
                  pl.BlockSpec((8, GLA_V_W), lambda b: (0, 0))] + ([] if s0 is None else [state_spec]),
        out_specs=[pl.BlockSpec((rows, GLA_V_W), lambda b: (b, 0)), state_spec],
        out_shape=[jax.ShapeDtypeStruct((n_seq * seq_len, GLA_V_W), F32),
                   jax.ShapeDtypeStruct((n_seq, 2, n_heads, 64, LANE), F32)],
        scratch_shapes=[pltpu.VMEM((rows, GLA_V_W), F32)] * 2
                       + [pltpu.VMEM((nseq, 2, npp, LANE, 2 * LANE), F32),
                          pltpu.VMEM((2, npp, total_chunks, CHUNK, LANE), BF16),
                          pltpu.VMEM((2, npp, total_chunks, CHUNK, 2 * LANE), F32),
                          pltpu.VMEM((2, npp, total_chunks, LANE, LANE), F32),
                          pltpu.VMEM((2, npp, total_chunks, LANE, 2 * LANE), F32)],
        compiler_params=_cparams(1),
        name="gla_mixer",
    )(proj, proj, proj, proj, proj, prm['gk2p'], prm['gvec'], *([] if s0 is None else [s0]))
    return y, s_out


OUT_TM = 512
ROUTE_NEG = -1e30
LANE_GROUP0 = N_EXPERTS


def _route(logits):
    lane = _iota(logits.shape, 1)
    lane_f = lane.astype(F32)
    big = float(LANE)
    is_g = (lane >= LANE_GROUP0) & (lane < LANE_GROUP0 + 4)
    gmax = jnp.max(jnp.where(is_g, logits, ROUTE_NEG), axis=-1, keepdims=True)
    gidx = jnp.min(jnp.where(is_g & (logits == gmax), lane_f, big), axis=-1, keepdims=True) - LANE_GROUP0
    gsum = jnp.sum(jnp.where(is_g, jnp.exp(jnp.minimum(logits - gmax, 0.0)), 0.0), axis=-1, keepdims=True)
    g_w = 1.0 / gsum
    in_grp = (lane < N_EXPERTS) & ((lane // 4).astype(F32) == gidx)
    m1 = jnp.max(jnp.where(in_grp, logits, ROUTE_NEG), axis=-1, keepdims=True)
    i1 = jnp.min(jnp.where(in_grp & (logits == m1), lane_f, big), axis=-1, keepdims=True)
    rest = in_grp & (lane_f != i1)
    m2 = jnp.max(jnp.where(rest, logits, ROUTE_NEG), axis=-1, keepdims=True)
    i2 = jnp.min(jnp.where(rest & (logits == m2), lane_f, big), axis=-1, keepdims=True)
    t = jnp.exp(m2 - m1)
    w1 = g_w / (1.0 + t)
    return jnp.where(lane_f == i1, w1, 0.0) + jnp.where(lane_f == i2, w1 * t, 0.0)


def _outproj_kernel(tiles, yrc_ref, yrd_ref, ygc_ref, ygd_ref, xc_ref, xd_ref, mod_ref, wo_ref, g_ref, wr_ref,
                    br_ref, x1_ref, h2_ref, cmb_ref):
    def run(yr_ref, yg_ref, x_ref):
        m = mod_ref[0]
        mix = _mm(yr_ref[...], wo_ref[0:RW_W, :]) + _mm(yg_ref[...], wo_ref[RW_W:RW_W + GLA_V_W, :])
        x1 = x_ref[...] + m[2:3] * mix
        h2 = _rmsnorm_rows(x1) * g_ref[...] * (1.0 + m[4:5]) + m[3:4]
        x1_ref[...] = x1
        h2_ref[...] = h2.astype(BF16)
        h_hi, h_lo = _split2(h2)
        both = _dot(h_hi, wr_ref[...])
        logits = both[:, 0:LANE] + both[:, LANE:2 * LANE] + _dot(h_lo, wr_ref[:, 0:LANE])
        cmb_ref[...] = _route(logits + br_ref[...])

    tiles.by_pass(functools.partial(run, yrc_ref, ygc_ref, xc_ref), functools.partial(run, yrd_ref, ygd_ref, xd_ref))


def _out_projection(tiles, y_rw, y_gla, x, mod, w_out, norm_g, w_route, b_route):
    n = x[0].shape[0] + x[1].shape[0]
    full = lambda a: pl.BlockSpec(a.shape, lambda i: (0,) * a.ndim)
    return pl.pallas_call(
        functools.partial(_outproj_kernel, tiles),
        grid=(tiles.n_ctx + tiles.n_dec,),
        in_specs=[*tiles.specs(RW_W), *tiles.specs(GLA_V_W), *tiles.specs(D_MODEL), tiles.mod_spec(),
                  full(w_out), full(norm_g), full(w_route), full(b_route)],
        out_specs=[tiles.merged(D_MODEL), tiles.merged(D_MODEL), tiles.merged(LANE)],
        out_shape=[jax.ShapeDtypeStruct((n, D_MODEL), F32), jax.ShapeDtypeStruct((n, D_MODEL), BF16),
                   jax.ShapeDtypeStruct((n, LANE), F32)],
        compiler_params=_cparams(1),
        name="out_proj_router",
    )(*y_rw, *y_gla, *x, mod, w_out, norm_g, w_route, b_route)


MOE_TM = 512
MOE_RB = 128
MOE_INTERLEAVE = 4
SLOT_ALIGN = 16
MOE_SLOTS = 2 * MOE_TM + N_EXPERTS * SLOT_ALIGN + MOE_RB


def _stage_expert_weights(srcs_hbm, dst_refs, stage_refs, sems):
    def copies(e):
        return [pltpu.make_async_copy(src.at[e], stage.at[e % 2], sem.at[e % 2])
                for src, stage, sem in zip(srcs_hbm, stage_refs, sems)]

    for c in copies(0):
        c.start()
    for e in range(N_EXPERTS):
        if e + 1 < N_EXPERTS:
            for c in copies(e + 1):
                c.start()
        for c, dst, stage in zip(copies(e), dst_refs, stage_refs):
            c.wait()
            dst[e] = stage[e % 2].astype(BF16)


def _moe_kernel(tiles, h2_ref, cmb_ref, x1_ref, mod_ref, w1_hbm, w3_hbm, w2_hbm, fg_ref, yc_ref, yd_ref,
                xs_ref, ys_ref, w1_ref, w3_ref, w2_ref, stage1_ref, stage3_ref, stage2_ref, sem1, sem3, sem2):
    @pl.when(pl.program_id(0) == 0)
    def _():
        _stage_expert_weights((w1_hbm, w3_hbm, w2_hbm), (w1_ref, w3_ref, w2_ref),
                              (stage1_ref, stage3_ref, stage2_ref), (sem1, sem3, sem2))

    cmb = cmb_ref[...]
    lane = _iota(cmb.shape, 1).astype(F32)
    sel = cmb > 0.0
    sel01 = jnp.where(sel, 1.0, 0.0).astype(BF16)
    before = (_iota((MOE_TM, MOE_TM), 0) > _iota((MOE_TM, MOE_TM), 1)).astype(BF16)
    pos = _dot(before, sel01)
    cnt = pos[MOE_TM - 1:MOE_TM] + sel01[MOE_TM - 1:MOE_TM].astype(F32)
    seg = jnp.floor((cnt + (SLOT_ALIGN - 1)) * (1.0 / SLOT_ALIGN))
    lower_experts = (_iota((LANE, LANE), 0) < _iota((LANE, LANE), 1)).astype(BF16)
    start = _dot(jnp.broadcast_to(seg, (8, LANE)).astype(BF16), lower_experts)[0:1] * SLOT_ALIGN
    n_blk = jnp.floor((cnt + (MOE_RB - 1)) * (1.0 / MOE_RB)).astype(jnp.int32)
    start_i = start.astype(jnp.int32)
    cnt_i = cnt.astype(jnp.int32)
    slot = start + pos
    e_a = jnp.min(jnp.where(sel, lane, float(LANE)), axis=-1, keepdims=True)
    e_b = jnp.max(jnp.where(sel, lane, -1.0), axis=-1, keepdims=True)
    pick = lambda e, x: jnp.sum(jnp.where(lane == e, x, 0.0), axis=-1, keepdims=True)
    slot_a, w_a = pick(e_a, slot), pick(e_a, cmb)
    slot_b = jnp.where(e_b != e_a, pick(e_b, slot), -1.0)
    w_b = pick(e_b, cmb)

    slots_t = jnp.where(lane == 0.0, slot_a, jnp.where(lane == 1.0, slot_b, -1.0)).T
    row_slot = _iota((MOE_SLOTS, MOE_TM), 0).astype(F32)
    gather = jnp.where((row_slot == slots_t[0:1]) | (row_slot == slots_t[1:2]), 1.0, 0.0).astype(BF16)
    xs_ref[...] = _dot(gather, h2_ref[...]).astype(BF16)
    ys_ref[...] = jnp.zeros_like(ys_ref)

    row_in_blk = _iota((MOE_RB, D_MODEL), 0)

    def expert_blocks(experts, r0s, ends):
        xbs = [xs_ref[pl.ds(r0, MOE_RB), :] for r0 in r0s]
        gates = [_dot(xb, w3_ref[e]) for xb, e in zip(xbs, experts)]
        ups = [_dot(xb, w1_ref[e]) for xb, e in zip(xbs, experts)]
        acts = [(_silu(g) * u).astype(BF16) for g, u in zip(gates, ups)]
        outs = [_dot(a, w2_ref[e]) for a, e in zip(acts, experts)]
        for r0, end, out in zip(r0s, ends, outs):
            keep = row_in_blk + r0 >= end
            ys_ref[pl.ds(r0, MOE_RB), :] = jnp.where(keep, ys_ref[pl.ds(r0, MOE_RB), :], out.astype(BF16))

    seg_start = [pl.multiple_of(start_i[0, e], SLOT_ALIGN) for e in range(N_EXPERTS)]
    seg_end = [seg_start[e] + cnt_i[0, e] for e in range(N_EXPERTS)]
    for e0 in range(0, N_EXPERTS, MOE_INTERLEAVE):
        es = list(range(e0, e0 + MOE_INTERLEAVE))
        expert_blocks(es, [seg_start[e] for e in es], [seg_end[e] for e in es])
    for e in range(N_EXPERTS):
        def extra_block(b, carry, e=e):
            expert_blocks([e], [pl.multiple_of(seg_start[e] + b * MOE_RB, SLOT_ALIGN)], [seg_end[e]])
            return carry

        lax.fori_loop(1, n_blk[0, e], extra_block, 0)

    col_slot = _iota((MOE_TM, MOE_SLOTS), 1).astype(F32)
    scatter = (jnp.where(col_slot == slot_a, w_a, 0.0) + jnp.where(col_slot == slot_b, w_b, 0.0)).astype(BF16)
    x2 = x1_ref[...] + mod_ref[0][5:6] * _dot(scatter, ys_ref[...])
    y = _rmsnorm_rows(x2) * fg_ref[...]

    def write(y_ref):
        y_ref[...] = y

    tiles.by_pass(functools.partial(write, yc_ref), functools.partial(write, yd_ref))


def _moe(tiles, h2, cmb, x1, mod, w1, w3, w2, final_g):
    assert tiles.tm == MOE_TM
    hbm = pl.BlockSpec(memory_space=pl.ANY)
    out_ctx, out_dec = tiles.specs(D_MODEL)
    return pl.pallas_call(
        functools.partial(_moe_kernel, tiles),
        grid=(tiles.n_ctx + tiles.n_dec,),
        in_specs=[tiles.merged(D_MODEL), tiles.merged(LANE), tiles.merged(D_MODEL), tiles.mod_spec(),
                  hbm, hbm, hbm, pl.BlockSpec((1, D_MODEL), lambda i: (0, 0))],
        out_specs=[out_ctx, out_dec],
        out_shape=[jax.ShapeDtypeStruct((tiles.n_ctx * MOE_TM, D_MODEL), F32),
                   jax.ShapeDtypeStruct((tiles.n_dec * MOE_TM, D_MODEL), F32)],
        scratch_shapes=[pltpu.VMEM((MOE_SLOTS, D_MODEL), BF16), pltpu.VMEM((MOE_SLOTS, D_MODEL), BF16),
                        pltpu.VMEM((N_EXPERTS, D_MODEL, D_EXPERT), BF16),
                        pltpu.VMEM((N_EXPERTS, D_MODEL, D_EXPERT), BF16),
                        pltpu.VMEM((N_EXPERTS, D_EXPERT, D_MODEL), BF16),
                        pltpu.VMEM((2, D_MODEL, D_EXPERT), F32), pltpu.VMEM((2, D_MODEL, D_EXPERT), F32),
                        pltpu.VMEM((2, D_EXPERT, D_MODEL), F32)] + [pltpu.SemaphoreType.DMA((2,))] * 3,
        compiler_params=_cparams(1),
        name="moe_experts",
    )(h2, cmb, x1, mod, w1, w3, w2, final_g)


def _pad_rows(x, rows):
    return jnp.pad(x, ((0, rows - x.shape[0]),) + ((0, 0),) * (x.ndim - 1))


def _pack_params(l, w_in, rw_conv, rw_w0, rw_w2, rw_a0, rw_a2, rw_g2, rw_k_k, rw_k_a, rw_r_k, rw_ln_w, rw_ln_b,
                 gla_gk2, gla_gk_b, gla_norm_g, moe_w_group, moe_b_group, moe_w_expert, moe_b_expert):
    wi = w_in[l]
    z = lambda n: jnp.zeros((D_MODEL, n), F32)
    w_in_t = jnp.swapaxes(wi, 0, 1)
    z64 = jnp.zeros((64, RW_W), F32)
    w2p = jnp.stack([jnp.concatenate([rw_w2[l, 0], z64], 0), jnp.concatenate([z64, rw_w2[l, 1]], 0)])
    a2p = jnp.stack([jnp.concatenate([rw_a2[l, 0], z64], 0), jnp.concatenate([z64, rw_a2[l, 1]], 0)])
    vec = _pad_rows(jnp.stack([rw_w0[l, 0], rw_w0[l, 1], rw_a0[l, 0], rw_a0[l, 1], rw_k_k[l], rw_k_a[l],
                               rw_r_k[l].reshape(RW_W), rw_ln_w[l], rw_ln_b[l]]), 16)
    rw = {'conv': rw_conv[l].reshape(9, 3 * RW_W), 'w2p': w2p.astype(BF16), 'a2p': a2p.astype(BF16),
          'g2': rw_g2[l].astype(BF16), 'vec': vec}
    gk2p = jnp.stack([_pad_rows(gla_gk2[l, 0], LANE),
                      _pad_rows(jnp.concatenate([jnp.zeros((16, GLA_QK_W), F32), gla_gk2[l, 1]], 0), LANE)])
    gk_b = jnp.pad(gla_gk_b[l], ((0, 0), (0, GLA_V_W - GLA_QK_W)))
    gvec = _pad_rows(jnp.concatenate([gk_b, jnp.tile(gla_norm_g[l], GLA_V_W // LANE)[None]], axis=0), 8)
    gla = {'gk2p': gk2p.astype(BF16), 'gvec': gvec}
    w_route = jnp.concatenate(_split2(jnp.concatenate([moe_w_expert[l], moe_w_group[l], z(LANE - N_EXPERTS - 4)],
                                                      axis=1)), axis=1)
    b_route = jnp.concatenate([moe_b_expert[l], moe_b_group[l], jnp.zeros((LANE - N_EXPERTS - 4,), F32)])[None]
    return w_in_t, rw, gla, w_route, b_route


def kernel(x_prompt, x_sample, state_rwkv, state_gla, c, c_ctx, norm1_g, norm2_g, w_ada, b_ada, w_in, w_out,
           rw_conv, rw_w0, rw_w2, rw_a0, rw_a2, rw_g2, rw_k_k, rw_k_a, rw_r_k, rw_ln_w, rw_ln_b,
           gla_gk2, gla_gk_b, gla_norm_g, moe_w_group, moe_b_group, moe_w_expert, moe_b_expert,
           moe_w1, moe_w3, moe_w2, final_g):
    depth = w_in.shape[0]
    assert depth == 1, "the packed layout below handles the single-layer trunk of this problem"
    l = 0
    n_dec = x_sample.shape[0]
    ctx_row = n_dec
    cond8 = _pad_rows(jnp.concatenate([c, c_ctx[None]], axis=0), 8)
    mod = _modulation(cond8, w_ada[l], b_ada[l][None]).reshape(8, N_MOD, D_MODEL)
    pk = _pack_params(l, w_in, rw_conv, rw_w0, rw_w2, rw_a0, rw_a2, rw_g2, rw_k_k, rw_k_a, rw_r_k, rw_ln_w,
                      rw_ln_b, gla_gk2, gla_gk_b, gla_norm_g, moe_w_group, moe_b_group, moe_w_expert,
                      moe_b_expert)
    w_in_t, rw, gla, w_route, b_route = pk

    n_ctx, ctx_len, _ = x_prompt.shape
    dec_len = x_sample.shape[1]
    x_ctx = x_prompt.reshape(n_ctx * ctx_len, D_MODEL)
    x_dec = x_sample.reshape(n_dec * dec_len, D_MODEL)
    assert (n_ctx * ctx_len) % dec_len == 0, "denoising sequences must start on a dec_len row block of proj"
    first_dec = n_ctx * ctx_len // dec_len
    tiles = _Tiles(n_ctx * ctx_len, n_dec * dec_len, dec_len, ctx_row, PROJ_TM)
    assert PROJ_TM == OUT_TM == MOE_TM

    proj = _in_projection(tiles, x_ctx, x_dec, mod, norm1_g[l][None], w_in_t)
    y_rw_c, s_rw = _rwkv_mixer(proj, 0, n_ctx, ctx_len, False, rw, None)
    y_gla_c, s_gla = _gla_mixer(proj, 0, n_ctx, ctx_len, gla, None)
    y_rw_d, _ = _rwkv_mixer(proj, first_dec, n_dec, dec_len, True, rw, state_rwkv[:, l])
    y_gla_d, _ = _gla_mixer(proj, first_dec, n_dec, dec_len, gla, state_gla[:, l])
    x1, h2, cmb = _out_projection(tiles, (y_rw_c, y_rw_d), (y_gla_c, y_gla_d), (x_ctx, x_dec), mod,
                                  w_out[l].astype(BF16), norm2_g[l][None], w_route, b_route)
    y_ctx, y_dec = _moe(tiles, h2, cmb, x1, mod, moe_w1[l], moe_w3[l], moe_w2[l], final_g[None])
    return (y_ctx.reshape(x_prompt.shape), y_dec.reshape(x_sample.shape), s_rw[:, None], s_gla[:, None])
```

```python
import functools

import jax
import jax.numpy as jnp
from jax import lax
from jax.experimental import pallas as pl
from jax.experimental.pallas import tpu as pltpu

F32 = jnp.float32
BF16 = jnp.bfloat16

D_MODEL = 1024
RW_W = 512
GLA_V_W = 512
GLA_QK_W = 256
N_EXPERTS = 16
D_EXPERT = 256
N_MOD = 6
EPS = 1e-6
RW_LN_EPS = 64e-5
RW_DECAY_SCALE = 0.606531
GLA_GATE_NORM = 16.0
GLA_Q_SCALE = 64 ** -0.5
GRID_W = 64

LANE = 128
CHUNK = 64
CONV_PAD = 128
TERM_UNROLL = 4
RW_PAIRS_PER_STEP = 2
GLA_PAIRS = 2
RW_GROUPS = 4
D_PROJ = 28 * LANE
VMEM_LIMIT = 56 * 1024 * 1024

CB_R, CB_K, CB_V, CB_LORA, CB_LGK, CB_GQ, CB_GK, CB_GV, CB_OG = 0, 4, 8, 12, 15, 16, 18, 20, 24

_NN = (((1,), (0,)), ((), ()))
_NT = (((1,), (1,)), ((), ()))
_TN = (((0,), (0,)), ((), ()))


def _dot(a, b, dims=_NN):
    return lax.dot_general(a, b, dims, preferred_element_type=F32)


def _mm(a, b, dims=_NN):
    return _dot(a.astype(BF16), b.astype(BF16), dims)


def _split2(x):
    hi = x.astype(BF16)
    lo = (x - hi.astype(F32)).astype(BF16)
    return hi, lo


def _mm3(a, b, dims=_NN):
    ah, al = _split2(a)
    bh, bl = _split2(b)
    return _dot(ah, bh, dims) + _dot(ah, bl, dims) + _dot(al, bh, dims)


def _mm_01_lhs(a01, b, dims=_NN):
    n = b.shape[1]
    both = _dot(a01, jnp.concatenate(_split2(b), axis=1), dims)
    return both[:, 0:n] + both[:, n:2 * n]


def _sigmoid(x):
    return 0.5 * jnp.tanh(0.5 * x) + 0.5


def _silu(x):
    return x * _sigmoid(x)


def _log_sigmoid(x):
    return jnp.minimum(x, 0.0) - jnp.log(1.0 + jnp.exp(-jnp.abs(x)))


def _iota(shape, dim):
    return lax.broadcasted_iota(jnp.int32, shape, dim)


def _cparams(n_axes):
    return pltpu.CompilerParams(dimension_semantics=("arbitrary",) * n_axes, vmem_limit_bytes=VMEM_LIMIT)


MOD_TN = 768


def _mod_kernel(c_ref, w_ref, b_ref, o_ref):
    o_ref[...] = _mm3(_silu(c_ref[...]), w_ref[...]) + b_ref[...]


def _modulation(cond8, w_ada, b_ada):
    n = w_ada.shape[1]
    return pl.pallas_call(
        _mod_kernel,
        grid=(n // MOD_TN,),
        in_specs=[pl.BlockSpec((8, D_MODEL), lambda j: (0, 0)),
                  pl.BlockSpec((D_MODEL, MOD_TN), lambda j: (0, j)),
                  pl.BlockSpec((1, MOD_TN), lambda j: (0, j))],
        out_specs=pl.BlockSpec((8, MOD_TN), lambda j: (0, j)),
        out_shape=jax.ShapeDtypeStruct((8, n), F32),
        compiler_params=_cparams(1),
        name="adaln_mod",
    )(cond8, w_ada, b_ada)


PROJ_TM = 512
D_IN = 3488
N_LGK = 32


def _rmsnorm_rows(x):
    return x * lax.rsqrt(jnp.mean(x * x, axis=-1, keepdims=True) + EPS)


class _Tiles:
    def __init__(self, n_ctx_tokens, n_dec_tokens, dec_seq_len, ctx_row, tm):
        self.tm = tm
        self.n_ctx = n_ctx_tokens // tm
        self.n_dec = n_dec_tokens // tm
        self.per_seq = dec_seq_len // tm
        self.ctx_row = ctx_row

    def specs(self, width):
        last_ctx = self.n_ctx - 1
        n_ctx = self.n_ctx
        return (pl.BlockSpec((self.tm, width), lambda i: (jnp.minimum(i, last_ctx), 0)),
                pl.BlockSpec((self.tm, width), lambda i: (jnp.maximum(i - n_ctx, 0), 0)))

    def merged(self, width):
        return pl.BlockSpec((self.tm, width), lambda i: (i, 0))

    def mod_spec(self):
        n_ctx, per_seq, ctx_row = self.n_ctx, self.per_seq, self.ctx_row
        return pl.BlockSpec((1, N_MOD, D_MODEL),
                            lambda i: (jnp.where(i < n_ctx, ctx_row, (i - n_ctx) // per_seq), 0, 0))

    def by_pass(self, run_ctx, run_dec):
        i = pl.program_id(0)
        pl.when(i < self.n_ctx)(run_ctx)
        pl.when(i >= self.n_ctx)(run_dec)


def _inproj_kernel(tiles, xc_ref, xd_ref, mod_ref, g_ref, wt_ref, o_ref, w_ref):
    @pl.when(pl.program_id(0) == 0)
    def _():
        for j in range(D_PROJ // LANE):
            if j == CB_LGK:
                blk = jnp.concatenate([wt_ref[D_IN - N_LGK:D_IN, :], jnp.zeros((LANE - N_LGK, D_MODEL), F32)], axis=0)
            else:
                src = j if j < CB_LGK else j - 1
                blk = wt_ref[src * LANE:(src + 1) * LANE, :]
            w_ref[:, j * LANE:(j + 1) * LANE] = blk.T.astype(BF16)

    def run(x_ref):
        m = mod_ref[0]
        h = _rmsnorm_rows(x_ref[...]) * g_ref[...] * (1.0 + m[1:2]) + m[0:1]
        o_ref[...] = _mm(h, w_ref[...])

    tiles.by_pass(functools.partial(run, xc_ref), functools.partial(run, xd_ref))


def _in_projection(tiles, x_ctx, x_dec, mod, norm_g, w_in_t):
    full = lambda a: pl.BlockSpec(a.shape, lambda i: (0,) * a.ndim)
    return pl.pallas_call(
        functools.partial(_inproj_kernel, tiles),
        grid=(tiles.n_ctx + tiles.n_dec,),
        in_specs=[*tiles.specs(D_MODEL), tiles.mod_spec(), full(norm_g),
                  pl.BlockSpec(w_in_t.shape, lambda i: (0, 0), pipeline_mode=pl.Buffered(1))],
        out_specs=tiles.merged(D_PROJ),
        out_shape=jax.ShapeDtypeStruct((x_ctx.shape[0] + x_dec.shape[0], D_PROJ), F32),
        scratch_shapes=[pltpu.VMEM((D_MODEL, D_PROJ), BF16)],
        compiler_params=_cparams(1),
        name="in_proj",
    )(x_ctx, x_dec, mod, norm_g, w_in_t)


def _time_masks(reverse):
    r = _iota((2 * CHUNK, 2 * CHUNK), 0) % CHUNK
    c = _iota((2 * CHUNK, 2 * CHUNK), 1) % CHUNK
    if reverse:
        return r < c, r <= c
    return r > c, r >= c


def _cumsum_matrix(reverse):
    r = _iota((CHUNK, CHUNK), 0)
    c = _iota((CHUNK, CHUNK), 1)
    tri = (r <= c) if reverse else (r >= c)
    return tri.astype(BF16)


def _stack_heads(x, half):
    m0 = _iota(x.shape, 1) < half
    return jnp.concatenate([jnp.where(m0, x, 0.0), jnp.where(m0, 0.0, x)], axis=0)


def _head_sums(x):
    parts = []
    for p in range(x.shape[1] // LANE):
        xp = x[:, p * LANE:(p + 1) * LANE]
        m0 = _iota(xp.shape, 1) < 64
        s0 = jnp.sum(jnp.where(m0, xp, 0.0), axis=-1, keepdims=True)
        s1 = jnp.sum(jnp.where(m0, 0.0, xp), axis=-1, keepdims=True)
        parts.append(jnp.where(m0, s0, s1))
    return parts[0] if len(parts) == 1 else jnp.concatenate(parts, axis=1)


def _rwkv_chunk_terms(insts, interleaved=()):
    c = CHUNK
    step_row = _iota((c, LANE), 0)
    step_col = _iota((c, LANE), 1) % c
    eye_w = (step_row == step_col).astype(F32)
    same_head = (_iota((LANE, LANE), 0) // 64) == (_iota((LANE, LANE), 1) // 64)
    stack_bf = lambda x: _stack_heads(x, 64).astype(BF16)
    cums = [_mm_01_lhs(_cumsum_matrix(rev), lw) for (_, lw, _, _, _, _, rev) in insts]
    pre = []
    for (r, lw, kd, a, b, v, rev), cum in zip(insts, cums):
        end = cum[0:1] if rev else cum[c - 1:c]
        inv_w = jnp.exp(-cum)
        rem_w = jnp.exp(end - cum)
        a_t = a * jnp.exp(cum - lw)
        r_t = r * jnp.exp(cum)
        bk_s = jnp.concatenate([stack_bf(b * inv_w), stack_bf(kd * inv_w)], axis=0)
        bkh = jnp.concatenate([b * rem_w, kd * rem_w], axis=0).astype(BF16)
        pre.append((a_t, r_t, bk_s, bkh, v, jnp.exp(end)))
    ms = [_dot(jnp.concatenate([a_t, r_t], axis=0).astype(BF16), bk_s, _NT) for (a_t, r_t, bk_s, _, _, _) in pre]
    mats = []
    for m, (_, _, _, _, _, _, rev) in zip(ms, insts):
        strict = (step_row < step_col) if rev else (step_row > step_col)
        incl = (step_row <= step_col) if rev else (step_row >= step_col)
        l_ab = jnp.where(strict, m[0:c, 0:LANE], 0.0)
        l_akrk = jnp.concatenate([jnp.where(strict, m[0:c, LANE:2 * LANE], 0.0),
                                  jnp.where(incl, m[c:2 * c, LANE:2 * LANE], 0.0)], axis=0).astype(BF16)
        m_rb = jnp.where(incl, m[c:2 * c, 0:LANE], 0.0).astype(BF16)
        mats.append((l_ab, l_akrk, m_rb))
    pending = list(interleaved)

    def run_interleaved():
        if pending:
            pending.pop(0)()

    lvs = [_dot(l_akrk, stack_bf(pr[4])) for (_, l_akrk, _), pr in zip(mats, pre)]
    run_interleaved()
    ps = [eye_w + l_ab for (l_ab, _, _) in mats]
    lps = [_dot(l_ab.astype(BF16), stack_bf(l_ab)) for (l_ab, _, _) in mats]
    for level in range(1, 6):
        if level < 5:
            xs = [_dot(lp.astype(BF16), jnp.concatenate([stack_bf(p), stack_bf(lp)], axis=1))
                  for lp, p in zip(lps, ps)]
            ps = [p + x[:, 0:LANE] for p, x in zip(ps, xs)]
            lps = [x[:, LANE:2 * LANE] for x in xs]
        else:
            ps = [p + _dot(lp.astype(BF16), stack_bf(p)) for lp, p in zip(lps, ps)]
        if level in (2, 4):
            run_interleaved()
    pxs = [_dot(p.astype(BF16), jnp.concatenate([stack_bf(pr[0]), stack_bf(lv[0:c])], axis=1))
           for p, pr, lv in zip(ps, pre, lvs)]
    run_interleaved()
    mzs = [_dot(mt[2], jnp.concatenate([stack_bf(px[:, 0:LANE]), stack_bf(px[:, LANE:2 * LANE])], axis=1))
           for mt, px in zip(mats, pxs)]
    ts = [_dot(px[:, 0:LANE].astype(BF16), pr[3][0:c], _TN) for px, pr in zip(pxs, pre)]
    gs = [_dot(jnp.concatenate([px[:, LANE:2 * LANE], pr[4]], axis=0).astype(BF16), pr[3], _TN)
          for px, pr in zip(pxs, pre)]
    while pending:
        run_interleaved()
    out = []
    for pr, lv, mz, t, g in zip(pre, lvs, mzs, ts, gs):
        q = pr[1] + mz[:, 0:LANE]
        y0 = mz[:, LANE:2 * LANE] + lv[c:2 * c]
        g_wide = jnp.where(_iota((c, LANE), 1) < 64, g[0:c], g[c:2 * c])
        out.append((jnp.where(same_head, t, 0.0).astype(BF16), g_wide, pr[5],
                    q.astype(BF16), y0))
    return out


def _gla_chunk_terms(insts, interleaved=()):
    c = CHUNK
    step_row = _iota((c, LANE), 0)
    step_col = _iota((c, LANE), 1) % c
    same_head = (_iota((LANE, 2 * LANE), 0) // 64) == (_iota((LANE, 2 * LANE), 1) // LANE)
    cums = [_mm_01_lhs(_cumsum_matrix(rev), g) for (_, _, _, g, rev) in insts]
    pre = []
    for (q, k, v, g, rev), cum in zip(insts, cums):
        end = cum[0:1] if rev else cum[c - 1:c]
        qt = (q * jnp.exp(cum)).astype(BF16)
        k_s = _stack_heads(k * jnp.exp(-cum), 64).astype(BF16)
        kh = (k * jnp.exp(end - cum)).astype(BF16)
        a_col = jnp.broadcast_to(jnp.exp(end), (LANE, LANE)).T
        pre.append((qt, k_s, kh, v.astype(BF16), _stack_heads(v, LANE).astype(BF16), a_col))
    pending = list(interleaved)

    def run_interleaved():
        if pending:
            pending.pop(0)()

    run_interleaved()
    atts = [_dot(pr[0], pr[1], _NT) for pr in pre]
    run_interleaved()
    atts = [jnp.where((step_row <= step_col) if inst[4] else (step_row >= step_col), att, 0.0).astype(BF16)
            for att, inst in zip(atts, insts)]
    o0s = [_dot(att, pr[4]) for att, pr in zip(atts, pre)]
    run_interleaved()
    kvs = [jnp.where(same_head, _dot(pr[2], pr[3], _TN), 0.0) for pr in pre]
    while pending:
        run_interleaved()
    return [(pr[0], o0, pr[5], kv) for pr, o0, kv in zip(pre, o0s, kvs)]


def _rwkv_kernel(seq_len, nseq, is_grid, zero_init, r_ref, k_ref, v_ref, lora_ref, cwr_ref, cwk_ref, cwv_ref,
                 w2_ref, a2_ref, g2_ref, vec_ref, *rest):
    s0_ref = None if zero_init else rest[0]
    (y_ref, sout_ref, pad_ref, rs_ref, ks_ref, vs_ref, kk_ref, bonus_ref, gate_ref, yf_ref, yb_ref, st_ref, tt_ref,
     tg_ref, tw_ref, tq_ref, ty_ref) = rest[0 if zero_init else 1:][:17]
    left_ref, right_ref = rest[-2:] if is_grid else (None, None)
    n_chunks = seq_len // CHUNK
    npp = RW_PAIRS_PER_STEP
    w = npp * LANE
    pair = lambda x, p: x[:, p * LANE:(p + 1) * LANE]
    vec = vec_ref[...]
    w0 = (vec[0:1], vec[1:2])
    a0 = (vec[2:3], vec[3:4])
    k_k, k_a, r_k, ln_w, ln_b = vec[4:5], vec[5:6], vec[6:7], vec[7:8], vec[8:9]
    block_sum = _head_sums
    chains = [(d, p) for d in range(2) for p in range(npp)]
    rows_of = lambda s, c: slice((s * n_chunks + c) * CHUNK, (s * n_chunks + c + 1) * CHUNK)

    zeros = jnp.zeros((CONV_PAD, 3 * w), F32)
    for ref in (pad_ref, left_ref, right_ref) if is_grid else (pad_ref,):
        ref[0:CONV_PAD, :] = zeros
        ref[CONV_PAD + seq_len:2 * CONV_PAD + seq_len, :] = zeros
    cw = jnp.concatenate([cwr_ref[...], cwk_ref[...], cwv_ref[...]], axis=1)
    col = _iota((CHUNK, 3 * w), 0)

    def shift_body(c, carry):
        base = pl.multiple_of(CONV_PAD + c * CHUNK, CHUNK)
        win = pad_ref[pl.ds(base - 8, CHUNK + 16), :]
        left_ref[pl.ds(base, CHUNK), :] = jnp.where(col >= 1, win[7:7 + CHUNK], 0.0)
        right_ref[pl.ds(base, CHUNK), :] = jnp.where(col <= GRID_W - 2, win[9:9 + CHUNK], 0.0)
        return carry

    def conv_chunk(c):
        base = CONV_PAD + c * CHUNK
        acc = jnp.zeros((CHUNK, 3 * w), F32)
        if is_grid:
            for di in (-1, 0, 1):
                row = pl.ds(base + di * GRID_W, CHUNK)
                for dj, src in ((-1, left_ref), (0, pad_ref), (1, right_ref)):
                    tap = (di + 1) * 3 + dj + 1
                    acc = acc + src[row, :] * cw[tap:tap + 1]
        else:
            win = pad_ref[pl.ds(base - 8, CHUNK + 16), :]
            for dj in (-1, 0, 1):
                acc = acc + win[8 + dj:8 + dj + CHUNK] * cw[4 + dj:5 + dj]
        return acc[:, 0:w], acc[:, w:2 * w], acc[:, 2 * w:3 * w]

    def load_sequence(s):
        seq_rows = slice(s * seq_len, (s + 1) * seq_len)
        pad_ref[CONV_PAD:CONV_PAD + seq_len, 0:w] = r_ref[seq_rows, :]
        pad_ref[CONV_PAD:CONV_PAD + seq_len, w:2 * w] = k_ref[seq_rows, :]
        pad_ref[CONV_PAD:CONV_PAD + seq_len, 2 * w:3 * w] = v_ref[seq_rows, :]

    def conv_store(s, c):
        rows = rows_of(s, c)
        rc, kc, vc = conv_chunk(c)
        kk = kc * k_k
        rs_ref[rows, :] = rc
        ks_ref[rows, :] = kc
        vs_ref[rows, :] = vc
        kk_ref[rows, :] = kk * lax.rsqrt(block_sum(kk * kk) + EPS)
        bonus_ref[rows, :] = block_sum(rc * kc * r_k) * vc
        gate_ref[rows, :] = _mm(_sigmoid(lora_ref[rows, 2 * LANE:3 * LANE]), g2_ref[...])

    for s in range(nseq):
        for d, p in chains:
            if zero_init:
                st_ref[s, d, p] = jnp.zeros((64, LANE), F32)
            else:
                st_ref[s, d, p] = jnp.concatenate([s0_ref[s, d, 2 * p], s0_ref[s, d, 2 * p + 1]], axis=1)

    if nseq == 1:
        groups = [(0, list(range(g * TERM_UNROLL, (g + 1) * TERM_UNROLL)),
                   list(range(n_chunks - 1 - g * TERM_UNROLL, n_chunks - 1 - (g + 1) * TERM_UNROLL, -1)))
                  for g in range(n_chunks // TERM_UNROLL)]
    else:
        groups = [(s, list(range(n_chunks)), list(range(n_chunks - 1, -1, -1))) for s in range(nseq)]

    def group_terms(group, interleaved):
        s, fwd, bwd = group
        insts, where, lora_in = [], [], {}
        for d, chunks in ((0, fwd), (1, bwd)):
            for c in chunks:
                rows = rows_of(s, c)
                if c not in lora_in:
                    lora_in[c] = (jnp.tanh(lora_ref[rows, 0:LANE]).astype(BF16),
                                  lora_ref[rows, LANE:2 * LANE].astype(BF16))
                lw = -RW_DECAY_SCALE * _sigmoid(w0[d] + _dot(lora_in[c][0], w2_ref[d]))
                ag = _sigmoid(a0[d] + _dot(lora_in[c][1], a2_ref[d]))
                rc, kc, vc, kk = rs_ref[rows, :], ks_ref[rows, :], vs_ref[rows, :], kk_ref[rows, :]
                kd = kc * (1.0 + (ag - 1.0) * k_a)
                kb = kk * ag
                for p in range(npp):
                    insts.append((pair(rc, p), pair(lw, p), pair(kd, p), -pair(kk, p), pair(kb, p), pair(vc, p),
                                  d == 1))
                    where.append((d, p, s * n_chunks + c))
        for (d, p, gc), (t, g, w_end, q, y0) in zip(where, _rwkv_chunk_terms(insts, interleaved)):
            tt_ref[d, p, gc] = t
            tg_ref[d, p, gc] = g
            tw_ref[d, p, gc] = jnp.broadcast_to(w_end, (8, LANE))
            tq_ref[d, p, gc] = q
            ty_ref[d, p, gc] = y0

    def scan_step(s, chunk_of_dir):
        gcs = [s * n_chunks + chunk_of_dir[d] for d, _ in chains]
        ss = [st_ref[s, d, p] for d, p in chains]
        ys = [_dot(tq_ref[d, p, gc], _stack_heads(x, 64).astype(BF16), _NT) + ty_ref[d, p, gc]
              for (d, p), gc, x in zip(chains, gcs, ss)]
        sn = [x * tw_ref[d, p, gc][0:1] + _dot(x.astype(BF16), tt_ref[d, p, gc]) + tg_ref[d, p, gc]
              for (d, p), gc, x in zip(chains, gcs, ss)]
        for (d, p), gc, y, x in zip(chains, gcs, ys, sn):
            st_ref[s, d, p] = x
            out_ref = yf_ref if d == 0 else yb_ref
            out_ref[gc * CHUNK:(gc + 1) * CHUNK, p * LANE:(p + 1) * LANE] = y

    def group_scan(group):
        s, fwd, bwd = group
        return [functools.partial(scan_step, s, (cf, cb)) for cf, cb in zip(fwd, bwd)]

    conv_done, seq_loaded = set(), set()

    def group_conv(group):
        s, fwd, bwd = group
        todo = [c for c in sorted(set(fwd) | set(bwd)) if (s, c) not in conv_done]
        conv_done.update((s, c) for c in todo)
        thunks = []
        for i in range(0, len(todo), max(1, -(-len(todo) // TERM_UNROLL))):
            part = todo[i:i + max(1, -(-len(todo) // TERM_UNROLL))]
            need_load = s not in seq_loaded
            seq_loaded.add(s)

            def run(part=part, need_load=need_load):
                if need_load:
                    assert not is_grid or nseq == 1
                    load_sequence(s)
                for c in part:
                    conv_store(s, c)

            thunks.append(run)
        return thunks

    def merge(a, b):
        n = max(len(a), len(b))
        pick = lambda lst, i: lst[i] if i < len(lst) else (lambda: None)
        return [lambda i=i: (pick(a, i)(), pick(b, i)()) for i in range(n)]

    if is_grid:
        load_sequence(0)
        seq_loaded.add(0)
        lax.fori_loop(0, n_chunks, shift_body, 0)
    for thunk in group_conv(groups[0]):
        thunk()
    for g, group in enumerate(groups):
        scans = group_scan(groups[g - 1]) if g else []
        convs = group_conv(groups[g + 1]) if g + 1 < len(groups) else []
        group_terms(group, merge(scans, convs))
    for step in group_scan(groups[-1]):
        step()
    for s in range(nseq):
        for d, p in chains:
            x = st_ref[s, d, p]
            sout_ref[s, d, 2 * p] = x[:, 0:64]
            sout_ref[s, d, 2 * p + 1] = x[:, 64:LANE]

    def post_body(j, carry):
        offs = [pl.multiple_of((j * TERM_UNROLL + u) * CHUNK, CHUNK) for u in range(TERM_UNROLL)]
        ys = [yf_ref[pl.ds(off, CHUNK), :] + yb_ref[pl.ds(off, CHUNK), :] for off in offs]
        mus = [block_sum(y) * (1.0 / 64) for y in ys]
        dlts = [y - mu for y, mu in zip(ys, mus)]
        vrs = [block_sum(dlt * dlt) * (1.0 / 64) for dlt in dlts]
        for off, dlt, var in zip(offs, dlts, vrs):
            yn = dlt * lax.rsqrt(var + RW_LN_EPS) * ln_w + ln_b
            y_ref[pl.ds(off, CHUNK), :] = (yn + bonus_ref[pl.ds(off, CHUNK), :]) * gate_ref[pl.ds(off, CHUNK), :]
        return carry

    lax.fori_loop(0, nseq * n_chunks // TERM_UNROLL, post_body, 0)


def _rwkv_mixer(proj, first_seq, n_seq, seq_len, is_grid, prm, s0):
    n_pairs = RW_W // LANE
    n_chunks = seq_len // CHUNK
    npp = RW_PAIRS_PER_STEP
    w = npp * LANE
    nseq = max(1, RW_GROUPS * TERM_UNROLL // n_chunks)
    assert n_chunks % TERM_UNROLL == 0 and n_pairs % npp == 0
    assert (n_chunks == TERM_UNROLL or nseq == 1) and n_seq % nseq == 0 and first_seq % nseq == 0
    rows = nseq * seq_len
    total_chunks = nseq * n_chunks
    first = first_seq // nseq
    col = lambda cb: (lambda b, p: (b + first, cb // npp + p))
    par = lambda cb: (lambda b, p: (0, cb // npp + p))
    state_spec = pl.BlockSpec((nseq, 2, 2 * npp, 64, 64), lambda b, p: (b, 0, p, 0, 0))
    kernel = functools.partial(_rwkv_kernel, seq_len, nseq, is_grid, s0 is None)
    y, s_out = pl.pallas_call(
        kernel,
        grid=(n_seq // nseq, n_pairs // npp),
        in_specs=[pl.BlockSpec((rows, w), col(CB_R)),
                  pl.BlockSpec((rows, w), col(CB_K)),
                  pl.BlockSpec((rows, w), col(CB_V)),
                  pl.BlockSpec((rows, 3 * LANE), lambda b, p: (b + first, CB_LORA // 3)),
                  pl.BlockSpec((9, w), par(CB_R)),
                  pl.BlockSpec((9, w), par(CB_K)),
                  pl.BlockSpec((9, w), par(CB_V)),
                  pl.BlockSpec((2, LANE, w), lambda b, p: (0, 0, p)),
                  pl.BlockSpec((2, LANE, w), lambda b, p: (0, 0, p)),
                  pl.BlockSpec((LANE, w), lambda b, p: (0, p)),
                  pl.BlockSpec((16, w), lambda b, p: (0, p))] + ([] if s0 is None else [state_spec]),
        out_specs=[pl.BlockSpec((rows, w), lambda b, p: (b, p)), state_spec],
        out_shape=[jax.ShapeDtypeStruct((n_seq * seq_len, RW_W), F32),
                   jax.ShapeDtypeStruct((n_seq, 2, 2 * n_pairs, 64, 64), F32)],
        scratch_shapes=[pltpu.VMEM((seq_len + 2 * CONV_PAD, 3 * w), F32)]
                       + [pltpu.VMEM((rows, w), F32)] * 8
                       + [pltpu.VMEM((nseq, 2, npp, 64, LANE), F32),
                          pltpu.VMEM((2, npp, total_chunks, LANE, LANE), BF16),
                          pltpu.VMEM((2, npp, total_chunks, 64, LANE), F32),
                          pltpu.VMEM((2, npp, total_chunks, 8, LANE), F32),
                          pltpu.VMEM((2, npp, total_chunks, CHUNK, LANE), BF16),
                          pltpu.VMEM((2, npp, total_chunks, CHUNK, LANE), F32)]
                       + ([pltpu.VMEM((seq_len + 2 * CONV_PAD, 3 * w), F32)] * 2 if is_grid else []),
        compiler_params=_cparams(2),
        name="rwkv_mixer",
    )(proj, proj, proj, proj, prm['conv'], prm['conv'], prm['conv'], prm['w2p'], prm['a2p'], prm['g2'],
      prm['vec'], *([] if s0 is None else [s0]))
    return y, s_out


def _gla_kernel(seq_len, nseq, zero_init, q_ref, k_ref, v_ref, og_ref, lgk_ref, gk2_ref, gvec_ref, *rest):
    s0_ref = None if zero_init else rest[0]
    y_ref, sout_ref, of_ref, ob_ref, st_ref, tq_ref, to_ref, ta_ref, tkv_ref = rest[0 if zero_init else 1:]
    n_chunks = seq_len // CHUNK
    npp = GLA_PAIRS
    gvec = gvec_ref[...]
    chains = [(d, p) for d in range(2) for p in range(npp)]

    for s in range(nseq):
        for d, p in chains:
            if zero_init:
                st_ref[s, d, p] = jnp.zeros((LANE, 2 * LANE), F32)
            else:
                z = jnp.zeros((64, LANE), F32)
                st_ref[s, d, p] = jnp.concatenate([jnp.concatenate([s0_ref[s, d, 2 * p], z], axis=1),
                                                   jnp.concatenate([z, s0_ref[s, d, 2 * p + 1]], axis=1)], axis=0)

    if nseq == 1:
        groups = [(0, list(range(g * TERM_UNROLL, (g + 1) * TERM_UNROLL)),
                   list(range(n_chunks - 1 - g * TERM_UNROLL, n_chunks - 1 - (g + 1) * TERM_UNROLL, -1)))
                  for g in range(n_chunks // TERM_UNROLL)]
    else:
        groups = [(s, list(range(n_chunks)), list(range(n_chunks - 1, -1, -1))) for s in range(nseq)]

    def group_terms(group, interleaved):
        s, fwd, bwd = group
        insts, where, lgk_in = [], [], {}
        for d, chunks in ((0, fwd), (1, bwd)):
            for c in chunks:
                gc = s * n_chunks + c
                rows = slice(gc * CHUNK, (gc + 1) * CHUNK)
                if c not in lgk_in:
                    lgk_in[c] = lgk_ref[rows, :].astype(BF16)
                x = _dot(lgk_in[c], gk2_ref[d]) + gvec[d:d + 1, 0:GLA_QK_W]
                g = _log_sigmoid(x) * (1.0 / GLA_GATE_NORM)
                qc = q_ref[rows, :] * GLA_Q_SCALE
                kc = k_ref[rows, :]
                vc = v_ref[rows, :]
                for p in range(npp):
                    qk = slice(p * LANE, (p + 1) * LANE)
                    insts.append((qc[:, qk], kc[:, qk], vc[:, 2 * p * LANE:2 * (p + 1) * LANE], g[:, qk], d == 1))
                    where.append((d, p, gc))
        for (d, p, gc), (qt, o0, a_col, kv) in zip(where, _gla_chunk_terms(insts, interleaved)):
            tq_ref[d, p, gc] = qt
            to_ref[d, p, gc] = o0
            ta_ref[d, p, gc] = a_col
            tkv_ref[d, p, gc] = kv

    def scan_step(s, chunk_of_dir):
        gcs = [s * n_chunks + chunk_of_dir[d] for d, _ in chains]
        ss = [st_ref[s, d, p] for d, p in chains]
        os_ = [_dot(tq_ref[d, p, gc], x.astype(BF16)) + to_ref[d, p, gc] for (d, p), gc, x in zip(chains, gcs, ss)]
        for (d, p), gc, x, o in zip(chains, gcs, ss, os_):
            a_col = ta_ref[d, p, gc]
            st_ref[s, d, p] = x * jnp.concatenate([a_col, a_col], axis=1) + tkv_ref[d, p, gc]
            out_ref = of_ref if d == 0 else ob_ref
            out_ref[gc * CHUNK:(gc + 1) * CHUNK, 2 * p * LANE:2 * (p + 1) * LANE] = o

    def group_scan(group):
        s, fwd, bwd = group
        return [functools.partial(scan_step, s, (cf, cb)) for cf, cb in zip(fwd, bwd)]

    for g, group in enumerate(groups):
        group_terms(group, group_scan(groups[g - 1]) if g else [])
    for step in group_scan(groups[-1]):
        step()
    for s in range(nseq):
        for d, p in chains:
            x = st_ref[s, d, p]
            sout_ref[s, d, 2 * p] = x[0:64, 0:LANE]
            sout_ref[s, d, 2 * p + 1] = x[64:LANE, LANE:2 * LANE]

    def post_body(c, carry):
        off = pl.multiple_of(c * CHUNK, CHUNK)
        for h in range(2 * npp):
            hs = slice(h * LANE, (h + 1) * LANE)
            o = of_ref[pl.ds(off, CHUNK), hs] + ob_ref[pl.ds(off, CHUNK), hs]
            gate = _silu(og_ref[pl.ds(off, CHUNK), hs])
            y_ref[pl.ds(off, CHUNK), hs] = _rmsnorm_rows(o) * gvec[2:3, hs] * gate
        return carry

    lax.fori_loop(0, nseq * n_chunks, post_body, 0)


def _gla_mixer(proj, first_seq, n_seq, seq_len, prm, s0):
    npp = GLA_PAIRS
    n_heads = 2 * npp
    n_chunks = seq_len // CHUNK
    nseq = max(1, RW_GROUPS * TERM_UNROLL // n_chunks)
    assert n_chunks % TERM_UNROLL == 0
    assert (n_chunks == TERM_UNROLL or nseq == 1) and n_seq % nseq == 0 and first_seq % nseq == 0
    rows = nseq * seq_len
    total_chunks = nseq * n_chunks
    first = first_seq // nseq
    state_spec = pl.BlockSpec((nseq, 2, n_heads, 64, LANE), lambda b: (b, 0, 0, 0, 0))
    kernel = functools.partial(_gla_kernel, seq_len, nseq, s0 is None)
    y, s_out = pl.pallas_call(
        kernel,
        grid=(n_seq // nseq,),
        in_specs=[pl.BlockSpec((rows, GLA_QK_W), lambda b: (b + first, CB_GQ * LANE // GLA_QK_W)),
                  pl.BlockSpec((rows, GLA_QK_W), lambda b: (b + first, CB_GK * LANE // GLA_QK_W)),
                  pl.BlockSpec((rows, GLA_V_W), lambda b: (b + first, CB_GV * LANE // GLA_V_W)),
                  pl.BlockSpec((rows, GLA_V_W), lambda b: (b + first, CB_OG * LANE // GLA_V_W)),
                  pl.BlockSpec((rows, LANE), lambda b: (b + first, CB_LGK)),
                  pl.BlockSpec((2, LANE, GLA_QK_W), lambda b: (0, 0, 0)),
                  pl.BlockSpec((8, GLA_V_W), lambda b: (0, 0))] + ([] if s0 is None else [state_spec]),
        out_specs=[pl.BlockSpec((rows, GLA_V_W), lambda b: (b, 0)), state_spec],
        out_shape=[jax.ShapeDtypeStruct((n_seq * seq_len, GLA_V_W), F32),
                   jax.ShapeDtypeStruct((n_seq, 2, n_heads, 64, LANE), F32)],
        scratch_shapes=[pltpu.VMEM((rows, GLA_V_W), F32)] * 2
                       + [pltpu.VMEM((nseq, 2, npp, LANE, 2 * LANE), F32),
                          pltpu.VMEM((2, npp, total_chunks, CHUNK, LANE), BF16),
                          pltpu.VMEM((2, npp, total_chunks, CHUNK, 2 * LANE), F32),
                          pltpu.VMEM((2, npp, total_chunks, LANE, LANE), F32),
                          pltpu.VMEM((2, npp, total_chunks, LANE, 2 * LANE), F32)],
        compiler_params=_cparams(1),
        name="gla_mixer",
    )(proj, proj, proj, proj, proj, prm['gk2p'], prm['gvec'], *([] if s0 is None else [s0]))
    return y, s_out


OUT_TM = 512
ROUTE_NEG = -1e30
LANE_GROUP0 = N_EXPERTS


def _route(logits):
    lane = _iota(logits.shape, 1)
    lane_f = lane.astype(F32)
    big = float(LANE)
    is_g = (lane >= LANE_GROUP0) & (lane < LANE_GROUP0 + 4)
    gmax = jnp.max(jnp.where(is_g, logits, ROUTE_NEG), axis=-1, keepdims=True)
    gidx = jnp.min(jnp.where(is_g & (logits == gmax), lane_f, big), axis=-1, keepdims=True) - LANE_GROUP0
    gsum = jnp.sum(jnp.where(is_g, jnp.exp(jnp.minimum(logits - gmax, 0.0)), 0.0), axis=-1, keepdims=True)
    g_w = 1.0 / gsum
    in_grp = (lane < N_EXPERTS) & ((lane // 4).astype(F32) == gidx)
    m1 = jnp.max(jnp.where(in_grp, logits, ROUTE_NEG), axis=-1, keepdims=True)
    i1 = jnp.min(jnp.where(in_grp & (logits == m1), lane_f, big), axis=-1, keepdims=True)
    rest = in_grp & (lane_f != i1)
    m2 = jnp.max(jnp.where(rest, logits, ROUTE_NEG), axis=-1, keepdims=True)
    i2 = jnp.min(jnp.where(rest & (logits == m2), lane_f, big), axis=-1, keepdims=True)
    t = jnp.exp(m2 - m1)
    w1 = g_w / (1.0 + t)
    return jnp.where(lane_f == i1, w1, 0.0) + jnp.where(lane_f == i2, w1 * t, 0.0)


def _outproj_kernel(tiles, yrc_ref, yrd_ref, ygc_ref, ygd_ref, xc_ref, xd_ref, mod_ref, wo_ref, g_ref, wr_ref,
                    br_ref, x1_ref, h2_ref, cmb_ref):
    def run(yr_ref, yg_ref, x_ref):
        m = mod_ref[0]
        mix = _mm(yr_ref[...], wo_ref[0:RW_W, :]) + _mm(yg_ref[...], wo_ref[RW_W:RW_W + GLA_V_W, :])
        x1 = x_ref[...] + m[2:3] * mix
        h2 = _rmsnorm_rows(x1) * g_ref[...] * (1.0 + m[4:5]) + m[3:4]
        x1_ref[...] = x1
        h2_ref[...] = h2.astype(BF16)
        h_hi, h_lo = _split2(h2)
        both = _dot(h_hi, wr_ref[...])
        logits = both[:, 0:LANE] + both[:, LANE:2 * LANE] + _dot(h_lo, wr_ref[:, 0:LANE])
        cmb_ref[...] = _route(logits + br_ref[...])

    tiles.by_pass(functools.partial(run, yrc_ref, ygc_ref, xc_ref), functools.partial(run, yrd_ref, ygd_ref, xd_ref))


def _out_projection(tiles, y_rw, y_gla, x, mod, w_out, norm_g, w_route, b_route):
    n = x[0].shape[0] + x[1].shape[0]
    full = lambda a: pl.BlockSpec(a.shape, lambda i: (0,) * a.ndim)
    return pl.pallas_call(
        functools.partial(_outproj_kernel, tiles),
        grid=(tiles.n_ctx + tiles.n_dec,),
        in_specs=[*tiles.specs(RW_W), *tiles.specs(GLA_V_W), *tiles.specs(D_MODEL), tiles.mod_spec(),
                  full(w_out), full(norm_g), full(w_route), full(b_route)],
        out_specs=[tiles.merged(D_MODEL), tiles.merged(D_MODEL), tiles.merged(LANE)],
        out_shape=[jax.ShapeDtypeStruct((n, D_MODEL), F32), jax.ShapeDtypeStruct((n, D_MODEL), BF16),
                   jax.ShapeDtypeStruct((n, LANE), F32)],
        compiler_params=_cparams(1),
        name="out_proj_router",
    )(*y_rw, *y_gla, *x, mod, w_out, norm_g, w_route, b_route)


MOE_TM = 512
MOE_RB = 128
MOE_INTERLEAVE = 4
SLOT_ALIGN = 16
MOE_SLOTS = 2 * MOE_TM + N_EXPERTS * SLOT_ALIGN + MOE_RB


def _stage_expert_weights(srcs_hbm, dst_refs, stage_refs, sems):
    def copies(e):
        return [pltpu.make_async_copy(src.at[e], stage.at[e % 2], sem.at[e % 2])
                for src, stage, sem in zip(srcs_hbm, stage_refs, sems)]

    for c in copies(0):
        c.start()
    for e in range(N_EXPERTS):
        if e + 1 < N_EXPERTS:
            for c in copies(e + 1):
                c.start()
        for c, dst, stage in zip(copies(e), dst_refs, stage_refs):
            c.wait()
            dst[e] = stage[e % 2].astype(BF16)


def _moe_kernel(tiles, h2_ref, cmb_ref, x1_ref, mod_ref, w1_hbm, w3_hbm, w2_hbm, fg_ref, yc_ref, yd_ref,
                xs_ref, ys_ref, w1_ref, w3_ref, w2_ref, stage1_ref, stage3_ref, stage2_ref, sem1, sem3, sem2):
    @pl.when(pl.program_id(0) == 0)
    def _():
        _stage_expert_weights((w1_hbm, w3_hbm, w2_hbm), (w1_ref, w3_ref, w2_ref),
                              (stage1_ref, stage3_ref, stage2_ref), (sem1, sem3, sem2))

    cmb = cmb_ref[...]
    lane = _iota(cmb.shape, 1).astype(F32)
    sel = cmb > 0.0
    sel01 = jnp.where(sel, 1.0, 0.0).astype(BF16)
    before = (_iota((MOE_TM, MOE_TM), 0) > _iota((MOE_TM, MOE_TM), 1)).astype(BF16)
    pos = _dot(before, sel01)
    cnt = pos[MOE_TM - 1:MOE_TM] + sel01[MOE_TM - 1:MOE_TM].astype(F32)
    seg = jnp.floor((cnt + (SLOT_ALIGN - 1)) * (1.0 / SLOT_ALIGN))
    lower_experts = (_iota((LANE, LANE), 0) < _iota((LANE, LANE), 1)).astype(BF16)
    start = _dot(jnp.broadcast_to(seg, (8, LANE)).astype(BF16), lower_experts)[0:1] * SLOT_ALIGN
    n_blk = jnp.floor((cnt + (MOE_RB - 1)) * (1.0 / MOE_RB)).astype(jnp.int32)
    start_i = start.astype(jnp.int32)
    cnt_i = cnt.astype(jnp.int32)
    slot = start + pos
    e_a = jnp.min(jnp.where(sel, lane, float(LANE)), axis=-1, keepdims=True)
    e_b = jnp.max(jnp.where(sel, lane, -1.0), axis=-1, keepdims=True)
    pick = lambda e, x: jnp.sum(jnp.where(lane == e, x, 0.0), axis=-1, keepdims=True)
    slot_a, w_a = pick(e_a, slot), pick(e_a, cmb)
    slot_b = jnp.where(e_b != e_a, pick(e_b, slot), -1.0)
    w_b = pick(e_b, cmb)

    slots_t = jnp.where(lane == 0.0, slot_a, jnp.where(lane == 1.0, slot_b, -1.0)).T
    row_slot = _iota((MOE_SLOTS, MOE_TM), 0).astype(F32)
    gather = jnp.where((row_slot == slots_t[0:1]) | (row_slot == slots_t[1:2]), 1.0, 0.0).astype(BF16)
    xs_ref[...] = _dot(gather, h2_ref[...]).astype(BF16)
    ys_ref[...] = jnp.zeros_like(ys_ref)

    row_in_blk = _iota((MOE_RB, D_MODEL), 0)

    def expert_blocks(experts, r0s, ends):
        xbs = [xs_ref[pl.ds(r0, MOE_RB), :] for r0 in r0s]
        gates = [_dot(xb, w3_ref[e]) for xb, e in zip(xbs, experts)]
        ups = [_dot(xb, w1_ref[e]) for xb, e in zip(xbs, experts)]
        acts = [(_silu(g) * u).astype(BF16) for g, u in zip(gates, ups)]
        outs = [_dot(a, w2_ref[e]) for a, e in zip(acts, experts)]
        for r0, end, out in zip(r0s, ends, outs):
            keep = row_in_blk + r0 >= end
            ys_ref[pl.ds(r0, MOE_RB), :] = jnp.where(keep, ys_ref[pl.ds(r0, MOE_RB), :], out.astype(BF16))

    seg_start = [pl.multiple_of(start_i[0, e], SLOT_ALIGN) for e in range(N_EXPERTS)]
    seg_end = [seg_start[e] + cnt_i[0, e] for e in range(N_EXPERTS)]
    for e0 in range(0, N_EXPERTS, MOE_INTERLEAVE):
        es = list(range(e0, e0 + MOE_INTERLEAVE))
        expert_blocks(es, [seg_start[e] for e in es], [seg_end[e] for e in es])
    for e in range(N_EXPERTS):
        def extra_block(b, carry, e=e):
            expert_blocks([e], [pl.multiple_of(seg_start[e] + b * MOE_RB, SLOT_ALIGN)], [seg_end[e]])
            return carry

        lax.fori_loop(1, n_blk[0, e], extra_block, 0)

    col_slot = _iota((MOE_TM, MOE_SLOTS), 1).astype(F32)
    scatter = (jnp.where(col_slot == slot_a, w_a, 0.0) + jnp.where(col_slot == slot_b, w_b, 0.0)).astype(BF16)
    x2 = x1_ref[...] + mod_ref[0][5:6] * _dot(scatter, ys_ref[...])
    y = _rmsnorm_rows(x2) * fg_ref[...]

    def write(y_ref):
        y_ref[...] = y

    tiles.by_pass(functools.partial(write, yc_ref), functools.partial(write, yd_ref))


def _moe(tiles, h2, cmb, x1, mod, w1, w3, w2, final_g):
    assert tiles.tm == MOE_TM
    hbm = pl.BlockSpec(memory_space=pl.ANY)
    out_ctx, out_dec = tiles.specs(D_MODEL)
    return pl.pallas_call(
        functools.partial(_moe_kernel, tiles),
        grid=(tiles.n_ctx + tiles.n_dec,),
        in_specs=[tiles.merged(D_MODEL), tiles.merged(LANE), tiles.merged(D_MODEL), tiles.mod_spec(),
                  hbm, hbm, hbm, pl.BlockSpec((1, D_MODEL), lambda i: (0, 0))],
        out_specs=[out_ctx, out_dec],
        out_shape=[jax.ShapeDtypeStruct((tiles.n_ctx * MOE_TM, D_MODEL), F32),
                   jax.ShapeDtypeStruct((tiles.n_dec * MOE_TM, D_MODEL), F32)],
        scratch_shapes=[pltpu.VMEM((MOE_SLOTS, D_MODEL), BF16), pltpu.VMEM((MOE_SLOTS, D_MODEL), BF16),
                        pltpu.VMEM((N_EXPERTS, D_MODEL, D_EXPERT), BF16),
                        pltpu.VMEM((N_EXPERTS, D_MODEL, D_EXPERT), BF16),
                        pltpu.VMEM((N_EXPERTS, D_EXPERT, D_MODEL), BF16),
                        pltpu.VMEM((2, D_MODEL, D_EXPERT), F32), pltpu.VMEM((2, D_MODEL, D_EXPERT), F32),
                        pltpu.VMEM((2, D_EXPERT, D_MODEL), F32)] + [pltpu.SemaphoreType.DMA((2,))] * 3,
        compiler_params=_cparams(1),
        name="moe_experts",
    )(h2, cmb, x1, mod, w1, w3, w2, final_g)


def _pad_rows(x, rows):
    return jnp.pad(x, ((0, rows - x.shape[0]),) + ((0, 0),) * (x.ndim - 1))


def _pack_params(l, w_in, rw_conv, rw_w0, rw_w2, rw_a0, rw_a2, rw_g2, rw_k_k, rw_k_a, rw_r_k, rw_ln_w, rw_ln_b,
                 gla_gk2, gla_gk_b, gla_norm_g, moe_w_group, moe_b_group, moe_w_expert, moe_b_expert):
    wi = w_in[l]
    z = lambda n: jnp.zeros((D_MODEL, n), F32)
    w_in_t = jnp.swapaxes(wi, 0, 1)
    z64 = jnp.zeros((64, RW_W), F32)
    w2p = jnp.stack([jnp.concatenate([rw_w2[l, 0], z64], 0), jnp.concatenate([z64, rw_w2[l, 1]], 0)])
    a2p = jnp.stack([jnp.concatenate([rw_a2[l, 0], z64], 0), jnp.concatenate([z64, rw_a2[l, 1]], 0)])
    vec = _pad_rows(jnp.stack([rw_w0[l, 0], rw_w0[l, 1], rw_a0[l, 0], rw_a0[l, 1], rw_k_k[l], rw_k_a[l],
                               rw_r_k[l].reshape(RW_W), rw_ln_w[l], rw_ln_b[l]]), 16)
    rw = {'conv': rw_conv[l].reshape(9, 3 * RW_W), 'w2p': w2p.astype(BF16), 'a2p': a2p.astype(BF16),
          'g2': rw_g2[l].astype(BF16), 'vec': vec}
    gk2p = jnp.stack([_pad_rows(gla_gk2[l, 0], LANE),
                      _pad_rows(jnp.concatenate([jnp.zeros((16, GLA_QK_W), F32), gla_gk2[l, 1]], 0), LANE)])
    gk_b = jnp.pad(gla_gk_b[l], ((0, 0), (0, GLA_V_W - GLA_QK_W)))
    gvec = _pad_rows(jnp.concatenate([gk_b, jnp.tile(gla_norm_g[l], GLA_V_W // LANE)[None]], axis=0), 8)
    gla = {'gk2p': gk2p.astype(BF16), 'gvec': gvec}
    w_route = jnp.concatenate(_split2(jnp.concatenate([moe_w_expert[l], moe_w_group[l], z(LANE - N_EXPERTS - 4)],
                                                      axis=1)), axis=1)
    b_route = jnp.concatenate([moe_b_expert[l], moe_b_group[l], jnp.zeros((LANE - N_EXPERTS - 4,), F32)])[None]
    return w_in_t, rw, gla, w_route, b_route


def kernel(x_prompt, x_sample, state_rwkv, state_gla, c, c_ctx, norm1_g, norm2_g, w_ada, b_ada, w_in, w_out,
           rw_conv, rw_w0, rw_w2, rw_a0, rw_a2, rw_g2, rw_k_k, rw_k_a, rw_r_k, rw_ln_w, rw_ln_b,
           gla_gk2, gla_gk_b, gla_norm_g, moe_w_group, moe_b_group, moe_w_expert, moe_b_expert,
           moe_w1, moe_w3, moe_w2, final_g):
    depth = w_in.shape[0]
    assert depth == 1, "the packed layout below handles the single-layer trunk of this problem"
    l = 0
    n_dec = x_sample.shape[0]
    ctx_row = n_dec
    cond8 = _pad_rows(jnp.concatenate([c, c_ctx[None]], axis=0), 8)
    mod = _modulation(cond8, w_ada[l], b_ada[l][None]).reshape(8, N_MOD, D_MODEL)
    pk = _pack_params(l, w_in, rw_conv, rw_w0, rw_w2, rw_a0, rw_a2, rw_g2, rw_k_k, rw_k_a, rw_r_k, rw_ln_w,
                      rw_ln_b, gla_gk2, gla_gk_b, gla_norm_g, moe_w_group, moe_b_group, moe_w_expert,
                      moe_b_expert)
    w_in_t, rw, gla, w_route, b_route = pk

    n_ctx, ctx_len, _ = x_prompt.shape
    dec_len = x_sample.shape[1]
    x_ctx = x_prompt.reshape(n_ctx * ctx_len, D_MODEL)
    x_dec = x_sample.reshape(n_dec * dec_len, D_MODEL)
    assert (n_ctx * ctx_len) % dec_len == 0, "denoising sequences must start on a dec_len row block of proj"
    first_dec = n_ctx * ctx_len // dec_len
    tiles = _Tiles(n_ctx * ctx_len, n_dec * dec_len, dec_len, ctx_row, PROJ_TM)
    assert PROJ_TM == OUT_TM == MOE_TM

    proj = _in_projection(tiles, x_ctx, x_dec, mod, norm1_g[l][None], w_in_t)
    y_rw_c, s_rw = _rwkv_mixer(proj, 0, n_ctx, ctx_len, False, rw, None)
    y_gla_c, s_gla = _gla_mixer(proj, 0, n_ctx, ctx_len, gla, None)
    y_rw_d, _ = _rwkv_mixer(proj, first_dec, n_dec, dec_len, True, rw, state_rwkv[:, l])
    y_gla_d, _ = _gla_mixer(proj, first_dec, n_dec, dec_len, gla, state_gla[:, l])
    x1, h2, cmb = _out_projection(tiles, (y_rw_c, y_rw_d), (y_gla_c, y_gla_d), (x_ctx, x_dec), mod,
                                  w_out[l].astype(BF16), norm2_g[l][None], w_route, b_route)
    y_ctx, y_dec = _moe(tiles, h2, cmb, x1, mod, moe_w1[l], moe_w3[l], moe_w2[l], final_g[None])
    return (y_ctx.reshape(x_prompt.shape), y_dec.reshape(x_sample.shape), s_rw[:, None], s_gla[:, None])
```

```python
import functools

import jax
import jax.numpy as jnp
from jax import lax
from jax.experimental import pallas as pl
from jax.experimental.pallas import tpu as pltpu

F32 = jnp.float32
BF16 = jnp.bfloat16

D_MODEL = 1024
RW_W = 512
GLA_V_W = 512
GLA_QK_W = 256
N_EXPERTS = 16
D_EXPERT = 256
N_MOD = 6
EPS = 1e-6
RW_LN_EPS = 64e-5
RW_DECAY_SCALE = 0.606531
GLA_GATE_NORM = 16.0
GLA_Q_SCALE = 64 ** -0.5
GRID_W = 64

LANE = 128
CHUNK = 64
CONV_PAD = 128
TERM_UNROLL = 4
RW_PAIRS_PER_STEP = 2
GLA_PAIRS = 2
RW_GROUPS = 4
D_PROJ = 28 * LANE
VMEM_LIMIT = 56 * 1024 * 1024

CB_R, CB_K, CB_V, CB_LORA, CB_LGK, CB_GQ, CB_GK, CB_GV, CB_OG = 0, 4, 8, 12, 15, 16, 18, 20, 24

_NN = (((1,), (0,)), ((), ()))
_NT = (((1,), (1,)), ((), ()))
_TN = (((0,), (0,)), ((), ()))


def _dot(a, b, dims=_NN):
    return lax.dot_general(a, b, dims, preferred_element_type=F32)


def _mm(a, b, dims=_NN):
    return _dot(a.astype(BF16), b.astype(BF16), dims)


def _split2(x):
    hi = x.astype(BF16)
    lo = (x - hi.astype(F32)).astype(BF16)
    return hi, lo


def _mm3(a, b, dims=_NN):
    ah, al = _split2(a)
    bh, bl = _split2(b)
    return _dot(ah, bh, dims) + _dot(ah, bl, dims) + _dot(al, bh, dims)


def _mm_01_lhs(a01, b, dims=_NN):
    n = b.shape[1]
    both = _dot(a01, jnp.concatenate(_split2(b), axis=1), dims)
    return both[:, 0:n] + both[:, n:2 * n]


def _sigmoid(x):
    return 0.5 * jnp.tanh(0.5 * x) + 0.5


def _silu(x):
    return x * _sigmoid(x)


def _log_sigmoid(x):
    return jnp.minimum(x, 0.0) - jnp.log(1.0 + jnp.exp(-jnp.abs(x)))


def _iota(shape, dim):
    return lax.broadcasted_iota(jnp.int32, shape, dim)


def _cparams(n_axes):
    return pltpu.CompilerParams(dimension_semantics=("arbitrary",) * n_axes, vmem_limit_bytes=VMEM_LIMIT)


MOD_TN = 768


def _mod_kernel(c_ref, w_ref, b_ref, o_ref):
    o_ref[...] = _mm3(_silu(c_ref[...]), w_ref[...]) + b_ref[...]


def _modulation(cond8, w_ada, b_ada):
    n = w_ada.shape[1]
    return pl.pallas_call(
        _mod_kernel,
        grid=(n // MOD_TN,),
        in_specs=[pl.BlockSpec((8, D_MODEL), lambda j: (0, 0)),
                  pl.BlockSpec((D_MODEL, MOD_TN), lambda j: (0, j)),
                  pl.BlockSpec((1, MOD_TN), lambda j: (0, j))],
        out_specs=pl.BlockSpec((8, MOD_TN), lambda j: (0, j)),
        out_shape=jax.ShapeDtypeStruct((8, n), F32),
        compiler_params=_cparams(1),
        name="adaln_mod",
    )(cond8, w_ada, b_ada)


PROJ_TM = 512
D_IN = 3488
N_LGK = 32


def _rmsnorm_rows(x):
    return x * lax.rsqrt(jnp.mean(x * x, axis=-1, keepdims=True) + EPS)


class _Tiles:
    def __init__(self, n_ctx_tokens, n_dec_tokens, dec_seq_len, ctx_row, tm):
        self.tm = tm
        self.n_ctx = n_ctx_tokens // tm
        self.n_dec = n_dec_tokens // tm
        self.per_seq = dec_seq_len // tm
        self.ctx_row = ctx_row

    def specs(self, width):
        last_ctx = self.n_ctx - 1
        n_ctx = self.n_ctx
        return (pl.BlockSpec((self.tm, width), lambda i: (jnp.minimum(i, last_ctx), 0)),
                pl.BlockSpec((self.tm, width), lambda i: (jnp.maximum(i - n_ctx, 0), 0)))

    def merged(self, width):
        return pl.BlockSpec((self.tm, width), lambda i: (i, 0))

    def mod_spec(self):
        n_ctx, per_seq, ctx_row = self.n_ctx, self.per_seq, self.ctx_row
        return pl.BlockSpec((1, N_MOD, D_MODEL),
                            lambda i: (jnp.where(i < n_ctx, ctx_row, (i - n_ctx) // per_seq), 0, 0))

    def by_pass(self, run_ctx, run_dec):
        i = pl.program_id(0)
        pl.when(i < self.n_ctx)(run_ctx)
        pl.when(i >= self.n_ctx)(run_dec)


def _inproj_kernel(tiles, xc_ref, xd_ref, mod_ref, g_ref, wt_ref, o_ref, w_ref):
    @pl.when(pl.program_id(0) == 0)
    def _():
        for j in range(D_PROJ // LANE):
            if j == CB_LGK:
                blk = jnp.concatenate([wt_ref[D_IN - N_LGK:D_IN, :], jnp.zeros((LANE - N_LGK, D_MODEL), F32)], axis=0)
            else:
                src = j if j < CB_LGK else j - 1
                blk = wt_ref[src * LANE:(src + 1) * LANE, :]
            w_ref[:, j * LANE:(j + 1) * LANE] = blk.T.astype(BF16)

    def run(x_ref):
        m = mod_ref[0]
        h = _rmsnorm_rows(x_ref[...]) * g_ref[...] * (1.0 + m[1:2]) + m[0:1]
        o_ref[...] = _mm(h, w_ref[...])

    tiles.by_pass(functools.partial(run, xc_ref), functools.partial(run, xd_ref))


def _in_projection(tiles, x_ctx, x_dec, mod, norm_g, w_in_t):
    full = lambda a: pl.BlockSpec(a.shape, lambda i: (0,) * a.ndim)
    return pl.pallas_call(
        functools.partial(_inproj_kernel, tiles),
        grid=(tiles.n_ctx + tiles.n_dec,),
        in_specs=[*tiles.specs(D_MODEL), tiles.mod_spec(), full(norm_g),
                  pl.BlockSpec(w_in_t.shape, lambda i: (0, 0), pipeline_mode=pl.Buffered(1))],
        out_specs=tiles.merged(D_PROJ),
        out_shape=jax.ShapeDtypeStruct((x_ctx.shape[0] + x_dec.shape[0], D_PROJ), F32),
        scratch_shapes=[pltpu.VMEM((D_MODEL, D_PROJ), BF16)],
        compiler_params=_cparams(1),
        name="in_proj",
    )(x_ctx, x_dec, mod, norm_g, w_in_t)


def _time_masks(reverse):
    r = _iota((2 * CHUNK, 2 * CHUNK), 0) % CHUNK
    c = _iota((2 * CHUNK, 2 * CHUNK), 1) % CHUNK
    if reverse:
        return r < c, r <= c
    return r > c, r >= c


def _cumsum_matrix(reverse):
    r = _iota((CHUNK, CHUNK), 0)
    c = _iota((CHUNK, CHUNK), 1)
    tri = (r <= c) if reverse else (r >= c)
    return tri.astype(BF16)


def _stack_heads(x, half):
    m0 = _iota(x.shape, 1) < half
    return jnp.concatenate([jnp.where(m0, x, 0.0), jnp.where(m0, 0.0, x)], axis=0)


def _head_sums(x):
    parts = []
    for p in range(x.shape[1] // LANE):
        xp = x[:, p * LANE:(p + 1) * LANE]
        m0 = _iota(xp.shape, 1) < 64
        s0 = jnp.sum(jnp.where(m0, xp, 0.0), axis=-1, keepdims=True)
        s1 = jnp.sum(jnp.where(m0, 0.0, xp), axis=-1, keepdims=True)
        parts.append(jnp.where(m0, s0, s1))
    return parts[0] if len(parts) == 1 else jnp.concatenate(parts, axis=1)


def _rwkv_chunk_terms(insts, interleaved=()):
    c = CHUNK
    step_row = _iota((c, LANE), 0)
    step_col = _iota((c, LANE), 1) % c
    eye_w = (step_row == step_col).astype(F32)
    same_head = (_iota((LANE, LANE), 0) // 64) == (_iota((LANE, LANE), 1) // 64)
    stack_bf = lambda x: _stack_heads(x, 64).astype(BF16)
    cums = [_mm_01_lhs(_cumsum_matrix(rev), lw) for (_, lw, _, _, _, _, rev) in insts]
    pre = []
    for (r, lw, kd, a, b, v, rev), cum in zip(insts, cums):
        end = cum[0:1] if rev else cum[c - 1:c]
        inv_w = jnp.exp(-cum)
        rem_w = jnp.exp(end - cum)
        a_t = a * jnp.exp(cum - lw)
        r_t = r * jnp.exp(cum)
        bk_s = jnp.concatenate([stack_bf(b * inv_w), stack_bf(kd * inv_w)], axis=0)
        bkh = jnp.concatenate([b * rem_w, kd * rem_w], axis=0).astype(BF16)
        pre.append((a_t, r_t, bk_s, bkh, v, jnp.exp(end)))
    ms = [_dot(jnp.concatenate([a_t, r_t], axis=0).astype(BF16), bk_s, _NT) for (a_t, r_t, bk_s, _, _, _) in pre]
    mats = []
    for m, (_, _, _, _, _, _, rev) in zip(ms, insts):
        strict = (step_row < step_col) if rev else (step_row > step_col)
        incl = (step_row <= step_col) if rev else (step_row >= step_col)
        l_ab = jnp.where(strict, m[0:c, 0:LANE], 0.0)
        l_akrk = jnp.concatenate([jnp.where(strict, m[0:c, LANE:2 * LANE], 0.0),
                                  jnp.where(incl, m[c:2 * c, LANE:2 * LANE], 0.0)], axis=0).astype(BF16)
        m_rb = jnp.where(incl, m[c:2 * c, 0:LANE], 0.0).astype(BF16)
        mats.append((l_ab, l_akrk, m_rb))
    pending = list(interleaved)

    def run_interleaved():
        if pending:
            pending.pop(0)()

    lvs = [_dot(l_akrk, stack_bf(pr[4])) for (_, l_akrk, _), pr in zip(mats, pre)]
    run_interleaved()
    ps = [eye_w + l_ab for (l_ab, _, _) in mats]
    lps = [_dot(l_ab.astype(BF16), stack_bf(l_ab)) for (l_ab, _, _) in mats]
    for level in range(1, 6):
        if level < 5:
            xs = [_dot(lp.astype(BF16), jnp.concatenate([stack_bf(p), stack_bf(lp)], axis=1))
                  for lp, p in zip(lps, ps)]
            ps = [p + x[:, 0:LANE] for p, x in zip(ps, xs)]
            lps = [x[:, LANE:2 * LANE] for x in xs]
        else:
            ps = [p + _dot(lp.astype(BF16), stack_bf(p)) for lp, p in zip(lps, ps)]
        if level in (2, 4):
            run_interleaved()
    pxs = [_dot(p.astype(BF16), jnp.concatenate([stack_bf(pr[0]), stack_bf(lv[0:c])], axis=1))
           for p, pr, lv in zip(ps, pre, lvs)]
    run_interleaved()
    mzs = [_dot(mt[2], jnp.concatenate([stack_bf(px[:, 0:LANE]), stack_bf(px[:, LANE:2 * LANE])], axis=1))
           for mt, px in zip(mats, pxs)]
    ts = [_dot(px[:, 0:LANE].astype(BF16), pr[3][0:c], _TN) for px, pr in zip(pxs, pre)]
    gs = [_dot(jnp.concatenate([px[:, LANE:2 * LANE], pr[4]], axis=0).astype(BF16), pr[3], _TN)
          for px, pr in zip(pxs, pre)]
    while pending:
        run_interleaved()
    out = []
    for pr, lv, mz, t, g in zip(pre, lvs, mzs, ts, gs):
        q = pr[1] + mz[:, 0:LANE]
        y0 = mz[:, LANE:2 * LANE] + lv[c:2 * c]
        g_wide = jnp.where(_iota((c, LANE), 1) < 64, g[0:c], g[c:2 * c])
        out.append((jnp.where(same_head, t, 0.0).astype(BF16), g_wide, pr[5],
                    q.astype(BF16), y0))
    return out


def _gla_chunk_terms(insts, interleaved=()):
    c = CHUNK
    step_row = _iota((c, LANE), 0)
    step_col = _iota((c, LANE), 1) % c
    same_head = (_iota((LANE, 2 * LANE), 0) // 64) == (_iota((LANE, 2 * LANE), 1) // LANE)
    cums = [_mm_01_lhs(_cumsum_matrix(rev), g) for (_, _, _, g, rev) in insts]
    pre = []
    for (q, k, v, g, rev), cum in zip(insts, cums):
        end = cum[0:1] if rev else cum[c - 1:c]
        qt = (q * jnp.exp(cum)).astype(BF16)
        k_s = _stack_heads(k * jnp.exp(-cum), 64).astype(BF16)
        kh = (k * jnp.exp(end - cum)).astype(BF16)
        a_col = jnp.broadcast_to(jnp.exp(end), (LANE, LANE)).T
        pre.append((qt, k_s, kh, v.astype(BF16), _stack_heads(v, LANE).astype(BF16), a_col))
    pending = list(interleaved)

    def run_interleaved():
        if pending:
            pending.pop(0)()

    run_interleaved()
    atts = [_dot(pr[0], pr[1], _NT) for pr in pre]
    run_interleaved()
    atts = [jnp.where((step_row <= step_col) if inst[4] else (step_row >= step_col), att, 0.0).astype(BF16)
            for att, inst in zip(atts, insts)]
    o0s = [_dot(att, pr[4]) for att, pr in zip(atts, pre)]
    run_interleaved()
    kvs = [jnp.where(same_head, _dot(pr[2], pr[3], _TN), 0.0) for pr in pre]
    while pending:
        run_interleaved()
    return [(pr[0], o0, pr[5], kv) for pr, o0, kv in zip(pre, o0s, kvs)]


def _rwkv_kernel(seq_len, nseq, is_grid, zero_init, r_ref, k_ref, v_ref, lora_ref, cwr_ref, cwk_ref, cwv_ref,
                 w2_ref, a2_ref, g2_ref, vec_ref, *rest):
    s0_ref = None if zero_init else rest[0]
    (y_ref, sout_ref, pad_ref, rs_ref, ks_ref, vs_ref, kk_ref, bonus_ref, gate_ref, yf_ref, yb_ref, st_ref, tt_ref,
     tg_ref, tw_ref, tq_ref, ty_ref) = rest[0 if zero_init else 1:][:17]
    left_ref, right_ref = rest[-2:] if is_grid else (None, None)
    n_chunks = seq_len // CHUNK
    npp = RW_PAIRS_PER_STEP
    w = npp * LANE
    pair = lambda x, p: x[:, p * LANE:(p + 1) * LANE]
    vec = vec_ref[...]
    w0 = (vec[0:1], vec[1:2])
    a0 = (vec[2:3], vec[3:4])
    k_k, k_a, r_k, ln_w, ln_b = vec[4:5], vec[5:6], vec[6:7], vec[7:8], vec[8:9]
    block_sum = _head_sums
    chains = [(d, p) for d in range(2) for p in range(npp)]
    rows_of = lambda s, c: slice((s * n_chunks + c) * CHUNK, (s * n_chunks + c + 1) * CHUNK)

    zeros = jnp.zeros((CONV_PAD, 3 * w), F32)
    for ref in (pad_ref, left_ref, right_ref) if is_grid else (pad_ref,):
        ref[0:CONV_PAD, :] = zeros
        ref[CONV_PAD + seq_len:2 * CONV_PAD + seq_len, :] = zeros
    cw = jnp.concatenate([cwr_ref[...], cwk_ref[...], cwv_ref[...]], axis=1)
    col = _iota((CHUNK, 3 * w), 0)

    def shift_body(c, carry):
        base = pl.multiple_of(CONV_PAD + c * CHUNK, CHUNK)
        win = pad_ref[pl.ds(base - 8, CHUNK + 16), :]
        left_ref[pl.ds(base, CHUNK), :] = jnp.where(col >= 1, win[7:7 + CHUNK], 0.0)
        right_ref[pl.ds(base, CHUNK), :] = jnp.where(col <= GRID_W - 2, win[9:9 + CHUNK], 0.0)
        return carry

    def conv_chunk(c):
        base = CONV_PAD + c * CHUNK
        acc = jnp.zeros((CHUNK, 3 * w), F32)
        if is_grid:
            for di in (-1, 0, 1):
                row = pl.ds(base + di * GRID_W, CHUNK)
                for dj, src in ((-1, left_ref), (0, pad_ref), (1, right_ref)):
                    tap = (di + 1) * 3 + dj + 1
                    acc = acc + src[row, :] * cw[tap:tap + 1]
        else:
            win = pad_ref[pl.ds(base - 8, CHUNK + 16), :]
            for dj in (-1, 0, 1):
                acc = acc + win[8 + dj:8 + dj + CHUNK] * cw[4 + dj:5 + dj]
        return acc[:, 0:w], acc[:, w:2 * w], acc[:, 2 * w:3 * w]

    def load_sequence(s):
        seq_rows = slice(s * seq_len, (s + 1) * seq_len)
        pad_ref[CONV_PAD:CONV_PAD + seq_len, 0:w] = r_ref[seq_rows, :]
        pad_ref[CONV_PAD:CONV_PAD + seq_len, w:2 * w] = k_ref[seq_rows, :]
        pad_ref[CONV_PAD:CONV_PAD + seq_len, 2 * w:3 * w] = v_ref[seq_rows, :]

    def conv_store(s, c):
        rows = rows_of(s, c)
        rc, kc, vc = conv_chunk(c)
        kk = kc * k_k
        rs_ref[rows, :] = rc
        ks_ref[rows, :] = kc
        vs_ref[rows, :] = vc
        kk_ref[rows, :] = kk * lax.rsqrt(block_sum(kk * kk) + EPS)
        bonus_ref[rows, :] = block_sum(rc * kc * r_k) * vc
        gate_ref[rows, :] = _mm(_sigmoid(lora_ref[rows, 2 * LANE:3 * LANE]), g2_ref[...])

    for s in range(nseq):
        for d, p in chains:
            if zero_init:
                st_ref[s, d, p] = jnp.zeros((64, LANE), F32)
            else:
                st_ref[s, d, p] = jnp.concatenate([s0_ref[s, d, 2 * p], s0_ref[s, d, 2 * p + 1]], axis=1)

    if nseq == 1:
        groups = [(0, list(range(g * TERM_UNROLL, (g + 1) * TERM_UNROLL)),
                   list(range(n_chunks - 1 - g * TERM_UNROLL, n_chunks - 1 - (g + 1) * TERM_UNROLL, -1)))
                  for g in range(n_chunks // TERM_UNROLL)]
    else:
        groups = [(s, list(range(n_chunks)), list(range(n_chunks - 1, -1, -1))) for s in range(nseq)]

    def group_terms(group, interleaved):
        s, fwd, bwd = group
        insts, where, lora_in = [], [], {}
        for d, chunks in ((0, fwd), (1, bwd)):
            for c in chunks:
                rows = rows_of(s, c)
                if c not in lora_in:
                    lora_in[c] = (jnp.tanh(lora_ref[rows, 0:LANE]).astype(BF16),
                                  lora_ref[rows, LANE:2 * LANE].astype(BF16))
                lw = -RW_DECAY_SCALE * _sigmoid(w0[d] + _dot(lora_in[c][0], w2_ref[d]))
                ag = _sigmoid(a0[d] + _dot(lora_in[c][1], a2_ref[d]))
                rc, kc, vc, kk = rs_ref[rows, :], ks_ref[rows, :], vs_ref[rows, :], kk_ref[rows, :]
                kd = kc * (1.0 + (ag - 1.0) * k_a)
                kb = kk * ag
                for p in range(npp):
                    insts.append((pair(rc, p), pair(lw, p), pair(kd, p), -pair(kk, p), pair(kb, p), pair(vc, p),
                                  d == 1))
                    where.append((d, p, s * n_chunks + c))
        for (d, p, gc), (t, g, w_end, q, y0) in zip(where, _rwkv_chunk_terms(insts, interleaved)):
            tt_ref[d, p, gc] = t
            tg_ref[d, p, gc] = g
            tw_ref[d, p, gc] = jnp.broadcast_to(w_end, (8, LANE))
            tq_ref[d, p, gc] = q
            ty_ref[d, p, gc] = y0

    def scan_step(s, chunk_of_dir):
        gcs = [s * n_chunks + chunk_of_dir[d] for d, _ in chains]
        ss = [st_ref[s, d, p] for d, p in chains]
        ys = [_dot(tq_ref[d, p, gc], _stack_heads(x, 64).astype(BF16), _NT) + ty_ref[d, p, gc]
              for (d, p), gc, x in zip(chains, gcs, ss)]
        sn = [x * tw_ref[d, p, gc][0:1] + _dot(x.astype(BF16), tt_ref[d, p, gc]) + tg_ref[d, p, gc]
              for (d, p), gc, x in zip(chains, gcs, ss)]
        for (d, p), gc, y, x in zip(chains, gcs, ys, sn):
            st_ref[s, d, p] = x
            out_ref = yf_ref if d == 0 else yb_ref
            out_ref[gc * CHUNK:(gc + 1) * CHUNK, p * LANE:(p + 1) * LANE] = y

    def group_scan(group):
        s, fwd, bwd = group
        return [functools.partial(scan_step, s, (cf, cb)) for cf, cb in zip(fwd, bwd)]

    conv_done, seq_loaded = set(), set()

    def group_conv(group):
        s, fwd, bwd = group
        todo = [c for c in sorted(set(fwd) | set(bwd)) if (s, c) not in conv_done]
        conv_done.update((s, c) for c in todo)
        thunks = []
        for i in range(0, len(todo), max(1, -(-len(todo) // TERM_UNROLL))):
            part = todo[i:i + max(1, -(-len(todo) // TERM_UNROLL))]
            need_load = s not in seq_loaded
            seq_loaded.add(s)

            def run(part=part, need_load=need_load):
                if need_load:
                    assert not is_grid or nseq == 1
                    load_sequence(s)
                for c in part:
                    conv_store(s, c)

            thunks.append(run)
        return thunks

    def merge(*lists):
        n = max(len(lst) for lst in lists)
        pick = lambda lst, i: lst[i] if i < len(lst) else (lambda: None)
        return [lambda i=i: [pick(lst, i)() for lst in lists] for i in range(n)]

    def post(offs):
        ys = [yf_ref[pl.ds(off, CHUNK), :] + yb_ref[pl.ds(off, CHUNK), :] for off in offs]
        mus = [block_sum(y) * (1.0 / 64) for y in ys]
        dlts = [y - mu for y, mu in zip(ys, mus)]
        vrs = [block_sum(dlt * dlt) * (1.0 / 64) for dlt in dlts]
        for off, dlt, var in zip(offs, dlts, vrs):
            yn = dlt * lax.rsqrt(var + RW_LN_EPS) * ln_w + ln_b
            y_ref[pl.ds(off, CHUNK), :] = (yn + bonus_ref[pl.ds(off, CHUNK), :]) * gate_ref[pl.ds(off, CHUNK), :]

    def sequence_post(s):
        return [functools.partial(post, [(s * n_chunks + c) * CHUNK]) for c in range(n_chunks)]

    if is_grid:
        load_sequence(0)
        seq_loaded.add(0)
        lax.fori_loop(0, n_chunks, shift_body, 0)
    for thunk in group_conv(groups[0]):
        thunk()
    posted = 0
    for g, group in enumerate(groups):
        scans = group_scan(groups[g - 1]) if g else []
        convs = group_conv(groups[g + 1]) if g + 1 < len(groups) else []
        posts = []
        if nseq > 1 and g >= 2:
            posts = sequence_post(groups[g - 2][0])
            posted += 1
        group_terms(group, merge(scans, convs, posts))
    for step in group_scan(groups[-1]):
        step()
    for s in range(nseq):
        for d, p in chains:
            x = st_ref[s, d, p]
            sout_ref[s, d, 2 * p] = x[:, 0:64]
            sout_ref[s, d, 2 * p + 1] = x[:, 64:LANE]

    def post_body(j, carry):
        post([pl.multiple_of((j * TERM_UNROLL + u) * CHUNK, CHUNK) for u in range(TERM_UNROLL)])
        return carry

    lax.fori_loop(posted * n_chunks // TERM_UNROLL, nseq * n_chunks // TERM_UNROLL, post_body, 0)


def _rwkv_mixer(proj, first_seq, n_seq, seq_len, is_grid, prm, s0):
    n_pairs = RW_W // LANE
    n_chunks = seq_len // CHUNK
    npp = RW_PAIRS_PER_STEP
    w = npp * LANE
    nseq = max(1, RW_GROUPS * TERM_UNROLL // n_chunks)
    assert n_chunks % TERM_UNROLL == 0 and n_pairs % npp == 0
    assert (n_chunks == TERM_UNROLL or nseq == 1) and n_seq % nseq == 0 and first_seq % nseq == 0
    rows = nseq * seq_len
    total_chunks = nseq * n_chunks
    first = first_seq // nseq
    col = lambda cb: (lambda b, p: (b + first, cb // npp + p))
    par = lambda cb: (lambda b, p: (0, cb // npp + p))
    state_spec = pl.BlockSpec((nseq, 2, 2 * npp, 64, 64), lambda b, p: (b, 0, p, 0, 0))
    kernel = functools.partial(_rwkv_kernel, seq_len, nseq, is_grid, s0 is None)
    y, s_out = pl.pallas_call(
        kernel,
        grid=(n_seq // nseq, n_pairs // npp),
        in_specs=[pl.BlockSpec((rows, w), col(CB_R)),
                  pl.BlockSpec((rows, w), col(CB_K)),
                  pl.BlockSpec((rows, w), col(CB_V)),
                  pl.BlockSpec((rows, 3 * LANE), lambda b, p: (b + first, CB_LORA // 3)),
                  pl.BlockSpec((9, w), par(CB_R)),
                  pl.BlockSpec((9, w), par(CB_K)),
                  pl.BlockSpec((9, w), par(CB_V)),
                  pl.BlockSpec((2, LANE, w), lambda b, p: (0, 0, p)),
                  pl.BlockSpec((2, LANE, w), lambda b, p: (0, 0, p)),
                  pl.BlockSpec((LANE, w), lambda b, p: (0, p)),
                  pl.BlockSpec((16, w), lambda b, p: (0, p))] + ([] if s0 is None else [state_spec]),
        out_specs=[pl.BlockSpec((rows, w), lambda b, p: (b, p)), state_spec],
        out_shape=[jax.ShapeDtypeStruct((n_seq * seq_len, RW_W), F32),
                   jax.ShapeDtypeStruct((n_seq, 2, 2 * n_pairs, 64, 64), F32)],
        scratch_shapes=[pltpu.VMEM((seq_len + 2 * CONV_PAD, 3 * w), F32)]
                       + [pltpu.VMEM((rows, w), F32)] * 8
                       + [pltpu.VMEM((nseq, 2, npp, 64, LANE), F32),
                          pltpu.VMEM((2, npp, total_chunks, LANE, LANE), BF16),
                          pltpu.VMEM((2, npp, total_chunks, 64, LANE), F32),
                          pltpu.VMEM((2, npp, total_chunks, 8, LANE), F32),
                          pltpu.VMEM((2, npp, total_chunks, CHUNK, LANE), BF16),
                          pltpu.VMEM((2, npp, total_chunks, CHUNK, LANE), F32)]
                       + ([pltpu.VMEM((seq_len + 2 * CONV_PAD, 3 * w), F32)] * 2 if is_grid else []),
        compiler_params=_cparams(2),
        name="rwkv_mixer",
    )(proj, proj, proj, proj, prm['conv'], prm['conv'], prm['conv'], prm['w2p'], prm['a2p'], prm['g2'],
      prm['vec'], *([] if s0 is None else [s0]))
    return y, s_out


def _gla_kernel(seq_len, nseq, zero_init, q_ref, k_ref, v_ref, og_ref, lgk_ref, gk2_ref, gvec_ref, *rest):
    s0_ref = None if zero_init else rest[0]
    y_ref, sout_ref, of_ref, ob_ref, st_ref, tq_ref, to_ref, ta_ref, tkv_ref = rest[0 if zero_init else 1:]
    n_chunks = seq_len // CHUNK
    npp = GLA_PAIRS
    gvec = gvec_ref[...]
    chains = [(d, p) for d in range(2) for p in range(npp)]

    for s in range(nseq):
        for d, p in chains:
            if zero_init:
                st_ref[s, d, p] = jnp.zeros((LANE, 2 * LANE), F32)
            else:
                z = jnp.zeros((64, LANE), F32)
                st_ref[s, d, p] = jnp.concatenate([jnp.concatenate([s0_ref[s, d, 2 * p], z], axis=1),
                                                   jnp.concatenate([z, s0_ref[s, d, 2 * p + 1]], axis=1)], axis=0)

    if nseq == 1:
        groups = [(0, list(range(g * TERM_UNROLL, (g + 1) * TERM_UNROLL)),
                   list(range(n_chunks - 1 - g * TERM_UNROLL, n_chunks - 1 - (g + 1) * TERM_UNROLL, -1)))
                  for g in range(n_chunks // TERM_UNROLL)]
    else:
        groups = [(s, list(range(n_chunks)), list(range(n_chunks - 1, -1, -1))) for s in range(nseq)]

    def group_terms(group, interleaved):
        s, fwd, bwd = group
        insts, where, lgk_in = [], [], {}
        for d, chunks in ((0, fwd), (1, bwd)):
            for c in chunks:
                gc = s * n_chunks + c
                rows = slice(gc * CHUNK, (gc + 1) * CHUNK)
                if c not in lgk_in:
                    lgk_in[c] = lgk_ref[rows, :].astype(BF16)
                x = _dot(lgk_in[c], gk2_ref[d]) + gvec[d:d + 1, 0:GLA_QK_W]
                g = _log_sigmoid(x) * (1.0 / GLA_GATE_NORM)
                qc = q_ref[rows, :] * GLA_Q_SCALE
                kc = k_ref[rows, :]
                vc = v_ref[rows, :]
                for p in range(npp):
                    qk = slice(p * LANE, (p + 1) * LANE)
                    insts.append((qc[:, qk], kc[:, qk], vc[:, 2 * p * LANE:2 * (p + 1) * LANE], g[:, qk], d == 1))
                    where.append((d, p, gc))
        for (d, p, gc), (qt, o0, a_col, kv) in zip(where, _gla_chunk_terms(insts, interleaved)):
            tq_ref[d, p, gc] = qt
            to_ref[d, p, gc] = o0
            ta_ref[d, p, gc] = a_col
            tkv_ref[d, p, gc] = kv

    def scan_step(s, chunk_of_dir):
        gcs = [s * n_chunks + chunk_of_dir[d] for d, _ in chains]
        ss = [st_ref[s, d, p] for d, p in chains]
        os_ = [_dot(tq_ref[d, p, gc], x.astype(BF16)) + to_ref[d, p, gc] for (d, p), gc, x in zip(chains, gcs, ss)]
        for (d, p), gc, x, o in zip(chains, gcs, ss, os_):
            a_col = ta_ref[d, p, gc]
            st_ref[s, d, p] = x * jnp.concatenate([a_col, a_col], axis=1) + tkv_ref[d, p, gc]
            out_ref = of_ref if d == 0 else ob_ref
            out_ref[gc * CHUNK:(gc + 1) * CHUNK, 2 * p * LANE:2 * (p + 1) * LANE] = o

    def group_scan(group):
        s, fwd, bwd = group
        return [functools.partial(scan_step, s, (cf, cb)) for cf, cb in zip(fwd, bwd)]

    def post(off):
        for h in range(2 * npp):
            hs = slice(h * LANE, (h + 1) * LANE)
            o = of_ref[pl.ds(off, CHUNK), hs] + ob_ref[pl.ds(off, CHUNK), hs]
            gate = _silu(og_ref[pl.ds(off, CHUNK), hs])
            y_ref[pl.ds(off, CHUNK), hs] = _rmsnorm_rows(o) * gvec[2:3, hs] * gate

    posted = 0
    for g, group in enumerate(groups):
        thunks = group_scan(groups[g - 1]) if g else []
        if nseq > 1 and g >= 2:
            s_done = groups[g - 2][0]
            posts = [functools.partial(post, (s_done * n_chunks + c) * CHUNK) for c in range(n_chunks)]
            thunks = [lambda a=a, b=b: (a(), b()) for a, b in zip(thunks, posts)]
            posted += 1
        group_terms(group, thunks)
    for step in group_scan(groups[-1]):
        step()
    for s in range(nseq):
        for d, p in chains:
            x = st_ref[s, d, p]
            sout_ref[s, d, 2 * p] = x[0:64, 0:LANE]
            sout_ref[s, d, 2 * p + 1] = x[64:LANE, LANE:2 * LANE]

    def post_body(c, carry):
        post(pl.multiple_of(c * CHUNK, CHUNK))
        return carry

    lax.fori_loop(posted * n_chunks, nseq * n_chunks, post_body, 0)


def _gla_mixer(proj, first_seq, n_seq, seq_len, prm, s0):
    npp = GLA_PAIRS
    n_heads = 2 * npp
    n_chunks = seq_len // CHUNK
    nseq = max(1, RW_GROUPS * TERM_UNROLL // n_chunks)
    assert n_chunks % TERM_UNROLL == 0
    assert (n_chunks == TERM_UNROLL or nseq == 1) and n_seq % nseq == 0 and first_seq % nseq == 0
    rows = nseq * seq_len
    total_chunks = nseq * n_chunks
    first = first_seq // nseq
    state_spec = pl.BlockSpec((nseq, 2, n_heads, 64, LANE), lambda b: (b, 0, 0, 0, 0))
    kernel = functools.partial(_gla_kernel, seq_len, nseq, s0 is None)
    y, s_out = pl.pallas_call(
        kernel,
        grid=(n_seq // nseq,),
        in_specs=[pl.BlockSpec((rows, GLA_QK_W), lambda b: (b + first, CB_GQ * LANE // GLA_QK_W)),
                  pl.BlockSpec((rows, GLA_QK_W), lambda b: (b + first, CB_GK * LANE // GLA_QK_W)),
                  pl.BlockSpec((rows, GLA_V_W), lambda b: (b + first, CB_GV * LANE // GLA_V_W)),
                  pl.BlockSpec((rows, GLA_V_W), lambda b: (b + first, CB_OG * LANE // GLA_V_W)),
                  pl.BlockSpec((rows, LANE), lambda b: (b + first, CB_LGK)),
                  pl.BlockSpec((2, LANE, GLA_QK_W), lambda b: (0, 0, 0)),
                  pl.BlockSpec((8, GLA_V_W), lambda b: (0, 0))] + ([] if s0 is None else [state_spec]),
        out_specs=[pl.BlockSpec((rows, GLA_V_W), lambda b: (b, 0)), state_spec],
        out_shape=[jax.ShapeDtypeStruct((n_seq * seq_len, GLA_V_W), F32),
                   jax.ShapeDtypeStruct((n_seq, 2, n_heads, 64, LANE), F32)],
        scratch_shapes=[pltpu.VMEM((rows, GLA_V_W), F32)] * 2
                       + [pltpu.VMEM((nseq, 2, npp, LANE, 2 * LANE), F32),
                          pltpu.VMEM((2, npp, total_chunks, CHUNK, LANE), BF16),
                          pltpu.VMEM((2, npp, total_chunks, CHUNK, 2 * LANE), F32),
                          pltpu.VMEM((2, npp, total_chunks, LANE, LANE), F32),
                          pltpu.VMEM((2, npp, total_chunks, LANE, 2 * LANE), F32)],
        compiler_params=_cparams(1),
        name="gla_mixer",
    )(proj, proj, proj, proj, proj, prm['gk2p'], prm['gvec'], *([] if s0 is None else [s0]))
    return y, s_out


OUT_TM = 512
ROUTE_NEG = -1e30
LANE_GROUP0 = N_EXPERTS


def _route(logits):
    lane = _iota(logits.shape, 1)
    lane_f = lane.astype(F32)
    big = float(LANE)
    is_g = (lane >= LANE_GROUP0) & (lane < LANE_GROUP0 + 4)
    gmax = jnp.max(jnp.where(is_g, logits, ROUTE_NEG), axis=-1, keepdims=True)
    gidx = jnp.min(jnp.where(is_g & (logits == gmax), lane_f, big), axis=-1, keepdims=True) - LANE_GROUP0
    gsum = jnp.sum(jnp.where(is_g, jnp.exp(jnp.minimum(logits - gmax, 0.0)), 0.0), axis=-1, keepdims=True)
    g_w = 1.0 / gsum
    in_grp = (lane < N_EXPERTS) & ((lane // 4).astype(F32) == gidx)
    m1 = jnp.max(jnp.where(in_grp, logits, ROUTE_NEG), axis=-1, keepdims=True)
    i1 = jnp.min(jnp.where(in_grp & (logits == m1), lane_f, big), axis=-1, keepdims=True)
    rest = in_grp & (lane_f != i1)
    m2 = jnp.max(jnp.where(rest, logits, ROUTE_NEG), axis=-1, keepdims=True)
    i2 = jnp.min(jnp.where(rest & (logits == m2), lane_f, big), axis=-1, keepdims=True)
    t = jnp.exp(m2 - m1)
    w1 = g_w / (1.0 + t)
    return jnp.where(lane_f == i1, w1, 0.0) + jnp.where(lane_f == i2, w1 * t, 0.0)


def _outproj_kernel(tiles, yrc_ref, yrd_ref, ygc_ref, ygd_ref, xc_ref, xd_ref, mod_ref, wo_ref, g_ref, wr_ref,
                    br_ref, x1_ref, h2_ref, cmb_ref):
    def run(yr_ref, yg_ref, x_ref):
        m = mod_ref[0]
        mix = _mm(yr_ref[...], wo_ref[0:RW_W, :]) + _mm(yg_ref[...], wo_ref[RW_W:RW_W + GLA_V_W, :])
        x1 = x_ref[...] + m[2:3] * mix
        h2 = _rmsnorm_rows(x1) * g_ref[...] * (1.0 + m[4:5]) + m[3:4]
        x1_ref[...] = x1
        h2_ref[...] = h2.astype(BF16)
        h_hi, h_lo = _split2(h2)
        both = _dot(h_hi, wr_ref[...])
        logits = both[:, 0:LANE] + both[:, LANE:2 * LANE] + _dot(h_lo, wr_ref[:, 0:LANE])
        cmb_ref[...] = _route(logits + br_ref[...])

    tiles.by_pass(functools.partial(run, yrc_ref, ygc_ref, xc_ref), functools.partial(run, yrd_ref, ygd_ref, xd_ref))


def _out_projection(tiles, y_rw, y_gla, x, mod, w_out, norm_g, w_route, b_route):
    n = x[0].shape[0] + x[1].shape[0]
    full = lambda a: pl.BlockSpec(a.shape, lambda i: (0,) * a.ndim)
    return pl.pallas_call(
        functools.partial(_outproj_kernel, tiles),
        grid=(tiles.n_ctx + tiles.n_dec,),
        in_specs=[*tiles.specs(RW_W), *tiles.specs(GLA_V_W), *tiles.specs(D_MODEL), tiles.mod_spec(),
                  full(w_out), full(norm_g), full(w_route), full(b_route)],
        out_specs=[tiles.merged(D_MODEL), tiles.merged(D_MODEL), tiles.merged(LANE)],
        out_shape=[jax.ShapeDtypeStruct((n, D_MODEL), F32), jax.ShapeDtypeStruct((n, D_MODEL), BF16),
                   jax.ShapeDtypeStruct((n, LANE), F32)],
        compiler_params=_cparams(1),
        name="out_proj_router",
    )(*y_rw, *y_gla, *x, mod, w_out, norm_g, w_route, b_route)


MOE_TM = 512
MOE_RB = 128
MOE_INTERLEAVE = 4
SLOT_ALIGN = 16
MOE_SLOTS = 2 * MOE_TM + N_EXPERTS * SLOT_ALIGN + MOE_RB


def _stage_expert_weights(srcs_hbm, dst_refs, stage_refs, sems):
    def copies(e):
        return [pltpu.make_async_copy(src.at[e], stage.at[e % 2], sem.at[e % 2])
                for src, stage, sem in zip(srcs_hbm, stage_refs, sems)]

    for c in copies(0):
        c.start()
    for e in range(N_EXPERTS):
        if e + 1 < N_EXPERTS:
            for c in copies(e + 1):
                c.start()
        for c, dst, stage in zip(copies(e), dst_refs, stage_refs):
            c.wait()
            dst[e] = stage[e % 2].astype(BF16)


def _moe_kernel(tiles, h2_ref, cmb_ref, x1_ref, mod_ref, w1_hbm, w3_hbm, w2_hbm, fg_ref, yc_ref, yd_ref,
                xs_ref, ys_ref, w1_ref, w3_ref, w2_ref, stage1_ref, stage3_ref, stage2_ref, sem1, sem3, sem2):
    @pl.when(pl.program_id(0) == 0)
    def _():
        _stage_expert_weights((w1_hbm, w3_hbm, w2_hbm), (w1_ref, w3_ref, w2_ref),
                              (stage1_ref, stage3_ref, stage2_ref), (sem1, sem3, sem2))

    cmb = cmb_ref[...]
    lane = _iota(cmb.shape, 1).astype(F32)
    sel = cmb > 0.0
    sel01 = jnp.where(sel, 1.0, 0.0).astype(BF16)
    before = (_iota((MOE_TM, MOE_TM), 0) > _iota((MOE_TM, MOE_TM), 1)).astype(BF16)
    pos = _dot(before, sel01)
    cnt = pos[MOE_TM - 1:MOE_TM] + sel01[MOE_TM - 1:MOE_TM].astype(F32)
    seg = jnp.floor((cnt + (SLOT_ALIGN - 1)) * (1.0 / SLOT_ALIGN))
    lower_experts = (_iota((LANE, LANE), 0) < _iota((LANE, LANE), 1)).astype(BF16)
    start = _dot(jnp.broadcast_to(seg, (8, LANE)).astype(BF16), lower_experts)[0:1] * SLOT_ALIGN
    n_blk = jnp.floor((cnt + (MOE_RB - 1)) * (1.0 / MOE_RB)).astype(jnp.int32)
    start_i = start.astype(jnp.int32)
    cnt_i = cnt.astype(jnp.int32)
    slot = start + pos
    e_a = jnp.min(jnp.where(sel, lane, float(LANE)), axis=-1, keepdims=True)
    e_b = jnp.max(jnp.where(sel, lane, -1.0), axis=-1, keepdims=True)
    pick = lambda e, x: jnp.sum(jnp.where(lane == e, x, 0.0), axis=-1, keepdims=True)
    slot_a, w_a = pick(e_a, slot), pick(e_a, cmb)
    slot_b = jnp.where(e_b != e_a, pick(e_b, slot), -1.0)
    w_b = pick(e_b, cmb)

    slots_t = jnp.where(lane == 0.0, slot_a, jnp.where(lane == 1.0, slot_b, -1.0)).T
    row_slot = _iota((MOE_SLOTS, MOE_TM), 0).astype(F32)
    gather = jnp.where((row_slot == slots_t[0:1]) | (row_slot == slots_t[1:2]), 1.0, 0.0).astype(BF16)
    xs_ref[...] = _dot(gather, h2_ref[...]).astype(BF16)
    ys_ref[...] = jnp.zeros_like(ys_ref)

    row_in_blk = _iota((MOE_RB, D_MODEL), 0)

    def expert_blocks(experts, r0s, ends):
        xbs = [xs_ref[pl.ds(r0, MOE_RB), :] for r0 in r0s]
        gates = [_dot(xb, w3_ref[e]) for xb, e in zip(xbs, experts)]
        ups = [_dot(xb, w1_ref[e]) for xb, e in zip(xbs, experts)]
        acts = [(_silu(g) * u).astype(BF16) for g, u in zip(gates, ups)]
        outs = [_dot(a, w2_ref[e]) for a, e in zip(acts, experts)]
        for r0, end, out in zip(r0s, ends, outs):
            keep = row_in_blk + r0 >= end
            ys_ref[pl.ds(r0, MOE_RB), :] = jnp.where(keep, ys_ref[pl.ds(r0, MOE_RB), :], out.astype(BF16))

    seg_start = [pl.multiple_of(start_i[0, e], SLOT_ALIGN) for e in range(N_EXPERTS)]
    seg_end = [seg_start[e] + cnt_i[0, e] for e in range(N_EXPERTS)]
    for e0 in range(0, N_EXPERTS, MOE_INTERLEAVE):
        es = list(range(e0, e0 + MOE_INTERLEAVE))
        expert_blocks(es, [seg_start[e] for e in es], [seg_end[e] for e in es])
    for e in range(N_EXPERTS):
        def extra_block(b, carry, e=e):
            expert_blocks([e], [pl.multiple_of(seg_start[e] + b * MOE_RB, SLOT_ALIGN)], [seg_end[e]])
            return carry

        lax.fori_loop(1, n_blk[0, e], extra_block, 0)

    col_slot = _iota((MOE_TM, MOE_SLOTS), 1).astype(F32)
    scatter = (jnp.where(col_slot == slot_a, w_a, 0.0) + jnp.where(col_slot == slot_b, w_b, 0.0)).astype(BF16)
    x2 = x1_ref[...] + mod_ref[0][5:6] * _dot(scatter, ys_ref[...])
    y = _rmsnorm_rows(x2) * fg_ref[...]

    def write(y_ref):
        y_ref[...] = y

    tiles.by_pass(functools.partial(write, yc_ref), functools.partial(write, yd_ref))


def _moe(tiles, h2, cmb, x1, mod, w1, w3, w2, final_g):
    assert tiles.tm == MOE_TM
    hbm = pl.BlockSpec(memory_space=pl.ANY)
    out_ctx, out_dec = tiles.specs(D_MODEL)
    return pl.pallas_call(
        functools.partial(_moe_kernel, tiles),
        grid=(tiles.n_ctx + tiles.n_dec,),
        in_specs=[tiles.merged(D_MODEL), tiles.merged(LANE), tiles.merged(D_MODEL), tiles.mod_spec(),
                  hbm, hbm, hbm, pl.BlockSpec((1, D_MODEL), lambda i: (0, 0))],
        out_specs=[out_ctx, out_dec],
        out_shape=[jax.ShapeDtypeStruct((tiles.n_ctx * MOE_TM, D_MODEL), F32),
                   jax.ShapeDtypeStruct((tiles.n_dec * MOE_TM, D_MODEL), F32)],
        scratch_shapes=[pltpu.VMEM((MOE_SLOTS, D_MODEL), BF16), pltpu.VMEM((MOE_SLOTS, D_MODEL), BF16),
                        pltpu.VMEM((N_EXPERTS, D_MODEL, D_EXPERT), BF16),
                        pltpu.VMEM((N_EXPERTS, D_MODEL, D_EXPERT), BF16),
                        pltpu.VMEM((N_EXPERTS, D_EXPERT, D_MODEL), BF16),
                        pltpu.VMEM((2, D_MODEL, D_EXPERT), F32), pltpu.VMEM((2, D_MODEL, D_EXPERT), F32),
                        pltpu.VMEM((2, D_EXPERT, D_MODEL), F32)] + [pltpu.SemaphoreType.DMA((2,))] * 3,
        compiler_params=_cparams(1),
        name="moe_experts",
    )(h2, cmb, x1, mod, w1, w3, w2, final_g)


def _pad_rows(x, rows):
    return jnp.pad(x, ((0, rows - x.shape[0]),) + ((0, 0),) * (x.ndim - 1))


def _pack_params(l, w_in, rw_conv, rw_w0, rw_w2, rw_a0, rw_a2, rw_g2, rw_k_k, rw_k_a, rw_r_k, rw_ln_w, rw_ln_b,
                 gla_gk2, gla_gk_b, gla_norm_g, moe_w_group, moe_b_group, moe_w_expert, moe_b_expert):
    wi = w_in[l]
    z = lambda n: jnp.zeros((D_MODEL, n), F32)
    w_in_t = jnp.swapaxes(wi, 0, 1)
    z64 = jnp.zeros((64, RW_W), F32)
    w2p = jnp.stack([jnp.concatenate([rw_w2[l, 0], z64], 0), jnp.concatenate([z64, rw_w2[l, 1]], 0)])
    a2p = jnp.stack([jnp.concatenate([rw_a2[l, 0], z64], 0), jnp.concatenate([z64, rw_a2[l, 1]], 0)])
    vec = _pad_rows(jnp.stack([rw_w0[l, 0], rw_w0[l, 1], rw_a0[l, 0], rw_a0[l, 1], rw_k_k[l], rw_k_a[l],
                               rw_r_k[l].reshape(RW_W), rw_ln_w[l], rw_ln_b[l]]), 16)
    rw = {'conv': rw_conv[l].reshape(9, 3 * RW_W), 'w2p': w2p.astype(BF16), 'a2p': a2p.astype(BF16),
          'g2': rw_g2[l].astype(BF16), 'vec': vec}
    gk2p = jnp.stack([_pad_rows(gla_gk2[l, 0], LANE),
                      _pad_rows(jnp.concatenate([jnp.zeros((16, GLA_QK_W), F32), gla_gk2[l, 1]], 0), LANE)])
    gk_b = jnp.pad(gla_gk_b[l], ((0, 0), (0, GLA_V_W - GLA_QK_W)))
    gvec = _pad_rows(jnp.concatenate([gk_b, jnp.tile(gla_norm_g[l], GLA_V_W // LANE)[None]], axis=0), 8)
    gla = {'gk2p': gk2p.astype(BF16), 'gvec': gvec}
    w_route = jnp.concatenate(_split2(jnp.concatenate([moe_w_expert[l], moe_w_group[l], z(LANE - N_EXPERTS - 4)],
                                                      axis=1)), axis=1)
    b_route = jnp.concatenate([moe_b_expert[l], moe_b_group[l], jnp.zeros((LANE - N_EXPERTS - 4,), F32)])[None]
    return w_in_t, rw, gla, w_route, b_route


def kernel(x_prompt, x_sample, state_rwkv, state_gla, c, c_ctx, norm1_g, norm2_g, w_ada, b_ada, w_in, w_out,
           rw_conv, rw_w0, rw_w2, rw_a0, rw_a2, rw_g2, rw_k_k, rw_k_a, rw_r_k, rw_ln_w, rw_ln_b,
           gla_gk2, gla_gk_b, gla_norm_g, moe_w_group, moe_b_group, moe_w_expert, moe_b_expert,
           moe_w1, moe_w3, moe_w2, final_g):
    depth = w_in.shape[0]
    assert depth == 1, "the packed layout below handles the single-layer trunk of this problem"
    l = 0
    n_dec = x_sample.shape[0]
    ctx_row = n_dec
    cond8 = _pad_rows(jnp.concatenate([c, c_ctx[None]], axis=0), 8)
    mod = _modulation(cond8, w_ada[l], b_ada[l][None]).reshape(8, N_MOD, D_MODEL)
    pk = _pack_params(l, w_in, rw_conv, rw_w0, rw_w2, rw_a0, rw_a2, rw_g2, rw_k_k, rw_k_a, rw_r_k, rw_ln_w,
                      rw_ln_b, gla_gk2, gla_gk_b, gla_norm_g, moe_w_group, moe_b_group, moe_w_expert,
                      moe_b_expert)
    w_in_t, rw, gla, w_route, b_route = pk

    n_ctx, ctx_len, _ = x_prompt.shape
    dec_len = x_sample.shape[1]
    x_ctx = x_prompt.reshape(n_ctx * ctx_len, D_MODEL)
    x_dec = x_sample.reshape(n_dec * dec_len, D_MODEL)
    assert (n_ctx * ctx_len) % dec_len == 0, "denoising sequences must start on a dec_len row block of proj"
    first_dec = n_ctx * ctx_len // dec_len
    tiles = _Tiles(n_ctx * ctx_len, n_dec * dec_len, dec_len, ctx_row, PROJ_TM)
    assert PROJ_TM == OUT_TM == MOE_TM

    proj = _in_projection(tiles, x_ctx, x_dec, mod, norm1_g[l][None], w_in_t)
    y_rw_c, s_rw = _rwkv_mixer(proj, 0, n_ctx, ctx_len, False, rw, None)
    y_gla_c, s_gla = _gla_mixer(proj, 0, n_ctx, ctx_len, gla, None)
    y_rw_d, _ = _rwkv_mixer(proj, first_dec, n_dec, dec_len, True, rw, state_rwkv[:, l])
    y_gla_d, _ = _gla_mixer(proj, first_dec, n_dec, dec_len, gla, state_gla[:, l])
    x1, h2, cmb = _out_projection(tiles, (y_rw_c, y_rw_d), (y_gla_c, y_gla_d), (x_ctx, x_dec), mod,
                                  w_out[l].astype(BF16), norm2_g[l][None], w_route, b_route)
    y_ctx, y_dec = _moe(tiles, h2, cmb, x1, mod, moe_w1[l], moe_w3[l], moe_w2[l], final_g[None])
    return (y_ctx.reshape(x_prompt.shape), y_dec.reshape(x_sample.shape), s_rw[:, None], s_gla[:, None])
```

```python
import functools
import itertools

import jax
import jax.numpy as jnp
from jax import lax
from jax.experimental import pallas as pl
from jax.experimental.pallas import tpu as pltpu

F32 = jnp.float32
BF16 = jnp.bfloat16

D_MODEL = 1024
RW_W = 512
GLA_V_W = 512
GLA_QK_W = 256
N_EXPERTS = 16
D_EXPERT = 256
N_MOD = 6
EPS = 1e-6
RW_LN_EPS = 64e-5
RW_DECAY_SCALE = 0.606531
GLA_GATE_NORM = 16.0
GLA_Q_SCALE = 64 ** -0.5
GRID_W = 64

LANE = 128
CHUNK = 64
CONV_PAD = 128
TERM_UNROLL = 4
RW_PAIRS_PER_STEP = 2
GLA_PAIRS = 2
RW_GROUPS = 4
D_PROJ = 28 * LANE
VMEM_LIMIT = 56 * 1024 * 1024

CB_R, CB_K, CB_V, CB_LORA, CB_LGK, CB_GQ, CB_GK, CB_GV, CB_OG = 0, 4, 8, 12, 15, 16, 18, 20, 24

_NN = (((1,), (0,)), ((), ()))
_NT = (((1,), (1,)), ((), ()))
_TN = (((0,), (0,)), ((), ()))


def _dot(a, b, dims=_NN):
    return lax.dot_general(a, b, dims, preferred_element_type=F32)


def _mm(a, b, dims=_NN):
    return _dot(a.astype(BF16), b.astype(BF16), dims)


def _split2(x):
    hi = x.astype(BF16)
    lo = (x - hi.astype(F32)).astype(BF16)
    return hi, lo


def _mm3(a, b, dims=_NN):
    ah, al = _split2(a)
    bh, bl = _split2(b)
    return _dot(ah, bh, dims) + _dot(ah, bl, dims) + _dot(al, bh, dims)


def _mm_01_lhs(a01, b, dims=_NN):
    n = b.shape[1]
    both = _dot(a01, jnp.concatenate(_split2(b), axis=1), dims)
    return both[:, 0:n] + both[:, n:2 * n]


def _sigmoid(x):
    return 0.5 * jnp.tanh(0.5 * x) + 0.5


def _silu(x):
    return x * _sigmoid(x)


def _log_sigmoid(x):
    return jnp.minimum(x, 0.0) - jnp.log(1.0 + jnp.exp(-jnp.abs(x)))


def _iota(shape, dim):
    return lax.broadcasted_iota(jnp.int32, shape, dim)


def _cparams(n_axes):
    return pltpu.CompilerParams(dimension_semantics=("arbitrary",) * n_axes, vmem_limit_bytes=VMEM_LIMIT)


MOD_TN = 768


def _mod_kernel(c_ref, w_ref, b_ref, o_ref):
    o_ref[...] = _mm3(_silu(c_ref[...]), w_ref[...]) + b_ref[...]


def _modulation(cond8, w_ada, b_ada):
    n = w_ada.shape[1]
    return pl.pallas_call(
        _mod_kernel,
        grid=(n // MOD_TN,),
        in_specs=[pl.BlockSpec((8, D_MODEL), lambda j: (0, 0)),
                  pl.BlockSpec((D_MODEL, MOD_TN), lambda j: (0, j)),
                  pl.BlockSpec((1, MOD_TN), lambda j: (0, j))],
        out_specs=pl.BlockSpec((8, MOD_TN), lambda j: (0, j)),
        out_shape=jax.ShapeDtypeStruct((8, n), F32),
        compiler_params=_cparams(1),
        name="adaln_mod",
    )(cond8, w_ada, b_ada)


PROJ_TM = 512
D_IN = 3488
N_LGK = 32


def _rmsnorm_rows(x):
    return x * lax.rsqrt(jnp.mean(x * x, axis=-1, keepdims=True) + EPS)


class _Tiles:
    def __init__(self, n_ctx_tokens, n_dec_tokens, dec_seq_len, ctx_row, tm):
        self.tm = tm
        self.n_ctx = n_ctx_tokens // tm
        self.n_dec = n_dec_tokens // tm
        self.per_seq = dec_seq_len // tm
        self.ctx_row = ctx_row

    def specs(self, width):
        last_ctx = self.n_ctx - 1
        n_ctx = self.n_ctx
        return (pl.BlockSpec((self.tm, width), lambda i: (jnp.minimum(i, last_ctx), 0)),
                pl.BlockSpec((self.tm, width), lambda i: (jnp.maximum(i - n_ctx, 0), 0)))

    def merged(self, width):
        return pl.BlockSpec((self.tm, width), lambda i: (i, 0))

    def mod_spec(self):
        n_ctx, per_seq, ctx_row = self.n_ctx, self.per_seq, self.ctx_row
        return pl.BlockSpec((1, N_MOD, D_MODEL),
                            lambda i: (jnp.where(i < n_ctx, ctx_row, (i - n_ctx) // per_seq), 0, 0))

    def by_pass(self, run_ctx, run_dec):
        i = pl.program_id(0)
        pl.when(i < self.n_ctx)(run_ctx)
        pl.when(i >= self.n_ctx)(run_dec)


def _inproj_kernel(tiles, xc_ref, xd_ref, mod_ref, g_ref, wt_ref, o_ref, w_ref):
    @pl.when(pl.program_id(0) == 0)
    def _():
        for j in range(D_PROJ // LANE):
            if j == CB_LGK:
                blk = jnp.concatenate([wt_ref[D_IN - N_LGK:D_IN, :], jnp.zeros((LANE - N_LGK, D_MODEL), F32)], axis=0)
            else:
                src = j if j < CB_LGK else j - 1
                blk = wt_ref[src * LANE:(src + 1) * LANE, :]
            w_ref[:, j * LANE:(j + 1) * LANE] = blk.T.astype(BF16)

    def run(x_ref):
        m = mod_ref[0]
        h = _rmsnorm_rows(x_ref[...]) * g_ref[...] * (1.0 + m[1:2]) + m[0:1]
        o_ref[...] = _mm(h, w_ref[...])

    tiles.by_pass(functools.partial(run, xc_ref), functools.partial(run, xd_ref))


def _in_projection(tiles, x_ctx, x_dec, mod, norm_g, w_in_t):
    full = lambda a: pl.BlockSpec(a.shape, lambda i: (0,) * a.ndim)
    return pl.pallas_call(
        functools.partial(_inproj_kernel, tiles),
        grid=(tiles.n_ctx + tiles.n_dec,),
        in_specs=[*tiles.specs(D_MODEL), tiles.mod_spec(), full(norm_g),
                  pl.BlockSpec(w_in_t.shape, lambda i: (0, 0), pipeline_mode=pl.Buffered(1))],
        out_specs=tiles.merged(D_PROJ),
        out_shape=jax.ShapeDtypeStruct((x_ctx.shape[0] + x_dec.shape[0], D_PROJ), F32),
        scratch_shapes=[pltpu.VMEM((D_MODEL, D_PROJ), BF16)],
        compiler_params=_cparams(1),
        name="in_proj",
    )(x_ctx, x_dec, mod, norm_g, w_in_t)


def _time_masks(reverse):
    r = _iota((2 * CHUNK, 2 * CHUNK), 0) % CHUNK
    c = _iota((2 * CHUNK, 2 * CHUNK), 1) % CHUNK
    if reverse:
        return r < c, r <= c
    return r > c, r >= c


def _cumsum_matrix(reverse):
    r = _iota((CHUNK, CHUNK), 0)
    c = _iota((CHUNK, CHUNK), 1)
    tri = (r <= c) if reverse else (r >= c)
    return tri.astype(BF16)


def _stack_heads(x, half):
    m0 = _iota(x.shape, 1) < half
    return jnp.concatenate([jnp.where(m0, x, 0.0), jnp.where(m0, 0.0, x)], axis=0)


def _finished_before_last_scan(groups, n_chunks):
    done_fwd, done_bwd = set(), set()
    for s, fwd, bwd in groups[:-1]:
        done_fwd.update(s * n_chunks + c for c in fwd)
        done_bwd.update(s * n_chunks + c for c in bwd)
    return sorted(done_fwd & done_bwd)


def _head_sums(x):
    parts = []
    for p in range(x.shape[1] // LANE):
        xp = x[:, p * LANE:(p + 1) * LANE]
        m0 = _iota(xp.shape, 1) < 64
        s0 = jnp.sum(jnp.where(m0, xp, 0.0), axis=-1, keepdims=True)
        s1 = jnp.sum(jnp.where(m0, 0.0, xp), axis=-1, keepdims=True)
        parts.append(jnp.where(m0, s0, s1))
    return parts[0] if len(parts) == 1 else jnp.concatenate(parts, axis=1)


def _rwkv_chunk_terms(insts, interleaved=()):
    c = CHUNK
    step_row = _iota((c, LANE), 0)
    step_col = _iota((c, LANE), 1) % c
    eye_w = (step_row == step_col).astype(F32)
    same_head = (_iota((LANE, LANE), 0) // 64) == (_iota((LANE, LANE), 1) // 64)
    stack_bf = lambda x: _stack_heads(x, 64).astype(BF16)
    cums = [_mm_01_lhs(_cumsum_matrix(rev), lw) for (_, lw, _, _, _, _, rev) in insts]
    pre = []
    for (r, lw, kd, a, b, v, rev), cum in zip(insts, cums):
        end = cum[0:1] if rev else cum[c - 1:c]
        inv_w = jnp.exp(-cum)
        rem_w = jnp.exp(end - cum)
        a_t = a * jnp.exp(cum - lw)
        r_t = r * jnp.exp(cum)
        bk_s = jnp.concatenate([stack_bf(b * inv_w), stack_bf(kd * inv_w)], axis=0)
        bkh = jnp.concatenate([b * rem_w, kd * rem_w], axis=0).astype(BF16)
        pre.append((a_t, r_t, bk_s, bkh, v, jnp.exp(end)))
    ms = [_dot(jnp.concatenate([a_t, r_t], axis=0).astype(BF16), bk_s, _NT) for (a_t, r_t, bk_s, _, _, _) in pre]
    mats = []
    for m, (_, _, _, _, _, _, rev) in zip(ms, insts):
        strict = (step_row < step_col) if rev else (step_row > step_col)
        incl = (step_row <= step_col) if rev else (step_row >= step_col)
        l_ab = jnp.where(strict, m[0:c, 0:LANE], 0.0)
        l_akrk = jnp.concatenate([jnp.where(strict, m[0:c, LANE:2 * LANE], 0.0),
                                  jnp.where(incl, m[c:2 * c, LANE:2 * LANE], 0.0)], axis=0).astype(BF16)
        m_rb = jnp.where(incl, m[c:2 * c, 0:LANE], 0.0).astype(BF16)
        mats.append((l_ab, l_akrk, m_rb))
    pending = list(interleaved)

    def run_interleaved():
        if pending:
            pending.pop(0)()

    lvs = [_dot(l_akrk, stack_bf(pr[4])) for (_, l_akrk, _), pr in zip(mats, pre)]
    run_interleaved()
    ps = [eye_w + l_ab for (l_ab, _, _) in mats]
    lps = [_dot(l_ab.astype(BF16), stack_bf(l_ab)) for (l_ab, _, _) in mats]
    for level in range(1, 6):
        if level < 5:
            xs = [_dot(lp.astype(BF16), jnp.concatenate([stack_bf(p), stack_bf(lp)], axis=1))
                  for lp, p in zip(lps, ps)]
            ps = [p + x[:, 0:LANE] for p, x in zip(ps, xs)]
            lps = [x[:, LANE:2 * LANE] for x in xs]
        else:
            ps = [p + _dot(lp.astype(BF16), stack_bf(p)) for lp, p in zip(lps, ps)]
        if level in (2, 4):
            run_interleaved()
    pxs = [_dot(p.astype(BF16), jnp.concatenate([stack_bf(pr[0]), stack_bf(lv[0:c])], axis=1))
           for p, pr, lv in zip(ps, pre, lvs)]
    run_interleaved()
    mzs = [_dot(mt[2], jnp.concatenate([stack_bf(px[:, 0:LANE]), stack_bf(px[:, LANE:2 * LANE])], axis=1))
           for mt, px in zip(mats, pxs)]
    ts = [_dot(px[:, 0:LANE].astype(BF16), pr[3][0:c], _TN) for px, pr in zip(pxs, pre)]
    gs = [_dot(jnp.concatenate([px[:, LANE:2 * LANE], pr[4]], axis=0).astype(BF16), pr[3], _TN)
          for px, pr in zip(pxs, pre)]
    while pending:
        run_interleaved()
    out = []
    for pr, lv, mz, t, g in zip(pre, lvs, mzs, ts, gs):
        q = pr[1] + mz[:, 0:LANE]
        y0 = mz[:, LANE:2 * LANE] + lv[c:2 * c]
        g_wide = jnp.where(_iota((c, LANE), 1) < 64, g[0:c], g[c:2 * c])
        out.append((jnp.where(same_head, t, 0.0).astype(BF16), g_wide, pr[5],
                    q.astype(BF16), y0))
    return out


def _gla_chunk_terms(insts, interleaved=()):
    c = CHUNK
    step_row = _iota((c, LANE), 0)
    step_col = _iota((c, LANE), 1) % c
    same_head = (_iota((LANE, 2 * LANE), 0) // 64) == (_iota((LANE, 2 * LANE), 1) // LANE)
    cums = [_mm_01_lhs(_cumsum_matrix(rev), g) for (_, _, _, g, rev) in insts]
    pre = []
    for (q, k, v, g, rev), cum in zip(insts, cums):
        end = cum[0:1] if rev else cum[c - 1:c]
        qt = (q * jnp.exp(cum)).astype(BF16)
        k_s = _stack_heads(k * jnp.exp(-cum), 64).astype(BF16)
        kh = (k * jnp.exp(end - cum)).astype(BF16)
        a_col = jnp.broadcast_to(jnp.exp(end), (LANE, LANE)).T
        pre.append((qt, k_s, kh, v.astype(BF16), _stack_heads(v, LANE).astype(BF16), a_col))
    pending = list(interleaved)

    def run_interleaved():
        if pending:
            pending.pop(0)()

    run_interleaved()
    atts = [_dot(pr[0], pr[1], _NT) for pr in pre]
    run_interleaved()
    atts = [jnp.where((step_row <= step_col) if inst[4] else (step_row >= step_col), att, 0.0).astype(BF16)
            for att, inst in zip(atts, insts)]
    o0s = [_dot(att, pr[4]) for att, pr in zip(atts, pre)]
    run_interleaved()
    kvs = [jnp.where(same_head, _dot(pr[2], pr[3], _TN), 0.0) for pr in pre]
    while pending:
        run_interleaved()
    return [(pr[0], o0, pr[5], kv) for pr, o0, kv in zip(pre, o0s, kvs)]


def _rwkv_kernel(seq_len, nseq, is_grid, zero_init, r_ref, k_ref, v_ref, lora_ref, cwr_ref, cwk_ref, cwv_ref,
                 w2_ref, a2_ref, g2_ref, vec_ref, *rest):
    s0_ref = None if zero_init else rest[0]
    (y_ref, sout_ref, pad_ref, rs_ref, ks_ref, vs_ref, kk_ref, bonus_ref, gate_ref, yf_ref, yb_ref, st_ref, tt_ref,
     tg_ref, tw_ref, tq_ref, ty_ref) = rest[0 if zero_init else 1:][:17]
    left_ref, right_ref = rest[-2:] if is_grid else (None, None)
    n_chunks = seq_len // CHUNK
    npp = RW_PAIRS_PER_STEP
    w = npp * LANE
    pair = lambda x, p: x[:, p * LANE:(p + 1) * LANE]
    vec = vec_ref[...]
    w0 = (vec[0:1], vec[1:2])
    a0 = (vec[2:3], vec[3:4])
    k_k, k_a, r_k, ln_w, ln_b = vec[4:5], vec[5:6], vec[6:7], vec[7:8], vec[8:9]
    block_sum = _head_sums
    chains = [(d, p) for d in range(2) for p in range(npp)]
    rows_of = lambda s, c: slice((s * n_chunks + c) * CHUNK, (s * n_chunks + c + 1) * CHUNK)

    zeros = jnp.zeros((CONV_PAD, 3 * w), F32)
    for ref in (pad_ref, left_ref, right_ref) if is_grid else (pad_ref,):
        ref[0:CONV_PAD, :] = zeros
        ref[CONV_PAD + seq_len:2 * CONV_PAD + seq_len, :] = zeros
    cw = jnp.concatenate([cwr_ref[...], cwk_ref[...], cwv_ref[...]], axis=1)
    col = _iota((CHUNK, 3 * w), 0)

    def shift_body(c, carry):
        base = pl.multiple_of(CONV_PAD + c * CHUNK, CHUNK)
        win = pad_ref[pl.ds(base - 8, CHUNK + 16), :]
        left_ref[pl.ds(base, CHUNK), :] = jnp.where(col >= 1, win[7:7 + CHUNK], 0.0)
        right_ref[pl.ds(base, CHUNK), :] = jnp.where(col <= GRID_W - 2, win[9:9 + CHUNK], 0.0)
        return carry

    def conv_chunk(c):
        base = CONV_PAD + c * CHUNK
        acc = jnp.zeros((CHUNK, 3 * w), F32)
        if is_grid:
            for di in (-1, 0, 1):
                row = pl.ds(base + di * GRID_W, CHUNK)
                for dj, src in ((-1, left_ref), (0, pad_ref), (1, right_ref)):
                    tap = (di + 1) * 3 + dj + 1
                    acc = acc + src[row, :] * cw[tap:tap + 1]
        else:
            win = pad_ref[pl.ds(base - 8, CHUNK + 16), :]
            for dj in (-1, 0, 1):
                acc = acc + win[8 + dj:8 + dj + CHUNK] * cw[4 + dj:5 + dj]
        return acc[:, 0:w], acc[:, w:2 * w], acc[:, 2 * w:3 * w]

    def load_sequence(s):
        seq_rows = slice(s * seq_len, (s + 1) * seq_len)
        pad_ref[CONV_PAD:CONV_PAD + seq_len, 0:w] = r_ref[seq_rows, :]
        pad_ref[CONV_PAD:CONV_PAD + seq_len, w:2 * w] = k_ref[seq_rows, :]
        pad_ref[CONV_PAD:CONV_PAD + seq_len, 2 * w:3 * w] = v_ref[seq_rows, :]

    def conv_store(s, c):
        rows = rows_of(s, c)
        rc, kc, vc = conv_chunk(c)
        kk = kc * k_k
        rs_ref[rows, :] = rc
        ks_ref[rows, :] = kc
        vs_ref[rows, :] = vc
        kk_ref[rows, :] = kk * lax.rsqrt(block_sum(kk * kk) + EPS)
        bonus_ref[rows, :] = block_sum(rc * kc * r_k) * vc
        gate_ref[rows, :] = _mm(_sigmoid(lora_ref[rows, 2 * LANE:3 * LANE]), g2_ref[...])

    for s in range(nseq):
        for d, p in chains:
            if zero_init:
                st_ref[s, d, p] = jnp.zeros((64, LANE), F32)
            else:
                st_ref[s, d, p] = jnp.concatenate([s0_ref[s, d, 2 * p], s0_ref[s, d, 2 * p + 1]], axis=1)

    if nseq == 1:
        groups = [(0, list(range(g * TERM_UNROLL, (g + 1) * TERM_UNROLL)),
                   list(range(n_chunks - 1 - g * TERM_UNROLL, n_chunks - 1 - (g + 1) * TERM_UNROLL, -1)))
                  for g in range(n_chunks // TERM_UNROLL)]
    else:
        groups = [(s, list(range(n_chunks)), list(range(n_chunks - 1, -1, -1))) for s in range(nseq)]

    def group_terms(group, interleaved):
        s, fwd, bwd = group
        insts, where, lora_in = [], [], {}
        for d, chunks in ((0, fwd), (1, bwd)):
            for c in chunks:
                rows = rows_of(s, c)
                if c not in lora_in:
                    lora_in[c] = (jnp.tanh(lora_ref[rows, 0:LANE]).astype(BF16),
                                  lora_ref[rows, LANE:2 * LANE].astype(BF16))
                lw = -RW_DECAY_SCALE * _sigmoid(w0[d] + _dot(lora_in[c][0], w2_ref[d]))
                ag = _sigmoid(a0[d] + _dot(lora_in[c][1], a2_ref[d]))
                rc, kc, vc, kk = rs_ref[rows, :], ks_ref[rows, :], vs_ref[rows, :], kk_ref[rows, :]
                kd = kc * (1.0 + (ag - 1.0) * k_a)
                kb = kk * ag
                for p in range(npp):
                    insts.append((pair(rc, p), pair(lw, p), pair(kd, p), -pair(kk, p), pair(kb, p), pair(vc, p),
                                  d == 1))
                    where.append((d, p, s * n_chunks + c))
        for (d, p, gc), (t, g, w_end, q, y0) in zip(where, _rwkv_chunk_terms(insts, interleaved)):
            tt_ref[d, p, gc] = t
            tg_ref[d, p, gc] = g
            tw_ref[d, p, gc] = jnp.broadcast_to(w_end, (8, LANE))
            tq_ref[d, p, gc] = q
            ty_ref[d, p, gc] = y0

    def scan_step(s, chunk_of_dir):
        gcs = [s * n_chunks + chunk_of_dir[d] for d, _ in chains]
        ss = [st_ref[s, d, p] for d, p in chains]
        ys = [_dot(tq_ref[d, p, gc], _stack_heads(x, 64).astype(BF16), _NT) + ty_ref[d, p, gc]
              for (d, p), gc, x in zip(chains, gcs, ss)]
        sn = [x * tw_ref[d, p, gc][0:1] + _dot(x.astype(BF16), tt_ref[d, p, gc]) + tg_ref[d, p, gc]
              for (d, p), gc, x in zip(chains, gcs, ss)]
        for (d, p), gc, y, x in zip(chains, gcs, ys, sn):
            st_ref[s, d, p] = x
            out_ref = yf_ref if d == 0 else yb_ref
            out_ref[gc * CHUNK:(gc + 1) * CHUNK, p * LANE:(p + 1) * LANE] = y

    def group_scan(group):
        s, fwd, bwd = group
        return [functools.partial(scan_step, s, (cf, cb)) for cf, cb in zip(fwd, bwd)]

    conv_done, seq_loaded = set(), set()

    def group_conv(group):
        s, fwd, bwd = group
        todo = [c for c in sorted(set(fwd) | set(bwd)) if (s, c) not in conv_done]
        conv_done.update((s, c) for c in todo)
        thunks = []
        for i in range(0, len(todo), max(1, -(-len(todo) // TERM_UNROLL))):
            part = todo[i:i + max(1, -(-len(todo) // TERM_UNROLL))]
            need_load = s not in seq_loaded
            seq_loaded.add(s)

            def run(part=part, need_load=need_load):
                if need_load:
                    assert not is_grid or nseq == 1
                    load_sequence(s)
                for c in part:
                    conv_store(s, c)

            thunks.append(run)
        return thunks

    def merge(*lists):
        n = max(len(lst) for lst in lists)
        pick = lambda lst, i: lst[i] if i < len(lst) else (lambda: None)
        return [lambda i=i: [pick(lst, i)() for lst in lists] for i in range(n)]

    def post(offs):
        ys = [yf_ref[pl.ds(off, CHUNK), :] + yb_ref[pl.ds(off, CHUNK), :] for off in offs]
        mus = [block_sum(y) * (1.0 / 64) for y in ys]
        dlts = [y - mu for y, mu in zip(ys, mus)]
        vrs = [block_sum(dlt * dlt) * (1.0 / 64) for dlt in dlts]
        for off, dlt, var in zip(offs, dlts, vrs):
            yn = dlt * lax.rsqrt(var + RW_LN_EPS) * ln_w + ln_b
            y_ref[pl.ds(off, CHUNK), :] = (yn + bonus_ref[pl.ds(off, CHUNK), :]) * gate_ref[pl.ds(off, CHUNK), :]

    def sequence_post(s):
        return [functools.partial(post, [(s * n_chunks + c) * CHUNK]) for c in range(n_chunks)]

    if is_grid:
        load_sequence(0)
        seq_loaded.add(0)
        lax.fori_loop(0, n_chunks, shift_body, 0)
    for thunk in group_conv(groups[0]):
        thunk()
    posted = set()
    for g, group in enumerate(groups):
        scans = group_scan(groups[g - 1]) if g else []
        convs = group_conv(groups[g + 1]) if g + 1 < len(groups) else []
        posts = []
        if nseq > 1 and g >= 2:
            s_done = groups[g - 2][0]
            posts = sequence_post(s_done)
            posted.update(range(s_done * n_chunks, (s_done + 1) * n_chunks))
        group_terms(group, merge(scans, convs, posts))
    early = [gc for gc in _finished_before_last_scan(groups, n_chunks) if gc not in posted]
    last_scan = group_scan(groups[-1])
    per_step = max(1, -(-len(early) // len(last_scan)))
    early_posts = [functools.partial(post, [gc * CHUNK for gc in early[i:i + per_step]])
                   for i in range(0, len(early), per_step)]
    for step, extra in itertools.zip_longest(last_scan, early_posts, fillvalue=lambda: None):
        step()
        extra()
    posted.update(early)
    for s in range(nseq):
        for d, p in chains:
            x = st_ref[s, d, p]
            sout_ref[s, d, 2 * p] = x[:, 0:64]
            sout_ref[s, d, 2 * p + 1] = x[:, 64:LANE]

    left = [gc for gc in range(nseq * n_chunks) if gc not in posted]
    for i in range(0, len(left), TERM_UNROLL):
        post([gc * CHUNK for gc in left[i:i + TERM_UNROLL]])


def _rwkv_mixer(proj, first_seq, n_seq, seq_len, is_grid, prm, s0):
    n_pairs = RW_W // LANE
    n_chunks = seq_len // CHUNK
    npp = RW_PAIRS_PER_STEP
    w = npp * LANE
    nseq = max(1, RW_GROUPS * TERM_UNROLL // n_chunks)
    assert n_chunks % TERM_UNROLL == 0 and n_pairs % npp == 0
    assert (n_chunks == TERM_UNROLL or nseq == 1) and n_seq % nseq == 0 and first_seq % nseq == 0
    rows = nseq * seq_len
    total_chunks = nseq * n_chunks
    first = first_seq // nseq
    col = lambda cb: (lambda b, p: (b + first, cb // npp + p))
    par = lambda cb: (lambda b, p: (0, cb // npp + p))
    state_spec = pl.BlockSpec((nseq, 2, 2 * npp, 64, 64), lambda b, p: (b, 0, p, 0, 0))
    kernel = functools.partial(_rwkv_kernel, seq_len, nseq, is_grid, s0 is None)
    y, s_out = pl.pallas_call(
        kernel,
        grid=(n_seq // nseq, n_pairs // npp),
        in_specs=[pl.BlockSpec((rows, w), col(CB_R)),
                  pl.BlockSpec((rows, w), col(CB_K)),
                  pl.BlockSpec((rows, w), col(CB_V)),
                  pl.BlockSpec((rows, 3 * LANE), lambda b, p: (b + first, CB_LORA // 3)),
                  pl.BlockSpec((9, w), par(CB_R)),
                  pl.BlockSpec((9, w), par(CB_K)),
                  pl.BlockSpec((9, w), par(CB_V)),
                  pl.BlockSpec((2, LANE, w), lambda b, p: (0, 0, p)),
                  pl.BlockSpec((2, LANE, w), lambda b, p: (0, 0, p)),
                  pl.BlockSpec((LANE, w), lambda b, p: (0, p)),
                  pl.BlockSpec((16, w), lambda b, p: (0, p))] + ([] if s0 is None else [state_spec]),
        out_specs=[pl.BlockSpec((rows, w), lambda b, p: (b, p)), state_spec],
        out_shape=[jax.ShapeDtypeStruct((n_seq * seq_len, RW_W), F32),
                   jax.ShapeDtypeStruct((n_seq, 2, 2 * n_pairs, 64, 64), F32)],
        scratch_shapes=[pltpu.VMEM((seq_len + 2 * CONV_PAD, 3 * w), F32)]
                       + [pltpu.VMEM((rows, w), F32)] * 8
                       + [pltpu.VMEM((nseq, 2, npp, 64, LANE), F32),
                          pltpu.VMEM((2, npp, total_chunks, LANE, LANE), BF16),
                          pltpu.VMEM((2, npp, total_chunks, 64, LANE), F32),
                          pltpu.VMEM((2, npp, total_chunks, 8, LANE), F32),
                          pltpu.VMEM((2, npp, total_chunks, CHUNK, LANE), BF16),
                          pltpu.VMEM((2, npp, total_chunks, CHUNK, LANE), F32)]
                       + ([pltpu.VMEM((seq_len + 2 * CONV_PAD, 3 * w), F32)] * 2 if is_grid else []),
        compiler_params=_cparams(2),
        name="rwkv_mixer",
    )(proj, proj, proj, proj, prm['conv'], prm['conv'], prm['conv'], prm['w2p'], prm['a2p'], prm['g2'],
      prm['vec'], *([] if s0 is None else [s0]))
    return y, s_out


def _gla_kernel(seq_len, nseq, zero_init, q_ref, k_ref, v_ref, og_ref, lgk_ref, gk2_ref, gvec_ref, *rest):
    s0_ref = None if zero_init else rest[0]
    y_ref, sout_ref, of_ref, ob_ref, st_ref, tq_ref, to_ref, ta_ref, tkv_ref = rest[0 if zero_init else 1:]
    n_chunks = seq_len // CHUNK
    npp = GLA_PAIRS
    gvec = gvec_ref[...]
    chains = [(d, p) for d in range(2) for p in range(npp)]

    for s in range(nseq):
        for d, p in chains:
            if zero_init:
                st_ref[s, d, p] = jnp.zeros((LANE, 2 * LANE), F32)
            else:
                z = jnp.zeros((64, LANE), F32)
                st_ref[s, d, p] = jnp.concatenate([jnp.concatenate([s0_ref[s, d, 2 * p], z], axis=1),
                                                   jnp.concatenate([z, s0_ref[s, d, 2 * p + 1]], axis=1)], axis=0)

    if nseq == 1:
        groups = [(0, list(range(g * TERM_UNROLL, (g + 1) * TERM_UNROLL)),
                   list(range(n_chunks - 1 - g * TERM_UNROLL, n_chunks - 1 - (g + 1) * TERM_UNROLL, -1)))
                  for g in range(n_chunks // TERM_UNROLL)]
    else:
        groups = [(s, list(range(n_chunks)), list(range(n_chunks - 1, -1, -1))) for s in range(nseq)]

    def group_terms(group, interleaved):
        s, fwd, bwd = group
        insts, where, lgk_in = [], [], {}
        for d, chunks in ((0, fwd), (1, bwd)):
            for c in chunks:
                gc = s * n_chunks + c
                rows = slice(gc * CHUNK, (gc + 1) * CHUNK)
                if c not in lgk_in:
                    lgk_in[c] = lgk_ref[rows, :].astype(BF16)
                x = _dot(lgk_in[c], gk2_ref[d]) + gvec[d:d + 1, 0:GLA_QK_W]
                g = _log_sigmoid(x) * (1.0 / GLA_GATE_NORM)
                qc = q_ref[rows, :] * GLA_Q_SCALE
                kc = k_ref[rows, :]
                vc = v_ref[rows, :]
                for p in range(npp):
                    qk = slice(p * LANE, (p + 1) * LANE)
                    insts.append((qc[:, qk], kc[:, qk], vc[:, 2 * p * LANE:2 * (p + 1) * LANE], g[:, qk], d == 1))
                    where.append((d, p, gc))
        for (d, p, gc), (qt, o0, a_col, kv) in zip(where, _gla_chunk_terms(insts, interleaved)):
            tq_ref[d, p, gc] = qt
            to_ref[d, p, gc] = o0
            ta_ref[d, p, gc] = a_col
            tkv_ref[d, p, gc] = kv

    def scan_step(s, chunk_of_dir):
        gcs = [s * n_chunks + chunk_of_dir[d] for d, _ in chains]
        ss = [st_ref[s, d, p] for d, p in chains]
        os_ = [_dot(tq_ref[d, p, gc], x.astype(BF16)) + to_ref[d, p, gc] for (d, p), gc, x in zip(chains, gcs, ss)]
        for (d, p), gc, x, o in zip(chains, gcs, ss, os_):
            a_col = ta_ref[d, p, gc]
            st_ref[s, d, p] = x * jnp.concatenate([a_col, a_col], axis=1) + tkv_ref[d, p, gc]
            out_ref = of_ref if d == 0 else ob_ref
            out_ref[gc * CHUNK:(gc + 1) * CHUNK, 2 * p * LANE:2 * (p + 1) * LANE] = o

    def group_scan(group):
        s, fwd, bwd = group
        return [functools.partial(scan_step, s, (cf, cb)) for cf, cb in zip(fwd, bwd)]

    def post(off):
        for h in range(2 * npp):
            hs = slice(h * LANE, (h + 1) * LANE)
            o = of_ref[pl.ds(off, CHUNK), hs] + ob_ref[pl.ds(off, CHUNK), hs]
            gate = _silu(og_ref[pl.ds(off, CHUNK), hs])
            y_ref[pl.ds(off, CHUNK), hs] = _rmsnorm_rows(o) * gvec[2:3, hs] * gate

    posted = set()
    for g, group in enumerate(groups):
        thunks = group_scan(groups[g - 1]) if g else []
        if nseq > 1 and g >= 2:
            s_done = groups[g - 2][0]
            posts = [functools.partial(post, (s_done * n_chunks + c) * CHUNK) for c in range(n_chunks)]
            thunks = [lambda a=a, b=b: (a(), b()) for a, b in zip(thunks, posts)]
            posted.update(range(s_done * n_chunks, (s_done + 1) * n_chunks))
        group_terms(group, thunks)
    early = [gc for gc in _finished_before_last_scan(groups, n_chunks) if gc not in posted]
    last_scan = group_scan(groups[-1])
    per_step = max(1, -(-len(early) // len(last_scan)))
    for i, step in enumerate(last_scan):
        step()
        for gc in early[i * per_step:(i + 1) * per_step]:
            post(gc * CHUNK)
    for gc in early[len(last_scan) * per_step:]:
        post(gc * CHUNK)
    posted.update(early)
    for s in range(nseq):
        for d, p in chains:
            x = st_ref[s, d, p]
            sout_ref[s, d, 2 * p] = x[0:64, 0:LANE]
            sout_ref[s, d, 2 * p + 1] = x[64:LANE, LANE:2 * LANE]
    for gc in range(nseq * n_chunks):
        if gc not in posted:
            post(gc * CHUNK)


def _gla_mixer(proj, first_seq, n_seq, seq_len, prm, s0):
    npp = GLA_PAIRS
    n_heads = 2 * npp
    n_chunks = seq_len // CHUNK
    nseq = max(1, RW_GROUPS * TERM_UNROLL // n_chunks)
    assert n_chunks % TERM_UNROLL == 0
    assert (n_chunks == TERM_UNROLL or nseq == 1) and n_seq % nseq == 0 and first_seq % nseq == 0
    rows = nseq * seq_len
    total_chunks = nseq * n_chunks
    first = first_seq // nseq
    state_spec = pl.BlockSpec((nseq, 2, n_heads, 64, LANE), lambda b: (b, 0, 0, 0, 0))
    kernel = functools.partial(_gla_kernel, seq_len, nseq, s0 is None)
    y, s_out = pl.pallas_call(
        kernel,
        grid=(n_seq // nseq,),
        in_specs=[pl.BlockSpec((rows, GLA_QK_W), lambda b: (b + first, CB_GQ * LANE // GLA_QK_W)),
                  pl.BlockSpec((rows, GLA_QK_W), lambda b: (b + first, CB_GK * LANE // GLA_QK_W)),
                  pl.BlockSpec((rows, GLA_V_W), lambda b: (b + first, CB_GV * LANE // GLA_V_W)),
                  pl.BlockSpec((rows, GLA_V_W), lambda b: (b + first, CB_OG * LANE // GLA_V_W)),
                  pl.BlockSpec((rows, LANE), lambda b: (b + first, CB_LGK)),
                  pl.BlockSpec((2, LANE, GLA_QK_W), lambda b: (0, 0, 0)),
                  pl.BlockSpec((8, GLA_V_W), lambda b: (0, 0))] + ([] if s0 is None else [state_spec]),
        out_specs=[pl.BlockSpec((rows, GLA_V_W), lambda b: (b, 0)), state_spec],
        out_shape=[jax.ShapeDtypeStruct((n_seq * seq_len, GLA_V_W), F32),
                   jax.ShapeDtypeStruct((n_seq, 2, n_heads, 64, LANE), F32)],
        scratch_shapes=[pltpu.VMEM((rows, GLA_V_W), F32)] * 2
                       + [pltpu.VMEM((nseq, 2, npp, LANE, 2 * LANE), F32),
                          pltpu.VMEM((2, npp, total_chunks, CHUNK, LANE), BF16),
                          pltpu.VMEM((2, npp, total_chunks, CHUNK, 2 * LANE), F32),
                          pltpu.VMEM((2, npp, total_chunks, LANE, LANE), F32),
                          pltpu.VMEM((2, npp, total_chunks, LANE, 2 * LANE), F32)],
        compiler_params=_cparams(1),
        name="gla_mixer",
    )(proj, proj, proj, proj, proj, prm['gk2p'], prm['gvec'], *([] if s0 is None else [s0]))
    return y, s_out


OUT_TM = 512
ROUTE_NEG = -1e30
LANE_GROUP0 = N_EXPERTS


def _route(logits):
    lane = _iota(logits.shape, 1)
    lane_f = lane.astype(F32)
    big = float(LANE)
    is_g = (lane >= LANE_GROUP0) & (lane < LANE_GROUP0 + 4)
    gmax = jnp.max(jnp.where(is_g, logits, ROUTE_NEG), axis=-1, keepdims=True)
    gidx = jnp.min(jnp.where(is_g & (logits == gmax), lane_f, big), axis=-1, keepdims=True) - LANE_GROUP0
    gsum = jnp.sum(jnp.where(is_g, jnp.exp(jnp.minimum(logits - gmax, 0.0)), 0.0), axis=-1, keepdims=True)
    g_w = 1.0 / gsum
    in_grp = (lane < N_EXPERTS) & ((lane // 4).astype(F32) == gidx)
    m1 = jnp.max(jnp.where(in_grp, logits, ROUTE_NEG), axis=-1, keepdims=True)
    i1 = jnp.min(jnp.where(in_grp & (logits == m1), lane_f, big), axis=-1, keepdims=True)
    rest = in_grp & (lane_f != i1)
    m2 = jnp.max(jnp.where(rest, logits, ROUTE_NEG), axis=-1, keepdims=True)
    i2 = jnp.min(jnp.where(rest & (logits == m2), lane_f, big), axis=-1, keepdims=True)
    t = jnp.exp(m2 - m1)
    w1 = g_w / (1.0 + t)
    return jnp.where(lane_f == i1, w1, 0.0) + jnp.where(lane_f == i2, w1 * t, 0.0)


def _outproj_kernel(tiles, yrc_ref, yrd_ref, ygc_ref, ygd_ref, xc_ref, xd_ref, mod_ref, wo_ref, g_ref, wr_ref,
                    br_ref, x1_ref, h2_ref, cmb_ref):
    def run(yr_ref, yg_ref, x_ref):
        m = mod_ref[0]
        mix = _mm(yr_ref[...], wo_ref[0:RW_W, :]) + _mm(yg_ref[...], wo_ref[RW_W:RW_W + GLA_V_W, :])
        x1 = x_ref[...] + m[2:3] * mix
        h2 = _rmsnorm_rows(x1) * g_ref[...] * (1.0 + m[4:5]) + m[3:4]
        x1_ref[...] = x1
        h2_ref[...] = h2.astype(BF16)
        h_hi, h_lo = _split2(h2)
        both = _dot(h_hi, wr_ref[...])
        logits = both[:, 0:LANE] + both[:, LANE:2 * LANE] + _dot(h_lo, wr_ref[:, 0:LANE])
        cmb_ref[...] = _route(logits + br_ref[...])

    tiles.by_pass(functools.partial(run, yrc_ref, ygc_ref, xc_ref), functools.partial(run, yrd_ref, ygd_ref, xd_ref))


def _out_projection(tiles, y_rw, y_gla, x, mod, w_out, norm_g, w_route, b_route):
    n = x[0].shape[0] + x[1].shape[0]
    full = lambda a: pl.BlockSpec(a.shape, lambda i: (0,) * a.ndim)
    return pl.pallas_call(
        functools.partial(_outproj_kernel, tiles),
        grid=(tiles.n_ctx + tiles.n_dec,),
        in_specs=[*tiles.specs(RW_W), *tiles.specs(GLA_V_W), *tiles.specs(D_MODEL), tiles.mod_spec(),
                  full(w_out), full(norm_g), full(w_route), full(b_route)],
        out_specs=[tiles.merged(D_MODEL), tiles.merged(D_MODEL), tiles.merged(LANE)],
        out_shape=[jax.ShapeDtypeStruct((n, D_MODEL), F32), jax.ShapeDtypeStruct((n, D_MODEL), BF16),
                   jax.ShapeDtypeStruct((n, LANE), F32)],
        compiler_params=_cparams(1),
        name="out_proj_router",
    )(*y_rw, *y_gla, *x, mod, w_out, norm_g, w_route, b_route)


MOE_TM = 512
MOE_RB = 128
MOE_INTERLEAVE = 4
SLOT_ALIGN = 16
MOE_SLOTS = 2 * MOE_TM + N_EXPERTS * SLOT_ALIGN + MOE_RB


def _stage_expert_weights(srcs_hbm, dst_refs, stage_refs, sems):
    def copies(e):
        return [pltpu.make_async_copy(src.at[e], stage.at[e % 2], sem.at[e % 2])
                for src, stage, sem in zip(srcs_hbm, stage_refs, sems)]

    for c in copies(0):
        c.start()
    for e in range(N_EXPERTS):
        if e + 1 < N_EXPERTS:
            for c in copies(e + 1):
                c.start()
        for c, dst, stage in zip(copies(e), dst_refs, stage_refs):
            c.wait()
            dst[e] = stage[e % 2].astype(BF16)


def _moe_kernel(tiles, h2_ref, cmb_ref, x1_ref, mod_ref, w1_hbm, w3_hbm, w2_hbm, fg_ref, yc_ref, yd_ref,
                xs_ref, ys_ref, w1_ref, w3_ref, w2_ref, stage1_ref, stage3_ref, stage2_ref, sem1, sem3, sem2):
    @pl.when(pl.program_id(0) == 0)
    def _():
        _stage_expert_weights((w1_hbm, w3_hbm, w2_hbm), (w1_ref, w3_ref, w2_ref),
                              (stage1_ref, stage3_ref, stage2_ref), (sem1, sem3, sem2))

    cmb = cmb_ref[...]
    lane = _iota(cmb.shape, 1).astype(F32)
    sel = cmb > 0.0
    sel01 = jnp.where(sel, 1.0, 0.0).astype(BF16)
    before = (_iota((MOE_TM, MOE_TM), 0) > _iota((MOE_TM, MOE_TM), 1)).astype(BF16)
    pos = _dot(before, sel01)
    cnt = pos[MOE_TM - 1:MOE_TM] + sel01[MOE_TM - 1:MOE_TM].astype(F32)
    seg = jnp.floor((cnt + (SLOT_ALIGN - 1)) * (1.0 / SLOT_ALIGN))
    lower_experts = (_iota((LANE, LANE), 0) < _iota((LANE, LANE), 1)).astype(BF16)
    start = _dot(jnp.broadcast_to(seg, (8, LANE)).astype(BF16), lower_experts)[0:1] * SLOT_ALIGN
    n_blk = jnp.floor((cnt + (MOE_RB - 1)) * (1.0 / MOE_RB)).astype(jnp.int32)
    start_i = start.astype(jnp.int32)
    cnt_i = cnt.astype(jnp.int32)
    slot = start + pos
    e_a = jnp.min(jnp.where(sel, lane, float(LANE)), axis=-1, keepdims=True)
    e_b = jnp.max(jnp.where(sel, lane, -1.0), axis=-1, keepdims=True)
    pick = lambda e, x: jnp.sum(jnp.where(lane == e, x, 0.0), axis=-1, keepdims=True)
    slot_a, w_a = pick(e_a, slot), pick(e_a, cmb)
    slot_b = jnp.where(e_b != e_a, pick(e_b, slot), -1.0)
    w_b = pick(e_b, cmb)

    slots_t = jnp.where(lane == 0.0, slot_a, jnp.where(lane == 1.0, slot_b, -1.0)).T
    row_slot = _iota((MOE_SLOTS, MOE_TM), 0).astype(F32)
    gather = jnp.where((row_slot == slots_t[0:1]) | (row_slot == slots_t[1:2]), 1.0, 0.0).astype(BF16)
    xs_ref[...] = _dot(gather, h2_ref[...]).astype(BF16)
    ys_ref[...] = jnp.zeros_like(ys_ref)

    row_in_blk = _iota((MOE_RB, D_MODEL), 0)

    def expert_blocks(experts, r0s, ends):
        xbs = [xs_ref[pl.ds(r0, MOE_RB), :] for r0 in r0s]
        gates = [_dot(xb, w3_ref[e]) for xb, e in zip(xbs, experts)]
        ups = [_dot(xb, w1_ref[e]) for xb, e in zip(xbs, experts)]
        acts = [(_silu(g) * u).astype(BF16) for g, u in zip(gates, ups)]
        outs = [_dot(a, w2_ref[e]) for a, e in zip(acts, experts)]
        for r0, end, out in zip(r0s, ends, outs):
            keep = row_in_blk + r0 >= end
            ys_ref[pl.ds(r0, MOE_RB), :] = jnp.where(keep, ys_ref[pl.ds(r0, MOE_RB), :], out.astype(BF16))

    seg_start = [pl.multiple_of(start_i[0, e], SLOT_ALIGN) for e in range(N_EXPERTS)]
    seg_end = [seg_start[e] + cnt_i[0, e] for e in range(N_EXPERTS)]
    for e0 in range(0, N_EXPERTS, MOE_INTERLEAVE):
        es = list(range(e0, e0 + MOE_INTERLEAVE))
        expert_blocks(es, [seg_start[e] for e in es], [seg_end[e] for e in es])
    for e in range(N_EXPERTS):
        def extra_block(b, carry, e=e):
            expert_blocks([e], [pl.multiple_of(seg_start[e] + b * MOE_RB, SLOT_ALIGN)], [seg_end[e]])
            return carry

        lax.fori_loop(1, n_blk[0, e], extra_block, 0)

    col_slot = _iota((MOE_TM, MOE_SLOTS), 1).astype(F32)
    scatter = (jnp.where(col_slot == slot_a, w_a, 0.0) + jnp.where(col_slot == slot_b, w_b, 0.0)).astype(BF16)
    x2 = x1_ref[...] + mod_ref[0][5:6] * _dot(scatter, ys_ref[...])
    y = _rmsnorm_rows(x2) * fg_ref[...]

    def write(y_ref):
        y_ref[...] = y

    tiles.by_pass(functools.partial(write, yc_ref), functools.partial(write, yd_ref))


def _moe(tiles, h2, cmb, x1, mod, w1, w3, w2, final_g):
    assert tiles.tm == MOE_TM
    hbm = pl.BlockSpec(memory_space=pl.ANY)
    out_ctx, out_dec = tiles.specs(D_MODEL)
    return pl.pallas_call(
        functools.partial(_moe_kernel, tiles),
        grid=(tiles.n_ctx + tiles.n_dec,),
        in_specs=[tiles.merged(D_MODEL), tiles.merged(LANE), tiles.merged(D_MODEL), tiles.mod_spec(),
                  hbm, hbm, hbm, pl.BlockSpec((1, D_MODEL), lambda i: (0, 0))],
        out_specs=[out_ctx, out_dec],
        out_shape=[jax.ShapeDtypeStruct((tiles.n_ctx * MOE_TM, D_MODEL), F32),
                   jax.ShapeDtypeStruct((tiles.n_dec * MOE_TM, D_MODEL), F32)],
        scratch_shapes=[pltpu.VMEM((MOE_SLOTS, D_MODEL), BF16), pltpu.VMEM((MOE_SLOTS, D_MODEL), BF16),
                        pltpu.VMEM((N_EXPERTS, D_MODEL, D_EXPERT), BF16),
                        pltpu.VMEM((N_EXPERTS, D_MODEL, D_EXPERT), BF16),
                        pltpu.VMEM((N_EXPERTS, D_EXPERT, D_MODEL), BF16),
                        pltpu.VMEM((2, D_MODEL, D_EXPERT), F32), pltpu.VMEM((2, D_MODEL, D_EXPERT), F32),
                        pltpu.VMEM((2, D_EXPERT, D_MODEL), F32)] + [pltpu.SemaphoreType.DMA((2,))] * 3,
        compiler_params=_cparams(1),
        name="moe_experts",
    )(h2, cmb, x1, mod, w1, w3, w2, final_g)


def _pad_rows(x, rows):
    return jnp.pad(x, ((0, rows - x.shape[0]),) + ((0, 0),) * (x.ndim - 1))


def _pack_params(l, w_in, rw_conv, rw_w0, rw_w2, rw_a0, rw_a2, rw_g2, rw_k_k, rw_k_a, rw_r_k, rw_ln_w, rw_ln_b,
                 gla_gk2, gla_gk_b, gla_norm_g, moe_w_group, moe_b_group, moe_w_expert, moe_b_expert):
    wi = w_in[l]
    z = lambda n: jnp.zeros((D_MODEL, n), F32)
    w_in_t = jnp.swapaxes(wi, 0, 1)
    z64 = jnp.zeros((64, RW_W), F32)
    w2p = jnp.stack([jnp.concatenate([rw_w2[l, 0], z64], 0), jnp.concatenate([z64, rw_w2[l, 1]], 0)])
    a2p = jnp.stack([jnp.concatenate([rw_a2[l, 0], z64], 0), jnp.concatenate([z64, rw_a2[l, 1]], 0)])
    vec = _pad_rows(jnp.stack([rw_w0[l, 0], rw_w0[l, 1], rw_a0[l, 0], rw_a0[l, 1], rw_k_k[l], rw_k_a[l],
                               rw_r_k[l].reshape(RW_W), rw_ln_w[l], rw_ln_b[l]]), 16)
    rw = {'conv': rw_conv[l].reshape(9, 3 * RW_W), 'w2p': w2p.astype(BF16), 'a2p': a2p.astype(BF16),
          'g2': rw_g2[l].astype(BF16), 'vec': vec}
    gk2p = jnp.stack([_pad_rows(gla_gk2[l, 0], LANE),
                      _pad_rows(jnp.concatenate([jnp.zeros((16, GLA_QK_W), F32), gla_gk2[l, 1]], 0), LANE)])
    gk_b = jnp.pad(gla_gk_b[l], ((0, 0), (0, GLA_V_W - GLA_QK_W)))
    gvec = _pad_rows(jnp.concatenate([gk_b, jnp.tile(gla_norm_g[l], GLA_V_W // LANE)[None]], axis=0), 8)
    gla = {'gk2p': gk2p.astype(BF16), 'gvec': gvec}
    w_route = jnp.concatenate(_split2(jnp.concatenate([moe_w_expert[l], moe_w_group[l], z(LANE - N_EXPERTS - 4)],
                                                      axis=1)), axis=1)
    b_route = jnp.concatenate([moe_b_expert[l], moe_b_group[l], jnp.zeros((LANE - N_EXPERTS - 4,), F32)])[None]
    return w_in_t, rw, gla, w_route, b_route


def kernel(x_prompt, x_sample, state_rwkv, state_gla, c, c_ctx, norm1_g, norm2_g, w_ada, b_ada, w_in, w_out,
           rw_conv, rw_w0, rw_w2, rw_a0, rw_a2, rw_g2, rw_k_k, rw_k_a, rw_r_k, rw_ln_w, rw_ln_b,
           gla_gk2, gla_gk_b, gla_norm_g, moe_w_group, moe_b_group, moe_w_expert, moe_b_expert,
           moe_w1, moe_w3, moe_w2, final_g):
    depth = w_in.shape[0]
    assert depth == 1, "the packed layout below handles the single-layer trunk of this problem"
    l = 0
    n_dec = x_sample.shape[0]
    ctx_row = n_dec
    cond8 = _pad_rows(jnp.concatenate([c, c_ctx[None]], axis=0), 8)
    mod = _modulation(cond8, w_ada[l], b_ada[l][None]).reshape(8, N_MOD, D_MODEL)
    pk = _pack_params(l, w_in, rw_conv, rw_w0, rw_w2, rw_a0, rw_a2, rw_g2, rw_k_k, rw_k_a, rw_r_k, rw_ln_w,
                      rw_ln_b, gla_gk2, gla_gk_b, gla_norm_g, moe_w_group, moe_b_group, moe_w_expert,
                      moe_b_expert)
    w_in_t, rw, gla, w_route, b_route = pk

    n_ctx, ctx_len, _ = x_prompt.shape
    dec_len = x_sample.shape[1]
    x_ctx = x_prompt.reshape(n_ctx * ctx_len, D_MODEL)
    x_dec = x_sample.reshape(n_dec * dec_len, D_MODEL)
    assert (n_ctx * ctx_len) % dec_len == 0, "denoising sequences must start on a dec_len row block of proj"
    first_dec = n_ctx * ctx_len // dec_len
    tiles = _Tiles(n_ctx * ctx_len, n_dec * dec_len, dec_len, ctx_row, PROJ_TM)
    assert PROJ_TM == OUT_TM == MOE_TM

    proj = _in_projection(tiles, x_ctx, x_dec, mod, norm1_g[l][None], w_in_t)
    y_rw_c, s_rw = _rwkv_mixer(proj, 0, n_ctx, ctx_len, False, rw, None)
    y_gla_c, s_gla = _gla_mixer(proj, 0, n_ctx, ctx_len, gla, None)
    y_rw_d, _ = _rwkv_mixer(proj, first_dec, n_dec, dec_len, True, rw, state_rwkv[:, l])
    y_gla_d, _ = _gla_mixer(proj, first_dec, n_dec, dec_len, gla, state_gla[:, l])
    x1, h2, cmb = _out_projection(tiles, (y_rw_c, y_rw_d), (y_gla_c, y_gla_d), (x_ctx, x_dec), mod,
                                  w_out[l].astype(BF16), norm2_g[l][None], w_route, b_route)
    y_ctx, y_dec = _moe(tiles, h2, cmb, x1, mod, moe_w1[l], moe_w3[l], moe_w2[l], final_g[None])
    return (y_ctx.reshape(x_prompt.shape), y_dec.reshape(x_sample.shape), s_rw[:, None], s_gla[:, None])
```

```python
import functools
import itertools

import jax
import jax.numpy as jnp
from jax import lax
from jax.experimental import pallas as pl
from jax.experimental.pallas import tpu as pltpu

F32 = jnp.float32
BF16 = jnp.bfloat16

D_MODEL = 1024
RW_W = 512
GLA_V_W = 512
GLA_QK_W = 256
N_EXPERTS = 16
D_EXPERT = 256
N_MOD = 6
EPS = 1e-6
RW_LN_EPS = 64e-5
RW_DECAY_SCALE = 0.606531
GLA_GATE_NORM = 16.0
GLA_Q_SCALE = 64 ** -0.5
GRID_W = 64

LANE = 128
CHUNK = 64
CONV_PAD = 128
TERM_UNROLL = 4
RW_PAIRS_PER_STEP = 2
GLA_PAIRS = 2
RW_GROUPS = 4
D_PROJ = 28 * LANE
VMEM_LIMIT = 56 * 1024 * 1024

CB_R, CB_K, CB_V, CB_LORA, CB_LGK, CB_GQ, CB_GK, CB_GV, CB_OG = 0, 4, 8, 12, 15, 16, 18, 20, 24

_NN = (((1,), (0,)), ((), ()))
_NT = (((1,), (1,)), ((), ()))
_TN = (((0,), (0,)), ((), ()))


def _dot(a, b, dims=_NN):
    return lax.dot_general(a, b, dims, preferred_element_type=F32)


def _mm(a, b, dims=_NN):
    return _dot(a.astype(BF16), b.astype(BF16), dims)


def _split2(x):
    hi = x.astype(BF16)
    lo = (x - hi.astype(F32)).astype(BF16)
    return hi, lo


def _mm3(a, b, dims=_NN):
    ah, al = _split2(a)
    bh, bl = _split2(b)
    return _dot(ah, bh, dims) + _dot(ah, bl, dims) + _dot(al, bh, dims)


def _mm_01_lhs(a01, b, dims=_NN):
    n = b.shape[1]
    both = _dot(a01, jnp.concatenate(_split2(b), axis=1), dims)
    return both[:, 0:n] + both[:, n:2 * n]


def _sigmoid(x):
    return 0.5 * jnp.tanh(0.5 * x) + 0.5


def _silu(x):
    return x * _sigmoid(x)


def _log_sigmoid(x):
    return jnp.minimum(x, 0.0) - jnp.log(1.0 + jnp.exp(-jnp.abs(x)))


def _iota(shape, dim):
    return lax.broadcasted_iota(jnp.int32, shape, dim)


def _cparams(n_axes):
    return pltpu.CompilerParams(dimension_semantics=("arbitrary",) * n_axes, vmem_limit_bytes=VMEM_LIMIT)


MOD_TN = 768


def _mod_kernel(c_ref, w_ref, b_ref, o_ref):
    o_ref[...] = _mm3(_silu(c_ref[...]), w_ref[...]) + b_ref[...]


def _modulation(cond8, w_ada, b_ada):
    n = w_ada.shape[1]
    return pl.pallas_call(
        _mod_kernel,
        grid=(n // MOD_TN,),
        in_specs=[pl.BlockSpec((8, D_MODEL), lambda j: (0, 0)),
                  pl.BlockSpec((D_MODEL, MOD_TN), lambda j: (0, j)),
                  pl.BlockSpec((1, MOD_TN), lambda j: (0, j))],
        out_specs=pl.BlockSpec((8, MOD_TN), lambda j: (0, j)),
        out_shape=jax.ShapeDtypeStruct((8, n), F32),
        compiler_params=_cparams(1),
        name="adaln_mod",
    )(cond8, w_ada, b_ada)


PROJ_TM = 512
D_IN = 3488
N_LGK = 32


def _rmsnorm_rows(x):
    return x * lax.rsqrt(jnp.mean(x * x, axis=-1, keepdims=True) + EPS)


class _Tiles:
    def __init__(self, n_ctx_tokens, n_dec_tokens, dec_seq_len, ctx_row, tm):
        self.tm = tm
        self.n_ctx = n_ctx_tokens // tm
        self.n_dec = n_dec_tokens // tm
        self.per_seq = dec_seq_len // tm
        self.ctx_row = ctx_row

    def specs(self, width):
        last_ctx = self.n_ctx - 1
        n_ctx = self.n_ctx
        return (pl.BlockSpec((self.tm, width), lambda i: (jnp.minimum(i, last_ctx), 0)),
                pl.BlockSpec((self.tm, width), lambda i: (jnp.maximum(i - n_ctx, 0), 0)))

    def merged(self, width):
        return pl.BlockSpec((self.tm, width), lambda i: (i, 0))

    def mod_spec(self):
        n_ctx, per_seq, ctx_row = self.n_ctx, self.per_seq, self.ctx_row
        return pl.BlockSpec((1, N_MOD, D_MODEL),
                            lambda i: (jnp.where(i < n_ctx, ctx_row, (i - n_ctx) // per_seq), 0, 0))

    def by_pass(self, run_ctx, run_dec):
        i = pl.program_id(0)
        pl.when(i < self.n_ctx)(run_ctx)
        pl.when(i >= self.n_ctx)(run_dec)


def _inproj_kernel(tiles, xc_ref, xd_ref, mod_ref, g_ref, wt_ref, o_ref, w_ref):
    @pl.when(pl.program_id(0) == 0)
    def _():
        for j in range(D_PROJ // LANE):
            if j == CB_LGK:
                blk = jnp.concatenate([wt_ref[D_IN - N_LGK:D_IN, :], jnp.zeros((LANE - N_LGK, D_MODEL), F32)], axis=0)
            else:
                src = j if j < CB_LGK else j - 1
                blk = wt_ref[src * LANE:(src + 1) * LANE, :]
            w_ref[:, j * LANE:(j + 1) * LANE] = blk.T.astype(BF16)

    def run(x_ref):
        m = mod_ref[0]
        h = _rmsnorm_rows(x_ref[...]) * g_ref[...] * (1.0 + m[1:2]) + m[0:1]
        o_ref[...] = _mm(h, w_ref[...])

    tiles.by_pass(functools.partial(run, xc_ref), functools.partial(run, xd_ref))


def _in_projection(tiles, x_ctx, x_dec, mod, norm_g, w_in_t):
    full = lambda a: pl.BlockSpec(a.shape, lambda i: (0,) * a.ndim)
    return pl.pallas_call(
        functools.partial(_inproj_kernel, tiles),
        grid=(tiles.n_ctx + tiles.n_dec,),
        in_specs=[*tiles.specs(D_MODEL), tiles.mod_spec(), full(norm_g),
                  pl.BlockSpec(w_in_t.shape, lambda i: (0, 0), pipeline_mode=pl.Buffered(1))],
        out_specs=tiles.merged(D_PROJ),
        out_shape=jax.ShapeDtypeStruct((x_ctx.shape[0] + x_dec.shape[0], D_PROJ), F32),
        scratch_shapes=[pltpu.VMEM((D_MODEL, D_PROJ), BF16)],
        compiler_params=_cparams(1),
        name="in_proj",
    )(x_ctx, x_dec, mod, norm_g, w_in_t)


def _time_masks(reverse):
    r = _iota((2 * CHUNK, 2 * CHUNK), 0) % CHUNK
    c = _iota((2 * CHUNK, 2 * CHUNK), 1) % CHUNK
    if reverse:
        return r < c, r <= c
    return r > c, r >= c


def _cumsum_matrix(reverse):
    r = _iota((CHUNK, CHUNK), 0)
    c = _iota((CHUNK, CHUNK), 1)
    tri = (r <= c) if reverse else (r >= c)
    return tri.astype(BF16)


def _stack_heads(x, half):
    m0 = _iota(x.shape, 1) < half
    return jnp.concatenate([jnp.where(m0, x, 0.0), jnp.where(m0, 0.0, x)], axis=0)


def _finished_before_last_scan(groups, n_chunks):
    done_fwd, done_bwd = set(), set()
    for s, fwd, bwd in groups[:-1]:
        done_fwd.update(s * n_chunks + c for c in fwd)
        done_bwd.update(s * n_chunks + c for c in bwd)
    return sorted(done_fwd & done_bwd)


def _head_sums(x):
    parts = []
    for p in range(x.shape[1] // LANE):
        xp = x[:, p * LANE:(p + 1) * LANE]
        m0 = _iota(xp.shape, 1) < 64
        s0 = jnp.sum(jnp.where(m0, xp, 0.0), axis=-1, keepdims=True)
        s1 = jnp.sum(jnp.where(m0, 0.0, xp), axis=-1, keepdims=True)
        parts.append(jnp.where(m0, s0, s1))
    return parts[0] if len(parts) == 1 else jnp.concatenate(parts, axis=1)


def _rwkv_chunk_terms(insts, interleaved=()):
    c = CHUNK
    step_row = _iota((c, LANE), 0)
    step_col = _iota((c, LANE), 1) % c
    eye_w = (step_row == step_col).astype(F32)
    same_head = (_iota((LANE, LANE), 0) // 64) == (_iota((LANE, LANE), 1) // 64)
    stack_bf = lambda x: _stack_heads(x, 64).astype(BF16)
    cums = [_mm_01_lhs(_cumsum_matrix(rev), lw) for (_, lw, _, _, _, _, rev) in insts]
    pre = []
    for (r, lw, kd, a, b, v, rev), cum in zip(insts, cums):
        end = cum[0:1] if rev else cum[c - 1:c]
        inv_w = jnp.exp(-cum)
        rem_w = jnp.exp(end - cum)
        a_t = a * jnp.exp(cum - lw)
        r_t = r * jnp.exp(cum)
        bk_s = jnp.concatenate([stack_bf(b * inv_w), stack_bf(kd * inv_w)], axis=0)
        bkh = jnp.concatenate([b * rem_w, kd * rem_w], axis=0).astype(BF16)
        pre.append((a_t, r_t, bk_s, bkh, v, jnp.exp(end)))
    ms = [_dot(jnp.concatenate([a_t, r_t], axis=0).astype(BF16), bk_s, _NT) for (a_t, r_t, bk_s, _, _, _) in pre]
    mats = []
    for m, (_, _, _, _, _, _, rev) in zip(ms, insts):
        strict = (step_row < step_col) if rev else (step_row > step_col)
        incl = (step_row <= step_col) if rev else (step_row >= step_col)
        l_ab = jnp.where(strict, m[0:c, 0:LANE], 0.0)
        l_akrk = jnp.concatenate([jnp.where(strict, m[0:c, LANE:2 * LANE], 0.0),
                                  jnp.where(incl, m[c:2 * c, LANE:2 * LANE], 0.0)], axis=0).astype(BF16)
        m_rb = jnp.where(incl, m[c:2 * c, 0:LANE], 0.0).astype(BF16)
        mats.append((l_ab, l_akrk, m_rb))
    pending = list(interleaved)

    def run_interleaved():
        if pending:
            pending.pop(0)()

    lvs = [_dot(l_akrk, stack_bf(pr[4])) for (_, l_akrk, _), pr in zip(mats, pre)]
    run_interleaved()
    ps = [eye_w + l_ab for (l_ab, _, _) in mats]
    lps = [_dot(l_ab.astype(BF16), stack_bf(l_ab)) for (l_ab, _, _) in mats]
    for level in range(1, 6):
        if level < 5:
            xs = [_dot(lp.astype(BF16), jnp.concatenate([stack_bf(p), stack_bf(lp)], axis=1))
                  for lp, p in zip(lps, ps)]
            ps = [p + x[:, 0:LANE] for p, x in zip(ps, xs)]
            lps = [x[:, LANE:2 * LANE] for x in xs]
        else:
            ps = [p + _dot(lp.astype(BF16), stack_bf(p)) for lp, p in zip(lps, ps)]
        if level in (2, 4):
            run_interleaved()
    pxs = [_dot(p.astype(BF16), jnp.concatenate([stack_bf(pr[0]), stack_bf(lv[0:c])], axis=1))
           for p, pr, lv in zip(ps, pre, lvs)]
    run_interleaved()
    mzs = [_dot(mt[2], jnp.concatenate([stack_bf(px[:, 0:LANE]), stack_bf(px[:, LANE:2 * LANE])], axis=1))
           for mt, px in zip(mats, pxs)]
    ts = [_dot(px[:, 0:LANE].astype(BF16), pr[3][0:c], _TN) for px, pr in zip(pxs, pre)]
    gs = [_dot(jnp.concatenate([px[:, LANE:2 * LANE], pr[4]], axis=0).astype(BF16), pr[3], _TN)
          for px, pr in zip(pxs, pre)]
    while pending:
        run_interleaved()
    out = []
    for pr, lv, mz, t, g in zip(pre, lvs, mzs, ts, gs):
        q = pr[1] + mz[:, 0:LANE]
        y0 = mz[:, LANE:2 * LANE] + lv[c:2 * c]
        g_wide = jnp.where(_iota((c, LANE), 1) < 64, g[0:c], g[c:2 * c])
        out.append((jnp.where(same_head, t, 0.0).astype(BF16), g_wide, pr[5],
                    q.astype(BF16), y0))
    return out


def _gla_chunk_terms(insts, interleaved=()):
    c = CHUNK
    step_row = _iota((c, LANE), 0)
    step_col = _iota((c, LANE), 1) % c
    same_head = (_iota((LANE, 2 * LANE), 0) // 64) == (_iota((LANE, 2 * LANE), 1) // LANE)
    cums = [_mm_01_lhs(_cumsum_matrix(rev), g) for (_, _, _, g, rev) in insts]
    pre = []
    for (q, k, v, g, rev), cum in zip(insts, cums):
        end = cum[0:1] if rev else cum[c - 1:c]
        qt = (q * jnp.exp(cum)).astype(BF16)
        k_s = _stack_heads(k * jnp.exp(-cum), 64).astype(BF16)
        kh = (k * jnp.exp(end - cum)).astype(BF16)
        a_col = jnp.broadcast_to(jnp.exp(end), (LANE, LANE)).T
        pre.append((qt, k_s, kh, v.astype(BF16), _stack_heads(v, LANE).astype(BF16), a_col))
    pending = list(interleaved)

    def run_interleaved():
        if pending:
            pending.pop(0)()

    run_interleaved()
    atts = [_dot(pr[0], pr[1], _NT) for pr in pre]
    run_interleaved()
    atts = [jnp.where((step_row <= step_col) if inst[4] else (step_row >= step_col), att, 0.0).astype(BF16)
            for att, inst in zip(atts, insts)]
    o0s = [_dot(att, pr[4]) for att, pr in zip(atts, pre)]
    run_interleaved()
    kvs = [jnp.where(same_head, _dot(pr[2], pr[3], _TN), 0.0) for pr in pre]
    while pending:
        run_interleaved()
    return [(pr[0], o0, pr[5], kv) for pr, o0, kv in zip(pre, o0s, kvs)]


def _rwkv_kernel(seq_len, nseq, is_grid, zero_init, r_ref, k_ref, v_ref, lora_ref, cwr_ref, cwk_ref, cwv_ref,
                 w2_ref, a2_ref, g2_ref, vec_ref, *rest):
    s0_ref = None if zero_init else rest[0]
    (y_ref, sout_ref, pad_ref, rs_ref, ks_ref, vs_ref, kk_ref, bonus_ref, gate_ref, yf_ref, yb_ref, st_ref, tt_ref,
     tg_ref, tw_ref, tq_ref, ty_ref) = rest[0 if zero_init else 1:][:17]
    left_ref, right_ref = rest[-2:] if is_grid else (None, None)
    n_chunks = seq_len // CHUNK
    npp = RW_PAIRS_PER_STEP
    w = npp * LANE
    pair = lambda x, p: x[:, p * LANE:(p + 1) * LANE]
    vec = vec_ref[...]
    w0 = (vec[0:1], vec[1:2])
    a0 = (vec[2:3], vec[3:4])
    k_k, k_a, r_k, ln_w, ln_b = vec[4:5], vec[5:6], vec[6:7], vec[7:8], vec[8:9]
    block_sum = _head_sums
    chains = [(d, p) for d in range(2) for p in range(npp)]
    rows_of = lambda s, c: slice((s * n_chunks + c) * CHUNK, (s * n_chunks + c + 1) * CHUNK)

    zeros = jnp.zeros((CONV_PAD, 3 * w), F32)
    for ref in (pad_ref, left_ref, right_ref) if is_grid else (pad_ref,):
        ref[0:CONV_PAD, :] = zeros
        ref[CONV_PAD + seq_len:2 * CONV_PAD + seq_len, :] = zeros
    cw = jnp.concatenate([cwr_ref[...], cwk_ref[...], cwv_ref[...]], axis=1)
    col = _iota((CHUNK, 3 * w), 0)

    def shift_body(c, carry):
        base = pl.multiple_of(CONV_PAD + c * CHUNK, CHUNK)
        win = pad_ref[pl.ds(base - 8, CHUNK + 16), :]
        left_ref[pl.ds(base, CHUNK), :] = jnp.where(col >= 1, win[7:7 + CHUNK], 0.0)
        right_ref[pl.ds(base, CHUNK), :] = jnp.where(col <= GRID_W - 2, win[9:9 + CHUNK], 0.0)
        return carry

    def conv_chunk(c):
        base = CONV_PAD + c * CHUNK
        acc = jnp.zeros((CHUNK, 3 * w), F32)
        if is_grid:
            for di in (-1, 0, 1):
                row = pl.ds(base + di * GRID_W, CHUNK)
                for dj, src in ((-1, left_ref), (0, pad_ref), (1, right_ref)):
                    tap = (di + 1) * 3 + dj + 1
                    acc = acc + src[row, :] * cw[tap:tap + 1]
        else:
            win = pad_ref[pl.ds(base - 8, CHUNK + 16), :]
            for dj in (-1, 0, 1):
                acc = acc + win[8 + dj:8 + dj + CHUNK] * cw[4 + dj:5 + dj]
        return acc[:, 0:w], acc[:, w:2 * w], acc[:, 2 * w:3 * w]

    def load_sequence(s):
        seq_rows = slice(s * seq_len, (s + 1) * seq_len)
        pad_ref[CONV_PAD:CONV_PAD + seq_len, 0:w] = r_ref[seq_rows, :]
        pad_ref[CONV_PAD:CONV_PAD + seq_len, w:2 * w] = k_ref[seq_rows, :]
        pad_ref[CONV_PAD:CONV_PAD + seq_len, 2 * w:3 * w] = v_ref[seq_rows, :]

    def conv_store(s, c):
        rows = rows_of(s, c)
        rc, kc, vc = conv_chunk(c)
        kk = kc * k_k
        rs_ref[rows, :] = rc
        ks_ref[rows, :] = kc
        vs_ref[rows, :] = vc
        kk_ref[rows, :] = kk * lax.rsqrt(block_sum(kk * kk) + EPS)
        bonus_ref[rows, :] = block_sum(rc * kc * r_k) * vc
        gate_ref[rows, :] = _mm(_sigmoid(lora_ref[rows, 2 * LANE:3 * LANE]), g2_ref[...])

    for s in range(nseq):
        for d, p in chains:
            if zero_init:
                st_ref[s, d, p] = jnp.zeros((64, LANE), F32)
            else:
                st_ref[s, d, p] = jnp.concatenate([s0_ref[s, d, 2 * p], s0_ref[s, d, 2 * p + 1]], axis=1)

    if nseq == 1:
        groups = [(0, list(range(g * TERM_UNROLL, (g + 1) * TERM_UNROLL)),
                   list(range(n_chunks - 1 - g * TERM_UNROLL, n_chunks - 1 - (g + 1) * TERM_UNROLL, -1)))
                  for g in range(n_chunks // TERM_UNROLL)]
    else:
        groups = [(s, list(range(n_chunks)), list(range(n_chunks - 1, -1, -1))) for s in range(nseq)]

    def group_terms(group, interleaved):
        s, fwd, bwd = group
        insts, where, lora_in = [], [], {}
        for d, chunks in ((0, fwd), (1, bwd)):
            for c in chunks:
                rows = rows_of(s, c)
                if c not in lora_in:
                    lora_in[c] = (jnp.tanh(lora_ref[rows, 0:LANE]).astype(BF16),
                                  lora_ref[rows, LANE:2 * LANE].astype(BF16))
                lw = -RW_DECAY_SCALE * _sigmoid(w0[d] + _dot(lora_in[c][0], w2_ref[d]))
                ag = _sigmoid(a0[d] + _dot(lora_in[c][1], a2_ref[d]))
                rc, kc, vc, kk = rs_ref[rows, :], ks_ref[rows, :], vs_ref[rows, :], kk_ref[rows, :]
                kd = kc * (1.0 + (ag - 1.0) * k_a)
                kb = kk * ag
                for p in range(npp):
                    insts.append((pair(rc, p), pair(lw, p), pair(kd, p), -pair(kk, p), pair(kb, p), pair(vc, p),
                                  d == 1))
                    where.append((d, p, s * n_chunks + c))
        for (d, p, gc), (t, g, w_end, q, y0) in zip(where, _rwkv_chunk_terms(insts, interleaved)):
            tt_ref[d, p, gc] = t
            tg_ref[d, p, gc] = g
            tw_ref[d, p, gc] = jnp.broadcast_to(w_end, (8, LANE))
            tq_ref[d, p, gc] = q
            ty_ref[d, p, gc] = y0

    def scan_step(s, chunk_of_dir):
        gcs = [s * n_chunks + chunk_of_dir[d] for d, _ in chains]
        ss = [st_ref[s, d, p] for d, p in chains]
        ys = [_dot(tq_ref[d, p, gc], _stack_heads(x, 64).astype(BF16), _NT) + ty_ref[d, p, gc]
              for (d, p), gc, x in zip(chains, gcs, ss)]
        sn = [x * tw_ref[d, p, gc][0:1] + _dot(x.astype(BF16), tt_ref[d, p, gc]) + tg_ref[d, p, gc]
              for (d, p), gc, x in zip(chains, gcs, ss)]
        for (d, p), gc, y, x in zip(chains, gcs, ys, sn):
            st_ref[s, d, p] = x
            out_ref = yf_ref if d == 0 else yb_ref
            out_ref[gc * CHUNK:(gc + 1) * CHUNK, p * LANE:(p + 1) * LANE] = y

    def group_scan(group):
        s, fwd, bwd = group
        return [functools.partial(scan_step, s, (cf, cb)) for cf, cb in zip(fwd, bwd)]

    conv_done, seq_loaded = set(), set()

    def group_conv(group):
        s, fwd, bwd = group
        todo = [c for c in sorted(set(fwd) | set(bwd)) if (s, c) not in conv_done]
        conv_done.update((s, c) for c in todo)
        thunks = []
        for i in range(0, len(todo), max(1, -(-len(todo) // TERM_UNROLL))):
            part = todo[i:i + max(1, -(-len(todo) // TERM_UNROLL))]
            need_load = s not in seq_loaded
            seq_loaded.add(s)

            def run(part=part, need_load=need_load):
                if need_load:
                    assert not is_grid or nseq == 1
                    load_sequence(s)
                for c in part:
                    conv_store(s, c)

            thunks.append(run)
        return thunks

    def merge(*lists):
        n = max(len(lst) for lst in lists)
        pick = lambda lst, i: lst[i] if i < len(lst) else (lambda: None)
        return [lambda i=i: [pick(lst, i)() for lst in lists] for i in range(n)]

    def post(offs):
        ys = [yf_ref[pl.ds(off, CHUNK), :] + yb_ref[pl.ds(off, CHUNK), :] for off in offs]
        mus = [block_sum(y) * (1.0 / 64) for y in ys]
        dlts = [y - mu for y, mu in zip(ys, mus)]
        vrs = [block_sum(dlt * dlt) * (1.0 / 64) for dlt in dlts]
        for off, dlt, var in zip(offs, dlts, vrs):
            yn = dlt * lax.rsqrt(var + RW_LN_EPS) * ln_w + ln_b
            y_ref[pl.ds(off, CHUNK), :] = (yn + bonus_ref[pl.ds(off, CHUNK), :]) * gate_ref[pl.ds(off, CHUNK), :]

    def sequence_post(s):
        return [functools.partial(post, [(s * n_chunks + c) * CHUNK]) for c in range(n_chunks)]

    if is_grid:
        load_sequence(0)
        seq_loaded.add(0)
        lax.fori_loop(0, n_chunks, shift_body, 0)
    for thunk in group_conv(groups[0]):
        thunk()
    posted = set()
    for g, group in enumerate(groups):
        scans = group_scan(groups[g - 1]) if g else []
        convs = group_conv(groups[g + 1]) if g + 1 < len(groups) else []
        posts = []
        if nseq > 1 and g >= 2:
            s_done = groups[g - 2][0]
            posts = sequence_post(s_done)
            posted.update(range(s_done * n_chunks, (s_done + 1) * n_chunks))
        group_terms(group, merge(scans, convs, posts))
    early = [gc for gc in _finished_before_last_scan(groups, n_chunks) if gc not in posted]
    last_scan = group_scan(groups[-1])
    per_step = max(1, -(-len(early) // len(last_scan)))
    early_posts = [functools.partial(post, [gc * CHUNK for gc in early[i:i + per_step]])
                   for i in range(0, len(early), per_step)]
    for step, extra in itertools.zip_longest(last_scan, early_posts, fillvalue=lambda: None):
        step()
        extra()
    posted.update(early)
    for s in range(nseq):
        for d, p in chains:
            x = st_ref[s, d, p]
            sout_ref[s, d, 2 * p] = x[:, 0:64]
            sout_ref[s, d, 2 * p + 1] = x[:, 64:LANE]

    left = [gc for gc in range(nseq * n_chunks) if gc not in posted]
    for i in range(0, len(left), TERM_UNROLL):
        post([gc * CHUNK for gc in left[i:i + TERM_UNROLL]])


def _rwkv_mixer(proj, first_seq, n_seq, seq_len, is_grid, prm, s0):
    n_pairs = RW_W // LANE
    n_chunks = seq_len // CHUNK
    npp = RW_PAIRS_PER_STEP
    w = npp * LANE
    nseq = max(1, RW_GROUPS * TERM_UNROLL // n_chunks)
    assert n_chunks % TERM_UNROLL == 0 and n_pairs % npp == 0
    assert (n_chunks == TERM_UNROLL or nseq == 1) and n_seq % nseq == 0 and first_seq % nseq == 0
    rows = nseq * seq_len
    total_chunks = nseq * n_chunks
    first = first_seq // nseq
    col = lambda cb: (lambda b, p: (b + first, cb // npp + p))
    par = lambda cb: (lambda b, p: (0, cb // npp + p))
    state_spec = pl.BlockSpec((nseq, 2, 2 * npp, 64, 64), lambda b, p: (b, 0, p, 0, 0))
    kernel = functools.partial(_rwkv_kernel, seq_len, nseq, is_grid, s0 is None)
    y, s_out = pl.pallas_call(
        kernel,
        grid=(n_seq // nseq, n_pairs // npp),
        in_specs=[pl.BlockSpec((rows, w), col(CB_R)),
                  pl.BlockSpec((rows, w), col(CB_K)),
                  pl.BlockSpec((rows, w), col(CB_V)),
                  pl.BlockSpec((rows, 3 * LANE), lambda b, p: (b + first, CB_LORA // 3)),
                  pl.BlockSpec((9, w), par(CB_R)),
                  pl.BlockSpec((9, w), par(CB_K)),
                  pl.BlockSpec((9, w), par(CB_V)),
                  pl.BlockSpec((2, LANE, w), lambda b, p: (0, 0, p)),
                  pl.BlockSpec((2, LANE, w), lambda b, p: (0, 0, p)),
                  pl.BlockSpec((LANE, w), lambda b, p: (0, p)),
                  pl.BlockSpec((16, w), lambda b, p: (0, p))] + ([] if s0 is None else [state_spec]),
        out_specs=[pl.BlockSpec((rows, w), lambda b, p: (b, p)), state_spec],
        out_shape=[jax.ShapeDtypeStruct((n_seq * seq_len, RW_W), F32),
                   jax.ShapeDtypeStruct((n_seq, 2, 2 * n_pairs, 64, 64), F32)],
        scratch_shapes=[pltpu.VMEM((seq_len + 2 * CONV_PAD, 3 * w), F32)]
                       + [pltpu.VMEM((rows, w), F32)] * 8
                       + [pltpu.VMEM((nseq, 2, npp, 64, LANE), F32),
                          pltpu.VMEM((2, npp, total_chunks, LANE, LANE), BF16),
                          pltpu.VMEM((2, npp, total_chunks, 64, LANE), F32),
                          pltpu.VMEM((2, npp, total_chunks, 8, LANE), F32),
                          pltpu.VMEM((2, npp, total_chunks, CHUNK, LANE), BF16),
                          pltpu.VMEM((2, npp, total_chunks, CHUNK, LANE), F32)]
                       + ([pltpu.VMEM((seq_len + 2 * CONV_PAD, 3 * w), F32)] * 2 if is_grid else []),
        compiler_params=_cparams(2),
        name="rwkv_mixer",
    )(proj, proj, proj, proj, prm['conv'], prm['conv'], prm['conv'], prm['w2p'], prm['a2p'], prm['g2'],
      prm['vec'], *([] if s0 is None else [s0]))
    return y, s_out


def _gla_kernel(seq_len, nseq, zero_init, q_ref, k_ref, v_ref, og_ref, lgk_ref, gk2_ref, gvec_ref, *rest):
    s0_ref = None if zero_init else rest[0]
    y_ref, sout_ref, of_ref, ob_ref, st_ref, tq_ref, to_ref, ta_ref, tkv_ref = rest[0 if zero_init else 1:]
    n_chunks = seq_len // CHUNK
    npp = GLA_PAIRS
    gvec = gvec_ref[...]
    chains = [(d, p) for d in range(2) for p in range(npp)]

    for s in range(nseq):
        for d, p in chains:
            if zero_init:
                st_ref[s, d, p] = jnp.zeros((LANE, 2 * LANE), F32)
            else:
                z = jnp.zeros((64, LANE), F32)
                st_ref[s, d, p] = jnp.concatenate([jnp.concatenate([s0_ref[s, d, 2 * p], z], axis=1),
                                                   jnp.concatenate([z, s0_ref[s, d, 2 * p + 1]], axis=1)], axis=0)

    if nseq == 1:
        groups = [(0, list(range(g * TERM_UNROLL, (g + 1) * TERM_UNROLL)),
                   list(range(n_chunks - 1 - g * TERM_UNROLL, n_chunks - 1 - (g + 1) * TERM_UNROLL, -1)))
                  for g in range(n_chunks // TERM_UNROLL)]
    else:
        groups = [(s, list(range(n_chunks)), list(range(n_chunks - 1, -1, -1))) for s in range(nseq)]

    def group_terms(group, interleaved):
        s, fwd, bwd = group
        insts, where, lgk_in = [], [], {}
        for d, chunks in ((0, fwd), (1, bwd)):
            for c in chunks:
                gc = s * n_chunks + c
                rows = slice(gc * CHUNK, (gc + 1) * CHUNK)
                if c not in lgk_in:
                    lgk_in[c] = lgk_ref[rows, :].astype(BF16)
                x = _dot(lgk_in[c], gk2_ref[d]) + gvec[d:d + 1, 0:GLA_QK_W]
                g = _log_sigmoid(x) * (1.0 / GLA_GATE_NORM)
                qc = q_ref[rows, :] * GLA_Q_SCALE
                kc = k_ref[rows, :]
                vc = v_ref[rows, :]
                for p in range(npp):
                    qk = slice(p * LANE, (p + 1) * LANE)
                    insts.append((qc[:, qk], kc[:, qk], vc[:, 2 * p * LANE:2 * (p + 1) * LANE], g[:, qk], d == 1))
                    where.append((d, p, gc))
        for (d, p, gc), (qt, o0, a_col, kv) in zip(where, _gla_chunk_terms(insts, interleaved)):
            tq_ref[d, p, gc] = qt
            to_ref[d, p, gc] = o0
            ta_ref[d, p, gc] = a_col
            tkv_ref[d, p, gc] = kv

    def scan_step(s, chunk_of_dir):
        gcs = [s * n_chunks + chunk_of_dir[d] for d, _ in chains]
        ss = [st_ref[s, d, p] for d, p in chains]
        os_ = [_dot(tq_ref[d, p, gc], x.astype(BF16)) + to_ref[d, p, gc] for (d, p), gc, x in zip(chains, gcs, ss)]
        for (d, p), gc, x, o in zip(chains, gcs, ss, os_):
            a_col = ta_ref[d, p, gc]
            st_ref[s, d, p] = x * jnp.concatenate([a_col, a_col], axis=1) + tkv_ref[d, p, gc]
            out_ref = of_ref if d == 0 else ob_ref
            out_ref[gc * CHUNK:(gc + 1) * CHUNK, 2 * p * LANE:2 * (p + 1) * LANE] = o

    def group_scan(group):
        s, fwd, bwd = group
        return [functools.partial(scan_step, s, (cf, cb)) for cf, cb in zip(fwd, bwd)]

    def post(off):
        for h in range(2 * npp):
            hs = slice(h * LANE, (h + 1) * LANE)
            o = of_ref[pl.ds(off, CHUNK), hs] + ob_ref[pl.ds(off, CHUNK), hs]
            gate = _silu(og_ref[pl.ds(off, CHUNK), hs])
            y_ref[pl.ds(off, CHUNK), hs] = _rmsnorm_rows(o) * gvec[2:3, hs] * gate

    posted = set()
    for g, group in enumerate(groups):
        thunks = group_scan(groups[g - 1]) if g else []
        if nseq > 1 and g >= 2:
            s_done = groups[g - 2][0]
            posts = [functools.partial(post, (s_done * n_chunks + c) * CHUNK) for c in range(n_chunks)]
            thunks = [lambda a=a, b=b: (a(), b()) for a, b in zip(thunks, posts)]
            posted.update(range(s_done * n_chunks, (s_done + 1) * n_chunks))
        group_terms(group, thunks)
    early = [gc for gc in _finished_before_last_scan(groups, n_chunks) if gc not in posted]
    last_scan = group_scan(groups[-1])
    per_step = max(1, -(-len(early) // len(last_scan)))
    for i, step in enumerate(last_scan):
        step()
        for gc in early[i * per_step:(i + 1) * per_step]:
            post(gc * CHUNK)
    for gc in early[len(last_scan) * per_step:]:
        post(gc * CHUNK)
    posted.update(early)
    for s in range(nseq):
        for d, p in chains:
            x = st_ref[s, d, p]
            sout_ref[s, d, 2 * p] = x[0:64, 0:LANE]
            sout_ref[s, d, 2 * p + 1] = x[64:LANE, LANE:2 * LANE]
    for gc in range(nseq * n_chunks):
        if gc not in posted:
            post(gc * CHUNK)


def _gla_mixer(proj, first_seq, n_seq, seq_len, prm, s0):
    npp = GLA_PAIRS
    n_heads = 2 * npp
    n_chunks = seq_len // CHUNK
    nseq = max(1, RW_GROUPS * TERM_UNROLL // n_chunks)
    assert n_chunks % TERM_UNROLL == 0
    assert (n_chunks == TERM_UNROLL or nseq == 1) and n_seq % nseq == 0 and first_seq % nseq == 0
    rows = nseq * seq_len
    total_chunks = nseq * n_chunks
    first = first_seq // nseq
    state_spec = pl.BlockSpec((nseq, 2, n_heads, 64, LANE), lambda b: (b, 0, 0, 0, 0))
    kernel = functools.partial(_gla_kernel, seq_len, nseq, s0 is None)
    y, s_out = pl.pallas_call(
        kernel,
        grid=(n_seq // nseq,),
        in_specs=[pl.BlockSpec((rows, GLA_QK_W), lambda b: (b + first, CB_GQ * LANE // GLA_QK_W)),
                  pl.BlockSpec((rows, GLA_QK_W), lambda b: (b + first, CB_GK * LANE // GLA_QK_W)),
                  pl.BlockSpec((rows, GLA_V_W), lambda b: (b + first, CB_GV * LANE // GLA_V_W)),
                  pl.BlockSpec((rows, GLA_V_W), lambda b: (b + first, CB_OG * LANE // GLA_V_W)),
                  pl.BlockSpec((rows, LANE), lambda b: (b + first, CB_LGK)),
                  pl.BlockSpec((2, LANE, GLA_QK_W), lambda b: (0, 0, 0)),
                  pl.BlockSpec((8, GLA_V_W), lambda b: (0, 0))] + ([] if s0 is None else [state_spec]),
        out_specs=[pl.BlockSpec((rows, GLA_V_W), lambda b: (b, 0)), state_spec],
        out_shape=[jax.ShapeDtypeStruct((n_seq * seq_len, GLA_V_W), F32),
                   jax.ShapeDtypeStruct((n_seq, 2, n_heads, 64, LANE), F32)],
        scratch_shapes=[pltpu.VMEM((rows, GLA_V_W), F32)] * 2
                       + [pltpu.VMEM((nseq, 2, npp, LANE, 2 * LANE), F32),
                          pltpu.VMEM((2, npp, total_chunks, CHUNK, LANE), BF16),
                          pltpu.VMEM((2, npp, total_chunks, CHUNK, 2 * LANE), F32),
                          pltpu.VMEM((2, npp, total_chunks, LANE, LANE), F32),
                          pltpu.VMEM((2, npp, total_chunks, LANE, 2 * LANE), F32)],
        compiler_params=_cparams(1),
        name="gla_mixer",
    )(proj, proj, proj, proj, proj, prm['gk2p'], prm['gvec'], *([] if s0 is None else [s0]))
    return y, s_out


OUT_TM = 512
ROUTE_NEG = -1e30
LANE_GROUP0 = N_EXPERTS


def _route(logits):
    lane = _iota(logits.shape, 1)
    lane_f = lane.astype(F32)
    big = float(LANE)
    is_g = (lane >= LANE_GROUP0) & (lane < LANE_GROUP0 + 4)
    gmax = jnp.max(jnp.where(is_g, logits, ROUTE_NEG), axis=-1, keepdims=True)
    gidx = jnp.min(jnp.where(is_g & (logits == gmax), lane_f, big), axis=-1, keepdims=True) - LANE_GROUP0
    gsum = jnp.sum(jnp.where(is_g, jnp.exp(jnp.minimum(logits - gmax, 0.0)), 0.0), axis=-1, keepdims=True)
    g_w = 1.0 / gsum
    in_grp = (lane < N_EXPERTS) & ((lane // 4).astype(F32) == gidx)
    m1 = jnp.max(jnp.where(in_grp, logits, ROUTE_NEG), axis=-1, keepdims=True)
    i1 = jnp.min(jnp.where(in_grp & (logits == m1), lane_f, big), axis=-1, keepdims=True)
    rest = in_grp & (lane_f != i1)
    m2 = jnp.max(jnp.where(rest, logits, ROUTE_NEG), axis=-1, keepdims=True)
    i2 = jnp.min(jnp.where(rest & (logits == m2), lane_f, big), axis=-1, keepdims=True)
    t = jnp.exp(m2 - m1)
    w1 = g_w / (1.0 + t)
    return jnp.where(lane_f == i1, w1, 0.0) + jnp.where(lane_f == i2, w1 * t, 0.0)


def _outproj_kernel(tiles, yrc_ref, yrd_ref, ygc_ref, ygd_ref, xc_ref, xd_ref, mod_ref, wo_ref, g_ref, wr_ref,
                    br_ref, x1_ref, h2_ref, cmb_ref):
    def run(yr_ref, yg_ref, x_ref):
        m = mod_ref[0]
        mix = _mm(yr_ref[...], wo_ref[0:RW_W, :]) + _mm(yg_ref[...], wo_ref[RW_W:RW_W + GLA_V_W, :])
        x1 = x_ref[...] + m[2:3] * mix
        h2 = _rmsnorm_rows(x1) * g_ref[...] * (1.0 + m[4:5]) + m[3:4]
        x1_ref[...] = x1
        h2_ref[...] = h2.astype(BF16)
        h_hi, h_lo = _split2(h2)
        both = _dot(h_hi, wr_ref[...])
        logits = both[:, 0:LANE] + both[:, LANE:2 * LANE] + _dot(h_lo, wr_ref[:, 0:LANE])
        cmb_ref[...] = _route(logits + br_ref[...])

    tiles.by_pass(functools.partial(run, yrc_ref, ygc_ref, xc_ref), functools.partial(run, yrd_ref, ygd_ref, xd_ref))


def _out_projection(tiles, y_rw, y_gla, x, mod, w_out, norm_g, w_route, b_route):
    n = x[0].shape[0] + x[1].shape[0]
    full = lambda a: pl.BlockSpec(a.shape, lambda i: (0,) * a.ndim)
    return pl.pallas_call(
        functools.partial(_outproj_kernel, tiles),
        grid=(tiles.n_ctx + tiles.n_dec,),
        in_specs=[*tiles.specs(RW_W), *tiles.specs(GLA_V_W), *tiles.specs(D_MODEL), tiles.mod_spec(),
                  full(w_out), full(norm_g), full(w_route), full(b_route)],
        out_specs=[tiles.merged(D_MODEL), tiles.merged(D_MODEL), tiles.merged(LANE)],
        out_shape=[jax.ShapeDtypeStruct((n, D_MODEL), F32), jax.ShapeDtypeStruct((n, D_MODEL), BF16),
                   jax.ShapeDtypeStruct((n, LANE), F32)],
        compiler_params=_cparams(1),
        name="out_proj_router",
    )(*y_rw, *y_gla, *x, mod, w_out, norm_g, w_route, b_route)


MOE_TM = 512
MOE_RB = 128
MOE_INTERLEAVE = 4
SLOT_ALIGN = 16
MOE_SLOTS = 2 * MOE_TM + N_EXPERTS * SLOT_ALIGN


def _stage_expert_weights(srcs_hbm, dst_refs, stage_refs, sems):
    def copies(e):
        return [pltpu.make_async_copy(src.at[e], stage.at[e % 2], sem.at[e % 2])
                for src, stage, sem in zip(srcs_hbm, stage_refs, sems)]

    for c in copies(0):
        c.start()
    for e in range(N_EXPERTS):
        if e + 1 < N_EXPERTS:
            for c in copies(e + 1):
                c.start()
        for c, dst, stage in zip(copies(e), dst_refs, stage_refs):
            c.wait()
            dst[e] = stage[e % 2].astype(BF16)


def _moe_kernel(tiles, h2_ref, cmb_ref, x1_ref, mod_ref, w1_hbm, w3_hbm, w2_hbm, fg_ref, yc_ref, yd_ref,
                xs_ref, ys_ref, w1_ref, w3_ref, w2_ref, stage1_ref, stage3_ref, stage2_ref, sem1, sem3, sem2):
    @pl.when(pl.program_id(0) == 0)
    def _():
        _stage_expert_weights((w1_hbm, w3_hbm, w2_hbm), (w1_ref, w3_ref, w2_ref),
                              (stage1_ref, stage3_ref, stage2_ref), (sem1, sem3, sem2))

    cmb = cmb_ref[...]
    lane = _iota(cmb.shape, 1).astype(F32)
    sel = cmb > 0.0
    sel01 = jnp.where(sel, 1.0, 0.0).astype(BF16)
    before = (_iota((MOE_TM, MOE_TM), 0) > _iota((MOE_TM, MOE_TM), 1)).astype(BF16)
    pos = _dot(before, sel01)
    cnt = pos[MOE_TM - 1:MOE_TM] + sel01[MOE_TM - 1:MOE_TM].astype(F32)
    seg = jnp.floor((cnt + (SLOT_ALIGN - 1)) * (1.0 / SLOT_ALIGN))
    lower_experts = (_iota((LANE, LANE), 0) < _iota((LANE, LANE), 1)).astype(BF16)
    start = _dot(jnp.broadcast_to(seg, (8, LANE)).astype(BF16), lower_experts)[0:1] * SLOT_ALIGN
    n_blk = jnp.floor((cnt + (MOE_RB - 1)) * (1.0 / MOE_RB)).astype(jnp.int32)
    start_i = start.astype(jnp.int32)
    cnt_i = cnt.astype(jnp.int32)
    slot = start + pos
    e_a = jnp.min(jnp.where(sel, lane, float(LANE)), axis=-1, keepdims=True)
    e_b = jnp.max(jnp.where(sel, lane, -1.0), axis=-1, keepdims=True)
    pick = lambda e, x: jnp.sum(jnp.where(lane == e, x, 0.0), axis=-1, keepdims=True)
    slot_a, w_a = pick(e_a, slot), pick(e_a, cmb)
    slot_b = jnp.where(e_b != e_a, pick(e_b, slot), -1.0)
    w_b = pick(e_b, cmb)

    slots_t = jnp.where(lane == 0.0, slot_a, jnp.where(lane == 1.0, slot_b, -1.0)).T
    row_slot = _iota((MOE_SLOTS, MOE_TM), 0).astype(F32)
    gather = jnp.where((row_slot == slots_t[0:1]) | (row_slot == slots_t[1:2]), 1.0, 0.0).astype(BF16)
    xs_ref[...] = _dot(gather, h2_ref[...]).astype(BF16)
    ys_ref[...] = jnp.zeros_like(ys_ref)

    row_in_blk = _iota((MOE_RB, D_MODEL), 0)

    def expert_blocks(experts, r0s, ends):
        base = [pl.multiple_of(jnp.minimum(r0, MOE_SLOTS - MOE_RB), SLOT_ALIGN) for r0 in r0s]
        xbs = [xs_ref[pl.ds(b, MOE_RB), :] for b in base]
        gates = [_dot(xb, w3_ref[e]) for xb, e in zip(xbs, experts)]
        ups = [_dot(xb, w1_ref[e]) for xb, e in zip(xbs, experts)]
        acts = [(_silu(g) * u).astype(BF16) for g, u in zip(gates, ups)]
        outs = [_dot(a, w2_ref[e]) for a, e in zip(acts, experts)]
        for b, r0, end, out in zip(base, r0s, ends, outs):
            row = row_in_blk + b
            keep = (row >= end) | (row < r0)
            ys_ref[pl.ds(b, MOE_RB), :] = jnp.where(keep, ys_ref[pl.ds(b, MOE_RB), :], out.astype(BF16))

    seg_start = [pl.multiple_of(start_i[0, e], SLOT_ALIGN) for e in range(N_EXPERTS)]
    seg_end = [seg_start[e] + cnt_i[0, e] for e in range(N_EXPERTS)]
    for e0 in range(0, N_EXPERTS, MOE_INTERLEAVE):
        es = list(range(e0, e0 + MOE_INTERLEAVE))
        expert_blocks(es, [seg_start[e] for e in es], [seg_end[e] for e in es])
    for e in range(N_EXPERTS):
        def extra_block(b, carry, e=e):
            expert_blocks([e], [pl.multiple_of(seg_start[e] + b * MOE_RB, SLOT_ALIGN)], [seg_end[e]])
            return carry

        lax.fori_loop(1, n_blk[0, e], extra_block, 0)

    col_slot = _iota((MOE_TM, MOE_SLOTS), 1).astype(F32)
    scatter = (jnp.where(col_slot == slot_a, w_a, 0.0) + jnp.where(col_slot == slot_b, w_b, 0.0)).astype(BF16)
    x2 = x1_ref[...] + mod_ref[0][5:6] * _dot(scatter, ys_ref[...])
    y = _rmsnorm_rows(x2) * fg_ref[...]

    def write(y_ref):
        y_ref[...] = y

    tiles.by_pass(functools.partial(write, yc_ref), functools.partial(write, yd_ref))


def _moe(tiles, h2, cmb, x1, mod, w1, w3, w2, final_g):
    assert tiles.tm == MOE_TM
    hbm = pl.BlockSpec(memory_space=pl.ANY)
    out_ctx, out_dec = tiles.specs(D_MODEL)
    return pl.pallas_call(
        functools.partial(_moe_kernel, tiles),
        grid=(tiles.n_ctx + tiles.n_dec,),
        in_specs=[tiles.merged(D_MODEL), tiles.merged(LANE), tiles.merged(D_MODEL), tiles.mod_spec(),
                  hbm, hbm, hbm, pl.BlockSpec((1, D_MODEL), lambda i: (0, 0))],
        out_specs=[out_ctx, out_dec],
        out_shape=[jax.ShapeDtypeStruct((tiles.n_ctx * MOE_TM, D_MODEL), F32),
                   jax.ShapeDtypeStruct((tiles.n_dec * MOE_TM, D_MODEL), F32)],
        scratch_shapes=[pltpu.VMEM((MOE_SLOTS, D_MODEL), BF16), pltpu.VMEM((MOE_SLOTS, D_MODEL), BF16),
                        pltpu.VMEM((N_EXPERTS, D_MODEL, D_EXPERT), BF16),
                        pltpu.VMEM((N_EXPERTS, D_MODEL, D_EXPERT), BF16),
                        pltpu.VMEM((N_EXPERTS, D_EXPERT, D_MODEL), BF16),
                        pltpu.VMEM((2, D_MODEL, D_EXPERT), F32), pltpu.VMEM((2, D_MODEL, D_EXPERT), F32),
                        pltpu.VMEM((2, D_EXPERT, D_MODEL), F32)] + [pltpu.SemaphoreType.DMA((2,))] * 3,
        compiler_params=_cparams(1),
        name="moe_experts",
    )(h2, cmb, x1, mod, w1, w3, w2, final_g)


def _pad_rows(x, rows):
    return jnp.pad(x, ((0, rows - x.shape[0]),) + ((0, 0),) * (x.ndim - 1))


def _pack_params(l, w_in, rw_conv, rw_w0, rw_w2, rw_a0, rw_a2, rw_g2, rw_k_k, rw_k_a, rw_r_k, rw_ln_w, rw_ln_b,
                 gla_gk2, gla_gk_b, gla_norm_g, moe_w_group, moe_b_group, moe_w_expert, moe_b_expert):
    wi = w_in[l]
    z = lambda n: jnp.zeros((D_MODEL, n), F32)
    w_in_t = jnp.swapaxes(wi, 0, 1)
    z64 = jnp.zeros((64, RW_W), F32)
    w2p = jnp.stack([jnp.concatenate([rw_w2[l, 0], z64], 0), jnp.concatenate([z64, rw_w2[l, 1]], 0)])
    a2p = jnp.stack([jnp.concatenate([rw_a2[l, 0], z64], 0), jnp.concatenate([z64, rw_a2[l, 1]], 0)])
    vec = _pad_rows(jnp.stack([rw_w0[l, 0], rw_w0[l, 1], rw_a0[l, 0], rw_a0[l, 1], rw_k_k[l], rw_k_a[l],
                               rw_r_k[l].reshape(RW_W), rw_ln_w[l], rw_ln_b[l]]), 16)
    rw = {'conv': rw_conv[l].reshape(9, 3 * RW_W), 'w2p': w2p.astype(BF16), 'a2p': a2p.astype(BF16),
          'g2': rw_g2[l].astype(BF16), 'vec': vec}
    gk2p = jnp.stack([_pad_rows(gla_gk2[l, 0], LANE),
                      _pad_rows(jnp.concatenate([jnp.zeros((16, GLA_QK_W), F32), gla_gk2[l, 1]], 0), LANE)])
    gk_b = jnp.pad(gla_gk_b[l], ((0, 0), (0, GLA_V_W - GLA_QK_W)))
    gvec = _pad_rows(jnp.concatenate([gk_b, jnp.tile(gla_norm_g[l], GLA_V_W // LANE)[None]], axis=0), 8)
    gla = {'gk2p': gk2p.astype(BF16), 'gvec': gvec}
    w_route = jnp.concatenate(_split2(jnp.concatenate([moe_w_expert[l], moe_w_group[l], z(LANE - N_EXPERTS - 4)],
                                                      axis=1)), axis=1)
    b_route = jnp.concatenate([moe_b_expert[l], moe_b_group[l], jnp.zeros((LANE - N_EXPERTS - 4,), F32)])[None]
    return w_in_t, rw, gla, w_route, b_route


def kernel(x_prompt, x_sample, state_rwkv, state_gla, c, c_ctx, norm1_g, norm2_g, w_ada, b_ada, w_in, w_out,
           rw_conv, rw_w0, rw_w2, rw_a0, rw_a2, rw_g2, rw_k_k, rw_k_a, rw_r_k, rw_ln_w, rw_ln_b,
           gla_gk2, gla_gk_b, gla_norm_g, moe_w_group, moe_b_group, moe_w_expert, moe_b_expert,
           moe_w1, moe_w3, moe_w2, final_g):
    depth = w_in.shape[0]
    assert depth == 1, "the packed layout below handles the single-layer trunk of this problem"
    l = 0
    n_dec = x_sample.shape[0]
    ctx_row = n_dec
    cond8 = _pad_rows(jnp.concatenate([c, c_ctx[None]], axis=0), 8)
    mod = _modulation(cond8, w_ada[l], b_ada[l][None]).reshape(8, N_MOD, D_MODEL)
    pk = _pack_params(l, w_in, rw_conv, rw_w0, rw_w2, rw_a0, rw_a2, rw_g2, rw_k_k, rw_k_a, rw_r_k, rw_ln_w,
                      rw_ln_b, gla_gk2, gla_gk_b, gla_norm_g, moe_w_group, moe_b_group, moe_w_expert,
                      moe_b_expert)
    w_in_t, rw, gla, w_route, b_route = pk

    n_ctx, ctx_len, _ = x_prompt.shape
    dec_len = x_sample.shape[1]
    x_ctx = x_prompt.reshape(n_ctx * ctx_len, D_MODEL)
    x_dec = x_sample.reshape(n_dec * dec_len, D_MODEL)
    assert (n_ctx * ctx_len) % dec_len == 0, "denoising sequences must start on a dec_len row block of proj"
    first_dec = n_ctx * ctx_len // dec_len
    tiles = _Tiles(n_ctx * ctx_len, n_dec * dec_len, dec_len, ctx_row, PROJ_TM)
    assert PROJ_TM == OUT_TM == MOE_TM

    proj = _in_projection(tiles, x_ctx, x_dec, mod, norm1_g[l][None], w_in_t)
    y_rw_c, s_rw = _rwkv_mixer(proj, 0, n_ctx, ctx_len, False, rw, None)
    y_gla_c, s_gla = _gla_mixer(proj, 0, n_ctx, ctx_len, gla, None)
    y_rw_d, _ = _rwkv_mixer(proj, first_dec, n_dec, dec_len, True, rw, state_rwkv[:, l])
    y_gla_d, _ = _gla_mixer(proj, first_dec, n_dec, dec_len, gla, state_gla[:, l])
    x1, h2, cmb = _out_projection(tiles, (y_rw_c, y_rw_d), (y_gla_c, y_gla_d), (x_ctx, x_dec), mod,
                                  w_out[l].astype(BF16), norm2_g[l][None], w_route, b_route)
    y_ctx, y_dec = _moe(tiles, h2, cmb, x1, mod, moe_w1[l], moe_w3[l], moe_w2[l], final_g[None])
    return (y_ctx.reshape(x_prompt.shape), y_dec.reshape(x_sample.shape), s_rw[:, None], s_gla[:, None])
```

```python
import functools
import itertools

import jax
import jax.numpy as jnp
from jax import lax
from jax.experimental import pallas as pl
from jax.experimental.pallas import tpu as pltpu

F32 = jnp.float32
BF16 = jnp.bfloat16

D_MODEL = 1024
RW_W = 512
GLA_V_W = 512
GLA_QK_W = 256
N_EXPERTS = 16
D_EXPERT = 256
N_MOD = 6
EPS = 1e-6
RW_LN_EPS = 64e-5
RW_DECAY_SCALE = 0.606531
GLA_GATE_NORM = 16.0
GLA_Q_SCALE = 64 ** -0.5
GRID_W = 64

LANE = 128
CHUNK = 64
CONV_PAD = 128
TERM_UNROLL = 4
RW_PAIRS_PER_STEP = 2
GLA_PAIRS = 2
RW_GROUPS = 4
D_PROJ = 28 * LANE
VMEM_LIMIT = 56 * 1024 * 1024

CB_R, CB_K, CB_V, CB_LORA, CB_LGK, CB_GQ, CB_GK, CB_GV, CB_OG = 0, 4, 8, 12, 15, 16, 18, 20, 24

_NN = (((1,), (0,)), ((), ()))
_NT = (((1,), (1,)), ((), ()))
_TN = (((0,), (0,)), ((), ()))


def _dot(a, b, dims=_NN):
    return lax.dot_general(a, b, dims, preferred_element_type=F32)


def _mm(a, b, dims=_NN):
    return _dot(a.astype(BF16), b.astype(BF16), dims)


def _split2(x):
    hi = x.astype(BF16)
    lo = (x - hi.astype(F32)).astype(BF16)
    return hi, lo


def _mm3(a, b, dims=_NN):
    ah, al = _split2(a)
    bh, bl = _split2(b)
    return _dot(ah, bh, dims) + _dot(ah, bl, dims) + _dot(al, bh, dims)


def _mm_01_lhs(a01, b, dims=_NN):
    n = b.shape[1]
    both = _dot(a01, jnp.concatenate(_split2(b), axis=1), dims)
    return both[:, 0:n] + both[:, n:2 * n]


def _sigmoid(x):
    return 0.5 * jnp.tanh(0.5 * x) + 0.5


def _silu(x):
    return x * _sigmoid(x)


def _log_sigmoid(x):
    return jnp.minimum(x, 0.0) - jnp.log(1.0 + jnp.exp(-jnp.abs(x)))


def _iota(shape, dim):
    return lax.broadcasted_iota(jnp.int32, shape, dim)


def _cparams(n_axes):
    return pltpu.CompilerParams(dimension_semantics=("arbitrary",) * n_axes, vmem_limit_bytes=VMEM_LIMIT)


MOD_TN = 768


def _mod_kernel(c_ref, w_ref, b_ref, o_ref):
    o_ref[...] = _mm3(_silu(c_ref[...]), w_ref[...]) + b_ref[...]


def _modulation(cond8, w_ada, b_ada):
    n = w_ada.shape[1]
    return pl.pallas_call(
        _mod_kernel,
        grid=(n // MOD_TN,),
        in_specs=[pl.BlockSpec((8, D_MODEL), lambda j: (0, 0)),
                  pl.BlockSpec((D_MODEL, MOD_TN), lambda j: (0, j)),
                  pl.BlockSpec((1, MOD_TN), lambda j: (0, j))],
        out_specs=pl.BlockSpec((8, MOD_TN), lambda j: (0, j)),
        out_shape=jax.ShapeDtypeStruct((8, n), F32),
        compiler_params=_cparams(1),
        name="adaln_mod",
    )(cond8, w_ada, b_ada)


PROJ_TM = 512
D_IN = 3488
N_LGK = 32


def _rmsnorm_rows(x):
    return x * lax.rsqrt(jnp.mean(x * x, axis=-1, keepdims=True) + EPS)


class _Tiles:
    def __init__(self, n_ctx_tokens, n_dec_tokens, dec_seq_len, ctx_row, tm):
        self.tm = tm
        self.n_ctx = n_ctx_tokens // tm
        self.n_dec = n_dec_tokens // tm
        self.per_seq = dec_seq_len // tm
        self.ctx_row = ctx_row

    def specs(self, width):
        last_ctx = self.n_ctx - 1
        n_ctx = self.n_ctx
        return (pl.BlockSpec((self.tm, width), lambda i: (jnp.minimum(i, last_ctx), 0)),
                pl.BlockSpec((self.tm, width), lambda i: (jnp.maximum(i - n_ctx, 0), 0)))

    def merged(self, width):
        return pl.BlockSpec((self.tm, width), lambda i: (i, 0))

    def mod_spec(self):
        n_ctx, per_seq, ctx_row = self.n_ctx, self.per_seq, self.ctx_row
        return pl.BlockSpec((1, N_MOD, D_MODEL),
                            lambda i: (jnp.where(i < n_ctx, ctx_row, (i - n_ctx) // per_seq), 0, 0))

    def by_pass(self, run_ctx, run_dec):
        i = pl.program_id(0)
        pl.when(i < self.n_ctx)(run_ctx)
        pl.when(i >= self.n_ctx)(run_dec)


def _inproj_kernel(tiles, xc_ref, xd_ref, mod_ref, g_ref, wt_ref, o_ref, w_ref):
    @pl.when(pl.program_id(0) == 0)
    def _():
        for j in range(D_PROJ // LANE):
            if j == CB_LGK:
                blk = jnp.concatenate([wt_ref[D_IN - N_LGK:D_IN, :], jnp.zeros((LANE - N_LGK, D_MODEL), F32)], axis=0)
            else:
                src = j if j < CB_LGK else j - 1
                blk = wt_ref[src * LANE:(src + 1) * LANE, :]
            w_ref[:, j * LANE:(j + 1) * LANE] = blk.T.astype(BF16)

    def run(x_ref):
        m = mod_ref[0]
        h = _rmsnorm_rows(x_ref[...]) * g_ref[...] * (1.0 + m[1:2]) + m[0:1]
        o_ref[...] = _mm(h, w_ref[...])

    tiles.by_pass(functools.partial(run, xc_ref), functools.partial(run, xd_ref))


def _in_projection(tiles, x_ctx, x_dec, mod, norm_g, w_in_t):
    full = lambda a: pl.BlockSpec(a.shape, lambda i: (0,) * a.ndim)
    return pl.pallas_call(
        functools.partial(_inproj_kernel, tiles),
        grid=(tiles.n_ctx + tiles.n_dec,),
        in_specs=[*tiles.specs(D_MODEL), tiles.mod_spec(), full(norm_g),
                  pl.BlockSpec(w_in_t.shape, lambda i: (0, 0), pipeline_mode=pl.Buffered(1))],
        out_specs=tiles.merged(D_PROJ),
        out_shape=jax.ShapeDtypeStruct((x_ctx.shape[0] + x_dec.shape[0], D_PROJ), F32),
        scratch_shapes=[pltpu.VMEM((D_MODEL, D_PROJ), BF16)],
        compiler_params=_cparams(1),
        name="in_proj",
    )(x_ctx, x_dec, mod, norm_g, w_in_t)


def _time_masks(reverse):
    r = _iota((2 * CHUNK, 2 * CHUNK), 0) % CHUNK
    c = _iota((2 * CHUNK, 2 * CHUNK), 1) % CHUNK
    if reverse:
        return r < c, r <= c
    return r > c, r >= c


def _cumsum_matrix(reverse):
    r = _iota((CHUNK, CHUNK), 0)
    c = _iota((CHUNK, CHUNK), 1)
    tri = (r <= c) if reverse else (r >= c)
    return tri.astype(BF16)


def _stack_heads(x, half):
    m0 = _iota(x.shape, 1) < half
    return jnp.concatenate([jnp.where(m0, x, 0.0), jnp.where(m0, 0.0, x)], axis=0)


def _finished_before_last_scan(groups, n_chunks):
    done_fwd, done_bwd = set(), set()
    for s, fwd, bwd in groups[:-1]:
        done_fwd.update(s * n_chunks + c for c in fwd)
        done_bwd.update(s * n_chunks + c for c in bwd)
    return sorted(done_fwd & done_bwd)


def _head_sums(x):
    parts = []
    for p in range(x.shape[1] // LANE):
        xp = x[:, p * LANE:(p + 1) * LANE]
        m0 = _iota(xp.shape, 1) < 64
        s0 = jnp.sum(jnp.where(m0, xp, 0.0), axis=-1, keepdims=True)
        s1 = jnp.sum(jnp.where(m0, 0.0, xp), axis=-1, keepdims=True)
        parts.append(jnp.where(m0, s0, s1))
    return parts[0] if len(parts) == 1 else jnp.concatenate(parts, axis=1)


def _rwkv_chunk_terms(insts, interleaved=()):
    c = CHUNK
    step_row = _iota((c, LANE), 0)
    step_col = _iota((c, LANE), 1) % c
    eye_w = (step_row == step_col).astype(F32)
    same_head = (_iota((LANE, LANE), 0) // 64) == (_iota((LANE, LANE), 1) // 64)
    stack_bf = lambda x: _stack_heads(x, 64).astype(BF16)
    cums = [_mm_01_lhs(_cumsum_matrix(rev), lw) for (_, lw, _, _, _, _, rev) in insts]
    pre = []
    for (r, lw, kd, a, b, v, rev), cum in zip(insts, cums):
        end = cum[0:1] if rev else cum[c - 1:c]
        inv_w = jnp.exp(-cum)
        rem_w = jnp.exp(end - cum)
        a_t = a * jnp.exp(cum - lw)
        r_t = r * jnp.exp(cum)
        bk_s = jnp.concatenate([stack_bf(b * inv_w), stack_bf(kd * inv_w)], axis=0)
        bkh = jnp.concatenate([b * rem_w, kd * rem_w], axis=0).astype(BF16)
        pre.append((a_t, r_t, bk_s, bkh, v, jnp.exp(end)))
    ms = [_dot(jnp.concatenate([a_t, r_t], axis=0).astype(BF16), bk_s, _NT) for (a_t, r_t, bk_s, _, _, _) in pre]
    mats = []
    for m, (_, _, _, _, _, _, rev) in zip(ms, insts):
        strict = (step_row < step_col) if rev else (step_row > step_col)
        incl = (step_row <= step_col) if rev else (step_row >= step_col)
        l_ab = jnp.where(strict, m[0:c, 0:LANE], 0.0)
        l_akrk = jnp.concatenate([jnp.where(strict, m[0:c, LANE:2 * LANE], 0.0),
                                  jnp.where(incl, m[c:2 * c, LANE:2 * LANE], 0.0)], axis=0).astype(BF16)
        m_rb = jnp.where(incl, m[c:2 * c, 0:LANE], 0.0).astype(BF16)
        mats.append((l_ab, l_akrk, m_rb))
    pending = list(interleaved)

    def run_interleaved():
        if pending:
            pending.pop(0)()

    lvs = [_dot(l_akrk, stack_bf(pr[4])) for (_, l_akrk, _), pr in zip(mats, pre)]
    run_interleaved()
    ps = [eye_w + l_ab for (l_ab, _, _) in mats]
    lps = [_dot(l_ab.astype(BF16), stack_bf(l_ab)) for (l_ab, _, _) in mats]
    for level in range(1, 6):
        if level < 5:
            xs = [_dot(lp.astype(BF16), jnp.concatenate([stack_bf(p), stack_bf(lp)], axis=1))
                  for lp, p in zip(lps, ps)]
            ps = [p + x[:, 0:LANE] for p, x in zip(ps, xs)]
            lps = [x[:, LANE:2 * LANE] for x in xs]
        else:
            ps = [p + _dot(lp.astype(BF16), stack_bf(p)) for lp, p in zip(lps, ps)]
        if level in (2, 4):
            run_interleaved()
    pxs = [_dot(p.astype(BF16), jnp.concatenate([stack_bf(pr[0]), stack_bf(lv[0:c])], axis=1))
           for p, pr, lv in zip(ps, pre, lvs)]
    run_interleaved()
    mzs = [_dot(mt[2], jnp.concatenate([stack_bf(px[:, 0:LANE]), stack_bf(px[:, LANE:2 * LANE])], axis=1))
           for mt, px in zip(mats, pxs)]
    ts = [_dot(px[:, 0:LANE].astype(BF16), pr[3][0:c], _TN) for px, pr in zip(pxs, pre)]
    gs = [_dot(jnp.concatenate([px[:, LANE:2 * LANE], pr[4]], axis=0).astype(BF16), pr[3], _TN)
          for px, pr in zip(pxs, pre)]
    while pending:
        run_interleaved()
    out = []
    for pr, lv, mz, t, g in zip(pre, lvs, mzs, ts, gs):
        q = pr[1] + mz[:, 0:LANE]
        y0 = mz[:, LANE:2 * LANE] + lv[c:2 * c]
        g_wide = jnp.where(_iota((c, LANE), 1) < 64, g[0:c], g[c:2 * c])
        out.append((jnp.where(same_head, t, 0.0).astype(BF16), g_wide, pr[5],
                    q.astype(BF16), y0))
    return out


def _gla_chunk_terms(insts, interleaved=()):
    c = CHUNK
    step_row = _iota((c, LANE), 0)
    step_col = _iota((c, LANE), 1) % c
    same_head = (_iota((LANE, 2 * LANE), 0) // 64) == (_iota((LANE, 2 * LANE), 1) // LANE)
    cums = [_mm_01_lhs(_cumsum_matrix(rev), g) for (_, _, _, g, rev) in insts]
    pre = []
    for (q, k, v, g, rev), cum in zip(insts, cums):
        end = cum[0:1] if rev else cum[c - 1:c]
        qt = (q * jnp.exp(cum)).astype(BF16)
        k_s = _stack_heads(k * jnp.exp(-cum), 64).astype(BF16)
        kh = (k * jnp.exp(end - cum)).astype(BF16)
        a_col = jnp.broadcast_to(jnp.exp(end), (LANE, LANE)).T
        pre.append((qt, k_s, kh, v.astype(BF16), _stack_heads(v, LANE).astype(BF16), a_col))
    pending = list(interleaved)

    def run_interleaved():
        if pending:
            pending.pop(0)()

    run_interleaved()
    atts = [_dot(pr[0], pr[1], _NT) for pr in pre]
    run_interleaved()
    atts = [jnp.where((step_row <= step_col) if inst[4] else (step_row >= step_col), att, 0.0).astype(BF16)
            for att, inst in zip(atts, insts)]
    o0s = [_dot(att, pr[4]) for att, pr in zip(atts, pre)]
    run_interleaved()
    kvs = [jnp.where(same_head, _dot(pr[2], pr[3], _TN), 0.0) for pr in pre]
    while pending:
        run_interleaved()
    return [(pr[0], o0, pr[5], kv) for pr, o0, kv in zip(pre, o0s, kvs)]


def _rwkv_kernel(seq_len, nseq, is_grid, zero_init, r_ref, k_ref, v_ref, lora_ref, cwr_ref, cwk_ref, cwv_ref,
                 w2_ref, a2_ref, g2_ref, vec_ref, *rest):
    s0_ref = None if zero_init else rest[0]
    (y_ref, sout_ref, pad_ref, rs_ref, ks_ref, vs_ref, kk_ref, bonus_ref, gate_ref, yf_ref, yb_ref, st_ref, tt_ref,
     tg_ref, tw_ref, tq_ref, ty_ref) = rest[0 if zero_init else 1:][:17]
    left_ref, right_ref = rest[-2:] if is_grid else (None, None)
    n_chunks = seq_len // CHUNK
    npp = RW_PAIRS_PER_STEP
    w = npp * LANE
    pair = lambda x, p: x[:, p * LANE:(p + 1) * LANE]
    vec = vec_ref[...]
    w0 = (vec[0:1], vec[1:2])
    a0 = (vec[2:3], vec[3:4])
    k_k, k_a, r_k, ln_w, ln_b = vec[4:5], vec[5:6], vec[6:7], vec[7:8], vec[8:9]
    block_sum = _head_sums
    chains = [(d, p) for d in range(2) for p in range(npp)]
    rows_of = lambda s, c: slice((s * n_chunks + c) * CHUNK, (s * n_chunks + c + 1) * CHUNK)

    if not is_grid:
        zeros = jnp.zeros((CONV_PAD, 3 * w), F32)
        pad_ref[0:CONV_PAD, :] = zeros
        pad_ref[CONV_PAD + seq_len:2 * CONV_PAD + seq_len, :] = zeros
    cw = jnp.concatenate([cwr_ref[...], cwk_ref[...], cwv_ref[...]], axis=1)
    col = _iota((CHUNK, 3 * w), 0)
    rkv_rows = lambda rows: jnp.concatenate([r_ref[rows, :], k_ref[rows, :], v_ref[rows, :]], axis=1)

    def shift_body(c, carry):
        rows = pl.ds(pl.multiple_of(c * CHUNK, CHUNK), CHUNK)
        x = rkv_rows(rows)
        left_ref[rows, :] = jnp.where(col >= 1, pltpu.roll(x, 1, 0), 0.0)
        right_ref[rows, :] = jnp.where(col <= GRID_W - 2, pltpu.roll(x, CHUNK - 1, 0), 0.0)
        return carry

    def conv_chunk(c):
        base = CONV_PAD + c * CHUNK
        acc = jnp.zeros((CHUNK, 3 * w), F32)
        if is_grid:
            for di in (-1, 0, 1):
                if not 0 <= c + di < n_chunks:
                    continue
                rows = slice((c + di) * CHUNK, (c + di + 1) * CHUNK)
                for dj, src in ((-1, left_ref[rows, :]), (0, rkv_rows(rows)), (1, right_ref[rows, :])):
                    tap = (di + 1) * 3 + dj + 1
                    acc = acc + src * cw[tap:tap + 1]
        else:
            win = pad_ref[pl.ds(base - 8, CHUNK + 16), :]
            for dj in (-1, 0, 1):
                acc = acc + win[8 + dj:8 + dj + CHUNK] * cw[4 + dj:5 + dj]
        return acc[:, 0:w], acc[:, w:2 * w], acc[:, 2 * w:3 * w]

    def load_sequence(s):
        seq_rows = slice(s * seq_len, (s + 1) * seq_len)
        pad_ref[CONV_PAD:CONV_PAD + seq_len, 0:w] = r_ref[seq_rows, :]
        pad_ref[CONV_PAD:CONV_PAD + seq_len, w:2 * w] = k_ref[seq_rows, :]
        pad_ref[CONV_PAD:CONV_PAD + seq_len, 2 * w:3 * w] = v_ref[seq_rows, :]

    def conv_store(s, c):
        rows = rows_of(s, c)
        rc, kc, vc = conv_chunk(c)
        kk = kc * k_k
        rs_ref[rows, :] = rc
        ks_ref[rows, :] = kc
        vs_ref[rows, :] = vc
        kk_ref[rows, :] = kk * lax.rsqrt(block_sum(kk * kk) + EPS)
        bonus_ref[rows, :] = block_sum(rc * kc * r_k) * vc
        gate_ref[rows, :] = _mm(_sigmoid(lora_ref[rows, 2 * LANE:3 * LANE]), g2_ref[...])

    for s in range(nseq):
        for d, p in chains:
            if zero_init:
                st_ref[s, d, p] = jnp.zeros((64, LANE), F32)
            else:
                st_ref[s, d, p] = jnp.concatenate([s0_ref[s, d, 2 * p], s0_ref[s, d, 2 * p + 1]], axis=1)

    if nseq == 1:
        groups = [(0, list(range(g * TERM_UNROLL, (g + 1) * TERM_UNROLL)),
                   list(range(n_chunks - 1 - g * TERM_UNROLL, n_chunks - 1 - (g + 1) * TERM_UNROLL, -1)))
                  for g in range(n_chunks // TERM_UNROLL)]
    else:
        groups = [(s, list(range(n_chunks)), list(range(n_chunks - 1, -1, -1))) for s in range(nseq)]

    def group_terms(group, interleaved):
        s, fwd, bwd = group
        insts, where, lora_in = [], [], {}
        for d, chunks in ((0, fwd), (1, bwd)):
            for c in chunks:
                rows = rows_of(s, c)
                if c not in lora_in:
                    lora_in[c] = (jnp.tanh(lora_ref[rows, 0:LANE]).astype(BF16),
                                  lora_ref[rows, LANE:2 * LANE].astype(BF16))
                lw = -RW_DECAY_SCALE * _sigmoid(w0[d] + _dot(lora_in[c][0], w2_ref[d]))
                ag = _sigmoid(a0[d] + _dot(lora_in[c][1], a2_ref[d]))
                rc, kc, vc, kk = rs_ref[rows, :], ks_ref[rows, :], vs_ref[rows, :], kk_ref[rows, :]
                kd = kc * (1.0 + (ag - 1.0) * k_a)
                kb = kk * ag
                for p in range(npp):
                    insts.append((pair(rc, p), pair(lw, p), pair(kd, p), -pair(kk, p), pair(kb, p), pair(vc, p),
                                  d == 1))
                    where.append((d, p, s * n_chunks + c))
        for (d, p, gc), (t, g, w_end, q, y0) in zip(where, _rwkv_chunk_terms(insts, interleaved)):
            tt_ref[d, p, gc] = t
            tg_ref[d, p, gc] = g
            tw_ref[d, p, gc] = jnp.broadcast_to(w_end, (8, LANE))
            tq_ref[d, p, gc] = q
            ty_ref[d, p, gc] = y0

    def scan_step(s, chunk_of_dir):
        gcs = [s * n_chunks + chunk_of_dir[d] for d, _ in chains]
        ss = [st_ref[s, d, p] for d, p in chains]
        ys = [_dot(tq_ref[d, p, gc], _stack_heads(x, 64).astype(BF16), _NT) + ty_ref[d, p, gc]
              for (d, p), gc, x in zip(chains, gcs, ss)]
        sn = [x * tw_ref[d, p, gc][0:1] + _dot(x.astype(BF16), tt_ref[d, p, gc]) + tg_ref[d, p, gc]
              for (d, p), gc, x in zip(chains, gcs, ss)]
        for (d, p), gc, y, x in zip(chains, gcs, ys, sn):
            st_ref[s, d, p] = x
            out_ref = yf_ref if d == 0 else yb_ref
            out_ref[gc * CHUNK:(gc + 1) * CHUNK, p * LANE:(p + 1) * LANE] = y

    def group_scan(group):
        s, fwd, bwd = group
        return [functools.partial(scan_step, s, (cf, cb)) for cf, cb in zip(fwd, bwd)]

    conv_done, seq_loaded = set(), set()

    def group_conv(group):
        s, fwd, bwd = group
        todo = [c for c in sorted(set(fwd) | set(bwd)) if (s, c) not in conv_done]
        conv_done.update((s, c) for c in todo)
        thunks = []
        for i in range(0, len(todo), max(1, -(-len(todo) // TERM_UNROLL))):
            part = todo[i:i + max(1, -(-len(todo) // TERM_UNROLL))]
            need_load = s not in seq_loaded
            seq_loaded.add(s)

            def run(part=part, need_load=need_load):
                if need_load:
                    assert not is_grid or nseq == 1
                    load_sequence(s)
                for c in part:
                    conv_store(s, c)

            thunks.append(run)
        return thunks

    def merge(*lists):
        n = max(len(lst) for lst in lists)
        pick = lambda lst, i: lst[i] if i < len(lst) else (lambda: None)
        return [lambda i=i: [pick(lst, i)() for lst in lists] for i in range(n)]

    def post(offs):
        ys = [yf_ref[pl.ds(off, CHUNK), :] + yb_ref[pl.ds(off, CHUNK), :] for off in offs]
        mus = [block_sum(y) * (1.0 / 64) for y in ys]
        dlts = [y - mu for y, mu in zip(ys, mus)]
        vrs = [block_sum(dlt * dlt) * (1.0 / 64) for dlt in dlts]
        for off, dlt, var in zip(offs, dlts, vrs):
            yn = dlt * lax.rsqrt(var + RW_LN_EPS) * ln_w + ln_b
            y_ref[pl.ds(off, CHUNK), :] = (yn + bonus_ref[pl.ds(off, CHUNK), :]) * gate_ref[pl.ds(off, CHUNK), :]

    def sequence_post(s):
        return [functools.partial(post, [(s * n_chunks + c) * CHUNK]) for c in range(n_chunks)]

    if is_grid:
        assert nseq == 1
        seq_loaded.add(0)
        lax.fori_loop(0, n_chunks, shift_body, 0)
    for thunk in group_conv(groups[0]):
        thunk()
    posted = set()
    for g, group in enumerate(groups):
        scans = group_scan(groups[g - 1]) if g else []
        convs = group_conv(groups[g + 1]) if g + 1 < len(groups) else []
        posts = []
        if nseq > 1 and g >= 2:
            s_done = groups[g - 2][0]
            posts = sequence_post(s_done)
            posted.update(range(s_done * n_chunks, (s_done + 1) * n_chunks))
        group_terms(group, merge(scans, convs, posts))
    early = [gc for gc in _finished_before_last_scan(groups, n_chunks) if gc not in posted]
    last_scan = group_scan(groups[-1])
    per_step = max(1, -(-len(early) // len(last_scan)))
    early_posts = [functools.partial(post, [gc * CHUNK for gc in early[i:i + per_step]])
                   for i in range(0, len(early), per_step)]
    for step, extra in itertools.zip_longest(last_scan, early_posts, fillvalue=lambda: None):
        step()
        extra()
    posted.update(early)
    for s in range(nseq):
        for d, p in chains:
            x = st_ref[s, d, p]
            sout_ref[s, d, 2 * p] = x[:, 0:64]
            sout_ref[s, d, 2 * p + 1] = x[:, 64:LANE]

    left = [gc for gc in range(nseq * n_chunks) if gc not in posted]
    for i in range(0, len(left), TERM_UNROLL):
        post([gc * CHUNK for gc in left[i:i + TERM_UNROLL]])


def _rwkv_mixer(proj, first_seq, n_seq, seq_len, is_grid, prm, s0):
    n_pairs = RW_W // LANE
    n_chunks = seq_len // CHUNK
    npp = RW_PAIRS_PER_STEP
    w = npp * LANE
    nseq = max(1, RW_GROUPS * TERM_UNROLL // n_chunks)
    assert n_chunks % TERM_UNROLL == 0 and n_pairs % npp == 0
    assert (n_chunks == TERM_UNROLL or nseq == 1) and n_seq % nseq == 0 and first_seq % nseq == 0
    rows = nseq * seq_len
    total_chunks = nseq * n_chunks
    first = first_seq // nseq
    col = lambda cb: (lambda b, p: (b + first, cb // npp + p))
    par = lambda cb: (lambda b, p: (0, cb // npp + p))
    state_spec = pl.BlockSpec((nseq, 2, 2 * npp, 64, 64), lambda b, p: (b, 0, p, 0, 0))
    kernel = functools.partial(_rwkv_kernel, seq_len, nseq, is_grid, s0 is None)
    y, s_out = pl.pallas_call(
        kernel,
        grid=(n_seq // nseq, n_pairs // npp),
        in_specs=[pl.BlockSpec((rows, w), col(CB_R)),
                  pl.BlockSpec((rows, w), col(CB_K)),
                  pl.BlockSpec((rows, w), col(CB_V)),
                  pl.BlockSpec((rows, 3 * LANE), lambda b, p: (b + first, CB_LORA // 3)),
                  pl.BlockSpec((9, w), par(CB_R)),
                  pl.BlockSpec((9, w), par(CB_K)),
                  pl.BlockSpec((9, w), par(CB_V)),
                  pl.BlockSpec((2, LANE, w), lambda b, p: (0, 0, p)),
                  pl.BlockSpec((2, LANE, w), lambda b, p: (0, 0, p)),
                  pl.BlockSpec((LANE, w), lambda b, p: (0, p)),
                  pl.BlockSpec((16, w), lambda b, p: (0, p))] + ([] if s0 is None else [state_spec]),
        out_specs=[pl.BlockSpec((rows, w), lambda b, p: (b, p)), state_spec],
        out_shape=[jax.ShapeDtypeStruct((n_seq * seq_len, RW_W), F32),
                   jax.ShapeDtypeStruct((n_seq, 2, 2 * n_pairs, 64, 64), F32)],
        scratch_shapes=[pltpu.VMEM((8, LANE) if is_grid else (seq_len + 2 * CONV_PAD, 3 * w), F32)]
                       + [pltpu.VMEM((rows, w), F32)] * 8
                       + [pltpu.VMEM((nseq, 2, npp, 64, LANE), F32),
                          pltpu.VMEM((2, npp, total_chunks, LANE, LANE), BF16),
                          pltpu.VMEM((2, npp, total_chunks, 64, LANE), F32),
                          pltpu.VMEM((2, npp, total_chunks, 8, LANE), F32),
                          pltpu.VMEM((2, npp, total_chunks, CHUNK, LANE), BF16),
                          pltpu.VMEM((2, npp, total_chunks, CHUNK, LANE), F32)]
                       + ([pltpu.VMEM((seq_len, 3 * w), F32)] * 2 if is_grid else []),
        compiler_params=_cparams(2),
        name="rwkv_mixer",
    )(proj, proj, proj, proj, prm['conv'], prm['conv'], prm['conv'], prm['w2p'], prm['a2p'], prm['g2'],
      prm['vec'], *([] if s0 is None else [s0]))
    return y, s_out


def _gla_kernel(seq_len, nseq, zero_init, q_ref, k_ref, v_ref, og_ref, lgk_ref, gk2_ref, gvec_ref, *rest):
    s0_ref = None if zero_init else rest[0]
    y_ref, sout_ref, of_ref, ob_ref, st_ref, tq_ref, to_ref, ta_ref, tkv_ref = rest[0 if zero_init else 1:]
    n_chunks = seq_len // CHUNK
    npp = GLA_PAIRS
    gvec = gvec_ref[...]
    chains = [(d, p) for d in range(2) for p in range(npp)]

    for s in range(nseq):
        for d, p in chains:
            if zero_init:
                st_ref[s, d, p] = jnp.zeros((LANE, 2 * LANE), F32)
            else:
                z = jnp.zeros((64, LANE), F32)
                st_ref[s, d, p] = jnp.concatenate([jnp.concatenate([s0_ref[s, d, 2 * p], z], axis=1),
                                                   jnp.concatenate([z, s0_ref[s, d, 2 * p + 1]], axis=1)], axis=0)

    if nseq == 1:
        groups = [(0, list(range(g * TERM_UNROLL, (g + 1) * TERM_UNROLL)),
                   list(range(n_chunks - 1 - g * TERM_UNROLL, n_chunks - 1 - (g + 1) * TERM_UNROLL, -1)))
                  for g in range(n_chunks // TERM_UNROLL)]
    else:
        groups = [(s, list(range(n_chunks)), list(range(n_chunks - 1, -1, -1))) for s in range(nseq)]

    def group_terms(group, interleaved):
        s, fwd, bwd = group
        insts, where, lgk_in = [], [], {}
        for d, chunks in ((0, fwd), (1, bwd)):
            for c in chunks:
                gc = s * n_chunks + c
                rows = slice(gc * CHUNK, (gc + 1) * CHUNK)
                if c not in lgk_in:
                    lgk_in[c] = lgk_ref[rows, :].astype(BF16)
                x = _dot(lgk_in[c], gk2_ref[d]) + gvec[d:d + 1, 0:GLA_QK_W]
                g = _log_sigmoid(x) * (1.0 / GLA_GATE_NORM)
                qc = q_ref[rows, :] * GLA_Q_SCALE
                kc = k_ref[rows, :]
                vc = v_ref[rows, :]
                for p in range(npp):
                    qk = slice(p * LANE, (p + 1) * LANE)
                    insts.append((qc[:, qk], kc[:, qk], vc[:, 2 * p * LANE:2 * (p + 1) * LANE], g[:, qk], d == 1))
                    where.append((d, p, gc))
        for (d, p, gc), (qt, o0, a_col, kv) in zip(where, _gla_chunk_terms(insts, interleaved)):
            tq_ref[d, p, gc] = qt
            to_ref[d, p, gc] = o0
            ta_ref[d, p, gc] = a_col
            tkv_ref[d, p, gc] = kv

    def scan_step(s, chunk_of_dir):
        gcs = [s * n_chunks + chunk_of_dir[d] for d, _ in chains]
        ss = [st_ref[s, d, p] for d, p in chains]
        os_ = [_dot(tq_ref[d, p, gc], x.astype(BF16)) + to_ref[d, p, gc] for (d, p), gc, x in zip(chains, gcs, ss)]
        for (d, p), gc, x, o in zip(chains, gcs, ss, os_):
            a_col = ta_ref[d, p, gc]
            st_ref[s, d, p] = x * jnp.concatenate([a_col, a_col], axis=1) + tkv_ref[d, p, gc]
            out_ref = of_ref if d == 0 else ob_ref
            out_ref[gc * CHUNK:(gc + 1) * CHUNK, 2 * p * LANE:2 * (p + 1) * LANE] = o

    def group_scan(group):
        s, fwd, bwd = group
        return [functools.partial(scan_step, s, (cf, cb)) for cf, cb in zip(fwd, bwd)]

    def post(off):
        for h in range(2 * npp):
            hs = slice(h * LANE, (h + 1) * LANE)
            o = of_ref[pl.ds(off, CHUNK), hs] + ob_ref[pl.ds(off, CHUNK), hs]
            gate = _silu(og_ref[pl.ds(off, CHUNK), hs])
            y_ref[pl.ds(off, CHUNK), hs] = _rmsnorm_rows(o) * gvec[2:3, hs] * gate

    posted = set()
    for g, group in enumerate(groups):
        thunks = group_scan(groups[g - 1]) if g else []
        if nseq > 1 and g >= 2:
            s_done = groups[g - 2][0]
            posts = [functools.partial(post, (s_done * n_chunks + c) * CHUNK) for c in range(n_chunks)]
            thunks = [lambda a=a, b=b: (a(), b()) for a, b in zip(thunks, posts)]
            posted.update(range(s_done * n_chunks, (s_done + 1) * n_chunks))
        group_terms(group, thunks)
    early = [gc for gc in _finished_before_last_scan(groups, n_chunks) if gc not in posted]
    last_scan = group_scan(groups[-1])
    per_step = max(1, -(-len(early) // len(last_scan)))
    for i, step in enumerate(last_scan):
        step()
        for gc in early[i * per_step:(i + 1) * per_step]:
            post(gc * CHUNK)
    for gc in early[len(last_scan) * per_step:]:
        post(gc * CHUNK)
    posted.update(early)
    for s in range(nseq):
        for d, p in chains:
            x = st_ref[s, d, p]
            sout_ref[s, d, 2 * p] = x[0:64, 0:LANE]
            sout_ref[s, d, 2 * p + 1] = x[64:LANE, LANE:2 * LANE]
    for gc in range(nseq * n_chunks):
        if gc not in posted:
            post(gc * CHUNK)


def _gla_mixer(proj, first_seq, n_seq, seq_len, prm, s0):
    npp = GLA_PAIRS
    n_heads = 2 * npp
    n_chunks = seq_len // CHUNK
    nseq = max(1, RW_GROUPS * TERM_UNROLL // n_chunks)
    assert n_chunks % TERM_UNROLL == 0
    assert (n_chunks == TERM_UNROLL or nseq == 1) and n_seq % nseq == 0 and first_seq % nseq == 0
    rows = nseq * seq_len
    total_chunks = nseq * n_chunks
    first = first_seq // nseq
    state_spec = pl.BlockSpec((nseq, 2, n_heads, 64, LANE), lambda b: (b, 0, 0, 0, 0))
    kernel = functools.partial(_gla_kernel, seq_len, nseq, s0 is None)
    y, s_out = pl.pallas_call(
        kernel,
        grid=(n_seq // nseq,),
        in_specs=[pl.BlockSpec((rows, GLA_QK_W), lambda b: (b + first, CB_GQ * LANE // GLA_QK_W)),
                  pl.BlockSpec((rows, GLA_QK_W), lambda b: (b + first, CB_GK * LANE // GLA_QK_W)),
                  pl.BlockSpec((rows, GLA_V_W), lambda b: (b + first, CB_GV * LANE // GLA_V_W)),
                  pl.BlockSpec((rows, GLA_V_W), lambda b: (b + first, CB_OG * LANE // GLA_V_W)),
                  pl.BlockSpec((rows, LANE), lambda b: (b + first, CB_LGK)),
                  pl.BlockSpec((2, LANE, GLA_QK_W), lambda b: (0, 0, 0)),
                  pl.BlockSpec((8, GLA_V_W), lambda b: (0, 0))] + ([] if s0 is None else [state_spec]),
        out_specs=[pl.BlockSpec((rows, GLA_V_W), lambda b: (b, 0)), state_spec],
        out_shape=[jax.ShapeDtypeStruct((n_seq * seq_len, GLA_V_W), F32),
                   jax.ShapeDtypeStruct((n_seq, 2, n_heads, 64, LANE), F32)],
        scratch_shapes=[pltpu.VMEM((rows, GLA_V_W), F32)] * 2
                       + [pltpu.VMEM((nseq, 2, npp, LANE, 2 * LANE), F32),
                          pltpu.VMEM((2, npp, total_chunks, CHUNK, LANE), BF16),
                          pltpu.VMEM((2, npp, total_chunks, CHUNK, 2 * LANE), F32),
                          pltpu.VMEM((2, npp, total_chunks, LANE, LANE), F32),
                          pltpu.VMEM((2, npp, total_chunks, LANE, 2 * LANE), F32)],
        compiler_params=_cparams(1),
        name="gla_mixer",
    )(proj, proj, proj, proj, proj, prm['gk2p'], prm['gvec'], *([] if s0 is None else [s0]))
    return y, s_out


OUT_TM = 512
ROUTE_NEG = -1e30
LANE_GROUP0 = N_EXPERTS


def _route(logits):
    lane = _iota(logits.shape, 1)
    lane_f = lane.astype(F32)
    big = float(LANE)
    is_g = (lane >= LANE_GROUP0) & (lane < LANE_GROUP0 + 4)
    gmax = jnp.max(jnp.where(is_g, logits, ROUTE_NEG), axis=-1, keepdims=True)
    gidx = jnp.min(jnp.where(is_g & (logits == gmax), lane_f, big), axis=-1, keepdims=True) - LANE_GROUP0
    gsum = jnp.sum(jnp.where(is_g, jnp.exp(jnp.minimum(logits - gmax, 0.0)), 0.0), axis=-1, keepdims=True)
    g_w = 1.0 / gsum
    in_grp = (lane < N_EXPERTS) & ((lane // 4).astype(F32) == gidx)
    m1 = jnp.max(jnp.where(in_grp, logits, ROUTE_NEG), axis=-1, keepdims=True)
    i1 = jnp.min(jnp.where(in_grp & (logits == m1), lane_f, big), axis=-1, keepdims=True)
    rest = in_grp & (lane_f != i1)
    m2 = jnp.max(jnp.where(rest, logits, ROUTE_NEG), axis=-1, keepdims=True)
    i2 = jnp.min(jnp.where(rest & (logits == m2), lane_f, big), axis=-1, keepdims=True)
    t = jnp.exp(m2 - m1)
    w1 = g_w / (1.0 + t)
    return jnp.where(lane_f == i1, w1, 0.0) + jnp.where(lane_f == i2, w1 * t, 0.0)


def _outproj_kernel(tiles, yrc_ref, yrd_ref, ygc_ref, ygd_ref, xc_ref, xd_ref, mod_ref, wo_ref, g_ref, wr_ref,
                    br_ref, x1_ref, h2_ref, cmb_ref):
    def run(yr_ref, yg_ref, x_ref):
        m = mod_ref[0]
        mix = _mm(yr_ref[...], wo_ref[0:RW_W, :]) + _mm(yg_ref[...], wo_ref[RW_W:RW_W + GLA_V_W, :])
        x1 = x_ref[...] + m[2:3] * mix
        h2 = _rmsnorm_rows(x1) * g_ref[...] * (1.0 + m[4:5]) + m[3:4]
        x1_ref[...] = x1
        h2_ref[...] = h2.astype(BF16)
        h_hi, h_lo = _split2(h2)
        both = _dot(h_hi, wr_ref[...])
        logits = both[:, 0:LANE] + both[:, LANE:2 * LANE] + _dot(h_lo, wr_ref[:, 0:LANE])
        cmb_ref[...] = _route(logits + br_ref[...])

    tiles.by_pass(functools.partial(run, yrc_ref, ygc_ref, xc_ref), functools.partial(run, yrd_ref, ygd_ref, xd_ref))


def _out_projection(tiles, y_rw, y_gla, x, mod, w_out, norm_g, w_route, b_route):
    n = x[0].shape[0] + x[1].shape[0]
    full = lambda a: pl.BlockSpec(a.shape, lambda i: (0,) * a.ndim)
    return pl.pallas_call(
        functools.partial(_outproj_kernel, tiles),
        grid=(tiles.n_ctx + tiles.n_dec,),
        in_specs=[*tiles.specs(RW_W), *tiles.specs(GLA_V_W), *tiles.specs(D_MODEL), tiles.mod_spec(),
                  full(w_out), full(norm_g), full(w_route), full(b_route)],
        out_specs=[tiles.merged(D_MODEL), tiles.merged(D_MODEL), tiles.merged(LANE)],
        out_shape=[jax.ShapeDtypeStruct((n, D_MODEL), F32), jax.ShapeDtypeStruct((n, D_MODEL), BF16),
                   jax.ShapeDtypeStruct((n, LANE), F32)],
        compiler_params=_cparams(1),
        name="out_proj_router",
    )(*y_rw, *y_gla, *x, mod, w_out, norm_g, w_route, b_route)


MOE_TM = 512
MOE_RB = 128
MOE_INTERLEAVE = 4
SLOT_ALIGN = 16
MOE_SLOTS = 2 * MOE_TM + N_EXPERTS * SLOT_ALIGN


def _stage_expert_weights(srcs_hbm, dst_refs, stage_refs, sems):
    def copies(e):
        return [pltpu.make_async_copy(src.at[e], stage.at[e % 2], sem.at[e % 2])
                for src, stage, sem in zip(srcs_hbm, stage_refs, sems)]

    for c in copies(0):
        c.start()
    for e in range(N_EXPERTS):
        if e + 1 < N_EXPERTS:
            for c in copies(e + 1):
                c.start()
        for c, dst, stage in zip(copies(e), dst_refs, stage_refs):
            c.wait()
            dst[e] = stage[e % 2].astype(BF16)


def _moe_kernel(tiles, h2_ref, cmb_ref, x1_ref, mod_ref, w1_hbm, w3_hbm, w2_hbm, fg_ref, yc_ref, yd_ref,
                xs_ref, ys_ref, w1_ref, w3_ref, w2_ref, stage1_ref, stage3_ref, stage2_ref, sem1, sem3, sem2):
    @pl.when(pl.program_id(0) == 0)
    def _():
        _stage_expert_weights((w1_hbm, w3_hbm, w2_hbm), (w1_ref, w3_ref, w2_ref),
                              (stage1_ref, stage3_ref, stage2_ref), (sem1, sem3, sem2))

    cmb = cmb_ref[...]
    lane = _iota(cmb.shape, 1).astype(F32)
    sel = cmb > 0.0
    sel01 = jnp.where(sel, 1.0, 0.0).astype(BF16)
    before = (_iota((MOE_TM, MOE_TM), 0) > _iota((MOE_TM, MOE_TM), 1)).astype(BF16)
    pos = _dot(before, sel01)
    cnt = pos[MOE_TM - 1:MOE_TM] + sel01[MOE_TM - 1:MOE_TM].astype(F32)
    seg = jnp.floor((cnt + (SLOT_ALIGN - 1)) * (1.0 / SLOT_ALIGN))
    lower_experts = (_iota((LANE, LANE), 0) < _iota((LANE, LANE), 1)).astype(BF16)
    start = _dot(jnp.broadcast_to(seg, (8, LANE)).astype(BF16), lower_experts)[0:1] * SLOT_ALIGN
    n_blk = jnp.floor((cnt + (MOE_RB - 1)) * (1.0 / MOE_RB)).astype(jnp.int32)
    start_i = start.astype(jnp.int32)
    cnt_i = cnt.astype(jnp.int32)
    slot = start + pos
    e_a = jnp.min(jnp.where(sel, lane, float(LANE)), axis=-1, keepdims=True)
    e_b = jnp.max(jnp.where(sel, lane, -1.0), axis=-1, keepdims=True)
    pick = lambda e, x: jnp.sum(jnp.where(lane == e, x, 0.0), axis=-1, keepdims=True)
    slot_a, w_a = pick(e_a, slot), pick(e_a, cmb)
    slot_b = jnp.where(e_b != e_a, pick(e_b, slot), -1.0)
    w_b = pick(e_b, cmb)

    slots_t = jnp.where(lane == 0.0, slot_a, jnp.where(lane == 1.0, slot_b, -1.0)).T
    row_slot = _iota((MOE_SLOTS, MOE_TM), 0).astype(F32)
    gather = jnp.where((row_slot == slots_t[0:1]) | (row_slot == slots_t[1:2]), 1.0, 0.0).astype(BF16)
    xs_ref[...] = _dot(gather, h2_ref[...]).astype(BF16)
    ys_ref[...] = jnp.zeros_like(ys_ref)

    row_in_blk = _iota((MOE_RB, D_MODEL), 0)

    def expert_blocks(experts, r0s, ends):
        base = [pl.multiple_of(jnp.minimum(r0, MOE_SLOTS - MOE_RB), SLOT_ALIGN) for r0 in r0s]
        xbs = [xs_ref[pl.ds(b, MOE_RB), :] for b in base]
        gates = [_dot(xb, w3_ref[e]) for xb, e in zip(xbs, experts)]
        ups = [_dot(xb, w1_ref[e]) for xb, e in zip(xbs, experts)]
        acts = [(_silu(g) * u).astype(BF16) for g, u in zip(gates, ups)]
        outs = [_dot(a, w2_ref[e]) for a, e in zip(acts, experts)]
        for b, r0, end, out in zip(base, r0s, ends, outs):
            row = row_in_blk + b
            keep = (row >= end) | (row < r0)
            ys_ref[pl.ds(b, MOE_RB), :] = jnp.where(keep, ys_ref[pl.ds(b, MOE_RB), :], out.astype(BF16))

    seg_start = [pl.multiple_of(start_i[0, e], SLOT_ALIGN) for e in range(N_EXPERTS)]
    seg_end = [seg_start[e] + cnt_i[0, e] for e in range(N_EXPERTS)]
    for e0 in range(0, N_EXPERTS, MOE_INTERLEAVE):
        es = list(range(e0, e0 + MOE_INTERLEAVE))
        expert_blocks(es, [seg_start[e] for e in es], [seg_end[e] for e in es])
    for e in range(N_EXPERTS):
        def extra_block(b, carry, e=e):
            expert_blocks([e], [pl.multiple_of(seg_start[e] + b * MOE_RB, SLOT_ALIGN)], [seg_end[e]])
            return carry

        lax.fori_loop(1, n_blk[0, e], extra_block, 0)

    col_slot = _iota((MOE_TM, MOE_SLOTS), 1).astype(F32)
    scatter = (jnp.where(col_slot == slot_a, w_a, 0.0) + jnp.where(col_slot == slot_b, w_b, 0.0)).astype(BF16)
    x2 = x1_ref[...] + mod_ref[0][5:6] * _dot(scatter, ys_ref[...])
    y = _rmsnorm_rows(x2) * fg_ref[...]

    def write(y_ref):
        y_ref[...] = y

    tiles.by_pass(functools.partial(write, yc_ref), functools.partial(write, yd_ref))


def _moe(tiles, h2, cmb, x1, mod, w1, w3, w2, final_g):
    assert tiles.tm == MOE_TM
    hbm = pl.BlockSpec(memory_space=pl.ANY)
    out_ctx, out_dec = tiles.specs(D_MODEL)
    return pl.pallas_call(
        functools.partial(_moe_kernel, tiles),
        grid=(tiles.n_ctx + tiles.n_dec,),
        in_specs=[tiles.merged(D_MODEL), tiles.merged(LANE), tiles.merged(D_MODEL), tiles.mod_spec(),
                  hbm, hbm, hbm, pl.BlockSpec((1, D_MODEL), lambda i: (0, 0))],
        out_specs=[out_ctx, out_dec],
        out_shape=[jax.ShapeDtypeStruct((tiles.n_ctx * MOE_TM, D_MODEL), F32),
                   jax.ShapeDtypeStruct((tiles.n_dec * MOE_TM, D_MODEL), F32)],
        scratch_shapes=[pltpu.VMEM((MOE_SLOTS, D_MODEL), BF16), pltpu.VMEM((MOE_SLOTS, D_MODEL), BF16),
                        pltpu.VMEM((N_EXPERTS, D_MODEL, D_EXPERT), BF16),
                        pltpu.VMEM((N_EXPERTS, D_MODEL, D_EXPERT), BF16),
                        pltpu.VMEM((N_EXPERTS, D_EXPERT, D_MODEL), BF16),
                        pltpu.VMEM((2, D_MODEL, D_EXPERT), F32), pltpu.VMEM((2, D_MODEL, D_EXPERT), F32),
                        pltpu.VMEM((2, D_EXPERT, D_MODEL), F32)] + [pltpu.SemaphoreType.DMA((2,))] * 3,
        compiler_params=_cparams(1),
        name="moe_experts",
    )(h2, cmb, x1, mod, w1, w3, w2, final_g)


def _pad_rows(x, rows):
    return jnp.pad(x, ((0, rows - x.shape[0]),) + ((0, 0),) * (x.ndim - 1))


def _pack_params(l, w_in, rw_conv, rw_w0, rw_w2, rw_a0, rw_a2, rw_g2, rw_k_k, rw_k_a, rw_r_k, rw_ln_w, rw_ln_b,
                 gla_gk2, gla_gk_b, gla_norm_g, moe_w_group, moe_b_group, moe_w_expert, moe_b_expert):
    wi = w_in[l]
    z = lambda n: jnp.zeros((D_MODEL, n), F32)
    w_in_t = jnp.swapaxes(wi, 0, 1)
    z64 = jnp.zeros((64, RW_W), F32)
    w2p = jnp.stack([jnp.concatenate([rw_w2[l, 0], z64], 0), jnp.concatenate([z64, rw_w2[l, 1]], 0)])
    a2p = jnp.stack([jnp.concatenate([rw_a2[l, 0], z64], 0), jnp.concatenate([z64, rw_a2[l, 1]], 0)])
    vec = _pad_rows(jnp.stack([rw_w0[l, 0], rw_w0[l, 1], rw_a0[l, 0], rw_a0[l, 1], rw_k_k[l], rw_k_a[l],
                               rw_r_k[l].reshape(RW_W), rw_ln_w[l], rw_ln_b[l]]), 16)
    rw = {'conv': rw_conv[l].reshape(9, 3 * RW_W), 'w2p': w2p.astype(BF16), 'a2p': a2p.astype(BF16),
          'g2': rw_g2[l].astype(BF16), 'vec': vec}
    gk2p = jnp.stack([_pad_rows(gla_gk2[l, 0], LANE),
                      _pad_rows(jnp.concatenate([jnp.zeros((16, GLA_QK_W), F32), gla_gk2[l, 1]], 0), LANE)])
    gk_b = jnp.pad(gla_gk_b[l], ((0, 0), (0, GLA_V_W - GLA_QK_W)))
    gvec = _pad_rows(jnp.concatenate([gk_b, jnp.tile(gla_norm_g[l], GLA_V_W // LANE)[None]], axis=0), 8)
    gla = {'gk2p': gk2p.astype(BF16), 'gvec': gvec}
    w_route = jnp.concatenate(_split2(jnp.concatenate([moe_w_expert[l], moe_w_group[l], z(LANE - N_EXPERTS - 4)],
                                                      axis=1)), axis=1)
    b_route = jnp.concatenate([moe_b_expert[l], moe_b_group[l], jnp.zeros((LANE - N_EXPERTS - 4,), F32)])[None]
    return w_in_t, rw, gla, w_route, b_route


def kernel(x_prompt, x_sample, state_rwkv, state_gla, c, c_ctx, norm1_g, norm2_g, w_ada, b_ada, w_in, w_out,
           rw_conv, rw_w0, rw_w2, rw_a0, rw_a2, rw_g2, rw_k_k, rw_k_a, rw_r_k, rw_ln_w, rw_ln_b,
           gla_gk2, gla_gk_b, gla_norm_g, moe_w_group, moe_b_group, moe_w_expert, moe_b_expert,
           moe_w1, moe_w3, moe_w2, final_g):
    depth = w_in.shape[0]
    assert depth == 1, "the packed layout below handles the single-layer trunk of this problem"
    l = 0
    n_dec = x_sample.shape[0]
    ctx_row = n_dec
    cond8 = _pad_rows(jnp.concatenate([c, c_ctx[None]], axis=0), 8)
    mod = _modulation(cond8, w_ada[l], b_ada[l][None]).reshape(8, N_MOD, D_MODEL)
    pk = _pack_params(l, w_in, rw_conv, rw_w0, rw_w2, rw_a0, rw_a2, rw_g2, rw_k_k, rw_k_a, rw_r_k, rw_ln_w,
                      rw_ln_b, gla_gk2, gla_gk_b, gla_norm_g, moe_w_group, moe_b_group, moe_w_expert,
                      moe_b_expert)
    w_in_t, rw, gla, w_route, b_route = pk

    n_ctx, ctx_len, _ = x_prompt.shape
    dec_len = x_sample.shape[1]
    x_ctx = x_prompt.reshape(n_ctx * ctx_len, D_MODEL)
    x_dec = x_sample.reshape(n_dec * dec_len, D_MODEL)
    assert (n_ctx * ctx_len) % dec_len == 0, "denoising sequences must start on a dec_len row block of proj"
    first_dec = n_ctx * ctx_len // dec_len
    tiles = _Tiles(n_ctx * ctx_len, n_dec * dec_len, dec_len, ctx_row, PROJ_TM)
    assert PROJ_TM == OUT_TM == MOE_TM

    proj = _in_projection(tiles, x_ctx, x_dec, mod, norm1_g[l][None], w_in_t)
    y_rw_c, s_rw = _rwkv_mixer(proj, 0, n_ctx, ctx_len, False, rw, None)
    y_gla_c, s_gla = _gla_mixer(proj, 0, n_ctx, ctx_len, gla, None)
    y_rw_d, _ = _rwkv_mixer(proj, first_dec, n_dec, dec_len, True, rw, state_rwkv[:, l])
    y_gla_d, _ = _gla_mixer(proj, first_dec, n_dec, dec_len, gla, state_gla[:, l])
    x1, h2, cmb = _out_projection(tiles, (y_rw_c, y_rw_d), (y_gla_c, y_gla_d), (x_ctx, x_dec), mod,
                                  w_out[l].astype(BF16), norm2_g[l][None], w_route, b_route)
    y_ctx, y_dec = _moe(tiles, h2, cmb, x1, mod, moe_w1[l], moe_w3[l], moe_w2[l], final_g[None])
    return (y_ctx.reshape(x_prompt.shape), y_dec.reshape(x_sample.shape), s_rw[:, None], s_gla[:, None])
```

```python
import functools
import itertools

import jax
import jax.numpy as jnp
from jax import lax
from jax.experimental import pallas as pl
from jax.experimental.pallas import tpu as pltpu

F32 = jnp.float32
BF16 = jnp.bfloat16

D_MODEL = 1024
RW_W = 512
GLA_V_W = 512
GLA_QK_W = 256
N_EXPERTS = 16
D_EXPERT = 256
N_MOD = 6
EPS = 1e-6
RW_LN_EPS = 64e-5
RW_DECAY_SCALE = 0.606531
GLA_GATE_NORM = 16.0
GLA_Q_SCALE = 64 ** -0.5
GRID_W = 64

LANE = 128
CHUNK = 64
CONV_PAD = 128
TERM_UNROLL = 4
RW_PAIRS_PER_STEP = 2
GLA_PAIRS = 2
RW_GROUPS = 4
D_PROJ = 28 * LANE
VMEM_LIMIT = 56 * 1024 * 1024

CB_R, CB_K, CB_V, CB_LORA, CB_LGK, CB_GQ, CB_GK, CB_GV, CB_OG = 0, 4, 8, 12, 15, 16, 18, 20, 24

_NN = (((1,), (0,)), ((), ()))
_NT = (((1,), (1,)), ((), ()))
_TN = (((0,), (0,)), ((), ()))


def _dot(a, b, dims=_NN):
    return lax.dot_general(a, b, dims, preferred_element_type=F32)


def _mm(a, b, dims=_NN):
    return _dot(a.astype(BF16), b.astype(BF16), dims)


def _split2(x):
    hi = x.astype(BF16)
    lo = (x - hi.astype(F32)).astype(BF16)
    return hi, lo


def _mm3(a, b, dims=_NN):
    ah, al = _split2(a)
    bh, bl = _split2(b)
    return _dot(ah, bh, dims) + _dot(ah, bl, dims) + _dot(al, bh, dims)


def _mm_01_lhs(a01, b, dims=_NN):
    n = b.shape[1]
    both = _dot(a01, jnp.concatenate(_split2(b), axis=1), dims)
    return both[:, 0:n] + both[:, n:2 * n]


def _sigmoid(x):
    return 0.5 * jnp.tanh(0.5 * x) + 0.5


def _silu(x):
    return x * _sigmoid(x)


def _log_sigmoid(x):
    return jnp.minimum(x, 0.0) - jnp.log(1.0 + jnp.exp(-jnp.abs(x)))


def _iota(shape, dim):
    return lax.broadcasted_iota(jnp.int32, shape, dim)


def _cparams(n_axes):
    return pltpu.CompilerParams(dimension_semantics=("arbitrary",) * n_axes, vmem_limit_bytes=VMEM_LIMIT)


MOD_TN = 768


def _mod_kernel(c_ref, w_ref, b_ref, o_ref):
    o_ref[...] = _mm3(_silu(c_ref[...]), w_ref[...]) + b_ref[...]


def _modulation(cond8, w_ada, b_ada):
    n = w_ada.shape[1]
    return pl.pallas_call(
        _mod_kernel,
        grid=(n // MOD_TN,),
        in_specs=[pl.BlockSpec((8, D_MODEL), lambda j: (0, 0)),
                  pl.BlockSpec((D_MODEL, MOD_TN), lambda j: (0, j)),
                  pl.BlockSpec((1, MOD_TN), lambda j: (0, j))],
        out_specs=pl.BlockSpec((8, MOD_TN), lambda j: (0, j)),
        out_shape=jax.ShapeDtypeStruct((8, n), F32),
        compiler_params=_cparams(1),
        name="adaln_mod",
    )(cond8, w_ada, b_ada)


PROJ_TM = 512
D_IN = 3488
N_LGK = 32


def _rmsnorm_rows(x):
    return x * lax.rsqrt(jnp.mean(x * x, axis=-1, keepdims=True) + EPS)


class _Tiles:
    def __init__(self, n_ctx_tokens, n_dec_tokens, dec_seq_len, ctx_row, tm):
        self.tm = tm
        self.n_ctx = n_ctx_tokens // tm
        self.n_dec = n_dec_tokens // tm
        self.per_seq = dec_seq_len // tm
        self.ctx_row = ctx_row

    def specs(self, width):
        last_ctx = self.n_ctx - 1
        n_ctx = self.n_ctx
        return (pl.BlockSpec((self.tm, width), lambda i: (jnp.minimum(i, last_ctx), 0)),
                pl.BlockSpec((self.tm, width), lambda i: (jnp.maximum(i - n_ctx, 0), 0)))

    def merged(self, width):
        return pl.BlockSpec((self.tm, width), lambda i: (i, 0))

    def mod_spec(self):
        n_ctx, per_seq, ctx_row = self.n_ctx, self.per_seq, self.ctx_row
        return pl.BlockSpec((1, N_MOD, D_MODEL),
                            lambda i: (jnp.where(i < n_ctx, ctx_row, (i - n_ctx) // per_seq), 0, 0))

    def by_pass(self, run_ctx, run_dec):
        i = pl.program_id(0)
        pl.when(i < self.n_ctx)(run_ctx)
        pl.when(i >= self.n_ctx)(run_dec)


def _inproj_kernel(tiles, xc_ref, xd_ref, mod_ref, g_ref, wt_ref, o_ref, w_ref):
    @pl.when(pl.program_id(0) == 0)
    def _():
        for j in range(D_PROJ // LANE):
            if j == CB_LGK:
                blk = jnp.concatenate([wt_ref[D_IN - N_LGK:D_IN, :], jnp.zeros((LANE - N_LGK, D_MODEL), F32)], axis=0)
            else:
                src = j if j < CB_LGK else j - 1
                blk = wt_ref[src * LANE:(src + 1) * LANE, :]
            w_ref[:, j * LANE:(j + 1) * LANE] = blk.T.astype(BF16)

    def run(x_ref):
        m = mod_ref[0]
        h = _rmsnorm_rows(x_ref[...]) * g_ref[...] * (1.0 + m[1:2]) + m[0:1]
        o_ref[...] = _mm(h, w_ref[...])

    tiles.by_pass(functools.partial(run, xc_ref), functools.partial(run, xd_ref))


def _in_projection(tiles, x_ctx, x_dec, mod, norm_g, w_in_t):
    full = lambda a: pl.BlockSpec(a.shape, lambda i: (0,) * a.ndim)
    return pl.pallas_call(
        functools.partial(_inproj_kernel, tiles),
        grid=(tiles.n_ctx + tiles.n_dec,),
        in_specs=[*tiles.specs(D_MODEL), tiles.mod_spec(), full(norm_g),
                  pl.BlockSpec(w_in_t.shape, lambda i: (0, 0), pipeline_mode=pl.Buffered(1))],
        out_specs=tiles.merged(D_PROJ),
        out_shape=jax.ShapeDtypeStruct((x_ctx.shape[0] + x_dec.shape[0], D_PROJ), F32),
        scratch_shapes=[pltpu.VMEM((D_MODEL, D_PROJ), BF16)],
        compiler_params=_cparams(1),
        name="in_proj",
    )(x_ctx, x_dec, mod, norm_g, w_in_t)


def _time_masks(reverse):
    r = _iota((2 * CHUNK, 2 * CHUNK), 0) % CHUNK
    c = _iota((2 * CHUNK, 2 * CHUNK), 1) % CHUNK
    if reverse:
        return r < c, r <= c
    return r > c, r >= c


def _cumsum_matrix(reverse):
    r = _iota((CHUNK, CHUNK), 0)
    c = _iota((CHUNK, CHUNK), 1)
    tri = (r <= c) if reverse else (r >= c)
    return tri.astype(BF16)


def _stack_heads(x, half):
    m0 = _iota(x.shape, 1) < half
    return jnp.concatenate([jnp.where(m0, x, 0.0), jnp.where(m0, 0.0, x)], axis=0)


def _finished_before_last_scan(groups, n_chunks):
    done_fwd, done_bwd = set(), set()
    for s, fwd, bwd in groups[:-1]:
        done_fwd.update(s * n_chunks + c for c in fwd)
        done_bwd.update(s * n_chunks + c for c in bwd)
    return sorted(done_fwd & done_bwd)


def _head_sums(x):
    parts = []
    for p in range(x.shape[1] // LANE):
        xp = x[:, p * LANE:(p + 1) * LANE]
        m0 = _iota(xp.shape, 1) < 64
        s0 = jnp.sum(jnp.where(m0, xp, 0.0), axis=-1, keepdims=True)
        s1 = jnp.sum(jnp.where(m0, 0.0, xp), axis=-1, keepdims=True)
        parts.append(jnp.where(m0, s0, s1))
    return parts[0] if len(parts) == 1 else jnp.concatenate(parts, axis=1)


def _rwkv_chunk_terms(insts, interleaved=()):
    c = CHUNK
    step_row = _iota((c, LANE), 0)
    step_col = _iota((c, LANE), 1) % c
    eye_w = (step_row == step_col).astype(F32)
    same_head = (_iota((LANE, LANE), 0) // 64) == (_iota((LANE, LANE), 1) // 64)
    stack_bf = lambda x: _stack_heads(x, 64).astype(BF16)
    cums = [_mm_01_lhs(_cumsum_matrix(rev), lw) for (_, lw, _, _, _, _, rev) in insts]
    pre = []
    for (r, lw, kd, a, b, v, rev), cum in zip(insts, cums):
        end = cum[0:1] if rev else cum[c - 1:c]
        inv_w = jnp.exp(-cum)
        rem_w = jnp.exp(end - cum)
        a_t = a * jnp.exp(cum - lw)
        r_t = r * jnp.exp(cum)
        bk_s = jnp.concatenate([stack_bf(b * inv_w), stack_bf(kd * inv_w)], axis=0)
        bkh = jnp.concatenate([b * rem_w, kd * rem_w], axis=0).astype(BF16)
        pre.append((a_t, r_t, bk_s, bkh, v, jnp.exp(end)))
    ms = [_dot(jnp.concatenate([a_t, r_t], axis=0).astype(BF16), bk_s, _NT) for (a_t, r_t, bk_s, _, _, _) in pre]
    mats = []
    for m, (_, _, _, _, _, _, rev) in zip(ms, insts):
        strict = (step_row < step_col) if rev else (step_row > step_col)
        incl = (step_row <= step_col) if rev else (step_row >= step_col)
        l_ab = jnp.where(strict, m[0:c, 0:LANE], 0.0)
        l_akrk = jnp.concatenate([jnp.where(strict, m[0:c, LANE:2 * LANE], 0.0),
                                  jnp.where(incl, m[c:2 * c, LANE:2 * LANE], 0.0)], axis=0).astype(BF16)
        m_rb = jnp.where(incl, m[c:2 * c, 0:LANE], 0.0).astype(BF16)
        mats.append((l_ab, l_akrk, m_rb))
    pending = list(interleaved)

    def run_interleaved():
        if pending:
            pending.pop(0)()

    lvs = [_dot(l_akrk, stack_bf(pr[4])) for (_, l_akrk, _), pr in zip(mats, pre)]
    run_interleaved()
    ps = [eye_w + l_ab for (l_ab, _, _) in mats]
    lps = [_dot(l_ab.astype(BF16), stack_bf(l_ab)) for (l_ab, _, _) in mats]
    for level in range(1, 6):
        if level < 5:
            xs = [_dot(lp.astype(BF16), jnp.concatenate([stack_bf(p), stack_bf(lp)], axis=1))
                  for lp, p in zip(lps, ps)]
            ps = [p + x[:, 0:LANE] for p, x in zip(ps, xs)]
            lps = [x[:, LANE:2 * LANE] for x in xs]
        else:
            ps = [p + _dot(lp.astype(BF16), stack_bf(p)) for lp, p in zip(lps, ps)]
        if level in (2, 4):
            run_interleaved()
    pxs = [_dot(p.astype(BF16), jnp.concatenate([stack_bf(pr[0]), stack_bf(lv[0:c])], axis=1))
           for p, pr, lv in zip(ps, pre, lvs)]
    run_interleaved()
    mzs = [_dot(mt[2], jnp.concatenate([stack_bf(px[:, 0:LANE]), stack_bf(px[:, LANE:2 * LANE])], axis=1))
           for mt, px in zip(mats, pxs)]
    ts = [_dot(px[:, 0:LANE].astype(BF16), pr[3][0:c], _TN) for px, pr in zip(pxs, pre)]
    gs = [_dot(jnp.concatenate([px[:, LANE:2 * LANE], pr[4]], axis=0).astype(BF16), pr[3], _TN)
          for px, pr in zip(pxs, pre)]
    while pending:
        run_interleaved()
    out = []
    for pr, lv, mz, t, g in zip(pre, lvs, mzs, ts, gs):
        q = pr[1] + mz[:, 0:LANE]
        y0 = mz[:, LANE:2 * LANE] + lv[c:2 * c]
        g_wide = jnp.where(_iota((c, LANE), 1) < 64, g[0:c], g[c:2 * c])
        out.append((jnp.where(same_head, t, 0.0).astype(BF16), g_wide, pr[5],
                    q.astype(BF16), y0))
    return out


def _gla_chunk_terms(insts, interleaved=()):
    c = CHUNK
    step_row = _iota((c, LANE), 0)
    step_col = _iota((c, LANE), 1) % c
    same_head = (_iota((LANE, 2 * LANE), 0) // 64) == (_iota((LANE, 2 * LANE), 1) // LANE)
    cums = [_mm_01_lhs(_cumsum_matrix(rev), g) for (_, _, _, g, rev) in insts]
    pre = []
    for (q, k, v, g, rev), cum in zip(insts, cums):
        end = cum[0:1] if rev else cum[c - 1:c]
        qt = (q * jnp.exp(cum)).astype(BF16)
        k_s = _stack_heads(k * jnp.exp(-cum), 64).astype(BF16)
        kh = (k * jnp.exp(end - cum)).astype(BF16)
        a_col = jnp.broadcast_to(jnp.exp(end), (LANE, LANE)).T
        pre.append((qt, k_s, kh, v.astype(BF16), _stack_heads(v, LANE).astype(BF16), a_col))
    pending = list(interleaved)

    def run_interleaved():
        if pending:
            pending.pop(0)()

    run_interleaved()
    atts = [_dot(pr[0], pr[1], _NT) for pr in pre]
    run_interleaved()
    atts = [jnp.where((step_row <= step_col) if inst[4] else (step_row >= step_col), att, 0.0).astype(BF16)
            for att, inst in zip(atts, insts)]
    o0s = [_dot(att, pr[4]) for att, pr in zip(atts, pre)]
    run_interleaved()
    kvs = [jnp.where(same_head, _dot(pr[2], pr[3], _TN), 0.0) for pr in pre]
    while pending:
        run_interleaved()
    return [(pr[0], o0, pr[5], kv) for pr, o0, kv in zip(pre, o0s, kvs)]


def _rwkv_kernel(seq_len, nseq, is_grid, zero_init, r_ref, k_ref, v_ref, lora_ref, cwr_ref, cwk_ref, cwv_ref,
                 w2_ref, a2_ref, g2_ref, vec_ref, *rest):
    s0_ref = None if zero_init else rest[0]
    (y_ref, sout_ref, pad_ref, rs_ref, ks_ref, vs_ref, kk_ref, bonus_ref, gate_ref, yf_ref, yb_ref, st_ref, tt_ref,
     tg_ref, tw_ref, tq_ref, ty_ref) = rest[0 if zero_init else 1:][:17]
    left_ref, right_ref = rest[-2:] if is_grid else (None, None)
    n_chunks = seq_len // CHUNK
    npp = RW_PAIRS_PER_STEP
    w = npp * LANE
    pair = lambda x, p: x[:, p * LANE:(p + 1) * LANE]
    vec = vec_ref[...]
    w0 = (vec[0:1], vec[1:2])
    a0 = (vec[2:3], vec[3:4])
    k_k, k_a, r_k, ln_w, ln_b = vec[4:5], vec[5:6], vec[6:7], vec[7:8], vec[8:9]
    block_sum = _head_sums
    chains = [(d, p) for d in range(2) for p in range(npp)]
    rows_of = lambda s, c: slice((s * n_chunks + c) * CHUNK, (s * n_chunks + c + 1) * CHUNK)

    zeros = jnp.zeros((CONV_PAD, 3 * w), F32)
    for ref in (pad_ref, left_ref, right_ref) if is_grid else (pad_ref,):
        ref[0:CONV_PAD, :] = zeros
        ref[CONV_PAD + seq_len:2 * CONV_PAD + seq_len, :] = zeros
    cw = jnp.concatenate([cwr_ref[...], cwk_ref[...], cwv_ref[...]], axis=1)
    col = _iota((CHUNK, 3 * w), 0)

    def shift_body(c, carry):
        base = pl.multiple_of(CONV_PAD + c * CHUNK, CHUNK)
        win = pad_ref[pl.ds(base - 8, CHUNK + 16), :]
        left_ref[pl.ds(base, CHUNK), :] = jnp.where(col >= 1, win[7:7 + CHUNK], 0.0)
        right_ref[pl.ds(base, CHUNK), :] = jnp.where(col <= GRID_W - 2, win[9:9 + CHUNK], 0.0)
        return carry

    def conv_chunk(c):
        base = CONV_PAD + c * CHUNK
        acc = jnp.zeros((CHUNK, 3 * w), F32)
        if is_grid:
            for di in (-1, 0, 1):
                row = pl.ds(base + di * GRID_W, CHUNK)
                for dj, src in ((-1, left_ref), (0, pad_ref), (1, right_ref)):
                    tap = (di + 1) * 3 + dj + 1
                    acc = acc + src[row, :] * cw[tap:tap + 1]
        else:
            win = pad_ref[pl.ds(base - 8, CHUNK + 16), :]
            for dj in (-1, 0, 1):
                acc = acc + win[8 + dj:8 + dj + CHUNK] * cw[4 + dj:5 + dj]
        return acc[:, 0:w], acc[:, w:2 * w], acc[:, 2 * w:3 * w]

    def load_sequence(s):
        seq_rows = slice(s * seq_len, (s + 1) * seq_len)
        pad_ref[CONV_PAD:CONV_PAD + seq_len, 0:w] = r_ref[seq_rows, :]
        pad_ref[CONV_PAD:CONV_PAD + seq_len, w:2 * w] = k_ref[seq_rows, :]
        pad_ref[CONV_PAD:CONV_PAD + seq_len, 2 * w:3 * w] = v_ref[seq_rows, :]

    def conv_store(s, c):
        rows = rows_of(s, c)
        rc, kc, vc = conv_chunk(c)
        kk = kc * k_k
        rs_ref[rows, :] = rc
        ks_ref[rows, :] = kc
        vs_ref[rows, :] = vc
        kk_ref[rows, :] = kk * lax.rsqrt(block_sum(kk * kk) + EPS)
        bonus_ref[rows, :] = block_sum(rc * kc * r_k) * vc
        gate_ref[rows, :] = _mm(_sigmoid(lora_ref[rows, 2 * LANE:3 * LANE]), g2_ref[...])

    for s in range(nseq):
        for d, p in chains:
            if zero_init:
                st_ref[s, d, p] = jnp.zeros((64, LANE), F32)
            else:
                st_ref[s, d, p] = jnp.concatenate([s0_ref[s, d, 2 * p], s0_ref[s, d, 2 * p + 1]], axis=1)

    if nseq == 1:
        groups = [(0, list(range(g * TERM_UNROLL, (g + 1) * TERM_UNROLL)),
                   list(range(n_chunks - 1 - g * TERM_UNROLL, n_chunks - 1 - (g + 1) * TERM_UNROLL, -1)))
                  for g in range(n_chunks // TERM_UNROLL)]
    else:
        groups = [(s, list(range(n_chunks)), list(range(n_chunks - 1, -1, -1))) for s in range(nseq)]

    def group_terms(group, interleaved):
        s, fwd, bwd = group
        insts, where, lora_in = [], [], {}
        for d, chunks in ((0, fwd), (1, bwd)):
            for c in chunks:
                rows = rows_of(s, c)
                if c not in lora_in:
                    lora_in[c] = (jnp.tanh(lora_ref[rows, 0:LANE]).astype(BF16),
                                  lora_ref[rows, LANE:2 * LANE].astype(BF16))
                lw = -RW_DECAY_SCALE * _sigmoid(w0[d] + _dot(lora_in[c][0], w2_ref[d]))
                ag = _sigmoid(a0[d] + _dot(lora_in[c][1], a2_ref[d]))
                rc, kc, vc, kk = rs_ref[rows, :], ks_ref[rows, :], vs_ref[rows, :], kk_ref[rows, :]
                kd = kc * (1.0 + (ag - 1.0) * k_a)
                kb = kk * ag
                for p in range(npp):
                    insts.append((pair(rc, p), pair(lw, p), pair(kd, p), -pair(kk, p), pair(kb, p), pair(vc, p),
                                  d == 1))
                    where.append((d, p, s * n_chunks + c))
        for (d, p, gc), (t, g, w_end, q, y0) in zip(where, _rwkv_chunk_terms(insts, interleaved)):
            tt_ref[d, p, gc] = t
            tg_ref[d, p, gc] = g
            tw_ref[d, p, gc] = jnp.broadcast_to(w_end, (8, LANE))
            tq_ref[d, p, gc] = q
            ty_ref[d, p, gc] = y0

    def scan_step(s, chunk_of_dir):
        gcs = [s * n_chunks + chunk_of_dir[d] for d, _ in chains]
        ss = [st_ref[s, d, p] for d, p in chains]
        ys = [_dot(tq_ref[d, p, gc], _stack_heads(x, 64).astype(BF16), _NT) + ty_ref[d, p, gc]
              for (d, p), gc, x in zip(chains, gcs, ss)]
        sn = [x * tw_ref[d, p, gc][0:1] + _dot(x.astype(BF16), tt_ref[d, p, gc]) + tg_ref[d, p, gc]
              for (d, p), gc, x in zip(chains, gcs, ss)]
        for (d, p), gc, y, x in zip(chains, gcs, ys, sn):
            st_ref[s, d, p] = x
            out_ref = yf_ref if d == 0 else yb_ref
            out_ref[gc * CHUNK:(gc + 1) * CHUNK, p * LANE:(p + 1) * LANE] = y

    def group_scan(group):
        s, fwd, bwd = group
        return [functools.partial(scan_step, s, (cf, cb)) for cf, cb in zip(fwd, bwd)]

    conv_done, seq_loaded = set(), set()

    def group_conv(group):
        s, fwd, bwd = group
        todo = [c for c in sorted(set(fwd) | set(bwd)) if (s, c) not in conv_done]
        conv_done.update((s, c) for c in todo)
        thunks = []
        for i in range(0, len(todo), max(1, -(-len(todo) // TERM_UNROLL))):
            part = todo[i:i + max(1, -(-len(todo) // TERM_UNROLL))]
            need_load = s not in seq_loaded
            seq_loaded.add(s)

            def run(part=part, need_load=need_load):
                if need_load:
                    assert not is_grid or nseq == 1
                    load_sequence(s)
                for c in part:
                    conv_store(s, c)

            thunks.append(run)
        return thunks

    def merge(*lists):
        n = max(len(lst) for lst in lists)
        pick = lambda lst, i: lst[i] if i < len(lst) else (lambda: None)
        return [lambda i=i: [pick(lst, i)() for lst in lists] for i in range(n)]

    def post(offs):
        ys = [yf_ref[pl.ds(off, CHUNK), :] + yb_ref[pl.ds(off, CHUNK), :] for off in offs]
        mus = [block_sum(y) * (1.0 / 64) for y in ys]
        dlts = [y - mu for y, mu in zip(ys, mus)]
        vrs = [block_sum(dlt * dlt) * (1.0 / 64) for dlt in dlts]
        for off, dlt, var in zip(offs, dlts, vrs):
            yn = dlt * lax.rsqrt(var + RW_LN_EPS) * ln_w + ln_b
            y_ref[pl.ds(off, CHUNK), :] = (yn + bonus_ref[pl.ds(off, CHUNK), :]) * gate_ref[pl.ds(off, CHUNK), :]

    def sequence_post(s):
        return [functools.partial(post, [(s * n_chunks + c) * CHUNK]) for c in range(n_chunks)]

    if is_grid:
        load_sequence(0)
        seq_loaded.add(0)
        lax.fori_loop(0, n_chunks, shift_body, 0)
    for thunk in group_conv(groups[0]):
        thunk()
    posted = set()
    for g, group in enumerate(groups):
        scans = group_scan(groups[g - 1]) if g else []
        convs = group_conv(groups[g + 1]) if g + 1 < len(groups) else []
        posts = []
        if nseq > 1 and g >= 2:
            s_done = groups[g - 2][0]
            posts = sequence_post(s_done)
            posted.update(range(s_done * n_chunks, (s_done + 1) * n_chunks))
        group_terms(group, merge(scans, convs, posts))
    early = [gc for gc in _finished_before_last_scan(groups, n_chunks) if gc not in posted]
    last_scan = group_scan(groups[-1])
    per_step = max(1, -(-len(early) // len(last_scan)))
    early_posts = [functools.partial(post, [gc * CHUNK for gc in early[i:i + per_step]])
                   for i in range(0, len(early), per_step)]
    for step, extra in itertools.zip_longest(last_scan, early_posts, fillvalue=lambda: None):
        step()
        extra()
    posted.update(early)
    for s in range(nseq):
        for d, p in chains:
            x = st_ref[s, d, p]
            sout_ref[s, d, 2 * p] = x[:, 0:64]
            sout_ref[s, d, 2 * p + 1] = x[:, 64:LANE]

    left = [gc for gc in range(nseq * n_chunks) if gc not in posted]
    for i in range(0, len(left), TERM_UNROLL):
        post([gc * CHUNK for gc in left[i:i + TERM_UNROLL]])


def _rwkv_mixer(proj, first_seq, n_seq, seq_len, is_grid, prm, s0):
    n_pairs = RW_W // LANE
    n_chunks = seq_len // CHUNK
    npp = RW_PAIRS_PER_STEP
    w = npp * LANE
    nseq = max(1, RW_GROUPS * TERM_UNROLL // n_chunks)
    assert n_chunks % TERM_UNROLL == 0 and n_pairs % npp == 0
    assert (n_chunks == TERM_UNROLL or nseq == 1) and n_seq % nseq == 0 and first_seq % nseq == 0
    rows = nseq * seq_len
    total_chunks = nseq * n_chunks
    first = first_seq // nseq
    col = lambda cb: (lambda b, p: (b + first, cb // npp + p))
    par = lambda cb: (lambda b, p: (0, cb // npp + p))
    state_spec = pl.BlockSpec((nseq, 2, 2 * npp, 64, 64), lambda b, p: (b, 0, p, 0, 0))
    kernel = functools.partial(_rwkv_kernel, seq_len, nseq, is_grid, s0 is None)
    y, s_out = pl.pallas_call(
        kernel,
        grid=(n_seq // nseq, n_pairs // npp),
        in_specs=[pl.BlockSpec((rows, w), col(CB_R)),
                  pl.BlockSpec((rows, w), col(CB_K)),
                  pl.BlockSpec((rows, w), col(CB_V)),
                  pl.BlockSpec((rows, 3 * LANE), lambda b, p: (b + first, CB_LORA // 3)),
                  pl.BlockSpec((9, w), par(CB_R)),
                  pl.BlockSpec((9, w), par(CB_K)),
                  pl.BlockSpec((9, w), par(CB_V)),
                  pl.BlockSpec((2, LANE, w), lambda b, p: (0, 0, p)),
                  pl.BlockSpec((2, LANE, w), lambda b, p: (0, 0, p)),
                  pl.BlockSpec((LANE, w), lambda b, p: (0, p)),
                  pl.BlockSpec((16, w), lambda b, p: (0, p))] + ([] if s0 is None else [state_spec]),
        out_specs=[pl.BlockSpec((rows, w), lambda b, p: (b, p)), state_spec],
        out_shape=[jax.ShapeDtypeStruct((n_seq * seq_len, RW_W), F32),
                   jax.ShapeDtypeStruct((n_seq, 2, 2 * n_pairs, 64, 64), F32)],
        scratch_shapes=[pltpu.VMEM((seq_len + 2 * CONV_PAD, 3 * w), F32)]
                       + [pltpu.VMEM((rows, w), F32)] * 8
                       + [pltpu.VMEM((nseq, 2, npp, 64, LANE), F32),
                          pltpu.VMEM((2, npp, total_chunks, LANE, LANE), BF16),
                          pltpu.VMEM((2, npp, total_chunks, 64, LANE), F32),
                          pltpu.VMEM((2, npp, total_chunks, 8, LANE), F32),
                          pltpu.VMEM((2, npp, total_chunks, CHUNK, LANE), BF16),
                          pltpu.VMEM((2, npp, total_chunks, CHUNK, LANE), F32)]
                       + ([pltpu.VMEM((seq_len + 2 * CONV_PAD, 3 * w), F32)] * 2 if is_grid else []),
        compiler_params=_cparams(2),
        name="rwkv_mixer",
    )(proj, proj, proj, proj, prm['conv'], prm['conv'], prm['conv'], prm['w2p'], prm['a2p'], prm['g2'],
      prm['vec'], *([] if s0 is None else [s0]))
    return y, s_out


def _gla_kernel(seq_len, nseq, zero_init, q_ref, k_ref, v_ref, og_ref, lgk_ref, gk2_ref, gvec_ref, *rest):
    s0_ref = None if zero_init else rest[0]
    y_ref, sout_ref, of_ref, ob_ref, st_ref, tq_ref, to_ref, ta_ref, tkv_ref = rest[0 if zero_init else 1:]
    n_chunks = seq_len // CHUNK
    npp = GLA_PAIRS
    gvec = gvec_ref[...]
    chains = [(d, p) for d in range(2) for p in range(npp)]

    for s in range(nseq):
        for d, p in chains:
            if zero_init:
                st_ref[s, d, p] = jnp.zeros((LANE, 2 * LANE), F32)
            else:
                z = jnp.zeros((64, LANE), F32)
                st_ref[s, d, p] = jnp.concatenate([jnp.concatenate([s0_ref[s, d, 2 * p], z], axis=1),
                                                   jnp.concatenate([z, s0_ref[s, d, 2 * p + 1]], axis=1)], axis=0)

    if nseq == 1:
        groups = [(0, list(range(g * TERM_UNROLL, (g + 1) * TERM_UNROLL)),
                   list(range(n_chunks - 1 - g * TERM_UNROLL, n_chunks - 1 - (g + 1) * TERM_UNROLL, -1)))
                  for g in range(n_chunks // TERM_UNROLL)]
    else:
        groups = [(s, list(range(n_chunks)), list(range(n_chunks - 1, -1, -1))) for s in range(nseq)]

    def group_terms(group, interleaved):
        s, fwd, bwd = group
        insts, where, lgk_in = [], [], {}
        for d, chunks in ((0, fwd), (1, bwd)):
            for c in chunks:
                gc = s * n_chunks + c
                rows = slice(gc * CHUNK, (gc + 1) * CHUNK)
                if c not in lgk_in:
                    lgk_in[c] = lgk_ref[rows, :].astype(BF16)
                x = _dot(lgk_in[c], gk2_ref[d]) + gvec[d:d + 1, 0:GLA_QK_W]
                g = _log_sigmoid(x) * (1.0 / GLA_GATE_NORM)
                qc = q_ref[rows, :] * GLA_Q_SCALE
                kc = k_ref[rows, :]
                vc = v_ref[rows, :]
                for p in range(npp):
                    qk = slice(p * LANE, (p + 1) * LANE)
                    insts.append((qc[:, qk], kc[:, qk], vc[:, 2 * p * LANE:2 * (p + 1) * LANE], g[:, qk], d == 1))
                    where.append((d, p, gc))
        for (d, p, gc), (qt, o0, a_col, kv) in zip(where, _gla_chunk_terms(insts, interleaved)):
            tq_ref[d, p, gc] = qt
            to_ref[d, p, gc] = o0
            ta_ref[d, p, gc] = a_col
            tkv_ref[d, p, gc] = kv

    def scan_step(s, chunk_of_dir):
        gcs = [s * n_chunks + chunk_of_dir[d] for d, _ in chains]
        ss = [st_ref[s, d, p] for d, p in chains]
        os_ = [_dot(tq_ref[d, p, gc], x.astype(BF16)) + to_ref[d, p, gc] for (d, p), gc, x in zip(chains, gcs, ss)]
        for (d, p), gc, x, o in zip(chains, gcs, ss, os_):
            a_col = ta_ref[d, p, gc]
            st_ref[s, d, p] = x * jnp.concatenate([a_col, a_col], axis=1) + tkv_ref[d, p, gc]
            out_ref = of_ref if d == 0 else ob_ref
            out_ref[gc * CHUNK:(gc + 1) * CHUNK, 2 * p * LANE:2 * (p + 1) * LANE] = o

    def group_scan(group):
        s, fwd, bwd = group
        return [functools.partial(scan_step, s, (cf, cb)) for cf, cb in zip(fwd, bwd)]

    def post(off):
        for h in range(2 * npp):
            hs = slice(h * LANE, (h + 1) * LANE)
            o = of_ref[pl.ds(off, CHUNK), hs] + ob_ref[pl.ds(off, CHUNK), hs]
            gate = _silu(og_ref[pl.ds(off, CHUNK), hs])
            y_ref[pl.ds(off, CHUNK), hs] = _rmsnorm_rows(o) * gvec[2:3, hs] * gate

    posted = set()
    for g, group in enumerate(groups):
        thunks = group_scan(groups[g - 1]) if g else []
        if nseq > 1 and g >= 2:
            s_done = groups[g - 2][0]
            posts = [functools.partial(post, (s_done * n_chunks + c) * CHUNK) for c in range(n_chunks)]
            thunks = [lambda a=a, b=b: (a(), b()) for a, b in zip(thunks, posts)]
            posted.update(range(s_done * n_chunks, (s_done + 1) * n_chunks))
        group_terms(group, thunks)
    early = [gc for gc in _finished_before_last_scan(groups, n_chunks) if gc not in posted]
    last_scan = group_scan(groups[-1])
    per_step = max(1, -(-len(early) // len(last_scan)))
    for i, step in enumerate(last_scan):
        step()
        for gc in early[i * per_step:(i + 1) * per_step]:
            post(gc * CHUNK)
    for gc in early[len(last_scan) * per_step:]:
        post(gc * CHUNK)
    posted.update(early)
    for s in range(nseq):
        for d, p in chains:
            x = st_ref[s, d, p]
            sout_ref[s, d, 2 * p] = x[0:64, 0:LANE]
            sout_ref[s, d, 2 * p + 1] = x[64:LANE, LANE:2 * LANE]
    for gc in range(nseq * n_chunks):
        if gc not in posted:
            post(gc * CHUNK)


def _gla_mixer(proj, first_seq, n_seq, seq_len, prm, s0):
    npp = GLA_PAIRS
    n_heads = 2 * npp
    n_chunks = seq_len // CHUNK
    nseq = max(1, RW_GROUPS * TERM_UNROLL // n_chunks)
    assert n_chunks % TERM_UNROLL == 0
    assert (n_chunks == TERM_UNROLL or nseq == 1) and n_seq % nseq == 0 and first_seq % nseq == 0
    rows = nseq * seq_len
    total_chunks = nseq * n_chunks
    first = first_seq // nseq
    state_spec = pl.BlockSpec((nseq, 2, n_heads, 64, LANE), lambda b: (b, 0, 0, 0, 0))
    kernel = functools.partial(_gla_kernel, seq_len, nseq, s0 is None)
    y, s_out = pl.pallas_call(
        kernel,
        grid=(n_seq // nseq,),
        in_specs=[pl.BlockSpec((rows, GLA_QK_W), lambda b: (b + first, CB_GQ * LANE // GLA_QK_W)),
                  pl.BlockSpec((rows, GLA_QK_W), lambda b: (b + first, CB_GK * LANE // GLA_QK_W)),
                  pl.BlockSpec((rows, GLA_V_W), lambda b: (b + first, CB_GV * LANE // GLA_V_W)),
                  pl.BlockSpec((rows, GLA_V_W), lambda b: (b + first, CB_OG * LANE // GLA_V_W)),
                  pl.BlockSpec((rows, LANE), lambda b: (b + first, CB_LGK)),
                  pl.BlockSpec((2, LANE, GLA_QK_W), lambda b: (0, 0, 0)),
                  pl.BlockSpec((8, GLA_V_W), lambda b: (0, 0))] + ([] if s0 is None else [state_spec]),
        out_specs=[pl.BlockSpec((rows, GLA_V_W), lambda b: (b, 0)), state_spec],
        out_shape=[jax.ShapeDtypeStruct((n_seq * seq_len, GLA_V_W), F32),
                   jax.ShapeDtypeStruct((n_seq, 2, n_heads, 64, LANE), F32)],
        scratch_shapes=[pltpu.VMEM((rows, GLA_V_W), F32)] * 2
                       + [pltpu.VMEM((nseq, 2, npp, LANE, 2 * LANE), F32),
                          pltpu.VMEM((2, npp, total_chunks, CHUNK, LANE), BF16),
                          pltpu.VMEM((2, npp, total_chunks, CHUNK, 2 * LANE), F32),
                          pltpu.VMEM((2, npp, total_chunks, LANE, LANE), F32),
                          pltpu.VMEM((2, npp, total_chunks, LANE, 2 * LANE), F32)],
        compiler_params=_cparams(1),
        name="gla_mixer",
    )(proj, proj, proj, proj, proj, prm['gk2p'], prm['gvec'], *([] if s0 is None else [s0]))
    return y, s_out


OUT_TM = 512
OUT_PARTS = 4
ROUTE_NEG = -1e30
LANE_GROUP0 = N_EXPERTS


def _route(logits):
    lane = _iota(logits.shape, 1)
    lane_f = lane.astype(F32)
    big = float(LANE)
    is_g = (lane >= LANE_GROUP0) & (lane < LANE_GROUP0 + 4)
    gmax = jnp.max(jnp.where(is_g, logits, ROUTE_NEG), axis=-1, keepdims=True)
    gidx = jnp.min(jnp.where(is_g & (logits == gmax), lane_f, big), axis=-1, keepdims=True) - LANE_GROUP0
    gsum = jnp.sum(jnp.where(is_g, jnp.exp(jnp.minimum(logits - gmax, 0.0)), 0.0), axis=-1, keepdims=True)
    g_w = 1.0 / gsum
    in_grp = (lane < N_EXPERTS) & ((lane // 4).astype(F32) == gidx)
    m1 = jnp.max(jnp.where(in_grp, logits, ROUTE_NEG), axis=-1, keepdims=True)
    i1 = jnp.min(jnp.where(in_grp & (logits == m1), lane_f, big), axis=-1, keepdims=True)
    rest = in_grp & (lane_f != i1)
    m2 = jnp.max(jnp.where(rest, logits, ROUTE_NEG), axis=-1, keepdims=True)
    i2 = jnp.min(jnp.where(rest & (logits == m2), lane_f, big), axis=-1, keepdims=True)
    t = jnp.exp(m2 - m1)
    w1 = g_w / (1.0 + t)
    return jnp.where(lane_f == i1, w1, 0.0) + jnp.where(lane_f == i2, w1 * t, 0.0)


def _outproj_kernel(tiles, yrc_ref, yrd_ref, ygc_ref, ygd_ref, xc_ref, xd_ref, mod_ref, wo_ref, g_ref, wr_ref,
                    br_ref, x1_ref, h2_ref, cmb_ref):
    def run(yr_ref, yg_ref, x_ref):
        m = mod_ref[0]
        rows = OUT_TM // OUT_PARTS
        parts = [slice(i * rows, (i + 1) * rows) for i in range(OUT_PARTS)]
        mixes = [_mm(yr_ref[p, :], wo_ref[0:RW_W, :]) + _mm(yg_ref[p, :], wo_ref[RW_W:RW_W + GLA_V_W, :])
                 for p in parts]
        x1s = [x_ref[p, :] + m[2:3] * mix for p, mix in zip(parts, mixes)]
        h2s = [_rmsnorm_rows(x1) * g_ref[...] * (1.0 + m[4:5]) + m[3:4] for x1 in x1s]
        splits = [_split2(h2) for h2 in h2s]
        boths = [_dot(h_hi, wr_ref[...]) for h_hi, _ in splits]
        lows = [_dot(h_lo, wr_ref[:, 0:LANE]) for _, h_lo in splits]
        for p, x1, h2, both, low in zip(parts, x1s, h2s, boths, lows):
            x1_ref[p, :] = x1
            h2_ref[p, :] = h2.astype(BF16)
            cmb_ref[p, :] = _route(both[:, 0:LANE] + both[:, LANE:2 * LANE] + low + br_ref[...])

    tiles.by_pass(functools.partial(run, yrc_ref, ygc_ref, xc_ref), functools.partial(run, yrd_ref, ygd_ref, xd_ref))


def _out_projection(tiles, y_rw, y_gla, x, mod, w_out, norm_g, w_route, b_route):
    n = x[0].shape[0] + x[1].shape[0]
    full = lambda a: pl.BlockSpec(a.shape, lambda i: (0,) * a.ndim)
    return pl.pallas_call(
        functools.partial(_outproj_kernel, tiles),
        grid=(tiles.n_ctx + tiles.n_dec,),
        in_specs=[*tiles.specs(RW_W), *tiles.specs(GLA_V_W), *tiles.specs(D_MODEL), tiles.mod_spec(),
                  full(w_out), full(norm_g), full(w_route), full(b_route)],
        out_specs=[tiles.merged(D_MODEL), tiles.merged(D_MODEL), tiles.merged(LANE)],
        out_shape=[jax.ShapeDtypeStruct((n, D_MODEL), F32), jax.ShapeDtypeStruct((n, D_MODEL), BF16),
                   jax.ShapeDtypeStruct((n, LANE), F32)],
        compiler_params=_cparams(1),
        name="out_proj_router",
    )(*y_rw, *y_gla, *x, mod, w_out, norm_g, w_route, b_route)


MOE_TM = 512
MOE_RB = 128
MOE_INTERLEAVE = 4
SLOT_ALIGN = 16
MOE_SLOTS = 2 * MOE_TM + N_EXPERTS * SLOT_ALIGN


def _stage_expert_weights(srcs_hbm, dst_refs, stage_refs, sems):
    def copies(e):
        return [pltpu.make_async_copy(src.at[e], stage.at[e % 2], sem.at[e % 2])
                for src, stage, sem in zip(srcs_hbm, stage_refs, sems)]

    for c in copies(0):
        c.start()
    for e in range(N_EXPERTS):
        if e + 1 < N_EXPERTS:
            for c in copies(e + 1):
                c.start()
        for c, dst, stage in zip(copies(e), dst_refs, stage_refs):
            c.wait()
            dst[e] = stage[e % 2].astype(BF16)


def _moe_kernel(tiles, h2_ref, cmb_ref, x1_ref, mod_ref, w1_hbm, w3_hbm, w2_hbm, fg_ref, yc_ref, yd_ref,
                xs_ref, ys_ref, w1_ref, w3_ref, w2_ref, stage1_ref, stage3_ref, stage2_ref, sem1, sem3, sem2):
    @pl.when(pl.program_id(0) == 0)
    def _():
        _stage_expert_weights((w1_hbm, w3_hbm, w2_hbm), (w1_ref, w3_ref, w2_ref),
                              (stage1_ref, stage3_ref, stage2_ref), (sem1, sem3, sem2))

    cmb = cmb_ref[...]
    lane = _iota(cmb.shape, 1).astype(F32)
    sel = cmb > 0.0
    sel01 = jnp.where(sel, 1.0, 0.0).astype(BF16)
    before = (_iota((MOE_TM, MOE_TM), 0) > _iota((MOE_TM, MOE_TM), 1)).astype(BF16)
    pos = _dot(before, sel01)
    cnt = pos[MOE_TM - 1:MOE_TM] + sel01[MOE_TM - 1:MOE_TM].astype(F32)
    seg = jnp.floor((cnt + (SLOT_ALIGN - 1)) * (1.0 / SLOT_ALIGN))
    lower_experts = (_iota((LANE, LANE), 0) < _iota((LANE, LANE), 1)).astype(BF16)
    start = _dot(jnp.broadcast_to(seg, (8, LANE)).astype(BF16), lower_experts)[0:1] * SLOT_ALIGN
    n_blk = jnp.floor((cnt + (MOE_RB - 1)) * (1.0 / MOE_RB)).astype(jnp.int32)
    start_i = start.astype(jnp.int32)
    cnt_i = cnt.astype(jnp.int32)
    slot = start + pos
    e_a = jnp.min(jnp.where(sel, lane, float(LANE)), axis=-1, keepdims=True)
    e_b = jnp.max(jnp.where(sel, lane, -1.0), axis=-1, keepdims=True)
    pick = lambda e, x: jnp.sum(jnp.where(lane == e, x, 0.0), axis=-1, keepdims=True)
    slot_a, w_a = pick(e_a, slot), pick(e_a, cmb)
    slot_b = jnp.where(e_b != e_a, pick(e_b, slot), -1.0)
    w_b = pick(e_b, cmb)

    slots_t = jnp.where(lane == 0.0, slot_a, jnp.where(lane == 1.0, slot_b, -1.0)).T
    row_slot = _iota((MOE_SLOTS, MOE_TM), 0).astype(F32)
    gather = jnp.where((row_slot == slots_t[0:1]) | (row_slot == slots_t[1:2]), 1.0, 0.0).astype(BF16)
    xs_ref[...] = _dot(gather, h2_ref[...]).astype(BF16)
    ys_ref[...] = jnp.zeros_like(ys_ref)

    row_in_blk = _iota((MOE_RB, D_MODEL), 0)

    def expert_blocks(experts, r0s, ends):
        base = [pl.multiple_of(jnp.minimum(r0, MOE_SLOTS - MOE_RB), SLOT_ALIGN) for r0 in r0s]
        xbs = [xs_ref[pl.ds(b, MOE_RB), :] for b in base]
        gates = [_dot(xb, w3_ref[e]) for xb, e in zip(xbs, experts)]
        ups = [_dot(xb, w1_ref[e]) for xb, e in zip(xbs, experts)]
        acts = [(_silu(g) * u).astype(BF16) for g, u in zip(gates, ups)]
        outs = [_dot(a, w2_ref[e]) for a, e in zip(acts, experts)]
        for b, r0, end, out in zip(base, r0s, ends, outs):
            row = row_in_blk + b
            keep = (row >= end) | (row < r0)
            ys_ref[pl.ds(b, MOE_RB), :] = jnp.where(keep, ys_ref[pl.ds(b, MOE_RB), :], out.astype(BF16))

    seg_start = [pl.multiple_of(start_i[0, e], SLOT_ALIGN) for e in range(N_EXPERTS)]
    seg_end = [seg_start[e] + cnt_i[0, e] for e in range(N_EXPERTS)]
    for e0 in range(0, N_EXPERTS, MOE_INTERLEAVE):
        es = list(range(e0, e0 + MOE_INTERLEAVE))
        expert_blocks(es, [seg_start[e] for e in es], [seg_end[e] for e in es])
    for e in range(N_EXPERTS):
        def extra_block(b, carry, e=e):
            expert_blocks([e], [pl.multiple_of(seg_start[e] + b * MOE_RB, SLOT_ALIGN)], [seg_end[e]])
            return carry

        lax.fori_loop(1, n_blk[0, e], extra_block, 0)

    col_slot = _iota((MOE_TM, MOE_SLOTS), 1).astype(F32)
    scatter = (jnp.where(col_slot == slot_a, w_a, 0.0) + jnp.where(col_slot == slot_b, w_b, 0.0)).astype(BF16)
    x2 = x1_ref[...] + mod_ref[0][5:6] * _dot(scatter, ys_ref[...])
    y = _rmsnorm_rows(x2) * fg_ref[...]

    def write(y_ref):
        y_ref[...] = y

    tiles.by_pass(functools.partial(write, yc_ref), functools.partial(write, yd_ref))


def _moe(tiles, h2, cmb, x1, mod, w1, w3, w2, final_g):
    assert tiles.tm == MOE_TM
    hbm = pl.BlockSpec(memory_space=pl.ANY)
    out_ctx, out_dec = tiles.specs(D_MODEL)
    return pl.pallas_call(
        functools.partial(_moe_kernel, tiles),
        grid=(tiles.n_ctx + tiles.n_dec,),
        in_specs=[tiles.merged(D_MODEL), tiles.merged(LANE), tiles.merged(D_MODEL), tiles.mod_spec(),
                  hbm, hbm, hbm, pl.BlockSpec((1, D_MODEL), lambda i: (0, 0))],
        out_specs=[out_ctx, out_dec],
        out_shape=[jax.ShapeDtypeStruct((tiles.n_ctx * MOE_TM, D_MODEL), F32),
                   jax.ShapeDtypeStruct((tiles.n_dec * MOE_TM, D_MODEL), F32)],
        scratch_shapes=[pltpu.VMEM((MOE_SLOTS, D_MODEL), BF16), pltpu.VMEM((MOE_SLOTS, D_MODEL), BF16),
                        pltpu.VMEM((N_EXPERTS, D_MODEL, D_EXPERT), BF16),
                        pltpu.VMEM((N_EXPERTS, D_MODEL, D_EXPERT), BF16),
                        pltpu.VMEM((N_EXPERTS, D_EXPERT, D_MODEL), BF16),
                        pltpu.VMEM((2, D_MODEL, D_EXPERT), F32), pltpu.VMEM((2, D_MODEL, D_EXPERT), F32),
                        pltpu.VMEM((2, D_EXPERT, D_MODEL), F32)] + [pltpu.SemaphoreType.DMA((2,))] * 3,
        compiler_params=_cparams(1),
        name="moe_experts",
    )(h2, cmb, x1, mod, w1, w3, w2, final_g)


def _pad_rows(x, rows):
    return jnp.pad(x, ((0, rows - x.shape[0]),) + ((0, 0),) * (x.ndim - 1))


def _pack_params(l, w_in, rw_conv, rw_w0, rw_w2, rw_a0, rw_a2, rw_g2, rw_k_k, rw_k_a, rw_r_k, rw_ln_w, rw_ln_b,
                 gla_gk2, gla_gk_b, gla_norm_g, moe_w_group, moe_b_group, moe_w_expert, moe_b_expert):
    wi = w_in[l]
    z = lambda n: jnp.zeros((D_MODEL, n), F32)
    w_in_t = jnp.swapaxes(wi, 0, 1)
    z64 = jnp.zeros((64, RW_W), F32)
    w2p = jnp.stack([jnp.concatenate([rw_w2[l, 0], z64], 0), jnp.concatenate([z64, rw_w2[l, 1]], 0)])
    a2p = jnp.stack([jnp.concatenate([rw_a2[l, 0], z64], 0), jnp.concatenate([z64, rw_a2[l, 1]], 0)])
    vec = _pad_rows(jnp.stack([rw_w0[l, 0], rw_w0[l, 1], rw_a0[l, 0], rw_a0[l, 1], rw_k_k[l], rw_k_a[l],
                               rw_r_k[l].reshape(RW_W), rw_ln_w[l], rw_ln_b[l]]), 16)
    rw = {'conv': rw_conv[l].reshape(9, 3 * RW_W), 'w2p': w2p.astype(BF16), 'a2p': a2p.astype(BF16),
          'g2': rw_g2[l].astype(BF16), 'vec': vec}
    gk2p = jnp.stack([_pad_rows(gla_gk2[l, 0], LANE),
                      _pad_rows(jnp.concatenate([jnp.zeros((16, GLA_QK_W), F32), gla_gk2[l, 1]], 0), LANE)])
    gk_b = jnp.pad(gla_gk_b[l], ((0, 0), (0, GLA_V_W - GLA_QK_W)))
    gvec = _pad_rows(jnp.concatenate([gk_b, jnp.tile(gla_norm_g[l], GLA_V_W // LANE)[None]], axis=0), 8)
    gla = {'gk2p': gk2p.astype(BF16), 'gvec': gvec}
    w_route = jnp.concatenate(_split2(jnp.concatenate([moe_w_expert[l], moe_w_group[l], z(LANE - N_EXPERTS - 4)],
                                                      axis=1)), axis=1)
    b_route = jnp.concatenate([moe_b_expert[l], moe_b_group[l], jnp.zeros((LANE - N_EXPERTS - 4,), F32)])[None]
    return w_in_t, rw, gla, w_route, b_route


def kernel(x_prompt, x_sample, state_rwkv, state_gla, c, c_ctx, norm1_g, norm2_g, w_ada, b_ada, w_in, w_out,
           rw_conv, rw_w0, rw_w2, rw_a0, rw_a2, rw_g2, rw_k_k, rw_k_a, rw_r_k, rw_ln_w, rw_ln_b,
           gla_gk2, gla_gk_b, gla_norm_g, moe_w_group, moe_b_group, moe_w_expert, moe_b_expert,
           moe_w1, moe_w3, moe_w2, final_g):
    depth = w_in.shape[0]
    assert depth == 1, "the packed layout below handles the single-layer trunk of this problem"
    l = 0
    n_dec = x_sample.shape[0]
    ctx_row = n_dec
    cond8 = _pad_rows(jnp.concatenate([c, c_ctx[None]], axis=0), 8)
    mod = _modulation(cond8, w_ada[l], b_ada[l][None]).reshape(8, N_MOD, D_MODEL)
    pk = _pack_params(l, w_in, rw_conv, rw_w0, rw_w2, rw_a0, rw_a2, rw_g2, rw_k_k, rw_k_a, rw_r_k, rw_ln_w,
                      rw_ln_b, gla_gk2, gla_gk_b, gla_norm_g, moe_w_group, moe_b_group, moe_w_expert,
                      moe_b_expert)
    w_in_t, rw, gla, w_route, b_route = pk

    n_ctx, ctx_len, _ = x_prompt.shape
    dec_len = x_sample.shape[1]
    x_ctx = x_prompt.reshape(n_ctx * ctx_len, D_MODEL)
    x_dec = x_sample.reshape(n_dec * dec_len, D_MODEL)
    assert (n_ctx * ctx_len) % dec_len == 0, "denoising sequences must start on a dec_len row block of proj"
    first_dec = n_ctx * ctx_len // dec_len
    tiles = _Tiles(n_ctx * ctx_len, n_dec * dec_len, dec_len, ctx_row, PROJ_TM)
    assert PROJ_TM == OUT_TM == MOE_TM

    proj = _in_projection(tiles, x_ctx, x_dec, mod, norm1_g[l][None], w_in_t)
    y_rw_c, s_rw = _rwkv_mixer(proj, 0, n_ctx, ctx_len, False, rw, None)
    y_gla_c, s_gla = _gla_mixer(proj, 0, n_ctx, ctx_len, gla, None)
    y_rw_d, _ = _rwkv_mixer(proj, first_dec, n_dec, dec_len, True, rw, state_rwkv[:, l])
    y_gla_d, _ = _gla_mixer(proj, first_dec, n_dec, dec_len, gla, state_gla[:, l])
    x1, h2, cmb = _out_projection(tiles, (y_rw_c, y_rw_d), (y_gla_c, y_gla_d), (x_ctx, x_dec), mod,
                                  w_out[l].astype(BF16), norm2_g[l][None], w_route, b_route)
    y_ctx, y_dec = _moe(tiles, h2, cmb, x1, mod, moe_w1[l], moe_w3[l], moe_w2[l], final_g[None])
    return (y_ctx.reshape(x_prompt.shape), y_dec.reshape(x_sample.shape), s_rw[:, None], s_gla[:, None])
```

```python
import functools
import itertools

import jax
import jax.numpy as jnp
from jax import lax
from jax.experimental import pallas as pl
from jax.experimental.pallas import tpu as pltpu

F32 = jnp.float32
BF16 = jnp.bfloat16

D_MODEL = 1024
RW_W = 512
GLA_V_W = 512
GLA_QK_W = 256
N_EXPERTS = 16
D_EXPERT = 256
N_MOD = 6
EPS = 1e-6
RW_LN_EPS = 64e-5
RW_DECAY_SCALE = 0.606531
GLA_GATE_NORM = 16.0
GLA_Q_SCALE = 64 ** -0.5
GRID_W = 64

LANE = 128
CHUNK = 64
CONV_PAD = 128
TERM_UNROLL = 4
RW_PAIRS_PER_STEP = 2
GLA_PAIRS = 2
RW_GROUPS = 4
D_PROJ = 28 * LANE
VMEM_LIMIT = 56 * 1024 * 1024

CB_R, CB_K, CB_V, CB_LORA, CB_LGK, CB_GQ, CB_GK, CB_GV, CB_OG = 0, 4, 8, 12, 15, 16, 18, 20, 24

_NN = (((1,), (0,)), ((), ()))
_NT = (((1,), (1,)), ((), ()))
_TN = (((0,), (0,)), ((), ()))


def _dot(a, b, dims=_NN):
    return lax.dot_general(a, b, dims, preferred_element_type=F32)


def _mm(a, b, dims=_NN):
    return _dot(a.astype(BF16), b.astype(BF16), dims)


def _split2(x):
    hi = x.astype(BF16)
    lo = (x - hi.astype(F32)).astype(BF16)
    return hi, lo


def _mm3(a, b, dims=_NN):
    ah, al = _split2(a)
    bh, bl = _split2(b)
    return _dot(ah, bh, dims) + _dot(ah, bl, dims) + _dot(al, bh, dims)


def _mm_01_lhs(a01, b, dims=_NN):
    n = b.shape[1]
    both = _dot(a01, jnp.concatenate(_split2(b), axis=1), dims)
    return both[:, 0:n] + both[:, n:2 * n]


def _sigmoid(x):
    return 0.5 * jnp.tanh(0.5 * x) + 0.5


def _silu(x):
    return x * _sigmoid(x)


def _log_sigmoid(x):
    return jnp.minimum(x, 0.0) - jnp.log(1.0 + jnp.exp(-jnp.abs(x)))


def _iota(shape, dim):
    return lax.broadcasted_iota(jnp.int32, shape, dim)


def _cparams(n_axes):
    return pltpu.CompilerParams(dimension_semantics=("arbitrary",) * n_axes, vmem_limit_bytes=VMEM_LIMIT)


MOD_TN = 768


def _mod_kernel(c_ref, w_ref, b_ref, o_ref):
    o_ref[...] = _mm3(_silu(c_ref[...]), w_ref[...]) + b_ref[...]


def _modulation(cond8, w_ada, b_ada):
    n = w_ada.shape[1]
    return pl.pallas_call(
        _mod_kernel,
        grid=(n // MOD_TN,),
        in_specs=[pl.BlockSpec((8, D_MODEL), lambda j: (0, 0)),
                  pl.BlockSpec((D_MODEL, MOD_TN), lambda j: (0, j)),
                  pl.BlockSpec((1, MOD_TN), lambda j: (0, j))],
        out_specs=pl.BlockSpec((8, MOD_TN), lambda j: (0, j)),
        out_shape=jax.ShapeDtypeStruct((8, n), F32),
        compiler_params=_cparams(1),
        name="adaln_mod",
    )(cond8, w_ada, b_ada)


PROJ_TM = 512
D_IN = 3488
N_LGK = 32


def _rmsnorm_rows(x):
    return x * lax.rsqrt(jnp.mean(x * x, axis=-1, keepdims=True) + EPS)


class _Tiles:
    def __init__(self, n_ctx_tokens, n_dec_tokens, dec_seq_len, ctx_row, tm):
        self.tm = tm
        self.n_ctx = n_ctx_tokens // tm
        self.n_dec = n_dec_tokens // tm
        self.per_seq = dec_seq_len // tm
        self.ctx_row = ctx_row

    def specs(self, width):
        last_ctx = self.n_ctx - 1
        n_ctx = self.n_ctx
        return (pl.BlockSpec((self.tm, width), lambda i: (jnp.minimum(i, last_ctx), 0)),
                pl.BlockSpec((self.tm, width), lambda i: (jnp.maximum(i - n_ctx, 0), 0)))

    def merged(self, width):
        return pl.BlockSpec((self.tm, width), lambda i: (i, 0))

    def mod_spec(self):
        n_ctx, per_seq, ctx_row = self.n_ctx, self.per_seq, self.ctx_row
        return pl.BlockSpec((1, N_MOD, D_MODEL),
                            lambda i: (jnp.where(i < n_ctx, ctx_row, (i - n_ctx) // per_seq), 0, 0))

    def by_pass(self, run_ctx, run_dec):
        i = pl.program_id(0)
        pl.when(i < self.n_ctx)(run_ctx)
        pl.when(i >= self.n_ctx)(run_dec)


def _inproj_kernel(tiles, xc_ref, xd_ref, mod_ref, g_ref, wt_ref, o_ref, w_ref):
    @pl.when(pl.program_id(0) == 0)
    def _():
        for j in range(D_PROJ // LANE):
            if j == CB_LGK:
                blk = jnp.concatenate([wt_ref[D_IN - N_LGK:D_IN, :], jnp.zeros((LANE - N_LGK, D_MODEL), F32)], axis=0)
            else:
                src = j if j < CB_LGK else j - 1
                blk = wt_ref[src * LANE:(src + 1) * LANE, :]
            w_ref[:, j * LANE:(j + 1) * LANE] = blk.T.astype(BF16)

    def run(x_ref):
        m = mod_ref[0]
        h = _rmsnorm_rows(x_ref[...]) * g_ref[...] * (1.0 + m[1:2]) + m[0:1]
        o_ref[...] = _mm(h, w_ref[...])

    tiles.by_pass(functools.partial(run, xc_ref), functools.partial(run, xd_ref))


def _in_projection(tiles, x_ctx, x_dec, mod, norm_g, w_in_t):
    full = lambda a: pl.BlockSpec(a.shape, lambda i: (0,) * a.ndim)
    return pl.pallas_call(
        functools.partial(_inproj_kernel, tiles),
        grid=(tiles.n_ctx + tiles.n_dec,),
        in_specs=[*tiles.specs(D_MODEL), tiles.mod_spec(), full(norm_g),
                  pl.BlockSpec(w_in_t.shape, lambda i: (0, 0), pipeline_mode=pl.Buffered(1))],
        out_specs=tiles.merged(D_PROJ),
        out_shape=jax.ShapeDtypeStruct((x_ctx.shape[0] + x_dec.shape[0], D_PROJ), F32),
        scratch_shapes=[pltpu.VMEM((D_MODEL, D_PROJ), BF16)],
        compiler_params=_cparams(1),
        name="in_proj",
    )(x_ctx, x_dec, mod, norm_g, w_in_t)


def _time_masks(reverse):
    r = _iota((2 * CHUNK, 2 * CHUNK), 0) % CHUNK
    c = _iota((2 * CHUNK, 2 * CHUNK), 1) % CHUNK
    if reverse:
        return r < c, r <= c
    return r > c, r >= c


def _cumsum_matrix(reverse):
    r = _iota((CHUNK, CHUNK), 0)
    c = _iota((CHUNK, CHUNK), 1)
    tri = (r <= c) if reverse else (r >= c)
    return tri.astype(BF16)


def _stack_heads(x, half):
    m0 = _iota(x.shape, 1) < half
    return jnp.concatenate([jnp.where(m0, x, 0.0), jnp.where(m0, 0.0, x)], axis=0)


def _finished_before_last_scan(groups, n_chunks):
    done_fwd, done_bwd = set(), set()
    for s, fwd, bwd in groups[:-1]:
        done_fwd.update(s * n_chunks + c for c in fwd)
        done_bwd.update(s * n_chunks + c for c in bwd)
    return sorted(done_fwd & done_bwd)


def _head_sums(x):
    parts = []
    for p in range(x.shape[1] // LANE):
        xp = x[:, p * LANE:(p + 1) * LANE]
        m0 = _iota(xp.shape, 1) < 64
        s0 = jnp.sum(jnp.where(m0, xp, 0.0), axis=-1, keepdims=True)
        s1 = jnp.sum(jnp.where(m0, 0.0, xp), axis=-1, keepdims=True)
        parts.append(jnp.where(m0, s0, s1))
    return parts[0] if len(parts) == 1 else jnp.concatenate(parts, axis=1)


def _rwkv_chunk_terms(insts, interleaved=()):
    c = CHUNK
    step_row = _iota((c, LANE), 0)
    step_col = _iota((c, LANE), 1) % c
    eye_w = (step_row == step_col).astype(F32)
    same_head = (_iota((LANE, LANE), 0) // 64) == (_iota((LANE, LANE), 1) // 64)
    stack_bf = lambda x: _stack_heads(x, 64).astype(BF16)
    cums = [_mm_01_lhs(_cumsum_matrix(rev), lw) for (_, lw, _, _, _, _, rev) in insts]
    pre = []
    for (r, lw, kd, a, b, v, rev), cum in zip(insts, cums):
        end = cum[0:1] if rev else cum[c - 1:c]
        inv_w = jnp.exp(-cum)
        rem_w = jnp.exp(end - cum)
        a_t = a * jnp.exp(cum - lw)
        r_t = r * jnp.exp(cum)
        bk_s = jnp.concatenate([stack_bf(b * inv_w), stack_bf(kd * inv_w)], axis=0)
        bkh = jnp.concatenate([b * rem_w, kd * rem_w], axis=0).astype(BF16)
        pre.append((a_t, r_t, bk_s, bkh, v, jnp.exp(end)))
    ms = [_dot(jnp.concatenate([a_t, r_t], axis=0).astype(BF16), bk_s, _NT) for (a_t, r_t, bk_s, _, _, _) in pre]
    mats = []
    for m, (_, _, _, _, _, _, rev) in zip(ms, insts):
        strict = (step_row < step_col) if rev else (step_row > step_col)
        incl = (step_row <= step_col) if rev else (step_row >= step_col)
        l_ab = jnp.where(strict, m[0:c, 0:LANE], 0.0)
        l_akrk = jnp.concatenate([jnp.where(strict, m[0:c, LANE:2 * LANE], 0.0),
                                  jnp.where(incl, m[c:2 * c, LANE:2 * LANE], 0.0)], axis=0).astype(BF16)
        m_rb = jnp.where(incl, m[c:2 * c, 0:LANE], 0.0).astype(BF16)
        mats.append((l_ab, l_akrk, m_rb))
    pending = list(interleaved)

    def run_interleaved():
        if pending:
            pending.pop(0)()

    lvs = [_dot(l_akrk, stack_bf(pr[4])) for (_, l_akrk, _), pr in zip(mats, pre)]
    run_interleaved()
    ps = [eye_w + l_ab for (l_ab, _, _) in mats]
    lps = [_dot(l_ab.astype(BF16), stack_bf(l_ab)) for (l_ab, _, _) in mats]
    for level in range(1, 6):
        if level < 5:
            xs = [_dot(lp.astype(BF16), jnp.concatenate([stack_bf(p), stack_bf(lp)], axis=1))
                  for lp, p in zip(lps, ps)]
            ps = [p + x[:, 0:LANE] for p, x in zip(ps, xs)]
            lps = [x[:, LANE:2 * LANE] for x in xs]
        else:
            ps = [p + _dot(lp.astype(BF16), stack_bf(p)) for lp, p in zip(lps, ps)]
        if level in (2, 4):
            run_interleaved()
    pxs = [_dot(p.astype(BF16), jnp.concatenate([stack_bf(pr[0]), stack_bf(lv[0:c])], axis=1))
           for p, pr, lv in zip(ps, pre, lvs)]
    run_interleaved()
    mzs = [_dot(mt[2], jnp.concatenate([stack_bf(px[:, 0:LANE]), stack_bf(px[:, LANE:2 * LANE])], axis=1))
           for mt, px in zip(mats, pxs)]
    ts = [_dot(px[:, 0:LANE].astype(BF16), pr[3][0:c], _TN) for px, pr in zip(pxs, pre)]
    gs = [_dot(jnp.concatenate([px[:, LANE:2 * LANE], pr[4]], axis=0).astype(BF16), pr[3], _TN)
          for px, pr in zip(pxs, pre)]
    while pending:
        run_interleaved()
    out = []
    for pr, lv, mz, t, g in zip(pre, lvs, mzs, ts, gs):
        q = pr[1] + mz[:, 0:LANE]
        y0 = mz[:, LANE:2 * LANE] + lv[c:2 * c]
        g_wide = jnp.where(_iota((c, LANE), 1) < 64, g[0:c], g[c:2 * c])
        out.append((jnp.where(same_head, t, 0.0).astype(BF16), g_wide, pr[5],
                    q.astype(BF16), y0))
    return out


def _gla_chunk_terms(insts, interleaved=()):
    c = CHUNK
    step_row = _iota((c, LANE), 0)
    step_col = _iota((c, LANE), 1) % c
    same_head = (_iota((LANE, 2 * LANE), 0) // 64) == (_iota((LANE, 2 * LANE), 1) // LANE)
    cums = [_mm_01_lhs(_cumsum_matrix(rev), g) for (_, _, _, g, rev) in insts]
    pre = []
    for (q, k, v, g, rev), cum in zip(insts, cums):
        end = cum[0:1] if rev else cum[c - 1:c]
        qt = (q * jnp.exp(cum)).astype(BF16)
        k_s = _stack_heads(k * jnp.exp(-cum), 64).astype(BF16)
        kh = (k * jnp.exp(end - cum)).astype(BF16)
        a_col = jnp.broadcast_to(jnp.exp(end), (LANE, LANE)).T
        pre.append((qt, k_s, kh, v.astype(BF16), _stack_heads(v, LANE).astype(BF16), a_col))
    pending = list(interleaved)

    def run_interleaved():
        if pending:
            pending.pop(0)()

    run_interleaved()
    atts = [_dot(pr[0], pr[1], _NT) for pr in pre]
    run_interleaved()
    atts = [jnp.where((step_row <= step_col) if inst[4] else (step_row >= step_col), att, 0.0).astype(BF16)
            for att, inst in zip(atts, insts)]
    o0s = [_dot(att, pr[4]) for att, pr in zip(atts, pre)]
    run_interleaved()
    kvs = [jnp.where(same_head, _dot(pr[2], pr[3], _TN), 0.0) for pr in pre]
    while pending:
        run_interleaved()
    return [(pr[0], o0, pr[5], kv) for pr, o0, kv in zip(pre, o0s, kvs)]


def _rwkv_kernel(seq_len, nseq, is_grid, zero_init, r_ref, k_ref, v_ref, lora_ref, cwr_ref, cwk_ref, cwv_ref,
                 w2_ref, a2_ref, g2_ref, vec_ref, *rest):
    s0_ref = None if zero_init else rest[0]
    (y_ref, sout_ref, pad_ref, rs_ref, ks_ref, vs_ref, kk_ref, bonus_ref, gate_ref, yf_ref, yb_ref, st_ref, tt_ref,
     tg_ref, tw_ref, tq_ref, ty_ref) = rest[0 if zero_init else 1:][:17]
    left_ref, right_ref = rest[-2:] if is_grid else (None, None)
    n_chunks = seq_len // CHUNK
    npp = RW_PAIRS_PER_STEP
    w = npp * LANE
    pair = lambda x, p: x[:, p * LANE:(p + 1) * LANE]
    vec = vec_ref[...]
    w0 = (vec[0:1], vec[1:2])
    a0 = (vec[2:3], vec[3:4])
    k_k, k_a, r_k, ln_w, ln_b = vec[4:5], vec[5:6], vec[6:7], vec[7:8], vec[8:9]
    block_sum = _head_sums
    chains = [(d, p) for d in range(2) for p in range(npp)]
    rows_of = lambda s, c: slice((s * n_chunks + c) * CHUNK, (s * n_chunks + c + 1) * CHUNK)

    zeros = jnp.zeros((CONV_PAD, 3 * w), F32)
    for ref in (pad_ref, left_ref, right_ref) if is_grid else (pad_ref,):
        ref[0:CONV_PAD, :] = zeros
        ref[CONV_PAD + seq_len:2 * CONV_PAD + seq_len, :] = zeros
    cw = jnp.concatenate([cwr_ref[...], cwk_ref[...], cwv_ref[...]], axis=1)
    col = _iota((CHUNK, 3 * w), 0)

    def shift_body(c, carry):
        base = pl.multiple_of(CONV_PAD + c * CHUNK, CHUNK)
        win = pad_ref[pl.ds(base - 8, CHUNK + 16), :]
        left_ref[pl.ds(base, CHUNK), :] = jnp.where(col >= 1, win[7:7 + CHUNK], 0.0)
        right_ref[pl.ds(base, CHUNK), :] = jnp.where(col <= GRID_W - 2, win[9:9 + CHUNK], 0.0)
        return carry

    def conv_chunk(c):
        base = CONV_PAD + c * CHUNK
        acc = jnp.zeros((CHUNK, 3 * w), F32)
        if is_grid:
            for di in (-1, 0, 1):
                row = pl.ds(base + di * GRID_W, CHUNK)
                for dj, src in ((-1, left_ref), (0, pad_ref), (1, right_ref)):
                    tap = (di + 1) * 3 + dj + 1
                    acc = acc + src[row, :] * cw[tap:tap + 1]
        else:
            win = pad_ref[pl.ds(base - 8, CHUNK + 16), :]
            for dj in (-1, 0, 1):
                acc = acc + win[8 + dj:8 + dj + CHUNK] * cw[4 + dj:5 + dj]
        return acc[:, 0:w], acc[:, w:2 * w], acc[:, 2 * w:3 * w]

    def load_sequence(s):
        seq_rows = slice(s * seq_len, (s + 1) * seq_len)
        pad_ref[CONV_PAD:CONV_PAD + seq_len, 0:w] = r_ref[seq_rows, :]
        pad_ref[CONV_PAD:CONV_PAD + seq_len, w:2 * w] = k_ref[seq_rows, :]
        pad_ref[CONV_PAD:CONV_PAD + seq_len, 2 * w:3 * w] = v_ref[seq_rows, :]

    def conv_store(s, c):
        rows = rows_of(s, c)
        rc, kc, vc = conv_chunk(c)
        kk = kc * k_k
        rs_ref[rows, :] = rc
        ks_ref[rows, :] = kc
        vs_ref[rows, :] = vc
        kk_ref[rows, :] = kk * lax.rsqrt(block_sum(kk * kk) + EPS)
        bonus_ref[rows, :] = block_sum(rc * kc * r_k) * vc
        gate_ref[rows, :] = _mm(_sigmoid(lora_ref[rows, 2 * LANE:3 * LANE]), g2_ref[...])

    for s in range(nseq):
        for d, p in chains:
            if zero_init:
                st_ref[s, d, p] = jnp.zeros((64, LANE), F32)
            else:
                st_ref[s, d, p] = jnp.concatenate([s0_ref[s, d, 2 * p], s0_ref[s, d, 2 * p + 1]], axis=1)

    if nseq == 1:
        groups = [(0, list(range(g * TERM_UNROLL, (g + 1) * TERM_UNROLL)),
                   list(range(n_chunks - 1 - g * TERM_UNROLL, n_chunks - 1 - (g + 1) * TERM_UNROLL, -1)))
                  for g in range(n_chunks // TERM_UNROLL)]
    else:
        groups = [(s, list(range(n_chunks)), list(range(n_chunks - 1, -1, -1))) for s in range(nseq)]

    def group_terms(group, interleaved):
        s, fwd, bwd = group
        insts, where, lora_in = [], [], {}
        for d, chunks in ((0, fwd), (1, bwd)):
            for c in chunks:
                rows = rows_of(s, c)
                if c not in lora_in:
                    lora_in[c] = (jnp.tanh(lora_ref[rows, 0:LANE]).astype(BF16),
                                  lora_ref[rows, LANE:2 * LANE].astype(BF16))
                lw = -RW_DECAY_SCALE * _sigmoid(w0[d] + _dot(lora_in[c][0], w2_ref[d]))
                ag = _sigmoid(a0[d] + _dot(lora_in[c][1], a2_ref[d]))
                rc, kc, vc, kk = rs_ref[rows, :], ks_ref[rows, :], vs_ref[rows, :], kk_ref[rows, :]
                kd = kc * (1.0 + (ag - 1.0) * k_a)
                kb = kk * ag
                for p in range(npp):
                    insts.append((pair(rc, p), pair(lw, p), pair(kd, p), -pair(kk, p), pair(kb, p), pair(vc, p),
                                  d == 1))
                    where.append((d, p, s * n_chunks + c))
        for (d, p, gc), (t, g, w_end, q, y0) in zip(where, _rwkv_chunk_terms(insts, interleaved)):
            tt_ref[d, p, gc] = t
            tg_ref[d, p, gc] = g
            tw_ref[d, p, gc] = jnp.broadcast_to(w_end, (8, LANE))
            tq_ref[d, p, gc] = q
            ty_ref[d, p, gc] = y0

    def scan_step(s, chunk_of_dir):
        gcs = [s * n_chunks + chunk_of_dir[d] for d, _ in chains]
        ss = [st_ref[s, d, p] for d, p in chains]
        ys = [_dot(tq_ref[d, p, gc], _stack_heads(x, 64).astype(BF16), _NT) + ty_ref[d, p, gc]
              for (d, p), gc, x in zip(chains, gcs, ss)]
        sn = [x * tw_ref[d, p, gc][0:1] + _dot(x.astype(BF16), tt_ref[d, p, gc]) + tg_ref[d, p, gc]
              for (d, p), gc, x in zip(chains, gcs, ss)]
        for (d, p), gc, y, x in zip(chains, gcs, ys, sn):
            st_ref[s, d, p] = x
            out_ref = yf_ref if d == 0 else yb_ref
            out_ref[gc * CHUNK:(gc + 1) * CHUNK, p * LANE:(p + 1) * LANE] = y

    def group_scan(group):
        s, fwd, bwd = group
        return [functools.partial(scan_step, s, (cf, cb)) for cf, cb in zip(fwd, bwd)]

    conv_done, seq_loaded = set(), set()

    def group_conv(group):
        s, fwd, bwd = group
        todo = [c for c in sorted(set(fwd) | set(bwd)) if (s, c) not in conv_done]
        conv_done.update((s, c) for c in todo)
        thunks = []
        for i in range(0, len(todo), max(1, -(-len(todo) // TERM_UNROLL))):
            part = todo[i:i + max(1, -(-len(todo) // TERM_UNROLL))]
            need_load = s not in seq_loaded
            seq_loaded.add(s)

            def run(part=part, need_load=need_load):
                if need_load:
                    assert not is_grid or nseq == 1
                    load_sequence(s)
                for c in part:
                    conv_store(s, c)

            thunks.append(run)
        return thunks

    def merge(*lists):
        n = max(len(lst) for lst in lists)
        pick = lambda lst, i: lst[i] if i < len(lst) else (lambda: None)
        return [lambda i=i: [pick(lst, i)() for lst in lists] for i in range(n)]

    def post(offs):
        ys = [yf_ref[pl.ds(off, CHUNK), :] + yb_ref[pl.ds(off, CHUNK), :] for off in offs]
        mus = [block_sum(y) * (1.0 / 64) for y in ys]
        dlts = [y - mu for y, mu in zip(ys, mus)]
        vrs = [block_sum(dlt * dlt) * (1.0 / 64) for dlt in dlts]
        for off, dlt, var in zip(offs, dlts, vrs):
            yn = dlt * lax.rsqrt(var + RW_LN_EPS) * ln_w + ln_b
            y = (yn + bonus_ref[pl.ds(off, CHUNK), :]) * gate_ref[pl.ds(off, CHUNK), :]
            y_ref[pl.ds(off, CHUNK), :] = y.astype(y_ref.dtype)

    def sequence_post(s):
        return [functools.partial(post, [(s * n_chunks + c) * CHUNK]) for c in range(n_chunks)]

    if is_grid:
        load_sequence(0)
        seq_loaded.add(0)
        lax.fori_loop(0, n_chunks, shift_body, 0)
    for thunk in group_conv(groups[0]):
        thunk()
    posted = set()
    for g, group in enumerate(groups):
        scans = group_scan(groups[g - 1]) if g else []
        convs = group_conv(groups[g + 1]) if g + 1 < len(groups) else []
        posts = []
        if nseq > 1 and g >= 2:
            s_done = groups[g - 2][0]
            posts = sequence_post(s_done)
            posted.update(range(s_done * n_chunks, (s_done + 1) * n_chunks))
        group_terms(group, merge(scans, convs, posts))
    early = [gc for gc in _finished_before_last_scan(groups, n_chunks) if gc not in posted]
    last_scan = group_scan(groups[-1])
    per_step = max(1, -(-len(early) // len(last_scan)))
    early_posts = [functools.partial(post, [gc * CHUNK for gc in early[i:i + per_step]])
                   for i in range(0, len(early), per_step)]
    for step, extra in itertools.zip_longest(last_scan, early_posts, fillvalue=lambda: None):
        step()
        extra()
    posted.update(early)
    for s in range(nseq):
        for d, p in chains:
            x = st_ref[s, d, p]
            sout_ref[s, d, 2 * p] = x[:, 0:64]
            sout_ref[s, d, 2 * p + 1] = x[:, 64:LANE]

    left = [gc for gc in range(nseq * n_chunks) if gc not in posted]
    for i in range(0, len(left), TERM_UNROLL):
        post([gc * CHUNK for gc in left[i:i + TERM_UNROLL]])


def _rwkv_mixer(proj, first_seq, n_seq, seq_len, is_grid, prm, s0):
    n_pairs = RW_W // LANE
    n_chunks = seq_len // CHUNK
    npp = RW_PAIRS_PER_STEP
    w = npp * LANE
    nseq = max(1, RW_GROUPS * TERM_UNROLL // n_chunks)
    assert n_chunks % TERM_UNROLL == 0 and n_pairs % npp == 0
    assert (n_chunks == TERM_UNROLL or nseq == 1) and n_seq % nseq == 0 and first_seq % nseq == 0
    rows = nseq * seq_len
    total_chunks = nseq * n_chunks
    first = first_seq // nseq
    col = lambda cb: (lambda b, p: (b + first, cb // npp + p))
    par = lambda cb: (lambda b, p: (0, cb // npp + p))
    state_spec = pl.BlockSpec((nseq, 2, 2 * npp, 64, 64), lambda b, p: (b, 0, p, 0, 0))
    kernel = functools.partial(_rwkv_kernel, seq_len, nseq, is_grid, s0 is None)
    y, s_out = pl.pallas_call(
        kernel,
        grid=(n_seq // nseq, n_pairs // npp),
        in_specs=[pl.BlockSpec((rows, w), col(CB_R)),
                  pl.BlockSpec((rows, w), col(CB_K)),
                  pl.BlockSpec((rows, w), col(CB_V)),
                  pl.BlockSpec((rows, 3 * LANE), lambda b, p: (b + first, CB_LORA // 3)),
                  pl.BlockSpec((9, w), par(CB_R)),
                  pl.BlockSpec((9, w), par(CB_K)),
                  pl.BlockSpec((9, w), par(CB_V)),
                  pl.BlockSpec((2, LANE, w), lambda b, p: (0, 0, p)),
                  pl.BlockSpec((2, LANE, w), lambda b, p: (0, 0, p)),
                  pl.BlockSpec((LANE, w), lambda b, p: (0, p)),
                  pl.BlockSpec((16, w), lambda b, p: (0, p))] + ([] if s0 is None else [state_spec]),
        out_specs=[pl.BlockSpec((rows, w), lambda b, p: (b, p)), state_spec],
        out_shape=[jax.ShapeDtypeStruct((n_seq * seq_len, RW_W), BF16),
                   jax.ShapeDtypeStruct((n_seq, 2, 2 * n_pairs, 64, 64), F32)],
        scratch_shapes=[pltpu.VMEM((seq_len + 2 * CONV_PAD, 3 * w), F32)]
                       + [pltpu.VMEM((rows, w), F32)] * 8
                       + [pltpu.VMEM((nseq, 2, npp, 64, LANE), F32),
                          pltpu.VMEM((2, npp, total_chunks, LANE, LANE), BF16),
                          pltpu.VMEM((2, npp, total_chunks, 64, LANE), F32),
                          pltpu.VMEM((2, npp, total_chunks, 8, LANE), F32),
                          pltpu.VMEM((2, npp, total_chunks, CHUNK, LANE), BF16),
                          pltpu.VMEM((2, npp, total_chunks, CHUNK, LANE), F32)]
                       + ([pltpu.VMEM((seq_len + 2 * CONV_PAD, 3 * w), F32)] * 2 if is_grid else []),
        compiler_params=_cparams(2),
        name="rwkv_mixer",
    )(proj, proj, proj, proj, prm['conv'], prm['conv'], prm['conv'], prm['w2p'], prm['a2p'], prm['g2'],
      prm['vec'], *([] if s0 is None else [s0]))
    return y, s_out


def _gla_kernel(seq_len, nseq, zero_init, q_ref, k_ref, v_ref, og_ref, lgk_ref, gk2_ref, gvec_ref, *rest):
    s0_ref = None if zero_init else rest[0]
    y_ref, sout_ref, of_ref, ob_ref, st_ref, tq_ref, to_ref, ta_ref, tkv_ref = rest[0 if zero_init else 1:]
    n_chunks = seq_len // CHUNK
    npp = GLA_PAIRS
    gvec = gvec_ref[...]
    chains = [(d, p) for d in range(2) for p in range(npp)]

    for s in range(nseq):
        for d, p in chains:
            if zero_init:
                st_ref[s, d, p] = jnp.zeros((LANE, 2 * LANE), F32)
            else:
                z = jnp.zeros((64, LANE), F32)
                st_ref[s, d, p] = jnp.concatenate([jnp.concatenate([s0_ref[s, d, 2 * p], z], axis=1),
                                                   jnp.concatenate([z, s0_ref[s, d, 2 * p + 1]], axis=1)], axis=0)

    if nseq == 1:
        groups = [(0, list(range(g * TERM_UNROLL, (g + 1) * TERM_UNROLL)),
                   list(range(n_chunks - 1 - g * TERM_UNROLL, n_chunks - 1 - (g + 1) * TERM_UNROLL, -1)))
                  for g in range(n_chunks // TERM_UNROLL)]
    else:
        groups = [(s, list(range(n_chunks)), list(range(n_chunks - 1, -1, -1))) for s in range(nseq)]

    def group_terms(group, interleaved):
        s, fwd, bwd = group
        insts, where, lgk_in = [], [], {}
        for d, chunks in ((0, fwd), (1, bwd)):
            for c in chunks:
                gc = s * n_chunks + c
                rows = slice(gc * CHUNK, (gc + 1) * CHUNK)
                if c not in lgk_in:
                    lgk_in[c] = lgk_ref[rows, :].astype(BF16)
                x = _dot(lgk_in[c], gk2_ref[d]) + gvec[d:d + 1, 0:GLA_QK_W]
                g = _log_sigmoid(x) * (1.0 / GLA_GATE_NORM)
                qc = q_ref[rows, :] * GLA_Q_SCALE
                kc = k_ref[rows, :]
                vc = v_ref[rows, :]
                for p in range(npp):
                    qk = slice(p * LANE, (p + 1) * LANE)
                    insts.append((qc[:, qk], kc[:, qk], vc[:, 2 * p * LANE:2 * (p + 1) * LANE], g[:, qk], d == 1))
                    where.append((d, p, gc))
        for (d, p, gc), (qt, o0, a_col, kv) in zip(where, _gla_chunk_terms(insts, interleaved)):
            tq_ref[d, p, gc] = qt
            to_ref[d, p, gc] = o0
            ta_ref[d, p, gc] = a_col
            tkv_ref[d, p, gc] = kv

    def scan_step(s, chunk_of_dir):
        gcs = [s * n_chunks + chunk_of_dir[d] for d, _ in chains]
        ss = [st_ref[s, d, p] for d, p in chains]
        os_ = [_dot(tq_ref[d, p, gc], x.astype(BF16)) + to_ref[d, p, gc] for (d, p), gc, x in zip(chains, gcs, ss)]
        for (d, p), gc, x, o in zip(chains, gcs, ss, os_):
            a_col = ta_ref[d, p, gc]
            st_ref[s, d, p] = x * jnp.concatenate([a_col, a_col], axis=1) + tkv_ref[d, p, gc]
            out_ref = of_ref if d == 0 else ob_ref
            out_ref[gc * CHUNK:(gc + 1) * CHUNK, 2 * p * LANE:2 * (p + 1) * LANE] = o

    def group_scan(group):
        s, fwd, bwd = group
        return [functools.partial(scan_step, s, (cf, cb)) for cf, cb in zip(fwd, bwd)]

    def post(off):
        for h in range(2 * npp):
            hs = slice(h * LANE, (h + 1) * LANE)
            o = of_ref[pl.ds(off, CHUNK), hs] + ob_ref[pl.ds(off, CHUNK), hs]
            gate = _silu(og_ref[pl.ds(off, CHUNK), hs])
            y = _rmsnorm_rows(o) * gvec[2:3, hs] * gate
            y_ref[pl.ds(off, CHUNK), hs] = y.astype(y_ref.dtype)

    posted = set()
    for g, group in enumerate(groups):
        thunks = group_scan(groups[g - 1]) if g else []
        if nseq > 1 and g >= 2:
            s_done = groups[g - 2][0]
            posts = [functools.partial(post, (s_done * n_chunks + c) * CHUNK) for c in range(n_chunks)]
            thunks = [lambda a=a, b=b: (a(), b()) for a, b in zip(thunks, posts)]
            posted.update(range(s_done * n_chunks, (s_done + 1) * n_chunks))
        group_terms(group, thunks)
    early = [gc for gc in _finished_before_last_scan(groups, n_chunks) if gc not in posted]
    last_scan = group_scan(groups[-1])
    per_step = max(1, -(-len(early) // len(last_scan)))
    for i, step in enumerate(last_scan):
        step()
        for gc in early[i * per_step:(i + 1) * per_step]:
            post(gc * CHUNK)
    for gc in early[len(last_scan) * per_step:]:
        post(gc * CHUNK)
    posted.update(early)
    for s in range(nseq):
        for d, p in chains:
            x = st_ref[s, d, p]
            sout_ref[s, d, 2 * p] = x[0:64, 0:LANE]
            sout_ref[s, d, 2 * p + 1] = x[64:LANE, LANE:2 * LANE]
    for gc in range(nseq * n_chunks):
        if gc not in posted:
            post(gc * CHUNK)


def _gla_mixer(proj, first_seq, n_seq, seq_len, prm, s0):
    npp = GLA_PAIRS
    n_heads = 2 * npp
    n_chunks = seq_len // CHUNK
    nseq = max(1, RW_GROUPS * TERM_UNROLL // n_chunks)
    assert n_chunks % TERM_UNROLL == 0
    assert (n_chunks == TERM_UNROLL or nseq == 1) and n_seq % nseq == 0 and first_seq % nseq == 0
    rows = nseq * seq_len
    total_chunks = nseq * n_chunks
    first = first_seq // nseq
    state_spec = pl.BlockSpec((nseq, 2, n_heads, 64, LANE), lambda b: (b, 0, 0, 0, 0))
    kernel = functools.partial(_gla_kernel, seq_len, nseq, s0 is None)
    y, s_out = pl.pallas_call(
        kernel,
        grid=(n_seq // nseq,),
        in_specs=[pl.BlockSpec((rows, GLA_QK_W), lambda b: (b + first, CB_GQ * LANE // GLA_QK_W)),
                  pl.BlockSpec((rows, GLA_QK_W), lambda b: (b + first, CB_GK * LANE // GLA_QK_W)),
                  pl.BlockSpec((rows, GLA_V_W), lambda b: (b + first, CB_GV * LANE // GLA_V_W)),
                  pl.BlockSpec((rows, GLA_V_W), lambda b: (b + first, CB_OG * LANE // GLA_V_W)),
                  pl.BlockSpec((rows, LANE), lambda b: (b + first, CB_LGK)),
                  pl.BlockSpec((2, LANE, GLA_QK_W), lambda b: (0, 0, 0)),
                  pl.BlockSpec((8, GLA_V_W), lambda b: (0, 0))] + ([] if s0 is None else [state_spec]),
        out_specs=[pl.BlockSpec((rows, GLA_V_W), lambda b: (b, 0)), state_spec],
        out_shape=[jax.ShapeDtypeStruct((n_seq * seq_len, GLA_V_W), BF16),
                   jax.ShapeDtypeStruct((n_seq, 2, n_heads, 64, LANE), F32)],
        scratch_shapes=[pltpu.VMEM((rows, GLA_V_W), F32)] * 2
                       + [pltpu.VMEM((nseq, 2, npp, LANE, 2 * LANE), F32),
                          pltpu.VMEM((2, npp, total_chunks, CHUNK, LANE), BF16),
                          pltpu.VMEM((2, npp, total_chunks, CHUNK, 2 * LANE), F32),
                          pltpu.VMEM((2, npp, total_chunks, LANE, LANE), F32),
                          pltpu.VMEM((2, npp, total_chunks, LANE, 2 * LANE), F32)],
        compiler_params=_cparams(1),
        name="gla_mixer",
    )(proj, proj, proj, proj, proj, prm['gk2p'], prm['gvec'], *([] if s0 is None else [s0]))
    return y, s_out


OUT_TM = 512
ROUTE_NEG = -1e30
LANE_GROUP0 = N_EXPERTS


def _route(logits):
    lane = _iota(logits.shape, 1)
    lane_f = lane.astype(F32)
    big = float(LANE)
    is_g = (lane >= LANE_GROUP0) & (lane < LANE_GROUP0 + 4)
    gmax = jnp.max(jnp.where(is_g, logits, ROUTE_NEG), axis=-1, keepdims=True)
    gidx = jnp.min(jnp.where(is_g & (logits == gmax), lane_f, big), axis=-1, keepdims=True) - LANE_GROUP0
    gsum = jnp.sum(jnp.where(is_g, jnp.exp(jnp.minimum(logits - gmax, 0.0)), 0.0), axis=-1, keepdims=True)
    g_w = 1.0 / gsum
    in_grp = (lane < N_EXPERTS) & ((lane // 4).astype(F32) == gidx)
    m1 = jnp.max(jnp.where(in_grp, logits, ROUTE_NEG), axis=-1, keepdims=True)
    i1 = jnp.min(jnp.where(in_grp & (logits == m1), lane_f, big), axis=-1, keepdims=True)
    rest = in_grp & (lane_f != i1)
    m2 = jnp.max(jnp.where(rest, logits, ROUTE_NEG), axis=-1, keepdims=True)
    i2 = jnp.min(jnp.where(rest & (logits == m2), lane_f, big), axis=-1, keepdims=True)
    t = jnp.exp(m2 - m1)
    w1 = g_w / (1.0 + t)
    return jnp.where(lane_f == i1, w1, 0.0) + jnp.where(lane_f == i2, w1 * t, 0.0)


def _outproj_kernel(tiles, yrc_ref, yrd_ref, ygc_ref, ygd_ref, xc_ref, xd_ref, mod_ref, wo_ref, g_ref, wr_ref,
                    br_ref, x1_ref, h2_ref, cmb_ref):
    def run(yr_ref, yg_ref, x_ref):
        m = mod_ref[0]
        mix = _mm(yr_ref[...], wo_ref[0:RW_W, :]) + _mm(yg_ref[...], wo_ref[RW_W:RW_W + GLA_V_W, :])
        x1 = x_ref[...] + m[2:3] * mix
        h2 = _rmsnorm_rows(x1) * g_ref[...] * (1.0 + m[4:5]) + m[3:4]
        x1_ref[...] = x1
        h2_ref[...] = h2.astype(BF16)
        h_hi, h_lo = _split2(h2)
        both = _dot(h_hi, wr_ref[...])
        logits = both[:, 0:LANE] + both[:, LANE:2 * LANE] + _dot(h_lo, wr_ref[:, 0:LANE])
        cmb_ref[...] = _route(logits + br_ref[...])

    tiles.by_pass(functools.partial(run, yrc_ref, ygc_ref, xc_ref), functools.partial(run, yrd_ref, ygd_ref, xd_ref))


def _out_projection(tiles, y_rw, y_gla, x, mod, w_out, norm_g, w_route, b_route):
    n = x[0].shape[0] + x[1].shape[0]
    full = lambda a: pl.BlockSpec(a.shape, lambda i: (0,) * a.ndim)
    return pl.pallas_call(
        functools.partial(_outproj_kernel, tiles),
        grid=(tiles.n_ctx + tiles.n_dec,),
        in_specs=[*tiles.specs(RW_W), *tiles.specs(GLA_V_W), *tiles.specs(D_MODEL), tiles.mod_spec(),
                  full(w_out), full(norm_g), full(w_route), full(b_route)],
        out_specs=[tiles.merged(D_MODEL), tiles.merged(D_MODEL), tiles.merged(LANE)],
        out_shape=[jax.ShapeDtypeStruct((n, D_MODEL), F32), jax.ShapeDtypeStruct((n, D_MODEL), BF16),
                   jax.ShapeDtypeStruct((n, LANE), F32)],
        compiler_params=_cparams(1),
        name="out_proj_router",
    )(*y_rw, *y_gla, *x, mod, w_out, norm_g, w_route, b_route)


MOE_TM = 512
MOE_RB = 128
MOE_INTERLEAVE = 4
SLOT_ALIGN = 16
MOE_SLOTS = 2 * MOE_TM + N_EXPERTS * SLOT_ALIGN


def _stage_expert_weights(srcs_hbm, dst_refs, stage_refs, sems):
    def copies(e):
        return [pltpu.make_async_copy(src.at[e], stage.at[e % 2], sem.at[e % 2])
                for src, stage, sem in zip(srcs_hbm, stage_refs, sems)]

    for c in copies(0):
        c.start()
    for e in range(N_EXPERTS):
        if e + 1 < N_EXPERTS:
            for c in copies(e + 1):
                c.start()
        for c, dst, stage in zip(copies(e), dst_refs, stage_refs):
            c.wait()
            dst[e] = stage[e % 2].astype(BF16)


def _moe_kernel(tiles, h2_ref, cmb_ref, x1_ref, mod_ref, w1_hbm, w3_hbm, w2_hbm, fg_ref, yc_ref, yd_ref,
                xs_ref, ys_ref, w1_ref, w3_ref, w2_ref, stage1_ref, stage3_ref, stage2_ref, sem1, sem3, sem2):
    @pl.when(pl.program_id(0) == 0)
    def _():
        _stage_expert_weights((w1_hbm, w3_hbm, w2_hbm), (w1_ref, w3_ref, w2_ref),
                              (stage1_ref, stage3_ref, stage2_ref), (sem1, sem3, sem2))

    cmb = cmb_ref[...]
    lane = _iota(cmb.shape, 1).astype(F32)
    sel = cmb > 0.0
    sel01 = jnp.where(sel, 1.0, 0.0).astype(BF16)
    before = (_iota((MOE_TM, MOE_TM), 0) > _iota((MOE_TM, MOE_TM), 1)).astype(BF16)
    pos = _dot(before, sel01)
    cnt = pos[MOE_TM - 1:MOE_TM] + sel01[MOE_TM - 1:MOE_TM].astype(F32)
    seg = jnp.floor((cnt + (SLOT_ALIGN - 1)) * (1.0 / SLOT_ALIGN))
    lower_experts = (_iota((LANE, LANE), 0) < _iota((LANE, LANE), 1)).astype(BF16)
    start = _dot(jnp.broadcast_to(seg, (8, LANE)).astype(BF16), lower_experts)[0:1] * SLOT_ALIGN
    n_blk = jnp.floor((cnt + (MOE_RB - 1)) * (1.0 / MOE_RB)).astype(jnp.int32)
    start_i = start.astype(jnp.int32)
    cnt_i = cnt.astype(jnp.int32)
    slot = start + pos
    e_a = jnp.min(jnp.where(sel, lane, float(LANE)), axis=-1, keepdims=True)
    e_b = jnp.max(jnp.where(sel, lane, -1.0), axis=-1, keepdims=True)
    pick = lambda e, x: jnp.sum(jnp.where(lane == e, x, 0.0), axis=-1, keepdims=True)
    slot_a, w_a = pick(e_a, slot), pick(e_a, cmb)
    slot_b = jnp.where(e_b != e_a, pick(e_b, slot), -1.0)
    w_b = pick(e_b, cmb)

    slots_t = jnp.where(lane == 0.0, slot_a, jnp.where(lane == 1.0, slot_b, -1.0)).T
    row_slot = _iota((MOE_SLOTS, MOE_TM), 0).astype(F32)
    gather = jnp.where((row_slot == slots_t[0:1]) | (row_slot == slots_t[1:2]), 1.0, 0.0).astype(BF16)
    xs_ref[...] = _dot(gather, h2_ref[...]).astype(BF16)
    ys_ref[...] = jnp.zeros_like(ys_ref)

    row_in_blk = _iota((MOE_RB, D_MODEL), 0)

    def expert_blocks(experts, r0s, ends):
        base = [pl.multiple_of(jnp.minimum(r0, MOE_SLOTS - MOE_RB), SLOT_ALIGN) for r0 in r0s]
        xbs = [xs_ref[pl.ds(b, MOE_RB), :] for b in base]
        gates = [_dot(xb, w3_ref[e]) for xb, e in zip(xbs, experts)]
        ups = [_dot(xb, w1_ref[e]) for xb, e in zip(xbs, experts)]
        acts = [(_silu(g) * u).astype(BF16) for g, u in zip(gates, ups)]
        outs = [_dot(a, w2_ref[e]) for a, e in zip(acts, experts)]
        for b, r0, end, out in zip(base, r0s, ends, outs):
            row = row_in_blk + b
            keep = (row >= end) | (row < r0)
            ys_ref[pl.ds(b, MOE_RB), :] = jnp.where(keep, ys_ref[pl.ds(b, MOE_RB), :], out.astype(BF16))

    seg_start = [pl.multiple_of(start_i[0, e], SLOT_ALIGN) for e in range(N_EXPERTS)]
    seg_end = [seg_start[e] + cnt_i[0, e] for e in range(N_EXPERTS)]
    for e0 in range(0, N_EXPERTS, MOE_INTERLEAVE):
        es = list(range(e0, e0 + MOE_INTERLEAVE))
        expert_blocks(es, [seg_start[e] for e in es], [seg_end[e] for e in es])
    for e in range(N_EXPERTS):
        def extra_block(b, carry, e=e):
            expert_blocks([e], [pl.multiple_of(seg_start[e] + b * MOE_RB, SLOT_ALIGN)], [seg_end[e]])
            return carry

        lax.fori_loop(1, n_blk[0, e], extra_block, 0)

    col_slot = _iota((MOE_TM, MOE_SLOTS), 1).astype(F32)
    scatter = (jnp.where(col_slot == slot_a, w_a, 0.0) + jnp.where(col_slot == slot_b, w_b, 0.0)).astype(BF16)
    x2 = x1_ref[...] + mod_ref[0][5:6] * _dot(scatter, ys_ref[...])
    y = _rmsnorm_rows(x2) * fg_ref[...]

    def write(y_ref):
        y_ref[...] = y

    tiles.by_pass(functools.partial(write, yc_ref), functools.partial(write, yd_ref))


def _moe(tiles, h2, cmb, x1, mod, w1, w3, w2, final_g):
    assert tiles.tm == MOE_TM
    hbm = pl.BlockSpec(memory_space=pl.ANY)
    out_ctx, out_dec = tiles.specs(D_MODEL)
    return pl.pallas_call(
        functools.partial(_moe_kernel, tiles),
        grid=(tiles.n_ctx + tiles.n_dec,),
        in_specs=[tiles.merged(D_MODEL), tiles.merged(LANE), tiles.merged(D_MODEL), tiles.mod_spec(),
                  hbm, hbm, hbm, pl.BlockSpec((1, D_MODEL), lambda i: (0, 0))],
        out_specs=[out_ctx, out_dec],
        out_shape=[jax.ShapeDtypeStruct((tiles.n_ctx * MOE_TM, D_MODEL), F32),
                   jax.ShapeDtypeStruct((tiles.n_dec * MOE_TM, D_MODEL), F32)],
        scratch_shapes=[pltpu.VMEM((MOE_SLOTS, D_MODEL), BF16), pltpu.VMEM((MOE_SLOTS, D_MODEL), BF16),
                        pltpu.VMEM((N_EXPERTS, D_MODEL, D_EXPERT), BF16),
                        pltpu.VMEM((N_EXPERTS, D_MODEL, D_EXPERT), BF16),
                        pltpu.VMEM((N_EXPERTS, D_EXPERT, D_MODEL), BF16),
                        pltpu.VMEM((2, D_MODEL, D_EXPERT), F32), pltpu.VMEM((2, D_MODEL, D_EXPERT), F32),
                        pltpu.VMEM((2, D_EXPERT, D_MODEL), F32)] + [pltpu.SemaphoreType.DMA((2,))] * 3,
        compiler_params=_cparams(1),
        name="moe_experts",
    )(h2, cmb, x1, mod, w1, w3, w2, final_g)


def _pad_rows(x, rows):
    return jnp.pad(x, ((0, rows - x.shape[0]),) + ((0, 0),) * (x.ndim - 1))


def _pack_params(l, w_in, rw_conv, rw_w0, rw_w2, rw_a0, rw_a2, rw_g2, rw_k_k, rw_k_a, rw_r_k, rw_ln_w, rw_ln_b,
                 gla_gk2, gla_gk_b, gla_norm_g, moe_w_group, moe_b_group, moe_w_expert, moe_b_expert):
    wi = w_in[l]
    z = lambda n: jnp.zeros((D_MODEL, n), F32)
    w_in_t = jnp.swapaxes(wi, 0, 1)
    z64 = jnp.zeros((64, RW_W), F32)
    w2p = jnp.stack([jnp.concatenate([rw_w2[l, 0], z64], 0), jnp.concatenate([z64, rw_w2[l, 1]], 0)])
    a2p = jnp.stack([jnp.concatenate([rw_a2[l, 0], z64], 0), jnp.concatenate([z64, rw_a2[l, 1]], 0)])
    vec = _pad_rows(jnp.stack([rw_w0[l, 0], rw_w0[l, 1], rw_a0[l, 0], rw_a0[l, 1], rw_k_k[l], rw_k_a[l],
                               rw_r_k[l].reshape(RW_W), rw_ln_w[l], rw_ln_b[l]]), 16)
    rw = {'conv': rw_conv[l].reshape(9, 3 * RW_W), 'w2p': w2p.astype(BF16), 'a2p': a2p.astype(BF16),
          'g2': rw_g2[l].astype(BF16), 'vec': vec}
    gk2p = jnp.stack([_pad_rows(gla_gk2[l, 0], LANE),
                      _pad_rows(jnp.concatenate([jnp.zeros((16, GLA_QK_W), F32), gla_gk2[l, 1]], 0), LANE)])
    gk_b = jnp.pad(gla_gk_b[l], ((0, 0), (0, GLA_V_W - GLA_QK_W)))
    gvec = _pad_rows(jnp.concatenate([gk_b, jnp.tile(gla_norm_g[l], GLA_V_W // LANE)[None]], axis=0), 8)
    gla = {'gk2p': gk2p.astype(BF16), 'gvec': gvec}
    w_route = jnp.concatenate(_split2(jnp.concatenate([moe_w_expert[l], moe_w_group[l], z(LANE - N_EXPERTS - 4)],
                                                      axis=1)), axis=1)
    b_route = jnp.concatenate([moe_b_expert[l], moe_b_group[l], jnp.zeros((LANE - N_EXPERTS - 4,), F32)])[None]
    return w_in_t, rw, gla, w_route, b_route


def kernel(x_prompt, x_sample, state_rwkv, state_gla, c, c_ctx, norm1_g, norm2_g, w_ada, b_ada, w_in, w_out,
           rw_conv, rw_w0, rw_w2, rw_a0, rw_a2, rw_g2, rw_k_k, rw_k_a, rw_r_k, rw_ln_w, rw_ln_b,
           gla_gk2, gla_gk_b, gla_norm_g, moe_w_group, moe_b_group, moe_w_expert, moe_b_expert,
           moe_w1, moe_w3, moe_w2, final_g):
    depth = w_in.shape[0]
    assert depth == 1, "the packed layout below handles the single-layer trunk of this problem"
    l = 0
    n_dec = x_sample.shape[0]
    ctx_row = n_dec
    cond8 = _pad_rows(jnp.concatenate([c, c_ctx[None]], axis=0), 8)
    mod = _modulation(cond8, w_ada[l], b_ada[l][None]).reshape(8, N_MOD, D_MODEL)
    pk = _pack_params(l, w_in, rw_conv, rw_w0, rw_w2, rw_a0, rw_a2, rw_g2, rw_k_k, rw_k_a, rw_r_k, rw_ln_w,
                      rw_ln_b, gla_gk2, gla_gk_b, gla_norm_g, moe_w_group, moe_b_group, moe_w_expert,
                      moe_b_expert)
    w_in_t, rw, gla, w_route, b_route = pk

    n_ctx, ctx_len, _ = x_prompt.shape
    dec_len = x_sample.shape[1]
    x_ctx = x_prompt.reshape(n_ctx * ctx_len, D_MODEL)
    x_dec = x_sample.reshape(n_dec * dec_len, D_MODEL)
    assert (n_ctx * ctx_len) % dec_len == 0, "denoising sequences must start on a dec_len row block of proj"
    first_dec = n_ctx * ctx_len // dec_len
    tiles = _Tiles(n_ctx * ctx_len, n_dec * dec_len, dec_len, ctx_row, PROJ_TM)
    assert PROJ_TM == OUT_TM == MOE_TM

    proj = _in_projection(tiles, x_ctx, x_dec, mod, norm1_g[l][None], w_in_t)
    y_rw_c, s_rw = _rwkv_mixer(proj, 0, n_ctx, ctx_len, False, rw, None)
    y_gla_c, s_gla = _gla_mixer(proj, 0, n_ctx, ctx_len, gla, None)
    y_rw_d, _ = _rwkv_mixer(proj, first_dec, n_dec, dec_len, True, rw, state_rwkv[:, l])
    y_gla_d, _ = _gla_mixer(proj, first_dec, n_dec, dec_len, gla, state_gla[:, l])
    x1, h2, cmb = _out_projection(tiles, (y_rw_c, y_rw_d), (y_gla_c, y_gla_d), (x_ctx, x_dec), mod,
                                  w_out[l].astype(BF16), norm2_g[l][None], w_route, b_route)
    y_ctx, y_dec = _moe(tiles, h2, cmb, x1, mod, moe_w1[l], moe_w3[l], moe_w2[l], final_g[None])
    return (y_ctx.reshape(x_prompt.shape), y_dec.reshape(x_sample.shape), s_rw[:, None], s_gla[:, None])
```

```python
import functools
import itertools

import jax
import jax.numpy as jnp
from jax import lax
from jax.experimental import pallas as pl
from jax.experimental.pallas import tpu as pltpu

F32 = jnp.float32
BF16 = jnp.bfloat16

D_MODEL = 1024
RW_W = 512
GLA_V_W = 512
GLA_QK_W = 256
N_EXPERTS = 16
D_EXPERT = 256
N_MOD = 6
EPS = 1e-6
RW_LN_EPS = 64e-5
RW_DECAY_SCALE = 0.606531
GLA_GATE_NORM = 16.0
GLA_Q_SCALE = 64 ** -0.5
GRID_W = 64

LANE = 128
CHUNK = 64
CONV_PAD = 128
TERM_UNROLL = 4
RW_PAIRS_PER_STEP = 2
GLA_PAIRS = 2
RW_GROUPS = 4
D_PROJ = 28 * LANE
VMEM_LIMIT = 56 * 1024 * 1024

CB_R, CB_K, CB_V, CB_LORA, CB_LGK, CB_GQ, CB_GK, CB_GV, CB_OG = 0, 4, 8, 12, 15, 16, 18, 20, 24

_NN = (((1,), (0,)), ((), ()))
_NT = (((1,), (1,)), ((), ()))
_TN = (((0,), (0,)), ((), ()))


def _dot(a, b, dims=_NN):
    return lax.dot_general(a, b, dims, preferred_element_type=F32)


def _mm(a, b, dims=_NN):
    return _dot(a.astype(BF16), b.astype(BF16), dims)


def _split2(x):
    hi = x.astype(BF16)
    lo = (x - hi.astype(F32)).astype(BF16)
    return hi, lo


def _mm3(a, b, dims=_NN):
    ah, al = _split2(a)
    bh, bl = _split2(b)
    return _dot(ah, bh, dims) + _dot(ah, bl, dims) + _dot(al, bh, dims)


def _mm_01_lhs(a01, b, dims=_NN):
    n = b.shape[1]
    both = _dot(a01, jnp.concatenate(_split2(b), axis=1), dims)
    return both[:, 0:n] + both[:, n:2 * n]


def _sigmoid(x):
    return 0.5 * jnp.tanh(0.5 * x) + 0.5


def _silu(x):
    return x * _sigmoid(x)


def _log_sigmoid(x):
    return jnp.minimum(x, 0.0) - jnp.log(1.0 + jnp.exp(-jnp.abs(x)))


def _iota(shape, dim):
    return lax.broadcasted_iota(jnp.int32, shape, dim)


def _cparams(n_axes):
    return pltpu.CompilerParams(dimension_semantics=("arbitrary",) * n_axes, vmem_limit_bytes=VMEM_LIMIT)


MOD_TN = 768


def _mod_kernel(c_ref, w_ref, b_ref, o_ref):
    o_ref[...] = _mm3(_silu(c_ref[...]), w_ref[...]) + b_ref[...]


def _modulation(cond8, w_ada, b_ada):
    n = w_ada.shape[1]
    return pl.pallas_call(
        _mod_kernel,
        grid=(n // MOD_TN,),
        in_specs=[pl.BlockSpec((8, D_MODEL), lambda j: (0, 0)),
                  pl.BlockSpec((D_MODEL, MOD_TN), lambda j: (0, j)),
                  pl.BlockSpec((1, MOD_TN), lambda j: (0, j))],
        out_specs=pl.BlockSpec((8, MOD_TN), lambda j: (0, j)),
        out_shape=jax.ShapeDtypeStruct((8, n), F32),
        compiler_params=_cparams(1),
        name="adaln_mod",
    )(cond8, w_ada, b_ada)


PROJ_TM = 512
D_IN = 3488
N_LGK = 32


def _rmsnorm_rows(x):
    return x * lax.rsqrt(jnp.mean(x * x, axis=-1, keepdims=True) + EPS)


class _Tiles:
    def __init__(self, n_ctx_tokens, n_dec_tokens, dec_seq_len, ctx_row, tm):
        self.tm = tm
        self.n_ctx = n_ctx_tokens // tm
        self.n_dec = n_dec_tokens // tm
        self.per_seq = dec_seq_len // tm
        self.ctx_row = ctx_row

    def specs(self, width):
        last_ctx = self.n_ctx - 1
        n_ctx = self.n_ctx
        return (pl.BlockSpec((self.tm, width), lambda i: (jnp.minimum(i, last_ctx), 0)),
                pl.BlockSpec((self.tm, width), lambda i: (jnp.maximum(i - n_ctx, 0), 0)))

    def merged(self, width):
        return pl.BlockSpec((self.tm, width), lambda i: (i, 0))

    def mod_spec(self):
        n_ctx, per_seq, ctx_row = self.n_ctx, self.per_seq, self.ctx_row
        return pl.BlockSpec((1, N_MOD, D_MODEL),
                            lambda i: (jnp.where(i < n_ctx, ctx_row, (i - n_ctx) // per_seq), 0, 0))

    def by_pass(self, run_ctx, run_dec):
        i = pl.program_id(0)
        pl.when(i < self.n_ctx)(run_ctx)
        pl.when(i >= self.n_ctx)(run_dec)


def _inproj_kernel(tiles, xc_ref, xd_ref, mod_ref, g_ref, wt_ref, o_ref, w_ref):
    @pl.when(pl.program_id(0) == 0)
    def _():
        for j in range(D_PROJ // LANE):
            if j == CB_LGK:
                blk = jnp.concatenate([wt_ref[D_IN - N_LGK:D_IN, :], jnp.zeros((LANE - N_LGK, D_MODEL), F32)], axis=0)
            else:
                src = j if j < CB_LGK else j - 1
                blk = wt_ref[src * LANE:(src + 1) * LANE, :]
            w_ref[:, j * LANE:(j + 1) * LANE] = blk.T.astype(BF16)

    def run(x_ref):
        m = mod_ref[0]
        h = _rmsnorm_rows(x_ref[...]) * g_ref[...] * (1.0 + m[1:2]) + m[0:1]
        o_ref[...] = _mm(h, w_ref[...])

    tiles.by_pass(functools.partial(run, xc_ref), functools.partial(run, xd_ref))


def _in_projection(tiles, x_ctx, x_dec, mod, norm_g, w_in_t):
    full = lambda a: pl.BlockSpec(a.shape, lambda i: (0,) * a.ndim)
    return pl.pallas_call(
        functools.partial(_inproj_kernel, tiles),
        grid=(tiles.n_ctx + tiles.n_dec,),
        in_specs=[*tiles.specs(D_MODEL), tiles.mod_spec(), full(norm_g),
                  pl.BlockSpec(w_in_t.shape, lambda i: (0, 0), pipeline_mode=pl.Buffered(1))],
        out_specs=tiles.merged(D_PROJ),
        out_shape=jax.ShapeDtypeStruct((x_ctx.shape[0] + x_dec.shape[0], D_PROJ), F32),
        scratch_shapes=[pltpu.VMEM((D_MODEL, D_PROJ), BF16)],
        compiler_params=_cparams(1),
        name="in_proj",
    )(x_ctx, x_dec, mod, norm_g, w_in_t)


def _time_masks(reverse):
    r = _iota((2 * CHUNK, 2 * CHUNK), 0) % CHUNK
    c = _iota((2 * CHUNK, 2 * CHUNK), 1) % CHUNK
    if reverse:
        return r < c, r <= c
    return r > c, r >= c


def _cumsum_matrix(reverse):
    r = _iota((CHUNK, CHUNK), 0)
    c = _iota((CHUNK, CHUNK), 1)
    tri = (r <= c) if reverse else (r >= c)
    return tri.astype(BF16)


def _stack_heads(x, half):
    m0 = _iota(x.shape, 1) < half
    return jnp.concatenate([jnp.where(m0, x, 0.0), jnp.where(m0, 0.0, x)], axis=0)


def _finished_before_last_scan(groups, n_chunks):
    done_fwd, done_bwd = set(), set()
    for s, fwd, bwd in groups[:-1]:
        done_fwd.update(s * n_chunks + c for c in fwd)
        done_bwd.update(s * n_chunks + c for c in bwd)
    return sorted(done_fwd & done_bwd)


def _head_sums(x):
    parts = []
    for p in range(x.shape[1] // LANE):
        xp = x[:, p * LANE:(p + 1) * LANE]
        m0 = _iota(xp.shape, 1) < 64
        s0 = jnp.sum(jnp.where(m0, xp, 0.0), axis=-1, keepdims=True)
        s1 = jnp.sum(jnp.where(m0, 0.0, xp), axis=-1, keepdims=True)
        parts.append(jnp.where(m0, s0, s1))
    return parts[0] if len(parts) == 1 else jnp.concatenate(parts, axis=1)


def _rwkv_chunk_terms(insts, interleaved=()):
    c = CHUNK
    step_row = _iota((c, LANE), 0)
    step_col = _iota((c, LANE), 1) % c
    eye_w = (step_row == step_col).astype(F32)
    same_head = (_iota((LANE, LANE), 0) // 64) == (_iota((LANE, LANE), 1) // 64)
    stack_bf = lambda x: _stack_heads(x, 64).astype(BF16)
    cums = [_mm_01_lhs(_cumsum_matrix(rev), lw) for (_, lw, _, _, _, _, rev) in insts]
    pre = []
    for (r, lw, kd, a, b, v, rev), cum in zip(insts, cums):
        end = cum[0:1] if rev else cum[c - 1:c]
        inv_w = jnp.exp(-cum)
        rem_w = jnp.exp(end - cum)
        a_t = a * jnp.exp(cum - lw)
        r_t = r * jnp.exp(cum)
        bk_s = jnp.concatenate([stack_bf(b * inv_w), stack_bf(kd * inv_w)], axis=0)
        bkh = jnp.concatenate([b * rem_w, kd * rem_w], axis=0).astype(BF16)
        pre.append((a_t, r_t, bk_s, bkh, v, jnp.exp(end)))
    ms = [_dot(jnp.concatenate([a_t, r_t], axis=0).astype(BF16), bk_s, _NT) for (a_t, r_t, bk_s, _, _, _) in pre]
    mats = []
    for m, (_, _, _, _, _, _, rev) in zip(ms, insts):
        strict = (step_row < step_col) if rev else (step_row > step_col)
        incl = (step_row <= step_col) if rev else (step_row >= step_col)
        l_ab = jnp.where(strict, m[0:c, 0:LANE], 0.0)
        l_akrk = jnp.concatenate([jnp.where(strict, m[0:c, LANE:2 * LANE], 0.0),
                                  jnp.where(incl, m[c:2 * c, LANE:2 * LANE], 0.0)], axis=0).astype(BF16)
        m_rb = jnp.where(incl, m[c:2 * c, 0:LANE], 0.0).astype(BF16)
        mats.append((l_ab, l_akrk, m_rb))
    pending = list(interleaved)

    def run_interleaved():
        if pending:
            pending.pop(0)()

    lvs = [_dot(l_akrk, stack_bf(pr[4])) for (_, l_akrk, _), pr in zip(mats, pre)]
    run_interleaved()
    ps = [eye_w + l_ab for (l_ab, _, _) in mats]
    lps = [_dot(l_ab.astype(BF16), stack_bf(l_ab)) for (l_ab, _, _) in mats]
    for level in range(1, 6):
        if level < 5:
            xs = [_dot(lp.astype(BF16), jnp.concatenate([stack_bf(p), stack_bf(lp)], axis=1))
                  for lp, p in zip(lps, ps)]
            ps = [p + x[:, 0:LANE] for p, x in zip(ps, xs)]
            lps = [x[:, LANE:2 * LANE] for x in xs]
        else:
            ps = [p + _dot(lp.astype(BF16), stack_bf(p)) for lp, p in zip(lps, ps)]
        if level in (2, 4):
            run_interleaved()
    pxs = [_dot(p.astype(BF16), jnp.concatenate([stack_bf(pr[0]), stack_bf(lv[0:c])], axis=1))
           for p, pr, lv in zip(ps, pre, lvs)]
    run_interleaved()
    mzs = [_dot(mt[2], jnp.concatenate([stack_bf(px[:, 0:LANE]), stack_bf(px[:, LANE:2 * LANE])], axis=1))
           for mt, px in zip(mats, pxs)]
    ts = [_dot(px[:, 0:LANE].astype(BF16), pr[3][0:c], _TN) for px, pr in zip(pxs, pre)]
    gs = [_dot(jnp.concatenate([px[:, LANE:2 * LANE], pr[4]], axis=0).astype(BF16), pr[3], _TN)
          for px, pr in zip(pxs, pre)]
    while pending:
        run_interleaved()
    out = []
    for pr, lv, mz, t, g in zip(pre, lvs, mzs, ts, gs):
        q = pr[1] + mz[:, 0:LANE]
        y0 = mz[:, LANE:2 * LANE] + lv[c:2 * c]
        g_wide = jnp.where(_iota((c, LANE), 1) < 64, g[0:c], g[c:2 * c])
        out.append((jnp.where(same_head, t, 0.0).astype(BF16), g_wide, pr[5],
                    q.astype(BF16), y0))
    return out


def _gla_chunk_terms(insts, interleaved=()):
    c = CHUNK
    step_row = _iota((c, LANE), 0)
    step_col = _iota((c, LANE), 1) % c
    same_head = (_iota((LANE, 2 * LANE), 0) // 64) == (_iota((LANE, 2 * LANE), 1) // LANE)
    cums = [_mm_01_lhs(_cumsum_matrix(rev), g) for (_, _, _, g, rev) in insts]
    pre = []
    for (q, k, v, g, rev), cum in zip(insts, cums):
        end = cum[0:1] if rev else cum[c - 1:c]
        qt = (q * jnp.exp(cum)).astype(BF16)
        k_s = _stack_heads(k * jnp.exp(-cum), 64).astype(BF16)
        kh = (k * jnp.exp(end - cum)).astype(BF16)
        a_col = jnp.broadcast_to(jnp.exp(end), (LANE, LANE)).T
        pre.append((qt, k_s, kh, v.astype(BF16), _stack_heads(v, LANE).astype(BF16), a_col))
    pending = list(interleaved)

    def run_interleaved():
        if pending:
            pending.pop(0)()

    run_interleaved()
    atts = [_dot(pr[0], pr[1], _NT) for pr in pre]
    run_interleaved()
    atts = [jnp.where((step_row <= step_col) if inst[4] else (step_row >= step_col), att, 0.0).astype(BF16)
            for att, inst in zip(atts, insts)]
    o0s = [_dot(att, pr[4]) for att, pr in zip(atts, pre)]
    run_interleaved()
    kvs = [jnp.where(same_head, _dot(pr[2], pr[3], _TN), 0.0) for pr in pre]
    while pending:
        run_interleaved()
    return [(pr[0], o0, pr[5], kv) for pr, o0, kv in zip(pre, o0s, kvs)]


def _rwkv_kernel(seq_len, nseq, is_grid, zero_init, r_ref, k_ref, v_ref, lora_ref, cwr_ref, cwk_ref, cwv_ref,
                 w2_ref, a2_ref, g2_ref, vec_ref, *rest):
    s0_ref = None if zero_init else rest[0]
    (y_ref, sout_ref, pad_ref, rs_ref, ks_ref, vs_ref, kk_ref, bonus_ref, gate_ref, yf_ref, yb_ref, st_ref, tt_ref,
     tg_ref, tw_ref, tq_ref, ty_ref) = rest[0 if zero_init else 1:][:17]
    left_ref, right_ref = rest[-2:] if is_grid else (None, None)
    n_chunks = seq_len // CHUNK
    npp = RW_PAIRS_PER_STEP
    w = npp * LANE
    pair = lambda x, p: x[:, p * LANE:(p + 1) * LANE]
    vec = vec_ref[...]
    w0 = (vec[0:1], vec[1:2])
    a0 = (vec[2:3], vec[3:4])
    k_k, k_a, r_k, ln_w, ln_b = vec[4:5], vec[5:6], vec[6:7], vec[7:8], vec[8:9]
    block_sum = _head_sums
    chains = [(d, p) for d in range(2) for p in range(npp)]
    rows_of = lambda s, c: slice((s * n_chunks + c) * CHUNK, (s * n_chunks + c + 1) * CHUNK)

    if not is_grid:
        zeros = jnp.zeros((CONV_PAD, 3 * w), F32)
        pad_ref[0:CONV_PAD, :] = zeros
        pad_ref[CONV_PAD + seq_len:2 * CONV_PAD + seq_len, :] = zeros
    cw = jnp.concatenate([cwr_ref[...], cwk_ref[...], cwv_ref[...]], axis=1)
    col = _iota((CHUNK, 3 * w), 0)
    rkv_rows = lambda rows: jnp.concatenate([r_ref[rows, :], k_ref[rows, :], v_ref[rows, :]], axis=1)

    def shift_body(c, carry):
        rows = pl.ds(pl.multiple_of(c * CHUNK, CHUNK), CHUNK)
        x = rkv_rows(rows)
        left_ref[rows, :] = jnp.where(col >= 1, pltpu.roll(x, 1, 0), 0.0)
        right_ref[rows, :] = jnp.where(col <= GRID_W - 2, pltpu.roll(x, CHUNK - 1, 0), 0.0)
        return carry

    def conv_chunk(c):
        base = CONV_PAD + c * CHUNK
        acc = jnp.zeros((CHUNK, 3 * w), F32)
        if is_grid:
            for di in (-1, 0, 1):
                if not 0 <= c + di < n_chunks:
                    continue
                rows = slice((c + di) * CHUNK, (c + di + 1) * CHUNK)
                for dj, src in ((-1, left_ref[rows, :]), (0, rkv_rows(rows)), (1, right_ref[rows, :])):
                    tap = (di + 1) * 3 + dj + 1
                    acc = acc + src * cw[tap:tap + 1]
        else:
            win = pad_ref[pl.ds(base - 8, CHUNK + 16), :]
            for dj in (-1, 0, 1):
                acc = acc + win[8 + dj:8 + dj + CHUNK] * cw[4 + dj:5 + dj]
        return acc[:, 0:w], acc[:, w:2 * w], acc[:, 2 * w:3 * w]

    def load_sequence(s):
        seq_rows = slice(s * seq_len, (s + 1) * seq_len)
        pad_ref[CONV_PAD:CONV_PAD + seq_len, 0:w] = r_ref[seq_rows, :]
        pad_ref[CONV_PAD:CONV_PAD + seq_len, w:2 * w] = k_ref[seq_rows, :]
        pad_ref[CONV_PAD:CONV_PAD + seq_len, 2 * w:3 * w] = v_ref[seq_rows, :]

    def conv_store(s, c):
        rows = rows_of(s, c)
        rc, kc, vc = conv_chunk(c)
        kk = kc * k_k
        rs_ref[rows, :] = rc
        ks_ref[rows, :] = kc
        vs_ref[rows, :] = vc
        kk_ref[rows, :] = kk * lax.rsqrt(block_sum(kk * kk) + EPS)
        bonus_ref[rows, :] = block_sum(rc * kc * r_k) * vc
        gate_ref[rows, :] = _mm(_sigmoid(lora_ref[rows, 2 * LANE:3 * LANE]), g2_ref[...])

    for s in range(nseq):
        for d, p in chains:
            if zero_init:
                st_ref[s, d, p] = jnp.zeros((64, LANE), F32)
            else:
                st_ref[s, d, p] = jnp.concatenate([s0_ref[s, d, 2 * p], s0_ref[s, d, 2 * p + 1]], axis=1)

    if nseq == 1:
        groups = [(0, list(range(g * TERM_UNROLL, (g + 1) * TERM_UNROLL)),
                   list(range(n_chunks - 1 - g * TERM_UNROLL, n_chunks - 1 - (g + 1) * TERM_UNROLL, -1)))
                  for g in range(n_chunks // TERM_UNROLL)]
    else:
        groups = [(s, list(range(n_chunks)), list(range(n_chunks - 1, -1, -1))) for s in range(nseq)]

    def group_terms(group, interleaved):
        s, fwd, bwd = group
        insts, where, lora_in = [], [], {}
        for d, chunks in ((0, fwd), (1, bwd)):
            for c in chunks:
                rows = rows_of(s, c)
                if c not in lora_in:
                    lora_in[c] = (jnp.tanh(lora_ref[rows, 0:LANE]).astype(BF16),
                                  lora_ref[rows, LANE:2 * LANE].astype(BF16))
                lw = -RW_DECAY_SCALE * _sigmoid(w0[d] + _dot(lora_in[c][0], w2_ref[d]))
                ag = _sigmoid(a0[d] + _dot(lora_in[c][1], a2_ref[d]))
                rc, kc, vc, kk = rs_ref[rows, :], ks_ref[rows, :], vs_ref[rows, :], kk_ref[rows, :]
                kd = kc * (1.0 + (ag - 1.0) * k_a)
                kb = kk * ag
                for p in range(npp):
                    insts.append((pair(rc, p), pair(lw, p), pair(kd, p), -pair(kk, p), pair(kb, p), pair(vc, p),
                                  d == 1))
                    where.append((d, p, s * n_chunks + c))
        for (d, p, gc), (t, g, w_end, q, y0) in zip(where, _rwkv_chunk_terms(insts, interleaved)):
            tt_ref[d, p, gc] = t
            tg_ref[d, p, gc] = g
            tw_ref[d, p, gc] = jnp.broadcast_to(w_end, (8, LANE))
            tq_ref[d, p, gc] = q
            ty_ref[d, p, gc] = y0

    def scan_step(s, chunk_of_dir):
        gcs = [s * n_chunks + chunk_of_dir[d] for d, _ in chains]
        ss = [st_ref[s, d, p] for d, p in chains]
        ys = [_dot(tq_ref[d, p, gc], _stack_heads(x, 64).astype(BF16), _NT) + ty_ref[d, p, gc]
              for (d, p), gc, x in zip(chains, gcs, ss)]
        sn = [x * tw_ref[d, p, gc][0:1] + _dot(x.astype(BF16), tt_ref[d, p, gc]) + tg_ref[d, p, gc]
              for (d, p), gc, x in zip(chains, gcs, ss)]
        for (d, p), gc, y, x in zip(chains, gcs, ys, sn):
            st_ref[s, d, p] = x
            out_ref = yf_ref if d == 0 else yb_ref
            out_ref[gc * CHUNK:(gc + 1) * CHUNK, p * LANE:(p + 1) * LANE] = y

    def group_scan(group):
        s, fwd, bwd = group
        return [functools.partial(scan_step, s, (cf, cb)) for cf, cb in zip(fwd, bwd)]

    conv_done, seq_loaded = set(), set()

    def group_conv(group):
        s, fwd, bwd = group
        todo = [c for c in sorted(set(fwd) | set(bwd)) if (s, c) not in conv_done]
        conv_done.update((s, c) for c in todo)
        thunks = []
        for i in range(0, len(todo), max(1, -(-len(todo) // TERM_UNROLL))):
            part = todo[i:i + max(1, -(-len(todo) // TERM_UNROLL))]
            need_load = s not in seq_loaded
            seq_loaded.add(s)

            def run(part=part, need_load=need_load):
                if need_load:
                    assert not is_grid or nseq == 1
                    load_sequence(s)
                for c in part:
                    conv_store(s, c)

            thunks.append(run)
        return thunks

    def merge(*lists):
        n = max(len(lst) for lst in lists)
        pick = lambda lst, i: lst[i] if i < len(lst) else (lambda: None)
        return [lambda i=i: [pick(lst, i)() for lst in lists] for i in range(n)]

    def post(offs):
        ys = [yf_ref[pl.ds(off, CHUNK), :] + yb_ref[pl.ds(off, CHUNK), :] for off in offs]
        mus = [block_sum(y) * (1.0 / 64) for y in ys]
        dlts = [y - mu for y, mu in zip(ys, mus)]
        vrs = [block_sum(dlt * dlt) * (1.0 / 64) for dlt in dlts]
        for off, dlt, var in zip(offs, dlts, vrs):
            yn = dlt * lax.rsqrt(var + RW_LN_EPS) * ln_w + ln_b
            y = (yn + bonus_ref[pl.ds(off, CHUNK), :]) * gate_ref[pl.ds(off, CHUNK), :]
            y_ref[pl.ds(off, CHUNK), :] = y.astype(y_ref.dtype)

    def sequence_post(s):
        return [functools.partial(post, [(s * n_chunks + c) * CHUNK]) for c in range(n_chunks)]

    if is_grid:
        assert nseq == 1
        seq_loaded.add(0)
        lax.fori_loop(0, n_chunks, shift_body, 0)
    for thunk in group_conv(groups[0]):
        thunk()
    posted = set()
    for g, group in enumerate(groups):
        scans = group_scan(groups[g - 1]) if g else []
        convs = group_conv(groups[g + 1]) if g + 1 < len(groups) else []
        posts = []
        if nseq > 1 and g >= 2:
            s_done = groups[g - 2][0]
            posts = sequence_post(s_done)
            posted.update(range(s_done * n_chunks, (s_done + 1) * n_chunks))
        group_terms(group, merge(scans, convs, posts))
    early = [gc for gc in _finished_before_last_scan(groups, n_chunks) if gc not in posted]
    last_scan = group_scan(groups[-1])
    per_step = max(1, -(-len(early) // len(last_scan)))
    early_posts = [functools.partial(post, [gc * CHUNK for gc in early[i:i + per_step]])
                   for i in range(0, len(early), per_step)]
    for step, extra in itertools.zip_longest(last_scan, early_posts, fillvalue=lambda: None):
        step()
        extra()
    posted.update(early)
    for s in range(nseq):
        for d, p in chains:
            x = st_ref[s, d, p]
            sout_ref[s, d, 2 * p] = x[:, 0:64]
            sout_ref[s, d, 2 * p + 1] = x[:, 64:LANE]

    left = [gc for gc in range(nseq * n_chunks) if gc not in posted]
    for i in range(0, len(left), TERM_UNROLL):
        post([gc * CHUNK for gc in left[i:i + TERM_UNROLL]])


def _rwkv_mixer(proj, first_seq, n_seq, seq_len, is_grid, prm, s0):
    n_pairs = RW_W // LANE
    n_chunks = seq_len // CHUNK
    npp = RW_PAIRS_PER_STEP
    w = npp * LANE
    nseq = max(1, RW_GROUPS * TERM_UNROLL // n_chunks)
    assert n_chunks % TERM_UNROLL == 0 and n_pairs % npp == 0
    assert (n_chunks == TERM_UNROLL or nseq == 1) and n_seq % nseq == 0 and first_seq % nseq == 0
    rows = nseq * seq_len
    total_chunks = nseq * n_chunks
    first = first_seq // nseq
    col = lambda cb: (lambda b, p: (b + first, cb // npp + p))
    par = lambda cb: (lambda b, p: (0, cb // npp + p))
    state_spec = pl.BlockSpec((nseq, 2, 2 * npp, 64, 64), lambda b, p: (b, 0, p, 0, 0))
    kernel = functools.partial(_rwkv_kernel, seq_len, nseq, is_grid, s0 is None)
    y, s_out = pl.pallas_call(
        kernel,
        grid=(n_seq // nseq, n_pairs // npp),
        in_specs=[pl.BlockSpec((rows, w), col(CB_R)),
                  pl.BlockSpec((rows, w), col(CB_K)),
                  pl.BlockSpec((rows, w), col(CB_V)),
                  pl.BlockSpec((rows, 3 * LANE), lambda b, p: (b + first, CB_LORA // 3)),
                  pl.BlockSpec((9, w), par(CB_R)),
                  pl.BlockSpec((9, w), par(CB_K)),
                  pl.BlockSpec((9, w), par(CB_V)),
                  pl.BlockSpec((2, LANE, w), lambda b, p: (0, 0, p)),
                  pl.BlockSpec((2, LANE, w), lambda b, p: (0, 0, p)),
                  pl.BlockSpec((LANE, w), lambda b, p: (0, p)),
                  pl.BlockSpec((16, w), lambda b, p: (0, p))] + ([] if s0 is None else [state_spec]),
        out_specs=[pl.BlockSpec((rows, w), lambda b, p: (b, p)), state_spec],
        out_shape=[jax.ShapeDtypeStruct((n_seq * seq_len, RW_W), BF16),
                   jax.ShapeDtypeStruct((n_seq, 2, 2 * n_pairs, 64, 64), F32)],
        scratch_shapes=[pltpu.VMEM((8, LANE) if is_grid else (seq_len + 2 * CONV_PAD, 3 * w), F32)]
                       + [pltpu.VMEM((rows, w), F32)] * 8
                       + [pltpu.VMEM((nseq, 2, npp, 64, LANE), F32),
                          pltpu.VMEM((2, npp, total_chunks, LANE, LANE), BF16),
                          pltpu.VMEM((2, npp, total_chunks, 64, LANE), F32),
                          pltpu.VMEM((2, npp, total_chunks, 8, LANE), F32),
                          pltpu.VMEM((2, npp, total_chunks, CHUNK, LANE), BF16),
                          pltpu.VMEM((2, npp, total_chunks, CHUNK, LANE), F32)]
                       + ([pltpu.VMEM((seq_len, 3 * w), F32)] * 2 if is_grid else []),
        compiler_params=_cparams(2),
        name="rwkv_mixer",
    )(proj, proj, proj, proj, prm['conv'], prm['conv'], prm['conv'], prm['w2p'], prm['a2p'], prm['g2'],
      prm['vec'], *([] if s0 is None else [s0]))
    return y, s_out


def _gla_kernel(seq_len, nseq, zero_init, q_ref, k_ref, v_ref, og_ref, lgk_ref, gk2_ref, gvec_ref, *rest):
    s0_ref = None if zero_init else rest[0]
    y_ref, sout_ref, of_ref, ob_ref, st_ref, tq_ref, to_ref, ta_ref, tkv_ref = rest[0 if zero_init else 1:]
    n_chunks = seq_len // CHUNK
    npp = GLA_PAIRS
    gvec = gvec_ref[...]
    chains = [(d, p) for d in range(2) for p in range(npp)]

    for s in range(nseq):
        for d, p in chains:
            if zero_init:
                st_ref[s, d, p] = jnp.zeros((LANE, 2 * LANE), F32)
            else:
                z = jnp.zeros((64, LANE), F32)
                st_ref[s, d, p] = jnp.concatenate([jnp.concatenate([s0_ref[s, d, 2 * p], z], axis=1),
                                                   jnp.concatenate([z, s0_ref[s, d, 2 * p + 1]], axis=1)], axis=0)

    if nseq == 1:
        groups = [(0, list(range(g * TERM_UNROLL, (g + 1) * TERM_UNROLL)),
                   list(range(n_chunks - 1 - g * TERM_UNROLL, n_chunks - 1 - (g + 1) * TERM_UNROLL, -1)))
                  for g in range(n_chunks // TERM_UNROLL)]
    else:
        groups = [(s, list(range(n_chunks)), list(range(n_chunks - 1, -1, -1))) for s in range(nseq)]

    def group_terms(group, interleaved):
        s, fwd, bwd = group
        insts, where, lgk_in = [], [], {}
        for d, chunks in ((0, fwd), (1, bwd)):
            for c in chunks:
                gc = s * n_chunks + c
                rows = slice(gc * CHUNK, (gc + 1) * CHUNK)
                if c not in lgk_in:
                    lgk_in[c] = lgk_ref[rows, :].astype(BF16)
                x = _dot(lgk_in[c], gk2_ref[d]) + gvec[d:d + 1, 0:GLA_QK_W]
                g = _log_sigmoid(x) * (1.0 / GLA_GATE_NORM)
                qc = q_ref[rows, :] * GLA_Q_SCALE
                kc = k_ref[rows, :]
                vc = v_ref[rows, :]
                for p in range(npp):
                    qk = slice(p * LANE, (p + 1) * LANE)
                    insts.append((qc[:, qk], kc[:, qk], vc[:, 2 * p * LANE:2 * (p + 1) * LANE], g[:, qk], d == 1))
                    where.append((d, p, gc))
        for (d, p, gc), (qt, o0, a_col, kv) in zip(where, _gla_chunk_terms(insts, interleaved)):
            tq_ref[d, p, gc] = qt
            to_ref[d, p, gc] = o0
            ta_ref[d, p, gc] = a_col
            tkv_ref[d, p, gc] = kv

    def scan_step(s, chunk_of_dir):
        gcs = [s * n_chunks + chunk_of_dir[d] for d, _ in chains]
        ss = [st_ref[s, d, p] for d, p in chains]
        os_ = [_dot(tq_ref[d, p, gc], x.astype(BF16)) + to_ref[d, p, gc] for (d, p), gc, x in zip(chains, gcs, ss)]
        for (d, p), gc, x, o in zip(chains, gcs, ss, os_):
            a_col = ta_ref[d, p, gc]
            st_ref[s, d, p] = x * jnp.concatenate([a_col, a_col], axis=1) + tkv_ref[d, p, gc]
            out_ref = of_ref if d == 0 else ob_ref
            out_ref[gc * CHUNK:(gc + 1) * CHUNK, 2 * p * LANE:2 * (p + 1) * LANE] = o

    def group_scan(group):
        s, fwd, bwd = group
        return [functools.partial(scan_step, s, (cf, cb)) for cf, cb in zip(fwd, bwd)]

    def post(off):
        for h in range(2 * npp):
            hs = slice(h * LANE, (h + 1) * LANE)
            o = of_ref[pl.ds(off, CHUNK), hs] + ob_ref[pl.ds(off, CHUNK), hs]
            gate = _silu(og_ref[pl.ds(off, CHUNK), hs])
            y = _rmsnorm_rows(o) * gvec[2:3, hs] * gate
            y_ref[pl.ds(off, CHUNK), hs] = y.astype(y_ref.dtype)

    posted = set()
    for g, group in enumerate(groups):
        thunks = group_scan(groups[g - 1]) if g else []
        if nseq > 1 and g >= 2:
            s_done = groups[g - 2][0]
            posts = [functools.partial(post, (s_done * n_chunks + c) * CHUNK) for c in range(n_chunks)]
            thunks = [lambda a=a, b=b: (a(), b()) for a, b in zip(thunks, posts)]
            posted.update(range(s_done * n_chunks, (s_done + 1) * n_chunks))
        group_terms(group, thunks)
    early = [gc for gc in _finished_before_last_scan(groups, n_chunks) if gc not in posted]
    last_scan = group_scan(groups[-1])
    per_step = max(1, -(-len(early) // len(last_scan)))
    for i, step in enumerate(last_scan):
        step()
        for gc in early[i * per_step:(i + 1) * per_step]:
            post(gc * CHUNK)
    for gc in early[len(last_scan) * per_step:]:
        post(gc * CHUNK)
    posted.update(early)
    for s in range(nseq):
        for d, p in chains:
            x = st_ref[s, d, p]
            sout_ref[s, d, 2 * p] = x[0:64, 0:LANE]
            sout_ref[s, d, 2 * p + 1] = x[64:LANE, LANE:2 * LANE]
    for gc in range(nseq * n_chunks):
        if gc not in posted:
            post(gc * CHUNK)


def _gla_mixer(proj, first_seq, n_seq, seq_len, prm, s0):
    npp = GLA_PAIRS
    n_heads = 2 * npp
    n_chunks = seq_len // CHUNK
    nseq = max(1, RW_GROUPS * TERM_UNROLL // n_chunks)
    assert n_chunks % TERM_UNROLL == 0
    assert (n_chunks == TERM_UNROLL or nseq == 1) and n_seq % nseq == 0 and first_seq % nseq == 0
    rows = nseq * seq_len
    total_chunks = nseq * n_chunks
    first = first_seq // nseq
    state_spec = pl.BlockSpec((nseq, 2, n_heads, 64, LANE), lambda b: (b, 0, 0, 0, 0))
    kernel = functools.partial(_gla_kernel, seq_len, nseq, s0 is None)
    y, s_out = pl.pallas_call(
        kernel,
        grid=(n_seq // nseq,),
        in_specs=[pl.BlockSpec((rows, GLA_QK_W), lambda b: (b + first, CB_GQ * LANE // GLA_QK_W)),
                  pl.BlockSpec((rows, GLA_QK_W), lambda b: (b + first, CB_GK * LANE // GLA_QK_W)),
                  pl.BlockSpec((rows, GLA_V_W), lambda b: (b + first, CB_GV * LANE // GLA_V_W)),
                  pl.BlockSpec((rows, GLA_V_W), lambda b: (b + first, CB_OG * LANE // GLA_V_W)),
                  pl.BlockSpec((rows, LANE), lambda b: (b + first, CB_LGK)),
                  pl.BlockSpec((2, LANE, GLA_QK_W), lambda b: (0, 0, 0)),
                  pl.BlockSpec((8, GLA_V_W), lambda b: (0, 0))] + ([] if s0 is None else [state_spec]),
        out_specs=[pl.BlockSpec((rows, GLA_V_W), lambda b: (b, 0)), state_spec],
        out_shape=[jax.ShapeDtypeStruct((n_seq * seq_len, GLA_V_W), BF16),
                   jax.ShapeDtypeStruct((n_seq, 2, n_heads, 64, LANE), F32)],
        scratch_shapes=[pltpu.VMEM((rows, GLA_V_W), F32)] * 2
                       + [pltpu.VMEM((nseq, 2, npp, LANE, 2 * LANE), F32),
                          pltpu.VMEM((2, npp, total_chunks, CHUNK, LANE), BF16),
                          pltpu.VMEM((2, npp, total_chunks, CHUNK, 2 * LANE), F32),
                          pltpu.VMEM((2, npp, total_chunks, LANE, LANE), F32),
                          pltpu.VMEM((2, npp, total_chunks, LANE, 2 * LANE), F32)],
        compiler_params=_cparams(1),
        name="gla_mixer",
    )(proj, proj, proj, proj, proj, prm['gk2p'], prm['gvec'], *([] if s0 is None else [s0]))
    return y, s_out


OUT_TM = 512
ROUTE_NEG = -1e30
LANE_GROUP0 = N_EXPERTS


def _route(logits):
    lane = _iota(logits.shape, 1)
    lane_f = lane.astype(F32)
    big = float(LANE)
    is_g = (lane >= LANE_GROUP0) & (lane < LANE_GROUP0 + 4)
    gmax = jnp.max(jnp.where(is_g, logits, ROUTE_NEG), axis=-1, keepdims=True)
    gidx = jnp.min(jnp.where(is_g & (logits == gmax), lane_f, big), axis=-1, keepdims=True) - LANE_GROUP0
    gsum = jnp.sum(jnp.where(is_g, jnp.exp(jnp.minimum(logits - gmax, 0.0)), 0.0), axis=-1, keepdims=True)
    g_w = 1.0 / gsum
    in_grp = (lane < N_EXPERTS) & ((lane // 4).astype(F32) == gidx)
    m1 = jnp.max(jnp.where(in_grp, logits, ROUTE_NEG), axis=-1, keepdims=True)
    i1 = jnp.min(jnp.where(in_grp & (logits == m1), lane_f, big), axis=-1, keepdims=True)
    rest = in_grp & (lane_f != i1)
    m2 = jnp.max(jnp.where(rest, logits, ROUTE_NEG), axis=-1, keepdims=True)
    i2 = jnp.min(jnp.where(rest & (logits == m2), lane_f, big), axis=-1, keepdims=True)
    t = jnp.exp(m2 - m1)
    w1 = g_w / (1.0 + t)
    return jnp.where(lane_f == i1, w1, 0.0) + jnp.where(lane_f == i2, w1 * t, 0.0)


def _outproj_kernel(tiles, yrc_ref, yrd_ref, ygc_ref, ygd_ref, xc_ref, xd_ref, mod_ref, wo_ref, g_ref, wr_ref,
                    br_ref, x1_ref, h2_ref, cmb_ref):
    def run(yr_ref, yg_ref, x_ref):
        m = mod_ref[0]
        mix = _mm(yr_ref[...], wo_ref[0:RW_W, :]) + _mm(yg_ref[...], wo_ref[RW_W:RW_W + GLA_V_W, :])
        x1 = x_ref[...] + m[2:3] * mix
        h2 = _rmsnorm_rows(x1) * g_ref[...] * (1.0 + m[4:5]) + m[3:4]
        x1_ref[...] = x1
        h2_ref[...] = h2.astype(BF16)
        h_hi, h_lo = _split2(h2)
        both = _dot(h_hi, wr_ref[...])
        logits = both[:, 0:LANE] + both[:, LANE:2 * LANE] + _dot(h_lo, wr_ref[:, 0:LANE])
        cmb_ref[...] = _route(logits + br_ref[...])

    tiles.by_pass(functools.partial(run, yrc_ref, ygc_ref, xc_ref), functools.partial(run, yrd_ref, ygd_ref, xd_ref))


def _out_projection(tiles, y_rw, y_gla, x, mod, w_out, norm_g, w_route, b_route):
    n = x[0].shape[0] + x[1].shape[0]
    full = lambda a: pl.BlockSpec(a.shape, lambda i: (0,) * a.ndim)
    return pl.pallas_call(
        functools.partial(_outproj_kernel, tiles),
        grid=(tiles.n_ctx + tiles.n_dec,),
        in_specs=[*tiles.specs(RW_W), *tiles.specs(GLA_V_W), *tiles.specs(D_MODEL), tiles.mod_spec(),
                  full(w_out), full(norm_g), full(w_route), full(b_route)],
        out_specs=[tiles.merged(D_MODEL), tiles.merged(D_MODEL), tiles.merged(LANE)],
        out_shape=[jax.ShapeDtypeStruct((n, D_MODEL), F32), jax.ShapeDtypeStruct((n, D_MODEL), BF16),
                   jax.ShapeDtypeStruct((n, LANE), F32)],
        compiler_params=_cparams(1),
        name="out_proj_router",
    )(*y_rw, *y_gla, *x, mod, w_out, norm_g, w_route, b_route)


MOE_TM = 512
MOE_RB = 128
MOE_INTERLEAVE = 4
SLOT_ALIGN = 16
MOE_SLOTS = 2 * MOE_TM + N_EXPERTS * SLOT_ALIGN


def _stage_expert_weights(srcs_hbm, dst_refs, stage_refs, sems):
    def copies(e):
        return [pltpu.make_async_copy(src.at[e], stage.at[e % 2], sem.at[e % 2])
                for src, stage, sem in zip(srcs_hbm, stage_refs, sems)]

    for c in copies(0):
        c.start()
    for e in range(N_EXPERTS):
        if e + 1 < N_EXPERTS:
            for c in copies(e + 1):
                c.start()
        for c, dst, stage in zip(copies(e), dst_refs, stage_refs):
            c.wait()
            dst[e] = stage[e % 2].astype(BF16)


def _moe_kernel(tiles, h2_ref, cmb_ref, x1_ref, mod_ref, w1_hbm, w3_hbm, w2_hbm, fg_ref, yc_ref, yd_ref,
                xs_ref, ys_ref, w1_ref, w3_ref, w2_ref, stage1_ref, stage3_ref, stage2_ref, sem1, sem3, sem2):
    @pl.when(pl.program_id(0) == 0)
    def _():
        _stage_expert_weights((w1_hbm, w3_hbm, w2_hbm), (w1_ref, w3_ref, w2_ref),
                              (stage1_ref, stage3_ref, stage2_ref), (sem1, sem3, sem2))

    cmb = cmb_ref[...]
    lane = _iota(cmb.shape, 1).astype(F32)
    sel = cmb > 0.0
    sel01 = jnp.where(sel, 1.0, 0.0).astype(BF16)
    before = (_iota((MOE_TM, MOE_TM), 0) > _iota((MOE_TM, MOE_TM), 1)).astype(BF16)
    pos = _dot(before, sel01)
    cnt = pos[MOE_TM - 1:MOE_TM] + sel01[MOE_TM - 1:MOE_TM].astype(F32)
    seg = jnp.floor((cnt + (SLOT_ALIGN - 1)) * (1.0 / SLOT_ALIGN))
    lower_experts = (_iota((LANE, LANE), 0) < _iota((LANE, LANE), 1)).astype(BF16)
    start = _dot(jnp.broadcast_to(seg, (8, LANE)).astype(BF16), lower_experts)[0:1] * SLOT_ALIGN
    n_blk = jnp.floor((cnt + (MOE_RB - 1)) * (1.0 / MOE_RB)).astype(jnp.int32)
    start_i = start.astype(jnp.int32)
    cnt_i = cnt.astype(jnp.int32)
    slot = start + pos
    e_a = jnp.min(jnp.where(sel, lane, float(LANE)), axis=-1, keepdims=True)
    e_b = jnp.max(jnp.where(sel, lane, -1.0), axis=-1, keepdims=True)
    pick = lambda e, x: jnp.sum(jnp.where(lane == e, x, 0.0), axis=-1, keepdims=True)
    slot_a, w_a = pick(e_a, slot), pick(e_a, cmb)
    slot_b = jnp.where(e_b != e_a, pick(e_b, slot), -1.0)
    w_b = pick(e_b, cmb)

    slots_t = jnp.where(lane == 0.0, slot_a, jnp.where(lane == 1.0, slot_b, -1.0)).T
    row_slot = _iota((MOE_SLOTS, MOE_TM), 0).astype(F32)
    gather = jnp.where((row_slot == slots_t[0:1]) | (row_slot == slots_t[1:2]), 1.0, 0.0).astype(BF16)
    xs_ref[...] = _dot(gather, h2_ref[...]).astype(BF16)
    ys_ref[...] = jnp.zeros_like(ys_ref)

    row_in_blk = _iota((MOE_RB, D_MODEL), 0)

    def expert_blocks(experts, r0s, ends):
        base = [pl.multiple_of(jnp.minimum(r0, MOE_SLOTS - MOE_RB), SLOT_ALIGN) for r0 in r0s]
        xbs = [xs_ref[pl.ds(b, MOE_RB), :] for b in base]
        gates = [_dot(xb, w3_ref[e]) for xb, e in zip(xbs, experts)]
        ups = [_dot(xb, w1_ref[e]) for xb, e in zip(xbs, experts)]
        acts = [(_silu(g) * u).astype(BF16) for g, u in zip(gates, ups)]
        outs = [_dot(a, w2_ref[e]) for a, e in zip(acts, experts)]
        for b, r0, end, out in zip(base, r0s, ends, outs):
            row = row_in_blk + b
            keep = (row >= end) | (row < r0)
            ys_ref[pl.ds(b, MOE_RB), :] = jnp.where(keep, ys_ref[pl.ds(b, MOE_RB), :], out.astype(BF16))

    seg_start = [pl.multiple_of(start_i[0, e], SLOT_ALIGN) for e in range(N_EXPERTS)]
    seg_end = [seg_start[e] + cnt_i[0, e] for e in range(N_EXPERTS)]
    for e0 in range(0, N_EXPERTS, MOE_INTERLEAVE):
        es = list(range(e0, e0 + MOE_INTERLEAVE))
        expert_blocks(es, [seg_start[e] for e in es], [seg_end[e] for e in es])
    for e in range(N_EXPERTS):
        def extra_block(b, carry, e=e):
            expert_blocks([e], [pl.multiple_of(seg_start[e] + b * MOE_RB, SLOT_ALIGN)], [seg_end[e]])
            return carry

        lax.fori_loop(1, n_blk[0, e], extra_block, 0)

    col_slot = _iota((MOE_TM, MOE_SLOTS), 1).astype(F32)
    scatter = (jnp.where(col_slot == slot_a, w_a, 0.0) + jnp.where(col_slot == slot_b, w_b, 0.0)).astype(BF16)
    x2 = x1_ref[...] + mod_ref[0][5:6] * _dot(scatter, ys_ref[...])
    y = _rmsnorm_rows(x2) * fg_ref[...]

    def write(y_ref):
        y_ref[...] = y

    tiles.by_pass(functools.partial(write, yc_ref), functools.partial(write, yd_ref))


def _moe(tiles, h2, cmb, x1, mod, w1, w3, w2, final_g):
    assert tiles.tm == MOE_TM
    hbm = pl.BlockSpec(memory_space=pl.ANY)
    out_ctx, out_dec = tiles.specs(D_MODEL)
    return pl.pallas_call(
        functools.partial(_moe_kernel, tiles),
        grid=(tiles.n_ctx + tiles.n_dec,),
        in_specs=[tiles.merged(D_MODEL), tiles.merged(LANE), tiles.merged(D_MODEL), tiles.mod_spec(),
                  hbm, hbm, hbm, pl.BlockSpec((1, D_MODEL), lambda i: (0, 0))],
        out_specs=[out_ctx, out_dec],
        out_shape=[jax.ShapeDtypeStruct((tiles.n_ctx * MOE_TM, D_MODEL), F32),
                   jax.ShapeDtypeStruct((tiles.n_dec * MOE_TM, D_MODEL), F32)],
        scratch_shapes=[pltpu.VMEM((MOE_SLOTS, D_MODEL), BF16), pltpu.VMEM((MOE_SLOTS, D_MODEL), BF16),
                        pltpu.VMEM((N_EXPERTS, D_MODEL, D_EXPERT), BF16),
                        pltpu.VMEM((N_EXPERTS, D_MODEL, D_EXPERT), BF16),
                        pltpu.VMEM((N_EXPERTS, D_EXPERT, D_MODEL), BF16),
                        pltpu.VMEM((2, D_MODEL, D_EXPERT), F32), pltpu.VMEM((2, D_MODEL, D_EXPERT), F32),
                        pltpu.VMEM((2, D_EXPERT, D_MODEL), F32)] + [pltpu.SemaphoreType.DMA((2,))] * 3,
        compiler_params=_cparams(1),
        name="moe_experts",
    )(h2, cmb, x1, mod, w1, w3, w2, final_g)


def _pad_rows(x, rows):
    return jnp.pad(x, ((0, rows - x.shape[0]),) + ((0, 0),) * (x.ndim - 1))


def _pack_params(l, w_in, rw_conv, rw_w0, rw_w2, rw_a0, rw_a2, rw_g2, rw_k_k, rw_k_a, rw_r_k, rw_ln_w, rw_ln_b,
                 gla_gk2, gla_gk_b, gla_norm_g, moe_w_group, moe_b_group, moe_w_expert, moe_b_expert):
    wi = w_in[l]
    z = lambda n: jnp.zeros((D_MODEL, n), F32)
    w_in_t = jnp.swapaxes(wi, 0, 1)
    z64 = jnp.zeros((64, RW_W), F32)
    w2p = jnp.stack([jnp.concatenate([rw_w2[l, 0], z64], 0), jnp.concatenate([z64, rw_w2[l, 1]], 0)])
    a2p = jnp.stack([jnp.concatenate([rw_a2[l, 0], z64], 0), jnp.concatenate([z64, rw_a2[l, 1]], 0)])
    vec = _pad_rows(jnp.stack([rw_w0[l, 0], rw_w0[l, 1], rw_a0[l, 0], rw_a0[l, 1], rw_k_k[l], rw_k_a[l],
                               rw_r_k[l].reshape(RW_W), rw_ln_w[l], rw_ln_b[l]]), 16)
    rw = {'conv': rw_conv[l].reshape(9, 3 * RW_W), 'w2p': w2p.astype(BF16), 'a2p': a2p.astype(BF16),
          'g2': rw_g2[l].astype(BF16), 'vec': vec}
    gk2p = jnp.stack([_pad_rows(gla_gk2[l, 0], LANE),
                      _pad_rows(jnp.concatenate([jnp.zeros((16, GLA_QK_W), F32), gla_gk2[l, 1]], 0), LANE)])
    gk_b = jnp.pad(gla_gk_b[l], ((0, 0), (0, GLA_V_W - GLA_QK_W)))
    gvec = _pad_rows(jnp.concatenate([gk_b, jnp.tile(gla_norm_g[l], GLA_V_W // LANE)[None]], axis=0), 8)
    gla = {'gk2p': gk2p.astype(BF16), 'gvec': gvec}
    w_route = jnp.concatenate(_split2(jnp.concatenate([moe_w_expert[l], moe_w_group[l], z(LANE - N_EXPERTS - 4)],
                                                      axis=1)), axis=1)
    b_route = jnp.concatenate([moe_b_expert[l], moe_b_group[l], jnp.zeros((LANE - N_EXPERTS - 4,), F32)])[None]
    return w_in_t, rw, gla, w_route, b_route


def kernel(x_prompt, x_sample, state_rwkv, state_gla, c, c_ctx, norm1_g, norm2_g, w_ada, b_ada, w_in, w_out,
           rw_conv, rw_w0, rw_w2, rw_a0, rw_a2, rw_g2, rw_k_k, rw_k_a, rw_r_k, rw_ln_w, rw_ln_b,
           gla_gk2, gla_gk_b, gla_norm_g, moe_w_group, moe_b_group, moe_w_expert, moe_b_expert,
           moe_w1, moe_w3, moe_w2, final_g):
    depth = w_in.shape[0]
    assert depth == 1, "the packed layout below handles the single-layer trunk of this problem"
    l = 0
    n_dec = x_sample.shape[0]
    ctx_row = n_dec
    cond8 = _pad_rows(jnp.concatenate([c, c_ctx[None]], axis=0), 8)
    mod = _modulation(cond8, w_ada[l], b_ada[l][None]).reshape(8, N_MOD, D_MODEL)
    pk = _pack_params(l, w_in, rw_conv, rw_w0, rw_w2, rw_a0, rw_a2, rw_g2, rw_k_k, rw_k_a, rw_r_k, rw_ln_w,
                      rw_ln_b, gla_gk2, gla_gk_b, gla_norm_g, moe_w_group, moe_b_group, moe_w_expert,
                      moe_b_expert)
    w_in_t, rw, gla, w_route, b_route = pk

    n_ctx, ctx_len, _ = x_prompt.shape
    dec_len = x_sample.shape[1]
    x_ctx = x_prompt.reshape(n_ctx * ctx_len, D_MODEL)
    x_dec = x_sample.reshape(n_dec * dec_len, D_MODEL)
    assert (n_ctx * ctx_len) % dec_len == 0, "denoising sequences must start on a dec_len row block of proj"
    first_dec = n_ctx * ctx_len // dec_len
    tiles = _Tiles(n_ctx * ctx_len, n_dec * dec_len, dec_len, ctx_row, PROJ_TM)
    assert PROJ_TM == OUT_TM == MOE_TM

    proj = _in_projection(tiles, x_ctx, x_dec, mod, norm1_g[l][None], w_in_t)
    y_rw_c, s_rw = _rwkv_mixer(proj, 0, n_ctx, ctx_len, False, rw, None)
    y_gla_c, s_gla = _gla_mixer(proj, 0, n_ctx, ctx_len, gla, None)
    y_rw_d, _ = _rwkv_mixer(proj, first_dec, n_dec, dec_len, True, rw, state_rwkv[:, l])
    y_gla_d, _ = _gla_mixer(proj, first_dec, n_dec, dec_len, gla, state_gla[:, l])
    x1, h2, cmb = _out_projection(tiles, (y_rw_c, y_rw_d), (y_gla_c, y_gla_d), (x_ctx, x_dec), mod,
                                  w_out[l].astype(BF16), norm2_g[l][None], w_route, b_route)
    y_ctx, y_dec = _moe(tiles, h2, cmb, x1, mod, moe_w1[l], moe_w3[l], moe_w2[l], final_g[None])
    return (y_ctx.reshape(x_prompt.shape), y_dec.reshape(x_sample.shape), s_rw[:, None], s_gla[:, None])
```

```python
import functools
import itertools

import jax
import jax.numpy as jnp
from jax import lax
from jax.experimental import pallas as pl
from jax.experimental.pallas import tpu as pltpu

F32 = jnp.float32
BF16 = jnp.bfloat16

D_MODEL = 1024
RW_W = 512
GLA_V_W = 512
GLA_QK_W = 256
N_EXPERTS = 16
D_EXPERT = 256
N_MOD = 6
EPS = 1e-6
RW_LN_EPS = 64e-5
RW_DECAY_SCALE = 0.606531
GLA_GATE_NORM = 16.0
GLA_Q_SCALE = 64 ** -0.5
GRID_W = 64

LANE = 128
CHUNK = 64
CONV_PAD = 128
TERM_UNROLL = 4
RW_PAIRS_PER_STEP = 2
GLA_PAIRS = 2
RW_GROUPS = 4
D_PROJ = 28 * LANE
VMEM_LIMIT = 56 * 1024 * 1024

CB_R, CB_K, CB_V, CB_LORA, CB_LGK, CB_GQ, CB_GK, CB_GV, CB_OG = 0, 4, 8, 12, 15, 16, 18, 20, 24

_NN = (((1,), (0,)), ((), ()))
_NT = (((1,), (1,)), ((), ()))
_TN = (((0,), (0,)), ((), ()))


def _dot(a, b, dims=_NN):
    return lax.dot_general(a, b, dims, preferred_element_type=F32)


def _mm(a, b, dims=_NN):
    return _dot(a.astype(BF16), b.astype(BF16), dims)


def _split2(x):
    hi = x.astype(BF16)
    lo = (x - hi.astype(F32)).astype(BF16)
    return hi, lo


def _mm3(a, b, dims=_NN):
    ah, al = _split2(a)
    bh, bl = _split2(b)
    return _dot(ah, bh, dims) + _dot(ah, bl, dims) + _dot(al, bh, dims)


def _mm_01_lhs(a01, b, dims=_NN):
    n = b.shape[1]
    both = _dot(a01, jnp.concatenate(_split2(b), axis=1), dims)
    return both[:, 0:n] + both[:, n:2 * n]


def _sigmoid(x):
    return 0.5 * jnp.tanh(0.5 * x) + 0.5


def _silu(x):
    return x * _sigmoid(x)


def _log_sigmoid(x):
    return jnp.minimum(x, 0.0) - jnp.log(1.0 + jnp.exp(-jnp.abs(x)))


def _iota(shape, dim):
    return lax.broadcasted_iota(jnp.int32, shape, dim)


def _cparams(n_axes):
    return pltpu.CompilerParams(dimension_semantics=("arbitrary",) * n_axes, vmem_limit_bytes=VMEM_LIMIT)


MOD_TN = 768


def _mod_kernel(c_ref, w_ref, b_ref, o_ref):
    o_ref[...] = _mm3(_silu(c_ref[...]), w_ref[...]) + b_ref[...]


def _modulation(cond8, w_ada, b_ada):
    n = w_ada.shape[1]
    return pl.pallas_call(
        _mod_kernel,
        grid=(n // MOD_TN,),
        in_specs=[pl.BlockSpec((8, D_MODEL), lambda j: (0, 0)),
                  pl.BlockSpec((D_MODEL, MOD_TN), lambda j: (0, j)),
                  pl.BlockSpec((1, MOD_TN), lambda j: (0, j))],
        out_specs=pl.BlockSpec((8, MOD_TN), lambda j: (0, j)),
        out_shape=jax.ShapeDtypeStruct((8, n), F32),
        compiler_params=_cparams(1),
        name="adaln_mod",
    )(cond8, w_ada, b_ada)


PROJ_TM = 512
D_IN = 3488
N_LGK = 32


def _rmsnorm_rows(x):
    return x * lax.rsqrt(jnp.mean(x * x, axis=-1, keepdims=True) + EPS)


class _Tiles:
    def __init__(self, n_ctx_tokens, n_dec_tokens, dec_seq_len, ctx_row, tm):
        self.tm = tm
        self.n_ctx = n_ctx_tokens // tm
        self.n_dec = n_dec_tokens // tm
        self.per_seq = dec_seq_len // tm
        self.ctx_row = ctx_row

    def specs(self, width):
        last_ctx = self.n_ctx - 1
        n_ctx = self.n_ctx
        return (pl.BlockSpec((self.tm, width), lambda i: (jnp.minimum(i, last_ctx), 0)),
                pl.BlockSpec((self.tm, width), lambda i: (jnp.maximum(i - n_ctx, 0), 0)))

    def merged(self, width):
        return pl.BlockSpec((self.tm, width), lambda i: (i, 0))

    def mod_spec(self):
        n_ctx, per_seq, ctx_row = self.n_ctx, self.per_seq, self.ctx_row
        return pl.BlockSpec((1, N_MOD, D_MODEL),
                            lambda i: (jnp.where(i < n_ctx, ctx_row, (i - n_ctx) // per_seq), 0, 0))

    def by_pass(self, run_ctx, run_dec):
        i = pl.program_id(0)
        pl.when(i < self.n_ctx)(run_ctx)
        pl.when(i >= self.n_ctx)(run_dec)


def _inproj_kernel(tiles, xc_ref, xd_ref, mod_ref, g_ref, wt_ref, o_ref, w_ref):
    @pl.when(pl.program_id(0) == 0)
    def _():
        for j in range(D_PROJ // LANE):
            if j == CB_LGK:
                blk = jnp.concatenate([wt_ref[D_IN - N_LGK:D_IN, :], jnp.zeros((LANE - N_LGK, D_MODEL), F32)], axis=0)
            else:
                src = j if j < CB_LGK else j - 1
                blk = wt_ref[src * LANE:(src + 1) * LANE, :]
            w_ref[:, j * LANE:(j + 1) * LANE] = blk.T.astype(BF16)

    def run(x_ref):
        m = mod_ref[0]
        h = _rmsnorm_rows(x_ref[...]) * g_ref[...] * (1.0 + m[1:2]) + m[0:1]
        o_ref[...] = _mm(h, w_ref[...])

    tiles.by_pass(functools.partial(run, xc_ref), functools.partial(run, xd_ref))


def _in_projection(tiles, x_ctx, x_dec, mod, norm_g, w_in_t):
    full = lambda a: pl.BlockSpec(a.shape, lambda i: (0,) * a.ndim)
    return pl.pallas_call(
        functools.partial(_inproj_kernel, tiles),
        grid=(tiles.n_ctx + tiles.n_dec,),
        in_specs=[*tiles.specs(D_MODEL), tiles.mod_spec(), full(norm_g),
                  pl.BlockSpec(w_in_t.shape, lambda i: (0, 0), pipeline_mode=pl.Buffered(1))],
        out_specs=tiles.merged(D_PROJ),
        out_shape=jax.ShapeDtypeStruct((x_ctx.shape[0] + x_dec.shape[0], D_PROJ), F32),
        scratch_shapes=[pltpu.VMEM((D_MODEL, D_PROJ), BF16)],
        compiler_params=_cparams(1),
        name="in_proj",
    )(x_ctx, x_dec, mod, norm_g, w_in_t)


def _time_masks(reverse):
    r = _iota((2 * CHUNK, 2 * CHUNK), 0) % CHUNK
    c = _iota((2 * CHUNK, 2 * CHUNK), 1) % CHUNK
    if reverse:
        return r < c, r <= c
    return r > c, r >= c


def _cumsum_matrix(reverse):
    r = _iota((CHUNK, CHUNK), 0)
    c = _iota((CHUNK, CHUNK), 1)
    tri = (r <= c) if reverse else (r >= c)
    return tri.astype(BF16)


def _stack_heads(x, half):
    m0 = _iota(x.shape, 1) < half
    return jnp.concatenate([jnp.where(m0, x, 0.0), jnp.where(m0, 0.0, x)], axis=0)


def _finished_before_last_scan(groups, n_chunks):
    done_fwd, done_bwd = set(), set()
    for s, fwd, bwd in groups[:-1]:
        done_fwd.update(s * n_chunks + c for c in fwd)
        done_bwd.update(s * n_chunks + c for c in bwd)
    return sorted(done_fwd & done_bwd)


def _head_sums(x):
    parts = []
    for p in range(x.shape[1] // LANE):
        xp = x[:, p * LANE:(p + 1) * LANE]
        m0 = _iota(xp.shape, 1) < 64
        s0 = jnp.sum(jnp.where(m0, xp, 0.0), axis=-1, keepdims=True)
        s1 = jnp.sum(jnp.where(m0, 0.0, xp), axis=-1, keepdims=True)
        parts.append(jnp.where(m0, s0, s1))
    return parts[0] if len(parts) == 1 else jnp.concatenate(parts, axis=1)


def _rwkv_chunk_terms(insts, interleaved=()):
    c = CHUNK
    step_row = _iota((c, LANE), 0)
    step_col = _iota((c, LANE), 1) % c
    eye_w = (step_row == step_col).astype(F32)
    same_head = (_iota((LANE, LANE), 0) // 64) == (_iota((LANE, LANE), 1) // 64)
    stack_bf = lambda x: _stack_heads(x, 64).astype(BF16)
    cums = [_mm_01_lhs(_cumsum_matrix(rev), lw) for (_, lw, _, _, _, _, rev) in insts]
    pre = []
    for (r, lw, kd, a, b, v, rev), cum in zip(insts, cums):
        end = cum[0:1] if rev else cum[c - 1:c]
        inv_w = jnp.exp(-cum)
        rem_w = jnp.exp(end - cum)
        a_t = a * jnp.exp(cum - lw)
        r_t = r * jnp.exp(cum)
        bk_s = jnp.concatenate([stack_bf(b * inv_w), stack_bf(kd * inv_w)], axis=0)
        bkh = jnp.concatenate([b * rem_w, kd * rem_w], axis=0).astype(BF16)
        pre.append((a_t, r_t, bk_s, bkh, v, jnp.exp(end)))
    ms = [_dot(jnp.concatenate([a_t, r_t], axis=0).astype(BF16), bk_s, _NT) for (a_t, r_t, bk_s, _, _, _) in pre]
    mats = []
    for m, (_, _, _, _, _, _, rev) in zip(ms, insts):
        strict = (step_row < step_col) if rev else (step_row > step_col)
        incl = (step_row <= step_col) if rev else (step_row >= step_col)
        l_ab = jnp.where(strict, m[0:c, 0:LANE], 0.0)
        l_akrk = jnp.concatenate([jnp.where(strict, m[0:c, LANE:2 * LANE], 0.0),
                                  jnp.where(incl, m[c:2 * c, LANE:2 * LANE], 0.0)], axis=0).astype(BF16)
        m_rb = jnp.where(incl, m[c:2 * c, 0:LANE], 0.0).astype(BF16)
        mats.append((l_ab, l_akrk, m_rb))
    pending = list(interleaved)

    def run_interleaved():
        if pending:
            pending.pop(0)()

    lvs = [_dot(l_akrk, stack_bf(pr[4])) for (_, l_akrk, _), pr in zip(mats, pre)]
    run_interleaved()
    ps = [eye_w + l_ab for (l_ab, _, _) in mats]
    lps = [_dot(l_ab.astype(BF16), stack_bf(l_ab)) for (l_ab, _, _) in mats]
    for level in range(1, 6):
        if level < 5:
            xs = [_dot(lp.astype(BF16), jnp.concatenate([stack_bf(p), stack_bf(lp)], axis=1))
                  for lp, p in zip(lps, ps)]
            ps = [p + x[:, 0:LANE] for p, x in zip(ps, xs)]
            lps = [x[:, LANE:2 * LANE] for x in xs]
        else:
            ps = [p + _dot(lp.astype(BF16), stack_bf(p)) for lp, p in zip(lps, ps)]
        if level in (2, 4):
            run_interleaved()
    pxs = [_dot(p.astype(BF16), jnp.concatenate([stack_bf(pr[0]), stack_bf(lv[0:c])], axis=1))
           for p, pr, lv in zip(ps, pre, lvs)]
    run_interleaved()
    mzs = [_dot(mt[2], jnp.concatenate([stack_bf(px[:, 0:LANE]), stack_bf(px[:, LANE:2 * LANE])], axis=1))
           for mt, px in zip(mats, pxs)]
    ts = [_dot(px[:, 0:LANE].astype(BF16), pr[3][0:c], _TN) for px, pr in zip(pxs, pre)]
    gs = [_dot(jnp.concatenate([px[:, LANE:2 * LANE], pr[4]], axis=0).astype(BF16), pr[3], _TN)
          for px, pr in zip(pxs, pre)]
    while pending:
        run_interleaved()
    out = []
    for pr, lv, mz, t, g in zip(pre, lvs, mzs, ts, gs):
        q = pr[1] + mz[:, 0:LANE]
        y0 = mz[:, LANE:2 * LANE] + lv[c:2 * c]
        g_wide = jnp.where(_iota((c, LANE), 1) < 64, g[0:c], g[c:2 * c])
        out.append((jnp.where(same_head, t, 0.0).astype(BF16), g_wide, pr[5],
                    q.astype(BF16), y0))
    return out


def _gla_chunk_terms(insts, interleaved=()):
    c = CHUNK
    step_row = _iota((c, LANE), 0)
    step_col = _iota((c, LANE), 1) % c
    same_head = (_iota((LANE, 2 * LANE), 0) // 64) == (_iota((LANE, 2 * LANE), 1) // LANE)
    cums = [_mm_01_lhs(_cumsum_matrix(rev), g) for (_, _, _, g, rev) in insts]
    pre = []
    for (q, k, v, g, rev), cum in zip(insts, cums):
        end = cum[0:1] if rev else cum[c - 1:c]
        qt = (q * jnp.exp(cum)).astype(BF16)
        k_s = _stack_heads(k * jnp.exp(-cum), 64).astype(BF16)
        kh = (k * jnp.exp(end - cum)).astype(BF16)
        a_col = jnp.broadcast_to(jnp.exp(end), (LANE, LANE)).T
        pre.append((qt, k_s, kh, v.astype(BF16), _stack_heads(v, LANE).astype(BF16), a_col))
    pending = list(interleaved)

    def run_interleaved():
        if pending:
            pending.pop(0)()

    run_interleaved()
    atts = [_dot(pr[0], pr[1], _NT) for pr in pre]
    run_interleaved()
    atts = [jnp.where((step_row <= step_col) if inst[4] else (step_row >= step_col), att, 0.0).astype(BF16)
            for att, inst in zip(atts, insts)]
    o0s = [_dot(att, pr[4]) for att, pr in zip(atts, pre)]
    run_interleaved()
    kvs = [jnp.where(same_head, _dot(pr[2], pr[3], _TN), 0.0) for pr in pre]
    while pending:
        run_interleaved()
    return [(pr[0], o0, pr[5], kv) for pr, o0, kv in zip(pre, o0s, kvs)]


def _rwkv_kernel(seq_len, nseq, is_grid, zero_init, r_ref, k_ref, v_ref, lora_ref, cwr_ref, cwk_ref, cwv_ref,
                 w2_ref, a2_ref, g2_ref, vec_ref, *rest):
    s0_ref = None if zero_init else rest[0]
    (y_ref, sout_ref, pad_ref, rs_ref, ks_ref, vs_ref, kk_ref, bonus_ref, gate_ref, yf_ref, yb_ref, st_ref, tt_ref,
     tg_ref, tw_ref, tq_ref, ty_ref) = rest[0 if zero_init else 1:][:17]
    left_ref, right_ref = rest[-2:] if is_grid else (None, None)
    n_chunks = seq_len // CHUNK
    npp = RW_PAIRS_PER_STEP
    w = npp * LANE
    pair = lambda x, p: x[:, p * LANE:(p + 1) * LANE]
    vec = vec_ref[...]
    w0 = (vec[0:1], vec[1:2])
    a0 = (vec[2:3], vec[3:4])
    k_k, k_a, r_k, ln_w, ln_b = vec[4:5], vec[5:6], vec[6:7], vec[7:8], vec[8:9]
    block_sum = _head_sums
    chains = [(d, p) for d in range(2) for p in range(npp)]
    rows_of = lambda s, c: slice((s * n_chunks + c) * CHUNK, (s * n_chunks + c + 1) * CHUNK)

    if not is_grid:
        zeros = jnp.zeros((CONV_PAD, 3 * w), F32)
        pad_ref[0:CONV_PAD, :] = zeros
        pad_ref[CONV_PAD + seq_len:2 * CONV_PAD + seq_len, :] = zeros
    cw = jnp.concatenate([cwr_ref[...], cwk_ref[...], cwv_ref[...]], axis=1)
    col = _iota((CHUNK, 3 * w), 0)
    rkv_rows = lambda rows: jnp.concatenate([r_ref[rows, :], k_ref[rows, :], v_ref[rows, :]], axis=1)

    def shift_body(c, carry):
        rows = pl.ds(pl.multiple_of(c * CHUNK, CHUNK), CHUNK)
        x = rkv_rows(rows)
        left_ref[rows, :] = jnp.where(col >= 1, pltpu.roll(x, 1, 0), 0.0)
        right_ref[rows, :] = jnp.where(col <= GRID_W - 2, pltpu.roll(x, CHUNK - 1, 0), 0.0)
        return carry

    def conv_chunk(c):
        base = CONV_PAD + c * CHUNK
        acc = jnp.zeros((CHUNK, 3 * w), F32)
        if is_grid:
            for di in (-1, 0, 1):
                if not 0 <= c + di < n_chunks:
                    continue
                rows = slice((c + di) * CHUNK, (c + di + 1) * CHUNK)
                for dj, src in ((-1, left_ref[rows, :]), (0, rkv_rows(rows)), (1, right_ref[rows, :])):
                    tap = (di + 1) * 3 + dj + 1
                    acc = acc + src * cw[tap:tap + 1]
        else:
            win = pad_ref[pl.ds(base - 8, CHUNK + 16), :]
            for dj in (-1, 0, 1):
                acc = acc + win[8 + dj:8 + dj + CHUNK] * cw[4 + dj:5 + dj]
        return acc[:, 0:w], acc[:, w:2 * w], acc[:, 2 * w:3 * w]

    def load_sequence(s):
        seq_rows = slice(s * seq_len, (s + 1) * seq_len)
        pad_ref[CONV_PAD:CONV_PAD + seq_len, 0:w] = r_ref[seq_rows, :]
        pad_ref[CONV_PAD:CONV_PAD + seq_len, w:2 * w] = k_ref[seq_rows, :]
        pad_ref[CONV_PAD:CONV_PAD + seq_len, 2 * w:3 * w] = v_ref[seq_rows, :]

    def conv_store(s, c):
        rows = rows_of(s, c)
        rc, kc, vc = conv_chunk(c)
        kk = kc * k_k
        rs_ref[rows, :] = rc
        ks_ref[rows, :] = kc
        vs_ref[rows, :] = vc
        kk_ref[rows, :] = kk * lax.rsqrt(block_sum(kk * kk) + EPS)
        bonus_ref[rows, :] = block_sum(rc * kc * r_k) * vc
        gate_ref[rows, :] = _mm(_sigmoid(lora_ref[rows, 2 * LANE:3 * LANE]), g2_ref[...])

    for s in range(nseq):
        for d, p in chains:
            if zero_init:
                st_ref[s, d, p] = jnp.zeros((64, LANE), F32)
            else:
                st_ref[s, d, p] = jnp.concatenate([s0_ref[s, d, 2 * p], s0_ref[s, d, 2 * p + 1]], axis=1)

    if nseq == 1:
        groups = [(0, list(range(g * TERM_UNROLL, (g + 1) * TERM_UNROLL)),
                   list(range(n_chunks - 1 - g * TERM_UNROLL, n_chunks - 1 - (g + 1) * TERM_UNROLL, -1)))
                  for g in range(n_chunks // TERM_UNROLL)]
    else:
        groups = [(s, list(range(n_chunks)), list(range(n_chunks - 1, -1, -1))) for s in range(nseq)]

    def group_terms(group, interleaved):
        s, fwd, bwd = group
        insts, where, lora_in = [], [], {}
        for d, chunks in ((0, fwd), (1, bwd)):
            for c in chunks:
                rows = rows_of(s, c)
                if c not in lora_in:
                    lora_in[c] = (jnp.tanh(lora_ref[rows, 0:LANE]).astype(BF16),
                                  lora_ref[rows, LANE:2 * LANE].astype(BF16))
                lw = -RW_DECAY_SCALE * _sigmoid(w0[d] + _dot(lora_in[c][0], w2_ref[d]))
                ag = _sigmoid(a0[d] + _dot(lora_in[c][1], a2_ref[d]))
                rc, kc, vc, kk = rs_ref[rows, :], ks_ref[rows, :], vs_ref[rows, :], kk_ref[rows, :]
                kd = kc * (1.0 + (ag - 1.0) * k_a)
                kb = kk * ag
                for p in range(npp):
                    insts.append((pair(rc, p), pair(lw, p), pair(kd, p), -pair(kk, p), pair(kb, p), pair(vc, p),
                                  d == 1))
                    where.append((d, p, s * n_chunks + c))
        for (d, p, gc), (t, g, w_end, q, y0) in zip(where, _rwkv_chunk_terms(insts, interleaved)):
            tt_ref[d, p, gc] = t
            tg_ref[d, p, gc] = g
            tw_ref[d, p, gc] = jnp.broadcast_to(w_end, (8, LANE))
            tq_ref[d, p, gc] = q
            ty_ref[d, p, gc] = y0

    def scan_step(s, chunk_of_dir):
        gcs = [s * n_chunks + chunk_of_dir[d] for d, _ in chains]
        ss = [st_ref[s, d, p] for d, p in chains]
        ys = [_dot(tq_ref[d, p, gc], _stack_heads(x, 64).astype(BF16), _NT) + ty_ref[d, p, gc]
              for (d, p), gc, x in zip(chains, gcs, ss)]
        sn = [x * tw_ref[d, p, gc][0:1] + _dot(x.astype(BF16), tt_ref[d, p, gc]) + tg_ref[d, p, gc]
              for (d, p), gc, x in zip(chains, gcs, ss)]
        for (d, p), gc, y, x in zip(chains, gcs, ys, sn):
            st_ref[s, d, p] = x
            out_ref = yf_ref if d == 0 else yb_ref
            out_ref[gc * CHUNK:(gc + 1) * CHUNK, p * LANE:(p + 1) * LANE] = y

    def group_scan(group):
        s, fwd, bwd = group
        return [functools.partial(scan_step, s, (cf, cb)) for cf, cb in zip(fwd, bwd)]

    conv_done, seq_loaded = set(), set()

    def group_conv(group):
        s, fwd, bwd = group
        todo = [c for c in sorted(set(fwd) | set(bwd)) if (s, c) not in conv_done]
        conv_done.update((s, c) for c in todo)
        thunks = []
        for i in range(0, len(todo), max(1, -(-len(todo) // TERM_UNROLL))):
            part = todo[i:i + max(1, -(-len(todo) // TERM_UNROLL))]
            need_load = s not in seq_loaded
            seq_loaded.add(s)

            def run(part=part, need_load=need_load):
                if need_load:
                    assert not is_grid or nseq == 1
                    load_sequence(s)
                for c in part:
                    conv_store(s, c)

            thunks.append(run)
        return thunks

    def merge(*lists):
        n = max(len(lst) for lst in lists)
        pick = lambda lst, i: lst[i] if i < len(lst) else (lambda: None)
        return [lambda i=i: [pick(lst, i)() for lst in lists] for i in range(n)]

    def post(offs):
        ys = [yf_ref[pl.ds(off, CHUNK), :] + yb_ref[pl.ds(off, CHUNK), :] for off in offs]
        mus = [block_sum(y) * (1.0 / 64) for y in ys]
        dlts = [y - mu for y, mu in zip(ys, mus)]
        vrs = [block_sum(dlt * dlt) * (1.0 / 64) for dlt in dlts]
        for off, dlt, var in zip(offs, dlts, vrs):
            yn = dlt * lax.rsqrt(var + RW_LN_EPS) * ln_w + ln_b
            y = (yn + bonus_ref[pl.ds(off, CHUNK), :]) * gate_ref[pl.ds(off, CHUNK), :]
            y_ref[pl.ds(off, CHUNK), :] = y.astype(y_ref.dtype)

    def sequence_post(s):
        return [functools.partial(post, [(s * n_chunks + c) * CHUNK]) for c in range(n_chunks)]

    if is_grid:
        assert nseq == 1
        seq_loaded.add(0)
        lax.fori_loop(0, n_chunks, shift_body, 0)
    for thunk in group_conv(groups[0]):
        thunk()
    posted = set()
    for g, group in enumerate(groups):
        scans = group_scan(groups[g - 1]) if g else []
        convs = group_conv(groups[g + 1]) if g + 1 < len(groups) else []
        posts = []
        if nseq > 1 and g >= 2:
            s_done = groups[g - 2][0]
            posts = sequence_post(s_done)
            posted.update(range(s_done * n_chunks, (s_done + 1) * n_chunks))
        group_terms(group, merge(scans, convs, posts))
    early = [gc for gc in _finished_before_last_scan(groups, n_chunks) if gc not in posted]
    last_scan = group_scan(groups[-1])
    per_step = max(1, -(-len(early) // len(last_scan)))
    early_posts = [functools.partial(post, [gc * CHUNK for gc in early[i:i + per_step]])
                   for i in range(0, len(early), per_step)]
    for step, extra in itertools.zip_longest(last_scan, early_posts, fillvalue=lambda: None):
        step()
        extra()
    posted.update(early)
    for s in range(nseq):
        for d, p in chains:
            x = st_ref[s, d, p]
            sout_ref[s, d, 2 * p] = x[:, 0:64]
            sout_ref[s, d, 2 * p + 1] = x[:, 64:LANE]

    left = [gc for gc in range(nseq * n_chunks) if gc not in posted]
    for i in range(0, len(left), TERM_UNROLL):
        post([gc * CHUNK for gc in left[i:i + TERM_UNROLL]])


def _rwkv_mixer(proj, first_seq, n_seq, seq_len, is_grid, prm, s0):
    n_pairs = RW_W // LANE
    n_chunks = seq_len // CHUNK
    npp = RW_PAIRS_PER_STEP
    w = npp * LANE
    nseq = max(1, RW_GROUPS * TERM_UNROLL // n_chunks)
    assert n_chunks % TERM_UNROLL == 0 and n_pairs % npp == 0
    assert (n_chunks == TERM_UNROLL or nseq == 1) and n_seq % nseq == 0 and first_seq % nseq == 0
    rows = nseq * seq_len
    total_chunks = nseq * n_chunks
    first = first_seq // nseq
    col = lambda cb: (lambda b, p: (b + first, cb // npp + p))
    par = lambda cb: (lambda b, p: (0, cb // npp + p))
    state_spec = pl.BlockSpec((nseq, 2, 2 * npp, 64, 64), lambda b, p: (b, 0, p, 0, 0))
    kernel = functools.partial(_rwkv_kernel, seq_len, nseq, is_grid, s0 is None)
    y, s_out = pl.pallas_call(
        kernel,
        grid=(n_seq // nseq, n_pairs // npp),
        in_specs=[pl.BlockSpec((rows, w), col(CB_R)),
                  pl.BlockSpec((rows, w), col(CB_K)),
                  pl.BlockSpec((rows, w), col(CB_V)),
                  pl.BlockSpec((rows, 3 * LANE), lambda b, p: (b + first, CB_LORA // 3)),
                  pl.BlockSpec((9, w), par(CB_R)),
                  pl.BlockSpec((9, w), par(CB_K)),
                  pl.BlockSpec((9, w), par(CB_V)),
                  pl.BlockSpec((2, LANE, w), lambda b, p: (0, 0, p)),
                  pl.BlockSpec((2, LANE, w), lambda b, p: (0, 0, p)),
                  pl.BlockSpec((LANE, w), lambda b, p: (0, p)),
                  pl.BlockSpec((16, w), lambda b, p: (0, p))] + ([] if s0 is None else [state_spec]),
        out_specs=[pl.BlockSpec((rows, w), lambda b, p: (b, p)), state_spec],
        out_shape=[jax.ShapeDtypeStruct((n_seq * seq_len, RW_W), BF16),
                   jax.ShapeDtypeStruct((n_seq, 2, 2 * n_pairs, 64, 64), F32)],
        scratch_shapes=[pltpu.VMEM((8, LANE) if is_grid else (seq_len + 2 * CONV_PAD, 3 * w), F32)]
                       + [pltpu.VMEM((rows, w), F32)] * 8
                       + [pltpu.VMEM((nseq, 2, npp, 64, LANE), F32),
                          pltpu.VMEM((2, npp, total_chunks, LANE, LANE), BF16),
                          pltpu.VMEM((2, npp, total_chunks, 64, LANE), F32),
                          pltpu.VMEM((2, npp, total_chunks, 8, LANE), F32),
                          pltpu.VMEM((2, npp, total_chunks, CHUNK, LANE), BF16),
                          pltpu.VMEM((2, npp, total_chunks, CHUNK, LANE), F32)]
                       + ([pltpu.VMEM((seq_len, 3 * w), F32)] * 2 if is_grid else []),
        compiler_params=_cparams(2),
        name="rwkv_mixer",
    )(proj, proj, proj, proj, prm['conv'], prm['conv'], prm['conv'], prm['w2p'], prm['a2p'], prm['g2'],
      prm['vec'], *([] if s0 is None else [s0]))
    return y, s_out


def _gla_kernel(seq_len, nseq, zero_init, q_ref, k_ref, v_ref, og_ref, lgk_ref, gk2_ref, gvec_ref, *rest):
    s0_ref = None if zero_init else rest[0]
    y_ref, sout_ref, of_ref, ob_ref, st_ref, tq_ref, to_ref, ta_ref, tkv_ref = rest[0 if zero_init else 1:]
    n_chunks = seq_len // CHUNK
    npp = GLA_PAIRS
    gvec = gvec_ref[...]
    chains = [(d, p) for d in range(2) for p in range(npp)]

    for s in range(nseq):
        for d, p in chains:
            if zero_init:
                st_ref[s, d, p] = jnp.zeros((LANE, 2 * LANE), F32)
            else:
                z = jnp.zeros((64, LANE), F32)
                st_ref[s, d, p] = jnp.concatenate([jnp.concatenate([s0_ref[s, d, 2 * p], z], axis=1),
                                                   jnp.concatenate([z, s0_ref[s, d, 2 * p + 1]], axis=1)], axis=0)

    if nseq == 1:
        groups = [(0, list(range(g * TERM_UNROLL, (g + 1) * TERM_UNROLL)),
                   list(range(n_chunks - 1 - g * TERM_UNROLL, n_chunks - 1 - (g + 1) * TERM_UNROLL, -1)))
                  for g in range(n_chunks // TERM_UNROLL)]
    else:
        groups = [(s, list(range(n_chunks)), list(range(n_chunks - 1, -1, -1))) for s in range(nseq)]

    def group_terms(group, interleaved):
        s, fwd, bwd = group
        insts, where, lgk_in = [], [], {}
        for d, chunks in ((0, fwd), (1, bwd)):
            for c in chunks:
                gc = s * n_chunks + c
                rows = slice(gc * CHUNK, (gc + 1) * CHUNK)
                if c not in lgk_in:
                    lgk_in[c] = lgk_ref[rows, :].astype(BF16)
                x = _dot(lgk_in[c], gk2_ref[d]) + gvec[d:d + 1, 0:GLA_QK_W]
                g = _log_sigmoid(x) * (1.0 / GLA_GATE_NORM)
                qc = q_ref[rows, :] * GLA_Q_SCALE
                kc = k_ref[rows, :]
                vc = v_ref[rows, :]
                for p in range(npp):
                    qk = slice(p * LANE, (p + 1) * LANE)
                    insts.append((qc[:, qk], kc[:, qk], vc[:, 2 * p * LANE:2 * (p + 1) * LANE], g[:, qk], d == 1))
                    where.append((d, p, gc))
        for (d, p, gc), (qt, o0, a_col, kv) in zip(where, _gla_chunk_terms(insts, interleaved)):
            tq_ref[d, p, gc] = qt
            to_ref[d, p, gc] = o0
            ta_ref[d, p, gc] = a_col
            tkv_ref[d, p, gc] = kv

    def scan_step(s, chunk_of_dir):
        gcs = [s * n_chunks + chunk_of_dir[d] for d, _ in chains]
        ss = [st_ref[s, d, p] for d, p in chains]
        os_ = [_dot(tq_ref[d, p, gc], x.astype(BF16)) + to_ref[d, p, gc] for (d, p), gc, x in zip(chains, gcs, ss)]
        for (d, p), gc, x, o in zip(chains, gcs, ss, os_):
            a_col = ta_ref[d, p, gc]
            st_ref[s, d, p] = x * jnp.concatenate([a_col, a_col], axis=1) + tkv_ref[d, p, gc]
            out_ref = of_ref if d == 0 else ob_ref
            out_ref[gc * CHUNK:(gc + 1) * CHUNK, 2 * p * LANE:2 * (p + 1) * LANE] = o

    def group_scan(group):
        s, fwd, bwd = group
        return [functools.partial(scan_step, s, (cf, cb)) for cf, cb in zip(fwd, bwd)]

    def post(off):
        for h in range(2 * npp):
            hs = slice(h * LANE, (h + 1) * LANE)
            o = of_ref[pl.ds(off, CHUNK), hs] + ob_ref[pl.ds(off, CHUNK), hs]
            gate = _silu(og_ref[pl.ds(off, CHUNK), hs])
            y = _rmsnorm_rows(o) * gvec[2:3, hs] * gate
            y_ref[pl.ds(off, CHUNK), hs] = y.astype(y_ref.dtype)

    posted = set()
    for g, group in enumerate(groups):
        thunks = group_scan(groups[g - 1]) if g else []
        if nseq > 1 and g >= 2:
            s_done = groups[g - 2][0]
            posts = [functools.partial(post, (s_done * n_chunks + c) * CHUNK) for c in range(n_chunks)]
            thunks = [lambda a=a, b=b: (a(), b()) for a, b in zip(thunks, posts)]
            posted.update(range(s_done * n_chunks, (s_done + 1) * n_chunks))
        group_terms(group, thunks)
    early = [gc for gc in _finished_before_last_scan(groups, n_chunks) if gc not in posted]
    last_scan = group_scan(groups[-1])
    per_step = max(1, -(-len(early) // len(last_scan)))
    for i, step in enumerate(last_scan):
        step()
        for gc in early[i * per_step:(i + 1) * per_step]:
            post(gc * CHUNK)
    for gc in early[len(last_scan) * per_step:]:
        post(gc * CHUNK)
    posted.update(early)
    for s in range(nseq):
        for d, p in chains:
            x = st_ref[s, d, p]
            sout_ref[s, d, 2 * p] = x[0:64, 0:LANE]
            sout_ref[s, d, 2 * p + 1] = x[64:LANE, LANE:2 * LANE]
    for gc in range(nseq * n_chunks):
        if gc not in posted:
            post(gc * CHUNK)


def _gla_mixer(proj, first_seq, n_seq, seq_len, prm, s0):
    npp = GLA_PAIRS
    n_heads = 2 * npp
    n_chunks = seq_len // CHUNK
    nseq = max(1, RW_GROUPS * TERM_UNROLL // n_chunks)
    assert n_chunks % TERM_UNROLL == 0
    assert (n_chunks == TERM_UNROLL or nseq == 1) and n_seq % nseq == 0 and first_seq % nseq == 0
    rows = nseq * seq_len
    total_chunks = nseq * n_chunks
    first = first_seq // nseq
    state_spec = pl.BlockSpec((nseq, 2, n_heads, 64, LANE), lambda b: (b, 0, 0, 0, 0))
    kernel = functools.partial(_gla_kernel, seq_len, nseq, s0 is None)
    y, s_out = pl.pallas_call(
        kernel,
        grid=(n_seq // nseq,),
        in_specs=[pl.BlockSpec((rows, GLA_QK_W), lambda b: (b + first, CB_GQ * LANE // GLA_QK_W)),
                  pl.BlockSpec((rows, GLA_QK_W), lambda b: (b + first, CB_GK * LANE // GLA_QK_W)),
                  pl.BlockSpec((rows, GLA_V_W), lambda b: (b + first, CB_GV * LANE // GLA_V_W)),
                  pl.BlockSpec((rows, GLA_V_W), lambda b: (b + first, CB_OG * LANE // GLA_V_W)),
                  pl.BlockSpec((rows, LANE), lambda b: (b + first, CB_LGK)),
                  pl.BlockSpec((2, LANE, GLA_QK_W), lambda b: (0, 0, 0)),
                  pl.BlockSpec((8, GLA_V_W), lambda b: (0, 0))] + ([] if s0 is None else [state_spec]),
        out_specs=[pl.BlockSpec((rows, GLA_V_W), lambda b: (b, 0)), state_spec],
        out_shape=[jax.ShapeDtypeStruct((n_seq * seq_len, GLA_V_W), BF16),
                   jax.ShapeDtypeStruct((n_seq, 2, n_heads, 64, LANE), F32)],
        scratch_shapes=[pltpu.VMEM((rows, GLA_V_W), F32)] * 2
                       + [pltpu.VMEM((nseq, 2, npp, LANE, 2 * LANE), F32),
                          pltpu.VMEM((2, npp, total_chunks, CHUNK, LANE), BF16),
                          pltpu.VMEM((2, npp, total_chunks, CHUNK, 2 * LANE), F32),
                          pltpu.VMEM((2, npp, total_chunks, LANE, LANE), F32),
                          pltpu.VMEM((2, npp, total_chunks, LANE, 2 * LANE), F32)],
        compiler_params=_cparams(1),
        name="gla_mixer",
    )(proj, proj, proj, proj, proj, prm['gk2p'], prm['gvec'], *([] if s0 is None else [s0]))
    return y, s_out


OUT_TM = 512
ROUTE_NEG = -1e30
LANE_GROUP0 = N_EXPERTS


def _route(logits):
    lane = _iota(logits.shape, 1)
    lane_f = lane.astype(F32)
    big = float(LANE)
    is_g = (lane >= LANE_GROUP0) & (lane < LANE_GROUP0 + 4)
    gmax = jnp.max(jnp.where(is_g, logits, ROUTE_NEG), axis=-1, keepdims=True)
    gidx = jnp.min(jnp.where(is_g & (logits == gmax), lane_f, big), axis=-1, keepdims=True) - LANE_GROUP0
    gsum = jnp.sum(jnp.where(is_g, jnp.exp(jnp.minimum(logits - gmax, 0.0)), 0.0), axis=-1, keepdims=True)
    g_w = 1.0 / gsum
    in_grp = (lane < N_EXPERTS) & ((lane // 4).astype(F32) == gidx)
    m1 = jnp.max(jnp.where(in_grp, logits, ROUTE_NEG), axis=-1, keepdims=True)
    i1 = jnp.min(jnp.where(in_grp & (logits == m1), lane_f, big), axis=-1, keepdims=True)
    rest = in_grp & (lane_f != i1)
    m2 = jnp.max(jnp.where(rest, logits, ROUTE_NEG), axis=-1, keepdims=True)
    i2 = jnp.min(jnp.where(rest & (logits == m2), lane_f, big), axis=-1, keepdims=True)
    t = jnp.exp(m2 - m1)
    w1 = g_w / (1.0 + t)
    return jnp.where(lane_f == i1, w1, 0.0) + jnp.where(lane_f == i2, w1 * t, 0.0)


def _outproj_kernel(tiles, yrc_ref, yrd_ref, ygc_ref, ygd_ref, xc_ref, xd_ref, mod_ref, wo_ref, g_ref, wr_ref,
                    br_ref, x1_ref, h2_ref, cmb_ref):
    def run(yr_ref, yg_ref, x_ref):
        m = mod_ref[0]
        mix = _mm(yr_ref[...], wo_ref[0:RW_W, :]) + _mm(yg_ref[...], wo_ref[RW_W:RW_W + GLA_V_W, :])
        x1 = x_ref[...] + m[2:3] * mix
        h2 = _rmsnorm_rows(x1) * g_ref[...] * (1.0 + m[4:5]) + m[3:4]
        x1_ref[...] = x1
        h2_ref[...] = h2.astype(BF16)
        h_hi, h_lo = _split2(h2)
        both = _dot(h_hi, wr_ref[...])
        logits = both[:, 0:LANE] + both[:, LANE:2 * LANE] + _dot(h_lo, wr_ref[:, 0:LANE])
        cmb_ref[...] = _route(logits + br_ref[...])

    tiles.by_pass(functools.partial(run, yrc_ref, ygc_ref, xc_ref), functools.partial(run, yrd_ref, ygd_ref, xd_ref))


def _out_projection(tiles, y_rw, y_gla, x, mod, w_out, norm_g, w_route, b_route):
    n = x[0].shape[0] + x[1].shape[0]
    full = lambda a: pl.BlockSpec(a.shape, lambda i: (0,) * a.ndim)
    return pl.pallas_call(
        functools.partial(_outproj_kernel, tiles),
        grid=(tiles.n_ctx + tiles.n_dec,),
        in_specs=[*tiles.specs(RW_W), *tiles.specs(GLA_V_W), *tiles.specs(D_MODEL), tiles.mod_spec(),
                  full(w_out), full(norm_g), full(w_route), full(b_route)],
        out_specs=[tiles.merged(D_MODEL), tiles.merged(D_MODEL), tiles.merged(LANE)],
        out_shape=[jax.ShapeDtypeStruct((n, D_MODEL), F32), jax.ShapeDtypeStruct((n, D_MODEL), BF16),
                   jax.ShapeDtypeStruct((n, LANE), F32)],
        compiler_params=_cparams(1),
        name="out_proj_router",
    )(*y_rw, *y_gla, *x, mod, w_out, norm_g, w_route, b_route)


MOE_TM = 512
MOE_RB = 128
MOE_INTERLEAVE = 8
SLOT_ALIGN = 16
MOE_SLOTS = 2 * MOE_TM + N_EXPERTS * SLOT_ALIGN


def _stage_expert_weights(srcs_hbm, dst_refs, stage_refs, sems):
    def copies(e):
        return [pltpu.make_async_copy(src.at[e], stage.at[e % 2], sem.at[e % 2])
                for src, stage, sem in zip(srcs_hbm, stage_refs, sems)]

    for c in copies(0):
        c.start()
    for e in range(N_EXPERTS):
        if e + 1 < N_EXPERTS:
            for c in copies(e + 1):
                c.start()
        for c, dst, stage in zip(copies(e), dst_refs, stage_refs):
            c.wait()
            dst[e] = stage[e % 2].astype(BF16)


def _moe_kernel(tiles, h2_ref, cmb_ref, x1_ref, mod_ref, w1_hbm, w3_hbm, w2_hbm, fg_ref, yc_ref, yd_ref,
                xs_ref, ys_ref, w1_ref, w3_ref, w2_ref, stage1_ref, stage3_ref, stage2_ref, sem1, sem3, sem2):
    @pl.when(pl.program_id(0) == 0)
    def _():
        _stage_expert_weights((w1_hbm, w3_hbm, w2_hbm), (w1_ref, w3_ref, w2_ref),
                              (stage1_ref, stage3_ref, stage2_ref), (sem1, sem3, sem2))

    cmb = cmb_ref[...]
    lane = _iota(cmb.shape, 1).astype(F32)
    sel = cmb > 0.0
    sel01 = jnp.where(sel, 1.0, 0.0).astype(BF16)
    before = (_iota((MOE_TM, MOE_TM), 0) > _iota((MOE_TM, MOE_TM), 1)).astype(BF16)
    pos = _dot(before, sel01)
    cnt = pos[MOE_TM - 1:MOE_TM] + sel01[MOE_TM - 1:MOE_TM].astype(F32)
    seg = jnp.floor((cnt + (SLOT_ALIGN - 1)) * (1.0 / SLOT_ALIGN))
    lower_experts = (_iota((LANE, LANE), 0) < _iota((LANE, LANE), 1)).astype(BF16)
    start = _dot(jnp.broadcast_to(seg, (8, LANE)).astype(BF16), lower_experts)[0:1] * SLOT_ALIGN
    n_blk = jnp.floor((cnt + (MOE_RB - 1)) * (1.0 / MOE_RB)).astype(jnp.int32)
    start_i = start.astype(jnp.int32)
    cnt_i = cnt.astype(jnp.int32)
    slot = start + pos
    e_a = jnp.min(jnp.where(sel, lane, float(LANE)), axis=-1, keepdims=True)
    e_b = jnp.max(jnp.where(sel, lane, -1.0), axis=-1, keepdims=True)
    pick = lambda e, x: jnp.sum(jnp.where(lane == e, x, 0.0), axis=-1, keepdims=True)
    slot_a, w_a = pick(e_a, slot), pick(e_a, cmb)
    slot_b = jnp.where(e_b != e_a, pick(e_b, slot), -1.0)
    w_b = pick(e_b, cmb)

    slots_t = jnp.where(lane == 0.0, slot_a, jnp.where(lane == 1.0, slot_b, -1.0)).T
    row_slot = _iota((MOE_SLOTS, MOE_TM), 0).astype(F32)
    gather = jnp.where((row_slot == slots_t[0:1]) | (row_slot == slots_t[1:2]), 1.0, 0.0).astype(BF16)
    xs_ref[...] = _dot(gather, h2_ref[...]).astype(BF16)
    ys_ref[...] = jnp.zeros_like(ys_ref)

    row_in_blk = _iota((MOE_RB, D_MODEL), 0)

    def expert_blocks(experts, r0s, ends):
        base = [pl.multiple_of(jnp.minimum(r0, MOE_SLOTS - MOE_RB), SLOT_ALIGN) for r0 in r0s]
        xbs = [xs_ref[pl.ds(b, MOE_RB), :] for b in base]
        gates = [_dot(xb, w3_ref[e]) for xb, e in zip(xbs, experts)]
        ups = [_dot(xb, w1_ref[e]) for xb, e in zip(xbs, experts)]
        acts = [(_silu(g) * u).astype(BF16) for g, u in zip(gates, ups)]
        outs = [_dot(a, w2_ref[e]) for a, e in zip(acts, experts)]
        for b, r0, end, out in zip(base, r0s, ends, outs):
            row = row_in_blk + b
            keep = (row >= end) | (row < r0)
            ys_ref[pl.ds(b, MOE_RB), :] = jnp.where(keep, ys_ref[pl.ds(b, MOE_RB), :], out.astype(BF16))

    seg_start = [pl.multiple_of(start_i[0, e], SLOT_ALIGN) for e in range(N_EXPERTS)]
    seg_end = [seg_start[e] + cnt_i[0, e] for e in range(N_EXPERTS)]
    for e0 in range(0, N_EXPERTS, MOE_INTERLEAVE):
        es = list(range(e0, e0 + MOE_INTERLEAVE))
        expert_blocks(es, [seg_start[e] for e in es], [seg_end[e] for e in es])
    for e in range(N_EXPERTS):
        def extra_block(b, carry, e=e):
            expert_blocks([e], [pl.multiple_of(seg_start[e] + b * MOE_RB, SLOT_ALIGN)], [seg_end[e]])
            return carry

        lax.fori_loop(1, n_blk[0, e], extra_block, 0)

    col_slot = _iota((MOE_TM, MOE_SLOTS), 1).astype(F32)
    scatter = (jnp.where(col_slot == slot_a, w_a, 0.0) + jnp.where(col_slot == slot_b, w_b, 0.0)).astype(BF16)
    x2 = x1_ref[...] + mod_ref[0][5:6] * _dot(scatter, ys_ref[...])
    y = _rmsnorm_rows(x2) * fg_ref[...]

    def write(y_ref):
        y_ref[...] = y

    tiles.by_pass(functools.partial(write, yc_ref), functools.partial(write, yd_ref))


def _moe(tiles, h2, cmb, x1, mod, w1, w3, w2, final_g):
    assert tiles.tm == MOE_TM
    hbm = pl.BlockSpec(memory_space=pl.ANY)
    out_ctx, out_dec = tiles.specs(D_MODEL)
    return pl.pallas_call(
        functools.partial(_moe_kernel, tiles),
        grid=(tiles.n_ctx + tiles.n_dec,),
        in_specs=[tiles.merged(D_MODEL), tiles.merged(LANE), tiles.merged(D_MODEL), tiles.mod_spec(),
                  hbm, hbm, hbm, pl.BlockSpec((1, D_MODEL), lambda i: (0, 0))],
        out_specs=[out_ctx, out_dec],
        out_shape=[jax.ShapeDtypeStruct((tiles.n_ctx * MOE_TM, D_MODEL), F32),
                   jax.ShapeDtypeStruct((tiles.n_dec * MOE_TM, D_MODEL), F32)],
        scratch_shapes=[pltpu.VMEM((MOE_SLOTS, D_MODEL), BF16), pltpu.VMEM((MOE_SLOTS, D_MODEL), BF16),
                        pltpu.VMEM((N_EXPERTS, D_MODEL, D_EXPERT), BF16),
                        pltpu.VMEM((N_EXPERTS, D_MODEL, D_EXPERT), BF16),
                        pltpu.VMEM((N_EXPERTS, D_EXPERT, D_MODEL), BF16),
                        pltpu.VMEM((2, D_MODEL, D_EXPERT), F32), pltpu.VMEM((2, D_MODEL, D_EXPERT), F32),
                        pltpu.VMEM((2, D_EXPERT, D_MODEL), F32)] + [pltpu.SemaphoreType.DMA((2,))] * 3,
        compiler_params=_cparams(1),
        name="moe_experts",
    )(h2, cmb, x1, mod, w1, w3, w2, final_g)


def _pad_rows(x, rows):
    return jnp.pad(x, ((0, rows - x.shape[0]),) + ((0, 0),) * (x.ndim - 1))


def _pack_params(l, w_in, rw_conv, rw_w0, rw_w2, rw_a0, rw_a2, rw_g2, rw_k_k, rw_k_a, rw_r_k, rw_ln_w, rw_ln_b,
                 gla_gk2, gla_gk_b, gla_norm_g, moe_w_group, moe_b_group, moe_w_expert, moe_b_expert):
    wi = w_in[l]
    z = lambda n: jnp.zeros((D_MODEL, n), F32)
    w_in_t = jnp.swapaxes(wi, 0, 1)
    z64 = jnp.zeros((64, RW_W), F32)
    w2p = jnp.stack([jnp.concatenate([rw_w2[l, 0], z64], 0), jnp.concatenate([z64, rw_w2[l, 1]], 0)])
    a2p = jnp.stack([jnp.concatenate([rw_a2[l, 0], z64], 0), jnp.concatenate([z64, rw_a2[l, 1]], 0)])
    vec = _pad_rows(jnp.stack([rw_w0[l, 0], rw_w0[l, 1], rw_a0[l, 0], rw_a0[l, 1], rw_k_k[l], rw_k_a[l],
                               rw_r_k[l].reshape(RW_W), rw_ln_w[l], rw_ln_b[l]]), 16)
    rw = {'conv': rw_conv[l].reshape(9, 3 * RW_W), 'w2p': w2p.astype(BF16), 'a2p': a2p.astype(BF16),
          'g2': rw_g2[l].astype(BF16), 'vec': vec}
    gk2p = jnp.stack([_pad_rows(gla_gk2[l, 0], LANE),
                      _pad_rows(jnp.concatenate([jnp.zeros((16, GLA_QK_W), F32), gla_gk2[l, 1]], 0), LANE)])
    gk_b = jnp.pad(gla_gk_b[l], ((0, 0), (0, GLA_V_W - GLA_QK_W)))
    gvec = _pad_rows(jnp.concatenate([gk_b, jnp.tile(gla_norm_g[l], GLA_V_W // LANE)[None]], axis=0), 8)
    gla = {'gk2p': gk2p.astype(BF16), 'gvec': gvec}
    w_route = jnp.concatenate(_split2(jnp.concatenate([moe_w_expert[l], moe_w_group[l], z(LANE - N_EXPERTS - 4)],
                                                      axis=1)), axis=1)
    b_route = jnp.concatenate([moe_b_expert[l], moe_b_group[l], jnp.zeros((LANE - N_EXPERTS - 4,), F32)])[None]
    return w_in_t, rw, gla, w_route, b_route


def kernel(x_prompt, x_sample, state_rwkv, state_gla, c, c_ctx, norm1_g, norm2_g, w_ada, b_ada, w_in, w_out,
           rw_conv, rw_w0, rw_w2, rw_a0, rw_a2, rw_g2, rw_k_k, rw_k_a, rw_r_k, rw_ln_w, rw_ln_b,
           gla_gk2, gla_gk_b, gla_norm_g, moe_w_group, moe_b_group, moe_w_expert, moe_b_expert,
           moe_w1, moe_w3, moe_w2, final_g):
    depth = w_in.shape[0]
    assert depth == 1, "the packed layout below handles the single-layer trunk of this problem"
    l = 0
    n_dec = x_sample.shape[0]
    ctx_row = n_dec
    cond8 = _pad_rows(jnp.concatenate([c, c_ctx[None]], axis=0), 8)
    mod = _modulation(cond8, w_ada[l], b_ada[l][None]).reshape(8, N_MOD, D_MODEL)
    pk = _pack_params(l, w_in, rw_conv, rw_w0, rw_w2, rw_a0, rw_a2, rw_g2, rw_k_k, rw_k_a, rw_r_k, rw_ln_w,
                      rw_ln_b, gla_gk2, gla_gk_b, gla_norm_g, moe_w_group, moe_b_group, moe_w_expert,
                      moe_b_expert)
    w_in_t, rw, gla, w_route, b_route = pk

    n_ctx, ctx_len, _ = x_prompt.shape
    dec_len = x_sample.shape[1]
    x_ctx = x_prompt.reshape(n_ctx * ctx_len, D_MODEL)
    x_dec = x_sample.reshape(n_dec * dec_len, D_MODEL)
    assert (n_ctx * ctx_len) % dec_len == 0, "denoising sequences must start on a dec_len row block of proj"
    first_dec = n_ctx * ctx_len // dec_len
    tiles = _Tiles(n_ctx * ctx_len, n_dec * dec_len, dec_len, ctx_row, PROJ_TM)
    assert PROJ_TM == OUT_TM == MOE_TM

    proj = _in_projection(tiles, x_ctx, x_dec, mod, norm1_g[l][None], w_in_t)
    y_rw_c, s_rw = _rwkv_mixer(proj, 0, n_ctx, ctx_len, False, rw, None)
    y_gla_c, s_gla = _gla_mixer(proj, 0, n_ctx, ctx_len, gla, None)
    y_rw_d, _ = _rwkv_mixer(proj, first_dec, n_dec, dec_len, True, rw, state_rwkv[:, l])
    y_gla_d, _ = _gla_mixer(proj, first_dec, n_dec, dec_len, gla, state_gla[:, l])
    x1, h2, cmb = _out_projection(tiles, (y_rw_c, y_rw_d), (y_gla_c, y_gla_d), (x_ctx, x_dec), mod,
                                  w_out[l].astype(BF16), norm2_g[l][None], w_route, b_route)
    y_ctx, y_dec = _moe(tiles, h2, cmb, x1, mod, moe_w1[l], moe_w3[l], moe_w2[l], final_g[None])
    return (y_ctx.reshape(x_prompt.shape), y_dec.reshape(x_sample.shape), s_rw[:, None], s_gla[:, None])
```

```python
import functools
import itertools

import jax
import jax.numpy as jnp
from jax import lax
from jax.experimental import pallas as pl
from jax.experimental.pallas import tpu as pltpu

F32 = jnp.float32
BF16 = jnp.bfloat16

D_MODEL = 1024
RW_W = 512
GLA_V_W = 512
GLA_QK_W = 256
N_EXPERTS = 16
D_EXPERT = 256
N_MOD = 6
EPS = 1e-6
RW_LN_EPS = 64e-5
RW_DECAY_SCALE = 0.606531
GLA_GATE_NORM = 16.0
GLA_Q_SCALE = 64 ** -0.5
GRID_W = 64

LANE = 128
CHUNK = 64
CONV_PAD = 128
TERM_UNROLL = 4
RW_PAIRS_PER_STEP = 2
GLA_PAIRS = 2
RW_GROUPS = 4
D_PROJ = 28 * LANE
VMEM_LIMIT = 56 * 1024 * 1024

CB_R, CB_K, CB_V, CB_LORA, CB_LGK, CB_GQ, CB_GK, CB_GV, CB_OG = 0, 4, 8, 12, 15, 16, 18, 20, 24

_NN = (((1,), (0,)), ((), ()))
_NT = (((1,), (1,)), ((), ()))
_TN = (((0,), (0,)), ((), ()))


def _dot(a, b, dims=_NN):
    return lax.dot_general(a, b, dims, preferred_element_type=F32)


def _mm(a, b, dims=_NN):
    return _dot(a.astype(BF16), b.astype(BF16), dims)


def _split2(x):
    hi = x.astype(BF16)
    lo = (x - hi.astype(F32)).astype(BF16)
    return hi, lo


def _mm3(a, b, dims=_NN):
    ah, al = _split2(a)
    bh, bl = _split2(b)
    return _dot(ah, bh, dims) + _dot(ah, bl, dims) + _dot(al, bh, dims)


def _mm_01_lhs(a01, b, dims=_NN):
    n = b.shape[1]
    both = _dot(a01, jnp.concatenate(_split2(b), axis=1), dims)
    return both[:, 0:n] + both[:, n:2 * n]


def _sigmoid(x):
    return 0.5 * jnp.tanh(0.5 * x) + 0.5


def _silu(x):
    return x * _sigmoid(x)


def _log_sigmoid(x):
    return jnp.minimum(x, 0.0) - jnp.log(1.0 + jnp.exp(-jnp.abs(x)))


def _iota(shape, dim):
    return lax.broadcasted_iota(jnp.int32, shape, dim)


def _cparams(n_axes):
    return pltpu.CompilerParams(dimension_semantics=("arbitrary",) * n_axes, vmem_limit_bytes=VMEM_LIMIT)


MOD_TN = 768


def _mod_kernel(c_ref, w_ref, b_ref, o_ref):
    o_ref[...] = _mm3(_silu(c_ref[...]), w_ref[...]) + b_ref[...]


def _modulation(cond8, w_ada, b_ada):
    n = w_ada.shape[1]
    return pl.pallas_call(
        _mod_kernel,
        grid=(n // MOD_TN,),
        in_specs=[pl.BlockSpec((8, D_MODEL), lambda j: (0, 0)),
                  pl.BlockSpec((D_MODEL, MOD_TN), lambda j: (0, j)),
                  pl.BlockSpec((1, MOD_TN), lambda j: (0, j))],
        out_specs=pl.BlockSpec((8, MOD_TN), lambda j: (0, j)),
        out_shape=jax.ShapeDtypeStruct((8, n), F32),
        compiler_params=_cparams(1),
        name="adaln_mod",
    )(cond8, w_ada, b_ada)


PROJ_TM = 512
D_IN = 3488
N_LGK = 32


def _rmsnorm_rows(x):
    return x * lax.rsqrt(jnp.mean(x * x, axis=-1, keepdims=True) + EPS)


class _Tiles:
    def __init__(self, n_ctx_tokens, n_dec_tokens, dec_seq_len, ctx_row, tm):
        self.tm = tm
        self.n_ctx = n_ctx_tokens // tm
        self.n_dec = n_dec_tokens // tm
        self.per_seq = dec_seq_len // tm
        self.ctx_row = ctx_row

    def specs(self, width):
        last_ctx = self.n_ctx - 1
        n_ctx = self.n_ctx
        return (pl.BlockSpec((self.tm, width), lambda i: (jnp.minimum(i, last_ctx), 0)),
                pl.BlockSpec((self.tm, width), lambda i: (jnp.maximum(i - n_ctx, 0), 0)))

    def merged(self, width):
        return pl.BlockSpec((self.tm, width), lambda i: (i, 0))

    def mod_spec(self):
        n_ctx, per_seq, ctx_row = self.n_ctx, self.per_seq, self.ctx_row
        return pl.BlockSpec((1, N_MOD, D_MODEL),
                            lambda i: (jnp.where(i < n_ctx, ctx_row, (i - n_ctx) // per_seq), 0, 0))

    def by_pass(self, run_ctx, run_dec):
        i = pl.program_id(0)
        pl.when(i < self.n_ctx)(run_ctx)
        pl.when(i >= self.n_ctx)(run_dec)


def _inproj_kernel(tiles, xc_ref, xd_ref, mod_ref, g_ref, wt_ref, o_ref, w_ref):
    @pl.when(pl.program_id(0) == 0)
    def _():
        for j in range(D_PROJ // LANE):
            if j == CB_LGK:
                blk = jnp.concatenate([wt_ref[D_IN - N_LGK:D_IN, :], jnp.zeros((LANE - N_LGK, D_MODEL), F32)], axis=0)
            else:
                src = j if j < CB_LGK else j - 1
                blk = wt_ref[src * LANE:(src + 1) * LANE, :]
            w_ref[:, j * LANE:(j + 1) * LANE] = blk.T.astype(BF16)

    def run(x_ref):
        m = mod_ref[0]
        h = _rmsnorm_rows(x_ref[...]) * g_ref[...] * (1.0 + m[1:2]) + m[0:1]
        o_ref[...] = _mm(h, w_ref[...])

    tiles.by_pass(functools.partial(run, xc_ref), functools.partial(run, xd_ref))


def _in_projection(tiles, x_ctx, x_dec, mod, norm_g, w_in_t):
    full = lambda a: pl.BlockSpec(a.shape, lambda i: (0,) * a.ndim)
    return pl.pallas_call(
        functools.partial(_inproj_kernel, tiles),
        grid=(tiles.n_ctx + tiles.n_dec,),
        in_specs=[*tiles.specs(D_MODEL), tiles.mod_spec(), full(norm_g),
                  pl.BlockSpec(w_in_t.shape, lambda i: (0, 0), pipeline_mode=pl.Buffered(1))],
        out_specs=tiles.merged(D_PROJ),
        out_shape=jax.ShapeDtypeStruct((x_ctx.shape[0] + x_dec.shape[0], D_PROJ), F32),
        scratch_shapes=[pltpu.VMEM((D_MODEL, D_PROJ), BF16)],
        compiler_params=_cparams(1),
        name="in_proj",
    )(x_ctx, x_dec, mod, norm_g, w_in_t)


def _time_masks(reverse):
    r = _iota((2 * CHUNK, 2 * CHUNK), 0) % CHUNK
    c = _iota((2 * CHUNK, 2 * CHUNK), 1) % CHUNK
    if reverse:
        return r < c, r <= c
    return r > c, r >= c


def _cumsum_matrix(reverse):
    r = _iota((CHUNK, CHUNK), 0)
    c = _iota((CHUNK, CHUNK), 1)
    tri = (r <= c) if reverse else (r >= c)
    return tri.astype(BF16)


def _stack_heads(x, half):
    m0 = _iota(x.shape, 1) < half
    return jnp.concatenate([jnp.where(m0, x, 0.0), jnp.where(m0, 0.0, x)], axis=0)


def _finished_before_last_scan(groups, n_chunks):
    done_fwd, done_bwd = set(), set()
    for s, fwd, bwd in groups[:-1]:
        done_fwd.update(s * n_chunks + c for c in fwd)
        done_bwd.update(s * n_chunks + c for c in bwd)
    return sorted(done_fwd & done_bwd)


def _head_sums(x):
    parts = []
    for p in range(x.shape[1] // LANE):
        xp = x[:, p * LANE:(p + 1) * LANE]
        m0 = _iota(xp.shape, 1) < 64
        s0 = jnp.sum(jnp.where(m0, xp, 0.0), axis=-1, keepdims=True)
        s1 = jnp.sum(jnp.where(m0, 0.0, xp), axis=-1, keepdims=True)
        parts.append(jnp.where(m0, s0, s1))
    return parts[0] if len(parts) == 1 else jnp.concatenate(parts, axis=1)


def _rwkv_chunk_terms(insts, interleaved=()):
    c = CHUNK
    step_row = _iota((c, LANE), 0)
    step_col = _iota((c, LANE), 1) % c
    eye_w = (step_row == step_col).astype(F32)
    same_head = (_iota((LANE, LANE), 0) // 64) == (_iota((LANE, LANE), 1) // 64)
    stack_bf = lambda x: _stack_heads(x, 64).astype(BF16)
    cums = [_mm_01_lhs(_cumsum_matrix(rev), lw) for (_, lw, _, _, _, _, rev) in insts]
    pre = []
    for (r, lw, kd, a, b, v, rev), cum in zip(insts, cums):
        end = cum[0:1] if rev else cum[c - 1:c]
        inv_w = jnp.exp(-cum)
        rem_w = jnp.exp(end - cum)
        a_t = a * jnp.exp(cum - lw)
        r_t = r * jnp.exp(cum)
        bk_s = jnp.concatenate([stack_bf(b * inv_w), stack_bf(kd * inv_w)], axis=0)
        bkh = jnp.concatenate([b * rem_w, kd * rem_w], axis=0).astype(BF16)
        pre.append((a_t, r_t, bk_s, bkh, v, jnp.exp(end)))
    ms = [_dot(jnp.concatenate([a_t, r_t], axis=0).astype(BF16), bk_s, _NT) for (a_t, r_t, bk_s, _, _, _) in pre]
    mats = []
    for m, (_, _, _, _, _, _, rev) in zip(ms, insts):
        strict = (step_row < step_col) if rev else (step_row > step_col)
        incl = (step_row <= step_col) if rev else (step_row >= step_col)
        l_ab = jnp.where(strict, m[0:c, 0:LANE], 0.0)
        l_akrk = jnp.concatenate([jnp.where(strict, m[0:c, LANE:2 * LANE], 0.0),
                                  jnp.where(incl, m[c:2 * c, LANE:2 * LANE], 0.0)], axis=0).astype(BF16)
        m_rb = jnp.where(incl, m[c:2 * c, 0:LANE], 0.0).astype(BF16)
        mats.append((l_ab, l_akrk, m_rb))
    pending = list(interleaved)

    def run_interleaved():
        if pending:
            pending.pop(0)()

    lvs = [_dot(l_akrk, stack_bf(pr[4])) for (_, l_akrk, _), pr in zip(mats, pre)]
    run_interleaved()
    ps = [eye_w + l_ab for (l_ab, _, _) in mats]
    lps = [_dot(l_ab.astype(BF16), stack_bf(l_ab)) for (l_ab, _, _) in mats]
    for level in range(1, 6):
        if level < 5:
            xs = [_dot(lp.astype(BF16), jnp.concatenate([stack_bf(p), stack_bf(lp)], axis=1))
                  for lp, p in zip(lps, ps)]
            ps = [p + x[:, 0:LANE] for p, x in zip(ps, xs)]
            lps = [x[:, LANE:2 * LANE] for x in xs]
        else:
            ps = [p + _dot(lp.astype(BF16), stack_bf(p)) for lp, p in zip(lps, ps)]
        if level in (2, 4):
            run_interleaved()
    pxs = [_dot(p.astype(BF16), jnp.concatenate([stack_bf(pr[0]), stack_bf(lv[0:c])], axis=1))
           for p, pr, lv in zip(ps, pre, lvs)]
    run_interleaved()
    mzs = [_dot(mt[2], jnp.concatenate([stack_bf(px[:, 0:LANE]), stack_bf(px[:, LANE:2 * LANE])], axis=1))
           for mt, px in zip(mats, pxs)]
    ts = [_dot(px[:, 0:LANE].astype(BF16), pr[3][0:c], _TN) for px, pr in zip(pxs, pre)]
    gs = [_dot(jnp.concatenate([px[:, LANE:2 * LANE], pr[4]], axis=0).astype(BF16), pr[3], _TN)
          for px, pr in zip(pxs, pre)]
    while pending:
        run_interleaved()
    out = []
    for pr, lv, mz, t, g in zip(pre, lvs, mzs, ts, gs):
        q = pr[1] + mz[:, 0:LANE]
        y0 = mz[:, LANE:2 * LANE] + lv[c:2 * c]
        g_wide = jnp.where(_iota((c, LANE), 1) < 64, g[0:c], g[c:2 * c])
        out.append((jnp.where(same_head, t, 0.0).astype(BF16), g_wide, pr[5],
                    q.astype(BF16), y0))
    return out


def _gla_chunk_terms(insts, interleaved=()):
    c = CHUNK
    step_row = _iota((c, LANE), 0)
    step_col = _iota((c, LANE), 1) % c
    same_head = (_iota((LANE, 2 * LANE), 0) // 64) == (_iota((LANE, 2 * LANE), 1) // LANE)
    cums = [_mm_01_lhs(_cumsum_matrix(rev), g) for (_, _, _, g, rev) in insts]
    pre = []
    for (q, k, v, g, rev), cum in zip(insts, cums):
        end = cum[0:1] if rev else cum[c - 1:c]
        qt = (q * jnp.exp(cum)).astype(BF16)
        k_s = _stack_heads(k * jnp.exp(-cum), 64).astype(BF16)
        kh = (k * jnp.exp(end - cum)).astype(BF16)
        a_col = jnp.broadcast_to(jnp.exp(end), (LANE, LANE)).T
        pre.append((qt, k_s, kh, v.astype(BF16), _stack_heads(v, LANE).astype(BF16), a_col))
    pending = list(interleaved)

    def run_interleaved():
        if pending:
            pending.pop(0)()

    run_interleaved()
    atts = [_dot(pr[0], pr[1], _NT) for pr in pre]
    run_interleaved()
    atts = [jnp.where((step_row <= step_col) if inst[4] else (step_row >= step_col), att, 0.0).astype(BF16)
            for att, inst in zip(atts, insts)]
    o0s = [_dot(att, pr[4]) for att, pr in zip(atts, pre)]
    run_interleaved()
    kvs = [jnp.where(same_head, _dot(pr[2], pr[3], _TN), 0.0) for pr in pre]
    while pending:
        run_interleaved()
    return [(pr[0], o0, pr[5], kv) for pr, o0, kv in zip(pre, o0s, kvs)]


def _rwkv_kernel(seq_len, nseq, is_grid, zero_init, r_ref, k_ref, v_ref, lora_ref, cwr_ref, cwk_ref, cwv_ref,
                 w2_ref, a2_ref, g2_ref, vec_ref, *rest):
    s0_ref = None if zero_init else rest[0]
    (y_ref, sout_ref, pad_ref, rs_ref, ks_ref, vs_ref, kk_ref, bonus_ref, gate_ref, yf_ref, yb_ref, st_ref, tt_ref,
     tg_ref, tw_ref, tq_ref, ty_ref) = rest[0 if zero_init else 1:][:17]
    left_ref, right_ref = rest[-2:] if is_grid else (None, None)
    n_chunks = seq_len // CHUNK
    npp = RW_PAIRS_PER_STEP
    w = npp * LANE
    pair = lambda x, p: x[:, p * LANE:(p + 1) * LANE]
    vec = vec_ref[...]
    w0 = (vec[0:1], vec[1:2])
    a0 = (vec[2:3], vec[3:4])
    k_k, k_a, r_k, ln_w, ln_b = vec[4:5], vec[5:6], vec[6:7], vec[7:8], vec[8:9]
    block_sum = _head_sums
    chains = [(d, p) for d in range(2) for p in range(npp)]
    rows_of = lambda s, c: slice((s * n_chunks + c) * CHUNK, (s * n_chunks + c + 1) * CHUNK)

    if not is_grid:
        zeros = jnp.zeros((CONV_PAD, 3 * w), F32)
        pad_ref[0:CONV_PAD, :] = zeros
        pad_ref[CONV_PAD + seq_len:2 * CONV_PAD + seq_len, :] = zeros
    cw = jnp.concatenate([cwr_ref[...], cwk_ref[...], cwv_ref[...]], axis=1)
    col = _iota((CHUNK, 3 * w), 0)
    rkv_rows = lambda rows: jnp.concatenate([r_ref[rows, :], k_ref[rows, :], v_ref[rows, :]], axis=1)

    def shift_body(c, carry):
        rows = pl.ds(pl.multiple_of(c * CHUNK, CHUNK), CHUNK)
        x = rkv_rows(rows)
        left_ref[rows, :] = jnp.where(col >= 1, pltpu.roll(x, 1, 0), 0.0)
        right_ref[rows, :] = jnp.where(col <= GRID_W - 2, pltpu.roll(x, CHUNK - 1, 0), 0.0)
        return carry

    def conv_chunk(c):
        base = CONV_PAD + c * CHUNK
        acc = jnp.zeros((CHUNK, 3 * w), F32)
        if is_grid:
            for di in (-1, 0, 1):
                if not 0 <= c + di < n_chunks:
                    continue
                rows = slice((c + di) * CHUNK, (c + di + 1) * CHUNK)
                for dj, src in ((-1, left_ref[rows, :]), (0, rkv_rows(rows)), (1, right_ref[rows, :])):
                    tap = (di + 1) * 3 + dj + 1
                    acc = acc + src * cw[tap:tap + 1]
        else:
            win = pad_ref[pl.ds(base - 8, CHUNK + 16), :]
            for dj in (-1, 0, 1):
                acc = acc + win[8 + dj:8 + dj + CHUNK] * cw[4 + dj:5 + dj]
        return acc[:, 0:w], acc[:, w:2 * w], acc[:, 2 * w:3 * w]

    def load_sequence(s):
        seq_rows = slice(s * seq_len, (s + 1) * seq_len)
        pad_ref[CONV_PAD:CONV_PAD + seq_len, 0:w] = r_ref[seq_rows, :]
        pad_ref[CONV_PAD:CONV_PAD + seq_len, w:2 * w] = k_ref[seq_rows, :]
        pad_ref[CONV_PAD:CONV_PAD + seq_len, 2 * w:3 * w] = v_ref[seq_rows, :]

    def conv_store(s, c):
        rows = rows_of(s, c)
        rc, kc, vc = conv_chunk(c)
        kk = kc * k_k
        rs_ref[rows, :] = rc
        ks_ref[rows, :] = kc
        vs_ref[rows, :] = vc
        kk_ref[rows, :] = kk * lax.rsqrt(block_sum(kk * kk) + EPS)
        bonus_ref[rows, :] = block_sum(rc * kc * r_k) * vc
        gate_ref[rows, :] = _mm(_sigmoid(lora_ref[rows, 2 * LANE:3 * LANE]), g2_ref[...])

    for s in range(nseq):
        for d, p in chains:
            if zero_init:
                st_ref[s, d, p] = jnp.zeros((64, LANE), F32)
            else:
                st_ref[s, d, p] = jnp.concatenate([s0_ref[s, d, 2 * p], s0_ref[s, d, 2 * p + 1]], axis=1)

    if nseq == 1:
        groups = [(0, list(range(g * TERM_UNROLL, (g + 1) * TERM_UNROLL)),
                   list(range(n_chunks - 1 - g * TERM_UNROLL, n_chunks - 1 - (g + 1) * TERM_UNROLL, -1)))
                  for g in range(n_chunks // TERM_UNROLL)]
    else:
        groups = [(s, list(range(n_chunks)), list(range(n_chunks - 1, -1, -1))) for s in range(nseq)]

    def group_terms(group, interleaved):
        s, fwd, bwd = group
        insts, where, lora_in = [], [], {}
        for d, chunks in ((0, fwd), (1, bwd)):
            for c in chunks:
                rows = rows_of(s, c)
                if c not in lora_in:
                    lora_in[c] = (jnp.tanh(lora_ref[rows, 0:LANE]).astype(BF16),
                                  lora_ref[rows, LANE:2 * LANE].astype(BF16))
                lw = -RW_DECAY_SCALE * _sigmoid(w0[d] + _dot(lora_in[c][0], w2_ref[d]))
                ag = _sigmoid(a0[d] + _dot(lora_in[c][1], a2_ref[d]))
                rc, kc, vc, kk = rs_ref[rows, :], ks_ref[rows, :], vs_ref[rows, :], kk_ref[rows, :]
                kd = kc * (1.0 + (ag - 1.0) * k_a)
                kb = kk * ag
                for p in range(npp):
                    insts.append((pair(rc, p), pair(lw, p), pair(kd, p), -pair(kk, p), pair(kb, p), pair(vc, p),
                                  d == 1))
                    where.append((d, p, s * n_chunks + c))
        for (d, p, gc), (t, g, w_end, q, y0) in zip(where, _rwkv_chunk_terms(insts, interleaved)):
            tt_ref[d, p, gc] = t
            tg_ref[d, p, gc] = g
            tw_ref[d, p, gc] = jnp.broadcast_to(w_end, (8, LANE))
            tq_ref[d, p, gc] = q
            ty_ref[d, p, gc] = y0

    def scan_step(s, chunk_of_dir):
        gcs = [s * n_chunks + chunk_of_dir[d] for d, _ in chains]
        ss = [st_ref[s, d, p] for d, p in chains]
        ys = [_dot(tq_ref[d, p, gc], _stack_heads(x, 64).astype(BF16), _NT) + ty_ref[d, p, gc]
              for (d, p), gc, x in zip(chains, gcs, ss)]
        sn = [x * tw_ref[d, p, gc][0:1] + _dot(x.astype(BF16), tt_ref[d, p, gc]) + tg_ref[d, p, gc]
              for (d, p), gc, x in zip(chains, gcs, ss)]
        for (d, p), gc, y, x in zip(chains, gcs, ys, sn):
            st_ref[s, d, p] = x
            out_ref = yf_ref if d == 0 else yb_ref
            out_ref[gc * CHUNK:(gc + 1) * CHUNK, p * LANE:(p + 1) * LANE] = y

    def group_scan(group):
        s, fwd, bwd = group
        return [functools.partial(scan_step, s, (cf, cb)) for cf, cb in zip(fwd, bwd)]

    conv_done, seq_loaded = set(), set()

    def group_conv(group):
        s, fwd, bwd = group
        todo = [c for c in sorted(set(fwd) | set(bwd)) if (s, c) not in conv_done]
        conv_done.update((s, c) for c in todo)
        thunks = []
        for i in range(0, len(todo), max(1, -(-len(todo) // TERM_UNROLL))):
            part = todo[i:i + max(1, -(-len(todo) // TERM_UNROLL))]
            need_load = s not in seq_loaded
            seq_loaded.add(s)

            def run(part=part, need_load=need_load):
                if need_load:
                    assert not is_grid or nseq == 1
                    load_sequence(s)
                for c in part:
                    conv_store(s, c)

            thunks.append(run)
        return thunks

    def merge(*lists):
        n = max(len(lst) for lst in lists)
        pick = lambda lst, i: lst[i] if i < len(lst) else (lambda: None)
        return [lambda i=i: [pick(lst, i)() for lst in lists] for i in range(n)]

    def post(offs):
        ys = [yf_ref[pl.ds(off, CHUNK), :] + yb_ref[pl.ds(off, CHUNK), :] for off in offs]
        mus = [block_sum(y) * (1.0 / 64) for y in ys]
        dlts = [y - mu for y, mu in zip(ys, mus)]
        vrs = [block_sum(dlt * dlt) * (1.0 / 64) for dlt in dlts]
        for off, dlt, var in zip(offs, dlts, vrs):
            yn = dlt * lax.rsqrt(var + RW_LN_EPS) * ln_w + ln_b
            y = (yn + bonus_ref[pl.ds(off, CHUNK), :]) * gate_ref[pl.ds(off, CHUNK), :]
            y_ref[pl.ds(off, CHUNK), :] = y.astype(y_ref.dtype)

    def sequence_post(s):
        return [functools.partial(post, [(s * n_chunks + c) * CHUNK]) for c in range(n_chunks)]

    if is_grid:
        assert nseq == 1
        seq_loaded.add(0)
        lax.fori_loop(0, n_chunks, shift_body, 0)
    for thunk in group_conv(groups[0]):
        thunk()
    posted = set()
    for g, group in enumerate(groups):
        scans = group_scan(groups[g - 1]) if g else []
        convs = group_conv(groups[g + 1]) if g + 1 < len(groups) else []
        posts = []
        if nseq > 1 and g >= 2:
            s_done = groups[g - 2][0]
            posts = sequence_post(s_done)
            posted.update(range(s_done * n_chunks, (s_done + 1) * n_chunks))
        group_terms(group, merge(scans, convs, posts))
    early = [gc for gc in _finished_before_last_scan(groups, n_chunks) if gc not in posted]
    last_scan = group_scan(groups[-1])
    per_step = max(1, -(-len(early) // len(last_scan)))
    early_posts = [functools.partial(post, [gc * CHUNK for gc in early[i:i + per_step]])
                   for i in range(0, len(early), per_step)]
    for step, extra in itertools.zip_longest(last_scan, early_posts, fillvalue=lambda: None):
        step()
        extra()
    posted.update(early)
    for s in range(nseq):
        for d, p in chains:
            x = st_ref[s, d, p]
            sout_ref[s, d, 2 * p] = x[:, 0:64]
            sout_ref[s, d, 2 * p + 1] = x[:, 64:LANE]

    left = [gc for gc in range(nseq * n_chunks) if gc not in posted]
    for i in range(0, len(left), TERM_UNROLL):
        post([gc * CHUNK for gc in left[i:i + TERM_UNROLL]])


def _rwkv_mixer(proj, first_seq, n_seq, seq_len, is_grid, prm, s0):
    n_pairs = RW_W // LANE
    n_chunks = seq_len // CHUNK
    npp = RW_PAIRS_PER_STEP
    w = npp * LANE
    nseq = max(1, RW_GROUPS * TERM_UNROLL // n_chunks)
    assert n_chunks % TERM_UNROLL == 0 and n_pairs % npp == 0
    assert (n_chunks == TERM_UNROLL or nseq == 1) and n_seq % nseq == 0 and first_seq % nseq == 0
    rows = nseq * seq_len
    total_chunks = nseq * n_chunks
    first = first_seq // nseq
    col = lambda cb: (lambda b, p: (b + first, cb // npp + p))
    par = lambda cb: (lambda b, p: (0, cb // npp + p))
    state_spec = pl.BlockSpec((nseq, 2, 2 * npp, 64, 64), lambda b, p: (b, 0, p, 0, 0))
    kernel = functools.partial(_rwkv_kernel, seq_len, nseq, is_grid, s0 is None)
    y, s_out = pl.pallas_call(
        kernel,
        grid=(n_seq // nseq, n_pairs // npp),
        in_specs=[pl.BlockSpec((rows, w), col(CB_R)),
                  pl.BlockSpec((rows, w), col(CB_K)),
                  pl.BlockSpec((rows, w), col(CB_V)),
                  pl.BlockSpec((rows, 3 * LANE), lambda b, p: (b + first, CB_LORA // 3)),
                  pl.BlockSpec((9, w), par(CB_R)),
                  pl.BlockSpec((9, w), par(CB_K)),
                  pl.BlockSpec((9, w), par(CB_V)),
                  pl.BlockSpec((2, LANE, w), lambda b, p: (0, 0, p)),
                  pl.BlockSpec((2, LANE, w), lambda b, p: (0, 0, p)),
                  pl.BlockSpec((LANE, w), lambda b, p: (0, p)),
                  pl.BlockSpec((16, w), lambda b, p: (0, p))] + ([] if s0 is None else [state_spec]),
        out_specs=[pl.BlockSpec((rows, w), lambda b, p: (b, p)), state_spec],
        out_shape=[jax.ShapeDtypeStruct((n_seq * seq_len, RW_W), BF16),
                   jax.ShapeDtypeStruct((n_seq, 2, 2 * n_pairs, 64, 64), F32)],
        scratch_shapes=[pltpu.VMEM((8, LANE) if is_grid else (seq_len + 2 * CONV_PAD, 3 * w), F32)]
                       + [pltpu.VMEM((rows, w), F32)] * 8
                       + [pltpu.VMEM((nseq, 2, npp, 64, LANE), F32),
                          pltpu.VMEM((2, npp, total_chunks, LANE, LANE), BF16),
                          pltpu.VMEM((2, npp, total_chunks, 64, LANE), F32),
                          pltpu.VMEM((2, npp, total_chunks, 8, LANE), F32),
                          pltpu.VMEM((2, npp, total_chunks, CHUNK, LANE), BF16),
                          pltpu.VMEM((2, npp, total_chunks, CHUNK, LANE), F32)]
                       + ([pltpu.VMEM((seq_len, 3 * w), F32)] * 2 if is_grid else []),
        compiler_params=_cparams(2),
        name="rwkv_mixer",
    )(proj, proj, proj, proj, prm['conv'], prm['conv'], prm['conv'], prm['w2p'], prm['a2p'], prm['g2'],
      prm['vec'], *([] if s0 is None else [s0]))
    return y, s_out


def _gla_kernel(seq_len, nseq, zero_init, q_ref, k_ref, v_ref, og_ref, lgk_ref, gk2_ref, gvec_ref, *rest):
    s0_ref = None if zero_init else rest[0]
    y_ref, sout_ref, of_ref, ob_ref, st_ref, tq_ref, to_ref, ta_ref, tkv_ref = rest[0 if zero_init else 1:]
    n_chunks = seq_len // CHUNK
    npp = GLA_PAIRS
    gvec = gvec_ref[...]
    chains = [(d, p) for d in range(2) for p in range(npp)]

    for s in range(nseq):
        for d, p in chains:
            if zero_init:
                st_ref[s, d, p] = jnp.zeros((LANE, 2 * LANE), F32)
            else:
                z = jnp.zeros((64, LANE), F32)
                st_ref[s, d, p] = jnp.concatenate([jnp.concatenate([s0_ref[s, d, 2 * p], z], axis=1),
                                                   jnp.concatenate([z, s0_ref[s, d, 2 * p + 1]], axis=1)], axis=0)

    if nseq == 1:
        groups = [(0, list(range(g * TERM_UNROLL, (g + 1) * TERM_UNROLL)),
                   list(range(n_chunks - 1 - g * TERM_UNROLL, n_chunks - 1 - (g + 1) * TERM_UNROLL, -1)))
                  for g in range(n_chunks // TERM_UNROLL)]
    else:
        groups = [(s, list(range(n_chunks)), list(range(n_chunks - 1, -1, -1))) for s in range(nseq)]

    def group_terms(group, interleaved):
        s, fwd, bwd = group
        insts, where, lgk_in = [], [], {}
        for d, chunks in ((0, fwd), (1, bwd)):
            for c in chunks:
                gc = s * n_chunks + c
                rows = slice(gc * CHUNK, (gc + 1) * CHUNK)
                if c not in lgk_in:
                    lgk_in[c] = lgk_ref[rows, :].astype(BF16)
                x = _dot(lgk_in[c], gk2_ref[d]) + gvec[d:d + 1, 0:GLA_QK_W]
                g = _log_sigmoid(x) * (1.0 / GLA_GATE_NORM)
                qc = q_ref[rows, :] * GLA_Q_SCALE
                kc = k_ref[rows, :]
                vc = v_ref[rows, :]
                for p in range(npp):
                    qk = slice(p * LANE, (p + 1) * LANE)
                    insts.append((qc[:, qk], kc[:, qk], vc[:, 2 * p * LANE:2 * (p + 1) * LANE], g[:, qk], d == 1))
                    where.append((d, p, gc))
        for (d, p, gc), (qt, o0, a_col, kv) in zip(where, _gla_chunk_terms(insts, interleaved)):
            tq_ref[d, p, gc] = qt
            to_ref[d, p, gc] = o0
            ta_ref[d, p, gc] = a_col
            tkv_ref[d, p, gc] = kv

    def scan_step(s, chunk_of_dir):
        gcs = [s * n_chunks + chunk_of_dir[d] for d, _ in chains]
        ss = [st_ref[s, d, p] for d, p in chains]
        os_ = [_dot(tq_ref[d, p, gc], x.astype(BF16)) + to_ref[d, p, gc] for (d, p), gc, x in zip(chains, gcs, ss)]
        for (d, p), gc, x, o in zip(chains, gcs, ss, os_):
            a_col = ta_ref[d, p, gc]
            st_ref[s, d, p] = x * jnp.concatenate([a_col, a_col], axis=1) + tkv_ref[d, p, gc]
            out_ref = of_ref if d == 0 else ob_ref
            out_ref[gc * CHUNK:(gc + 1) * CHUNK, 2 * p * LANE:2 * (p + 1) * LANE] = o

    def group_scan(group):
        s, fwd, bwd = group
        return [functools.partial(scan_step, s, (cf, cb)) for cf, cb in zip(fwd, bwd)]

    def post(off):
        for h in range(2 * npp):
            hs = slice(h * LANE, (h + 1) * LANE)
            o = of_ref[pl.ds(off, CHUNK), hs] + ob_ref[pl.ds(off, CHUNK), hs]
            gate = _silu(og_ref[pl.ds(off, CHUNK), hs])
            y = _rmsnorm_rows(o) * gvec[2:3, hs] * gate
            y_ref[pl.ds(off, CHUNK), hs] = y.astype(y_ref.dtype)

    posted = set()
    for g, group in enumerate(groups):
        thunks = group_scan(groups[g - 1]) if g else []
        if nseq > 1 and g >= 2:
            s_done = groups[g - 2][0]
            posts = [functools.partial(post, (s_done * n_chunks + c) * CHUNK) for c in range(n_chunks)]
            thunks = [lambda a=a, b=b: (a(), b()) for a, b in zip(thunks, posts)]
            posted.update(range(s_done * n_chunks, (s_done + 1) * n_chunks))
        group_terms(group, thunks)
    early = [gc for gc in _finished_before_last_scan(groups, n_chunks) if gc not in posted]
    last_scan = group_scan(groups[-1])
    per_step = max(1, -(-len(early) // len(last_scan)))
    for i, step in enumerate(last_scan):
        step()
        for gc in early[i * per_step:(i + 1) * per_step]:
            post(gc * CHUNK)
    for gc in early[len(last_scan) * per_step:]:
        post(gc * CHUNK)
    posted.update(early)
    for s in range(nseq):
        for d, p in chains:
            x = st_ref[s, d, p]
            sout_ref[s, d, 2 * p] = x[0:64, 0:LANE]
            sout_ref[s, d, 2 * p + 1] = x[64:LANE, LANE:2 * LANE]
    for gc in range(nseq * n_chunks):
        if gc not in posted:
            post(gc * CHUNK)


def _gla_mixer(proj, first_seq, n_seq, seq_len, prm, s0):
    npp = GLA_PAIRS
    n_heads = 2 * npp
    n_chunks = seq_len // CHUNK
    nseq = max(1, RW_GROUPS * TERM_UNROLL // n_chunks)
    assert n_chunks % TERM_UNROLL == 0
    assert (n_chunks == TERM_UNROLL or nseq == 1) and n_seq % nseq == 0 and first_seq % nseq == 0
    rows = nseq * seq_len
    total_chunks = nseq * n_chunks
    first = first_seq // nseq
    state_spec = pl.BlockSpec((nseq, 2, n_heads, 64, LANE), lambda b: (b, 0, 0, 0, 0))
    kernel = functools.partial(_gla_kernel, seq_len, nseq, s0 is None)
    y, s_out = pl.pallas_call(
        kernel,
        grid=(n_seq // nseq,),
        in_specs=[pl.BlockSpec((rows, GLA_QK_W), lambda b: (b + first, CB_GQ * LANE // GLA_QK_W)),
                  pl.BlockSpec((rows, GLA_QK_W), lambda b: (b + first, CB_GK * LANE // GLA_QK_W)),
                  pl.BlockSpec((rows, GLA_V_W), lambda b: (b + first, CB_GV * LANE // GLA_V_W)),
                  pl.BlockSpec((rows, GLA_V_W), lambda b: (b + first, CB_OG * LANE // GLA_V_W)),
                  pl.BlockSpec((rows, LANE), lambda b: (b + first, CB_LGK)),
                  pl.BlockSpec((2, LANE, GLA_QK_W), lambda b: (0, 0, 0)),
                  pl.BlockSpec((8, GLA_V_W), lambda b: (0, 0))] + ([] if s0 is None else [state_spec]),
        out_specs=[pl.BlockSpec((rows, GLA_V_W), lambda b: (b, 0)), state_spec],
        out_shape=[jax.ShapeDtypeStruct((n_seq * seq_len, GLA_V_W), BF16),
                   jax.ShapeDtypeStruct((n_seq, 2, n_heads, 64, LANE), F32)],
        scratch_shapes=[pltpu.VMEM((rows, GLA_V_W), F32)] * 2
                       + [pltpu.VMEM((nseq, 2, npp, LANE, 2 * LANE), F32),
                          pltpu.VMEM((2, npp, total_chunks, CHUNK, LANE), BF16),
                          pltpu.VMEM((2, npp, total_chunks, CHUNK, 2 * LANE), F32),
                          pltpu.VMEM((2, npp, total_chunks, LANE, LANE), F32),
                          pltpu.VMEM((2, npp, total_chunks, LANE, 2 * LANE), F32)],
        compiler_params=_cparams(1),
        name="gla_mixer",
    )(proj, proj, proj, proj, proj, prm['gk2p'], prm['gvec'], *([] if s0 is None else [s0]))
    return y, s_out


OUT_TM = 512
OUT_X_SLOTS = 3
ROUTE_NEG = -1e30
LANE_GROUP0 = N_EXPERTS


def _route(logits):
    lane = _iota(logits.shape, 1)
    lane_f = lane.astype(F32)
    big = float(LANE)
    is_g = (lane >= LANE_GROUP0) & (lane < LANE_GROUP0 + 4)
    gmax = jnp.max(jnp.where(is_g, logits, ROUTE_NEG), axis=-1, keepdims=True)
    gidx = jnp.min(jnp.where(is_g & (logits == gmax), lane_f, big), axis=-1, keepdims=True) - LANE_GROUP0
    gsum = jnp.sum(jnp.where(is_g, jnp.exp(jnp.minimum(logits - gmax, 0.0)), 0.0), axis=-1, keepdims=True)
    g_w = 1.0 / gsum
    in_grp = (lane < N_EXPERTS) & ((lane // 4).astype(F32) == gidx)
    m1 = jnp.max(jnp.where(in_grp, logits, ROUTE_NEG), axis=-1, keepdims=True)
    i1 = jnp.min(jnp.where(in_grp & (logits == m1), lane_f, big), axis=-1, keepdims=True)
    rest = in_grp & (lane_f != i1)
    m2 = jnp.max(jnp.where(rest, logits, ROUTE_NEG), axis=-1, keepdims=True)
    i2 = jnp.min(jnp.where(rest & (logits == m2), lane_f, big), axis=-1, keepdims=True)
    t = jnp.exp(m2 - m1)
    w1 = g_w / (1.0 + t)
    return jnp.where(lane_f == i1, w1, 0.0) + jnp.where(lane_f == i2, w1 * t, 0.0)


def _outproj_kernel(tiles, yrc_ref, yrd_ref, ygc_ref, ygd_ref, xc_hbm, xd_hbm, mod_ref, wo_ref, g_ref, wr_ref,
                    br_ref, x1_ref, h2_ref, cmb_ref, xbuf_ref, xsem):
    i = pl.program_id(0)
    n_tiles = tiles.n_ctx + tiles.n_dec

    def x_copy(t, start):
        slot = t % OUT_X_SLOTS

        def go(src_hbm, tile):
            cp = pltpu.make_async_copy(src_hbm.at[pl.ds(tile * OUT_TM, OUT_TM)], xbuf_ref.at[slot], xsem.at[slot])
            cp.start() if start else cp.wait()

        pl.when(t < tiles.n_ctx)(lambda: go(xc_hbm, t))
        pl.when((t >= tiles.n_ctx) & (t < n_tiles))(lambda: go(xd_hbm, t - tiles.n_ctx))

    @pl.when(i == 0)
    def _():
        for t in range(OUT_X_SLOTS - 1):
            x_copy(jnp.int32(t), True)

    x_copy(i + (OUT_X_SLOTS - 1), True)
    x_copy(i, False)
    x_ref = xbuf_ref.at[i % OUT_X_SLOTS]

    def run(yr_ref, yg_ref):
        m = mod_ref[0]
        mix = _mm(yr_ref[...], wo_ref[0:RW_W, :]) + _mm(yg_ref[...], wo_ref[RW_W:RW_W + GLA_V_W, :])
        x1 = x_ref[...] + m[2:3] * mix
        h2 = _rmsnorm_rows(x1) * g_ref[...] * (1.0 + m[4:5]) + m[3:4]
        x1_ref[...] = x1
        h2_ref[...] = h2.astype(BF16)
        h_hi, h_lo = _split2(h2)
        both = _dot(h_hi, wr_ref[...])
        logits = both[:, 0:LANE] + both[:, LANE:2 * LANE] + _dot(h_lo, wr_ref[:, 0:LANE])
        cmb_ref[...] = _route(logits + br_ref[...])

    tiles.by_pass(functools.partial(run, yrc_ref, ygc_ref), functools.partial(run, yrd_ref, ygd_ref))


def _out_projection(tiles, y_rw, y_gla, x, mod, w_out, norm_g, w_route, b_route):
    n = x[0].shape[0] + x[1].shape[0]
    assert tiles.tm == OUT_TM
    full = lambda a: pl.BlockSpec(a.shape, lambda i: (0,) * a.ndim)
    hbm = pl.BlockSpec(memory_space=pl.ANY)
    return pl.pallas_call(
        functools.partial(_outproj_kernel, tiles),
        grid=(tiles.n_ctx + tiles.n_dec,),
        in_specs=[*tiles.specs(RW_W), *tiles.specs(GLA_V_W), hbm, hbm, tiles.mod_spec(),
                  full(w_out), full(norm_g), full(w_route), full(b_route)],
        out_specs=[tiles.merged(D_MODEL), tiles.merged(D_MODEL), tiles.merged(LANE)],
        out_shape=[jax.ShapeDtypeStruct((n, D_MODEL), F32), jax.ShapeDtypeStruct((n, D_MODEL), BF16),
                   jax.ShapeDtypeStruct((n, LANE), F32)],
        scratch_shapes=[pltpu.VMEM((OUT_X_SLOTS, OUT_TM, D_MODEL), F32), pltpu.SemaphoreType.DMA((OUT_X_SLOTS,))],
        compiler_params=_cparams(1),
        name="out_proj_router",
    )(*y_rw, *y_gla, *x, mod, w_out, norm_g, w_route, b_route)


MOE_TM = 512
MOE_RB = 128
MOE_INTERLEAVE = 4
SLOT_ALIGN = 16
MOE_SLOTS = 2 * MOE_TM + N_EXPERTS * SLOT_ALIGN


def _stage_expert_weights(srcs_hbm, dst_refs, stage_refs, sems):
    def copies(e):
        return [pltpu.make_async_copy(src.at[e], stage.at[e % 2], sem.at[e % 2])
                for src, stage, sem in zip(srcs_hbm, stage_refs, sems)]

    for c in copies(0):
        c.start()
    for e in range(N_EXPERTS):
        if e + 1 < N_EXPERTS:
            for c in copies(e + 1):
                c.start()
        for c, dst, stage in zip(copies(e), dst_refs, stage_refs):
            c.wait()
            dst[e] = stage[e % 2].astype(BF16)


def _moe_kernel(tiles, h2_ref, cmb_ref, x1_ref, mod_ref, w1_hbm, w3_hbm, w2_hbm, fg_ref, yc_ref, yd_ref,
                xs_ref, ys_ref, w1_ref, w3_ref, w2_ref, stage1_ref, stage3_ref, stage2_ref, sem1, sem3, sem2):
    @pl.when(pl.program_id(0) == 0)
    def _():
        _stage_expert_weights((w1_hbm, w3_hbm, w2_hbm), (w1_ref, w3_ref, w2_ref),
                              (stage1_ref, stage3_ref, stage2_ref), (sem1, sem3, sem2))

    cmb = cmb_ref[...]
    lane = _iota(cmb.shape, 1).astype(F32)
    sel = cmb > 0.0
    sel01 = jnp.where(sel, 1.0, 0.0).astype(BF16)
    before = (_iota((MOE_TM, MOE_TM), 0) > _iota((MOE_TM, MOE_TM), 1)).astype(BF16)
    pos = _dot(before, sel01)
    cnt = pos[MOE_TM - 1:MOE_TM] + sel01[MOE_TM - 1:MOE_TM].astype(F32)
    seg = jnp.floor((cnt + (SLOT_ALIGN - 1)) * (1.0 / SLOT_ALIGN))
    lower_experts = (_iota((LANE, LANE), 0) < _iota((LANE, LANE), 1)).astype(BF16)
    start = _dot(jnp.broadcast_to(seg, (8, LANE)).astype(BF16), lower_experts)[0:1] * SLOT_ALIGN
    n_blk = jnp.floor((cnt + (MOE_RB - 1)) * (1.0 / MOE_RB)).astype(jnp.int32)
    start_i = start.astype(jnp.int32)
    cnt_i = cnt.astype(jnp.int32)
    slot = start + pos
    e_a = jnp.min(jnp.where(sel, lane, float(LANE)), axis=-1, keepdims=True)
    e_b = jnp.max(jnp.where(sel, lane, -1.0), axis=-1, keepdims=True)
    pick = lambda e, x: jnp.sum(jnp.where(lane == e, x, 0.0), axis=-1, keepdims=True)
    slot_a, w_a = pick(e_a, slot), pick(e_a, cmb)
    slot_b = jnp.where(e_b != e_a, pick(e_b, slot), -1.0)
    w_b = pick(e_b, cmb)

    slots_t = jnp.where(lane == 0.0, slot_a, jnp.where(lane == 1.0, slot_b, -1.0)).T
    row_slot = _iota((MOE_SLOTS, MOE_TM), 0).astype(F32)
    gather = jnp.where((row_slot == slots_t[0:1]) | (row_slot == slots_t[1:2]), 1.0, 0.0).astype(BF16)
    xs_ref[...] = _dot(gather, h2_ref[...]).astype(BF16)
    ys_ref[...] = jnp.zeros_like(ys_ref)

    row_in_blk = _iota((MOE_RB, D_MODEL), 0)

    def expert_blocks(experts, r0s, ends):
        base = [pl.multiple_of(jnp.minimum(r0, MOE_SLOTS - MOE_RB), SLOT_ALIGN) for r0 in r0s]
        xbs = [xs_ref[pl.ds(b, MOE_RB), :] for b in base]
        gates = [_dot(xb, w3_ref[e]) for xb, e in zip(xbs, experts)]
        ups = [_dot(xb, w1_ref[e]) for xb, e in zip(xbs, experts)]
        acts = [(_silu(g) * u).astype(BF16) for g, u in zip(gates, ups)]
        outs = [_dot(a, w2_ref[e]) for a, e in zip(acts, experts)]
        for b, r0, end, out in zip(base, r0s, ends, outs):
            row = row_in_blk + b
            keep = (row >= end) | (row < r0)
            ys_ref[pl.ds(b, MOE_RB), :] = jnp.where(keep, ys_ref[pl.ds(b, MOE_RB), :], out.astype(BF16))

    seg_start = [pl.multiple_of(start_i[0, e], SLOT_ALIGN) for e in range(N_EXPERTS)]
    seg_end = [seg_start[e] + cnt_i[0, e] for e in range(N_EXPERTS)]
    for e0 in range(0, N_EXPERTS, MOE_INTERLEAVE):
        es = list(range(e0, e0 + MOE_INTERLEAVE))
        expert_blocks(es, [seg_start[e] for e in es], [seg_end[e] for e in es])
    for e in range(N_EXPERTS):
        def extra_block(b, carry, e=e):
            expert_blocks([e], [pl.multiple_of(seg_start[e] + b * MOE_RB, SLOT_ALIGN)], [seg_end[e]])
            return carry

        lax.fori_loop(1, n_blk[0, e], extra_block, 0)

    col_slot = _iota((MOE_TM, MOE_SLOTS), 1).astype(F32)
    scatter = (jnp.where(col_slot == slot_a, w_a, 0.0) + jnp.where(col_slot == slot_b, w_b, 0.0)).astype(BF16)
    x2 = x1_ref[...] + mod_ref[0][5:6] * _dot(scatter, ys_ref[...])
    y = _rmsnorm_rows(x2) * fg_ref[...]

    def write(y_ref):
        y_ref[...] = y

    tiles.by_pass(functools.partial(write, yc_ref), functools.partial(write, yd_ref))


def _moe(tiles, h2, cmb, x1, mod, w1, w3, w2, final_g):
    assert tiles.tm == MOE_TM
    hbm = pl.BlockSpec(memory_space=pl.ANY)
    out_ctx, out_dec = tiles.specs(D_MODEL)
    return pl.pallas_call(
        functools.partial(_moe_kernel, tiles),
        grid=(tiles.n_ctx + tiles.n_dec,),
        in_specs=[tiles.merged(D_MODEL), tiles.merged(LANE), tiles.merged(D_MODEL), tiles.mod_spec(),
                  hbm, hbm, hbm, pl.BlockSpec((1, D_MODEL), lambda i: (0, 0))],
        out_specs=[out_ctx, out_dec],
        out_shape=[jax.ShapeDtypeStruct((tiles.n_ctx * MOE_TM, D_MODEL), F32),
                   jax.ShapeDtypeStruct((tiles.n_dec * MOE_TM, D_MODEL), F32)],
        scratch_shapes=[pltpu.VMEM((MOE_SLOTS, D_MODEL), BF16), pltpu.VMEM((MOE_SLOTS, D_MODEL), BF16),
                        pltpu.VMEM((N_EXPERTS, D_MODEL, D_EXPERT), BF16),
                        pltpu.VMEM((N_EXPERTS, D_MODEL, D_EXPERT), BF16),
                        pltpu.VMEM((N_EXPERTS, D_EXPERT, D_MODEL), BF16),
                        pltpu.VMEM((2, D_MODEL, D_EXPERT), F32), pltpu.VMEM((2, D_MODEL, D_EXPERT), F32),
                        pltpu.VMEM((2, D_EXPERT, D_MODEL), F32)] + [pltpu.SemaphoreType.DMA((2,))] * 3,
        compiler_params=_cparams(1),
        name="moe_experts",
    )(h2, cmb, x1, mod, w1, w3, w2, final_g)


def _pad_rows(x, rows):
    return jnp.pad(x, ((0, rows - x.shape[0]),) + ((0, 0),) * (x.ndim - 1))


def _pack_params(l, w_in, rw_conv, rw_w0, rw_w2, rw_a0, rw_a2, rw_g2, rw_k_k, rw_k_a, rw_r_k, rw_ln_w, rw_ln_b,
                 gla_gk2, gla_gk_b, gla_norm_g, moe_w_group, moe_b_group, moe_w_expert, moe_b_expert):
    wi = w_in[l]
    z = lambda n: jnp.zeros((D_MODEL, n), F32)
    w_in_t = jnp.swapaxes(wi, 0, 1)
    z64 = jnp.zeros((64, RW_W), F32)
    w2p = jnp.stack([jnp.concatenate([rw_w2[l, 0], z64], 0), jnp.concatenate([z64, rw_w2[l, 1]], 0)])
    a2p = jnp.stack([jnp.concatenate([rw_a2[l, 0], z64], 0), jnp.concatenate([z64, rw_a2[l, 1]], 0)])
    vec = _pad_rows(jnp.stack([rw_w0[l, 0], rw_w0[l, 1], rw_a0[l, 0], rw_a0[l, 1], rw_k_k[l], rw_k_a[l],
                               rw_r_k[l].reshape(RW_W), rw_ln_w[l], rw_ln_b[l]]), 16)
    rw = {'conv': rw_conv[l].reshape(9, 3 * RW_W), 'w2p': w2p.astype(BF16), 'a2p': a2p.astype(BF16),
          'g2': rw_g2[l].astype(BF16), 'vec': vec}
    gk2p = jnp.stack([_pad_rows(gla_gk2[l, 0], LANE),
                      _pad_rows(jnp.concatenate([jnp.zeros((16, GLA_QK_W), F32), gla_gk2[l, 1]], 0), LANE)])
    gk_b = jnp.pad(gla_gk_b[l], ((0, 0), (0, GLA_V_W - GLA_QK_W)))
    gvec = _pad_rows(jnp.concatenate([gk_b, jnp.tile(gla_norm_g[l], GLA_V_W // LANE)[None]], axis=0), 8)
    gla = {'gk2p': gk2p.astype(BF16), 'gvec': gvec}
    w_route = jnp.concatenate(_split2(jnp.concatenate([moe_w_expert[l], moe_w_group[l], z(LANE - N_EXPERTS - 4)],
                                                      axis=1)), axis=1)
    b_route = jnp.concatenate([moe_b_expert[l], moe_b_group[l], jnp.zeros((LANE - N_EXPERTS - 4,), F32)])[None]
    return w_in_t, rw, gla, w_route, b_route


def kernel(x_prompt, x_sample, state_rwkv, state_gla, c, c_ctx, norm1_g, norm2_g, w_ada, b_ada, w_in, w_out,
           rw_conv, rw_w0, rw_w2, rw_a0, rw_a2, rw_g2, rw_k_k, rw_k_a, rw_r_k, rw_ln_w, rw_ln_b,
           gla_gk2, gla_gk_b, gla_norm_g, moe_w_group, moe_b_group, moe_w_expert, moe_b_expert,
           moe_w1, moe_w3, moe_w2, final_g):
    depth = w_in.shape[0]
    assert depth == 1, "the packed layout below handles the single-layer trunk of this problem"
    l = 0
    n_dec = x_sample.shape[0]
    ctx_row = n_dec
    cond8 = _pad_rows(jnp.concatenate([c, c_ctx[None]], axis=0), 8)
    mod = _modulation(cond8, w_ada[l], b_ada[l][None]).reshape(8, N_MOD, D_MODEL)
    pk = _pack_params(l, w_in, rw_conv, rw_w0, rw_w2, rw_a0, rw_a2, rw_g2, rw_k_k, rw_k_a, rw_r_k, rw_ln_w,
                      rw_ln_b, gla_gk2, gla_gk_b, gla_norm_g, moe_w_group, moe_b_group, moe_w_expert,
                      moe_b_expert)
    w_in_t, rw, gla, w_route, b_route = pk

    n_ctx, ctx_len, _ = x_prompt.shape
    dec_len = x_sample.shape[1]
    x_ctx = x_prompt.reshape(n_ctx * ctx_len, D_MODEL)
    x_dec = x_sample.reshape(n_dec * dec_len, D_MODEL)
    assert (n_ctx * ctx_len) % dec_len == 0, "denoising sequences must start on a dec_len row block of proj"
    first_dec = n_ctx * ctx_len // dec_len
    tiles = _Tiles(n_ctx * ctx_len, n_dec * dec_len, dec_len, ctx_row, PROJ_TM)
    assert PROJ_TM == OUT_TM == MOE_TM

    proj = _in_projection(tiles, x_ctx, x_dec, mod, norm1_g[l][None], w_in_t)
    y_rw_c, s_rw = _rwkv_mixer(proj, 0, n_ctx, ctx_len, False, rw, None)
    y_gla_c, s_gla = _gla_mixer(proj, 0, n_ctx, ctx_len, gla, None)
    y_rw_d, _ = _rwkv_mixer(proj, first_dec, n_dec, dec_len, True, rw, state_rwkv[:, l])
    y_gla_d, _ = _gla_mixer(proj, first_dec, n_dec, dec_len, gla, state_gla[:, l])
    x1, h2, cmb = _out_projection(tiles, (y_rw_c, y_rw_d), (y_gla_c, y_gla_d), (x_ctx, x_dec), mod,
                                  w_out[l].astype(BF16), norm2_g[l][None], w_route, b_route)
    y_ctx, y_dec = _moe(tiles, h2, cmb, x1, mod, moe_w1[l], moe_w3[l], moe_w2[l], final_g[None])
    return (y_ctx.reshape(x_prompt.shape), y_dec.reshape(x_sample.shape), s_rw[:, None], s_gla[:, None])
```
